```python
import math
import jax
import jax.numpy as jnp
from jax import lax
import numpy as np

D_MODEL = 1024
BATCH = 8
SEQ = 2048
DEPTH = 2
DEC_BATCH = 128
DEC_SEQ = 1
PAST_LEN = 16384
PAGE_SIZE = 128

N_EVEN = (DEPTH + 1) // 2
N_ODD = DEPTH // 2

POOL_WINDOWS = (2, 4, 8, 16)
POOL_GROUPS = len(POOL_WINDOWS)
POOL_WIDTH = D_MODEL // 2
POOL_GW = POOL_WIDTH // POOL_GROUPS
POOL_BUF = max(POOL_WINDOWS) - 1

MLSTM_HEADS = 4
MLSTM_WIDTH = D_MODEL // 2
MLSTM_HEAD_DIM = MLSTM_WIDTH // MLSTM_HEADS
MLSTM_CHUNK = 128

MIX_WIDTH = POOL_WIDTH + MLSTM_WIDTH
IN_AB = POOL_WIDTH + 4 * MLSTM_WIDTH + 2 * MLSTM_HEADS

S5_GROUP_WIDTH = 16
S5_GROUPS = D_MODEL // S5_GROUP_WIDTH
S5_STATE = 64
S5_CHUNK = 128
S5_DT_MIN = 1e-3
S5_DT_MAX = 1e-1

MOE_GROUPS = 4
MOE_EXPERTS_PER_GROUP = 8
MOE_EXPERTS = MOE_GROUPS * MOE_EXPERTS_PER_GROUP
MOE_TOP_K = 2
MOE_HIDDEN = 256

RMS_EPS = 1e-6

kernel_name = 'hybrid_pool_mlstm_s5_hmoe_step'


def _rmsnorm(x, g):
    xf = x.astype(jnp.float32)
    y = xf * lax.rsqrt(jnp.mean(xf * xf, axis=-1, keepdims=True) + RMS_EPS) * g.astype(jnp.float32)
    return y.astype(x.dtype)


def _chunk_len(t, c):
    return c if t % c == 0 else t


def _pool_mixer(u, buf, pos, pool_w, pool_scale):
    t = u.shape[1]
    uf = u.astype(jnp.float32)
    ext = jnp.concatenate([buf.astype(jnp.float32), uf], axis=1)
    cs = jnp.cumsum(ext, axis=1)
    cs = jnp.concatenate([jnp.zeros_like(cs[:, :1]), cs], axis=1)
    means = []
    for g, w in enumerate(POOL_WINDOWS):
        c0 = g * POOL_GW
        hi = cs[:, POOL_BUF + 1:, c0:c0 + POOL_GW]
        lo = cs[:, POOL_BUF + 1 - w:POOL_BUF + 1 - w + t, c0:c0 + POOL_GW]
        cnt = jnp.minimum(w, pos + 1).astype(jnp.float32)
        means.append((hi - lo) / cnt[None, :, None])
    d = jnp.concatenate(means, axis=-1) - uf
    d = d.reshape(d.shape[:2] + (POOL_GROUPS, POOL_GW))
    y = jnp.einsum('btgc,gcd->btgd', d, pool_w.astype(jnp.float32)).reshape(uf.shape)
    return y * pool_scale.astype(jnp.float32), ext[:, -POOL_BUF:]


def _mlstm_chunk(carry, inp):
    c, n, m = carry
    q, k, v, li, lf = inp
    ln = q.shape[2]
    bcum = jnp.cumsum(lf, axis=-1)
    causal = jnp.tril(jnp.ones((ln, ln), dtype=bool))
    dmat = jnp.where(causal, bcum[..., :, None] - bcum[..., None, :] + li[..., None, :], -jnp.inf)
    inter = bcum + m[..., None]
    m_row = jnp.maximum(inter, jnp.max(dmat, axis=-1))
    w_intra = jnp.exp(dmat - m_row[..., None])
    w_inter = jnp.exp(inter - m_row)
    s = jnp.einsum('bhld,bhsd->bhls', q, k) * w_intra
    num = jnp.einsum('bhls,bhsd->bhld', s, v) + w_inter[..., None] * jnp.einsum('bhvk,bhlk->bhlv', c, q)
    den = jnp.sum(s, axis=-1) + w_inter * jnp.einsum('bhk,bhlk->bhl', n, q)
    h = num / jnp.maximum(jnp.abs(den), jnp.exp(-m_row))[..., None]
    b_end = bcum[..., -1]
    g = b_end[..., None] - bcum + li
    m_new = jnp.maximum(b_end + m, jnp.max(g, axis=-1))
    wg = jnp.exp(g - m_new[..., None])
    decay = jnp.exp(b_end + m - m_new)
    c_new = decay[..., None, None] * c + jnp.einsum('bhsv,bhsk->bhvk', v * wg[..., None], k)
    n_new = decay[..., None] * n + jnp.einsum('bhs,bhsk->bhk', wg, k)
    return (c_new, n_new, m_new), h


def _mlstm_mixer(q, k, v, o_pre, i_pre, f_pre, c0, n0, m0):
    b, t, _ = q.shape
    f32 = jnp.float32

    def heads(a):
        return a.astype(f32).reshape(b, t, MLSTM_HEADS, MLSTM_HEAD_DIM).transpose(0, 2, 1, 3)

    qh, kh, vh = heads(q), heads(k) * (MLSTM_HEAD_DIM ** -0.5), heads(v)
    li = i_pre.astype(f32).transpose(0, 2, 1)
    lf = jax.nn.log_sigmoid(f_pre.astype(f32)).transpose(0, 2, 1)
    ln = _chunk_len(t, MLSTM_CHUNK)
    nc = t // ln

    def chunks(a):
        return jnp.moveaxis(a.reshape(a.shape[:2] + (nc, ln) + a.shape[3:]), 2, 0)

    (c1, n1, m1), hs = lax.scan(
        _mlstm_chunk, (c0.astype(f32), n0.astype(f32), m0.astype(f32)),
        (chunks(qh), chunks(kh), chunks(vh), chunks(li), chunks(lf)))
    hs = jnp.moveaxis(hs, 0, 2).reshape(b, MLSTM_HEADS, t, MLSTM_HEAD_DIM)
    hs = hs.transpose(0, 2, 1, 3).reshape(b, t, MLSTM_WIDTH)
    return hs * jax.nn.sigmoid(o_pre.astype(f32)), c1, n1, m1


def _linear_combine(e1, e2):
    a1, b1 = e1
    a2, b2 = e2
    return a1 * a2, a2 * b1 + b2


def _s5_scan(u, h_re, h_im, lam_re, lam_im, log_step, b_re, b_im, c_re, c_im, d_skip):
    b, t, _ = u.shape
    f32 = jnp.float32
    uf = u.astype(f32).reshape(b, t, S5_GROUPS, S5_GROUP_WIDTH)
    lam = lax.complex(lam_re.astype(f32), lam_im.astype(f32))
    dt = jnp.exp(log_step.astype(f32))[:, None]
    lam_bar = jnp.exp(lam * dt)
    b_bar = ((lam_bar - 1.0) / lam)[..., None] * lax.complex(b_re.astype(f32), b_im.astype(f32))
    c_c = lax.complex(c_re.astype(f32), c_im.astype(f32))
    d_g = d_skip.astype(f32).reshape(S5_GROUPS, S5_GROUP_WIDTH)
    ln = _chunk_len(t, S5_CHUNK)
    nc = t // ln
    u_chunks = jnp.moveaxis(uf.reshape(b, nc, ln, S5_GROUPS, S5_GROUP_WIDTH), 1, 0)

    def body(h, u_blk):
        bu = jnp.einsum('blgc,gpc->blgp', u_blk.astype(jnp.complex64), b_bar)
        bu = bu.at[:, 0].add(lam_bar * h)
        a = jnp.broadcast_to(lam_bar, bu.shape)
        _, hs = lax.associative_scan(_linear_combine, (a, bu), axis=1)
        y = jnp.real(jnp.einsum('blgp,gcp->blgc', hs, c_c)) + d_g * u_blk
        return hs[:, -1], y

    h0 = lax.complex(h_re.astype(f32), h_im.astype(f32))
    h_last, ys = lax.scan(body, h0, u_chunks)
    y = jnp.moveaxis(ys, 0, 1).reshape(b, t, D_MODEL)
    return y, jnp.real(h_last), jnp.imag(h_last)


def _hmoe(h, w_group, b_group, w_expert, b_expert, w_gate, w_up, w_down):
    bsz, t, d = h.shape
    f32 = jnp.float32
    x = h.reshape(bsz * t, d)
    xf = x.astype(f32)
    g_logits = xf @ w_group.astype(f32) + b_group.astype(f32)
    g_prob = jax.nn.softmax(g_logits, axis=-1)
    g_sel = jnp.argmax(g_logits, axis=-1)
    g_w = jnp.take_along_axis(g_prob, g_sel[:, None], axis=-1)
    e_logits = (xf @ w_expert.astype(f32) + b_expert.astype(f32)).reshape(-1, MOE_GROUPS, MOE_EXPERTS_PER_GROUP)
    e_in = jnp.take_along_axis(e_logits, g_sel[:, None, None], axis=1)[:, 0]
    top_v, top_i = lax.top_k(e_in, MOE_TOP_K)
    top_w = jax.nn.softmax(top_v, axis=-1) * g_w
    eid = g_sel[:, None] * MOE_EXPERTS_PER_GROUP + top_i
    combine = jnp.sum(jax.nn.one_hot(eid, MOE_EXPERTS, dtype=f32) * top_w[..., None], axis=1)
    y = jnp.zeros((bsz * t, d), f32)
    for g in range(MOE_GROUPS):
        sl = slice(g * MOE_EXPERTS_PER_GROUP, (g + 1) * MOE_EXPERTS_PER_GROUP)
        hg = jnp.einsum('nd,edf->nef', x, w_gate[sl])
        hu = jnp.einsum('nd,edf->nef', x, w_up[sl])
        act = (jax.nn.silu(hg.astype(f32)) * hu.astype(f32)) * combine[:, sl, None]
        y = y + jnp.einsum('nef,efd->nd', act.astype(x.dtype), w_down[sl]).astype(f32)
    return y.reshape(bsz, t, d).astype(h.dtype)


def _trunk(x, pos, pool_st, mc_st, mn_st, mm_st, re_st, im_st, w):
    pools, mcs, mns, mms, res, ims = [], [], [], [], [], []
    o1 = POOL_WIDTH
    o2 = o1 + MLSTM_WIDTH
    o3 = o2 + MLSTM_WIDTH
    o4 = o3 + MLSTM_WIDTH
    o5 = o4 + MLSTM_WIDTH
    for l in range(DEPTH):
        h = _rmsnorm(x, w['norm_mix'][l])
        if l % 2 == 0:
            e = l // 2
            z = h @ w['w_in_ab'][e]
            gates = z[..., o5:].astype(jnp.float32) + w['b_gates'][e].astype(jnp.float32)
            pool_out, pb = _pool_mixer(z[..., :o1], pool_st[e], pos, w['pool_w'][e], w['pool_scale'][e])
            ml_out, c1, n1, m1 = _mlstm_mixer(z[..., o1:o2], z[..., o2:o3], z[..., o3:o4], z[..., o4:o5],
                                              gates[..., :MLSTM_HEADS], gates[..., MLSTM_HEADS:],
                                              mc_st[e], mn_st[e], mm_st[e])
            mix = jnp.concatenate([pool_out, ml_out], axis=-1).astype(x.dtype) @ w['w_out_ab'][e]
            pools.append(pb)
            mcs.append(c1)
            mns.append(n1)
            mms.append(m1)
        else:
            o = l // 2
            y, r, i = _s5_scan(h, re_st[o], im_st[o], w['s5_lam_re'][o], w['s5_lam_im'][o], w['s5_log_step'][o],
                               w['s5_b_re'][o], w['s5_b_im'][o], w['s5_c_re'][o], w['s5_c_im'][o], w['s5_d'][o])
            ag = jax.nn.gelu(y).astype(x.dtype) @ w['w_glu'][o]
            mix = ag[..., :D_MODEL] * jax.nn.sigmoid(ag[..., D_MODEL:])
            res.append(r)
            ims.append(i)
        x = x + mix.astype(x.dtype)
        x = x + _hmoe(_rmsnorm(x, w['norm_ffn'][l]), w['moe_w_group'][l], w['moe_b_group'][l],
                      w['moe_w_expert'][l], w['moe_b_expert'][l], w['moe_w_gate'][l],
                      w['moe_w_up'][l], w['moe_w_down'][l])
    return (_rmsnorm(x, w['norm_final']), jnp.stack(pools), jnp.stack(mcs), jnp.stack(mns),
            jnp.stack(mms), jnp.stack(res), jnp.stack(ims))


def setup_inputs(seed: int = 0) -> dict:
    key = jax.random.key(seed)
    ks = list(jax.random.split(key, 40))
    f32 = jnp.float32
    nrm = lambda shape, scale: scale * jax.random.normal(ks.pop(), shape, f32)
    H = MLSTM_HEADS
    b_gates = jnp.concatenate([nrm((N_EVEN, H), 0.1),
                               jnp.linspace(3.0, 6.0, H, dtype=f32)[None] + nrm((N_EVEN, H), 0.1)], axis=-1)
    lam_re = -0.5 + nrm((N_ODD, S5_GROUPS, S5_STATE), 0.01)
    lam_im = jnp.pi * jnp.arange(S5_STATE, dtype=f32)[None, None] + nrm((N_ODD, S5_GROUPS, S5_STATE), 0.01)
    log_step = jax.random.uniform(ks.pop(), (N_ODD, S5_GROUPS), f32,
                                  minval=math.log(S5_DT_MIN), maxval=math.log(S5_DT_MAX))
    return {
        'x_prompt': nrm((BATCH, SEQ, D_MODEL), 1.0),
        'x_sample': nrm((DEC_BATCH, DEC_SEQ, D_MODEL), 1.0),
        'state_pool': nrm((N_EVEN, DEC_BATCH, POOL_BUF, POOL_WIDTH), 1.0),
        'state_mlstm_c': nrm((N_EVEN, DEC_BATCH, H, MLSTM_HEAD_DIM, MLSTM_HEAD_DIM), 0.1),
        'state_mlstm_n': nrm((N_EVEN, DEC_BATCH, H, MLSTM_HEAD_DIM), 0.1),
        'state_mlstm_m': nrm((N_EVEN, DEC_BATCH, H), 1.0),
        'state_s5_re': nrm((N_ODD, DEC_BATCH, S5_GROUPS, S5_STATE), 0.1),
        'state_s5_im': nrm((N_ODD, DEC_BATCH, S5_GROUPS, S5_STATE), 0.1),
        'norm_mix': 1.0 + nrm((DEPTH, D_MODEL), 0.02),
        'norm_ffn': 1.0 + nrm((DEPTH, D_MODEL), 0.02),
        'norm_final': 1.0 + nrm((D_MODEL,), 0.02),
        'w_in_ab': nrm((N_EVEN, D_MODEL, IN_AB), D_MODEL ** -0.5),
        'b_gates': b_gates,
        'pool_w': nrm((N_EVEN, POOL_GROUPS, POOL_GW, POOL_GW), POOL_GW ** -0.5),
        'pool_scale': 1.0 + nrm((N_EVEN, POOL_WIDTH), 0.02),
        'w_out_ab': nrm((N_EVEN, MIX_WIDTH, D_MODEL), MIX_WIDTH ** -0.5),
        's5_lam_re': lam_re,
        's5_lam_im': lam_im,
        's5_log_step': log_step,
        's5_b_re': nrm((N_ODD, S5_GROUPS, S5_STATE, S5_GROUP_WIDTH), (2 * S5_GROUP_WIDTH) ** -0.5),
        's5_b_im': nrm((N_ODD, S5_GROUPS, S5_STATE, S5_GROUP_WIDTH), (2 * S5_GROUP_WIDTH) ** -0.5),
        's5_c_re': nrm((N_ODD, S5_GROUPS, S5_GROUP_WIDTH, S5_STATE), (S5_STATE / 4) ** -0.5),
        's5_c_im': nrm((N_ODD, S5_GROUPS, S5_GROUP_WIDTH, S5_STATE), (S5_STATE / 4) ** -0.5),
        's5_d': nrm((N_ODD, D_MODEL), 0.5),
        'w_glu': nrm((N_ODD, D_MODEL, 2 * D_MODEL), D_MODEL ** -0.5),
        'moe_w_group': nrm((DEPTH, D_MODEL, MOE_GROUPS), D_MODEL ** -0.5),
        'moe_b_group': nrm((DEPTH, MOE_GROUPS), 0.01),
        'moe_w_expert': nrm((DEPTH, D_MODEL, MOE_EXPERTS), D_MODEL ** -0.5),
        'moe_b_expert': nrm((DEPTH, MOE_EXPERTS), 0.01),
        'moe_w_gate': nrm((DEPTH, MOE_EXPERTS, D_MODEL, MOE_HIDDEN), D_MODEL ** -0.5),
        'moe_w_up': nrm((DEPTH, MOE_EXPERTS, D_MODEL, MOE_HIDDEN), D_MODEL ** -0.5),
        'moe_w_down': nrm((DEPTH, MOE_EXPERTS, MOE_HIDDEN, D_MODEL), MOE_HIDDEN ** -0.5),
    }


def reference(x_prompt, x_sample, state_pool, state_mlstm_c, state_mlstm_n, state_mlstm_m,
              state_s5_re, state_s5_im, norm_mix, norm_ffn, norm_final, w_in_ab, b_gates, pool_w,
              pool_scale, w_out_ab, s5_lam_re, s5_lam_im, s5_log_step, s5_b_re, s5_b_im, s5_c_re,
              s5_c_im, s5_d, w_glu, moe_w_group, moe_b_group, moe_w_expert, moe_b_expert,
              moe_w_gate, moe_w_up, moe_w_down):
    w = dict(norm_mix=norm_mix, norm_ffn=norm_ffn, norm_final=norm_final, w_in_ab=w_in_ab,
             b_gates=b_gates, pool_w=pool_w, pool_scale=pool_scale, w_out_ab=w_out_ab,
             s5_lam_re=s5_lam_re, s5_lam_im=s5_lam_im, s5_log_step=s5_log_step, s5_b_re=s5_b_re,
             s5_b_im=s5_b_im, s5_c_re=s5_c_re, s5_c_im=s5_c_im, s5_d=s5_d, w_glu=w_glu,
             moe_w_group=moe_w_group, moe_b_group=moe_b_group, moe_w_expert=moe_w_expert,
             moe_b_expert=moe_b_expert, moe_w_gate=moe_w_gate, moe_w_up=moe_w_up, moe_w_down=moe_w_down)
    f32 = jnp.float32
    bp, tp = x_prompt.shape[0], x_prompt.shape[1]
    ts = x_sample.shape[1]
    pos_p = jnp.arange(tp, dtype=jnp.int32)
    pos_s = PAST_LEN + jnp.arange(ts, dtype=jnp.int32)
    z_pool = jnp.zeros((N_EVEN, bp, POOL_BUF, POOL_WIDTH), f32)
    z_c = jnp.zeros((N_EVEN, bp, MLSTM_HEADS, MLSTM_HEAD_DIM, MLSTM_HEAD_DIM), f32)
    z_n = jnp.zeros((N_EVEN, bp, MLSTM_HEADS, MLSTM_HEAD_DIM), f32)
    z_m = jnp.zeros((N_EVEN, bp, MLSTM_HEADS), f32)
    z_s5 = jnp.zeros((N_ODD, bp, S5_GROUPS, S5_STATE), f32)
    y_prompt, pool_p, c_p, n_p, m_p, re_p, im_p = _trunk(x_prompt, pos_p, z_pool, z_c, z_n, z_m, z_s5, z_s5, w)
    y_sample, pool_s, c_s, n_s, m_s, re_s, im_s = _trunk(x_sample, pos_s, state_pool, state_mlstm_c,
                                                         state_mlstm_n, state_mlstm_m, state_s5_re,
                                                         state_s5_im, w)
    return (y_prompt, y_sample, pool_p, c_p, n_p, m_p, re_p, im_p, pool_s, c_s, n_s, m_s, re_s, im_s)
```

```python
import functools

import jax
import jax.numpy as jnp
from jax import lax
from jax.experimental import pallas as pl
from jax.experimental.pallas import tpu as pltpu

F32 = jnp.float32
BF16 = jnp.bfloat16
I32 = jnp.int32

PAST_LEN = 16384
POOL_WINDOWS = (2, 4, 8, 16)
POOL_BUF = max(POOL_WINDOWS) - 1
MLSTM_CHUNK = 128
S5_SUB = 16
MOE_GROUPS = 4
MOE_EXPERTS_PER_GROUP = 8
RMS_EPS = 1e-6

LANES = 128
SUBLANES = 8
VMEM_LIMIT_BYTES = 48 * 1024 * 1024

TOKEN_TILE = 384
EXPERT_ROW_TILE = 256
POOL_TIME_TILE = 512

HIGHEST = lax.Precision.HIGHEST


def _params(*sem):
    return pltpu.CompilerParams(dimension_semantics=sem, vmem_limit_bytes=VMEM_LIMIT_BYTES)


def _rms(x, g):
    return x * lax.rsqrt(jnp.mean(x * x, axis=-1, keepdims=True) + RMS_EPS) * g


def _dot(a, b):
    return jnp.dot(a, b, preferred_element_type=F32)


def _dot_nt(a, b):
    return lax.dot_general(a, b, (((1,), (1,)), ((), ())), preferred_element_type=F32)


def _dot_tn(a, b):
    return lax.dot_general(a, b, (((0,), (0,)), ((), ())), preferred_element_type=F32)


def _inproj_kernel(x_ref, g_ref, w_ref, wg_ref, wgt_ref, bg_ref, bgt_ref,
                   z_ref, gates_ref, gatest_ref):
    h = _rms(x_ref[...], g_ref[...]).astype(BF16)
    z_ref[...] = _dot(h, w_ref[...])
    gates_ref[...] = _dot(h, wg_ref[...]) + bg_ref[...]
    gatest_ref[...] = _dot_nt(wgt_ref[...], h) + bgt_ref[...]


def _inproj(x, g, w, wg, wgt, bg, bgt):
    n, d = x.shape
    nz = w.shape[1]
    ng = wgt.shape[0]
    tm = TOKEN_TILE
    full = lambda i: (0, 0)
    return pl.pallas_call(
        _inproj_kernel,
        grid=(n // tm,),
        in_specs=[
            pl.BlockSpec((tm, d), lambda i: (i, 0)),
            pl.BlockSpec((1, d), full),
            pl.BlockSpec((d, nz), full),
            pl.BlockSpec((d, LANES), full),
            pl.BlockSpec((ng, d), full),
            pl.BlockSpec((1, LANES), full),
            pl.BlockSpec((ng, 1), full),
        ],
        out_specs=[
            pl.BlockSpec((tm, nz), lambda i: (i, 0)),
            pl.BlockSpec((tm, LANES), lambda i: (i, 0)),
            pl.BlockSpec((ng, tm), lambda i: (0, i)),
        ],
        out_shape=[
            jax.ShapeDtypeStruct((n, nz), F32),
            jax.ShapeDtypeStruct((n, LANES), F32),
            jax.ShapeDtypeStruct((ng, n), F32),
        ],
        compiler_params=_params("parallel"),
        name="inproj",
    )(x, g, w, wg, wgt, bg, bgt)


def _pool_prompt_kernel(u_ref, pw_ref, ps_ref, y_ref, st_ref, ext_ref, *, tt, gw):
    t = pl.program_id(1)
    nt = pl.num_programs(1)
    halo = POOL_BUF + 1
    width = ext_ref.shape[1]

    @pl.when(t == 0)
    def _():
        ext_ref[0:halo, :] = jnp.zeros((halo, width), F32)

    u = u_ref[...]
    ext_ref[halo:halo + tt, :] = u
    pos = t * tt + lax.broadcasted_iota(I32, (tt, 1), 0)
    for g, w in enumerate(POOL_WINDOWS):
        c0 = g * gw
        acc = u[:, c0:c0 + gw]
        for j in range(1, w):
            acc = acc + ext_ref[halo - j:halo - j + tt, c0:c0 + gw]
        cnt = jnp.minimum(w, pos + 1).astype(F32)
        d = acc / cnt - u[:, c0:c0 + gw]
        y = _dot(d.astype(BF16), pw_ref[g]) * ps_ref[:, c0:c0 + gw]
        y_ref[:, c0:c0 + gw] = y.astype(BF16)

    @pl.when(t == nt - 1)
    def _():
        st_ref[...] = ext_ref[tt + 1:tt + halo, :]

    ext_ref[0:halo, :] = ext_ref[tt:tt + halo, :]


def _pool_prompt(z, pw, ps, batch, seq):
    width = ps.shape[1]
    gw = width // len(POOL_WINDOWS)
    tt = POOL_TIME_TILE
    nt = seq // tt
    return pl.pallas_call(
        functools.partial(_pool_prompt_kernel, tt=tt, gw=gw),
        grid=(batch, nt),
        in_specs=[
            pl.BlockSpec((tt, width), lambda b, t: (b * nt + t, 0)),
            pl.BlockSpec(pw.shape, lambda b, t: (0, 0, 0)),
            pl.BlockSpec((1, width), lambda b, t: (0, 0)),
        ],
        out_specs=[
            pl.BlockSpec((tt, width), lambda b, t: (b * nt + t, 0)),
            pl.BlockSpec((None, POOL_BUF, width), lambda b, t: (b, 0, 0)),
        ],
        out_shape=[
            jax.ShapeDtypeStruct((batch * seq, width), BF16),
            jax.ShapeDtypeStruct((batch, POOL_BUF, width), F32),
        ],
        scratch_shapes=[pltpu.VMEM((POOL_BUF + 1 + tt, width), F32)],
        compiler_params=_params("parallel", "arbitrary"),
        name="pool_prompt",
    )(z, pw, ps)


def _pool_step_kernel(u_ref, buf_ref, pw_ref, ps_ref, y_ref, nb_ref, *, gw):
    u = u_ref[...]
    for g, w in enumerate(POOL_WINDOWS):
        c0 = g * gw
        acc = u[:, c0:c0 + gw]
        for j in range(1, w):
            acc = acc + buf_ref[POOL_BUF - j, :, c0:c0 + gw]
        cnt = float(min(w, PAST_LEN + 1))
        d = acc / cnt - u[:, c0:c0 + gw]
        y = _dot(d.astype(BF16), pw_ref[g]) * ps_ref[:, c0:c0 + gw]
        y_ref[:, c0:c0 + gw] = y.astype(BF16)
    nb_ref[0:POOL_BUF - 1] = buf_ref[1:POOL_BUF]
    nb_ref[POOL_BUF - 1] = u


def _pool_step(z, buf_t, pw, ps, row0):
    _, batch, width = buf_t.shape
    gw = width // len(POOL_WINDOWS)
    return pl.pallas_call(
        functools.partial(_pool_step_kernel, gw=gw),
        grid=(1,),
        in_specs=[
            pl.BlockSpec((batch, width), lambda i: (row0 // batch, 0)),
            pl.BlockSpec(buf_t.shape, lambda i: (0, 0, 0)),
            pl.BlockSpec(pw.shape, lambda i: (0, 0, 0)),
            pl.BlockSpec((1, width), lambda i: (0, 0)),
        ],
        out_specs=[
            pl.BlockSpec((batch, width), lambda i: (0, 0)),
            pl.BlockSpec(buf_t.shape, lambda i: (0, 0, 0)),
        ],
        out_shape=[
            jax.ShapeDtypeStruct((batch, width), BF16),
            jax.ShapeDtypeStruct(buf_t.shape, F32),
        ],
        compiler_params=_params("arbitrary"),
        name="pool_step",
    )(z, buf_t, pw, ps)


def _mlstm_prompt_kernel(q_ref, k_ref, v_ref, o_ref, gt_ref,
                         h_ref, c_out, n_out, m_out, c_s, n_s, m_s, *, nh, dh):
    ci = pl.program_id(1)
    nc = pl.num_programs(1)
    ln = q_ref.shape[0]

    @pl.when(ci == 0)
    def _():
        c_s[...] = jnp.zeros(c_s.shape, F32)
        n_s[...] = jnp.zeros(n_s.shape, F32)
        m_s[...] = jnp.zeros(m_s.shape, F32)

    row = lax.broadcasted_iota(I32, (ln, ln), 0)
    col = lax.broadcasted_iota(I32, (ln, ln), 1)
    causal = col <= row
    eye = col == row

    gt = gt_ref[...]
    li_all = gt[0:nh]
    bc_all = jax.nn.log_sigmoid(gt[nh:2 * nh])
    lane = lax.broadcasted_iota(I32, (nh, ln), 1)
    s = 1
    while s < ln:
        bc_all = bc_all + jnp.where(lane >= s, pltpu.roll(bc_all, s, 1), 0.0)
        s *= 2

    def to_col(r):
        return jnp.sum(jnp.where(eye, r, 0.0), axis=1, keepdims=True)

    scale = dh ** -0.5
    for h in range(nh):
        sl = slice(h * dh, (h + 1) * dh)
        q = q_ref[:, sl]
        k = k_ref[:, sl] * scale
        v = v_ref[:, sl]
        qb, kb, vb = q.astype(BF16), k.astype(BF16), v.astype(BF16)
        li_r = li_all[h:h + 1]
        bc_r = bc_all[h:h + 1]
        bc_c = to_col(bc_r)
        m0 = m_s[h][:, 0:1]
        c = c_s[h]
        n = n_s[h]

        dmat = jnp.where(causal, bc_c - bc_r + li_r, -jnp.inf)
        inter = bc_c + m0
        m_row = jnp.maximum(inter, jnp.max(dmat, axis=1, keepdims=True))
        w_intra = jnp.exp(dmat - m_row)
        w_inter = jnp.exp(inter - m_row)
        sc = _dot_nt(qb, kb) * w_intra
        num = _dot(sc.astype(BF16), vb) + w_inter * _dot_nt(qb, c.astype(BF16))
        den = jnp.sum(sc, axis=1, keepdims=True) + w_inter * jnp.sum(q * n, axis=1, keepdims=True)
        hh = num / jnp.maximum(jnp.abs(den), jnp.exp(-m_row))
        h_ref[:, sl] = (hh * jax.nn.sigmoid(o_ref[:, sl])).astype(BF16)

        b_end = bc_r[:, ln - 1:ln]
        g_r = b_end - bc_r + li_r
        m_new = jnp.maximum(b_end + m0, jnp.max(g_r, axis=1, keepdims=True))
        wg_c = jnp.exp(to_col(g_r) - m_new)
        decay = jnp.exp(b_end + m0 - m_new)
        c_s[h] = decay * c + _dot_tn((v * wg_c).astype(BF16), kb)
        n_s[h] = decay * n + jnp.sum(wg_c * k, axis=0, keepdims=True)
        m_s[h] = jnp.broadcast_to(m_new, (1, dh))

    @pl.when(ci == nc - 1)
    def _():
        c_out[...] = c_s[...]
        n_out[...] = n_s[...]
        m_out[...] = m_s[...]


def _mlstm_prompt(z, gates_t, batch, seq, nh, dh):
    ln = MLSTM_CHUNK
    nc = seq // ln
    width = nh * dh
    ng = gates_t.shape[0]
    blk = lambda j: pl.BlockSpec((ln, width), lambda b, c: (b * nc + c, j))
    return pl.pallas_call(
        functools.partial(_mlstm_prompt_kernel, nh=nh, dh=dh),
        grid=(batch, nc),
        in_specs=[blk(1), blk(2), blk(3), blk(4),
                  pl.BlockSpec((ng, ln), lambda b, c: (0, b * nc + c))],
        out_specs=[
            pl.BlockSpec((ln, width), lambda b, c: (b * nc + c, 0)),
            pl.BlockSpec((None, nh, dh, dh), lambda b, c: (b, 0, 0, 0)),
            pl.BlockSpec((None, nh, 1, dh), lambda b, c: (b, 0, 0, 0)),
            pl.BlockSpec((None, nh, 1, dh), lambda b, c: (b, 0, 0, 0)),
        ],
        out_shape=[
            jax.ShapeDtypeStruct((batch * seq, width), BF16),
            jax.ShapeDtypeStruct((batch, nh, dh, dh), F32),
            jax.ShapeDtypeStruct((batch, nh, 1, dh), F32),
            jax.ShapeDtypeStruct((batch, nh, 1, dh), F32),
        ],
        scratch_shapes=[pltpu.VMEM((nh, dh, dh), F32), pltpu.VMEM((nh, 1, dh), F32),
                        pltpu.VMEM((nh, 1, dh), F32)],
        compiler_params=_params("parallel", "arbitrary"),
        name="mlstm_prompt",
    )(z, z, z, z, gates_t)


def _mlstm_step_kernel(q_ref, k_ref, v_ref, o_ref, li_ref, fp_ref, c_ref, n_ref, m_ref,
                       h_ref, c_out, n_out, m_out, *, nh, dh):
    eye = (lax.broadcasted_iota(I32, (1, dh, dh), 1) == lax.broadcasted_iota(I32, (1, dh, dh), 2))
    scale = dh ** -0.5
    for h in range(nh):
        sl = slice(h * dh, (h + 1) * dh)
        q = q_ref[:, :, sl]
        k = k_ref[:, :, sl] * scale
        v = v_ref[:, :, sl]
        c = c_ref[:, h]
        n = n_ref[:, h]
        m = m_ref[:, h]
        li = li_ref[:, h]
        lf = jax.nn.log_sigmoid(fp_ref[:, h])
        inter = lf + m
        m_row = jnp.maximum(inter, li)
        w_intra = jnp.exp(li - m_row)
        w_inter = jnp.exp(inter - m_row)
        sc = jnp.sum(q * k, axis=-1, keepdims=True) * w_intra
        v_c = jnp.sum(jnp.where(eye, v, 0.0), axis=-1, keepdims=True)
        num = sc * v_c + w_inter * jnp.sum(c * q, axis=-1, keepdims=True)
        den = sc + w_inter * jnp.sum(n * q, axis=-1, keepdims=True)
        h_c = num / jnp.maximum(jnp.abs(den), jnp.exp(-m_row))
        h_l = jnp.sum(jnp.where(eye, h_c, 0.0), axis=1, keepdims=True)
        h_ref[:, :, sl] = h_l * jax.nn.sigmoid(o_ref[:, :, sl])
        wg = jnp.exp(li - m_row)
        decay = jnp.exp(inter - m_row)
        c_out[:, h] = decay * c + (v_c * wg) * k
        n_out[:, h] = decay * n + wg * k
        m_out[:, h] = m_row


def _mlstm_step(z3, li, fp, c, n, m, nh, dh):
    batch = c.shape[0]
    bb = SUBLANES
    width = nh * dh
    blk = lambda j: pl.BlockSpec((bb, 1, width), lambda i: (i, 0, j))
    st4 = lambda a, b: pl.BlockSpec((bb, nh, a, b), lambda i: (i, 0, 0, 0))
    return pl.pallas_call(
        functools.partial(_mlstm_step_kernel, nh=nh, dh=dh),
        grid=(batch // bb,),
        in_specs=[blk(1), blk(2), blk(3), blk(4), st4(1, 1), st4(1, 1),
                  st4(dh, dh), st4(1, dh), st4(1, 1)],
        out_specs=[pl.BlockSpec((bb, 1, width), lambda i: (i, 0, 0)),
                   st4(dh, dh), st4(1, dh), st4(1, 1)],
        out_shape=[
            jax.ShapeDtypeStruct((batch, 1, width), F32),
            jax.ShapeDtypeStruct((batch, nh, dh, dh), F32),
            jax.ShapeDtypeStruct((batch, nh, 1, dh), F32),
            jax.ShapeDtypeStruct((batch, nh, 1, 1), F32),
        ],
        compiler_params=_params("parallel"),
        name="mlstm_step",
    )(z3, z3, z3, z3, li, fp, c, n, m)


def _route(hn, wr_ref, br_ref, tril_ref, carry_ref, rinfo_ref):
    ngrp, epg = MOE_GROUPS, MOE_EXPERTS_PER_GROUP
    logits = jnp.dot(hn, wr_ref[...], precision=HIGHEST, preferred_element_type=F32) + br_ref[...]
    tm = logits.shape[0]
    lane = lax.broadcasted_iota(I32, (tm, LANES), 1)
    neg = -jnp.inf

    def first_max(x):
        mx = jnp.max(x, axis=1, keepdims=True)
        idx = jnp.min(jnp.where(x == mx, lane, LANES), axis=1, keepdims=True)
        return mx, idx

    is_grp = lane < ngrp
    gmax, gsel = first_max(jnp.where(is_grp, logits, neg))
    g_w = 1.0 / jnp.sum(jnp.where(is_grp, jnp.exp(logits - gmax), 0.0), axis=1, keepdims=True)
    lo = ngrp + gsel * epg
    el = jnp.where((lane >= lo) & (lane < lo + epg), logits, neg)
    v1, i1 = first_max(el)
    v2, i2 = first_max(jnp.where(lane == i1, neg, el))
    e2 = jnp.exp(v2 - v1)
    w1 = g_w / (1.0 + e2)
    w2 = g_w * e2 / (1.0 + e2)
    eid1 = i1 - ngrp
    eid2 = i2 - ngrp

    hit1 = lane == eid1
    hit2 = lane == eid2
    onehot = jnp.where(hit1 | hit2, 1.0, 0.0)
    carry = carry_ref[...]
    prefix = _dot(tril_ref[...], onehot.astype(BF16)) + carry
    rank1 = jnp.sum(jnp.where(hit1, prefix, 0.0), axis=1, keepdims=True)
    rank2 = jnp.sum(jnp.where(hit2, prefix, 0.0), axis=1, keepdims=True)
    carry_ref[...] = carry + jnp.sum(onehot, axis=0, keepdims=True)

    cols = (eid1.astype(F32), eid2.astype(F32), w1, w2, rank1, rank2)
    info = jnp.zeros((tm, LANES), F32)
    for j, cval in enumerate(cols):
        info = jnp.where(lane == j, cval, info)
    rinfo_ref[...] = info


def _outproj_route_kernel(x_ref, p_ref, m_ref, wo_ref, g_ref, wr_ref, br_ref, tril_ref,
                          x1_ref, hn_ref, rinfo_ref, cnt_ref, carry_ref):
    @pl.when(pl.program_id(0) == 0)
    def _():
        carry_ref[...] = jnp.zeros(carry_ref.shape, F32)

    half = p_ref.shape[1]
    mix = _dot(p_ref[...], wo_ref[0:half, :]) + _dot(m_ref[...], wo_ref[half:2 * half, :])
    x1 = x_ref[...] + mix
    x1_ref[...] = x1
    hn = _rms(x1, g_ref[...])
    hn_ref[...] = hn
    _route(hn, wr_ref, br_ref, tril_ref, carry_ref, rinfo_ref)
    cnt_ref[...] = carry_ref[...]


def _glu_route_kernel(x_ref, y_ref, wglu_ref, g_ref, wr_ref, br_ref, tril_ref,
                      x1_ref, hn_ref, rinfo_ref, cnt_ref, carry_ref):
    @pl.when(pl.program_id(0) == 0)
    def _():
        carry_ref[...] = jnp.zeros(carry_ref.shape, F32)

    d = x_ref.shape[1]
    ag = _dot(jax.nn.gelu(y_ref[...]).astype(BF16), wglu_ref[...])
    x1 = x_ref[...] + ag[:, 0:d] * jax.nn.sigmoid(ag[:, d:2 * d])
    x1_ref[...] = x1
    hn = _rms(x1, g_ref[...])
    hn_ref[...] = hn
    _route(hn, wr_ref, br_ref, tril_ref, carry_ref, rinfo_ref)
    cnt_ref[...] = carry_ref[...]


def _mix_route(kernel, name, x, acts, w, g, wr, br, tril):
    n, d = x.shape
    tm = TOKEN_TILE
    full = lambda i: (0, 0)
    row = lambda a: pl.BlockSpec((tm, a.shape[1]), lambda i: (i, 0))
    return pl.pallas_call(
        kernel,
        grid=(n // tm,),
        in_specs=[row(x)] + [row(a) for a in acts] + [
            pl.BlockSpec(w.shape, full),
            pl.BlockSpec((1, d), full),
            pl.BlockSpec((d, LANES), full),
            pl.BlockSpec((1, LANES), full),
            pl.BlockSpec((tm, tm), full),
        ],
        out_specs=[
            pl.BlockSpec((tm, d), lambda i: (i, 0)),
            pl.BlockSpec((tm, d), lambda i: (i, 0)),
            pl.BlockSpec((tm, LANES), lambda i: (i, 0)),
            pl.BlockSpec((1, LANES), full),
        ],
        out_shape=[
            jax.ShapeDtypeStruct((n, d), F32),
            jax.ShapeDtypeStruct((n, d), F32),
            jax.ShapeDtypeStruct((n, LANES), F32),
            jax.ShapeDtypeStruct((1, LANES), F32),
        ],
        scratch_shapes=[pltpu.VMEM((1, LANES), F32)],
        compiler_params=_params("arbitrary"),
        name=name,
    )(x, *acts, w, g, wr, br, tril)


def _moe_kernel(te_ref, ton_ref, src_hbm, hn_hbm, cw_ref, wg_ref, wu_ref, wd_ref,
                eo_ref, idx_s, xbuf, wgb, wub, wdb, sem_i, sem_g):
    i = pl.program_id(0)
    tr = xbuf.shape[0]
    prev = jnp.maximum(i - 1, 0)
    new_expert = jnp.logical_or(i == 0, te_ref[i] != te_ref[prev])

    @pl.when(ton_ref[i] == 1)
    def _():
        cp = pltpu.make_async_copy(src_hbm.at[i], idx_s, sem_i)
        cp.start()
        cp.wait()

        def issue(r, carry):
            t = idx_s[0, r]
            pltpu.make_async_copy(hn_hbm.at[pl.ds(t, 1)], xbuf.at[pl.ds(r, 1)], sem_g).start()
            return carry

        lax.fori_loop(0, tr, issue, 0)

        @pl.when(new_expert)
        def _():
            wgb[...] = wg_ref[...].astype(BF16)
            wub[...] = wu_ref[...].astype(BF16)
            wdb[...] = wd_ref[...].astype(BF16)

        pltpu.make_async_copy(hn_hbm.at[pl.ds(0, tr)], xbuf, sem_g).wait()
        x = xbuf[...].astype(BF16)
        hg = _dot(x, wgb[...])
        hu = _dot(x, wub[...])
        act = (jax.nn.silu(hg) * hu) * cw_ref[...]
        eo_ref[...] = _dot(act.astype(BF16), wdb[...])

    @pl.when(ton_ref[i] == 0)
    def _():
        eo_ref[...] = jnp.zeros(eo_ref.shape, F32)


def _moe_experts(tile_e, tile_on, src, hn, cw, wg, wu, wd):
    nt, _, tr = src.shape
    d = hn.shape[1]
    hid = wg.shape[2]
    grid_spec = pltpu.PrefetchScalarGridSpec(
        num_scalar_prefetch=2,
        grid=(nt,),
        in_specs=[
            pl.BlockSpec(memory_space=pl.ANY),
            pl.BlockSpec(memory_space=pl.ANY),
            pl.BlockSpec((None, tr, 1), lambda i, te, ton: (i, 0, 0)),
            pl.BlockSpec((None, d, hid), lambda i, te, ton: (te[i], 0, 0)),
            pl.BlockSpec((None, d, hid), lambda i, te, ton: (te[i], 0, 0)),
            pl.BlockSpec((None, hid, d), lambda i, te, ton: (te[i], 0, 0)),
        ],
        out_specs=pl.BlockSpec((tr, d), lambda i, te, ton: (i, 0)),
        scratch_shapes=[
            pltpu.SMEM((1, tr), I32),
            pltpu.VMEM((tr, d), F32),
            pltpu.VMEM((d, hid), BF16),
            pltpu.VMEM((d, hid), BF16),
            pltpu.VMEM((hid, d), BF16),
            pltpu.SemaphoreType.DMA,
            pltpu.SemaphoreType.DMA,
        ],
    )
    return pl.pallas_call(
        _moe_kernel,
        grid_spec=grid_spec,
        out_shape=jax.ShapeDtypeStruct((nt * tr, d), F32),
        compiler_params=_params("arbitrary"),
        name="moe_experts",
    )(tile_e, tile_on, src, hn, cw, wg, wu, wd)


def _combine_kernel(pos_hbm, eo_hbm, x_ref, g_ref, x2_ref, hn_ref, idx_s, a_buf, b_buf,
                    sem_i, sem_a, sem_b):
    i = pl.program_id(0)
    tm = a_buf.shape[0]
    cp = pltpu.make_async_copy(pos_hbm.at[i], idx_s, sem_i)
    cp.start()
    cp.wait()

    def issue(r, carry):
        pltpu.make_async_copy(eo_hbm.at[pl.ds(idx_s[0, r], 1)], a_buf.at[pl.ds(r, 1)], sem_a).start()
        pltpu.make_async_copy(eo_hbm.at[pl.ds(idx_s[0, tm + r], 1)], b_buf.at[pl.ds(r, 1)], sem_b).start()
        return carry

    lax.fori_loop(0, tm, issue, 0)
    pltpu.make_async_copy(eo_hbm.at[pl.ds(0, tm)], a_buf, sem_a).wait()
    pltpu.make_async_copy(eo_hbm.at[pl.ds(0, tm)], b_buf, sem_b).wait()
    x2 = x_ref[...] + (a_buf[...] + b_buf[...])
    x2_ref[...] = x2
    hn_ref[...] = _rms(x2, g_ref[...])


def _combine(pos_tiles, eo, x, g):
    n, d = x.shape
    tm = TOKEN_TILE
    return pl.pallas_call(
        _combine_kernel,
        grid=(n // tm,),
        in_specs=[
            pl.BlockSpec(memory_space=pl.ANY),
            pl.BlockSpec(memory_space=pl.ANY),
            pl.BlockSpec((tm, d), lambda i: (i, 0)),
            pl.BlockSpec((1, d), lambda i: (0, 0)),
        ],
        out_specs=[pl.BlockSpec((tm, d), lambda i: (i, 0)),
                   pl.BlockSpec((tm, d), lambda i: (i, 0))],
        out_shape=[jax.ShapeDtypeStruct((n, d), F32), jax.ShapeDtypeStruct((n, d), F32)],
        scratch_shapes=[
            pltpu.SMEM((1, 2 * tm), I32),
            pltpu.VMEM((tm, d), F32),
            pltpu.VMEM((tm, d), F32),
            pltpu.SemaphoreType.DMA,
            pltpu.SemaphoreType.DMA,
            pltpu.SemaphoreType.DMA,
        ],
        compiler_params=_params("arbitrary"),
        name="moe_combine",
    )(pos_tiles, eo, x, g)


def _moe(x1, hn, rinfo, counts, wg, wu, wd, next_gain):
    n = x1.shape[0]
    ne = wg.shape[0]
    tr = EXPERT_ROW_TILE
    tm = TOKEN_TILE
    nt = (2 * n + ne * (tr - 1)) // tr + 1
    eid = rinfo[:, 0:2].astype(I32)
    wts = rinfo[:, 2:4]
    rank = rinfo[:, 4:6].astype(I32)
    cnt = counts[0, :ne].astype(I32)
    tiles_e = (cnt + tr - 1) // tr
    tile_end = jnp.cumsum(tiles_e)
    off = (tile_end - tiles_e) * tr
    pos = off[eid] + rank
    used = tile_end[-1]
    tix = jnp.arange(nt, dtype=I32)
    tile_on = (tix < used).astype(I32)
    tile_e = jnp.searchsorted(tile_end, jnp.minimum(tix, used - 1), side="right").astype(I32)
    flat = pos.reshape(-1)
    tok = jnp.repeat(jnp.arange(n, dtype=I32), 2)
    src = jnp.zeros((nt * tr,), I32).at[flat].set(tok).reshape(nt, 1, tr)
    cw = jnp.zeros((nt * tr,), F32).at[flat].set(wts.reshape(-1)).reshape(nt, tr, 1)
    eo = _moe_experts(tile_e, tile_on, src, hn, cw, wg, wu, wd)
    pos_tiles = pos.reshape(n // tm, tm, 2).transpose(0, 2, 1).reshape(n // tm, 1, 2 * tm)
    return _combine(pos_tiles, eo, x1, next_gain)


def _s5_prep_kernel(ls_ref, lrc_ref, lic_ref, lrr_ref, lir_ref, cret_ref, cimt_ref,
                    bret_ref, bimt_ref,
                    t_ref, wre_ref, wim_ref, cyre_ref, cyim_ref, are_ref, aim_ref,
                    bbr_ref, bbi_ref, lbr_ref, lbi_ref):
    gw = bbr_ref.shape[1]
    sub = bret_ref.shape[1] // gw
    wide = sub * gw
    shift = gw.bit_length() - 1
    dt = jnp.exp(ls_ref[0])
    lr_c, li_c = lrc_ref[0], lic_ref[0]
    lr_r, li_r = lrr_ref[0], lir_ref[0]
    cre, cim = cret_ref[0], cimt_ref[0]

    def c_lam_pow(tau):
        mag = jnp.exp(tau * (lr_c * dt))
        ang = tau * (li_c * dt)
        pr, pi = mag * jnp.cos(ang), mag * jnp.sin(ang)
        return pr * cre - pi * cim, -(pr * cim + pi * cre)

    jl = jnp.right_shift(lax.broadcasted_iota(I32, (1, wide), 1), shift).astype(F32)
    clr0, cli0 = c_lam_pow(jl)
    clr1, cli1 = c_lam_pow(jl + 1.0)
    cyre_ref[0] = clr1.astype(BF16)
    cyim_ref[0] = cli1.astype(BF16)

    mag = jnp.exp(lr_r * dt)
    lbr, lbi = mag * jnp.cos(li_r * dt), mag * jnp.sin(li_r * dt)
    lbr_ref[0] = lbr
    lbi_ref[0] = lbi
    nr, ni = lbr - 1.0, lbi
    den = lr_r * lr_r + li_r * li_r
    fr = (nr * lr_r + ni * li_r) / den
    fi = (ni * lr_r - nr * li_r) / den
    bre, bim = bret_ref[0], bimt_ref[0]
    bbr = fr * bre - fi * bim
    bbi = fr * bim + fi * bre
    bbr_ref[0] = bbr[0:gw]
    bbi_ref[0] = bbi[0:gw]

    r = (jnp.dot(bbr[0:gw], clr0, precision=HIGHEST, preferred_element_type=F32)
         + jnp.dot(bbi[0:gw], cli0, precision=HIGHEST, preferred_element_type=F32))
    lane = lax.broadcasted_iota(I32, (gw, wide), 1)
    for j in range(sub):
        blk = r if j == 0 else jnp.where(lane >= gw * j, pltpu.roll(r, gw * j, 1), 0.0)
        t_ref[0, gw * j:gw * (j + 1), :] = blk.astype(BF16)

    rj = (sub - 1) - jnp.right_shift(lax.broadcasted_iota(I32, (wide, 1), 0), shift)
    rj = rj.astype(F32)
    mag = jnp.exp(rj * (lr_r * dt))
    ang = rj * (li_r * dt)
    pr, pi = mag * jnp.cos(ang), mag * jnp.sin(ang)
    wre_ref[0] = (pr * bbr - pi * bbi).astype(BF16)
    wim_ref[0] = (pr * bbi + pi * bbr).astype(BF16)
    mag = jnp.exp(float(sub) * (lr_r * dt))
    ang = float(sub) * (li_r * dt)
    are_ref[0] = mag * jnp.cos(ang)
    aim_ref[0] = mag * jnp.sin(ang)


def _s5_prep(log_step, lam_re, lam_im, b_re, b_im, c_re, c_im):
    ng, ns = lam_re.shape
    gw = b_re.shape[2]
    sub = S5_SUB
    wide = sub * gw
    ls = log_step.reshape(ng, 1, 1)
    lrc, lic = lam_re.reshape(ng, ns, 1), lam_im.reshape(ng, ns, 1)
    lrr, lir = lam_re.reshape(ng, 1, ns), lam_im.reshape(ng, 1, ns)
    cret = jnp.tile(jnp.transpose(c_re, (0, 2, 1)), (1, 1, sub))
    cimt = jnp.tile(jnp.transpose(c_im, (0, 2, 1)), (1, 1, sub))
    bret = jnp.tile(jnp.transpose(b_re, (0, 2, 1)), (1, sub, 1))
    bimt = jnp.tile(jnp.transpose(b_im, (0, 2, 1)), (1, sub, 1))
    ins = [ls, lrc, lic, lrr, lir, cret, cimt, bret, bimt]
    spec = lambda a: pl.BlockSpec((1,) + a.shape[1:], lambda g: (g, 0, 0))
    outs = [
        jax.ShapeDtypeStruct((ng, wide, wide), BF16),
        jax.ShapeDtypeStruct((ng, wide, ns), BF16),
        jax.ShapeDtypeStruct((ng, wide, ns), BF16),
        jax.ShapeDtypeStruct((ng, ns, wide), BF16),
        jax.ShapeDtypeStruct((ng, ns, wide), BF16),
        jax.ShapeDtypeStruct((ng, 1, ns), F32),
        jax.ShapeDtypeStruct((ng, 1, ns), F32),
        jax.ShapeDtypeStruct((ng, gw, ns), F32),
        jax.ShapeDtypeStruct((ng, gw, ns), F32),
        jax.ShapeDtypeStruct((ng, 1, ns), F32),
        jax.ShapeDtypeStruct((ng, 1, ns), F32),
    ]
    return pl.pallas_call(
        _s5_prep_kernel,
        grid=(ng,),
        in_specs=[spec(a) for a in ins],
        out_specs=[spec(o) for o in outs],
        out_shape=outs,
        compiler_params=_params("parallel"),
        name="s5_prep",
    )(*ins)


def _s5_prompt_kernel(u_ref, t_ref, wre_ref, wim_ref, cyre_ref, cyim_ref, are_ref, aim_ref,
                      d_ref, y_ref, hre_out, him_out, vre_s, vim_s, hre_s, him_s, *, batch):
    u = u_ref[0]
    ub = u.astype(BF16)
    vre_s[...] = _dot(ub, wre_ref[0])
    vim_s[...] = _dot(ub, wim_ref[0])
    ns = vre_s.shape[1]
    nk = u.shape[0] // batch
    ar = jnp.broadcast_to(are_ref[0], (batch, ns))
    ai = jnp.broadcast_to(aim_ref[0], (batch, ns))

    def step(k, carry):
        hr, hi = carry
        r0 = pl.multiple_of(k * batch, batch)
        hre_s[pl.ds(r0, batch), :] = hr
        him_s[pl.ds(r0, batch), :] = hi
        vr = vre_s[pl.ds(r0, batch), :]
        vi = vim_s[pl.ds(r0, batch), :]
        return ar * hr - ai * hi + vr, ar * hi + ai * hr + vi

    zero = jnp.zeros((batch, ns), F32)
    hr, hi = lax.fori_loop(0, nk, step, (zero, zero))
    hre_out[0] = hr
    him_out[0] = hi
    y = _dot(ub, t_ref[0])
    y = y + _dot(hre_s[...].astype(BF16), cyre_ref[0]) + _dot(him_s[...].astype(BF16), cyim_ref[0])
    y_ref[0] = y + d_ref[0] * u


def _s5_prompt(u_g, prep, d_t, batch):
    ng, rows, wide = u_g.shape
    tmat, wre, wim, cyre, cyim, are, aim = prep[:7]
    ns = are.shape[2]
    spec = lambda a: pl.BlockSpec((1,) + a.shape[1:], lambda g: (g, 0, 0))
    outs = [
        jax.ShapeDtypeStruct((ng, rows, wide), F32),
        jax.ShapeDtypeStruct((ng, batch, ns), F32),
        jax.ShapeDtypeStruct((ng, batch, ns), F32),
    ]
    ins = [u_g, tmat, wre, wim, cyre, cyim, are, aim, d_t]
    return pl.pallas_call(
        functools.partial(_s5_prompt_kernel, batch=batch),
        grid=(ng,),
        in_specs=[spec(a) for a in ins],
        out_specs=[spec(o) for o in outs],
        out_shape=outs,
        scratch_shapes=[pltpu.VMEM((rows, ns), F32) for _ in range(4)],
        compiler_params=_params("parallel"),
        name="s5_prompt",
    )(*ins)


def _s5_step_kernel(u_ref, h_ref, hsw_ref, bb_ref, la_ref, lb_ref, cc_ref, d_ref, hn_ref, y_ref):
    for g in range(u_ref.shape[0]):
        u = u_ref[g]
        hn = (la_ref[g] * h_ref[g] + lb_ref[g] * hsw_ref[g]
              + jnp.dot(u, bb_ref[g], precision=HIGHEST, preferred_element_type=F32))
        hn_ref[g] = hn
        y_ref[g] = (jnp.dot(hn, cc_ref[g], precision=HIGHEST, preferred_element_type=F32)
                    + d_ref[g] * u)


def _s5_step(u_p, h_cat, h_swp, bb_cat, la, lb, cc, d_p):
    ng = u_p.shape[0]
    gb = SUBLANES
    spec = lambda a: pl.BlockSpec((gb,) + a.shape[1:], lambda g: (g, 0, 0))
    ins = [u_p, h_cat, h_swp, bb_cat, la, lb, cc, d_p]
    outs = [jax.ShapeDtypeStruct(h_cat.shape, F32), jax.ShapeDtypeStruct(u_p.shape, F32)]
    return pl.pallas_call(
        _s5_step_kernel,
        grid=(ng // gb,),
        in_specs=[spec(a) for a in ins],
        out_specs=[spec(o) for o in outs],
        out_shape=outs,
        compiler_params=_params("parallel"),
        name="s5_step",
    )(*ins)


def _pad_to(a, axis, size):
    pad = [(0, 0)] * a.ndim
    pad[axis] = (0, size - a.shape[axis])
    return jnp.pad(a, pad)


def kernel(x_prompt, x_sample, state_pool, state_mlstm_c, state_mlstm_n, state_mlstm_m, state_s5_re, state_s5_im, norm_mix, norm_ffn, norm_final, w_in_ab, b_gates, pool_w, pool_scale, w_out_ab, s5_lam_re, s5_lam_im, s5_log_step, s5_b_re, s5_b_im, s5_c_re, s5_c_im, s5_d, w_glu, moe_w_group, moe_b_group, moe_w_expert, moe_b_expert, moe_w_gate, moe_w_up, moe_w_down):
    bp, tp, d = x_prompt.shape
    bs = x_sample.shape[0]
    n_p = bp * tp
    n = n_p + bs
    nh, dh = state_mlstm_c.shape[2], state_mlstm_c.shape[3]
    pool_width = state_pool.shape[3]
    ml_width = nh * dh
    n_main = pool_width + 4 * ml_width
    n_gates = 2 * nh
    ngrp_s5, n_state = s5_lam_re.shape[1], s5_lam_re.shape[2]
    gw_s5 = d // ngrp_s5
    tm = TOKEN_TILE

    x0 = jnp.concatenate([x_prompt.reshape(n_p, d), x_sample.reshape(bs, d)], axis=0)
    tril = jnp.tril(jnp.ones((tm, tm), BF16), -1)

    def router_weights(l):
        wr = jnp.concatenate([moe_w_group[l], moe_w_expert[l]], axis=1)
        br = jnp.concatenate([moe_b_group[l], moe_b_expert[l]])[None, :]
        return _pad_to(wr, 1, LANES), _pad_to(br, 1, LANES)

    w_in = w_in_ab[0]
    w_g = w_in[:, n_main:]
    z, gates, gates_t = _inproj(
        x0, norm_mix[0][None, :], w_in[:, :n_main].astype(BF16),
        _pad_to(w_g, 1, LANES).astype(BF16), w_g.T.astype(BF16),
        _pad_to(b_gates[0][None, :], 1, LANES), b_gates[0][:, None])

    pw = pool_w[0].astype(BF16)
    ps = pool_scale[0][None, :]
    pool_y_p, pool_p = _pool_prompt(z, pw, ps, bp, tp)
    pool_y_s, pool_s_t = _pool_step(z, jnp.transpose(state_pool[0], (1, 0, 2)), pw, ps, n_p)
    pool_s = jnp.transpose(pool_s_t, (1, 0, 2))

    ml_y_p, c_p, n_p_st, m_p = _mlstm_prompt(z, gates_t, bp, tp, nh, dh)
    g_s = gates[n_p:, :n_gates]
    ml_y_s, c_s, n_s_st, m_s = _mlstm_step(
        z[n_p:].reshape(bs, 1, n_main),
        g_s[:, :nh].reshape(bs, nh, 1, 1), g_s[:, nh:].reshape(bs, nh, 1, 1),
        state_mlstm_c[0], state_mlstm_n[0].reshape(bs, nh, 1, dh),
        state_mlstm_m[0].reshape(bs, nh, 1, 1), nh, dh)

    pool_y = jnp.concatenate([pool_y_p, pool_y_s], axis=0)
    ml_y = jnp.concatenate([ml_y_p, ml_y_s.reshape(bs, ml_width).astype(BF16)], axis=0)
    wr, br = router_weights(0)
    x1, hn, rinfo, counts = _mix_route(
        _outproj_route_kernel, "outproj_route", x0, [pool_y, ml_y], w_out_ab[0].astype(BF16),
        norm_ffn[0][None, :], wr, br, tril)
    x2, h1 = _moe(x1, hn, rinfo, counts, moe_w_gate[0], moe_w_up[0], moe_w_down[0],
                  norm_mix[1][None, :])

    prep = _s5_prep(s5_log_step[0], s5_lam_re[0], s5_lam_im[0], s5_b_re[0], s5_b_im[0],
                    s5_c_re[0], s5_c_im[0])
    bbr, bbi, lbr, lbi = prep[7:]
    sub = S5_SUB
    nk = tp // sub
    d_g = s5_d[0].reshape(ngrp_s5, 1, gw_s5)
    u_g = h1[:n_p].reshape(bp, nk, sub, ngrp_s5, gw_s5).transpose(3, 1, 0, 2, 4)
    u_g = u_g.reshape(ngrp_s5, nk * bp, sub * gw_s5)
    y_g, hre_p, him_p = _s5_prompt(u_g, prep, jnp.tile(d_g, (1, 1, sub)), bp)
    y_p = y_g.reshape(ngrp_s5, nk, bp, sub, gw_s5).transpose(2, 1, 3, 0, 4).reshape(n_p, d)
    s5_re_p = jnp.transpose(hre_p, (1, 0, 2))
    s5_im_p = jnp.transpose(him_p, (1, 0, 2))

    u_s = h1[n_p:].reshape(bs, ngrp_s5, gw_s5).transpose(1, 0, 2)
    h_re = jnp.transpose(state_s5_re[0], (1, 0, 2))
    h_im = jnp.transpose(state_s5_im[0], (1, 0, 2))
    cc = jnp.concatenate([jnp.transpose(s5_c_re[0], (0, 2, 1)),
                          -jnp.transpose(s5_c_im[0], (0, 2, 1))], axis=1)
    hn_s, y_s = _s5_step(
        _pad_to(u_s, 2, LANES),
        jnp.concatenate([h_re, h_im], axis=2), jnp.concatenate([h_im, h_re], axis=2),
        _pad_to(jnp.concatenate([bbr, bbi], axis=2), 1, LANES),
        jnp.concatenate([lbr, lbr], axis=2), jnp.concatenate([-lbi, lbi], axis=2),
        _pad_to(cc, 2, LANES), _pad_to(d_g, 2, LANES))
    s5_re_s = jnp.transpose(hn_s[:, :, :n_state], (1, 0, 2))
    s5_im_s = jnp.transpose(hn_s[:, :, n_state:], (1, 0, 2))
    y_all = jnp.concatenate(
        [y_p, jnp.transpose(y_s[:, :, :gw_s5], (1, 0, 2)).reshape(bs, d)], axis=0)

    wr, br = router_weights(1)
    x3, hn, rinfo, counts = _mix_route(
        _glu_route_kernel, "glu_route", x2, [y_all], w_glu[0].astype(BF16),
        norm_ffn[1][None, :], wr, br, tril)
    _, y_out = _moe(x3, hn, rinfo, counts, moe_w_gate[1], moe_w_up[1], moe_w_down[1],
                    norm_final[None, :])

    return (y_out[:n_p].reshape(bp, tp, d), y_out[n_p:].reshape(bs, 1, d),
            pool_p[None], c_p[None], n_p_st.reshape(1, bp, nh, dh), m_p[:, :, 0, 0][None],
            s5_re_p[None], s5_im_p[None],
            pool_s[None], c_s[None], n_s_st.reshape(1, bs, nh, dh), m_s.reshape(1, bs, nh),
            s5_re_s[None], s5_im_s[None])
```

```python
import functools

import jax
import jax.numpy as jnp
from jax import lax
from jax.experimental import pallas as pl
from jax.experimental.pallas import tpu as pltpu

F32 = jnp.float32
BF16 = jnp.bfloat16
I32 = jnp.int32

PAST_LEN = 16384
POOL_WINDOWS = (2, 4, 8, 16)
POOL_BUF = max(POOL_WINDOWS) - 1
MLSTM_CHUNK = 128
S5_SUB = 16
MOE_GROUPS = 4
MOE_EXPERTS_PER_GROUP = 8
RMS_EPS = 1e-6

LANES = 128
SUBLANES = 8
VMEM_LIMIT_BYTES = 56 * 1024 * 1024

TOKEN_TILE = 512
EXPERT_ROW_TILE = 256
POOL_TIME_TILE = 512
DMA_ISSUE_UNROLL = 8
S5_GROUPS_PER_BLOCK = 4
S5_BATCH_PER_STEP = 4

HIGHEST = lax.Precision.HIGHEST


def _params(*sem):
    return pltpu.CompilerParams(dimension_semantics=sem, vmem_limit_bytes=VMEM_LIMIT_BYTES)


def _rms(x, g):
    return x * lax.rsqrt(jnp.mean(x * x, axis=-1, keepdims=True) + RMS_EPS) * g


def _dot(a, b):
    return jnp.dot(a, b, preferred_element_type=F32)


def _dot_nt(a, b):
    return lax.dot_general(a, b, (((1,), (1,)), ((), ())), preferred_element_type=F32)


def _dot_tn(a, b):
    return lax.dot_general(a, b, (((0,), (0,)), ((), ())), preferred_element_type=F32)


def _inproj_kernel(x_ref, g_ref, w_ref, wg_ref, wgt_ref, bg_ref, bgt_ref,
                   z_ref, gates_ref, gatest_ref):
    h = _rms(x_ref[...], g_ref[...]).astype(BF16)
    z_ref[...] = _dot(h, w_ref[...])
    gates_ref[...] = _dot(h, wg_ref[...]) + bg_ref[...]
    gatest_ref[...] = _dot_nt(wgt_ref[...], h) + bgt_ref[...]


def _inproj(x, g, w, wg, wgt, bg, bgt):
    n, d = x.shape
    nz = w.shape[1]
    ng = wgt.shape[0]
    tm = TOKEN_TILE
    full = lambda i: (0, 0)
    return pl.pallas_call(
        _inproj_kernel,
        grid=(n // tm,),
        in_specs=[
            pl.BlockSpec((tm, d), lambda i: (i, 0)),
            pl.BlockSpec((1, d), full),
            pl.BlockSpec((d, nz), full),
            pl.BlockSpec((d, LANES), full),
            pl.BlockSpec((ng, d), full),
            pl.BlockSpec((1, LANES), full),
            pl.BlockSpec((ng, 1), full),
        ],
        out_specs=[
            pl.BlockSpec((tm, nz), lambda i: (i, 0)),
            pl.BlockSpec((tm, LANES), lambda i: (i, 0)),
            pl.BlockSpec((ng, tm), lambda i: (0, i)),
        ],
        out_shape=[
            jax.ShapeDtypeStruct((n, nz), F32),
            jax.ShapeDtypeStruct((n, LANES), F32),
            jax.ShapeDtypeStruct((ng, n), F32),
        ],
        compiler_params=_params("parallel"),
        name="inproj",
    )(x, g, w, wg, wgt, bg, bgt)


def _pool_prompt_kernel(u_ref, pw_ref, ps_ref, y_ref, st_ref, ext_ref, *, tt, gw):
    t = pl.program_id(1)
    nt = pl.num_programs(1)
    halo = POOL_BUF + 1
    width = ext_ref.shape[1]

    @pl.when(t == 0)
    def _():
        ext_ref[0:halo, :] = jnp.zeros((halo, width), F32)

    u = u_ref[...]
    ext_ref[halo:halo + tt, :] = u
    pos = t * tt + lax.broadcasted_iota(I32, (tt, 1), 0)
    for g, w in enumerate(POOL_WINDOWS):
        c0 = g * gw
        acc = u[:, c0:c0 + gw]
        for j in range(1, w):
            acc = acc + ext_ref[halo - j:halo - j + tt, c0:c0 + gw]
        cnt = jnp.minimum(w, pos + 1).astype(F32)
        d = acc / cnt - u[:, c0:c0 + gw]
        y = _dot(d.astype(BF16), pw_ref[g]) * ps_ref[:, c0:c0 + gw]
        y_ref[:, c0:c0 + gw] = y.astype(BF16)

    @pl.when(t == nt - 1)
    def _():
        st_ref[...] = ext_ref[tt + 1:tt + halo, :]

    ext_ref[0:halo, :] = ext_ref[tt:tt + halo, :]


def _pool_prompt(z, pw, ps, batch, seq):
    width = ps.shape[1]
    gw = width // len(POOL_WINDOWS)
    tt = POOL_TIME_TILE
    nt = seq // tt
    return pl.pallas_call(
        functools.partial(_pool_prompt_kernel, tt=tt, gw=gw),
        grid=(batch, nt),
        in_specs=[
            pl.BlockSpec((tt, width), lambda b, t: (b * nt + t, 0)),
            pl.BlockSpec(pw.shape, lambda b, t: (0, 0, 0)),
            pl.BlockSpec((1, width), lambda b, t: (0, 0)),
        ],
        out_specs=[
            pl.BlockSpec((tt, width), lambda b, t: (b * nt + t, 0)),
            pl.BlockSpec((None, POOL_BUF, width), lambda b, t: (b, 0, 0)),
        ],
        out_shape=[
            jax.ShapeDtypeStruct((batch * seq, width), BF16),
            jax.ShapeDtypeStruct((batch, POOL_BUF, width), F32),
        ],
        scratch_shapes=[pltpu.VMEM((POOL_BUF + 1 + tt, width), F32)],
        compiler_params=_params("parallel", "arbitrary"),
        name="pool_prompt",
    )(z, pw, ps)


def _pool_step_kernel(u_ref, buf_ref, pw_ref, ps_ref, y_ref, nb_ref, *, gw):
    u = u_ref[...]
    for g, w in enumerate(POOL_WINDOWS):
        c0 = g * gw
        acc = u[:, c0:c0 + gw]
        for j in range(1, w):
            acc = acc + buf_ref[POOL_BUF - j, :, c0:c0 + gw]
        cnt = float(min(w, PAST_LEN + 1))
        d = acc / cnt - u[:, c0:c0 + gw]
        y = _dot(d.astype(BF16), pw_ref[g]) * ps_ref[:, c0:c0 + gw]
        y_ref[:, c0:c0 + gw] = y.astype(BF16)
    nb_ref[0:POOL_BUF - 1] = buf_ref[1:POOL_BUF]
    nb_ref[POOL_BUF - 1] = u


def _pool_step(z, buf_t, pw, ps, row0):
    _, batch, width = buf_t.shape
    gw = width // len(POOL_WINDOWS)
    return pl.pallas_call(
        functools.partial(_pool_step_kernel, gw=gw),
        grid=(1,),
        in_specs=[
            pl.BlockSpec((batch, width), lambda i: (row0 // batch, 0)),
            pl.BlockSpec(buf_t.shape, lambda i: (0, 0, 0)),
            pl.BlockSpec(pw.shape, lambda i: (0, 0, 0)),
            pl.BlockSpec((1, width), lambda i: (0, 0)),
        ],
        out_specs=[
            pl.BlockSpec((batch, width), lambda i: (0, 0)),
            pl.BlockSpec(buf_t.shape, lambda i: (0, 0, 0)),
        ],
        out_shape=[
            jax.ShapeDtypeStruct((batch, width), BF16),
            jax.ShapeDtypeStruct(buf_t.shape, F32),
        ],
        compiler_params=_params("arbitrary"),
        name="pool_step",
    )(z, buf_t, pw, ps)


def _mlstm_prompt_kernel(q_ref, k_ref, v_ref, o_ref, gt_ref,
                         h_ref, c_out, n_out, m_out, c_s, n_s, m_s, *, nh, dh):
    ci = pl.program_id(1)
    nc = pl.num_programs(1)
    ln = q_ref.shape[0]

    @pl.when(ci == 0)
    def _():
        c_s[...] = jnp.zeros(c_s.shape, F32)
        n_s[...] = jnp.zeros(n_s.shape, F32)
        m_s[...] = jnp.zeros(m_s.shape, F32)

    row = lax.broadcasted_iota(I32, (ln, ln), 0)
    col = lax.broadcasted_iota(I32, (ln, ln), 1)
    causal = col <= row
    eye = col == row

    gt = gt_ref[...]
    li_all = gt[0:nh]
    bc_all = jax.nn.log_sigmoid(gt[nh:2 * nh])
    lane = lax.broadcasted_iota(I32, (nh, ln), 1)
    s = 1
    while s < ln:
        bc_all = bc_all + jnp.where(lane >= s, pltpu.roll(bc_all, s, 1), 0.0)
        s *= 2

    def to_col(r):
        return jnp.sum(jnp.where(eye, r, 0.0), axis=1, keepdims=True)

    scale = dh ** -0.5
    for h in range(nh):
        sl = slice(h * dh, (h + 1) * dh)
        q = q_ref[:, sl]
        k = k_ref[:, sl] * scale
        v = v_ref[:, sl]
        qb, kb, vb = q.astype(BF16), k.astype(BF16), v.astype(BF16)
        li_r = li_all[h:h + 1]
        bc_r = bc_all[h:h + 1]
        bc_c = to_col(bc_r)
        m0 = m_s[h][:, 0:1]
        c = c_s[h]
        n = n_s[h]

        dmat = jnp.where(causal, bc_c - bc_r + li_r, -jnp.inf)
        inter = bc_c + m0
        m_row = jnp.maximum(inter, jnp.max(dmat, axis=1, keepdims=True))
        w_intra = jnp.exp(dmat - m_row)
        w_inter = jnp.exp(inter - m_row)
        sc = _dot_nt(qb, kb) * w_intra
        num = _dot(sc.astype(BF16), vb) + w_inter * _dot_nt(qb, c.astype(BF16))
        den = jnp.sum(sc, axis=1, keepdims=True) + w_inter * jnp.sum(q * n, axis=1, keepdims=True)
        hh = num / jnp.maximum(jnp.abs(den), jnp.exp(-m_row))
        h_ref[:, sl] = (hh * jax.nn.sigmoid(o_ref[:, sl])).astype(BF16)

        b_end = bc_r[:, ln - 1:ln]
        g_r = b_end - bc_r + li_r
        m_new = jnp.maximum(b_end + m0, jnp.max(g_r, axis=1, keepdims=True))
        wg_c = jnp.exp(to_col(g_r) - m_new)
        decay = jnp.exp(b_end + m0 - m_new)
        c_s[h] = decay * c + _dot_tn((v * wg_c).astype(BF16), kb)
        n_s[h] = decay * n + jnp.sum(wg_c * k, axis=0, keepdims=True)
        m_s[h] = jnp.broadcast_to(m_new, (1, dh))

    @pl.when(ci == nc - 1)
    def _():
        c_out[...] = c_s[...]
        n_out[...] = n_s[...]
        m_out[...] = m_s[...]


def _mlstm_prompt(z, gates_t, batch, seq, nh, dh):
    ln = MLSTM_CHUNK
    nc = seq // ln
    width = nh * dh
    ng = gates_t.shape[0]
    blk = lambda j: pl.BlockSpec((ln, width), lambda b, c: (b * nc + c, j))
    return pl.pallas_call(
        functools.partial(_mlstm_prompt_kernel, nh=nh, dh=dh),
        grid=(batch, nc),
        in_specs=[blk(1), blk(2), blk(3), blk(4),
                  pl.BlockSpec((ng, ln), lambda b, c: (0, b * nc + c))],
        out_specs=[
            pl.BlockSpec((ln, width), lambda b, c: (b * nc + c, 0)),
            pl.BlockSpec((None, nh, dh, dh), lambda b, c: (b, 0, 0, 0)),
            pl.BlockSpec((None, nh, 1, dh), lambda b, c: (b, 0, 0, 0)),
            pl.BlockSpec((None, nh, 1, dh), lambda b, c: (b, 0, 0, 0)),
        ],
        out_shape=[
            jax.ShapeDtypeStruct((batch * seq, width), BF16),
            jax.ShapeDtypeStruct((batch, nh, dh, dh), F32),
            jax.ShapeDtypeStruct((batch, nh, 1, dh), F32),
            jax.ShapeDtypeStruct((batch, nh, 1, dh), F32),
        ],
        scratch_shapes=[pltpu.VMEM((nh, dh, dh), F32), pltpu.VMEM((nh, 1, dh), F32),
                        pltpu.VMEM((nh, 1, dh), F32)],
        compiler_params=_params("parallel", "arbitrary"),
        name="mlstm_prompt",
    )(z, z, z, z, gates_t)


def _mlstm_step_kernel(q_ref, k_ref, v_ref, o_ref, li_ref, fp_ref, c_ref, n_ref, m_ref,
                       h_ref, c_out, n_out, m_out, *, nh, dh):
    eye = (lax.broadcasted_iota(I32, (1, dh, dh), 1) == lax.broadcasted_iota(I32, (1, dh, dh), 2))
    scale = dh ** -0.5
    for h in range(nh):
        sl = slice(h * dh, (h + 1) * dh)
        q = q_ref[:, :, sl]
        k = k_ref[:, :, sl] * scale
        v = v_ref[:, :, sl]
        c = c_ref[:, h]
        n = n_ref[:, h]
        m = m_ref[:, h]
        li = li_ref[:, h]
        lf = jax.nn.log_sigmoid(fp_ref[:, h])
        inter = lf + m
        m_row = jnp.maximum(inter, li)
        w_intra = jnp.exp(li - m_row)
        w_inter = jnp.exp(inter - m_row)
        sc = jnp.sum(q * k, axis=-1, keepdims=True) * w_intra
        v_c = jnp.sum(jnp.where(eye, v, 0.0), axis=-1, keepdims=True)
        num = sc * v_c + w_inter * jnp.sum(c * q, axis=-1, keepdims=True)
        den = sc + w_inter * jnp.sum(n * q, axis=-1, keepdims=True)
        h_c = num / jnp.maximum(jnp.abs(den), jnp.exp(-m_row))
        h_l = jnp.sum(jnp.where(eye, h_c, 0.0), axis=1, keepdims=True)
        h_ref[:, :, sl] = h_l * jax.nn.sigmoid(o_ref[:, :, sl])
        wg = jnp.exp(li - m_row)
        decay = jnp.exp(inter - m_row)
        c_out[:, h] = decay * c + (v_c * wg) * k
        n_out[:, h] = decay * n + wg * k
        m_out[:, h] = m_row


def _mlstm_step(z3, li, fp, c, n, m, nh, dh):
    batch = c.shape[0]
    bb = SUBLANES
    width = nh * dh
    blk = lambda j: pl.BlockSpec((bb, 1, width), lambda i: (i, 0, j))
    st4 = lambda a, b: pl.BlockSpec((bb, nh, a, b), lambda i: (i, 0, 0, 0))
    return pl.pallas_call(
        functools.partial(_mlstm_step_kernel, nh=nh, dh=dh),
        grid=(batch // bb,),
        in_specs=[blk(1), blk(2), blk(3), blk(4), st4(1, 1), st4(1, 1),
                  st4(dh, dh), st4(1, dh), st4(1, 1)],
        out_specs=[pl.BlockSpec((bb, 1, width), lambda i: (i, 0, 0)),
                   st4(dh, dh), st4(1, dh), st4(1, 1)],
        out_shape=[
            jax.ShapeDtypeStruct((batch, 1, width), F32),
            jax.ShapeDtypeStruct((batch, nh, dh, dh), F32),
            jax.ShapeDtypeStruct((batch, nh, 1, dh), F32),
            jax.ShapeDtypeStruct((batch, nh, 1, 1), F32),
        ],
        compiler_params=_params("parallel"),
        name="mlstm_step",
    )(z3, z3, z3, z3, li, fp, c, n, m)


def _route(hn, wr_ref, br_ref, tril_ref, carry_ref, rinfo_ref):
    ngrp, epg = MOE_GROUPS, MOE_EXPERTS_PER_GROUP
    logits = jnp.dot(hn, wr_ref[...], precision=HIGHEST, preferred_element_type=F32) + br_ref[...]
    tm = logits.shape[0]
    lane = lax.broadcasted_iota(I32, (tm, LANES), 1)
    neg = -jnp.inf

    def first_max(x):
        mx = jnp.max(x, axis=1, keepdims=True)
        idx = jnp.min(jnp.where(x == mx, lane, LANES), axis=1, keepdims=True)
        return mx, idx

    is_grp = lane < ngrp
    gmax, gsel = first_max(jnp.where(is_grp, logits, neg))
    g_w = 1.0 / jnp.sum(jnp.where(is_grp, jnp.exp(logits - gmax), 0.0), axis=1, keepdims=True)
    lo = ngrp + gsel * epg
    el = jnp.where((lane >= lo) & (lane < lo + epg), logits, neg)
    v1, i1 = first_max(el)
    v2, i2 = first_max(jnp.where(lane == i1, neg, el))
    e2 = jnp.exp(v2 - v1)
    w1 = g_w / (1.0 + e2)
    w2 = g_w * e2 / (1.0 + e2)
    eid1 = i1 - ngrp
    eid2 = i2 - ngrp

    hit1 = lane == eid1
    hit2 = lane == eid2
    onehot = jnp.where(hit1 | hit2, 1.0, 0.0)
    carry = carry_ref[...]
    prefix = _dot(tril_ref[...], onehot.astype(BF16)) + carry
    rank1 = jnp.sum(jnp.where(hit1, prefix, 0.0), axis=1, keepdims=True)
    rank2 = jnp.sum(jnp.where(hit2, prefix, 0.0), axis=1, keepdims=True)
    carry_ref[...] = carry + jnp.sum(onehot, axis=0, keepdims=True)

    cols = (eid1.astype(F32), eid2.astype(F32), w1, w2, rank1, rank2)
    info = jnp.zeros((tm, LANES), F32)
    for j, cval in enumerate(cols):
        info = jnp.where(lane == j, cval, info)
    rinfo_ref[...] = info


def _outproj_route_kernel(x_ref, p_ref, m_ref, wo_ref, g_ref, wr_ref, br_ref, tril_ref,
                          x1_ref, hn_ref, rinfo_ref, cnt_ref, carry_ref):
    @pl.when(pl.program_id(0) == 0)
    def _():
        carry_ref[...] = jnp.zeros(carry_ref.shape, F32)

    half = p_ref.shape[1]
    mix = _dot(p_ref[...], wo_ref[0:half, :]) + _dot(m_ref[...], wo_ref[half:2 * half, :])
    x1 = x_ref[...] + mix
    x1_ref[...] = x1
    hn = _rms(x1, g_ref[...])
    hn_ref[...] = hn
    _route(hn, wr_ref, br_ref, tril_ref, carry_ref, rinfo_ref)
    cnt_ref[...] = carry_ref[...]


def _glu_route_kernel(x_ref, y_ref, wglu_ref, g_ref, wr_ref, br_ref, tril_ref,
                      x1_ref, hn_ref, rinfo_ref, cnt_ref, carry_ref):
    @pl.when(pl.program_id(0) == 0)
    def _():
        carry_ref[...] = jnp.zeros(carry_ref.shape, F32)

    d = x_ref.shape[1]
    ag = _dot(jax.nn.gelu(y_ref[...]).astype(BF16), wglu_ref[...])
    x1 = x_ref[...] + ag[:, 0:d] * jax.nn.sigmoid(ag[:, d:2 * d])
    x1_ref[...] = x1
    hn = _rms(x1, g_ref[...])
    hn_ref[...] = hn
    _route(hn, wr_ref, br_ref, tril_ref, carry_ref, rinfo_ref)
    cnt_ref[...] = carry_ref[...]


def _mix_route(kernel, name, x, acts, w, g, wr, br, tril):
    n, d = x.shape
    tm = TOKEN_TILE
    full = lambda i: (0, 0)
    row = lambda a: pl.BlockSpec((tm, a.shape[1]), lambda i: (i, 0))
    return pl.pallas_call(
        kernel,
        grid=(n // tm,),
        in_specs=[row(x)] + [row(a) for a in acts] + [
            pl.BlockSpec(w.shape, full),
            pl.BlockSpec((1, d), full),
            pl.BlockSpec((d, LANES), full),
            pl.BlockSpec((1, LANES), full),
            pl.BlockSpec((tm, tm), full),
        ],
        out_specs=[
            pl.BlockSpec((tm, d), lambda i: (i, 0)),
            pl.BlockSpec((tm, d), lambda i: (i, 0)),
            pl.BlockSpec((tm, LANES), lambda i: (i, 0)),
            pl.BlockSpec((1, LANES), full),
        ],
        out_shape=[
            jax.ShapeDtypeStruct((n, d), F32),
            jax.ShapeDtypeStruct((n, d), F32),
            jax.ShapeDtypeStruct((n, LANES), F32),
            jax.ShapeDtypeStruct((1, LANES), F32),
        ],
        scratch_shapes=[pltpu.VMEM((1, LANES), F32)],
        compiler_params=_params("arbitrary"),
        name=name,
    )(x, *acts, w, g, wr, br, tril)


def _load_indices(pos_hbm, idx_s, sem):
    cp = pltpu.make_async_copy(pos_hbm.at[pl.program_id(0)], idx_s, sem)
    cp.start()
    cp.wait()


def _dispatch_kernel(pos_hbm, hn_ref, xs_hbm, idx_s, sem_i, sem_d):
    tm = hn_ref.shape[0]
    _load_indices(pos_hbm, idx_s, sem_i)

    def issue(r, carry):
        row = hn_ref.at[pl.ds(r, 1)]
        pltpu.make_async_copy(row, xs_hbm.at[pl.ds(idx_s[0, r], 1)], sem_d).start()
        pltpu.make_async_copy(row, xs_hbm.at[pl.ds(idx_s[0, tm + r], 1)], sem_d).start()
        return carry

    lax.fori_loop(0, tm, issue, 0, unroll=DMA_ISSUE_UNROLL)
    whole = pltpu.make_async_copy(hn_ref, xs_hbm.at[pl.ds(0, tm)], sem_d)
    whole.wait()
    whole.wait()


def _dispatch(pos_tiles, hn):
    n, d = hn.shape
    tm = TOKEN_TILE
    return pl.pallas_call(
        _dispatch_kernel,
        grid=(n // tm,),
        in_specs=[pl.BlockSpec(memory_space=pl.ANY),
                  pl.BlockSpec((tm, d), lambda i: (i, 0))],
        out_specs=pl.BlockSpec(memory_space=pl.ANY),
        out_shape=jax.ShapeDtypeStruct((2 * n, d), F32),
        scratch_shapes=[pltpu.SMEM((1, 2 * tm), I32), pltpu.SemaphoreType.DMA,
                        pltpu.SemaphoreType.DMA],
        compiler_params=_params("arbitrary"),
        name="moe_dispatch",
    )(pos_tiles, hn)


def _moe_kernel(vt_ref, ve_ref, von_ref, lo_ref, hi_ref, xs_ref, wg_ref, wu_ref, wd_ref,
                eo_ref, wgb, wub, wdb):
    v = pl.program_id(0)
    tr = xs_ref.shape[0]
    prev = jnp.maximum(v - 1, 0)
    e = ve_ref[v]
    new_expert = jnp.logical_or(v == 0, e != ve_ref[prev])
    first_visit = jnp.logical_or(v == 0, vt_ref[v] != vt_ref[prev])

    @pl.when(von_ref[v] == 1)
    def _():
        @pl.when(new_expert)
        def _():
            wgb[...] = wg_ref[...].astype(BF16)
            wub[...] = wu_ref[...].astype(BF16)
            wdb[...] = wd_ref[...].astype(BF16)

        x = xs_ref[...].astype(BF16)
        act = jax.nn.silu(_dot(x, wgb[...])) * _dot(x, wub[...])
        row = vt_ref[v] * tr + lax.broadcasted_iota(I32, (tr, 1), 0)
        act = jnp.where((row >= lo_ref[e]) & (row < hi_ref[e]), act, 0.0)
        res = _dot(act.astype(BF16), wdb[...])

        @pl.when(first_visit)
        def _():
            eo_ref[...] = res

        @pl.when(jnp.logical_not(first_visit))
        def _():
            eo_ref[...] += res


def _moe_experts(vt, ve, von, lo, hi, xs, wg, wu, wd):
    nv = vt.shape[0]
    rows, d = xs.shape
    hid = wg.shape[2]
    tr = EXPERT_ROW_TILE
    wspec = lambda a, b: pl.BlockSpec((None, a, b), lambda v, vt, ve, von, lo, hi: (ve[v], 0, 0))
    grid_spec = pltpu.PrefetchScalarGridSpec(
        num_scalar_prefetch=5,
        grid=(nv,),
        in_specs=[
            pl.BlockSpec((tr, d), lambda v, vt, ve, von, lo, hi: (vt[v], 0)),
            wspec(d, hid), wspec(d, hid), wspec(hid, d),
        ],
        out_specs=pl.BlockSpec((tr, d), lambda v, vt, ve, von, lo, hi: (vt[v], 0)),
        scratch_shapes=[pltpu.VMEM((d, hid), BF16), pltpu.VMEM((d, hid), BF16),
                        pltpu.VMEM((hid, d), BF16)],
    )
    return pl.pallas_call(
        _moe_kernel,
        grid_spec=grid_spec,
        out_shape=jax.ShapeDtypeStruct((rows, d), F32),
        compiler_params=_params("arbitrary"),
        name="moe_experts",
    )(vt, ve, von, lo, hi, xs, wg, wu, wd)


def _combine_kernel(pos_hbm, eo_hbm, x_ref, rinfo_ref, g_ref, x2_ref, hn_ref, idx_s, a_buf, b_buf,
                    sem_i, sem_a, sem_b):
    tm = a_buf.shape[0]
    _load_indices(pos_hbm, idx_s, sem_i)

    def issue(r, carry):
        pltpu.make_async_copy(eo_hbm.at[pl.ds(idx_s[0, r], 1)], a_buf.at[pl.ds(r, 1)], sem_a).start()
        pltpu.make_async_copy(eo_hbm.at[pl.ds(idx_s[0, tm + r], 1)], b_buf.at[pl.ds(r, 1)], sem_b).start()
        return carry

    lax.fori_loop(0, tm, issue, 0, unroll=DMA_ISSUE_UNROLL)
    pltpu.make_async_copy(eo_hbm.at[pl.ds(0, tm)], a_buf, sem_a).wait()
    pltpu.make_async_copy(eo_hbm.at[pl.ds(0, tm)], b_buf, sem_b).wait()
    info = rinfo_ref[...]
    x2 = x_ref[...] + (info[:, 2:3] * a_buf[...] + info[:, 3:4] * b_buf[...])
    x2_ref[...] = x2
    hn_ref[...] = _rms(x2, g_ref[...])


def _combine(pos_tiles, eo, x, rinfo, g):
    n, d = x.shape
    tm = TOKEN_TILE
    return pl.pallas_call(
        _combine_kernel,
        grid=(n // tm,),
        in_specs=[
            pl.BlockSpec(memory_space=pl.ANY),
            pl.BlockSpec(memory_space=pl.ANY),
            pl.BlockSpec((tm, d), lambda i: (i, 0)),
            pl.BlockSpec((tm, LANES), lambda i: (i, 0)),
            pl.BlockSpec((1, d), lambda i: (0, 0)),
        ],
        out_specs=[pl.BlockSpec((tm, d), lambda i: (i, 0)),
                   pl.BlockSpec((tm, d), lambda i: (i, 0))],
        out_shape=[jax.ShapeDtypeStruct((n, d), F32), jax.ShapeDtypeStruct((n, d), F32)],
        scratch_shapes=[
            pltpu.SMEM((1, 2 * tm), I32),
            pltpu.VMEM((tm, d), F32),
            pltpu.VMEM((tm, d), F32),
            pltpu.SemaphoreType.DMA,
            pltpu.SemaphoreType.DMA,
            pltpu.SemaphoreType.DMA,
        ],
        compiler_params=_params("arbitrary"),
        name="moe_combine",
    )(pos_tiles, eo, x, rinfo, g)


def _moe(x1, hn, rinfo, counts, wg, wu, wd, next_gain):
    n = x1.shape[0]
    ne = wg.shape[0]
    tr = EXPERT_ROW_TILE
    tm = TOKEN_TILE
    nv = (2 * n) // tr + ne - 1
    eid = rinfo[:, 0:2].astype(I32)
    rank = rinfo[:, 4:6].astype(I32)
    cnt = counts[0, :ne].astype(I32)
    seg_end = jnp.cumsum(cnt)
    seg_start = seg_end - cnt
    experts = jnp.arange(ne, dtype=I32)
    pos = jnp.sum(jnp.where(eid[:, :, None] == experts, seg_start, 0), axis=-1) + rank
    pos_tiles = pos.reshape(n // tm, tm, 2).transpose(0, 2, 1).reshape(n // tm, 1, 2 * tm)

    first_tile = seg_start // tr
    tiles_e = jnp.where(cnt > 0, (seg_end - 1) // tr - first_tile + 1, 0)
    v_end = jnp.cumsum(tiles_e)
    v_start = v_end - tiles_e
    total = v_end[-1]
    vis = jnp.arange(nv, dtype=I32)
    vc = jnp.minimum(vis, total - 1)
    ve = jnp.sum((vc[:, None] >= v_end[None, :]).astype(I32), axis=1)
    pick = lambda tab: jnp.sum(jnp.where(ve[:, None] == experts, tab, 0), axis=1)
    vt = pick(first_tile) + (vc - pick(v_start))
    von = (vis < total).astype(I32)

    xs = _dispatch(pos_tiles, hn)
    eo = _moe_experts(vt, ve, von, seg_start, seg_end, xs, wg, wu, wd)
    return _combine(pos_tiles, eo, x1, rinfo, next_gain)


def _cis(log_mag, ang):
    mag = jnp.exp(log_mag)
    return mag * jnp.cos(ang), mag * jnp.sin(ang)


def _dot_hi(a, b):
    return jnp.dot(a, b, precision=HIGHEST, preferred_element_type=F32)


def _s5_prep_kernel(lsc_ref, lsr_ref, lrc_ref, lic_ref, lrr_ref, lir_ref, ctre_ref, ctim_ref,
                    btre_ref, btim_ref,
                    t_ref, wre_ref, wim_ref, cyre_ref, cyim_ref, apr_ref, api_ref,
                    bbr_ref, bbi_ref, lbr_ref, lbi_ref, *, gw, gpb, sub):
    blk = gpb * gw
    wide = sub * blk
    nsl = lrc_ref.shape[1]
    ns = nsl // gpb
    sh_blk, sh_gw, sh_ns = blk.bit_length() - 1, gw.bit_length() - 1, ns.bit_length() - 1
    dt_c, dt_r = jnp.exp(lsc_ref[0]), jnp.exp(lsr_ref[0])
    ldt_c_re, ldt_c_im = lrc_ref[0] * dt_c, lic_ref[0] * dt_c
    ldt_r_re, ldt_r_im = lrr_ref[0] * dt_r, lir_ref[0] * dt_r

    lane_w = lax.broadcasted_iota(I32, (1, wide), 1)
    spread = jnp.where((lax.broadcasted_iota(I32, (gw, wide), 1) & (gw - 1))
                       == lax.broadcasted_iota(I32, (gw, wide), 0), 1.0, 0.0)
    cre = _dot_hi(ctre_ref[0], spread)
    cim = _dot_hi(ctim_ref[0], spread)
    same = (jnp.right_shift(lax.broadcasted_iota(I32, (nsl, 1), 0), sh_ns)
            == (jnp.right_shift(lane_w, sh_gw) & (gpb - 1)))
    jl = jnp.right_shift(lane_w, sh_blk).astype(F32)

    def c_lam_pow(tau):
        pr, pi = _cis(tau * ldt_c_re, tau * ldt_c_im)
        return (jnp.where(same, pr * cre - pi * cim, 0.0),
                jnp.where(same, -(pr * cim + pi * cre), 0.0))

    clr0, cli0 = c_lam_pow(jl)
    clr1, cli1 = c_lam_pow(jl + 1.0)
    cyre_ref[0] = clr1.astype(BF16)
    cyim_ref[0] = cli1.astype(BF16)

    lbr, lbi = _cis(ldt_r_re, ldt_r_im)
    lbr_ref[0] = lbr
    lbi_ref[0] = lbi
    lr, li = lrr_ref[0], lir_ref[0]
    nr, ni = lbr - 1.0, lbi
    den = lr * lr + li * li
    fr = (nr * lr + ni * li) / den
    fi = (ni * lr - nr * li) / den
    bre, bim = btre_ref[0], btim_ref[0]
    bbr = fr * bre - fi * bim
    bbi = fr * bim + fi * bre
    bbr_ref[0] = bbr
    bbi_ref[0] = bbi

    lane_t = lax.broadcasted_iota(I32, (gw, wide), 1)
    for g in range(gpb):
        ps = slice(g * ns, (g + 1) * ns)
        r = _dot_hi(bbr[:, ps], clr0[ps, :]) + _dot_hi(bbi[:, ps], cli0[ps, :])
        for j in range(sub):
            tb = r if j == 0 else jnp.where(lane_t >= blk * j, pltpu.roll(r, blk * j, 1), 0.0)
            r0 = j * blk + g * gw
            t_ref[0, r0:r0 + gw, :] = tb.astype(BF16)

    rows = lax.broadcasted_iota(I32, (wide, 1), 0)
    spread_t = jnp.where((lax.broadcasted_iota(I32, (wide, gw), 0) & (gw - 1))
                         == lax.broadcasted_iota(I32, (wide, gw), 1), 1.0, 0.0)
    bbr_t = _dot_hi(spread_t, bbr)
    bbi_t = _dot_hi(spread_t, bbi)
    rj = ((sub - 1) - jnp.right_shift(rows, sh_blk)).astype(F32)
    same_w = ((jnp.right_shift(rows, sh_gw) & (gpb - 1))
              == jnp.right_shift(lax.broadcasted_iota(I32, (1, nsl), 1), sh_ns))
    pr, pi = _cis(rj * ldt_r_re, rj * ldt_r_im)
    wre_ref[0] = jnp.where(same_w, pr * bbr_t - pi * bbi_t, 0.0).astype(BF16)
    wim_ref[0] = jnp.where(same_w, pr * bbi_t + pi * bbr_t, 0.0).astype(BF16)

    nlev = apr_ref.shape[1]
    pw = (sub * jnp.left_shift(1, lax.broadcasted_iota(I32, (nlev, 1), 0))).astype(F32)
    apr, api = _cis(pw * ldt_r_re, pw * ldt_r_im)
    apr_ref[0] = apr
    api_ref[0] = api


def _s5_prep(log_step, lam_re, lam_im, b_re, b_im, c_re, c_im):
    ng, ns = lam_re.shape
    gw = b_re.shape[2]
    sub, gpb = S5_SUB, S5_GROUPS_PER_BLOCK
    nblk = ng // gpb
    nsl = gpb * ns
    wide = sub * gpb * gw
    ls = jnp.repeat(log_step, ns)
    ct = lambda c: jnp.transpose(c, (0, 2, 1)).reshape(nblk, nsl, gw)
    bt = lambda b: jnp.transpose(b.reshape(nblk, gpb, ns, gw), (0, 3, 1, 2)).reshape(nblk, gw, nsl)
    ins = [ls.reshape(nblk, nsl, 1), ls.reshape(nblk, 1, nsl),
           lam_re.reshape(nblk, nsl, 1), lam_im.reshape(nblk, nsl, 1),
           lam_re.reshape(nblk, 1, nsl), lam_im.reshape(nblk, 1, nsl),
           ct(c_re), ct(c_im), bt(b_re), bt(b_im)]
    spec = lambda a: pl.BlockSpec((1,) + a.shape[1:], lambda g: (g, 0, 0))
    outs = [
        jax.ShapeDtypeStruct((nblk, wide, wide), BF16),
        jax.ShapeDtypeStruct((nblk, wide, nsl), BF16),
        jax.ShapeDtypeStruct((nblk, wide, nsl), BF16),
        jax.ShapeDtypeStruct((nblk, nsl, wide), BF16),
        jax.ShapeDtypeStruct((nblk, nsl, wide), BF16),
        jax.ShapeDtypeStruct((nblk, SUBLANES, nsl), F32),
        jax.ShapeDtypeStruct((nblk, SUBLANES, nsl), F32),
        jax.ShapeDtypeStruct((nblk, gw, nsl), F32),
        jax.ShapeDtypeStruct((nblk, gw, nsl), F32),
        jax.ShapeDtypeStruct((nblk, 1, nsl), F32),
        jax.ShapeDtypeStruct((nblk, 1, nsl), F32),
    ]
    return pl.pallas_call(
        functools.partial(_s5_prep_kernel, gw=gw, gpb=gpb, sub=sub),
        grid=(nblk,),
        in_specs=[spec(a) for a in ins],
        out_specs=[spec(o) for o in outs],
        out_shape=outs,
        compiler_params=_params("parallel"),
        name="s5_prep",
    )(*ins)


def _s5_prompt_kernel(x_ref, t_ref, wre_ref, wim_ref, cyre_ref, cyim_ref, apr_ref, api_ref, d_ref,
                      y_ref, hre_out, him_out, u_s, y4_s, sre_s, sim_s, *, nb, seq, sub):
    nk = seq // sub
    rows = nb * nk
    hl = LANES // 2
    pad = nk // 2
    nlev = nk.bit_length() - 1
    nsl = sre_s.shape[1]
    low = lax.broadcasted_iota(I32, (nk, LANES), 1) < hl
    kidx = lax.broadcasted_iota(I32, (rows, 1), 0) & (nk - 1)
    sre_s[0:pad, :] = jnp.zeros((pad, nsl), F32)
    sim_s[0:pad, :] = jnp.zeros((pad, nsl), F32)

    def step_rows(b, j):
        return (pl.ds(b * seq + j, nk, stride=sub), slice(None))

    for b in range(nb):
        rs = slice(b * nk, (b + 1) * nk)
        for m in range(sub // 2):
            ls = slice(m * LANES, (m + 1) * LANES)
            s0 = x_ref[step_rows(b, 2 * m)]
            s1 = x_ref[step_rows(b, 2 * m + 1)]
            u_s[0, rs, ls] = jnp.where(low, s0, pltpu.roll(s1, hl, 1)).astype(BF16)
            u_s[1, rs, ls] = jnp.where(low, pltpu.roll(s0, hl, 1), s1).astype(BF16)

    live = slice(pad, pad + rows)
    for half in range(2):
        u = u_s[half]
        hre = _dot(u, wre_ref[half])
        him = _dot(u, wim_ref[half])
        for i in range(nlev):
            s = 1 << i
            sre_s[live, :] = hre
            sim_s[live, :] = him
            pre = sre_s[pad - s:pad - s + rows, :]
            pim = sim_s[pad - s:pad - s + rows, :]
            ar = apr_ref[half, i:i + 1, :]
            ai = api_ref[half, i:i + 1, :]
            ok = kidx >= s
            hre = hre + jnp.where(ok, ar * pre - ai * pim, 0.0)
            him = him + jnp.where(ok, ar * pim + ai * pre, 0.0)
        sre_s[live, :] = hre
        sim_s[live, :] = him
        ls = slice(half * nsl, (half + 1) * nsl)
        for b in range(nb):
            last = pad + (b + 1) * nk - 1
            hre_out[b:b + 1, ls] = sre_s[last:last + 1, :]
            him_out[b:b + 1, ls] = sim_s[last:last + 1, :]
        started = kidx >= 1
        hpre = jnp.where(started, sre_s[pad - 1:pad - 1 + rows, :], 0.0).astype(BF16)
        hpim = jnp.where(started, sim_s[pad - 1:pad - 1 + rows, :], 0.0).astype(BF16)
        y4_s[half] = _dot(u, t_ref[half]) + _dot(hpre, cyre_ref[half]) + _dot(hpim, cyim_ref[half])

    d = d_ref[...]
    for b in range(nb):
        rs = slice(b * nk, (b + 1) * nk)
        for m in range(sub // 2):
            ls = slice(m * LANES, (m + 1) * LANES)
            ca = y4_s[0, rs, ls]
            cb = y4_s[1, rs, ls]
            y0 = jnp.where(low, ca, pltpu.roll(cb, hl, 1))
            y1 = jnp.where(low, pltpu.roll(ca, hl, 1), cb)
            y_ref[step_rows(b, 2 * m)] = y0 + d * x_ref[step_rows(b, 2 * m)]
            y_ref[step_rows(b, 2 * m + 1)] = y1 + d * x_ref[step_rows(b, 2 * m + 1)]


def _s5_prompt(h, prep, d_row, batch, seq):
    d = h.shape[1]
    tmat, wre, wim, cyre, cyim, apr, api = prep[:7]
    nblk, wide, nsl = wre.shape
    sub, nb = S5_SUB, S5_BATCH_PER_STEP
    nk = seq // sub
    assert nk & (nk - 1) == 0 and nk.bit_length() - 1 <= apr.shape[1]
    ntile = d // LANES
    assert nblk == 2 * ntile
    rows = nb * nk
    wspec = lambda a: pl.BlockSpec((2,) + a.shape[1:], lambda t, b: (t, 0, 0))
    xspec = pl.BlockSpec((nb * seq, LANES), lambda t, b: (b, t))
    sspec = pl.BlockSpec((None, nb, 2 * nsl), lambda t, b: (b, 0, t))
    return pl.pallas_call(
        functools.partial(_s5_prompt_kernel, nb=nb, seq=seq, sub=sub),
        grid=(ntile, batch // nb),
        in_specs=[xspec, wspec(tmat), wspec(wre), wspec(wim), wspec(cyre), wspec(cyim),
                  wspec(apr), wspec(api), pl.BlockSpec((1, LANES), lambda t, b: (0, t))],
        out_specs=[xspec, sspec, sspec],
        out_shape=[
            jax.ShapeDtypeStruct((batch * seq, d), F32),
            jax.ShapeDtypeStruct((batch // nb, nb, ntile * 2 * nsl), F32),
            jax.ShapeDtypeStruct((batch // nb, nb, ntile * 2 * nsl), F32),
        ],
        scratch_shapes=[
            pltpu.VMEM((2, rows, wide), BF16),
            pltpu.VMEM((2, rows, wide), F32),
            pltpu.VMEM((nk // 2 + rows, nsl), F32),
            pltpu.VMEM((nk // 2 + rows, nsl), F32),
        ],
        compiler_params=_params("parallel", "parallel"),
        name="s5_prompt",
    )(h, tmat, wre, wim, cyre, cyim, apr, api, d_row)


def _s5_step_kernel(u_ref, h_ref, hsw_ref, bb_ref, la_ref, lb_ref, cc_ref, d_ref, hn_ref, y_ref):
    for g in range(u_ref.shape[0]):
        u = u_ref[g]
        hn = (la_ref[g] * h_ref[g] + lb_ref[g] * hsw_ref[g]
              + jnp.dot(u, bb_ref[g], precision=HIGHEST, preferred_element_type=F32))
        hn_ref[g] = hn
        y_ref[g] = (jnp.dot(hn, cc_ref[g], precision=HIGHEST, preferred_element_type=F32)
                    + d_ref[g] * u)


def _s5_step(u_p, h_cat, h_swp, bb_cat, la, lb, cc, d_p):
    ng = u_p.shape[0]
    gb = SUBLANES
    spec = lambda a: pl.BlockSpec((gb,) + a.shape[1:], lambda g: (g, 0, 0))
    ins = [u_p, h_cat, h_swp, bb_cat, la, lb, cc, d_p]
    outs = [jax.ShapeDtypeStruct(h_cat.shape, F32), jax.ShapeDtypeStruct(u_p.shape, F32)]
    return pl.pallas_call(
        _s5_step_kernel,
        grid=(ng // gb,),
        in_specs=[spec(a) for a in ins],
        out_specs=[spec(o) for o in outs],
        out_shape=outs,
        compiler_params=_params("parallel"),
        name="s5_step",
    )(*ins)


def _pad_to(a, axis, size):
    pad = [(0, 0)] * a.ndim
    pad[axis] = (0, size - a.shape[axis])
    return jnp.pad(a, pad)


def kernel(x_prompt, x_sample, state_pool, state_mlstm_c, state_mlstm_n, state_mlstm_m, state_s5_re, state_s5_im, norm_mix, norm_ffn, norm_final, w_in_ab, b_gates, pool_w, pool_scale, w_out_ab, s5_lam_re, s5_lam_im, s5_log_step, s5_b_re, s5_b_im, s5_c_re, s5_c_im, s5_d, w_glu, moe_w_group, moe_b_group, moe_w_expert, moe_b_expert, moe_w_gate, moe_w_up, moe_w_down):
    bp, tp, d = x_prompt.shape
    bs = x_sample.shape[0]
    n_p = bp * tp
    tm = TOKEN_TILE
    assert n_p % tm == 0 and bs <= tm
    n_fill = tm - bs
    nh, dh = state_mlstm_c.shape[2], state_mlstm_c.shape[3]
    pool_width = state_pool.shape[3]
    ml_width = nh * dh
    n_main = pool_width + 4 * ml_width
    n_gates = 2 * nh
    ngrp_s5, n_state = s5_lam_re.shape[1], s5_lam_re.shape[2]
    gw_s5 = d // ngrp_s5

    def stack(prompt_rows, sample_rows):
        fill = jnp.zeros((n_fill, prompt_rows.shape[1]), prompt_rows.dtype)
        return jnp.concatenate([prompt_rows, sample_rows.astype(prompt_rows.dtype), fill], axis=0)

    x0 = stack(x_prompt.reshape(n_p, d), x_sample.reshape(bs, d))
    tril = jnp.tril(jnp.ones((tm, tm), BF16), -1)

    def router_weights(l):
        wr = jnp.concatenate([moe_w_group[l], moe_w_expert[l]], axis=1)
        br = jnp.concatenate([moe_b_group[l], moe_b_expert[l]])[None, :]
        return _pad_to(wr, 1, LANES), _pad_to(br, 1, LANES)

    w_in = w_in_ab[0]
    w_g = w_in[:, n_main:]
    z, gates, gates_t = _inproj(
        x0, norm_mix[0][None, :], w_in[:, :n_main].astype(BF16),
        _pad_to(w_g, 1, LANES).astype(BF16), w_g.T.astype(BF16),
        _pad_to(b_gates[0][None, :], 1, LANES), b_gates[0][:, None])

    pw = pool_w[0].astype(BF16)
    ps = pool_scale[0][None, :]
    pool_y_p, pool_p = _pool_prompt(z, pw, ps, bp, tp)
    pool_y_s, pool_s_t = _pool_step(z, jnp.transpose(state_pool[0], (1, 0, 2)), pw, ps, n_p)
    pool_s = jnp.transpose(pool_s_t, (1, 0, 2))

    ml_y_p, c_p, n_p_st, m_p = _mlstm_prompt(z, gates_t, bp, tp, nh, dh)
    g_s = gates[n_p:n_p + bs, :n_gates]
    ml_y_s, c_s, n_s_st, m_s = _mlstm_step(
        z[n_p:n_p + bs].reshape(bs, 1, n_main),
        g_s[:, :nh].reshape(bs, nh, 1, 1), g_s[:, nh:].reshape(bs, nh, 1, 1),
        state_mlstm_c[0], state_mlstm_n[0].reshape(bs, nh, 1, dh),
        state_mlstm_m[0].reshape(bs, nh, 1, 1), nh, dh)

    pool_y = stack(pool_y_p, pool_y_s)
    ml_y = stack(ml_y_p, ml_y_s.reshape(bs, ml_width))
    wr, br = router_weights(0)
    x1, hn, rinfo, counts = _mix_route(
        _outproj_route_kernel, "outproj_route", x0, [pool_y, ml_y], w_out_ab[0].astype(BF16),
        norm_ffn[0][None, :], wr, br, tril)
    x2, h1 = _moe(x1, hn, rinfo, counts, moe_w_gate[0], moe_w_up[0], moe_w_down[0],
                  norm_mix[1][None, :])

    prep = _s5_prep(s5_log_step[0], s5_lam_re[0], s5_lam_im[0], s5_b_re[0], s5_b_im[0],
                    s5_c_re[0], s5_c_im[0])
    y_p, hre_p, him_p = _s5_prompt(h1, prep, s5_d[0][None, :], bp, tp)
    s5_re_p = hre_p.reshape(bp, ngrp_s5, n_state)
    s5_im_p = him_p.reshape(bp, ngrp_s5, n_state)

    gpb = S5_GROUPS_PER_BLOCK
    per_group = lambda a: jnp.transpose(
        a.reshape(ngrp_s5 // gpb, gw_s5, gpb, n_state), (0, 2, 1, 3)).reshape(ngrp_s5, gw_s5, n_state)
    bbr, bbi = per_group(prep[7]), per_group(prep[8])
    lbr, lbi = prep[9].reshape(ngrp_s5, 1, n_state), prep[10].reshape(ngrp_s5, 1, n_state)
    d_g = s5_d[0].reshape(ngrp_s5, 1, gw_s5)
    u_s = h1[n_p:n_p + bs].reshape(bs, ngrp_s5, gw_s5).transpose(1, 0, 2)
    h_re = jnp.transpose(state_s5_re[0], (1, 0, 2))
    h_im = jnp.transpose(state_s5_im[0], (1, 0, 2))
    cc = jnp.concatenate([jnp.transpose(s5_c_re[0], (0, 2, 1)),
                          -jnp.transpose(s5_c_im[0], (0, 2, 1))], axis=1)
    hn_s, y_s = _s5_step(
        _pad_to(u_s, 2, LANES),
        jnp.concatenate([h_re, h_im], axis=2), jnp.concatenate([h_im, h_re], axis=2),
        _pad_to(jnp.concatenate([bbr, bbi], axis=2), 1, LANES),
        jnp.concatenate([lbr, lbr], axis=2), jnp.concatenate([-lbi, lbi], axis=2),
        _pad_to(cc, 2, LANES), _pad_to(d_g, 2, LANES))
    s5_re_s = jnp.transpose(hn_s[:, :, :n_state], (1, 0, 2))
    s5_im_s = jnp.transpose(hn_s[:, :, n_state:], (1, 0, 2))
    y_all = stack(y_p, jnp.transpose(y_s[:, :, :gw_s5], (1, 0, 2)).reshape(bs, d))

    wr, br = router_weights(1)
    x3, hn, rinfo, counts = _mix_route(
        _glu_route_kernel, "glu_route", x2, [y_all], w_glu[0].astype(BF16),
        norm_ffn[1][None, :], wr, br, tril)
    _, y_out = _moe(x3, hn, rinfo, counts, moe_w_gate[1], moe_w_up[1], moe_w_down[1],
                    norm_final[None, :])

    return (y_out[:n_p].reshape(bp, tp, d), y_out[n_p:n_p + bs].reshape(bs, 1, d),
            pool_p[None], c_p[None], n_p_st.reshape(1, bp, nh, dh), m_p[:, :, 0, 0][None],
            s5_re_p[None], s5_im_p[None],
            pool_s[None], c_s[None], n_s_st.reshape(1, bs, nh, dh), m_s.reshape(1, bs, nh),
            s5_re_s[None], s5_im_s[None])
```

```python
import functools

import jax
import jax.numpy as jnp
from jax import lax
from jax.experimental import pallas as pl
from jax.experimental.pallas import tpu as pltpu

F32 = jnp.float32
BF16 = jnp.bfloat16
I32 = jnp.int32

PAST_LEN = 16384
POOL_WINDOWS = (2, 4, 8, 16)
POOL_BUF = max(POOL_WINDOWS) - 1
MLSTM_CHUNK = 128
S5_SUB = 16
MOE_GROUPS = 4
MOE_EXPERTS_PER_GROUP = 8
RMS_EPS = 1e-6

LANES = 128
SUBLANES = 8
VMEM_LIMIT_BYTES = 56 * 1024 * 1024

TOKEN_TILE = 512
EXPERT_ROW_TILE = 256
POOL_TIME_TILE = 512
DMA_ISSUE_UNROLL = 8
S5_GROUPS_PER_BLOCK = 4
S5_BATCH_PER_STEP = 4

HIGHEST = lax.Precision.HIGHEST


def _params(*sem):
    return pltpu.CompilerParams(dimension_semantics=sem, vmem_limit_bytes=VMEM_LIMIT_BYTES)


def _rms(x, g):
    return x * lax.rsqrt(jnp.mean(x * x, axis=-1, keepdims=True) + RMS_EPS) * g


def _dot(a, b):
    return jnp.dot(a, b, preferred_element_type=F32)


def _dot_nt(a, b):
    return lax.dot_general(a, b, (((1,), (1,)), ((), ())), preferred_element_type=F32)


def _dot_tn(a, b):
    return lax.dot_general(a, b, (((0,), (0,)), ((), ())), preferred_element_type=F32)


def _stacked(main_ref, tail_ref):
    last = pl.program_id(0) == pl.num_programs(0) - 1
    return jnp.where(last, tail_ref[...], main_ref[...])


def _stacked_specs(main, tail):
    tm, w = tail.shape
    last_main = main.shape[0] // tm - 1
    return [pl.BlockSpec((tm, w), lambda i: (jnp.minimum(i, last_main), 0)),
            pl.BlockSpec((tm, w), lambda i: (0, 0))]


def _inproj_kernel(xm_ref, xt_ref, g_ref, w_ref, wg_ref, wgt_ref, bg_ref, bgt_ref,
                   z_ref, gates_ref, gatest_ref):
    h = _rms(_stacked(xm_ref, xt_ref), g_ref[...]).astype(BF16)
    z_ref[...] = _dot(h, w_ref[...])
    gates_ref[...] = _dot(h, wg_ref[...]) + bg_ref[...]
    gatest_ref[...] = _dot_nt(wgt_ref[...], h) + bgt_ref[...]


def _inproj(x_main, x_tail, g, w, wg, wgt, bg, bgt):
    d = x_main.shape[1]
    tm = TOKEN_TILE
    n = x_main.shape[0] + tm
    nz = w.shape[1]
    ng = wgt.shape[0]
    full = lambda i: (0, 0)
    return pl.pallas_call(
        _inproj_kernel,
        grid=(n // tm,),
        in_specs=_stacked_specs(x_main, x_tail) + [
            pl.BlockSpec((1, d), full),
            pl.BlockSpec((d, nz), full),
            pl.BlockSpec((d, LANES), full),
            pl.BlockSpec((ng, d), full),
            pl.BlockSpec((1, LANES), full),
            pl.BlockSpec((ng, 1), full),
        ],
        out_specs=[
            pl.BlockSpec((tm, nz), lambda i: (i, 0)),
            pl.BlockSpec((tm, LANES), lambda i: (i, 0)),
            pl.BlockSpec((ng, tm), lambda i: (0, i)),
        ],
        out_shape=[
            jax.ShapeDtypeStruct((n, nz), F32),
            jax.ShapeDtypeStruct((n, LANES), F32),
            jax.ShapeDtypeStruct((ng, n), F32),
        ],
        compiler_params=_params("parallel"),
        name="inproj",
    )(x_main, x_tail, g, w, wg, wgt, bg, bgt)


def _pool_prompt_kernel(u_ref, pw_ref, ps_ref, y_ref, st_ref, ext_ref, *, tt, gw):
    t = pl.program_id(1)
    nt = pl.num_programs(1)
    halo = POOL_BUF + 1
    width = ext_ref.shape[1]

    @pl.when(t == 0)
    def _():
        ext_ref[0:halo, :] = jnp.zeros((halo, width), F32)

    u = u_ref[...]
    ext_ref[halo:halo + tt, :] = u
    pos = t * tt + lax.broadcasted_iota(I32, (tt, 1), 0)
    for g, w in enumerate(POOL_WINDOWS):
        c0 = g * gw
        acc = u[:, c0:c0 + gw]
        for j in range(1, w):
            acc = acc + ext_ref[halo - j:halo - j + tt, c0:c0 + gw]
        cnt = jnp.minimum(w, pos + 1).astype(F32)
        d = acc / cnt - u[:, c0:c0 + gw]
        y = _dot(d.astype(BF16), pw_ref[g]) * ps_ref[:, c0:c0 + gw]
        y_ref[:, c0:c0 + gw] = y.astype(BF16)

    @pl.when(t == nt - 1)
    def _():
        st_ref[...] = ext_ref[tt + 1:tt + halo, :]

    ext_ref[0:halo, :] = ext_ref[tt:tt + halo, :]


def _pool_prompt(z, pw, ps, batch, seq):
    width = ps.shape[1]
    gw = width // len(POOL_WINDOWS)
    tt = POOL_TIME_TILE
    nt = seq // tt
    return pl.pallas_call(
        functools.partial(_pool_prompt_kernel, tt=tt, gw=gw),
        grid=(batch, nt),
        in_specs=[
            pl.BlockSpec((tt, width), lambda b, t: (b * nt + t, 0)),
            pl.BlockSpec(pw.shape, lambda b, t: (0, 0, 0)),
            pl.BlockSpec((1, width), lambda b, t: (0, 0)),
        ],
        out_specs=[
            pl.BlockSpec((tt, width), lambda b, t: (b * nt + t, 0)),
            pl.BlockSpec((None, POOL_BUF, width), lambda b, t: (b, 0, 0)),
        ],
        out_shape=[
            jax.ShapeDtypeStruct((batch * seq, width), BF16),
            jax.ShapeDtypeStruct((batch, POOL_BUF, width), F32),
        ],
        scratch_shapes=[pltpu.VMEM((POOL_BUF + 1 + tt, width), F32)],
        compiler_params=_params("parallel", "arbitrary"),
        name="pool_prompt",
    )(z, pw, ps)


def _pool_step_kernel(u_ref, buf_ref, pw_ref, ps_ref, y_ref, nb_ref, *, gw):
    u = u_ref[...]
    for g, w in enumerate(POOL_WINDOWS):
        c0 = g * gw
        acc = u[:, c0:c0 + gw]
        for j in range(1, w):
            acc = acc + buf_ref[POOL_BUF - j, :, c0:c0 + gw]
        cnt = float(min(w, PAST_LEN + 1))
        d = acc / cnt - u[:, c0:c0 + gw]
        y = _dot(d.astype(BF16), pw_ref[g]) * ps_ref[:, c0:c0 + gw]
        y_ref[:, c0:c0 + gw] = y.astype(BF16)
    nb_ref[0:POOL_BUF - 1] = buf_ref[1:POOL_BUF]
    nb_ref[POOL_BUF - 1] = u


def _pool_step(z, buf_t, pw, ps, row0):
    _, batch, width = buf_t.shape
    gw = width // len(POOL_WINDOWS)
    return pl.pallas_call(
        functools.partial(_pool_step_kernel, gw=gw),
        grid=(1,),
        in_specs=[
            pl.BlockSpec((batch, width), lambda i: (row0 // batch, 0)),
            pl.BlockSpec(buf_t.shape, lambda i: (0, 0, 0)),
            pl.BlockSpec(pw.shape, lambda i: (0, 0, 0)),
            pl.BlockSpec((1, width), lambda i: (0, 0)),
        ],
        out_specs=[
            pl.BlockSpec((batch, width), lambda i: (0, 0)),
            pl.BlockSpec(buf_t.shape, lambda i: (0, 0, 0)),
        ],
        out_shape=[
            jax.ShapeDtypeStruct((batch, width), BF16),
            jax.ShapeDtypeStruct(buf_t.shape, F32),
        ],
        compiler_params=_params("arbitrary"),
        name="pool_step",
    )(z, buf_t, pw, ps)


def _mlstm_prompt_kernel(q_ref, k_ref, v_ref, o_ref, gt_ref,
                         h_ref, c_out, n_out, m_out, c_s, n_s, m_s, *, nh, dh):
    ci = pl.program_id(1)
    nc = pl.num_programs(1)
    ln = q_ref.shape[0]

    @pl.when(ci == 0)
    def _():
        c_s[...] = jnp.zeros(c_s.shape, F32)
        n_s[...] = jnp.zeros(n_s.shape, F32)
        m_s[...] = jnp.zeros(m_s.shape, F32)

    row = lax.broadcasted_iota(I32, (ln, ln), 0)
    col = lax.broadcasted_iota(I32, (ln, ln), 1)
    causal = col <= row
    eye = col == row

    gt = gt_ref[...]
    li_all = gt[0:nh]
    bc_all = jax.nn.log_sigmoid(gt[nh:2 * nh])
    lane = lax.broadcasted_iota(I32, (nh, ln), 1)
    s = 1
    while s < ln:
        bc_all = bc_all + jnp.where(lane >= s, pltpu.roll(bc_all, s, 1), 0.0)
        s *= 2

    def to_col(r):
        return jnp.sum(jnp.where(eye, r, 0.0), axis=1, keepdims=True)

    scale = dh ** -0.5
    c_old = [c_s[h] for h in range(nh)]
    n_old = [n_s[h] for h in range(nh)]
    m_old = [m_s[h][:, 0:1] for h in range(nh)]
    new_state = []
    for h in range(nh):
        sl = slice(h * dh, (h + 1) * dh)
        q = q_ref[:, sl]
        k = k_ref[:, sl] * scale
        v = v_ref[:, sl]
        qb, kb, vb = q.astype(BF16), k.astype(BF16), v.astype(BF16)
        li_r = li_all[h:h + 1]
        bc_r = bc_all[h:h + 1]
        bc_c = to_col(bc_r)
        m0, c, n = m_old[h], c_old[h], n_old[h]

        dmat = jnp.where(causal, bc_c - bc_r + li_r, -jnp.inf)
        inter = bc_c + m0
        m_row = jnp.maximum(inter, jnp.max(dmat, axis=1, keepdims=True))
        w_intra = jnp.exp(dmat - m_row)
        w_inter = jnp.exp(inter - m_row)
        sc = _dot_nt(qb, kb) * w_intra
        num = _dot(sc.astype(BF16), vb) + w_inter * _dot_nt(qb, c.astype(BF16))
        den = jnp.sum(sc, axis=1, keepdims=True) + w_inter * jnp.sum(q * n, axis=1, keepdims=True)
        hh = num / jnp.maximum(jnp.abs(den), jnp.exp(-m_row))
        h_ref[:, sl] = (hh * jax.nn.sigmoid(o_ref[:, sl])).astype(BF16)

        b_end = bc_r[:, ln - 1:ln]
        g_r = b_end - bc_r + li_r
        m_new = jnp.maximum(b_end + m0, jnp.max(g_r, axis=1, keepdims=True))
        wg_c = jnp.exp(to_col(g_r) - m_new)
        decay = jnp.exp(b_end + m0 - m_new)
        new_state.append((decay * c + _dot_tn((v * wg_c).astype(BF16), kb),
                          decay * n + jnp.sum(wg_c * k, axis=0, keepdims=True),
                          jnp.broadcast_to(m_new, (1, dh))))

    for h, (c_new, n_new, m_new) in enumerate(new_state):
        c_s[h] = c_new
        n_s[h] = n_new
        m_s[h] = m_new

    @pl.when(ci == nc - 1)
    def _():
        c_out[...] = c_s[...]
        n_out[...] = n_s[...]
        m_out[...] = m_s[...]


def _mlstm_prompt(z, gates_t, batch, seq, nh, dh):
    ln = MLSTM_CHUNK
    nc = seq // ln
    width = nh * dh
    ng = gates_t.shape[0]
    blk = lambda j: pl.BlockSpec((ln, width), lambda b, c: (b * nc + c, j))
    return pl.pallas_call(
        functools.partial(_mlstm_prompt_kernel, nh=nh, dh=dh),
        grid=(batch, nc),
        in_specs=[blk(1), blk(2), blk(3), blk(4),
                  pl.BlockSpec((ng, ln), lambda b, c: (0, b * nc + c))],
        out_specs=[
            pl.BlockSpec((ln, width), lambda b, c: (b * nc + c, 0)),
            pl.BlockSpec((None, nh, dh, dh), lambda b, c: (b, 0, 0, 0)),
            pl.BlockSpec((None, nh, 1, dh), lambda b, c: (b, 0, 0, 0)),
            pl.BlockSpec((None, nh, 1, dh), lambda b, c: (b, 0, 0, 0)),
        ],
        out_shape=[
            jax.ShapeDtypeStruct((batch * seq, width), BF16),
            jax.ShapeDtypeStruct((batch, nh, dh, dh), F32),
            jax.ShapeDtypeStruct((batch, nh, 1, dh), F32),
            jax.ShapeDtypeStruct((batch, nh, 1, dh), F32),
        ],
        scratch_shapes=[pltpu.VMEM((nh, dh, dh), F32), pltpu.VMEM((nh, 1, dh), F32),
                        pltpu.VMEM((nh, 1, dh), F32)],
        compiler_params=_params("parallel", "arbitrary"),
        name="mlstm_prompt",
    )(z, z, z, z, gates_t)


def _mlstm_step_kernel(q_ref, k_ref, v_ref, o_ref, li_ref, fp_ref, c_ref, n_ref, m_ref,
                       h_ref, c_out, n_out, m_out, *, nh, dh):
    eye = (lax.broadcasted_iota(I32, (1, dh, dh), 1) == lax.broadcasted_iota(I32, (1, dh, dh), 2))
    scale = dh ** -0.5
    for h in range(nh):
        sl = slice(h * dh, (h + 1) * dh)
        q = q_ref[:, :, sl]
        k = k_ref[:, :, sl] * scale
        v = v_ref[:, :, sl]
        c = c_ref[:, h]
        n = n_ref[:, h]
        m = m_ref[:, h]
        li = li_ref[:, h]
        lf = jax.nn.log_sigmoid(fp_ref[:, h])
        inter = lf + m
        m_row = jnp.maximum(inter, li)
        w_intra = jnp.exp(li - m_row)
        w_inter = jnp.exp(inter - m_row)
        sc = jnp.sum(q * k, axis=-1, keepdims=True) * w_intra
        v_c = jnp.sum(jnp.where(eye, v, 0.0), axis=-1, keepdims=True)
        num = sc * v_c + w_inter * jnp.sum(c * q, axis=-1, keepdims=True)
        den = sc + w_inter * jnp.sum(n * q, axis=-1, keepdims=True)
        h_c = num / jnp.maximum(jnp.abs(den), jnp.exp(-m_row))
        h_l = jnp.sum(jnp.where(eye, h_c, 0.0), axis=1, keepdims=True)
        h_ref[:, :, sl] = h_l * jax.nn.sigmoid(o_ref[:, :, sl])
        wg = jnp.exp(li - m_row)
        decay = jnp.exp(inter - m_row)
        c_out[:, h] = decay * c + (v_c * wg) * k
        n_out[:, h] = decay * n + wg * k
        m_out[:, h] = m_row


def _mlstm_step(z3, li, fp, c, n, m, nh, dh):
    batch = c.shape[0]
    bb = SUBLANES
    width = nh * dh
    blk = lambda j: pl.BlockSpec((bb, 1, width), lambda i: (i, 0, j))
    st4 = lambda a, b: pl.BlockSpec((bb, nh, a, b), lambda i: (i, 0, 0, 0))
    return pl.pallas_call(
        functools.partial(_mlstm_step_kernel, nh=nh, dh=dh),
        grid=(batch // bb,),
        in_specs=[blk(1), blk(2), blk(3), blk(4), st4(1, 1), st4(1, 1),
                  st4(dh, dh), st4(1, dh), st4(1, 1)],
        out_specs=[pl.BlockSpec((bb, 1, width), lambda i: (i, 0, 0)),
                   st4(dh, dh), st4(1, dh), st4(1, 1)],
        out_shape=[
            jax.ShapeDtypeStruct((batch, 1, width), F32),
            jax.ShapeDtypeStruct((batch, nh, dh, dh), F32),
            jax.ShapeDtypeStruct((batch, nh, 1, dh), F32),
            jax.ShapeDtypeStruct((batch, nh, 1, 1), F32),
        ],
        compiler_params=_params("parallel"),
        name="mlstm_step",
    )(z3, z3, z3, z3, li, fp, c, n, m)


def _split_bf16(a):
    hi = a.astype(BF16)
    return hi, (a - hi.astype(F32)).astype(BF16)


def _route(hn, wr_ref, br_ref, tril_ref, carry_ref, rinfo_ref):
    ngrp, epg = MOE_GROUPS, MOE_EXPERTS_PER_GROUP
    h_hi, h_lo = _split_bf16(hn)
    w_hi, w_lo = wr_ref[0], wr_ref[1]
    logits = _dot(h_hi, w_hi) + (_dot(h_lo, w_hi) + _dot(h_hi, w_lo)) + br_ref[...]
    tm = logits.shape[0]
    lane = lax.broadcasted_iota(I32, (tm, LANES), 1)
    neg = -jnp.inf

    def first_max(x):
        mx = jnp.max(x, axis=1, keepdims=True)
        idx = jnp.min(jnp.where(x == mx, lane, LANES), axis=1, keepdims=True)
        return mx, idx

    is_grp = lane < ngrp
    gmax, gsel = first_max(jnp.where(is_grp, logits, neg))
    g_w = 1.0 / jnp.sum(jnp.where(is_grp, jnp.exp(logits - gmax), 0.0), axis=1, keepdims=True)
    lo = ngrp + gsel * epg
    el = jnp.where((lane >= lo) & (lane < lo + epg), logits, neg)
    v1, i1 = first_max(el)
    v2, i2 = first_max(jnp.where(lane == i1, neg, el))
    e2 = jnp.exp(v2 - v1)
    w1 = g_w / (1.0 + e2)
    w2 = g_w * e2 / (1.0 + e2)
    eid1 = i1 - ngrp
    eid2 = i2 - ngrp

    hit1 = lane == eid1
    hit2 = lane == eid2
    onehot = jnp.where(hit1 | hit2, 1.0, 0.0)
    carry = carry_ref[...]
    prefix = _dot(tril_ref[...], onehot.astype(BF16)) + carry
    rank1 = jnp.sum(jnp.where(hit1, prefix, 0.0), axis=1, keepdims=True)
    rank2 = jnp.sum(jnp.where(hit2, prefix, 0.0), axis=1, keepdims=True)
    carry_ref[...] = carry + jnp.sum(onehot, axis=0, keepdims=True)

    cols = (eid1.astype(F32), eid2.astype(F32), w1, w2, rank1, rank2)
    info = jnp.zeros((tm, LANES), F32)
    for j, cval in enumerate(cols):
        info = jnp.where(lane == j, cval, info)
    rinfo_ref[...] = info


def _outproj_route_kernel(xm_ref, xt_ref, pm_ref, pt_ref, mm_ref, mt_ref,
                          wo_ref, g_ref, wr_ref, br_ref, tril_ref,
                          x1_ref, hn_ref, rinfo_ref, cnt_ref, carry_ref):
    @pl.when(pl.program_id(0) == 0)
    def _():
        carry_ref[...] = jnp.zeros(carry_ref.shape, F32)

    half = pm_ref.shape[1]
    mix = (_dot(_stacked(pm_ref, pt_ref), wo_ref[0:half, :])
           + _dot(_stacked(mm_ref, mt_ref), wo_ref[half:2 * half, :]))
    x1 = _stacked(xm_ref, xt_ref) + mix
    x1_ref[...] = x1
    hn = _rms(x1, g_ref[...])
    hn_ref[...] = hn
    _route(hn, wr_ref, br_ref, tril_ref, carry_ref, rinfo_ref)
    cnt_ref[...] = carry_ref[...]


def _glu_route_kernel(x_ref, ym_ref, yt_ref, wglu_ref, g_ref, wr_ref, br_ref, tril_ref,
                      x1_ref, hn_ref, rinfo_ref, cnt_ref, carry_ref):
    @pl.when(pl.program_id(0) == 0)
    def _():
        carry_ref[...] = jnp.zeros(carry_ref.shape, F32)

    d = x_ref.shape[1]
    ag = _dot(jax.nn.gelu(_stacked(ym_ref, yt_ref)).astype(BF16), wglu_ref[...])
    x1 = x_ref[...] + ag[:, 0:d] * jax.nn.sigmoid(ag[:, d:2 * d])
    x1_ref[...] = x1
    hn = _rms(x1, g_ref[...])
    hn_ref[...] = hn
    _route(hn, wr_ref, br_ref, tril_ref, carry_ref, rinfo_ref)
    cnt_ref[...] = carry_ref[...]


def _mix_route(kernel, name, n, row_specs, rows, w, g, wr, br, tril):
    d = g.shape[1]
    tm = TOKEN_TILE
    full = lambda i: (0, 0)
    return pl.pallas_call(
        kernel,
        grid=(n // tm,),
        in_specs=row_specs + [
            pl.BlockSpec(w.shape, full),
            pl.BlockSpec((1, d), full),
            pl.BlockSpec((2, d, LANES), lambda i: (0, 0, 0)),
            pl.BlockSpec((1, LANES), full),
            pl.BlockSpec((tm, tm), full),
        ],
        out_specs=[
            pl.BlockSpec((tm, d), lambda i: (i, 0)),
            pl.BlockSpec((tm, d), lambda i: (i, 0)),
            pl.BlockSpec((tm, LANES), lambda i: (i, 0)),
            pl.BlockSpec((1, LANES), full),
        ],
        out_shape=[
            jax.ShapeDtypeStruct((n, d), F32),
            jax.ShapeDtypeStruct((n, d), F32),
            jax.ShapeDtypeStruct((n, LANES), F32),
            jax.ShapeDtypeStruct((1, LANES), F32),
        ],
        scratch_shapes=[pltpu.VMEM((1, LANES), F32)],
        compiler_params=_params("arbitrary"),
        name=name,
    )(*rows, w, g, wr, br, tril)


def _index_copy(pos_hbm, idx_s, sem_i, tile, slot):
    return pltpu.make_async_copy(pos_hbm.at[tile], idx_s.at[slot], sem_i.at[slot])


def _dispatch_kernel(pos_hbm, hn_hbm, xs_hbm, idx_s, sem_i, sem_d, *, tm):
    i = pl.program_id(0)
    nt = pl.num_programs(0)
    slot = i % 2

    @pl.when(i == 0)
    def _():
        _index_copy(pos_hbm, idx_s, sem_i, 0, 0).start()

    _index_copy(pos_hbm, idx_s, sem_i, i, slot).wait()

    @pl.when(i + 1 < nt)
    def _():
        _index_copy(pos_hbm, idx_s, sem_i, i + 1, 1 - slot).start()

    base = i * tm

    def issue(r, carry):
        row = hn_hbm.at[pl.ds(base + r, 1)]
        pltpu.make_async_copy(row, xs_hbm.at[pl.ds(idx_s[slot, 0, r], 1)], sem_d).start()
        pltpu.make_async_copy(row, xs_hbm.at[pl.ds(idx_s[slot, 0, tm + r], 1)], sem_d).start()
        return carry

    lax.fori_loop(0, tm, issue, 0, unroll=DMA_ISSUE_UNROLL)

    def wait_tile():
        whole = pltpu.make_async_copy(hn_hbm.at[pl.ds(0, tm)], xs_hbm.at[pl.ds(0, tm)], sem_d)
        whole.wait()
        whole.wait()

    @pl.when(i > 0)
    def _():
        wait_tile()

    @pl.when(i == nt - 1)
    def _():
        wait_tile()


def _dispatch(pos_tiles, hn):
    n, d = hn.shape
    tm = TOKEN_TILE
    return pl.pallas_call(
        functools.partial(_dispatch_kernel, tm=tm),
        grid=(n // tm,),
        in_specs=[pl.BlockSpec(memory_space=pl.ANY), pl.BlockSpec(memory_space=pl.ANY)],
        out_specs=pl.BlockSpec(memory_space=pl.ANY),
        out_shape=jax.ShapeDtypeStruct((2 * n, d), F32),
        scratch_shapes=[pltpu.SMEM((2, 1, 2 * tm), I32), pltpu.SemaphoreType.DMA((2,)),
                        pltpu.SemaphoreType.DMA],
        compiler_params=_params("arbitrary"),
        name="moe_dispatch",
    )(pos_tiles, hn)


def _moe_kernel(vt_ref, ve_ref, von_ref, lo_ref, hi_ref, xs_ref, wg_ref, wu_ref, wd_ref,
                eo_ref, wgb, wub, wdb):
    v = pl.program_id(0)
    tr = xs_ref.shape[0]
    prev = jnp.maximum(v - 1, 0)
    e = ve_ref[v]
    new_expert = jnp.logical_or(v == 0, e != ve_ref[prev])
    first_visit = jnp.logical_or(v == 0, vt_ref[v] != vt_ref[prev])

    @pl.when(von_ref[v] == 1)
    def _():
        @pl.when(new_expert)
        def _():
            wgb[...] = wg_ref[...].astype(BF16)
            wub[...] = wu_ref[...].astype(BF16)
            wdb[...] = wd_ref[...].astype(BF16)

        x = xs_ref[...].astype(BF16)
        act = jax.nn.silu(_dot(x, wgb[...])) * _dot(x, wub[...])
        row = vt_ref[v] * tr + lax.broadcasted_iota(I32, (tr, 1), 0)
        act = jnp.where((row >= lo_ref[e]) & (row < hi_ref[e]), act, 0.0)
        res = _dot(act.astype(BF16), wdb[...])

        @pl.when(first_visit)
        def _():
            eo_ref[...] = res

        @pl.when(jnp.logical_not(first_visit))
        def _():
            eo_ref[...] += res


def _moe_experts(vt, ve, von, lo, hi, xs, wg, wu, wd):
    nv = vt.shape[0]
    rows, d = xs.shape
    hid = wg.shape[2]
    tr = EXPERT_ROW_TILE
    wspec = lambda a, b: pl.BlockSpec((None, a, b), lambda v, vt, ve, von, lo, hi: (ve[v], 0, 0))
    grid_spec = pltpu.PrefetchScalarGridSpec(
        num_scalar_prefetch=5,
        grid=(nv,),
        in_specs=[
            pl.BlockSpec((tr, d), lambda v, vt, ve, von, lo, hi: (vt[v], 0)),
            wspec(d, hid), wspec(d, hid), wspec(hid, d),
        ],
        out_specs=pl.BlockSpec((tr, d), lambda v, vt, ve, von, lo, hi: (vt[v], 0)),
        scratch_shapes=[pltpu.VMEM((d, hid), BF16), pltpu.VMEM((d, hid), BF16),
                        pltpu.VMEM((hid, d), BF16)],
    )
    return pl.pallas_call(
        _moe_kernel,
        grid_spec=grid_spec,
        out_shape=jax.ShapeDtypeStruct((rows, d), F32),
        compiler_params=_params("arbitrary"),
        name="moe_experts",
    )(vt, ve, von, lo, hi, xs, wg, wu, wd)


def _combine_kernel(pos_hbm, eo_hbm, x_ref, rinfo_ref, g_ref, o1_ref, o2_ref, idx_s, a_buf, b_buf,
                    sem_i, sem_a, sem_b, *, last_layer):
    i = pl.program_id(0)
    nt = pl.num_programs(0)
    tm = a_buf.shape[1]
    slot = i % 2

    def gathers(s):
        def issue(r, carry):
            pltpu.make_async_copy(eo_hbm.at[pl.ds(idx_s[s, 0, r], 1)],
                                  a_buf.at[s, pl.ds(r, 1)], sem_a.at[s]).start()
            pltpu.make_async_copy(eo_hbm.at[pl.ds(idx_s[s, 0, tm + r], 1)],
                                  b_buf.at[s, pl.ds(r, 1)], sem_b.at[s]).start()
            return carry

        lax.fori_loop(0, tm, issue, 0, unroll=DMA_ISSUE_UNROLL)

    @pl.when(i == 0)
    def _():
        first = _index_copy(pos_hbm, idx_s, sem_i, 0, 0)
        first.start()
        first.wait()
        gathers(0)

        @pl.when(nt > 1)
        def _():
            _index_copy(pos_hbm, idx_s, sem_i, 1, 1).start()

    @pl.when(i + 1 < nt)
    def _():
        _index_copy(pos_hbm, idx_s, sem_i, i + 1, 1 - slot).wait()
        gathers(1 - slot)

    @pl.when(i + 2 < nt)
    def _():
        _index_copy(pos_hbm, idx_s, sem_i, i + 2, slot).start()

    pltpu.make_async_copy(eo_hbm.at[pl.ds(0, tm)], a_buf.at[slot], sem_a.at[slot]).wait()
    pltpu.make_async_copy(eo_hbm.at[pl.ds(0, tm)], b_buf.at[slot], sem_b.at[slot]).wait()
    info = rinfo_ref[...]
    x2 = x_ref[...] + (info[:, 2:3] * a_buf[slot] + info[:, 3:4] * b_buf[slot])
    hn = _rms(x2, g_ref[...])
    if last_layer:
        @pl.when(i < nt - 1)
        def _():
            o1_ref[...] = hn

        @pl.when(i == nt - 1)
        def _():
            o2_ref[...] = hn
    else:
        o1_ref[...] = x2
        o2_ref[...] = hn


def _combine(pos_tiles, eo, x, rinfo, g, last_layer):
    n, d = x.shape
    tm = TOKEN_TILE
    nt = n // tm
    row = pl.BlockSpec((tm, d), lambda i: (i, 0))
    if last_layer:
        out_specs = [pl.BlockSpec((tm, d), lambda i: (jnp.minimum(i, nt - 2), 0)),
                     pl.BlockSpec((tm, d), lambda i: (0, 0))]
        out_shape = [jax.ShapeDtypeStruct((n - tm, d), F32), jax.ShapeDtypeStruct((tm, d), F32)]
    else:
        out_specs = [row, row]
        out_shape = [jax.ShapeDtypeStruct((n, d), F32), jax.ShapeDtypeStruct((n, d), F32)]
    return pl.pallas_call(
        functools.partial(_combine_kernel, last_layer=last_layer),
        grid=(nt,),
        in_specs=[
            pl.BlockSpec(memory_space=pl.ANY),
            pl.BlockSpec(memory_space=pl.ANY),
            row,
            pl.BlockSpec((tm, LANES), lambda i: (i, 0)),
            pl.BlockSpec((1, d), lambda i: (0, 0)),
        ],
        out_specs=out_specs,
        out_shape=out_shape,
        scratch_shapes=[
            pltpu.SMEM((2, 1, 2 * tm), I32),
            pltpu.VMEM((2, tm, d), F32),
            pltpu.VMEM((2, tm, d), F32),
            pltpu.SemaphoreType.DMA((2,)),
            pltpu.SemaphoreType.DMA((2,)),
            pltpu.SemaphoreType.DMA((2,)),
        ],
        compiler_params=_params("arbitrary"),
        name="moe_combine",
    )(pos_tiles, eo, x, rinfo, g)


def _moe(x1, hn, rinfo, counts, wg, wu, wd, next_gain, last_layer):
    n = x1.shape[0]
    ne = wg.shape[0]
    tr = EXPERT_ROW_TILE
    tm = TOKEN_TILE
    nv = (2 * n) // tr + ne - 1
    eid = rinfo[:, 0:2].astype(I32)
    rank = rinfo[:, 4:6].astype(I32)
    cnt = counts[0, :ne].astype(I32)
    seg_end = jnp.cumsum(cnt)
    seg_start = seg_end - cnt
    experts = jnp.arange(ne, dtype=I32)
    pos = jnp.sum(jnp.where(eid[:, :, None] == experts, seg_start, 0), axis=-1) + rank
    pos_tiles = pos.reshape(n // tm, tm, 2).transpose(0, 2, 1).reshape(n // tm, 1, 2 * tm)

    first_tile = seg_start // tr
    tiles_e = jnp.where(cnt > 0, (seg_end - 1) // tr - first_tile + 1, 0)
    v_end = jnp.cumsum(tiles_e)
    v_start = v_end - tiles_e
    total = v_end[-1]
    vis = jnp.arange(nv, dtype=I32)
    vc = jnp.minimum(vis, jnp.maximum(total - 1, 0))
    ve = jnp.sum((vc[:, None] >= v_end[None, :]).astype(I32), axis=1)
    pick = lambda tab: jnp.sum(jnp.where(ve[:, None] == experts, tab, 0), axis=1)
    vt = pick(first_tile) + (vc - pick(v_start))
    von = (vis < total).astype(I32)

    xs = _dispatch(pos_tiles, hn)
    eo = _moe_experts(vt, ve, von, seg_start, seg_end, xs, wg, wu, wd)
    return _combine(pos_tiles, eo, x1, rinfo, next_gain, last_layer)


def _cis(log_mag, ang):
    mag = jnp.exp(log_mag)
    return mag * jnp.cos(ang), mag * jnp.sin(ang)


def _dot_hi(a, b):
    return jnp.dot(a, b, precision=HIGHEST, preferred_element_type=F32)


def _s5_prep_kernel(lsc_ref, lsr_ref, lrc_ref, lic_ref, lrr_ref, lir_ref, ctre_ref, ctim_ref,
                    btre_ref, btim_ref,
                    t_ref, wre_ref, wim_ref, cyre_ref, cyim_ref, apr_ref, api_ref,
                    bbr_ref, bbi_ref, lbr_ref, lbi_ref, *, gw, gpb, sub):
    blk = gpb * gw
    wide = sub * blk
    nsl = lrc_ref.shape[1]
    ns = nsl // gpb
    sh_blk, sh_gw, sh_ns = blk.bit_length() - 1, gw.bit_length() - 1, ns.bit_length() - 1
    dt_c, dt_r = jnp.exp(lsc_ref[0]), jnp.exp(lsr_ref[0])
    ldt_c_re, ldt_c_im = lrc_ref[0] * dt_c, lic_ref[0] * dt_c
    ldt_r_re, ldt_r_im = lrr_ref[0] * dt_r, lir_ref[0] * dt_r

    lane_w = lax.broadcasted_iota(I32, (1, wide), 1)
    spread = jnp.where((lax.broadcasted_iota(I32, (gw, wide), 1) & (gw - 1))
                       == lax.broadcasted_iota(I32, (gw, wide), 0), 1.0, 0.0)
    cre = _dot_hi(ctre_ref[0], spread)
    cim = _dot_hi(ctim_ref[0], spread)
    same = (jnp.right_shift(lax.broadcasted_iota(I32, (nsl, 1), 0), sh_ns)
            == (jnp.right_shift(lane_w, sh_gw) & (gpb - 1)))
    assert 2 * blk == LANES
    tau = lax.broadcasted_iota(I32, (1, LANES), 1).astype(F32)
    pw_re, pw_im = _cis(tau * ldt_c_re, tau * ldt_c_im)
    low = lax.broadcasted_iota(I32, (nsl, LANES), 1) < blk

    def spread_pow(p, first):
        col = lambda t: jnp.broadcast_to(p[:, t:t + 1], (nsl, LANES))
        return jnp.concatenate([jnp.where(low, col(first + 2 * m), col(first + 2 * m + 1))
                                for m in range(sub // 2)], axis=1)

    def c_lam_pow(first):
        pr, pi = spread_pow(pw_re, first), spread_pow(pw_im, first)
        return (jnp.where(same, pr * cre - pi * cim, 0.0),
                jnp.where(same, -(pr * cim + pi * cre), 0.0))

    clr0, cli0 = c_lam_pow(0)
    clr1, cli1 = c_lam_pow(1)
    cyre_ref[0] = clr1.astype(BF16)
    cyim_ref[0] = cli1.astype(BF16)

    lbr, lbi = _cis(ldt_r_re, ldt_r_im)
    lbr_ref[0] = lbr
    lbi_ref[0] = lbi
    lr, li = lrr_ref[0], lir_ref[0]
    nr, ni = lbr - 1.0, lbi
    den = lr * lr + li * li
    fr = (nr * lr + ni * li) / den
    fi = (ni * lr - nr * li) / den
    bre, bim = btre_ref[0], btim_ref[0]
    bbr = fr * bre - fi * bim
    bbi = fr * bim + fi * bre
    bbr_ref[0] = bbr
    bbi_ref[0] = bbi

    lane_t = lax.broadcasted_iota(I32, (gw, wide), 1)
    for g in range(gpb):
        ps = slice(g * ns, (g + 1) * ns)
        r = _dot_hi(bbr[:, ps], clr0[ps, :]) + _dot_hi(bbi[:, ps], cli0[ps, :])
        for j in range(sub):
            tb = r if j == 0 else jnp.where(lane_t >= blk * j, pltpu.roll(r, blk * j, 1), 0.0)
            r0 = j * blk + g * gw
            t_ref[0, r0:r0 + gw, :] = tb.astype(BF16)

    rows = lax.broadcasted_iota(I32, (wide, 1), 0)
    spread_t = jnp.where((lax.broadcasted_iota(I32, (wide, gw), 0) & (gw - 1))
                         == lax.broadcasted_iota(I32, (wide, gw), 1), 1.0, 0.0)
    bbr_t = _dot_hi(spread_t, bbr)
    bbi_t = _dot_hi(spread_t, bbi)
    same_w = ((jnp.right_shift(rows, sh_gw) & (gpb - 1))
              == jnp.right_shift(lax.broadcasted_iota(I32, (1, nsl), 1), sh_ns))
    rj = ((sub - 1) - lax.broadcasted_iota(I32, (sub, 1), 0)).astype(F32)
    q_re, q_im = _cis(rj * ldt_r_re, rj * ldt_r_im)
    per_step = lambda q: jnp.concatenate(
        [jnp.broadcast_to(q[j:j + 1, :], (blk, nsl)) for j in range(sub)], axis=0)
    pr, pi = per_step(q_re), per_step(q_im)
    wre_ref[0] = jnp.where(same_w, pr * bbr_t - pi * bbi_t, 0.0).astype(BF16)
    wim_ref[0] = jnp.where(same_w, pr * bbi_t + pi * bbr_t, 0.0).astype(BF16)

    nlev = apr_ref.shape[1]
    pw = (sub * jnp.left_shift(1, lax.broadcasted_iota(I32, (nlev, 1), 0))).astype(F32)
    apr, api = _cis(pw * ldt_r_re, pw * ldt_r_im)
    apr_ref[0] = apr
    api_ref[0] = api


def _s5_prep(log_step, lam_re, lam_im, b_re, b_im, c_re, c_im):
    ng, ns = lam_re.shape
    gw = b_re.shape[2]
    sub, gpb = S5_SUB, S5_GROUPS_PER_BLOCK
    nblk = ng // gpb
    nsl = gpb * ns
    wide = sub * gpb * gw
    ls = jnp.repeat(log_step, ns)
    ct = lambda c: jnp.transpose(c, (0, 2, 1)).reshape(nblk, nsl, gw)
    bt = lambda b: jnp.transpose(b.reshape(nblk, gpb, ns, gw), (0, 3, 1, 2)).reshape(nblk, gw, nsl)
    ins = [ls.reshape(nblk, nsl, 1), ls.reshape(nblk, 1, nsl),
           lam_re.reshape(nblk, nsl, 1), lam_im.reshape(nblk, nsl, 1),
           lam_re.reshape(nblk, 1, nsl), lam_im.reshape(nblk, 1, nsl),
           ct(c_re), ct(c_im), bt(b_re), bt(b_im)]
    spec = lambda a: pl.BlockSpec((1,) + a.shape[1:], lambda g: (g, 0, 0))
    outs = [
        jax.ShapeDtypeStruct((nblk, wide, wide), BF16),
        jax.ShapeDtypeStruct((nblk, wide, nsl), BF16),
        jax.ShapeDtypeStruct((nblk, wide, nsl), BF16),
        jax.ShapeDtypeStruct((nblk, nsl, wide), BF16),
        jax.ShapeDtypeStruct((nblk, nsl, wide), BF16),
        jax.ShapeDtypeStruct((nblk, SUBLANES, nsl), F32),
        jax.ShapeDtypeStruct((nblk, SUBLANES, nsl), F32),
        jax.ShapeDtypeStruct((nblk, gw, nsl), F32),
        jax.ShapeDtypeStruct((nblk, gw, nsl), F32),
        jax.ShapeDtypeStruct((nblk, 1, nsl), F32),
        jax.ShapeDtypeStruct((nblk, 1, nsl), F32),
    ]
    return pl.pallas_call(
        functools.partial(_s5_prep_kernel, gw=gw, gpb=gpb, sub=sub),
        grid=(nblk,),
        in_specs=[spec(a) for a in ins],
        out_specs=[spec(o) for o in outs],
        out_shape=outs,
        compiler_params=_params("parallel"),
        name="s5_prep",
    )(*ins)


def _s5_prompt_kernel(x_ref, t_ref, wre_ref, wim_ref, cyre_ref, cyim_ref, apr_ref, api_ref, d_ref,
                      y_ref, hre_out, him_out, u_s, y4_s, sre_s, sim_s, *, nb, seq, sub):
    nk = seq // sub
    rows = nb * nk
    hl = LANES // 2
    pad = nk // 2
    nlev = nk.bit_length() - 1
    nsl = sre_s.shape[1]
    low = lax.broadcasted_iota(I32, (nk, LANES), 1) < hl
    kidx = lax.broadcasted_iota(I32, (rows, 1), 0) & (nk - 1)
    sre_s[0:pad, :] = jnp.zeros((pad, nsl), F32)
    sim_s[0:pad, :] = jnp.zeros((pad, nsl), F32)

    def step_rows(b, j):
        return (pl.ds(b * seq + j, nk, stride=sub), slice(None))

    for b in range(nb):
        rs = slice(b * nk, (b + 1) * nk)
        for m in range(sub // 2):
            ls = slice(m * LANES, (m + 1) * LANES)
            s0 = x_ref[step_rows(b, 2 * m)]
            s1 = x_ref[step_rows(b, 2 * m + 1)]
            u_s[0, rs, ls] = jnp.where(low, s0, pltpu.roll(s1, hl, 1)).astype(BF16)
            u_s[1, rs, ls] = jnp.where(low, pltpu.roll(s0, hl, 1), s1).astype(BF16)

    live = slice(pad, pad + rows)
    for half in range(2):
        u = u_s[half]
        hre = _dot(u, wre_ref[half])
        him = _dot(u, wim_ref[half])
        for i in range(nlev):
            s = 1 << i
            sre_s[live, :] = hre
            sim_s[live, :] = him
            pre = sre_s[pad - s:pad - s + rows, :]
            pim = sim_s[pad - s:pad - s + rows, :]
            ar = apr_ref[half, i:i + 1, :]
            ai = api_ref[half, i:i + 1, :]
            ok = kidx >= s
            hre = hre + jnp.where(ok, ar * pre - ai * pim, 0.0)
            him = him + jnp.where(ok, ar * pim + ai * pre, 0.0)
        sre_s[live, :] = hre
        sim_s[live, :] = him
        ls = slice(half * nsl, (half + 1) * nsl)
        for b in range(nb):
            last = pad + (b + 1) * nk - 1
            hre_out[b:b + 1, ls] = sre_s[last:last + 1, :]
            him_out[b:b + 1, ls] = sim_s[last:last + 1, :]
        started = kidx >= 1
        hpre = jnp.where(started, sre_s[pad - 1:pad - 1 + rows, :], 0.0).astype(BF16)
        hpim = jnp.where(started, sim_s[pad - 1:pad - 1 + rows, :], 0.0).astype(BF16)
        y4_s[half] = _dot(u, t_ref[half]) + _dot(hpre, cyre_ref[half]) + _dot(hpim, cyim_ref[half])

    d = d_ref[...]
    for b in range(nb):
        rs = slice(b * nk, (b + 1) * nk)
        for m in range(sub // 2):
            ls = slice(m * LANES, (m + 1) * LANES)
            ca = y4_s[0, rs, ls]
            cb = y4_s[1, rs, ls]
            y0 = jnp.where(low, ca, pltpu.roll(cb, hl, 1))
            y1 = jnp.where(low, pltpu.roll(ca, hl, 1), cb)
            y_ref[step_rows(b, 2 * m)] = y0 + d * x_ref[step_rows(b, 2 * m)]
            y_ref[step_rows(b, 2 * m + 1)] = y1 + d * x_ref[step_rows(b, 2 * m + 1)]


def _s5_prompt(h, prep, d_row, batch, seq):
    d = h.shape[1]
    tmat, wre, wim, cyre, cyim, apr, api = prep[:7]
    nblk, wide, nsl = wre.shape
    sub, nb = S5_SUB, S5_BATCH_PER_STEP
    nk = seq // sub
    assert nk & (nk - 1) == 0 and nk.bit_length() - 1 <= apr.shape[1]
    ntile = d // LANES
    assert nblk == 2 * ntile
    rows = nb * nk
    wspec = lambda a: pl.BlockSpec((2,) + a.shape[1:], lambda t, b: (t, 0, 0))
    xspec = pl.BlockSpec((nb * seq, LANES), lambda t, b: (b, t))
    sspec = pl.BlockSpec((None, nb, 2 * nsl), lambda t, b: (b, 0, t))
    return pl.pallas_call(
        functools.partial(_s5_prompt_kernel, nb=nb, seq=seq, sub=sub),
        grid=(ntile, batch // nb),
        in_specs=[xspec, wspec(tmat), wspec(wre), wspec(wim), wspec(cyre), wspec(cyim),
                  wspec(apr), wspec(api), pl.BlockSpec((1, LANES), lambda t, b: (0, t))],
        out_specs=[xspec, sspec, sspec],
        out_shape=[
            jax.ShapeDtypeStruct((batch * seq, d), F32),
            jax.ShapeDtypeStruct((batch // nb, nb, ntile * 2 * nsl), F32),
            jax.ShapeDtypeStruct((batch // nb, nb, ntile * 2 * nsl), F32),
        ],
        scratch_shapes=[
            pltpu.VMEM((2, rows, wide), BF16),
            pltpu.VMEM((2, rows, wide), F32),
            pltpu.VMEM((nk // 2 + rows, nsl), F32),
            pltpu.VMEM((nk // 2 + rows, nsl), F32),
        ],
        compiler_params=_params("parallel", "parallel"),
        name="s5_prompt",
    )(h, tmat, wre, wim, cyre, cyim, apr, api, d_row)


def _s5_step_kernel(u_ref, h_ref, hsw_ref, bb_ref, la_ref, lb_ref, cc_ref, d_ref, hn_ref, y_ref):
    for g in range(u_ref.shape[0]):
        u = u_ref[g]
        hn = (la_ref[g] * h_ref[g] + lb_ref[g] * hsw_ref[g]
              + jnp.dot(u, bb_ref[g], precision=HIGHEST, preferred_element_type=F32))
        hn_ref[g] = hn
        y_ref[g] = (jnp.dot(hn, cc_ref[g], precision=HIGHEST, preferred_element_type=F32)
                    + d_ref[g] * u)


def _s5_step(u_p, h_cat, h_swp, bb_cat, la, lb, cc, d_p):
    ng = u_p.shape[0]
    gb = SUBLANES
    spec = lambda a: pl.BlockSpec((gb,) + a.shape[1:], lambda g: (g, 0, 0))
    ins = [u_p, h_cat, h_swp, bb_cat, la, lb, cc, d_p]
    outs = [jax.ShapeDtypeStruct(h_cat.shape, F32), jax.ShapeDtypeStruct(u_p.shape, F32)]
    return pl.pallas_call(
        _s5_step_kernel,
        grid=(ng // gb,),
        in_specs=[spec(a) for a in ins],
        out_specs=[spec(o) for o in outs],
        out_shape=outs,
        compiler_params=_params("parallel"),
        name="s5_step",
    )(*ins)


def _pad_to(a, axis, size):
    pad = [(0, 0)] * a.ndim
    pad[axis] = (0, size - a.shape[axis])
    return jnp.pad(a, pad)


def kernel(x_prompt, x_sample, state_pool, state_mlstm_c, state_mlstm_n, state_mlstm_m, state_s5_re, state_s5_im, norm_mix, norm_ffn, norm_final, w_in_ab, b_gates, pool_w, pool_scale, w_out_ab, s5_lam_re, s5_lam_im, s5_log_step, s5_b_re, s5_b_im, s5_c_re, s5_c_im, s5_d, w_glu, moe_w_group, moe_b_group, moe_w_expert, moe_b_expert, moe_w_gate, moe_w_up, moe_w_down):
    bp, tp, d = x_prompt.shape
    bs = x_sample.shape[0]
    n_p = bp * tp
    tm = TOKEN_TILE
    assert n_p % tm == 0 and bs <= tm
    n_fill = tm - bs
    nh, dh = state_mlstm_c.shape[2], state_mlstm_c.shape[3]
    pool_width = state_pool.shape[3]
    ml_width = nh * dh
    n_main = pool_width + 4 * ml_width
    n_gates = 2 * nh
    ngrp_s5, n_state = s5_lam_re.shape[1], s5_lam_re.shape[2]
    gw_s5 = d // ngrp_s5

    n = n_p + tm

    def tail_tile(sample_rows, dtype):
        return _pad_to(sample_rows.astype(dtype), 0, tm)

    x_main = x_prompt.reshape(n_p, d)
    x_tail = tail_tile(x_sample.reshape(bs, d), F32)
    tril = jnp.tril(jnp.ones((tm, tm), BF16), -1)

    def router_weights(l):
        wr = _pad_to(jnp.concatenate([moe_w_group[l], moe_w_expert[l]], axis=1), 1, LANES)
        br = jnp.concatenate([moe_b_group[l], moe_b_expert[l]])[None, :]
        hi = wr.astype(BF16)
        lo = (wr - hi.astype(F32)).astype(BF16)
        return jnp.stack([hi, lo]), _pad_to(br, 1, LANES)

    w_in = w_in_ab[0]
    w_g = w_in[:, n_main:]
    z, gates, gates_t = _inproj(
        x_main, x_tail, norm_mix[0][None, :], w_in[:, :n_main].astype(BF16),
        _pad_to(w_g, 1, LANES).astype(BF16), w_g.T.astype(BF16),
        _pad_to(b_gates[0][None, :], 1, LANES), b_gates[0][:, None])

    pw = pool_w[0].astype(BF16)
    ps = pool_scale[0][None, :]
    pool_y_p, pool_p = _pool_prompt(z, pw, ps, bp, tp)
    pool_y_s, pool_s_t = _pool_step(z, jnp.transpose(state_pool[0], (1, 0, 2)), pw, ps, n_p)
    pool_s = jnp.transpose(pool_s_t, (1, 0, 2))

    ml_y_p, c_p, n_p_st, m_p = _mlstm_prompt(z, gates_t, bp, tp, nh, dh)
    g_s = gates[n_p:n_p + bs, :n_gates]
    ml_y_s, c_s, n_s_st, m_s = _mlstm_step(
        z[n_p:n_p + bs].reshape(bs, 1, n_main),
        g_s[:, :nh].reshape(bs, nh, 1, 1), g_s[:, nh:].reshape(bs, nh, 1, 1),
        state_mlstm_c[0], state_mlstm_n[0].reshape(bs, nh, 1, dh),
        state_mlstm_m[0].reshape(bs, nh, 1, 1), nh, dh)

    rows = [x_main, x_tail, pool_y_p, tail_tile(pool_y_s, BF16),
            ml_y_p, tail_tile(ml_y_s.reshape(bs, ml_width), BF16)]
    specs = (_stacked_specs(x_main, x_tail) + _stacked_specs(pool_y_p, rows[3])
             + _stacked_specs(ml_y_p, rows[5]))
    wr, br = router_weights(0)
    x1, hn, rinfo, counts = _mix_route(
        _outproj_route_kernel, "outproj_route", n, specs, rows, w_out_ab[0].astype(BF16),
        norm_ffn[0][None, :], wr, br, tril)
    x2, h1 = _moe(x1, hn, rinfo, counts, moe_w_gate[0], moe_w_up[0], moe_w_down[0],
                  norm_mix[1][None, :], last_layer=False)

    prep = _s5_prep(s5_log_step[0], s5_lam_re[0], s5_lam_im[0], s5_b_re[0], s5_b_im[0],
                    s5_c_re[0], s5_c_im[0])
    y_p, hre_p, him_p = _s5_prompt(h1, prep, s5_d[0][None, :], bp, tp)
    s5_re_p = hre_p.reshape(bp, ngrp_s5, n_state)
    s5_im_p = him_p.reshape(bp, ngrp_s5, n_state)

    gpb = S5_GROUPS_PER_BLOCK
    per_group = lambda a: jnp.transpose(
        a.reshape(ngrp_s5 // gpb, gw_s5, gpb, n_state), (0, 2, 1, 3)).reshape(ngrp_s5, gw_s5, n_state)
    bbr, bbi = per_group(prep[7]), per_group(prep[8])
    lbr, lbi = prep[9].reshape(ngrp_s5, 1, n_state), prep[10].reshape(ngrp_s5, 1, n_state)
    d_g = s5_d[0].reshape(ngrp_s5, 1, gw_s5)
    u_s = h1[n_p:n_p + bs].reshape(bs, ngrp_s5, gw_s5).transpose(1, 0, 2)
    h_re = jnp.transpose(state_s5_re[0], (1, 0, 2))
    h_im = jnp.transpose(state_s5_im[0], (1, 0, 2))
    cc = jnp.concatenate([jnp.transpose(s5_c_re[0], (0, 2, 1)),
                          -jnp.transpose(s5_c_im[0], (0, 2, 1))], axis=1)
    hn_s, y_s = _s5_step(
        _pad_to(u_s, 2, LANES),
        jnp.concatenate([h_re, h_im], axis=2), jnp.concatenate([h_im, h_re], axis=2),
        _pad_to(jnp.concatenate([bbr, bbi], axis=2), 1, LANES),
        jnp.concatenate([lbr, lbr], axis=2), jnp.concatenate([-lbi, lbi], axis=2),
        _pad_to(cc, 2, LANES), _pad_to(d_g, 2, LANES))
    s5_re_s = jnp.transpose(hn_s[:, :, :n_state], (1, 0, 2))
    s5_im_s = jnp.transpose(hn_s[:, :, n_state:], (1, 0, 2))
    y_tail = tail_tile(jnp.transpose(y_s[:, :, :gw_s5], (1, 0, 2)).reshape(bs, d), F32)

    wr, br = router_weights(1)
    specs = [pl.BlockSpec((tm, d), lambda i: (i, 0))] + _stacked_specs(y_p, y_tail)
    x3, hn, rinfo, counts = _mix_route(
        _glu_route_kernel, "glu_route", n, specs, [x2, y_p, y_tail], w_glu[0].astype(BF16),
        norm_ffn[1][None, :], wr, br, tril)
    y_main, y_last = _moe(x3, hn, rinfo, counts, moe_w_gate[1], moe_w_up[1], moe_w_down[1],
                          norm_final[None, :], last_layer=True)

    return (y_main.reshape(bp, tp, d), y_last[:bs].reshape(bs, 1, d),
            pool_p[None], c_p[None], n_p_st.reshape(1, bp, nh, dh), m_p[:, :, 0, 0][None],
            s5_re_p[None], s5_im_p[None],
            pool_s[None], c_s[None], n_s_st.reshape(1, bs, nh, dh), m_s.reshape(1, bs, nh),
            s5_re_s[None], s5_im_s[None])
```

```python
import functools

import jax
import jax.numpy as jnp
from jax import lax
from jax.experimental import pallas as pl
from jax.experimental.pallas import tpu as pltpu

F32 = jnp.float32
BF16 = jnp.bfloat16
I32 = jnp.int32

PAST_LEN = 16384
POOL_WINDOWS = (2, 4, 8, 16)
POOL_BUF = max(POOL_WINDOWS) - 1
MLSTM_CHUNK = 128
S5_SUB = 16
MOE_GROUPS = 4
MOE_EXPERTS_PER_GROUP = 8
RMS_EPS = 1e-6

LANES = 128
SUBLANES = 8
VMEM_LIMIT_BYTES = 56 * 1024 * 1024

TOKEN_TILE = 512
EXPERT_ROW_TILE = 256
POOL_TIME_TILE = 512
DMA_ISSUE_UNROLL = 8
S5_GROUPS_PER_BLOCK = 4
S5_BATCH_PER_STEP = 4

HIGHEST = lax.Precision.HIGHEST


def _params(*sem):
    return pltpu.CompilerParams(dimension_semantics=sem, vmem_limit_bytes=VMEM_LIMIT_BYTES)


def _rms(x, g):
    return x * lax.rsqrt(jnp.mean(x * x, axis=-1, keepdims=True) + RMS_EPS) * g


def _dot(a, b):
    return jnp.dot(a, b, preferred_element_type=F32)


def _dot_nt(a, b):
    return lax.dot_general(a, b, (((1,), (1,)), ((), ())), preferred_element_type=F32)


def _dot_tn(a, b):
    return lax.dot_general(a, b, (((0,), (0,)), ((), ())), preferred_element_type=F32)


def _stacked(main_ref, tail_ref):
    last = pl.program_id(0) == pl.num_programs(0) - 1
    return jnp.where(last, tail_ref[...], main_ref[...])


def _stacked_specs(main, tail):
    tm, w = tail.shape
    last_main = main.shape[0] // tm - 1
    return [pl.BlockSpec((tm, w), lambda i: (jnp.minimum(i, last_main), 0)),
            pl.BlockSpec((tm, w), lambda i: (0, 0))]


def _inproj_kernel(xm_ref, xt_ref, g_ref, w_ref, wg_ref, wgt_ref, bg_ref, bgt_ref,
                   z_ref, gates_ref, gatest_ref):
    h = _rms(_stacked(xm_ref, xt_ref), g_ref[...]).astype(BF16)
    z_ref[...] = _dot(h, w_ref[...])
    gates_ref[...] = _dot(h, wg_ref[...]) + bg_ref[...]
    gatest_ref[...] = _dot_nt(wgt_ref[...], h) + bgt_ref[...]


def _inproj(x_main, x_tail, g, w, wg, wgt, bg, bgt):
    d = x_main.shape[1]
    tm = TOKEN_TILE
    n = x_main.shape[0] + tm
    nz = w.shape[1]
    ng = wgt.shape[0]
    full = lambda i: (0, 0)
    return pl.pallas_call(
        _inproj_kernel,
        grid=(n // tm,),
        in_specs=_stacked_specs(x_main, x_tail) + [
            pl.BlockSpec((1, d), full),
            pl.BlockSpec((d, nz), full),
            pl.BlockSpec((d, LANES), full),
            pl.BlockSpec((ng, d), full),
            pl.BlockSpec((1, LANES), full),
            pl.BlockSpec((ng, 1), full),
        ],
        out_specs=[
            pl.BlockSpec((tm, nz), lambda i: (i, 0)),
            pl.BlockSpec((tm, LANES), lambda i: (i, 0)),
            pl.BlockSpec((ng, tm), lambda i: (0, i)),
        ],
        out_shape=[
            jax.ShapeDtypeStruct((n, nz), F32),
            jax.ShapeDtypeStruct((n, LANES), F32),
            jax.ShapeDtypeStruct((ng, n), F32),
        ],
        compiler_params=_params("parallel"),
        name="inproj",
    )(x_main, x_tail, g, w, wg, wgt, bg, bgt)


def _pool_prompt_kernel(u_ref, pw_ref, ps_ref, y_ref, st_ref, ext_ref, *, tt, gw):
    t = pl.program_id(1)
    nt = pl.num_programs(1)
    halo = POOL_BUF + 1
    width = ext_ref.shape[1]

    @pl.when(t == 0)
    def _():
        ext_ref[0:halo, :] = jnp.zeros((halo, width), F32)

    u = u_ref[...]
    ext_ref[halo:halo + tt, :] = u
    pos = t * tt + lax.broadcasted_iota(I32, (tt, 1), 0)
    for g, w in enumerate(POOL_WINDOWS):
        c0 = g * gw
        acc = u[:, c0:c0 + gw]
        for j in range(1, w):
            acc = acc + ext_ref[halo - j:halo - j + tt, c0:c0 + gw]
        cnt = jnp.minimum(w, pos + 1).astype(F32)
        d = acc / cnt - u[:, c0:c0 + gw]
        y = _dot(d.astype(BF16), pw_ref[g]) * ps_ref[:, c0:c0 + gw]
        y_ref[:, c0:c0 + gw] = y.astype(BF16)

    @pl.when(t == nt - 1)
    def _():
        st_ref[...] = ext_ref[tt + 1:tt + halo, :]

    ext_ref[0:halo, :] = ext_ref[tt:tt + halo, :]


def _pool_prompt(z, pw, ps, batch, seq):
    width = ps.shape[1]
    gw = width // len(POOL_WINDOWS)
    tt = POOL_TIME_TILE
    nt = seq // tt
    return pl.pallas_call(
        functools.partial(_pool_prompt_kernel, tt=tt, gw=gw),
        grid=(batch, nt),
        in_specs=[
            pl.BlockSpec((tt, width), lambda b, t: (b * nt + t, 0)),
            pl.BlockSpec(pw.shape, lambda b, t: (0, 0, 0)),
            pl.BlockSpec((1, width), lambda b, t: (0, 0)),
        ],
        out_specs=[
            pl.BlockSpec((tt, width), lambda b, t: (b * nt + t, 0)),
            pl.BlockSpec((None, POOL_BUF, width), lambda b, t: (b, 0, 0)),
        ],
        out_shape=[
            jax.ShapeDtypeStruct((batch * seq, width), BF16),
            jax.ShapeDtypeStruct((batch, POOL_BUF, width), F32),
        ],
        scratch_shapes=[pltpu.VMEM((POOL_BUF + 1 + tt, width), F32)],
        compiler_params=_params("parallel", "arbitrary"),
        name="pool_prompt",
    )(z, pw, ps)


def _pool_step_kernel(u_ref, buf_ref, pw_ref, ps_ref, y_ref, nb_ref, *, gw):
    u = u_ref[...]
    for g, w in enumerate(POOL_WINDOWS):
        c0 = g * gw
        acc = u[:, c0:c0 + gw]
        for j in range(1, w):
            acc = acc + buf_ref[POOL_BUF - j, :, c0:c0 + gw]
        cnt = float(min(w, PAST_LEN + 1))
        d = acc / cnt - u[:, c0:c0 + gw]
        y = _dot(d.astype(BF16), pw_ref[g]) * ps_ref[:, c0:c0 + gw]
        y_ref[:, c0:c0 + gw] = y.astype(BF16)
    nb_ref[0:POOL_BUF - 1] = buf_ref[1:POOL_BUF]
    nb_ref[POOL_BUF - 1] = u


def _pool_step(z, buf_t, pw, ps, row0):
    _, batch, width = buf_t.shape
    gw = width // len(POOL_WINDOWS)
    return pl.pallas_call(
        functools.partial(_pool_step_kernel, gw=gw),
        grid=(1,),
        in_specs=[
            pl.BlockSpec((batch, width), lambda i: (row0 // batch, 0)),
            pl.BlockSpec(buf_t.shape, lambda i: (0, 0, 0)),
            pl.BlockSpec(pw.shape, lambda i: (0, 0, 0)),
            pl.BlockSpec((1, width), lambda i: (0, 0)),
        ],
        out_specs=[
            pl.BlockSpec((batch, width), lambda i: (0, 0)),
            pl.BlockSpec(buf_t.shape, lambda i: (0, 0, 0)),
        ],
        out_shape=[
            jax.ShapeDtypeStruct((batch, width), BF16),
            jax.ShapeDtypeStruct(buf_t.shape, F32),
        ],
        compiler_params=_params("arbitrary"),
        name="pool_step",
    )(z, buf_t, pw, ps)


def _mlstm_prompt_kernel(q_ref, k_ref, v_ref, o_ref, gt_ref,
                         h_ref, c_out, n_out, m_out, c_s, n_s, m_s, *, nh, dh):
    ci = pl.program_id(1)
    nc = pl.num_programs(1)
    ln = q_ref.shape[0]

    @pl.when(ci == 0)
    def _():
        c_s[...] = jnp.zeros(c_s.shape, F32)
        n_s[...] = jnp.zeros(n_s.shape, F32)
        m_s[...] = jnp.zeros(m_s.shape, F32)

    row = lax.broadcasted_iota(I32, (ln, ln), 0)
    col = lax.broadcasted_iota(I32, (ln, ln), 1)
    causal = col <= row
    eye = col == row

    gt = gt_ref[...]
    li_all = gt[0:nh]
    bc_all = jax.nn.log_sigmoid(gt[nh:2 * nh])
    lane = lax.broadcasted_iota(I32, (nh, ln), 1)
    s = 1
    while s < ln:
        bc_all = bc_all + jnp.where(lane >= s, pltpu.roll(bc_all, s, 1), 0.0)
        s *= 2

    def to_col(r):
        return jnp.sum(jnp.where(eye, r, 0.0), axis=1, keepdims=True)

    scale = dh ** -0.5
    c_old = [c_s[h] for h in range(nh)]
    n_old = [n_s[h] for h in range(nh)]
    m_old = [m_s[h][:, 0:1] for h in range(nh)]
    new_state = []
    for h in range(nh):
        sl = slice(h * dh, (h + 1) * dh)
        q = q_ref[:, sl]
        k = k_ref[:, sl] * scale
        v = v_ref[:, sl]
        qb, kb, vb = q.astype(BF16), k.astype(BF16), v.astype(BF16)
        li_r = li_all[h:h + 1]
        bc_r = bc_all[h:h + 1]
        bc_c = to_col(bc_r)
        m0, c, n = m_old[h], c_old[h], n_old[h]

        dmat = jnp.where(causal, bc_c - bc_r + li_r, -jnp.inf)
        inter = bc_c + m0
        m_row = jnp.maximum(inter, jnp.max(dmat, axis=1, keepdims=True))
        w_intra = jnp.exp(dmat - m_row)
        w_inter = jnp.exp(inter - m_row)
        sc = _dot_nt(qb, kb) * w_intra
        num = _dot(sc.astype(BF16), vb) + w_inter * _dot_nt(qb, c.astype(BF16))
        den = jnp.sum(sc, axis=1, keepdims=True) + w_inter * jnp.sum(q * n, axis=1, keepdims=True)
        hh = num / jnp.maximum(jnp.abs(den), jnp.exp(-m_row))
        h_ref[:, sl] = (hh * jax.nn.sigmoid(o_ref[:, sl])).astype(BF16)

        b_end = bc_r[:, ln - 1:ln]
        g_r = b_end - bc_r + li_r
        m_new = jnp.maximum(b_end + m0, jnp.max(g_r, axis=1, keepdims=True))
        wg_c = jnp.exp(to_col(g_r) - m_new)
        decay = jnp.exp(b_end + m0 - m_new)
        new_state.append((decay * c + _dot_tn((v * wg_c).astype(BF16), kb),
                          decay * n + jnp.sum(wg_c * k, axis=0, keepdims=True),
                          jnp.broadcast_to(m_new, (1, dh))))

    for h, (c_new, n_new, m_new) in enumerate(new_state):
        c_s[h] = c_new
        n_s[h] = n_new
        m_s[h] = m_new

    @pl.when(ci == nc - 1)
    def _():
        c_out[...] = c_s[...]
        n_out[...] = n_s[...]
        m_out[...] = m_s[...]


def _mlstm_prompt(z, gates_t, batch, seq, nh, dh):
    ln = MLSTM_CHUNK
    nc = seq // ln
    width = nh * dh
    ng = gates_t.shape[0]
    blk = lambda j: pl.BlockSpec((ln, width), lambda b, c: (b * nc + c, j))
    return pl.pallas_call(
        functools.partial(_mlstm_prompt_kernel, nh=nh, dh=dh),
        grid=(batch, nc),
        in_specs=[blk(1), blk(2), blk(3), blk(4),
                  pl.BlockSpec((ng, ln), lambda b, c: (0, b * nc + c))],
        out_specs=[
            pl.BlockSpec((ln, width), lambda b, c: (b * nc + c, 0)),
            pl.BlockSpec((None, nh, dh, dh), lambda b, c: (b, 0, 0, 0)),
            pl.BlockSpec((None, nh, 1, dh), lambda b, c: (b, 0, 0, 0)),
            pl.BlockSpec((None, nh, 1, dh), lambda b, c: (b, 0, 0, 0)),
        ],
        out_shape=[
            jax.ShapeDtypeStruct((batch * seq, width), BF16),
            jax.ShapeDtypeStruct((batch, nh, dh, dh), F32),
            jax.ShapeDtypeStruct((batch, nh, 1, dh), F32),
            jax.ShapeDtypeStruct((batch, nh, 1, dh), F32),
        ],
        scratch_shapes=[pltpu.VMEM((nh, dh, dh), F32), pltpu.VMEM((nh, 1, dh), F32),
                        pltpu.VMEM((nh, 1, dh), F32)],
        compiler_params=_params("parallel", "arbitrary"),
        name="mlstm_prompt",
    )(z, z, z, z, gates_t)


def _mlstm_step_kernel(q_ref, k_ref, v_ref, o_ref, li_ref, fp_ref, c_ref, n_ref, m_ref,
                       h_ref, c_out, n_out, m_out, *, nh, dh):
    eye = (lax.broadcasted_iota(I32, (1, dh, dh), 1) == lax.broadcasted_iota(I32, (1, dh, dh), 2))
    scale = dh ** -0.5
    for h in range(nh):
        sl = slice(h * dh, (h + 1) * dh)
        q = q_ref[:, :, sl]
        k = k_ref[:, :, sl] * scale
        v = v_ref[:, :, sl]
        c = c_ref[:, h]
        n = n_ref[:, h]
        m = m_ref[:, h]
        li = li_ref[:, h]
        lf = jax.nn.log_sigmoid(fp_ref[:, h])
        inter = lf + m
        m_row = jnp.maximum(inter, li)
        w_intra = jnp.exp(li - m_row)
        w_inter = jnp.exp(inter - m_row)
        sc = jnp.sum(q * k, axis=-1, keepdims=True) * w_intra
        v_c = jnp.sum(jnp.where(eye, v, 0.0), axis=-1, keepdims=True)
        num = sc * v_c + w_inter * jnp.sum(c * q, axis=-1, keepdims=True)
        den = sc + w_inter * jnp.sum(n * q, axis=-1, keepdims=True)
        h_c = num / jnp.maximum(jnp.abs(den), jnp.exp(-m_row))
        h_l = jnp.sum(jnp.where(eye, h_c, 0.0), axis=1, keepdims=True)
        h_ref[:, :, sl] = h_l * jax.nn.sigmoid(o_ref[:, :, sl])
        wg = jnp.exp(li - m_row)
        decay = jnp.exp(inter - m_row)
        c_out[:, h] = decay * c + (v_c * wg) * k
        n_out[:, h] = decay * n + wg * k
        m_out[:, h] = m_row


def _mlstm_step(z3, li, fp, c, n, m, nh, dh):
    batch = c.shape[0]
    bb = SUBLANES
    width = nh * dh
    blk = lambda j: pl.BlockSpec((bb, 1, width), lambda i: (i, 0, j))
    st4 = lambda a, b: pl.BlockSpec((bb, nh, a, b), lambda i: (i, 0, 0, 0))
    return pl.pallas_call(
        functools.partial(_mlstm_step_kernel, nh=nh, dh=dh),
        grid=(batch // bb,),
        in_specs=[blk(1), blk(2), blk(3), blk(4), st4(1, 1), st4(1, 1),
                  st4(dh, dh), st4(1, dh), st4(1, 1)],
        out_specs=[pl.BlockSpec((bb, 1, width), lambda i: (i, 0, 0)),
                   st4(dh, dh), st4(1, dh), st4(1, 1)],
        out_shape=[
            jax.ShapeDtypeStruct((batch, 1, width), F32),
            jax.ShapeDtypeStruct((batch, nh, dh, dh), F32),
            jax.ShapeDtypeStruct((batch, nh, 1, dh), F32),
            jax.ShapeDtypeStruct((batch, nh, 1, 1), F32),
        ],
        compiler_params=_params("parallel"),
        name="mlstm_step",
    )(z3, z3, z3, z3, li, fp, c, n, m)


def _split_bf16(a):
    hi = a.astype(BF16)
    return hi, (a - hi.astype(F32)).astype(BF16)


def _route(hn, wr_ref, br_ref, tril_ref, carry_ref, rinfo_ref):
    ngrp, epg = MOE_GROUPS, MOE_EXPERTS_PER_GROUP
    h_hi, h_lo = _split_bf16(hn)
    w_hi, w_lo = wr_ref[0], wr_ref[1]
    logits = _dot(h_hi, w_hi) + (_dot(h_lo, w_hi) + _dot(h_hi, w_lo)) + br_ref[...]
    tm = logits.shape[0]
    lane = lax.broadcasted_iota(I32, (tm, LANES), 1)
    neg = -jnp.inf

    def first_max(x):
        mx = jnp.max(x, axis=1, keepdims=True)
        idx = jnp.min(jnp.where(x == mx, lane, LANES), axis=1, keepdims=True)
        return mx, idx

    is_grp = lane < ngrp
    gmax, gsel = first_max(jnp.where(is_grp, logits, neg))
    g_w = 1.0 / jnp.sum(jnp.where(is_grp, jnp.exp(logits - gmax), 0.0), axis=1, keepdims=True)
    lo = ngrp + gsel * epg
    el = jnp.where((lane >= lo) & (lane < lo + epg), logits, neg)
    v1, i1 = first_max(el)
    v2, i2 = first_max(jnp.where(lane == i1, neg, el))
    e2 = jnp.exp(v2 - v1)
    w1 = g_w / (1.0 + e2)
    w2 = g_w * e2 / (1.0 + e2)
    eid1 = i1 - ngrp
    eid2 = i2 - ngrp

    hit1 = lane == eid1
    hit2 = lane == eid2
    onehot = jnp.where(hit1 | hit2, 1.0, 0.0)
    carry = carry_ref[...]
    prefix = _dot(tril_ref[...], onehot.astype(BF16)) + carry
    rank1 = jnp.sum(jnp.where(hit1, prefix, 0.0), axis=1, keepdims=True)
    rank2 = jnp.sum(jnp.where(hit2, prefix, 0.0), axis=1, keepdims=True)
    carry_ref[...] = carry + jnp.sum(onehot, axis=0, keepdims=True)

    cols = (eid1.astype(F32), eid2.astype(F32), w1, w2, rank1, rank2)
    info = jnp.zeros((tm, LANES), F32)
    for j, cval in enumerate(cols):
        info = jnp.where(lane == j, cval, info)
    rinfo_ref[...] = info


def _outproj_route_kernel(xm_ref, xt_ref, pm_ref, pt_ref, mm_ref, mt_ref,
                          wo_ref, g_ref, wr_ref, br_ref, tril_ref,
                          x1_ref, hn_ref, rinfo_ref, cnt_ref, carry_ref):
    @pl.when(pl.program_id(0) == 0)
    def _():
        carry_ref[...] = jnp.zeros(carry_ref.shape, F32)

    half = pm_ref.shape[1]
    mix = (_dot(_stacked(pm_ref, pt_ref), wo_ref[0:half, :])
           + _dot(_stacked(mm_ref, mt_ref), wo_ref[half:2 * half, :]))
    x1 = _stacked(xm_ref, xt_ref) + mix
    x1_ref[...] = x1
    hn = _rms(x1, g_ref[...])
    hn_ref[...] = hn
    _route(hn, wr_ref, br_ref, tril_ref, carry_ref, rinfo_ref)
    cnt_ref[...] = carry_ref[...]


def _glu_route_kernel(x_ref, ym_ref, yt_ref, wglu_ref, g_ref, wr_ref, br_ref, tril_ref,
                      x1_ref, hn_ref, rinfo_ref, cnt_ref, carry_ref):
    @pl.when(pl.program_id(0) == 0)
    def _():
        carry_ref[...] = jnp.zeros(carry_ref.shape, F32)

    d = x_ref.shape[1]
    ag = _dot(jax.nn.gelu(_stacked(ym_ref, yt_ref)).astype(BF16), wglu_ref[...])
    x1 = x_ref[...] + ag[:, 0:d] * jax.nn.sigmoid(ag[:, d:2 * d])
    x1_ref[...] = x1
    hn = _rms(x1, g_ref[...])
    hn_ref[...] = hn
    _route(hn, wr_ref, br_ref, tril_ref, carry_ref, rinfo_ref)
    cnt_ref[...] = carry_ref[...]


def _mix_route(kernel, name, n, row_specs, rows, w, g, wr, br, tril):
    d = g.shape[1]
    tm = TOKEN_TILE
    full = lambda i: (0, 0)
    return pl.pallas_call(
        kernel,
        grid=(n // tm,),
        in_specs=row_specs + [
            pl.BlockSpec(w.shape, full),
            pl.BlockSpec((1, d), full),
            pl.BlockSpec((2, d, LANES), lambda i: (0, 0, 0)),
            pl.BlockSpec((1, LANES), full),
            pl.BlockSpec((tm, tm), full),
        ],
        out_specs=[
            pl.BlockSpec((tm, d), lambda i: (i, 0)),
            pl.BlockSpec((tm, d), lambda i: (i, 0)),
            pl.BlockSpec((tm, LANES), lambda i: (i, 0)),
            pl.BlockSpec((1, LANES), full),
        ],
        out_shape=[
            jax.ShapeDtypeStruct((n, d), F32),
            jax.ShapeDtypeStruct((n, d), F32),
            jax.ShapeDtypeStruct((n, LANES), F32),
            jax.ShapeDtypeStruct((1, LANES), F32),
        ],
        scratch_shapes=[pltpu.VMEM((1, LANES), F32)],
        compiler_params=_params("arbitrary"),
        name=name,
    )(*rows, w, g, wr, br, tril)


def _index_copy(pos_hbm, idx_s, sem_i, tile, slot):
    return pltpu.make_async_copy(pos_hbm.at[tile], idx_s.at[slot], sem_i.at[slot])


def _dispatch_kernel(pos_hbm, hn_ref, xs_hbm, idx_s, sem_i, sem_d):
    i = pl.program_id(0)
    nt = pl.num_programs(0)
    tm = hn_ref.shape[0]
    slot = i % 2

    @pl.when(i == 0)
    def _():
        _index_copy(pos_hbm, idx_s, sem_i, 0, 0).start()

    _index_copy(pos_hbm, idx_s, sem_i, i, slot).wait()

    @pl.when(i + 1 < nt)
    def _():
        _index_copy(pos_hbm, idx_s, sem_i, i + 1, 1 - slot).start()

    def issue(r, carry):
        row = hn_ref.at[pl.ds(r, 1)]
        pltpu.make_async_copy(row, xs_hbm.at[pl.ds(idx_s[slot, 0, r], 1)], sem_d).start(priority=0)
        pltpu.make_async_copy(row, xs_hbm.at[pl.ds(idx_s[slot, 0, tm + r], 1)], sem_d).start(priority=1)
        return carry

    lax.fori_loop(0, tm, issue, 0, unroll=DMA_ISSUE_UNROLL)
    whole = pltpu.make_async_copy(hn_ref, xs_hbm.at[pl.ds(0, tm)], sem_d)
    whole.wait()
    whole.wait()


def _dispatch(pos_tiles, hn):
    n, d = hn.shape
    tm = TOKEN_TILE
    return pl.pallas_call(
        _dispatch_kernel,
        grid=(n // tm,),
        in_specs=[pl.BlockSpec(memory_space=pl.ANY), pl.BlockSpec((tm, d), lambda i: (i, 0))],
        out_specs=pl.BlockSpec(memory_space=pl.ANY),
        out_shape=jax.ShapeDtypeStruct((2 * n, d), F32),
        scratch_shapes=[pltpu.SMEM((2, 1, 2 * tm), I32), pltpu.SemaphoreType.DMA((2,)),
                        pltpu.SemaphoreType.DMA],
        compiler_params=_params("arbitrary"),
        name="moe_dispatch",
    )(pos_tiles, hn)


def _moe_kernel(vt_ref, ve_ref, von_ref, lo_ref, hi_ref, xs_ref, wg_ref, wu_ref, wd_ref,
                eo_ref, wgb, wub, wdb):
    v = pl.program_id(0)
    tr = xs_ref.shape[0]
    prev = jnp.maximum(v - 1, 0)
    e = ve_ref[v]
    new_expert = jnp.logical_or(v == 0, e != ve_ref[prev])
    first_visit = jnp.logical_or(v == 0, vt_ref[v] != vt_ref[prev])

    @pl.when(von_ref[v] == 1)
    def _():
        @pl.when(new_expert)
        def _():
            wgb[...] = wg_ref[...].astype(BF16)
            wub[...] = wu_ref[...].astype(BF16)
            wdb[...] = wd_ref[...].astype(BF16)

        x = xs_ref[...].astype(BF16)
        act = jax.nn.silu(_dot(x, wgb[...])) * _dot(x, wub[...])
        row = vt_ref[v] * tr + lax.broadcasted_iota(I32, (tr, 1), 0)
        act = jnp.where((row >= lo_ref[e]) & (row < hi_ref[e]), act, 0.0)
        res = _dot(act.astype(BF16), wdb[...])

        @pl.when(first_visit)
        def _():
            eo_ref[...] = res

        @pl.when(jnp.logical_not(first_visit))
        def _():
            eo_ref[...] += res


def _moe_experts(vt, ve, von, lo, hi, xs, wg, wu, wd, layer):
    nv = vt.shape[0]
    rows, d = xs.shape
    hid = wg.shape[3]
    tr = EXPERT_ROW_TILE
    wspec = lambda a, b: pl.BlockSpec((None, None, a, b),
                                      lambda v, vt, ve, von, lo, hi: (layer, ve[v], 0, 0))
    grid_spec = pltpu.PrefetchScalarGridSpec(
        num_scalar_prefetch=5,
        grid=(nv,),
        in_specs=[
            pl.BlockSpec((tr, d), lambda v, vt, ve, von, lo, hi: (vt[v], 0)),
            wspec(d, hid), wspec(d, hid), wspec(hid, d),
        ],
        out_specs=pl.BlockSpec((tr, d), lambda v, vt, ve, von, lo, hi: (vt[v], 0)),
        scratch_shapes=[pltpu.VMEM((d, hid), BF16), pltpu.VMEM((d, hid), BF16),
                        pltpu.VMEM((hid, d), BF16)],
    )
    return pl.pallas_call(
        _moe_kernel,
        grid_spec=grid_spec,
        out_shape=jax.ShapeDtypeStruct((rows, d), F32),
        compiler_params=_params("arbitrary"),
        name="moe_experts",
    )(vt, ve, von, lo, hi, xs, wg, wu, wd)


def _combine_kernel(pos_hbm, eo_hbm, x_ref, rinfo_ref, g_ref, o1_ref, o2_ref, idx_s, a_buf, b_buf,
                    sem_i, sem_a, sem_b, *, last_layer):
    i = pl.program_id(0)
    nt = pl.num_programs(0)
    tm = a_buf.shape[1]
    slot = i % 2

    def gathers(s):
        def issue(r, carry):
            pltpu.make_async_copy(eo_hbm.at[pl.ds(idx_s[s, 0, r], 1)],
                                  a_buf.at[s, pl.ds(r, 1)], sem_a.at[s]).start(priority=0)
            pltpu.make_async_copy(eo_hbm.at[pl.ds(idx_s[s, 0, tm + r], 1)],
                                  b_buf.at[s, pl.ds(r, 1)], sem_b.at[s]).start(priority=1)
            return carry

        lax.fori_loop(0, tm, issue, 0, unroll=DMA_ISSUE_UNROLL)

    @pl.when(i == 0)
    def _():
        first = _index_copy(pos_hbm, idx_s, sem_i, 0, 0)
        first.start()
        first.wait()
        gathers(0)

        @pl.when(nt > 1)
        def _():
            _index_copy(pos_hbm, idx_s, sem_i, 1, 1).start()

    @pl.when(i + 1 < nt)
    def _():
        _index_copy(pos_hbm, idx_s, sem_i, i + 1, 1 - slot).wait()
        gathers(1 - slot)

    @pl.when(i + 2 < nt)
    def _():
        _index_copy(pos_hbm, idx_s, sem_i, i + 2, slot).start()

    pltpu.make_async_copy(eo_hbm.at[pl.ds(0, tm)], a_buf.at[slot], sem_a.at[slot]).wait()
    pltpu.make_async_copy(eo_hbm.at[pl.ds(0, tm)], b_buf.at[slot], sem_b.at[slot]).wait()
    info = rinfo_ref[...]
    x2 = x_ref[...] + (info[:, 2:3] * a_buf[slot] + info[:, 3:4] * b_buf[slot])
    hn = _rms(x2, g_ref[...])
    if last_layer:
        @pl.when(i < nt - 1)
        def _():
            o1_ref[...] = hn

        @pl.when(i == nt - 1)
        def _():
            o2_ref[...] = hn
    else:
        o1_ref[...] = x2
        o2_ref[...] = hn


def _combine(pos_tiles, eo, x, rinfo, g, last_layer):
    n, d = x.shape
    tm = TOKEN_TILE
    nt = n // tm
    row = pl.BlockSpec((tm, d), lambda i: (i, 0))
    if last_layer:
        out_specs = [pl.BlockSpec((tm, d), lambda i: (jnp.minimum(i, nt - 2), 0)),
                     pl.BlockSpec((tm, d), lambda i: (0, 0))]
        out_shape = [jax.ShapeDtypeStruct((n - tm, d), F32), jax.ShapeDtypeStruct((tm, d), F32)]
    else:
        out_specs = [row, row]
        out_shape = [jax.ShapeDtypeStruct((n, d), F32), jax.ShapeDtypeStruct((n, d), F32)]
    return pl.pallas_call(
        functools.partial(_combine_kernel, last_layer=last_layer),
        grid=(nt,),
        in_specs=[
            pl.BlockSpec(memory_space=pl.ANY),
            pl.BlockSpec(memory_space=pl.ANY),
            row,
            pl.BlockSpec((tm, LANES), lambda i: (i, 0)),
            pl.BlockSpec((1, d), lambda i: (0, 0)),
        ],
        out_specs=out_specs,
        out_shape=out_shape,
        scratch_shapes=[
            pltpu.SMEM((2, 1, 2 * tm), I32),
            pltpu.VMEM((2, tm, d), F32),
            pltpu.VMEM((2, tm, d), F32),
            pltpu.SemaphoreType.DMA((2,)),
            pltpu.SemaphoreType.DMA((2,)),
            pltpu.SemaphoreType.DMA((2,)),
        ],
        compiler_params=_params("arbitrary"),
        name="moe_combine",
    )(pos_tiles, eo, x, rinfo, g)


def _moe(x1, hn, rinfo, counts, wg, wu, wd, layer, next_gain, last_layer):
    n = x1.shape[0]
    ne = wg.shape[1]
    tr = EXPERT_ROW_TILE
    tm = TOKEN_TILE
    nv = (2 * n) // tr + ne - 1
    eid = rinfo[:, 0:2].astype(I32)
    rank = rinfo[:, 4:6].astype(I32)
    cnt = counts[0, :ne].astype(I32)
    seg_end = jnp.cumsum(cnt)
    seg_start = seg_end - cnt
    experts = jnp.arange(ne, dtype=I32)
    pos = jnp.sum(jnp.where(eid[:, :, None] == experts, seg_start, 0), axis=-1) + rank
    pos_tiles = pos.reshape(n // tm, tm, 2).transpose(0, 2, 1).reshape(n // tm, 1, 2 * tm)

    first_tile = seg_start // tr
    tiles_e = jnp.where(cnt > 0, (seg_end - 1) // tr - first_tile + 1, 0)
    v_end = jnp.cumsum(tiles_e)
    v_start = v_end - tiles_e
    total = v_end[-1]
    vis = jnp.arange(nv, dtype=I32)
    vc = jnp.minimum(vis, jnp.maximum(total - 1, 0))
    ve = jnp.sum((vc[:, None] >= v_end[None, :]).astype(I32), axis=1)
    pick = lambda tab: jnp.sum(jnp.where(ve[:, None] == experts, tab, 0), axis=1)
    vt = pick(first_tile) + (vc - pick(v_start))
    von = (vis < total).astype(I32)

    xs = _dispatch(pos_tiles, hn)
    eo = _moe_experts(vt, ve, von, seg_start, seg_end, xs, wg, wu, wd, layer)
    return _combine(pos_tiles, eo, x1, rinfo, next_gain, last_layer)


def _cis(log_mag, ang):
    mag = jnp.exp(log_mag)
    return mag * jnp.cos(ang), mag * jnp.sin(ang)


def _dot_hi(a, b):
    return jnp.dot(a, b, precision=HIGHEST, preferred_element_type=F32)


def _s5_prep_kernel(lsc_ref, lsr_ref, lrc_ref, lic_ref, lrr_ref, lir_ref, ctre_ref, ctim_ref,
                    btre_ref, btim_ref,
                    t_ref, wre_ref, wim_ref, cyre_ref, cyim_ref, apr_ref, api_ref,
                    bbr_ref, bbi_ref, lbr_ref, lbi_ref, *, gw, gpb, sub):
    blk = gpb * gw
    wide = sub * blk
    nsl = lrc_ref.shape[1]
    ns = nsl // gpb
    sh_blk, sh_gw, sh_ns = blk.bit_length() - 1, gw.bit_length() - 1, ns.bit_length() - 1
    dt_c, dt_r = jnp.exp(lsc_ref[0]), jnp.exp(lsr_ref[0])
    ldt_c_re, ldt_c_im = lrc_ref[0] * dt_c, lic_ref[0] * dt_c
    ldt_r_re, ldt_r_im = lrr_ref[0] * dt_r, lir_ref[0] * dt_r

    lane_w = lax.broadcasted_iota(I32, (1, wide), 1)
    spread = jnp.where((lax.broadcasted_iota(I32, (gw, wide), 1) & (gw - 1))
                       == lax.broadcasted_iota(I32, (gw, wide), 0), 1.0, 0.0)
    cre = _dot_hi(ctre_ref[0], spread)
    cim = _dot_hi(ctim_ref[0], spread)
    same = (jnp.right_shift(lax.broadcasted_iota(I32, (nsl, 1), 0), sh_ns)
            == (jnp.right_shift(lane_w, sh_gw) & (gpb - 1)))
    assert 2 * blk == LANES
    tau = lax.broadcasted_iota(I32, (1, LANES), 1).astype(F32)
    pw_re, pw_im = _cis(tau * ldt_c_re, tau * ldt_c_im)
    low = lax.broadcasted_iota(I32, (nsl, LANES), 1) < blk

    def spread_pow(p, first):
        col = lambda t: jnp.broadcast_to(p[:, t:t + 1], (nsl, LANES))
        return jnp.concatenate([jnp.where(low, col(first + 2 * m), col(first + 2 * m + 1))
                                for m in range(sub // 2)], axis=1)

    def c_lam_pow(first):
        pr, pi = spread_pow(pw_re, first), spread_pow(pw_im, first)
        return (jnp.where(same, pr * cre - pi * cim, 0.0),
                jnp.where(same, -(pr * cim + pi * cre), 0.0))

    clr0, cli0 = c_lam_pow(0)
    clr1, cli1 = c_lam_pow(1)
    cyre_ref[0] = clr1.astype(BF16)
    cyim_ref[0] = cli1.astype(BF16)

    lbr, lbi = _cis(ldt_r_re, ldt_r_im)
    lbr_ref[0] = lbr
    lbi_ref[0] = lbi
    lr, li = lrr_ref[0], lir_ref[0]
    nr, ni = lbr - 1.0, lbi
    den = lr * lr + li * li
    fr = (nr * lr + ni * li) / den
    fi = (ni * lr - nr * li) / den
    bre, bim = btre_ref[0], btim_ref[0]
    bbr = fr * bre - fi * bim
    bbi = fr * bim + fi * bre
    bbr_ref[0] = bbr
    bbi_ref[0] = bbi

    lane_t = lax.broadcasted_iota(I32, (gw, wide), 1)
    for g in range(gpb):
        ps = slice(g * ns, (g + 1) * ns)
        r = _dot_hi(bbr[:, ps], clr0[ps, :]) + _dot_hi(bbi[:, ps], cli0[ps, :])
        for j in range(sub):
            tb = r if j == 0 else jnp.where(lane_t >= blk * j, pltpu.roll(r, blk * j, 1), 0.0)
            r0 = j * blk + g * gw
            t_ref[0, r0:r0 + gw, :] = tb.astype(BF16)

    rows = lax.broadcasted_iota(I32, (wide, 1), 0)
    spread_t = jnp.where((lax.broadcasted_iota(I32, (wide, gw), 0) & (gw - 1))
                         == lax.broadcasted_iota(I32, (wide, gw), 1), 1.0, 0.0)
    bbr_t = _dot_hi(spread_t, bbr)
    bbi_t = _dot_hi(spread_t, bbi)
    same_w = ((jnp.right_shift(rows, sh_gw) & (gpb - 1))
              == jnp.right_shift(lax.broadcasted_iota(I32, (1, nsl), 1), sh_ns))
    rj = ((sub - 1) - lax.broadcasted_iota(I32, (sub, 1), 0)).astype(F32)
    q_re, q_im = _cis(rj * ldt_r_re, rj * ldt_r_im)
    per_step = lambda q: jnp.concatenate(
        [jnp.broadcast_to(q[j:j + 1, :], (blk, nsl)) for j in range(sub)], axis=0)
    pr, pi = per_step(q_re), per_step(q_im)
    wre_ref[0] = jnp.where(same_w, pr * bbr_t - pi * bbi_t, 0.0).astype(BF16)
    wim_ref[0] = jnp.where(same_w, pr * bbi_t + pi * bbr_t, 0.0).astype(BF16)

    nlev = apr_ref.shape[1]
    pw = (sub * jnp.left_shift(1, lax.broadcasted_iota(I32, (nlev, 1), 0))).astype(F32)
    apr, api = _cis(pw * ldt_r_re, pw * ldt_r_im)
    apr_ref[0] = apr
    api_ref[0] = api


def _s5_prep(log_step, lam_re, lam_im, b_re, b_im, c_re, c_im):
    ng, ns = lam_re.shape
    gw = b_re.shape[2]
    sub, gpb = S5_SUB, S5_GROUPS_PER_BLOCK
    nblk = ng // gpb
    nsl = gpb * ns
    wide = sub * gpb * gw
    ls = jnp.repeat(log_step, ns)
    ct = lambda c: jnp.transpose(c, (0, 2, 1)).reshape(nblk, nsl, gw)
    bt = lambda b: jnp.transpose(b.reshape(nblk, gpb, ns, gw), (0, 3, 1, 2)).reshape(nblk, gw, nsl)
    ins = [ls.reshape(nblk, nsl, 1), ls.reshape(nblk, 1, nsl),
           lam_re.reshape(nblk, nsl, 1), lam_im.reshape(nblk, nsl, 1),
           lam_re.reshape(nblk, 1, nsl), lam_im.reshape(nblk, 1, nsl),
           ct(c_re), ct(c_im), bt(b_re), bt(b_im)]
    spec = lambda a: pl.BlockSpec((1,) + a.shape[1:], lambda g: (g, 0, 0))
    outs = [
        jax.ShapeDtypeStruct((nblk, wide, wide), BF16),
        jax.ShapeDtypeStruct((nblk, wide, nsl), BF16),
        jax.ShapeDtypeStruct((nblk, wide, nsl), BF16),
        jax.ShapeDtypeStruct((nblk, nsl, wide), BF16),
        jax.ShapeDtypeStruct((nblk, nsl, wide), BF16),
        jax.ShapeDtypeStruct((nblk, SUBLANES, nsl), F32),
        jax.ShapeDtypeStruct((nblk, SUBLANES, nsl), F32),
        jax.ShapeDtypeStruct((nblk, gw, nsl), F32),
        jax.ShapeDtypeStruct((nblk, gw, nsl), F32),
        jax.ShapeDtypeStruct((nblk, 1, nsl), F32),
        jax.ShapeDtypeStruct((nblk, 1, nsl), F32),
    ]
    return pl.pallas_call(
        functools.partial(_s5_prep_kernel, gw=gw, gpb=gpb, sub=sub),
        grid=(nblk,),
        in_specs=[spec(a) for a in ins],
        out_specs=[spec(o) for o in outs],
        out_shape=outs,
        compiler_params=_params("parallel"),
        name="s5_prep",
    )(*ins)


def _s5_prompt_kernel(x_ref, t_ref, wre_ref, wim_ref, cyre_ref, cyim_ref, apr_ref, api_ref, d_ref,
                      y_ref, hre_out, him_out, u_s, y4_s, sre_s, sim_s, *, nb, seq, sub):
    nk = seq // sub
    rows = nb * nk
    hl = LANES // 2
    pad = nk // 2
    nlev = nk.bit_length() - 1
    nsl = sre_s.shape[1]
    low = lax.broadcasted_iota(I32, (nk, LANES), 1) < hl
    kidx = lax.broadcasted_iota(I32, (rows, 1), 0) & (nk - 1)
    sre_s[0:pad, :] = jnp.zeros((pad, nsl), F32)
    sim_s[0:pad, :] = jnp.zeros((pad, nsl), F32)

    def step_rows(b, j):
        return (pl.ds(b * seq + j, nk, stride=sub), slice(None))

    for b in range(nb):
        rs = slice(b * nk, (b + 1) * nk)
        for m in range(sub // 2):
            ls = slice(m * LANES, (m + 1) * LANES)
            s0 = x_ref[step_rows(b, 2 * m)]
            s1 = x_ref[step_rows(b, 2 * m + 1)]
            u_s[0, rs, ls] = jnp.where(low, s0, pltpu.roll(s1, hl, 1)).astype(BF16)
            u_s[1, rs, ls] = jnp.where(low, pltpu.roll(s0, hl, 1), s1).astype(BF16)

    live = slice(pad, pad + rows)
    for half in range(2):
        u = u_s[half]
        hre = _dot(u, wre_ref[half])
        him = _dot(u, wim_ref[half])
        for i in range(nlev):
            s = 1 << i
            sre_s[live, :] = hre
            sim_s[live, :] = him
            pre = sre_s[pad - s:pad - s + rows, :]
            pim = sim_s[pad - s:pad - s + rows, :]
            ar = apr_ref[half, i:i + 1, :]
            ai = api_ref[half, i:i + 1, :]
            ok = kidx >= s
            hre = hre + jnp.where(ok, ar * pre - ai * pim, 0.0)
            him = him + jnp.where(ok, ar * pim + ai * pre, 0.0)
        sre_s[live, :] = hre
        sim_s[live, :] = him
        ls = slice(half * nsl, (half + 1) * nsl)
        for b in range(nb):
            last = pad + (b + 1) * nk - 1
            hre_out[b:b + 1, ls] = sre_s[last:last + 1, :]
            him_out[b:b + 1, ls] = sim_s[last:last + 1, :]
        started = kidx >= 1
        hpre = jnp.where(started, sre_s[pad - 1:pad - 1 + rows, :], 0.0).astype(BF16)
        hpim = jnp.where(started, sim_s[pad - 1:pad - 1 + rows, :], 0.0).astype(BF16)
        y4_s[half] = _dot(u, t_ref[half]) + _dot(hpre, cyre_ref[half]) + _dot(hpim, cyim_ref[half])

    d = d_ref[...]
    for b in range(nb):
        rs = slice(b * nk, (b + 1) * nk)
        for m in range(sub // 2):
            ls = slice(m * LANES, (m + 1) * LANES)
            ca = y4_s[0, rs, ls]
            cb = y4_s[1, rs, ls]
            y0 = jnp.where(low, ca, pltpu.roll(cb, hl, 1))
            y1 = jnp.where(low, pltpu.roll(ca, hl, 1), cb)
            y_ref[step_rows(b, 2 * m)] = y0 + d * x_ref[step_rows(b, 2 * m)]
            y_ref[step_rows(b, 2 * m + 1)] = y1 + d * x_ref[step_rows(b, 2 * m + 1)]


def _s5_prompt(h, prep, d_row, batch, seq):
    d = h.shape[1]
    tmat, wre, wim, cyre, cyim, apr, api = prep[:7]
    nblk, wide, nsl = wre.shape
    sub, nb = S5_SUB, S5_BATCH_PER_STEP
    nk = seq // sub
    assert nk & (nk - 1) == 0 and nk.bit_length() - 1 <= apr.shape[1]
    ntile = d // LANES
    assert nblk == 2 * ntile
    rows = nb * nk
    wspec = lambda a: pl.BlockSpec((2,) + a.shape[1:], lambda t, b: (t, 0, 0))
    xspec = pl.BlockSpec((nb * seq, LANES), lambda t, b: (b, t))
    sspec = pl.BlockSpec((None, nb, 2 * nsl), lambda t, b: (b, 0, t))
    return pl.pallas_call(
        functools.partial(_s5_prompt_kernel, nb=nb, seq=seq, sub=sub),
        grid=(ntile, batch // nb),
        in_specs=[xspec, wspec(tmat), wspec(wre), wspec(wim), wspec(cyre), wspec(cyim),
                  wspec(apr), wspec(api), pl.BlockSpec((1, LANES), lambda t, b: (0, t))],
        out_specs=[xspec, sspec, sspec],
        out_shape=[
            jax.ShapeDtypeStruct((batch * seq, d), F32),
            jax.ShapeDtypeStruct((batch // nb, nb, ntile * 2 * nsl), F32),
            jax.ShapeDtypeStruct((batch // nb, nb, ntile * 2 * nsl), F32),
        ],
        scratch_shapes=[
            pltpu.VMEM((2, rows, wide), BF16),
            pltpu.VMEM((2, rows, wide), F32),
            pltpu.VMEM((nk // 2 + rows, nsl), F32),
            pltpu.VMEM((nk // 2 + rows, nsl), F32),
        ],
        compiler_params=_params("parallel", "parallel"),
        name="s5_prompt",
    )(h, tmat, wre, wim, cyre, cyim, apr, api, d_row)


def _s5_step_kernel(u_ref, h_ref, hsw_ref, bb_ref, la_ref, lb_ref, cc_ref, d_ref, hn_ref, y_ref):
    for g in range(u_ref.shape[0]):
        u = u_ref[g]
        hn = (la_ref[g] * h_ref[g] + lb_ref[g] * hsw_ref[g]
              + jnp.dot(u, bb_ref[g], precision=HIGHEST, preferred_element_type=F32))
        hn_ref[g] = hn
        y_ref[g] = (jnp.dot(hn, cc_ref[g], precision=HIGHEST, preferred_element_type=F32)
                    + d_ref[g] * u)


def _s5_step(u_p, h_cat, h_swp, bb_cat, la, lb, cc, d_p):
    ng = u_p.shape[0]
    gb = SUBLANES
    spec = lambda a: pl.BlockSpec((gb,) + a.shape[1:], lambda g: (g, 0, 0))
    ins = [u_p, h_cat, h_swp, bb_cat, la, lb, cc, d_p]
    outs = [jax.ShapeDtypeStruct(h_cat.shape, F32), jax.ShapeDtypeStruct(u_p.shape, F32)]
    return pl.pallas_call(
        _s5_step_kernel,
        grid=(ng // gb,),
        in_specs=[spec(a) for a in ins],
        out_specs=[spec(o) for o in outs],
        out_shape=outs,
        compiler_params=_params("parallel"),
        name="s5_step",
    )(*ins)


def _pad_to(a, axis, size):
    pad = [(0, 0)] * a.ndim
    pad[axis] = (0, size - a.shape[axis])
    return jnp.pad(a, pad)


def kernel(x_prompt, x_sample, state_pool, state_mlstm_c, state_mlstm_n, state_mlstm_m, state_s5_re, state_s5_im, norm_mix, norm_ffn, norm_final, w_in_ab, b_gates, pool_w, pool_scale, w_out_ab, s5_lam_re, s5_lam_im, s5_log_step, s5_b_re, s5_b_im, s5_c_re, s5_c_im, s5_d, w_glu, moe_w_group, moe_b_group, moe_w_expert, moe_b_expert, moe_w_gate, moe_w_up, moe_w_down):
    bp, tp, d = x_prompt.shape
    bs = x_sample.shape[0]
    n_p = bp * tp
    tm = TOKEN_TILE
    assert n_p % tm == 0 and bs <= tm
    n_fill = tm - bs
    nh, dh = state_mlstm_c.shape[2], state_mlstm_c.shape[3]
    pool_width = state_pool.shape[3]
    ml_width = nh * dh
    n_main = pool_width + 4 * ml_width
    n_gates = 2 * nh
    ngrp_s5, n_state = s5_lam_re.shape[1], s5_lam_re.shape[2]
    gw_s5 = d // ngrp_s5

    n = n_p + tm

    def tail_tile(sample_rows, dtype):
        return _pad_to(sample_rows.astype(dtype), 0, tm)

    x_main = x_prompt.reshape(n_p, d)
    x_tail = tail_tile(x_sample.reshape(bs, d), F32)
    tril = jnp.tril(jnp.ones((tm, tm), BF16), -1)

    def router_weights(l):
        wr = _pad_to(jnp.concatenate([moe_w_group[l], moe_w_expert[l]], axis=1), 1, LANES)
        br = jnp.concatenate([moe_b_group[l], moe_b_expert[l]])[None, :]
        hi = wr.astype(BF16)
        lo = (wr - hi.astype(F32)).astype(BF16)
        return jnp.stack([hi, lo]), _pad_to(br, 1, LANES)

    w_in = w_in_ab[0]
    w_g = w_in[:, n_main:]
    z, gates, gates_t = _inproj(
        x_main, x_tail, norm_mix[0][None, :], w_in[:, :n_main].astype(BF16),
        _pad_to(w_g, 1, LANES).astype(BF16), w_g.T.astype(BF16),
        _pad_to(b_gates[0][None, :], 1, LANES), b_gates[0][:, None])

    pw = pool_w[0].astype(BF16)
    ps = pool_scale[0][None, :]
    pool_y_p, pool_p = _pool_prompt(z, pw, ps, bp, tp)
    pool_y_s, pool_s_t = _pool_step(z, jnp.transpose(state_pool[0], (1, 0, 2)), pw, ps, n_p)
    pool_s = jnp.transpose(pool_s_t, (1, 0, 2))

    ml_y_p, c_p, n_p_st, m_p = _mlstm_prompt(z, gates_t, bp, tp, nh, dh)
    g_s = gates[n_p:n_p + bs, :n_gates]
    ml_y_s, c_s, n_s_st, m_s = _mlstm_step(
        z[n_p:n_p + bs].reshape(bs, 1, n_main),
        g_s[:, :nh].reshape(bs, nh, 1, 1), g_s[:, nh:].reshape(bs, nh, 1, 1),
        state_mlstm_c[0], state_mlstm_n[0].reshape(bs, nh, 1, dh),
        state_mlstm_m[0].reshape(bs, nh, 1, 1), nh, dh)

    rows = [x_main, x_tail, pool_y_p, tail_tile(pool_y_s, BF16),
            ml_y_p, tail_tile(ml_y_s.reshape(bs, ml_width), BF16)]
    specs = (_stacked_specs(x_main, x_tail) + _stacked_specs(pool_y_p, rows[3])
             + _stacked_specs(ml_y_p, rows[5]))
    wr, br = router_weights(0)
    x1, hn, rinfo, counts = _mix_route(
        _outproj_route_kernel, "outproj_route", n, specs, rows, w_out_ab[0].astype(BF16),
        norm_ffn[0][None, :], wr, br, tril)
    x2, h1 = _moe(x1, hn, rinfo, counts, moe_w_gate, moe_w_up, moe_w_down, 0,
                  norm_mix[1][None, :], last_layer=False)

    prep = _s5_prep(s5_log_step[0], s5_lam_re[0], s5_lam_im[0], s5_b_re[0], s5_b_im[0],
                    s5_c_re[0], s5_c_im[0])
    y_p, hre_p, him_p = _s5_prompt(h1, prep, s5_d[0][None, :], bp, tp)
    s5_re_p = hre_p.reshape(bp, ngrp_s5, n_state)
    s5_im_p = him_p.reshape(bp, ngrp_s5, n_state)

    gpb = S5_GROUPS_PER_BLOCK
    per_group = lambda a: jnp.transpose(
        a.reshape(ngrp_s5 // gpb, gw_s5, gpb, n_state), (0, 2, 1, 3)).reshape(ngrp_s5, gw_s5, n_state)
    bbr, bbi = per_group(prep[7]), per_group(prep[8])
    lbr, lbi = prep[9].reshape(ngrp_s5, 1, n_state), prep[10].reshape(ngrp_s5, 1, n_state)
    d_g = s5_d[0].reshape(ngrp_s5, 1, gw_s5)
    u_s = h1[n_p:n_p + bs].reshape(bs, ngrp_s5, gw_s5).transpose(1, 0, 2)
    h_re = jnp.transpose(state_s5_re[0], (1, 0, 2))
    h_im = jnp.transpose(state_s5_im[0], (1, 0, 2))
    cc = jnp.concatenate([jnp.transpose(s5_c_re[0], (0, 2, 1)),
                          -jnp.transpose(s5_c_im[0], (0, 2, 1))], axis=1)
    hn_s, y_s = _s5_step(
        _pad_to(u_s, 2, LANES),
        jnp.concatenate([h_re, h_im], axis=2), jnp.concatenate([h_im, h_re], axis=2),
        _pad_to(jnp.concatenate([bbr, bbi], axis=2), 1, LANES),
        jnp.concatenate([lbr, lbr], axis=2), jnp.concatenate([-lbi, lbi], axis=2),
        _pad_to(cc, 2, LANES), _pad_to(d_g, 2, LANES))
    s5_re_s = jnp.transpose(hn_s[:, :, :n_state], (1, 0, 2))
    s5_im_s = jnp.transpose(hn_s[:, :, n_state:], (1, 0, 2))
    y_tail = tail_tile(jnp.transpose(y_s[:, :, :gw_s5], (1, 0, 2)).reshape(bs, d), F32)

    wr, br = router_weights(1)
    specs = [pl.BlockSpec((tm, d), lambda i: (i, 0))] + _stacked_specs(y_p, y_tail)
    x3, hn, rinfo, counts = _mix_route(
        _glu_route_kernel, "glu_route", n, specs, [x2, y_p, y_tail], w_glu[0].astype(BF16),
        norm_ffn[1][None, :], wr, br, tril)
    y_main, y_last = _moe(x3, hn, rinfo, counts, moe_w_gate, moe_w_up, moe_w_down, 1,
                          norm_final[None, :], last_layer=True)

    return (y_main.reshape(bp, tp, d), y_last[:bs].reshape(bs, 1, d),
            pool_p[None], c_p[None], n_p_st.reshape(1, bp, nh, dh), m_p[:, :, 0, 0][None],
            s5_re_p[None], s5_im_p[None],
            pool_s[None], c_s[None], n_s_st.reshape(1, bs, nh, dh), m_s.reshape(1, bs, nh),
            s5_re_s[None], s5_im_s[None])
```

```python
import functools

import jax
import jax.numpy as jnp
from jax import lax
from jax.experimental import pallas as pl
from jax.experimental.pallas import tpu as pltpu

F32 = jnp.float32
BF16 = jnp.bfloat16
I32 = jnp.int32

PAST_LEN = 16384
POOL_WINDOWS = (2, 4, 8, 16)
POOL_BUF = max(POOL_WINDOWS) - 1
MLSTM_CHUNK = 128
S5_SUB = 16
MOE_GROUPS = 4
MOE_EXPERTS_PER_GROUP = 8
RMS_EPS = 1e-6

LANES = 128
SUBLANES = 8
VMEM_LIMIT_BYTES = 56 * 1024 * 1024

TOKEN_TILE = 512
EXPERT_ROW_TILE = 512
MLSTM_SEQ_PER_STEP = 2
MLSTM_STEP_BATCH = 16
POOL_TIME_TILE = 512
DMA_ISSUE_UNROLL = 8
S5_GROUPS_PER_BLOCK = 4
S5_BATCH_PER_STEP = 4

HIGHEST = lax.Precision.HIGHEST


def _params(*sem):
    return pltpu.CompilerParams(dimension_semantics=sem, vmem_limit_bytes=VMEM_LIMIT_BYTES)


def _rms(x, g):
    return x * lax.rsqrt(jnp.mean(x * x, axis=-1, keepdims=True) + RMS_EPS) * g


def _dot(a, b):
    return jnp.dot(a, b, preferred_element_type=F32)


def _dot_nt(a, b):
    return lax.dot_general(a, b, (((1,), (1,)), ((), ())), preferred_element_type=F32)


def _dot_tn(a, b):
    return lax.dot_general(a, b, (((0,), (0,)), ((), ())), preferred_element_type=F32)


def _stacked(main_ref, tail_ref):
    last = pl.program_id(0) == pl.num_programs(0) - 1
    return jnp.where(last, tail_ref[...], main_ref[...])


def _stacked_specs(main, tail):
    tm, w = tail.shape
    last_main = main.shape[0] // tm - 1
    return [pl.BlockSpec((tm, w), lambda i: (jnp.minimum(i, last_main), 0)),
            pl.BlockSpec((tm, w), lambda i: (0, 0))]


def _inproj_kernel(xm_ref, xt_ref, g_ref, w_ref, wg_ref, wgt_ref, bg_ref, bgt_ref,
                   z_ref, gates_ref, gatest_ref):
    h = _rms(_stacked(xm_ref, xt_ref), g_ref[...]).astype(BF16)
    z_ref[...] = _dot(h, w_ref[...])
    gates_ref[...] = _dot(h, wg_ref[...]) + bg_ref[...]
    gatest_ref[...] = _dot_nt(wgt_ref[...], h) + bgt_ref[...]


def _inproj(x_main, x_tail, g, w, wg, wgt, bg, bgt):
    d = x_main.shape[1]
    tm = TOKEN_TILE
    n = x_main.shape[0] + tm
    nz = w.shape[1]
    ng = wgt.shape[0]
    full = lambda i: (0, 0)
    return pl.pallas_call(
        _inproj_kernel,
        grid=(n // tm,),
        in_specs=_stacked_specs(x_main, x_tail) + [
            pl.BlockSpec((1, d), full),
            pl.BlockSpec((d, nz), full),
            pl.BlockSpec((d, LANES), full),
            pl.BlockSpec((ng, d), full),
            pl.BlockSpec((1, LANES), full),
            pl.BlockSpec((ng, 1), full),
        ],
        out_specs=[
            pl.BlockSpec((tm, nz), lambda i: (i, 0)),
            pl.BlockSpec((tm, LANES), lambda i: (i, 0)),
            pl.BlockSpec((ng, tm), lambda i: (0, i)),
        ],
        out_shape=[
            jax.ShapeDtypeStruct((n, nz), F32),
            jax.ShapeDtypeStruct((n, LANES), F32),
            jax.ShapeDtypeStruct((ng, n), F32),
        ],
        compiler_params=_params("parallel"),
        name="inproj",
    )(x_main, x_tail, g, w, wg, wgt, bg, bgt)


def _pool_prompt_kernel(u_ref, pw_ref, ps_ref, y_ref, st_ref, ext_ref, *, tt, gw):
    t = pl.program_id(1)
    nt = pl.num_programs(1)
    halo = POOL_BUF + 1
    width = ext_ref.shape[1]

    @pl.when(t == 0)
    def _():
        ext_ref[0:halo, :] = jnp.zeros((halo, width), F32)

    u = u_ref[...]
    ext_ref[halo:halo + tt, :] = u
    pos = t * tt + lax.broadcasted_iota(I32, (tt, 1), 0)
    for g, w in enumerate(POOL_WINDOWS):
        c0 = g * gw
        acc = u[:, c0:c0 + gw]
        for j in range(1, w):
            acc = acc + ext_ref[halo - j:halo - j + tt, c0:c0 + gw]
        cnt = jnp.minimum(w, pos + 1).astype(F32)
        d = acc / cnt - u[:, c0:c0 + gw]
        y = _dot(d.astype(BF16), pw_ref[g]) * ps_ref[:, c0:c0 + gw]
        y_ref[:, c0:c0 + gw] = y.astype(BF16)

    @pl.when(t == nt - 1)
    def _():
        st_ref[...] = ext_ref[tt + 1:tt + halo, :]

    ext_ref[0:halo, :] = ext_ref[tt:tt + halo, :]


def _pool_prompt(z, pw, ps, batch, seq):
    width = ps.shape[1]
    gw = width // len(POOL_WINDOWS)
    tt = POOL_TIME_TILE
    nt = seq // tt
    return pl.pallas_call(
        functools.partial(_pool_prompt_kernel, tt=tt, gw=gw),
        grid=(batch, nt),
        in_specs=[
            pl.BlockSpec((tt, width), lambda b, t: (b * nt + t, 0)),
            pl.BlockSpec(pw.shape, lambda b, t: (0, 0, 0)),
            pl.BlockSpec((1, width), lambda b, t: (0, 0)),
        ],
        out_specs=[
            pl.BlockSpec((tt, width), lambda b, t: (b * nt + t, 0)),
            pl.BlockSpec((None, POOL_BUF, width), lambda b, t: (b, 0, 0)),
        ],
        out_shape=[
            jax.ShapeDtypeStruct((batch * seq, width), BF16),
            jax.ShapeDtypeStruct((batch, POOL_BUF, width), F32),
        ],
        scratch_shapes=[pltpu.VMEM((POOL_BUF + 1 + tt, width), F32)],
        compiler_params=_params("parallel", "arbitrary"),
        name="pool_prompt",
    )(z, pw, ps)


def _pool_step_kernel(u_ref, buf_ref, pw_ref, ps_ref, y_ref, nb_ref, *, gw):
    u = u_ref[...]
    for g, w in enumerate(POOL_WINDOWS):
        c0 = g * gw
        acc = u[:, c0:c0 + gw]
        for j in range(1, w):
            acc = acc + buf_ref[POOL_BUF - j, :, c0:c0 + gw]
        cnt = float(min(w, PAST_LEN + 1))
        d = acc / cnt - u[:, c0:c0 + gw]
        y = _dot(d.astype(BF16), pw_ref[g]) * ps_ref[:, c0:c0 + gw]
        y_ref[:, c0:c0 + gw] = y.astype(BF16)
    nb_ref[0:POOL_BUF - 1] = buf_ref[1:POOL_BUF]
    nb_ref[POOL_BUF - 1] = u


def _pool_step(z, buf_t, pw, ps, row0):
    _, batch, width = buf_t.shape
    gw = width // len(POOL_WINDOWS)
    return pl.pallas_call(
        functools.partial(_pool_step_kernel, gw=gw),
        grid=(1,),
        in_specs=[
            pl.BlockSpec((batch, width), lambda i: (row0 // batch, 0)),
            pl.BlockSpec(buf_t.shape, lambda i: (0, 0, 0)),
            pl.BlockSpec(pw.shape, lambda i: (0, 0, 0)),
            pl.BlockSpec((1, width), lambda i: (0, 0)),
        ],
        out_specs=[
            pl.BlockSpec((batch, width), lambda i: (0, 0)),
            pl.BlockSpec(buf_t.shape, lambda i: (0, 0, 0)),
        ],
        out_shape=[
            jax.ShapeDtypeStruct((batch, width), BF16),
            jax.ShapeDtypeStruct(buf_t.shape, F32),
        ],
        compiler_params=_params("arbitrary"),
        name="pool_step",
    )(z, buf_t, pw, ps)


def _mlstm_prompt_kernel(*refs, nh, dh, nseq):
    seq_in = [refs[5 * s:5 * s + 5] for s in range(nseq)]
    h_ref, c_out, n_out, m_out, c_s, n_s, m_s = refs[5 * nseq:]
    ci = pl.program_id(1)
    nc = pl.num_programs(1)
    ln = seq_in[0][0].shape[0]

    @pl.when(ci == 0)
    def _():
        c_s[...] = jnp.zeros(c_s.shape, F32)
        n_s[...] = jnp.zeros(n_s.shape, F32)
        m_s[...] = jnp.zeros(m_s.shape, F32)

    row = lax.broadcasted_iota(I32, (ln, ln), 0)
    col = lax.broadcasted_iota(I32, (ln, ln), 1)
    causal = col <= row
    eye = col == row
    lane = lax.broadcasted_iota(I32, (nh, ln), 1)

    def to_col(r):
        return jnp.sum(jnp.where(eye, r, 0.0), axis=1, keepdims=True)

    scale = dh ** -0.5
    pairs = [(s, h) for s in range(nseq) for h in range(nh)]
    c_old = {p: c_s[p[0], p[1]] for p in pairs}
    n_old = {p: n_s[p[0], p[1]] for p in pairs}
    m_old = {p: m_s[p[0], p[1]][:, 0:1] for p in pairs}
    gates = []
    for s in range(nseq):
        gt = seq_in[s][4][...]
        bc_all = jax.nn.log_sigmoid(gt[nh:2 * nh])
        sh = 1
        while sh < ln:
            bc_all = bc_all + jnp.where(lane >= sh, pltpu.roll(bc_all, sh, 1), 0.0)
            sh *= 2
        gates.append((gt[0:nh], bc_all))
    new_state = {}
    for s, h in pairs:
        q_ref, k_ref, v_ref, o_ref, _ = seq_in[s]
        li_all, bc_all = gates[s]
        sl = slice(h * dh, (h + 1) * dh)
        q = q_ref[:, sl]
        k = k_ref[:, sl] * scale
        v = v_ref[:, sl]
        qb, kb, vb = q.astype(BF16), k.astype(BF16), v.astype(BF16)
        li_r = li_all[h:h + 1]
        bc_r = bc_all[h:h + 1]
        bc_c = to_col(bc_r)
        m0, c, n = m_old[s, h], c_old[s, h], n_old[s, h]

        dmat = jnp.where(causal, bc_c - bc_r + li_r, -jnp.inf)
        inter = bc_c + m0
        m_row = jnp.maximum(inter, jnp.max(dmat, axis=1, keepdims=True))
        w_intra = jnp.exp(dmat - m_row)
        w_inter = jnp.exp(inter - m_row)
        sc = _dot_nt(qb, kb) * w_intra
        num = _dot(sc.astype(BF16), vb) + w_inter * _dot_nt(qb, c.astype(BF16))
        den = jnp.sum(sc, axis=1, keepdims=True) + w_inter * jnp.sum(q * n, axis=1, keepdims=True)
        hh = num / jnp.maximum(jnp.abs(den), jnp.exp(-m_row))
        h_ref[s, :, sl] = (hh * jax.nn.sigmoid(o_ref[:, sl])).astype(BF16)

        b_end = bc_r[:, ln - 1:ln]
        g_r = b_end - bc_r + li_r
        m_new = jnp.maximum(b_end + m0, jnp.max(g_r, axis=1, keepdims=True))
        wg_c = jnp.exp(to_col(g_r) - m_new)
        decay = jnp.exp(b_end + m0 - m_new)
        new_state[s, h] = (decay * c + _dot_tn((v * wg_c).astype(BF16), kb),
                           decay * n + jnp.sum(wg_c * k, axis=0, keepdims=True),
                           jnp.broadcast_to(m_new, (1, dh)))

    for (s, h), (c_new, n_new, m_new) in new_state.items():
        c_s[s, h] = c_new
        n_s[s, h] = n_new
        m_s[s, h] = m_new

    @pl.when(ci == nc - 1)
    def _():
        c_out[...] = c_s[...]
        n_out[...] = n_s[...]
        m_out[...] = m_s[...]


def _mlstm_prompt(z, gates_t, batch, seq, nh, dh):
    ln = MLSTM_CHUNK
    nc = seq // ln
    nseq = MLSTM_SEQ_PER_STEP
    width = nh * dh
    ng = gates_t.shape[0]
    rows_of = lambda s: (lambda b, c: (b * nseq + s) * nc + c)
    in_specs, operands = [], []
    for s in range(nseq):
        r = rows_of(s)
        for j in (1, 2, 3, 4):
            in_specs.append(pl.BlockSpec((ln, width), lambda b, c, r=r, j=j: (r(b, c), j)))
        in_specs.append(pl.BlockSpec((ng, ln), lambda b, c, r=r: (0, r(b, c))))
        operands += [z, z, z, z, gates_t]
    h_spec = pl.BlockSpec((None, nseq, ln, width), lambda b, c: (b, 0, c, 0))
    st = lambda a, b_: pl.BlockSpec((nseq, nh, a, b_), lambda b, c: (b, 0, 0, 0))
    return pl.pallas_call(
        functools.partial(_mlstm_prompt_kernel, nh=nh, dh=dh, nseq=nseq),
        grid=(batch // nseq, nc),
        in_specs=in_specs,
        out_specs=[h_spec, st(dh, dh), st(1, dh), st(1, dh)],
        out_shape=[
            jax.ShapeDtypeStruct((batch // nseq, nseq, seq, width), BF16),
            jax.ShapeDtypeStruct((batch, nh, dh, dh), F32),
            jax.ShapeDtypeStruct((batch, nh, 1, dh), F32),
            jax.ShapeDtypeStruct((batch, nh, 1, dh), F32),
        ],
        scratch_shapes=[pltpu.VMEM((nseq, nh, dh, dh), F32), pltpu.VMEM((nseq, nh, 1, dh), F32),
                        pltpu.VMEM((nseq, nh, 1, dh), F32)],
        compiler_params=_params("parallel", "arbitrary"),
        name="mlstm_prompt",
    )(*operands)


def _mlstm_step_kernel(q_ref, k_ref, v_ref, o_ref, li_ref, fp_ref, c_ref, n_ref, m_ref,
                       h_ref, c_out, n_out, m_out, *, nh, dh):
    eye = (lax.broadcasted_iota(I32, (1, dh, dh), 1) == lax.broadcasted_iota(I32, (1, dh, dh), 2))
    scale = dh ** -0.5
    for h in range(nh):
        sl = slice(h * dh, (h + 1) * dh)
        q = q_ref[:, :, sl]
        k = k_ref[:, :, sl] * scale
        v = v_ref[:, :, sl]
        c = c_ref[:, h]
        n = n_ref[:, h]
        m = m_ref[:, h]
        li = li_ref[:, h]
        lf = jax.nn.log_sigmoid(fp_ref[:, h])
        inter = lf + m
        m_row = jnp.maximum(inter, li)
        w_intra = jnp.exp(li - m_row)
        w_inter = jnp.exp(inter - m_row)
        sc = jnp.sum(q * k, axis=-1, keepdims=True) * w_intra
        v_c = jnp.sum(jnp.where(eye, v, 0.0), axis=-1, keepdims=True)
        num = sc * v_c + w_inter * jnp.sum(c * q, axis=-1, keepdims=True)
        den = sc + w_inter * jnp.sum(n * q, axis=-1, keepdims=True)
        h_c = num / jnp.maximum(jnp.abs(den), jnp.exp(-m_row))
        h_l = jnp.sum(jnp.where(eye, h_c, 0.0), axis=1, keepdims=True)
        h_ref[:, :, sl] = h_l * jax.nn.sigmoid(o_ref[:, :, sl])
        wg = jnp.exp(li - m_row)
        decay = jnp.exp(inter - m_row)
        c_out[:, h] = decay * c + (v_c * wg) * k
        n_out[:, h] = decay * n + wg * k
        m_out[:, h] = m_row


def _mlstm_step(z3, li, fp, c, n, m, nh, dh):
    batch = c.shape[0]
    bb = MLSTM_STEP_BATCH
    width = nh * dh
    blk = lambda j: pl.BlockSpec((bb, 1, width), lambda i: (i, 0, j))
    st4 = lambda a, b: pl.BlockSpec((bb, nh, a, b), lambda i: (i, 0, 0, 0))
    return pl.pallas_call(
        functools.partial(_mlstm_step_kernel, nh=nh, dh=dh),
        grid=(batch // bb,),
        in_specs=[blk(1), blk(2), blk(3), blk(4), st4(1, 1), st4(1, 1),
                  st4(dh, dh), st4(1, dh), st4(1, 1)],
        out_specs=[pl.BlockSpec((bb, 1, width), lambda i: (i, 0, 0)),
                   st4(dh, dh), st4(1, dh), st4(1, 1)],
        out_shape=[
            jax.ShapeDtypeStruct((batch, 1, width), F32),
            jax.ShapeDtypeStruct((batch, nh, dh, dh), F32),
            jax.ShapeDtypeStruct((batch, nh, 1, dh), F32),
            jax.ShapeDtypeStruct((batch, nh, 1, 1), F32),
        ],
        compiler_params=_params("parallel"),
        name="mlstm_step",
    )(z3, z3, z3, z3, li, fp, c, n, m)


def _split_bf16(a):
    hi = a.astype(BF16)
    return hi, (a - hi.astype(F32)).astype(BF16)


def _route(hn, wr_ref, br_ref, tril_ref, carry_ref, rinfo_ref):
    ngrp, epg = MOE_GROUPS, MOE_EXPERTS_PER_GROUP
    h_hi, h_lo = _split_bf16(hn)
    w_hi, w_lo = wr_ref[0], wr_ref[1]
    logits = _dot(h_hi, w_hi) + (_dot(h_lo, w_hi) + _dot(h_hi, w_lo)) + br_ref[...]
    tm = logits.shape[0]
    lane = lax.broadcasted_iota(I32, (tm, LANES), 1)
    neg = -jnp.inf

    def first_max(x):
        mx = jnp.max(x, axis=1, keepdims=True)
        idx = jnp.min(jnp.where(x == mx, lane, LANES), axis=1, keepdims=True)
        return mx, idx

    is_grp = lane < ngrp
    gmax, gsel = first_max(jnp.where(is_grp, logits, neg))
    g_w = 1.0 / jnp.sum(jnp.where(is_grp, jnp.exp(logits - gmax), 0.0), axis=1, keepdims=True)
    lo = ngrp + gsel * epg
    el = jnp.where((lane >= lo) & (lane < lo + epg), logits, neg)
    v1, i1 = first_max(el)
    v2, i2 = first_max(jnp.where(lane == i1, neg, el))
    e2 = jnp.exp(v2 - v1)
    w1 = g_w / (1.0 + e2)
    w2 = g_w * e2 / (1.0 + e2)
    eid1 = i1 - ngrp
    eid2 = i2 - ngrp

    hit1 = lane == eid1
    hit2 = lane == eid2
    onehot = jnp.where(hit1 | hit2, 1.0, 0.0)
    carry = carry_ref[...]
    prefix = _dot(tril_ref[...], onehot.astype(BF16)) + carry
    rank1 = jnp.sum(jnp.where(hit1, prefix, 0.0), axis=1, keepdims=True)
    rank2 = jnp.sum(jnp.where(hit2, prefix, 0.0), axis=1, keepdims=True)
    carry_ref[...] = carry + jnp.sum(onehot, axis=0, keepdims=True)

    cols = (eid1.astype(F32), eid2.astype(F32), w1, w2, rank1, rank2)
    info = jnp.zeros((tm, LANES), F32)
    for j, cval in enumerate(cols):
        info = jnp.where(lane == j, cval, info)
    rinfo_ref[...] = info


def _outproj_route_kernel(xm_ref, xt_ref, pm_ref, pt_ref, mm_ref, mt_ref,
                          wo_ref, g_ref, wr_ref, br_ref, tril_ref,
                          x1_ref, hn_ref, rinfo_ref, cnt_ref, carry_ref):
    @pl.when(pl.program_id(0) == 0)
    def _():
        carry_ref[...] = jnp.zeros(carry_ref.shape, F32)

    half = pm_ref.shape[1]
    mix = (_dot(_stacked(pm_ref, pt_ref), wo_ref[0:half, :])
           + _dot(_stacked(mm_ref, mt_ref), wo_ref[half:2 * half, :]))
    x1 = _stacked(xm_ref, xt_ref) + mix
    x1_ref[...] = x1
    hn = _rms(x1, g_ref[...])
    hn_ref[...] = hn
    _route(hn, wr_ref, br_ref, tril_ref, carry_ref, rinfo_ref)
    cnt_ref[...] = carry_ref[...]


def _glu_route_kernel(x_ref, ym_ref, yt_ref, wglu_ref, g_ref, wr_ref, br_ref, tril_ref,
                      x1_ref, hn_ref, rinfo_ref, cnt_ref, carry_ref):
    @pl.when(pl.program_id(0) == 0)
    def _():
        carry_ref[...] = jnp.zeros(carry_ref.shape, F32)

    d = x_ref.shape[1]
    ag = _dot(jax.nn.gelu(_stacked(ym_ref, yt_ref)).astype(BF16), wglu_ref[...])
    x1 = x_ref[...] + ag[:, 0:d] * jax.nn.sigmoid(ag[:, d:2 * d])
    x1_ref[...] = x1
    hn = _rms(x1, g_ref[...])
    hn_ref[...] = hn
    _route(hn, wr_ref, br_ref, tril_ref, carry_ref, rinfo_ref)
    cnt_ref[...] = carry_ref[...]


def _mix_route(kernel, name, n, row_specs, rows, w, g, wr, br, tril):
    d = g.shape[1]
    tm = TOKEN_TILE
    full = lambda i: (0, 0)
    return pl.pallas_call(
        kernel,
        grid=(n // tm,),
        in_specs=row_specs + [
            pl.BlockSpec(w.shape, full),
            pl.BlockSpec((1, d), full),
            pl.BlockSpec((2, d, LANES), lambda i: (0, 0, 0)),
            pl.BlockSpec((1, LANES), full),
            pl.BlockSpec((tm, tm), full),
        ],
        out_specs=[
            pl.BlockSpec((tm, d), lambda i: (i, 0)),
            pl.BlockSpec((tm, d), lambda i: (i, 0)),
            pl.BlockSpec((tm, LANES), lambda i: (i, 0)),
            pl.BlockSpec((1, LANES), full),
        ],
        out_shape=[
            jax.ShapeDtypeStruct((n, d), F32),
            jax.ShapeDtypeStruct((n, d), F32),
            jax.ShapeDtypeStruct((n, LANES), F32),
            jax.ShapeDtypeStruct((1, LANES), F32),
        ],
        scratch_shapes=[pltpu.VMEM((1, LANES), F32)],
        compiler_params=_params("arbitrary"),
        name=name,
    )(*rows, w, g, wr, br, tril)


def _index_copy(pos_hbm, idx_s, sem_i, tile, slot):
    return pltpu.make_async_copy(pos_hbm.at[tile], idx_s.at[slot], sem_i.at[slot])


def _dispatch_kernel(pos_hbm, hn_ref, xs_hbm, idx_s, sem_i, sem_d):
    i = pl.program_id(0)
    nt = pl.num_programs(0)
    tm = hn_ref.shape[0]
    slot = i % 2

    @pl.when(i == 0)
    def _():
        _index_copy(pos_hbm, idx_s, sem_i, 0, 0).start()

    _index_copy(pos_hbm, idx_s, sem_i, i, slot).wait()

    @pl.when(i + 1 < nt)
    def _():
        _index_copy(pos_hbm, idx_s, sem_i, i + 1, 1 - slot).start()

    def issue(r, carry):
        row = hn_ref.at[pl.ds(r, 1)]
        pltpu.make_async_copy(row, xs_hbm.at[pl.ds(idx_s[slot, 0, r], 1)], sem_d).start(priority=0)
        pltpu.make_async_copy(row, xs_hbm.at[pl.ds(idx_s[slot, 0, tm + r], 1)], sem_d).start(priority=1)
        return carry

    lax.fori_loop(0, tm, issue, 0, unroll=DMA_ISSUE_UNROLL)
    whole = pltpu.make_async_copy(hn_ref, xs_hbm.at[pl.ds(0, tm)], sem_d)
    whole.wait()
    whole.wait()


def _dispatch(pos_tiles, hn):
    n, d = hn.shape
    tm = TOKEN_TILE
    return pl.pallas_call(
        _dispatch_kernel,
        grid=(n // tm,),
        in_specs=[pl.BlockSpec(memory_space=pl.ANY), pl.BlockSpec((tm, d), lambda i: (i, 0))],
        out_specs=pl.BlockSpec(memory_space=pl.ANY),
        out_shape=jax.ShapeDtypeStruct((2 * n, d), F32),
        scratch_shapes=[pltpu.SMEM((2, 1, 2 * tm), I32), pltpu.SemaphoreType.DMA((2,)),
                        pltpu.SemaphoreType.DMA],
        compiler_params=_params("arbitrary"),
        name="moe_dispatch",
    )(pos_tiles, hn)


def _moe_kernel(vt_ref, ve_ref, von_ref, lo_ref, hi_ref, xs_ref, wg_ref, wu_ref, wd_ref,
                eo_ref, wgb, wub, wdb):
    v = pl.program_id(0)
    tr = xs_ref.shape[0]
    prev = jnp.maximum(v - 1, 0)
    e = ve_ref[v]
    new_expert = jnp.logical_or(v == 0, e != ve_ref[prev])
    first_visit = jnp.logical_or(v == 0, vt_ref[v] != vt_ref[prev])

    @pl.when(von_ref[v] == 1)
    def _():
        @pl.when(new_expert)
        def _():
            wgb[...] = wg_ref[...].astype(BF16)
            wub[...] = wu_ref[...].astype(BF16)
            wdb[...] = wd_ref[...].astype(BF16)

        x = xs_ref[...].astype(BF16)
        act = jax.nn.silu(_dot(x, wgb[...])) * _dot(x, wub[...])
        row = vt_ref[v] * tr + lax.broadcasted_iota(I32, (tr, 1), 0)
        act = jnp.where((row >= lo_ref[e]) & (row < hi_ref[e]), act, 0.0)
        res = _dot(act.astype(BF16), wdb[...])

        @pl.when(first_visit)
        def _():
            eo_ref[...] = res

        @pl.when(jnp.logical_not(first_visit))
        def _():
            eo_ref[...] += res


def _moe_experts(vt, ve, von, lo, hi, xs, wg, wu, wd, layer):
    nv = vt.shape[0]
    rows, d = xs.shape
    hid = wg.shape[3]
    tr = EXPERT_ROW_TILE
    wspec = lambda a, b: pl.BlockSpec((None, None, a, b),
                                      lambda v, vt, ve, von, lo, hi: (layer, ve[v], 0, 0))
    grid_spec = pltpu.PrefetchScalarGridSpec(
        num_scalar_prefetch=5,
        grid=(nv,),
        in_specs=[
            pl.BlockSpec((tr, d), lambda v, vt, ve, von, lo, hi: (vt[v], 0)),
            wspec(d, hid), wspec(d, hid), wspec(hid, d),
        ],
        out_specs=pl.BlockSpec((tr, d), lambda v, vt, ve, von, lo, hi: (vt[v], 0)),
        scratch_shapes=[pltpu.VMEM((d, hid), BF16), pltpu.VMEM((d, hid), BF16),
                        pltpu.VMEM((hid, d), BF16)],
    )
    return pl.pallas_call(
        _moe_kernel,
        grid_spec=grid_spec,
        out_shape=jax.ShapeDtypeStruct((rows, d), F32),
        compiler_params=_params("arbitrary"),
        name="moe_experts",
    )(vt, ve, von, lo, hi, xs, wg, wu, wd)


def _combine_kernel(pos_hbm, eo_hbm, x_ref, rinfo_ref, g_ref, o1_ref, o2_ref, idx_s, a_buf, b_buf,
                    sem_i, sem_a, sem_b, *, last_layer):
    i = pl.program_id(0)
    nt = pl.num_programs(0)
    tm = a_buf.shape[1]
    slot = i % 2

    def gathers(s):
        def issue(r, carry):
            pltpu.make_async_copy(eo_hbm.at[pl.ds(idx_s[s, 0, r], 1)],
                                  a_buf.at[s, pl.ds(r, 1)], sem_a.at[s]).start(priority=0)
            pltpu.make_async_copy(eo_hbm.at[pl.ds(idx_s[s, 0, tm + r], 1)],
                                  b_buf.at[s, pl.ds(r, 1)], sem_b.at[s]).start(priority=1)
            return carry

        lax.fori_loop(0, tm, issue, 0, unroll=DMA_ISSUE_UNROLL)

    @pl.when(i == 0)
    def _():
        first = _index_copy(pos_hbm, idx_s, sem_i, 0, 0)
        first.start()
        first.wait()
        gathers(0)

        @pl.when(nt > 1)
        def _():
            _index_copy(pos_hbm, idx_s, sem_i, 1, 1).start()

    @pl.when(i + 1 < nt)
    def _():
        _index_copy(pos_hbm, idx_s, sem_i, i + 1, 1 - slot).wait()
        gathers(1 - slot)

    @pl.when(i + 2 < nt)
    def _():
        _index_copy(pos_hbm, idx_s, sem_i, i + 2, slot).start()

    pltpu.make_async_copy(eo_hbm.at[pl.ds(0, tm)], a_buf.at[slot], sem_a.at[slot]).wait()
    pltpu.make_async_copy(eo_hbm.at[pl.ds(0, tm)], b_buf.at[slot], sem_b.at[slot]).wait()
    info = rinfo_ref[...]
    x2 = x_ref[...] + (info[:, 2:3] * a_buf[slot] + info[:, 3:4] * b_buf[slot])
    hn = _rms(x2, g_ref[...])
    if last_layer:
        @pl.when(i < nt - 1)
        def _():
            o1_ref[...] = hn

        @pl.when(i == nt - 1)
        def _():
            o2_ref[...] = hn
    else:
        o1_ref[...] = x2
        o2_ref[...] = hn


def _combine(pos_tiles, eo, x, rinfo, g, last_layer):
    n, d = x.shape
    tm = TOKEN_TILE
    nt = n // tm
    row = pl.BlockSpec((tm, d), lambda i: (i, 0))
    if last_layer:
        out_specs = [pl.BlockSpec((tm, d), lambda i: (jnp.minimum(i, nt - 2), 0)),
                     pl.BlockSpec((tm, d), lambda i: (0, 0))]
        out_shape = [jax.ShapeDtypeStruct((n - tm, d), F32), jax.ShapeDtypeStruct((tm, d), F32)]
    else:
        out_specs = [row, row]
        out_shape = [jax.ShapeDtypeStruct((n, d), F32), jax.ShapeDtypeStruct((n, d), F32)]
    return pl.pallas_call(
        functools.partial(_combine_kernel, last_layer=last_layer),
        grid=(nt,),
        in_specs=[
            pl.BlockSpec(memory_space=pl.ANY),
            pl.BlockSpec(memory_space=pl.ANY),
            row,
            pl.BlockSpec((tm, LANES), lambda i: (i, 0)),
            pl.BlockSpec((1, d), lambda i: (0, 0)),
        ],
        out_specs=out_specs,
        out_shape=out_shape,
        scratch_shapes=[
            pltpu.SMEM((2, 1, 2 * tm), I32),
            pltpu.VMEM((2, tm, d), F32),
            pltpu.VMEM((2, tm, d), F32),
            pltpu.SemaphoreType.DMA((2,)),
            pltpu.SemaphoreType.DMA((2,)),
            pltpu.SemaphoreType.DMA((2,)),
        ],
        compiler_params=_params("arbitrary"),
        name="moe_combine",
    )(pos_tiles, eo, x, rinfo, g)


def _moe(x1, hn, rinfo, counts, wg, wu, wd, layer, next_gain, last_layer):
    n = x1.shape[0]
    ne = wg.shape[1]
    tr = EXPERT_ROW_TILE
    tm = TOKEN_TILE
    nv = (2 * n) // tr + ne - 1
    eid = rinfo[:, 0:2].astype(I32)
    rank = rinfo[:, 4:6].astype(I32)
    cnt = counts[0, :ne].astype(I32)
    seg_end = jnp.cumsum(cnt)
    seg_start = seg_end - cnt
    experts = jnp.arange(ne, dtype=I32)
    pos = jnp.sum(jnp.where(eid[:, :, None] == experts, seg_start, 0), axis=-1) + rank
    pos_tiles = pos.reshape(n // tm, tm, 2).transpose(0, 2, 1).reshape(n // tm, 1, 2 * tm)

    first_tile = seg_start // tr
    tiles_e = jnp.where(cnt > 0, (seg_end - 1) // tr - first_tile + 1, 0)
    v_end = jnp.cumsum(tiles_e)
    v_start = v_end - tiles_e
    total = v_end[-1]
    vis = jnp.arange(nv, dtype=I32)
    vc = jnp.minimum(vis, jnp.maximum(total - 1, 0))
    ve = jnp.sum((vc[:, None] >= v_end[None, :]).astype(I32), axis=1)
    pick = lambda tab: jnp.sum(jnp.where(ve[:, None] == experts, tab, 0), axis=1)
    vt = pick(first_tile) + (vc - pick(v_start))
    von = (vis < total).astype(I32)

    xs = _dispatch(pos_tiles, hn)
    eo = _moe_experts(vt, ve, von, seg_start, seg_end, xs, wg, wu, wd, layer)
    return _combine(pos_tiles, eo, x1, rinfo, next_gain, last_layer)


def _cis(log_mag, ang):
    mag = jnp.exp(log_mag)
    return mag * jnp.cos(ang), mag * jnp.sin(ang)


def _split3(a):
    p1 = a.astype(BF16)
    r1 = a - p1.astype(F32)
    p2 = r1.astype(BF16)
    return p1, p2, (r1 - p2.astype(F32)).astype(BF16)


def _select_dot(a, sel, sel_first=False):
    sel = sel.astype(BF16)
    parts = [(_dot(sel, p) if sel_first else _dot(p, sel)) for p in _split3(a)]
    return parts[0] + (parts[1] + parts[2])


def _dot3(a, b):
    a_hi, a_lo = _split_bf16(a)
    b_hi, b_lo = _split_bf16(b)
    return _dot(a_hi, b_hi) + (_dot(a_lo, b_hi) + _dot(a_hi, b_lo))


def _s5_prep_kernel(lsc_ref, lsr_ref, lrc_ref, lic_ref, lrr_ref, lir_ref, ctre_ref, ctim_ref,
                    btre_ref, btim_ref,
                    t_ref, wre_ref, wim_ref, cyre_ref, cyim_ref, apr_ref, api_ref,
                    bbr_ref, bbi_ref, lbr_ref, lbi_ref, *, gw, gpb, sub):
    blk = gpb * gw
    wide = sub * blk
    nsl = lrc_ref.shape[1]
    ns = nsl // gpb
    sh_blk, sh_gw, sh_ns = blk.bit_length() - 1, gw.bit_length() - 1, ns.bit_length() - 1
    dt_c, dt_r = jnp.exp(lsc_ref[0]), jnp.exp(lsr_ref[0])
    ldt_c_re, ldt_c_im = lrc_ref[0] * dt_c, lic_ref[0] * dt_c
    ldt_r_re, ldt_r_im = lrr_ref[0] * dt_r, lir_ref[0] * dt_r

    lane_w = lax.broadcasted_iota(I32, (1, wide), 1)
    spread = jnp.where((lax.broadcasted_iota(I32, (gw, wide), 1) & (gw - 1))
                       == lax.broadcasted_iota(I32, (gw, wide), 0), 1.0, 0.0)
    cre = _select_dot(ctre_ref[0], spread)
    cim = _select_dot(ctim_ref[0], spread)
    same = (jnp.right_shift(lax.broadcasted_iota(I32, (nsl, 1), 0), sh_ns)
            == (jnp.right_shift(lane_w, sh_gw) & (gpb - 1)))
    assert 2 * blk == LANES
    tau = lax.broadcasted_iota(I32, (1, LANES), 1).astype(F32)
    pw_re, pw_im = _cis(tau * ldt_c_re, tau * ldt_c_im)
    low = lax.broadcasted_iota(I32, (nsl, LANES), 1) < blk

    def spread_pow(p, first):
        col = lambda t: jnp.broadcast_to(p[:, t:t + 1], (nsl, LANES))
        return jnp.concatenate([jnp.where(low, col(first + 2 * m), col(first + 2 * m + 1))
                                for m in range(sub // 2)], axis=1)

    def c_lam_pow(first):
        pr, pi = spread_pow(pw_re, first), spread_pow(pw_im, first)
        return (jnp.where(same, pr * cre - pi * cim, 0.0),
                jnp.where(same, -(pr * cim + pi * cre), 0.0))

    clr0, cli0 = c_lam_pow(0)
    clr1, cli1 = c_lam_pow(1)
    cyre_ref[0] = clr1.astype(BF16)
    cyim_ref[0] = cli1.astype(BF16)

    lbr, lbi = _cis(ldt_r_re, ldt_r_im)
    lbr_ref[0] = lbr
    lbi_ref[0] = lbi
    lr, li = lrr_ref[0], lir_ref[0]
    nr, ni = lbr - 1.0, lbi
    den = lr * lr + li * li
    fr = (nr * lr + ni * li) / den
    fi = (ni * lr - nr * li) / den
    bre, bim = btre_ref[0], btim_ref[0]
    bbr = fr * bre - fi * bim
    bbi = fr * bim + fi * bre
    bbr_ref[0] = bbr
    bbi_ref[0] = bbi

    lane_t = lax.broadcasted_iota(I32, (gw, wide), 1)
    for g in range(gpb):
        ps = slice(g * ns, (g + 1) * ns)
        r = _dot3(bbr[:, ps], clr0[ps, :]) + _dot3(bbi[:, ps], cli0[ps, :])
        for j in range(sub):
            tb = r if j == 0 else jnp.where(lane_t >= blk * j, pltpu.roll(r, blk * j, 1), 0.0)
            r0 = j * blk + g * gw
            t_ref[0, r0:r0 + gw, :] = tb.astype(BF16)

    rows = lax.broadcasted_iota(I32, (wide, 1), 0)
    spread_t = jnp.where((lax.broadcasted_iota(I32, (wide, gw), 0) & (gw - 1))
                         == lax.broadcasted_iota(I32, (wide, gw), 1), 1.0, 0.0)
    bbr_t = _select_dot(bbr, spread_t, sel_first=True)
    bbi_t = _select_dot(bbi, spread_t, sel_first=True)
    same_w = ((jnp.right_shift(rows, sh_gw) & (gpb - 1))
              == jnp.right_shift(lax.broadcasted_iota(I32, (1, nsl), 1), sh_ns))
    rj = ((sub - 1) - lax.broadcasted_iota(I32, (sub, 1), 0)).astype(F32)
    q_re, q_im = _cis(rj * ldt_r_re, rj * ldt_r_im)
    per_step = lambda q: jnp.concatenate(
        [jnp.broadcast_to(q[j:j + 1, :], (blk, nsl)) for j in range(sub)], axis=0)
    pr, pi = per_step(q_re), per_step(q_im)
    wre_ref[0] = jnp.where(same_w, pr * bbr_t - pi * bbi_t, 0.0).astype(BF16)
    wim_ref[0] = jnp.where(same_w, pr * bbi_t + pi * bbr_t, 0.0).astype(BF16)

    nlev = apr_ref.shape[1]
    pw = (sub * jnp.left_shift(1, lax.broadcasted_iota(I32, (nlev, 1), 0))).astype(F32)
    apr, api = _cis(pw * ldt_r_re, pw * ldt_r_im)
    apr_ref[0] = apr
    api_ref[0] = api


def _s5_prep(log_step, lam_re, lam_im, b_re, b_im, c_re, c_im):
    ng, ns = lam_re.shape
    gw = b_re.shape[2]
    sub, gpb = S5_SUB, S5_GROUPS_PER_BLOCK
    nblk = ng // gpb
    nsl = gpb * ns
    wide = sub * gpb * gw
    ls = jnp.repeat(log_step, ns)
    ct = lambda c: jnp.transpose(c, (0, 2, 1)).reshape(nblk, nsl, gw)
    bt = lambda b: jnp.transpose(b.reshape(nblk, gpb, ns, gw), (0, 3, 1, 2)).reshape(nblk, gw, nsl)
    ins = [ls.reshape(nblk, nsl, 1), ls.reshape(nblk, 1, nsl),
           lam_re.reshape(nblk, nsl, 1), lam_im.reshape(nblk, nsl, 1),
           lam_re.reshape(nblk, 1, nsl), lam_im.reshape(nblk, 1, nsl),
           ct(c_re), ct(c_im), bt(b_re), bt(b_im)]
    spec = lambda a: pl.BlockSpec((1,) + a.shape[1:], lambda g: (g, 0, 0))
    outs = [
        jax.ShapeDtypeStruct((nblk, wide, wide), BF16),
        jax.ShapeDtypeStruct((nblk, wide, nsl), BF16),
        jax.ShapeDtypeStruct((nblk, wide, nsl), BF16),
        jax.ShapeDtypeStruct((nblk, nsl, wide), BF16),
        jax.ShapeDtypeStruct((nblk, nsl, wide), BF16),
        jax.ShapeDtypeStruct((nblk, SUBLANES, nsl), F32),
        jax.ShapeDtypeStruct((nblk, SUBLANES, nsl), F32),
        jax.ShapeDtypeStruct((nblk, gw, nsl), F32),
        jax.ShapeDtypeStruct((nblk, gw, nsl), F32),
        jax.ShapeDtypeStruct((nblk, 1, nsl), F32),
        jax.ShapeDtypeStruct((nblk, 1, nsl), F32),
    ]
    return pl.pallas_call(
        functools.partial(_s5_prep_kernel, gw=gw, gpb=gpb, sub=sub),
        grid=(nblk,),
        in_specs=[spec(a) for a in ins],
        out_specs=[spec(o) for o in outs],
        out_shape=outs,
        compiler_params=_params("parallel"),
        name="s5_prep",
    )(*ins)


def _s5_prompt_kernel(x_ref, t_ref, wre_ref, wim_ref, cyre_ref, cyim_ref, apr_ref, api_ref, d_ref,
                      y_ref, hre_out, him_out, u_s, y4_s, sre_s, sim_s, *, nb, seq, sub):
    nk = seq // sub
    rows = nb * nk
    hl = LANES // 2
    pad = nk // 2
    nlev = nk.bit_length() - 1
    nsl = sre_s.shape[1]
    low = lax.broadcasted_iota(I32, (nk, LANES), 1) < hl
    kidx = lax.broadcasted_iota(I32, (rows, 1), 0) & (nk - 1)
    sre_s[0:pad, :] = jnp.zeros((pad, nsl), F32)
    sim_s[0:pad, :] = jnp.zeros((pad, nsl), F32)

    def step_rows(b, j):
        return (pl.ds(b * seq + j, nk, stride=sub), slice(None))

    for b in range(nb):
        rs = slice(b * nk, (b + 1) * nk)
        for m in range(sub // 2):
            ls = slice(m * LANES, (m + 1) * LANES)
            s0 = x_ref[step_rows(b, 2 * m)]
            s1 = x_ref[step_rows(b, 2 * m + 1)]
            u_s[0, rs, ls] = jnp.where(low, s0, pltpu.roll(s1, hl, 1)).astype(BF16)
            u_s[1, rs, ls] = jnp.where(low, pltpu.roll(s0, hl, 1), s1).astype(BF16)

    live = slice(pad, pad + rows)
    for half in range(2):
        u = u_s[half]
        hre = _dot(u, wre_ref[half])
        him = _dot(u, wim_ref[half])
        for i in range(nlev):
            s = 1 << i
            sre_s[live, :] = hre
            sim_s[live, :] = him
            pre = sre_s[pad - s:pad - s + rows, :]
            pim = sim_s[pad - s:pad - s + rows, :]
            ar = apr_ref[half, i:i + 1, :]
            ai = api_ref[half, i:i + 1, :]
            ok = kidx >= s
            hre = hre + jnp.where(ok, ar * pre - ai * pim, 0.0)
            him = him + jnp.where(ok, ar * pim + ai * pre, 0.0)
        sre_s[live, :] = hre
        sim_s[live, :] = him
        ls = slice(half * nsl, (half + 1) * nsl)
        for b in range(nb):
            last = pad + (b + 1) * nk - 1
            hre_out[b:b + 1, ls] = sre_s[last:last + 1, :]
            him_out[b:b + 1, ls] = sim_s[last:last + 1, :]
        started = kidx >= 1
        hpre = jnp.where(started, sre_s[pad - 1:pad - 1 + rows, :], 0.0).astype(BF16)
        hpim = jnp.where(started, sim_s[pad - 1:pad - 1 + rows, :], 0.0).astype(BF16)
        y4_s[half] = _dot(u, t_ref[half]) + _dot(hpre, cyre_ref[half]) + _dot(hpim, cyim_ref[half])

    d = d_ref[...]
    for b in range(nb):
        rs = slice(b * nk, (b + 1) * nk)
        for m in range(sub // 2):
            ls = slice(m * LANES, (m + 1) * LANES)
            ca = y4_s[0, rs, ls]
            cb = y4_s[1, rs, ls]
            y0 = jnp.where(low, ca, pltpu.roll(cb, hl, 1))
            y1 = jnp.where(low, pltpu.roll(ca, hl, 1), cb)
            y_ref[step_rows(b, 2 * m)] = y0 + d * x_ref[step_rows(b, 2 * m)]
            y_ref[step_rows(b, 2 * m + 1)] = y1 + d * x_ref[step_rows(b, 2 * m + 1)]


def _s5_prompt(h, prep, d_row, batch, seq):
    d = h.shape[1]
    tmat, wre, wim, cyre, cyim, apr, api = prep[:7]
    nblk, wide, nsl = wre.shape
    sub, nb = S5_SUB, S5_BATCH_PER_STEP
    nk = seq // sub
    assert nk & (nk - 1) == 0 and nk.bit_length() - 1 <= apr.shape[1]
    ntile = d // LANES
    assert nblk == 2 * ntile
    rows = nb * nk
    wspec = lambda a: pl.BlockSpec((2,) + a.shape[1:], lambda t, b: (t, 0, 0))
    xspec = pl.BlockSpec((nb * seq, LANES), lambda t, b: (b, t))
    sspec = pl.BlockSpec((None, nb, 2 * nsl), lambda t, b: (b, 0, t))
    return pl.pallas_call(
        functools.partial(_s5_prompt_kernel, nb=nb, seq=seq, sub=sub),
        grid=(ntile, batch // nb),
        in_specs=[xspec, wspec(tmat), wspec(wre), wspec(wim), wspec(cyre), wspec(cyim),
                  wspec(apr), wspec(api), pl.BlockSpec((1, LANES), lambda t, b: (0, t))],
        out_specs=[xspec, sspec, sspec],
        out_shape=[
            jax.ShapeDtypeStruct((batch * seq, d), F32),
            jax.ShapeDtypeStruct((batch // nb, nb, ntile * 2 * nsl), F32),
            jax.ShapeDtypeStruct((batch // nb, nb, ntile * 2 * nsl), F32),
        ],
        scratch_shapes=[
            pltpu.VMEM((2, rows, wide), BF16),
            pltpu.VMEM((2, rows, wide), F32),
            pltpu.VMEM((nk // 2 + rows, nsl), F32),
            pltpu.VMEM((nk // 2 + rows, nsl), F32),
        ],
        compiler_params=_params("parallel", "parallel"),
        name="s5_prompt",
    )(h, tmat, wre, wim, cyre, cyim, apr, api, d_row)


def _s5_step_kernel(u_ref, h_ref, hsw_ref, bb_ref, la_ref, lb_ref, cc_ref, d_ref, hn_ref, y_ref):
    for g in range(u_ref.shape[0]):
        u = u_ref[g]
        hn = (la_ref[g] * h_ref[g] + lb_ref[g] * hsw_ref[g]
              + jnp.dot(u, bb_ref[g], precision=HIGHEST, preferred_element_type=F32))
        hn_ref[g] = hn
        y_ref[g] = (jnp.dot(hn, cc_ref[g], precision=HIGHEST, preferred_element_type=F32)
                    + d_ref[g] * u)


def _s5_step(u_p, h_cat, h_swp, bb_cat, la, lb, cc, d_p):
    ng = u_p.shape[0]
    gb = SUBLANES
    spec = lambda a: pl.BlockSpec((gb,) + a.shape[1:], lambda g: (g, 0, 0))
    ins = [u_p, h_cat, h_swp, bb_cat, la, lb, cc, d_p]
    outs = [jax.ShapeDtypeStruct(h_cat.shape, F32), jax.ShapeDtypeStruct(u_p.shape, F32)]
    return pl.pallas_call(
        _s5_step_kernel,
        grid=(ng // gb,),
        in_specs=[spec(a) for a in ins],
        out_specs=[spec(o) for o in outs],
        out_shape=outs,
        compiler_params=_params("parallel"),
        name="s5_step",
    )(*ins)


def _pad_to(a, axis, size):
    pad = [(0, 0)] * a.ndim
    pad[axis] = (0, size - a.shape[axis])
    return jnp.pad(a, pad)


def kernel(x_prompt, x_sample, state_pool, state_mlstm_c, state_mlstm_n, state_mlstm_m, state_s5_re, state_s5_im, norm_mix, norm_ffn, norm_final, w_in_ab, b_gates, pool_w, pool_scale, w_out_ab, s5_lam_re, s5_lam_im, s5_log_step, s5_b_re, s5_b_im, s5_c_re, s5_c_im, s5_d, w_glu, moe_w_group, moe_b_group, moe_w_expert, moe_b_expert, moe_w_gate, moe_w_up, moe_w_down):
    bp, tp, d = x_prompt.shape
    bs = x_sample.shape[0]
    n_p = bp * tp
    tm = TOKEN_TILE
    assert n_p % tm == 0 and bs <= tm
    n_fill = tm - bs
    nh, dh = state_mlstm_c.shape[2], state_mlstm_c.shape[3]
    pool_width = state_pool.shape[3]
    ml_width = nh * dh
    n_main = pool_width + 4 * ml_width
    n_gates = 2 * nh
    ngrp_s5, n_state = s5_lam_re.shape[1], s5_lam_re.shape[2]
    gw_s5 = d // ngrp_s5

    n = n_p + tm

    def tail_tile(sample_rows, dtype):
        return _pad_to(sample_rows.astype(dtype), 0, tm)

    x_main = x_prompt.reshape(n_p, d)
    x_tail = tail_tile(x_sample.reshape(bs, d), F32)
    tril = jnp.tril(jnp.ones((tm, tm), BF16), -1)

    def router_weights(l):
        wr = _pad_to(jnp.concatenate([moe_w_group[l], moe_w_expert[l]], axis=1), 1, LANES)
        br = jnp.concatenate([moe_b_group[l], moe_b_expert[l]])[None, :]
        hi = wr.astype(BF16)
        lo = (wr - hi.astype(F32)).astype(BF16)
        return jnp.stack([hi, lo]), _pad_to(br, 1, LANES)

    w_in = w_in_ab[0]
    w_g = w_in[:, n_main:]
    z, gates, gates_t = _inproj(
        x_main, x_tail, norm_mix[0][None, :], w_in[:, :n_main].astype(BF16),
        _pad_to(w_g, 1, LANES).astype(BF16), w_g.T.astype(BF16),
        _pad_to(b_gates[0][None, :], 1, LANES), b_gates[0][:, None])

    pw = pool_w[0].astype(BF16)
    ps = pool_scale[0][None, :]
    pool_y_p, pool_p = _pool_prompt(z, pw, ps, bp, tp)
    pool_y_s, pool_s_t = _pool_step(z, jnp.transpose(state_pool[0], (1, 0, 2)), pw, ps, n_p)
    pool_s = jnp.transpose(pool_s_t, (1, 0, 2))

    ml_y_p, c_p, n_p_st, m_p = _mlstm_prompt(z, gates_t, bp, tp, nh, dh)
    ml_y_p = ml_y_p.reshape(n_p, ml_width)
    g_s = gates[n_p:n_p + bs, :n_gates]
    ml_y_s, c_s, n_s_st, m_s = _mlstm_step(
        z[n_p:n_p + bs].reshape(bs, 1, n_main),
        g_s[:, :nh].reshape(bs, nh, 1, 1), g_s[:, nh:].reshape(bs, nh, 1, 1),
        state_mlstm_c[0], state_mlstm_n[0].reshape(bs, nh, 1, dh),
        state_mlstm_m[0].reshape(bs, nh, 1, 1), nh, dh)

    rows = [x_main, x_tail, pool_y_p, tail_tile(pool_y_s, BF16),
            ml_y_p, tail_tile(ml_y_s.reshape(bs, ml_width), BF16)]
    specs = (_stacked_specs(x_main, x_tail) + _stacked_specs(pool_y_p, rows[3])
             + _stacked_specs(ml_y_p, rows[5]))
    wr, br = router_weights(0)
    x1, hn, rinfo, counts = _mix_route(
        _outproj_route_kernel, "outproj_route", n, specs, rows, w_out_ab[0].astype(BF16),
        norm_ffn[0][None, :], wr, br, tril)
    x2, h1 = _moe(x1, hn, rinfo, counts, moe_w_gate, moe_w_up, moe_w_down, 0,
                  norm_mix[1][None, :], last_layer=False)

    prep = _s5_prep(s5_log_step[0], s5_lam_re[0], s5_lam_im[0], s5_b_re[0], s5_b_im[0],
                    s5_c_re[0], s5_c_im[0])
    y_p, hre_p, him_p = _s5_prompt(h1, prep, s5_d[0][None, :], bp, tp)
    s5_re_p = hre_p.reshape(bp, ngrp_s5, n_state)
    s5_im_p = him_p.reshape(bp, ngrp_s5, n_state)

    gpb = S5_GROUPS_PER_BLOCK
    per_group = lambda a: jnp.transpose(
        a.reshape(ngrp_s5 // gpb, gw_s5, gpb, n_state), (0, 2, 1, 3)).reshape(ngrp_s5, gw_s5, n_state)
    bbr, bbi = per_group(prep[7]), per_group(prep[8])
    lbr, lbi = prep[9].reshape(ngrp_s5, 1, n_state), prep[10].reshape(ngrp_s5, 1, n_state)
    d_g = s5_d[0].reshape(ngrp_s5, 1, gw_s5)
    u_s = h1[n_p:n_p + bs].reshape(bs, ngrp_s5, gw_s5).transpose(1, 0, 2)
    h_re = jnp.transpose(state_s5_re[0], (1, 0, 2))
    h_im = jnp.transpose(state_s5_im[0], (1, 0, 2))
    cc = jnp.concatenate([jnp.transpose(s5_c_re[0], (0, 2, 1)),
                          -jnp.transpose(s5_c_im[0], (0, 2, 1))], axis=1)
    hn_s, y_s = _s5_step(
        _pad_to(u_s, 2, LANES),
        jnp.concatenate([h_re, h_im], axis=2), jnp.concatenate([h_im, h_re], axis=2),
        _pad_to(jnp.concatenate([bbr, bbi], axis=2), 1, LANES),
        jnp.concatenate([lbr, lbr], axis=2), jnp.concatenate([-lbi, lbi], axis=2),
        _pad_to(cc, 2, LANES), _pad_to(d_g, 2, LANES))
    s5_re_s = jnp.transpose(hn_s[:, :, :n_state], (1, 0, 2))
    s5_im_s = jnp.transpose(hn_s[:, :, n_state:], (1, 0, 2))
    y_tail = tail_tile(jnp.transpose(y_s[:, :, :gw_s5], (1, 0, 2)).reshape(bs, d), F32)

    wr, br = router_weights(1)
    specs = [pl.BlockSpec((tm, d), lambda i: (i, 0))] + _stacked_specs(y_p, y_tail)
    x3, hn, rinfo, counts = _mix_route(
        _glu_route_kernel, "glu_route", n, specs, [x2, y_p, y_tail], w_glu[0].astype(BF16),
        norm_ffn[1][None, :], wr, br, tril)
    y_main, y_last = _moe(x3, hn, rinfo, counts, moe_w_gate, moe_w_up, moe_w_down, 1,
                          norm_final[None, :], last_layer=True)

    return (y_main.reshape(bp, tp, d), y_last[:bs].reshape(bs, 1, d),
            pool_p[None], c_p[None], n_p_st.reshape(1, bp, nh, dh), m_p[:, :, 0, 0][None],
            s5_re_p[None], s5_im_p[None],
            pool_s[None], c_s[None], n_s_st.reshape(1, bs, nh, dh), m_s.reshape(1, bs, nh),
            s5_re_s[None], s5_im_s[None])
```

```python
import functools

import jax
import jax.numpy as jnp
from jax import lax
from jax.experimental import pallas as pl
from jax.experimental.pallas import tpu as pltpu

F32 = jnp.float32
BF16 = jnp.bfloat16
I32 = jnp.int32

PAST_LEN = 16384
POOL_WINDOWS = (2, 4, 8, 16)
POOL_BUF = max(POOL_WINDOWS) - 1
MLSTM_CHUNK = 128
S5_SUB = 16
MOE_GROUPS = 4
MOE_EXPERTS_PER_GROUP = 8
RMS_EPS = 1e-6

LANES = 128
SUBLANES = 8
VMEM_LIMIT_BYTES = 56 * 1024 * 1024

TOKEN_TILE = 512
EXPERT_ROW_TILE = 512
MLSTM_SEQ_PER_STEP = 2
MLSTM_STEP_BATCH = 16
POOL_TIME_TILE = 512
DMA_ISSUE_UNROLL = 8
S5_GROUPS_PER_BLOCK = 4
S5_BATCH_PER_STEP = 4

HIGHEST = lax.Precision.HIGHEST


def _params(*sem):
    return pltpu.CompilerParams(dimension_semantics=sem, vmem_limit_bytes=VMEM_LIMIT_BYTES)


def _rms(x, g):
    return x * lax.rsqrt(jnp.mean(x * x, axis=-1, keepdims=True) + RMS_EPS) * g


def _dot(a, b):
    return jnp.dot(a, b, preferred_element_type=F32)


def _dot_nt(a, b):
    return lax.dot_general(a, b, (((1,), (1,)), ((), ())), preferred_element_type=F32)


def _dot_tn(a, b):
    return lax.dot_general(a, b, (((0,), (0,)), ((), ())), preferred_element_type=F32)


def _stacked(main_ref, tail_ref):
    last = pl.program_id(0) == pl.num_programs(0) - 1
    return jnp.where(last, tail_ref[...], main_ref[...])


def _stacked_specs(main, tail):
    tm, w = tail.shape
    last_main = main.shape[0] // tm - 1
    return [pl.BlockSpec((tm, w), lambda i: (jnp.minimum(i, last_main), 0)),
            pl.BlockSpec((tm, w), lambda i: (0, 0))]


def _inproj_kernel(xm_ref, xt_ref, g_ref, w_ref, wg_ref, wgt_ref, bg_ref, bgt_ref,
                   z_ref, gates_ref, gatest_ref):
    h = _rms(_stacked(xm_ref, xt_ref), g_ref[...]).astype(BF16)
    z_ref[...] = _dot(h, w_ref[...])
    gates_ref[...] = _dot(h, wg_ref[...]) + bg_ref[...]
    gatest_ref[...] = _dot_nt(wgt_ref[...], h) + bgt_ref[...]


def _inproj(x_main, x_tail, g, w, wg, wgt, bg, bgt):
    d = x_main.shape[1]
    tm = TOKEN_TILE
    n = x_main.shape[0] + tm
    nz = w.shape[1]
    ng = wgt.shape[0]
    full = lambda i: (0, 0)
    return pl.pallas_call(
        _inproj_kernel,
        grid=(n // tm,),
        in_specs=_stacked_specs(x_main, x_tail) + [
            pl.BlockSpec((1, d), full),
            pl.BlockSpec((d, nz), full),
            pl.BlockSpec((d, LANES), full),
            pl.BlockSpec((ng, d), full),
            pl.BlockSpec((1, LANES), full),
            pl.BlockSpec((ng, 1), full),
        ],
        out_specs=[
            pl.BlockSpec((tm, nz), lambda i: (i, 0)),
            pl.BlockSpec((tm, LANES), lambda i: (i, 0)),
            pl.BlockSpec((ng, tm), lambda i: (0, i)),
        ],
        out_shape=[
            jax.ShapeDtypeStruct((n, nz), F32),
            jax.ShapeDtypeStruct((n, LANES), F32),
            jax.ShapeDtypeStruct((ng, n), F32),
        ],
        compiler_params=_params("parallel"),
        name="inproj",
    )(x_main, x_tail, g, w, wg, wgt, bg, bgt)


def _pool_prompt_kernel(u_ref, pw_ref, ps_ref, y_ref, st_ref, ext_ref, *, tt, gw):
    t = pl.program_id(1)
    nt = pl.num_programs(1)
    halo = POOL_BUF + 1
    width = ext_ref.shape[1]

    @pl.when(t == 0)
    def _():
        ext_ref[0:halo, :] = jnp.zeros((halo, width), F32)

    u = u_ref[...]
    ext_ref[halo:halo + tt, :] = u
    pos = t * tt + lax.broadcasted_iota(I32, (tt, 1), 0)
    for g, w in enumerate(POOL_WINDOWS):
        c0 = g * gw
        acc = u[:, c0:c0 + gw]
        for j in range(1, w):
            acc = acc + ext_ref[halo - j:halo - j + tt, c0:c0 + gw]
        cnt = jnp.minimum(w, pos + 1).astype(F32)
        d = acc / cnt - u[:, c0:c0 + gw]
        y = _dot(d.astype(BF16), pw_ref[g]) * ps_ref[:, c0:c0 + gw]
        y_ref[:, c0:c0 + gw] = y.astype(BF16)

    @pl.when(t == nt - 1)
    def _():
        st_ref[...] = ext_ref[tt + 1:tt + halo, :]

    ext_ref[0:halo, :] = ext_ref[tt:tt + halo, :]


def _pool_prompt(z, pw, ps, batch, seq):
    width = ps.shape[1]
    gw = width // len(POOL_WINDOWS)
    tt = POOL_TIME_TILE
    nt = seq // tt
    return pl.pallas_call(
        functools.partial(_pool_prompt_kernel, tt=tt, gw=gw),
        grid=(batch, nt),
        in_specs=[
            pl.BlockSpec((tt, width), lambda b, t: (b * nt + t, 0)),
            pl.BlockSpec(pw.shape, lambda b, t: (0, 0, 0)),
            pl.BlockSpec((1, width), lambda b, t: (0, 0)),
        ],
        out_specs=[
            pl.BlockSpec((tt, width), lambda b, t: (b * nt + t, 0)),
            pl.BlockSpec((None, POOL_BUF, width), lambda b, t: (b, 0, 0)),
        ],
        out_shape=[
            jax.ShapeDtypeStruct((batch * seq, width), BF16),
            jax.ShapeDtypeStruct((batch, POOL_BUF, width), F32),
        ],
        scratch_shapes=[pltpu.VMEM((POOL_BUF + 1 + tt, width), F32)],
        compiler_params=_params("parallel", "arbitrary"),
        name="pool_prompt",
    )(z, pw, ps)


def _pool_step_kernel(u_ref, buf_ref, pw_ref, ps_ref, y_ref, nb_ref, *, gw):
    u = u_ref[...]
    for g, w in enumerate(POOL_WINDOWS):
        c0 = g * gw
        acc = u[:, c0:c0 + gw]
        for j in range(1, w):
            acc = acc + buf_ref[POOL_BUF - j, :, c0:c0 + gw]
        cnt = float(min(w, PAST_LEN + 1))
        d = acc / cnt - u[:, c0:c0 + gw]
        y = _dot(d.astype(BF16), pw_ref[g]) * ps_ref[:, c0:c0 + gw]
        y_ref[:, c0:c0 + gw] = y.astype(BF16)
    nb_ref[0:POOL_BUF - 1] = buf_ref[1:POOL_BUF]
    nb_ref[POOL_BUF - 1] = u


def _pool_step(z, buf_t, pw, ps, row0):
    _, batch, width = buf_t.shape
    gw = width // len(POOL_WINDOWS)
    return pl.pallas_call(
        functools.partial(_pool_step_kernel, gw=gw),
        grid=(1,),
        in_specs=[
            pl.BlockSpec((batch, width), lambda i: (row0 // batch, 0)),
            pl.BlockSpec(buf_t.shape, lambda i: (0, 0, 0)),
            pl.BlockSpec(pw.shape, lambda i: (0, 0, 0)),
            pl.BlockSpec((1, width), lambda i: (0, 0)),
        ],
        out_specs=[
            pl.BlockSpec((batch, width), lambda i: (0, 0)),
            pl.BlockSpec(buf_t.shape, lambda i: (0, 0, 0)),
        ],
        out_shape=[
            jax.ShapeDtypeStruct((batch, width), BF16),
            jax.ShapeDtypeStruct(buf_t.shape, F32),
        ],
        compiler_params=_params("arbitrary"),
        name="pool_step",
    )(z, buf_t, pw, ps)


def _mlstm_prompt_kernel(*refs, nh, dh, nseq):
    seq_in = [refs[5 * s:5 * s + 5] for s in range(nseq)]
    h_ref, c_out, n_out, m_out, c_s, n_s, m_s = refs[5 * nseq:]
    ci = pl.program_id(1)
    nc = pl.num_programs(1)
    ln = seq_in[0][0].shape[0]

    @pl.when(ci == 0)
    def _():
        c_s[...] = jnp.zeros(c_s.shape, F32)
        n_s[...] = jnp.zeros(n_s.shape, F32)
        m_s[...] = jnp.zeros(m_s.shape, F32)

    row = lax.broadcasted_iota(I32, (ln, ln), 0)
    col = lax.broadcasted_iota(I32, (ln, ln), 1)
    causal = col <= row
    eye = col == row
    lane = lax.broadcasted_iota(I32, (nh, ln), 1)

    def to_col(r):
        return jnp.sum(jnp.where(eye, r, 0.0), axis=1, keepdims=True)

    scale = dh ** -0.5
    pairs = [(s, h) for s in range(nseq) for h in range(nh)]
    c_old = {p: c_s[p[0], p[1]] for p in pairs}
    n_old = {p: n_s[p[0], p[1]] for p in pairs}
    m_old = {p: m_s[p[0], p[1]][:, 0:1] for p in pairs}
    gates = []
    for s in range(nseq):
        gt = seq_in[s][4][...]
        bc_all = jax.nn.log_sigmoid(gt[nh:2 * nh])
        sh = 1
        while sh < ln:
            bc_all = bc_all + jnp.where(lane >= sh, pltpu.roll(bc_all, sh, 1), 0.0)
            sh *= 2
        gates.append((gt[0:nh], bc_all))
    new_state = {}
    for s, h in pairs:
        q_ref, k_ref, v_ref, o_ref, _ = seq_in[s]
        li_all, bc_all = gates[s]
        sl = slice(h * dh, (h + 1) * dh)
        q = q_ref[:, sl]
        k = k_ref[:, sl] * scale
        v = v_ref[:, sl]
        qb, kb, vb = q.astype(BF16), k.astype(BF16), v.astype(BF16)
        li_r = li_all[h:h + 1]
        bc_r = bc_all[h:h + 1]
        bc_c = to_col(bc_r)
        m0, c, n = m_old[s, h], c_old[s, h], n_old[s, h]

        dmat = jnp.where(causal, bc_c - bc_r + li_r, -jnp.inf)
        inter = bc_c + m0
        m_row = jnp.maximum(inter, jnp.max(dmat, axis=1, keepdims=True))
        w_intra = jnp.exp(dmat - m_row)
        w_inter = jnp.exp(inter - m_row)
        sc = _dot_nt(qb, kb) * w_intra
        num = _dot(sc.astype(BF16), vb) + w_inter * _dot_nt(qb, c.astype(BF16))
        den = jnp.sum(sc, axis=1, keepdims=True) + w_inter * jnp.sum(q * n, axis=1, keepdims=True)
        hh = num / jnp.maximum(jnp.abs(den), jnp.exp(-m_row))
        h_ref[s, :, sl] = (hh * jax.nn.sigmoid(o_ref[:, sl])).astype(BF16)

        b_end = bc_r[:, ln - 1:ln]
        g_r = b_end - bc_r + li_r
        m_new = jnp.maximum(b_end + m0, jnp.max(g_r, axis=1, keepdims=True))
        wg_c = jnp.exp(to_col(g_r) - m_new)
        decay = jnp.exp(b_end + m0 - m_new)
        new_state[s, h] = (decay * c + _dot_tn((v * wg_c).astype(BF16), kb),
                           decay * n + jnp.sum(wg_c * k, axis=0, keepdims=True),
                           jnp.broadcast_to(m_new, (1, dh)))

    for (s, h), (c_new, n_new, m_new) in new_state.items():
        c_s[s, h] = c_new
        n_s[s, h] = n_new
        m_s[s, h] = m_new

    @pl.when(ci == nc - 1)
    def _():
        c_out[...] = c_s[...]
        n_out[...] = n_s[...]
        m_out[...] = m_s[...]


def _mlstm_prompt(z, gates_t, batch, seq, nh, dh):
    ln = MLSTM_CHUNK
    nc = seq // ln
    nseq = MLSTM_SEQ_PER_STEP
    width = nh * dh
    ng = gates_t.shape[0]
    rows_of = lambda s: (lambda b, c: (b * nseq + s) * nc + c)
    in_specs, operands = [], []
    for s in range(nseq):
        r = rows_of(s)
        for j in (1, 2, 3, 4):
            in_specs.append(pl.BlockSpec((ln, width), lambda b, c, r=r, j=j: (r(b, c), j)))
        in_specs.append(pl.BlockSpec((ng, ln), lambda b, c, r=r: (0, r(b, c))))
        operands += [z, z, z, z, gates_t]
    h_spec = pl.BlockSpec((None, nseq, ln, width), lambda b, c: (b, 0, c, 0))
    st = lambda a, b_: pl.BlockSpec((nseq, nh, a, b_), lambda b, c: (b, 0, 0, 0))
    return pl.pallas_call(
        functools.partial(_mlstm_prompt_kernel, nh=nh, dh=dh, nseq=nseq),
        grid=(batch // nseq, nc),
        in_specs=in_specs,
        out_specs=[h_spec, st(dh, dh), st(1, dh), st(1, dh)],
        out_shape=[
            jax.ShapeDtypeStruct((batch // nseq, nseq, seq, width), BF16),
            jax.ShapeDtypeStruct((batch, nh, dh, dh), F32),
            jax.ShapeDtypeStruct((batch, nh, 1, dh), F32),
            jax.ShapeDtypeStruct((batch, nh, 1, dh), F32),
        ],
        scratch_shapes=[pltpu.VMEM((nseq, nh, dh, dh), F32), pltpu.VMEM((nseq, nh, 1, dh), F32),
                        pltpu.VMEM((nseq, nh, 1, dh), F32)],
        compiler_params=_params("parallel", "arbitrary"),
        name="mlstm_prompt",
    )(*operands)


def _mlstm_step_kernel(q_ref, k_ref, v_ref, o_ref, li_ref, fp_ref, c_ref, n_ref, m_ref,
                       h_ref, c_out, n_out, m_out, *, nh, dh):
    eye = (lax.broadcasted_iota(I32, (1, dh, dh), 1) == lax.broadcasted_iota(I32, (1, dh, dh), 2))
    scale = dh ** -0.5
    for h in range(nh):
        sl = slice(h * dh, (h + 1) * dh)
        q = q_ref[:, :, sl]
        k = k_ref[:, :, sl] * scale
        v = v_ref[:, :, sl]
        c = c_ref[:, h]
        n = n_ref[:, h]
        m = m_ref[:, h]
        li = li_ref[:, h]
        lf = jax.nn.log_sigmoid(fp_ref[:, h])
        inter = lf + m
        m_row = jnp.maximum(inter, li)
        w_intra = jnp.exp(li - m_row)
        w_inter = jnp.exp(inter - m_row)
        sc = jnp.sum(q * k, axis=-1, keepdims=True) * w_intra
        v_c = jnp.sum(jnp.where(eye, v, 0.0), axis=-1, keepdims=True)
        num = sc * v_c + w_inter * jnp.sum(c * q, axis=-1, keepdims=True)
        den = sc + w_inter * jnp.sum(n * q, axis=-1, keepdims=True)
        h_c = num / jnp.maximum(jnp.abs(den), jnp.exp(-m_row))
        h_l = jnp.sum(jnp.where(eye, h_c, 0.0), axis=1, keepdims=True)
        h_ref[:, :, sl] = h_l * jax.nn.sigmoid(o_ref[:, :, sl])
        wg = jnp.exp(li - m_row)
        decay = jnp.exp(inter - m_row)
        c_out[:, h] = decay * c + (v_c * wg) * k
        n_out[:, h] = decay * n + wg * k
        m_out[:, h] = m_row


def _mlstm_step(z3, li, fp, c, n, m, nh, dh):
    batch = c.shape[0]
    bb = MLSTM_STEP_BATCH
    width = nh * dh
    blk = lambda j: pl.BlockSpec((bb, 1, width), lambda i: (i, 0, j))
    st4 = lambda a, b: pl.BlockSpec((bb, nh, a, b), lambda i: (i, 0, 0, 0))
    return pl.pallas_call(
        functools.partial(_mlstm_step_kernel, nh=nh, dh=dh),
        grid=(batch // bb,),
        in_specs=[blk(1), blk(2), blk(3), blk(4), st4(1, 1), st4(1, 1),
                  st4(dh, dh), st4(1, dh), st4(1, 1)],
        out_specs=[pl.BlockSpec((bb, 1, width), lambda i: (i, 0, 0)),
                   st4(dh, dh), st4(1, dh), st4(1, 1)],
        out_shape=[
            jax.ShapeDtypeStruct((batch, 1, width), F32),
            jax.ShapeDtypeStruct((batch, nh, dh, dh), F32),
            jax.ShapeDtypeStruct((batch, nh, 1, dh), F32),
            jax.ShapeDtypeStruct((batch, nh, 1, 1), F32),
        ],
        compiler_params=_params("parallel"),
        name="mlstm_step",
    )(z3, z3, z3, z3, li, fp, c, n, m)


def _split_bf16(a):
    hi = a.astype(BF16)
    return hi, (a - hi.astype(F32)).astype(BF16)


def _route(hn, wr_ref, br_ref, tril_ref, carry_ref, rinfo_ref, rinfo_t_ref):
    ngrp, epg = MOE_GROUPS, MOE_EXPERTS_PER_GROUP
    h_hi, h_lo = _split_bf16(hn)
    w_hi, w_lo = wr_ref[0], wr_ref[1]
    logits = _dot(h_hi, w_hi) + (_dot(h_lo, w_hi) + _dot(h_hi, w_lo)) + br_ref[...]
    tm = logits.shape[0]
    lane = lax.broadcasted_iota(I32, (tm, LANES), 1)
    neg = -jnp.inf

    def first_max(x):
        mx = jnp.max(x, axis=1, keepdims=True)
        idx = jnp.min(jnp.where(x == mx, lane, LANES), axis=1, keepdims=True)
        return mx, idx

    is_grp = lane < ngrp
    gmax, gsel = first_max(jnp.where(is_grp, logits, neg))
    g_w = 1.0 / jnp.sum(jnp.where(is_grp, jnp.exp(logits - gmax), 0.0), axis=1, keepdims=True)
    lo = ngrp + gsel * epg
    el = jnp.where((lane >= lo) & (lane < lo + epg), logits, neg)
    v1, i1 = first_max(el)
    v2, i2 = first_max(jnp.where(lane == i1, neg, el))
    e2 = jnp.exp(v2 - v1)
    w1 = g_w / (1.0 + e2)
    w2 = g_w * e2 / (1.0 + e2)
    eid1 = i1 - ngrp
    eid2 = i2 - ngrp

    hit1 = lane == eid1
    hit2 = lane == eid2
    onehot = jnp.where(hit1 | hit2, 1.0, 0.0)
    carry = carry_ref[...]
    prefix = _dot(tril_ref[...], onehot.astype(BF16)) + carry
    rank1 = jnp.sum(jnp.where(hit1, prefix, 0.0), axis=1, keepdims=True)
    rank2 = jnp.sum(jnp.where(hit2, prefix, 0.0), axis=1, keepdims=True)
    carry_ref[...] = carry + jnp.sum(onehot, axis=0, keepdims=True)

    cols = (eid1.astype(F32), eid2.astype(F32), w1, w2, rank1, rank2)
    info = jnp.zeros((tm, LANES), F32)
    for j, cval in enumerate(cols):
        info = jnp.where(lane == j, cval, info)
    rinfo_ref[...] = info
    pick = jnp.where(lax.broadcasted_iota(I32, (SUBLANES, LANES), 0)
                     == lax.broadcasted_iota(I32, (SUBLANES, LANES), 1), 1.0, 0.0).astype(BF16)
    parts = [_dot_nt(pick, p) for p in _split3(info)]
    rinfo_t_ref[...] = parts[0] + (parts[1] + parts[2])


def _outproj_route_kernel(xm_ref, xt_ref, pm_ref, pt_ref, mm_ref, mt_ref,
                          wo_ref, g_ref, wr_ref, br_ref, tril_ref,
                          x1_ref, hn_ref, rinfo_ref, rinfo_t_ref, cnt_ref, carry_ref):
    @pl.when(pl.program_id(0) == 0)
    def _():
        carry_ref[...] = jnp.zeros(carry_ref.shape, F32)

    half = pm_ref.shape[1]
    mix = (_dot(_stacked(pm_ref, pt_ref), wo_ref[0:half, :])
           + _dot(_stacked(mm_ref, mt_ref), wo_ref[half:2 * half, :]))
    x1 = _stacked(xm_ref, xt_ref) + mix
    x1_ref[...] = x1
    hn = _rms(x1, g_ref[...])
    hn_ref[...] = hn
    _route(hn, wr_ref, br_ref, tril_ref, carry_ref, rinfo_ref, rinfo_t_ref)
    cnt_ref[...] = carry_ref[...]


def _glu_route_kernel(x_ref, ym_ref, yt_ref, wglu_ref, g_ref, wr_ref, br_ref, tril_ref,
                      x1_ref, hn_ref, rinfo_ref, rinfo_t_ref, cnt_ref, carry_ref):
    @pl.when(pl.program_id(0) == 0)
    def _():
        carry_ref[...] = jnp.zeros(carry_ref.shape, F32)

    d = x_ref.shape[1]
    ag = _dot(jax.nn.gelu(_stacked(ym_ref, yt_ref)).astype(BF16), wglu_ref[...])
    x1 = x_ref[...] + ag[:, 0:d] * jax.nn.sigmoid(ag[:, d:2 * d])
    x1_ref[...] = x1
    hn = _rms(x1, g_ref[...])
    hn_ref[...] = hn
    _route(hn, wr_ref, br_ref, tril_ref, carry_ref, rinfo_ref, rinfo_t_ref)
    cnt_ref[...] = carry_ref[...]


def _mix_route(kernel, name, n, row_specs, rows, w, g, wr, br, tril):
    d = g.shape[1]
    tm = TOKEN_TILE
    full = lambda i: (0, 0)
    return pl.pallas_call(
        kernel,
        grid=(n // tm,),
        in_specs=row_specs + [
            pl.BlockSpec(w.shape, full),
            pl.BlockSpec((1, d), full),
            pl.BlockSpec((2, d, LANES), lambda i: (0, 0, 0)),
            pl.BlockSpec((1, LANES), full),
            pl.BlockSpec((tm, tm), full),
        ],
        out_specs=[
            pl.BlockSpec((tm, d), lambda i: (i, 0)),
            pl.BlockSpec((tm, d), lambda i: (i, 0)),
            pl.BlockSpec((tm, LANES), lambda i: (i, 0)),
            pl.BlockSpec((SUBLANES, tm), lambda i: (0, i)),
            pl.BlockSpec((1, LANES), full),
        ],
        out_shape=[
            jax.ShapeDtypeStruct((n, d), F32),
            jax.ShapeDtypeStruct((n, d), F32),
            jax.ShapeDtypeStruct((n, LANES), F32),
            jax.ShapeDtypeStruct((SUBLANES, n), F32),
            jax.ShapeDtypeStruct((1, LANES), F32),
        ],
        scratch_shapes=[pltpu.VMEM((1, LANES), F32)],
        compiler_params=_params("arbitrary"),
        name=name,
    )(*rows, w, g, wr, br, tril)


def _index_copy(pos_hbm, idx_s, sem_i, tile, slot):
    return pltpu.make_async_copy(pos_hbm.at[tile], idx_s.at[slot], sem_i.at[slot])


def _dispatch_kernel(pos_hbm, hn_ref, xs_hbm, idx_s, sem_i, sem_d):
    i = pl.program_id(0)
    nt = pl.num_programs(0)
    tm = hn_ref.shape[0]
    slot = i % 2

    @pl.when(i == 0)
    def _():
        _index_copy(pos_hbm, idx_s, sem_i, 0, 0).start()

    _index_copy(pos_hbm, idx_s, sem_i, i, slot).wait()

    @pl.when(i + 1 < nt)
    def _():
        _index_copy(pos_hbm, idx_s, sem_i, i + 1, 1 - slot).start()

    def issue(r, carry):
        row = hn_ref.at[pl.ds(r, 1)]
        pltpu.make_async_copy(row, xs_hbm.at[pl.ds(idx_s[slot, 0, r], 1)], sem_d).start(priority=0)
        pltpu.make_async_copy(row, xs_hbm.at[pl.ds(idx_s[slot, 0, tm + r], 1)], sem_d).start(priority=1)
        return carry

    lax.fori_loop(0, tm, issue, 0, unroll=DMA_ISSUE_UNROLL)
    whole = pltpu.make_async_copy(hn_ref, xs_hbm.at[pl.ds(0, tm)], sem_d)
    whole.wait()
    whole.wait()


def _dispatch(pos_tiles, hn):
    n, d = hn.shape
    tm = TOKEN_TILE
    return pl.pallas_call(
        _dispatch_kernel,
        grid=(n // tm,),
        in_specs=[pl.BlockSpec(memory_space=pl.ANY), pl.BlockSpec((tm, d), lambda i: (i, 0))],
        out_specs=pl.BlockSpec(memory_space=pl.ANY),
        out_shape=jax.ShapeDtypeStruct((2 * n, d), F32),
        scratch_shapes=[pltpu.SMEM((2, 1, 2 * tm), I32), pltpu.SemaphoreType.DMA((2,)),
                        pltpu.SemaphoreType.DMA],
        compiler_params=_params("arbitrary"),
        name="moe_dispatch",
    )(pos_tiles, hn)


def _moe_kernel(vt_ref, ve_ref, von_ref, lo_ref, hi_ref, xs_ref, wg_ref, wu_ref, wd_ref,
                eo_ref, wgb, wub, wdb):
    v = pl.program_id(0)
    tr = xs_ref.shape[0]
    prev = jnp.maximum(v - 1, 0)
    e = ve_ref[v]
    new_expert = jnp.logical_or(v == 0, e != ve_ref[prev])
    first_visit = jnp.logical_or(v == 0, vt_ref[v] != vt_ref[prev])

    @pl.when(von_ref[v] == 1)
    def _():
        @pl.when(new_expert)
        def _():
            wgb[...] = wg_ref[...].astype(BF16)
            wub[...] = wu_ref[...].astype(BF16)
            wdb[...] = wd_ref[...].astype(BF16)

        x = xs_ref[...].astype(BF16)
        act = jax.nn.silu(_dot(x, wgb[...])) * _dot(x, wub[...])
        row = vt_ref[v] * tr + lax.broadcasted_iota(I32, (tr, 1), 0)
        act = jnp.where((row >= lo_ref[e]) & (row < hi_ref[e]), act, 0.0)
        res = _dot(act.astype(BF16), wdb[...])

        @pl.when(first_visit)
        def _():
            eo_ref[...] = res

        @pl.when(jnp.logical_not(first_visit))
        def _():
            eo_ref[...] += res


def _moe_experts(vt, ve, von, lo, hi, xs, wg, wu, wd, layer):
    nv = vt.shape[0]
    rows, d = xs.shape
    hid = wg.shape[3]
    tr = EXPERT_ROW_TILE
    wspec = lambda a, b: pl.BlockSpec((None, None, a, b),
                                      lambda v, vt, ve, von, lo, hi: (layer, ve[v], 0, 0))
    grid_spec = pltpu.PrefetchScalarGridSpec(
        num_scalar_prefetch=5,
        grid=(nv,),
        in_specs=[
            pl.BlockSpec((tr, d), lambda v, vt, ve, von, lo, hi: (vt[v], 0)),
            wspec(d, hid), wspec(d, hid), wspec(hid, d),
        ],
        out_specs=pl.BlockSpec((tr, d), lambda v, vt, ve, von, lo, hi: (vt[v], 0)),
        scratch_shapes=[pltpu.VMEM((d, hid), BF16), pltpu.VMEM((d, hid), BF16),
                        pltpu.VMEM((hid, d), BF16)],
    )
    return pl.pallas_call(
        _moe_kernel,
        grid_spec=grid_spec,
        out_shape=jax.ShapeDtypeStruct((rows, d), F32),
        compiler_params=_params("arbitrary"),
        name="moe_experts",
    )(vt, ve, von, lo, hi, xs, wg, wu, wd)


def _combine_kernel(pos_hbm, eo_hbm, x_ref, rinfo_ref, g_ref, o1_ref, o2_ref, idx_s, a_buf, b_buf,
                    sem_i, sem_a, sem_b, *, last_layer):
    i = pl.program_id(0)
    nt = pl.num_programs(0)
    tm = a_buf.shape[1]
    slot = i % 2

    def gathers(s):
        def issue(r, carry):
            pltpu.make_async_copy(eo_hbm.at[pl.ds(idx_s[s, 0, r], 1)],
                                  a_buf.at[s, pl.ds(r, 1)], sem_a.at[s]).start(priority=0)
            pltpu.make_async_copy(eo_hbm.at[pl.ds(idx_s[s, 0, tm + r], 1)],
                                  b_buf.at[s, pl.ds(r, 1)], sem_b.at[s]).start(priority=1)
            return carry

        lax.fori_loop(0, tm, issue, 0, unroll=DMA_ISSUE_UNROLL)

    @pl.when(i == 0)
    def _():
        first = _index_copy(pos_hbm, idx_s, sem_i, 0, 0)
        first.start()
        first.wait()
        gathers(0)

        @pl.when(nt > 1)
        def _():
            _index_copy(pos_hbm, idx_s, sem_i, 1, 1).start()

    @pl.when(i + 1 < nt)
    def _():
        _index_copy(pos_hbm, idx_s, sem_i, i + 1, 1 - slot).wait()
        gathers(1 - slot)

    @pl.when(i + 2 < nt)
    def _():
        _index_copy(pos_hbm, idx_s, sem_i, i + 2, slot).start()

    pltpu.make_async_copy(eo_hbm.at[pl.ds(0, tm)], a_buf.at[slot], sem_a.at[slot]).wait()
    pltpu.make_async_copy(eo_hbm.at[pl.ds(0, tm)], b_buf.at[slot], sem_b.at[slot]).wait()
    info = rinfo_ref[...]
    x2 = x_ref[...] + (info[:, 2:3] * a_buf[slot] + info[:, 3:4] * b_buf[slot])
    hn = _rms(x2, g_ref[...])
    if last_layer:
        @pl.when(i < nt - 1)
        def _():
            o1_ref[...] = hn

        @pl.when(i == nt - 1)
        def _():
            o2_ref[...] = hn
    else:
        o1_ref[...] = x2
        o2_ref[...] = hn


def _combine(pos_tiles, eo, x, rinfo, g, last_layer):
    n, d = x.shape
    tm = TOKEN_TILE
    nt = n // tm
    row = pl.BlockSpec((tm, d), lambda i: (i, 0))
    if last_layer:
        out_specs = [pl.BlockSpec((tm, d), lambda i: (jnp.minimum(i, nt - 2), 0)),
                     pl.BlockSpec((tm, d), lambda i: (0, 0))]
        out_shape = [jax.ShapeDtypeStruct((n - tm, d), F32), jax.ShapeDtypeStruct((tm, d), F32)]
    else:
        out_specs = [row, row]
        out_shape = [jax.ShapeDtypeStruct((n, d), F32), jax.ShapeDtypeStruct((n, d), F32)]
    return pl.pallas_call(
        functools.partial(_combine_kernel, last_layer=last_layer),
        grid=(nt,),
        in_specs=[
            pl.BlockSpec(memory_space=pl.ANY),
            pl.BlockSpec(memory_space=pl.ANY),
            row,
            pl.BlockSpec((tm, LANES), lambda i: (i, 0)),
            pl.BlockSpec((1, d), lambda i: (0, 0)),
        ],
        out_specs=out_specs,
        out_shape=out_shape,
        scratch_shapes=[
            pltpu.SMEM((2, 1, 2 * tm), I32),
            pltpu.VMEM((2, tm, d), F32),
            pltpu.VMEM((2, tm, d), F32),
            pltpu.SemaphoreType.DMA((2,)),
            pltpu.SemaphoreType.DMA((2,)),
            pltpu.SemaphoreType.DMA((2,)),
        ],
        compiler_params=_params("arbitrary"),
        name="moe_combine",
    )(pos_tiles, eo, x, rinfo, g)


def _moe(x1, hn, rinfo, rinfo_t, counts, wg, wu, wd, layer, next_gain, last_layer):
    n = x1.shape[0]
    ne = wg.shape[1]
    tr = EXPERT_ROW_TILE
    tm = TOKEN_TILE
    nv = (2 * n) // tr + ne - 1
    eid = rinfo_t[0:2].astype(I32)
    rank = rinfo_t[4:6].astype(I32)
    cnt = counts[0, :ne].astype(I32)
    seg_end = jnp.cumsum(cnt)
    seg_start = seg_end - cnt
    experts = jnp.arange(ne, dtype=I32)
    pos = rank
    for e in range(ne):
        pos = pos + jnp.where(eid == e, seg_start[e], 0)
    pos_tiles = pos.reshape(2, n // tm, tm).transpose(1, 0, 2).reshape(n // tm, 1, 2 * tm)

    first_tile = seg_start // tr
    tiles_e = jnp.where(cnt > 0, (seg_end - 1) // tr - first_tile + 1, 0)
    v_end = jnp.cumsum(tiles_e)
    v_start = v_end - tiles_e
    total = v_end[-1]
    vis = jnp.arange(nv, dtype=I32)
    vc = jnp.minimum(vis, jnp.maximum(total - 1, 0))
    ve = jnp.sum((vc[:, None] >= v_end[None, :]).astype(I32), axis=1)
    pick = lambda tab: jnp.sum(jnp.where(ve[:, None] == experts, tab, 0), axis=1)
    vt = pick(first_tile) + (vc - pick(v_start))
    von = (vis < total).astype(I32)

    xs = _dispatch(pos_tiles, hn)
    eo = _moe_experts(vt, ve, von, seg_start, seg_end, xs, wg, wu, wd, layer)
    return _combine(pos_tiles, eo, x1, rinfo, next_gain, last_layer)


def _cis(log_mag, ang):
    mag = jnp.exp(log_mag)
    return mag * jnp.cos(ang), mag * jnp.sin(ang)


def _split3(a):
    p1 = a.astype(BF16)
    r1 = a - p1.astype(F32)
    p2 = r1.astype(BF16)
    return p1, p2, (r1 - p2.astype(F32)).astype(BF16)


def _select_dot(a, sel, sel_first=False):
    sel = sel.astype(BF16)
    parts = [(_dot(sel, p) if sel_first else _dot(p, sel)) for p in _split3(a)]
    return parts[0] + (parts[1] + parts[2])


def _dot3(a, b):
    a_hi, a_lo = _split_bf16(a)
    b_hi, b_lo = _split_bf16(b)
    return _dot(a_hi, b_hi) + (_dot(a_lo, b_hi) + _dot(a_hi, b_lo))


def _s5_prep_kernel(lsc_ref, lsr_ref, lrc_ref, lic_ref, lrr_ref, lir_ref, ctre_ref, ctim_ref,
                    btre_ref, btim_ref, d_ref,
                    t_ref, wre_ref, wim_ref, cyre_ref, cyim_ref, apr_ref, api_ref,
                    bbr_ref, bbi_ref, lbr_ref, lbi_ref, *, gw, gpb, sub):
    blk = gpb * gw
    wide = sub * blk
    nsl = lrc_ref.shape[1]
    ns = nsl // gpb
    sh_blk, sh_gw, sh_ns = blk.bit_length() - 1, gw.bit_length() - 1, ns.bit_length() - 1
    dt_c, dt_r = jnp.exp(lsc_ref[0]), jnp.exp(lsr_ref[0])
    ldt_c_re, ldt_c_im = lrc_ref[0] * dt_c, lic_ref[0] * dt_c
    ldt_r_re, ldt_r_im = lrr_ref[0] * dt_r, lir_ref[0] * dt_r

    lane_w = lax.broadcasted_iota(I32, (1, wide), 1)
    spread = jnp.where((lax.broadcasted_iota(I32, (gw, wide), 1) & (gw - 1))
                       == lax.broadcasted_iota(I32, (gw, wide), 0), 1.0, 0.0)
    cre = _select_dot(ctre_ref[0], spread)
    cim = _select_dot(ctim_ref[0], spread)
    same = (jnp.right_shift(lax.broadcasted_iota(I32, (nsl, 1), 0), sh_ns)
            == (jnp.right_shift(lane_w, sh_gw) & (gpb - 1)))
    assert 2 * blk == LANES
    tau = lax.broadcasted_iota(I32, (1, LANES), 1).astype(F32)
    pw_re, pw_im = _cis(tau * ldt_c_re, tau * ldt_c_im)
    low = lax.broadcasted_iota(I32, (nsl, LANES), 1) < blk

    def spread_pow(p, first):
        col = lambda t: jnp.broadcast_to(p[:, t:t + 1], (nsl, LANES))
        return jnp.concatenate([jnp.where(low, col(first + 2 * m), col(first + 2 * m + 1))
                                for m in range(sub // 2)], axis=1)

    def c_lam_pow(first):
        pr, pi = spread_pow(pw_re, first), spread_pow(pw_im, first)
        return (jnp.where(same, pr * cre - pi * cim, 0.0),
                jnp.where(same, -(pr * cim + pi * cre), 0.0))

    clr0, cli0 = c_lam_pow(0)
    clr1, cli1 = c_lam_pow(1)
    cyre_ref[0] = clr1.astype(BF16)
    cyim_ref[0] = cli1.astype(BF16)

    lbr, lbi = _cis(ldt_r_re, ldt_r_im)
    lbr_ref[0] = lbr
    lbi_ref[0] = lbi
    lr, li = lrr_ref[0], lir_ref[0]
    nr, ni = lbr - 1.0, lbi
    den = lr * lr + li * li
    fr = (nr * lr + ni * li) / den
    fi = (ni * lr - nr * li) / den
    bre, bim = btre_ref[0], btim_ref[0]
    bbr = fr * bre - fi * bim
    bbi = fr * bim + fi * bre
    bbr_ref[0] = bbr
    bbi_ref[0] = bbi

    lane_t = lax.broadcasted_iota(I32, (gw, wide), 1)
    chan = lax.broadcasted_iota(I32, (gw, wide), 0)
    for g in range(gpb):
        ps = slice(g * ns, (g + 1) * ns)
        r = _dot3(bbr[:, ps], clr0[ps, :]) + _dot3(bbi[:, ps], cli0[ps, :])
        r = r + jnp.where(lane_t == g * gw + chan, d_ref[0][:, g:g + 1], 0.0)
        for j in range(sub):
            tb = r if j == 0 else jnp.where(lane_t >= blk * j, pltpu.roll(r, blk * j, 1), 0.0)
            r0 = j * blk + g * gw
            t_ref[0, r0:r0 + gw, :] = tb.astype(BF16)

    rows = lax.broadcasted_iota(I32, (wide, 1), 0)
    spread_t = jnp.where((lax.broadcasted_iota(I32, (wide, gw), 0) & (gw - 1))
                         == lax.broadcasted_iota(I32, (wide, gw), 1), 1.0, 0.0)
    bbr_t = _select_dot(bbr, spread_t, sel_first=True)
    bbi_t = _select_dot(bbi, spread_t, sel_first=True)
    same_w = ((jnp.right_shift(rows, sh_gw) & (gpb - 1))
              == jnp.right_shift(lax.broadcasted_iota(I32, (1, nsl), 1), sh_ns))
    rj = ((sub - 1) - lax.broadcasted_iota(I32, (sub, 1), 0)).astype(F32)
    q_re, q_im = _cis(rj * ldt_r_re, rj * ldt_r_im)
    per_step = lambda q: jnp.concatenate(
        [jnp.broadcast_to(q[j:j + 1, :], (blk, nsl)) for j in range(sub)], axis=0)
    pr, pi = per_step(q_re), per_step(q_im)
    wre_ref[0] = jnp.where(same_w, pr * bbr_t - pi * bbi_t, 0.0).astype(BF16)
    wim_ref[0] = jnp.where(same_w, pr * bbi_t + pi * bbr_t, 0.0).astype(BF16)

    nlev = apr_ref.shape[1]
    pw = (sub * jnp.left_shift(1, lax.broadcasted_iota(I32, (nlev, 1), 0))).astype(F32)
    apr, api = _cis(pw * ldt_r_re, pw * ldt_r_im)
    apr_ref[0] = apr
    api_ref[0] = api


def _s5_prep(log_step, lam_re, lam_im, b_re, b_im, c_re, c_im, d_skip):
    ng, ns = lam_re.shape
    gw = b_re.shape[2]
    sub, gpb = S5_SUB, S5_GROUPS_PER_BLOCK
    nblk = ng // gpb
    nsl = gpb * ns
    wide = sub * gpb * gw
    ls = jnp.repeat(log_step, ns)
    ct = lambda c: jnp.transpose(c, (0, 2, 1)).reshape(nblk, nsl, gw)
    bt = lambda b: jnp.transpose(b.reshape(nblk, gpb, ns, gw), (0, 3, 1, 2)).reshape(nblk, gw, nsl)
    ins = [ls.reshape(nblk, nsl, 1), ls.reshape(nblk, 1, nsl),
           lam_re.reshape(nblk, nsl, 1), lam_im.reshape(nblk, nsl, 1),
           lam_re.reshape(nblk, 1, nsl), lam_im.reshape(nblk, 1, nsl),
           ct(c_re), ct(c_im), bt(b_re), bt(b_im),
           jnp.transpose(d_skip.reshape(nblk, gpb, gw), (0, 2, 1))]
    spec = lambda a: pl.BlockSpec((1,) + a.shape[1:], lambda g: (g, 0, 0))
    outs = [
        jax.ShapeDtypeStruct((nblk, wide, wide), BF16),
        jax.ShapeDtypeStruct((nblk, wide, nsl), BF16),
        jax.ShapeDtypeStruct((nblk, wide, nsl), BF16),
        jax.ShapeDtypeStruct((nblk, nsl, wide), BF16),
        jax.ShapeDtypeStruct((nblk, nsl, wide), BF16),
        jax.ShapeDtypeStruct((nblk, SUBLANES, nsl), F32),
        jax.ShapeDtypeStruct((nblk, SUBLANES, nsl), F32),
        jax.ShapeDtypeStruct((nblk, gw, nsl), F32),
        jax.ShapeDtypeStruct((nblk, gw, nsl), F32),
        jax.ShapeDtypeStruct((nblk, 1, nsl), F32),
        jax.ShapeDtypeStruct((nblk, 1, nsl), F32),
    ]
    return pl.pallas_call(
        functools.partial(_s5_prep_kernel, gw=gw, gpb=gpb, sub=sub),
        grid=(nblk,),
        in_specs=[spec(a) for a in ins],
        out_specs=[spec(o) for o in outs],
        out_shape=outs,
        compiler_params=_params("parallel"),
        name="s5_prep",
    )(*ins)


def _s5_prompt_kernel(x_ref, t_ref, wre_ref, wim_ref, cyre_ref, cyim_ref, apr_ref, api_ref,
                      y_ref, hre_out, him_out, u_s, y4_s, sre_s, sim_s, *, nb, seq, sub):
    nk = seq // sub
    rows = nb * nk
    hl = LANES // 2
    pad = nk // 2
    nlev = nk.bit_length() - 1
    nsl = sre_s.shape[1]
    low = lax.broadcasted_iota(I32, (nk, LANES), 1) < hl
    kidx = lax.broadcasted_iota(I32, (rows, 1), 0) & (nk - 1)
    sre_s[0:pad, :] = jnp.zeros((pad, nsl), F32)
    sim_s[0:pad, :] = jnp.zeros((pad, nsl), F32)

    def step_rows(b, j):
        return (pl.ds(b * seq + j, nk, stride=sub), slice(None))

    for b in range(nb):
        rs = slice(b * nk, (b + 1) * nk)
        for m in range(sub // 2):
            ls = slice(m * LANES, (m + 1) * LANES)
            s0 = x_ref[step_rows(b, 2 * m)]
            s1 = x_ref[step_rows(b, 2 * m + 1)]
            u_s[0, rs, ls] = jnp.where(low, s0, pltpu.roll(s1, hl, 1)).astype(BF16)
            u_s[1, rs, ls] = jnp.where(low, pltpu.roll(s0, hl, 1), s1).astype(BF16)

    live = slice(pad, pad + rows)
    for half in range(2):
        u = u_s[half]
        hre = _dot(u, wre_ref[half])
        him = _dot(u, wim_ref[half])
        for i in range(nlev):
            s = 1 << i
            sre_s[live, :] = hre
            sim_s[live, :] = him
            pre = sre_s[pad - s:pad - s + rows, :]
            pim = sim_s[pad - s:pad - s + rows, :]
            ar = apr_ref[half, i:i + 1, :]
            ai = api_ref[half, i:i + 1, :]
            ok = kidx >= s
            hre = hre + jnp.where(ok, ar * pre - ai * pim, 0.0)
            him = him + jnp.where(ok, ar * pim + ai * pre, 0.0)
        sre_s[live, :] = hre
        sim_s[live, :] = him
        ls = slice(half * nsl, (half + 1) * nsl)
        for b in range(nb):
            last = pad + (b + 1) * nk - 1
            hre_out[b:b + 1, ls] = sre_s[last:last + 1, :]
            him_out[b:b + 1, ls] = sim_s[last:last + 1, :]
        started = kidx >= 1
        hpre = jnp.where(started, sre_s[pad - 1:pad - 1 + rows, :], 0.0).astype(BF16)
        hpim = jnp.where(started, sim_s[pad - 1:pad - 1 + rows, :], 0.0).astype(BF16)
        y4_s[half] = _dot(u, t_ref[half]) + _dot(hpre, cyre_ref[half]) + _dot(hpim, cyim_ref[half])

    for b in range(nb):
        rs = slice(b * nk, (b + 1) * nk)
        for m in range(sub // 2):
            ls = slice(m * LANES, (m + 1) * LANES)
            ca = y4_s[0, rs, ls]
            cb = y4_s[1, rs, ls]
            y_ref[step_rows(b, 2 * m)] = jnp.where(low, ca, pltpu.roll(cb, hl, 1))
            y_ref[step_rows(b, 2 * m + 1)] = jnp.where(low, pltpu.roll(ca, hl, 1), cb)


def _s5_prompt(h, prep, batch, seq):
    d = h.shape[1]
    tmat, wre, wim, cyre, cyim, apr, api = prep[:7]
    nblk, wide, nsl = wre.shape
    sub, nb = S5_SUB, S5_BATCH_PER_STEP
    nk = seq // sub
    assert nk & (nk - 1) == 0 and nk.bit_length() - 1 <= apr.shape[1]
    ntile = d // LANES
    assert nblk == 2 * ntile
    rows = nb * nk
    wspec = lambda a: pl.BlockSpec((2,) + a.shape[1:], lambda t, b: (t, 0, 0))
    xspec = pl.BlockSpec((nb * seq, LANES), lambda t, b: (b, t))
    sspec = pl.BlockSpec((None, nb, 2 * nsl), lambda t, b: (b, 0, t))
    return pl.pallas_call(
        functools.partial(_s5_prompt_kernel, nb=nb, seq=seq, sub=sub),
        grid=(ntile, batch // nb),
        in_specs=[xspec, wspec(tmat), wspec(wre), wspec(wim), wspec(cyre), wspec(cyim),
                  wspec(apr), wspec(api)],
        out_specs=[xspec, sspec, sspec],
        out_shape=[
            jax.ShapeDtypeStruct((batch * seq, d), F32),
            jax.ShapeDtypeStruct((batch // nb, nb, ntile * 2 * nsl), F32),
            jax.ShapeDtypeStruct((batch // nb, nb, ntile * 2 * nsl), F32),
        ],
        scratch_shapes=[
            pltpu.VMEM((2, rows, wide), BF16),
            pltpu.VMEM((2, rows, wide), F32),
            pltpu.VMEM((nk // 2 + rows, nsl), F32),
            pltpu.VMEM((nk // 2 + rows, nsl), F32),
        ],
        compiler_params=_params("parallel", "parallel"),
        name="s5_prompt",
    )(h, tmat, wre, wim, cyre, cyim, apr, api)


def _s5_step_kernel(u_ref, h_ref, hsw_ref, bb_ref, la_ref, lb_ref, cc_ref, d_ref, hn_ref, y_ref):
    for g in range(u_ref.shape[0]):
        u = u_ref[g]
        hn = (la_ref[g] * h_ref[g] + lb_ref[g] * hsw_ref[g]
              + jnp.dot(u, bb_ref[g], precision=HIGHEST, preferred_element_type=F32))
        hn_ref[g] = hn
        y_ref[g] = (jnp.dot(hn, cc_ref[g], precision=HIGHEST, preferred_element_type=F32)
                    + d_ref[g] * u)


def _s5_step(u_p, h_cat, h_swp, bb_cat, la, lb, cc, d_p):
    ng = u_p.shape[0]
    gb = SUBLANES
    spec = lambda a: pl.BlockSpec((gb,) + a.shape[1:], lambda g: (g, 0, 0))
    ins = [u_p, h_cat, h_swp, bb_cat, la, lb, cc, d_p]
    outs = [jax.ShapeDtypeStruct(h_cat.shape, F32), jax.ShapeDtypeStruct(u_p.shape, F32)]
    return pl.pallas_call(
        _s5_step_kernel,
        grid=(ng // gb,),
        in_specs=[spec(a) for a in ins],
        out_specs=[spec(o) for o in outs],
        out_shape=outs,
        compiler_params=_params("parallel"),
        name="s5_step",
    )(*ins)


def _pad_to(a, axis, size):
    pad = [(0, 0)] * a.ndim
    pad[axis] = (0, size - a.shape[axis])
    return jnp.pad(a, pad)


def kernel(x_prompt, x_sample, state_pool, state_mlstm_c, state_mlstm_n, state_mlstm_m, state_s5_re, state_s5_im, norm_mix, norm_ffn, norm_final, w_in_ab, b_gates, pool_w, pool_scale, w_out_ab, s5_lam_re, s5_lam_im, s5_log_step, s5_b_re, s5_b_im, s5_c_re, s5_c_im, s5_d, w_glu, moe_w_group, moe_b_group, moe_w_expert, moe_b_expert, moe_w_gate, moe_w_up, moe_w_down):
    bp, tp, d = x_prompt.shape
    bs = x_sample.shape[0]
    n_p = bp * tp
    tm = TOKEN_TILE
    assert n_p % tm == 0 and bs <= tm
    n_fill = tm - bs
    nh, dh = state_mlstm_c.shape[2], state_mlstm_c.shape[3]
    pool_width = state_pool.shape[3]
    ml_width = nh * dh
    n_main = pool_width + 4 * ml_width
    n_gates = 2 * nh
    ngrp_s5, n_state = s5_lam_re.shape[1], s5_lam_re.shape[2]
    gw_s5 = d // ngrp_s5

    n = n_p + tm

    def tail_tile(sample_rows, dtype):
        return _pad_to(sample_rows.astype(dtype), 0, tm)

    x_main = x_prompt.reshape(n_p, d)
    x_tail = tail_tile(x_sample.reshape(bs, d), F32)
    tril = jnp.tril(jnp.ones((tm, tm), BF16), -1)

    def router_weights(l):
        wr = _pad_to(jnp.concatenate([moe_w_group[l], moe_w_expert[l]], axis=1), 1, LANES)
        br = jnp.concatenate([moe_b_group[l], moe_b_expert[l]])[None, :]
        hi = wr.astype(BF16)
        lo = (wr - hi.astype(F32)).astype(BF16)
        return jnp.stack([hi, lo]), _pad_to(br, 1, LANES)

    w_in = w_in_ab[0]
    w_g = w_in[:, n_main:]
    z, gates, gates_t = _inproj(
        x_main, x_tail, norm_mix[0][None, :], w_in[:, :n_main].astype(BF16),
        _pad_to(w_g, 1, LANES).astype(BF16), w_g.T.astype(BF16),
        _pad_to(b_gates[0][None, :], 1, LANES), b_gates[0][:, None])

    pw = pool_w[0].astype(BF16)
    ps = pool_scale[0][None, :]
    pool_y_p, pool_p = _pool_prompt(z, pw, ps, bp, tp)
    pool_y_s, pool_s_t = _pool_step(z, jnp.transpose(state_pool[0], (1, 0, 2)), pw, ps, n_p)
    pool_s = jnp.transpose(pool_s_t, (1, 0, 2))

    ml_y_p, c_p, n_p_st, m_p = _mlstm_prompt(z, gates_t, bp, tp, nh, dh)
    ml_y_p = ml_y_p.reshape(n_p, ml_width)
    g_s = gates[n_p:n_p + bs, :n_gates]
    ml_y_s, c_s, n_s_st, m_s = _mlstm_step(
        z[n_p:n_p + bs].reshape(bs, 1, n_main),
        g_s[:, :nh].reshape(bs, nh, 1, 1), g_s[:, nh:].reshape(bs, nh, 1, 1),
        state_mlstm_c[0], state_mlstm_n[0].reshape(bs, nh, 1, dh),
        state_mlstm_m[0].reshape(bs, nh, 1, 1), nh, dh)

    rows = [x_main, x_tail, pool_y_p, tail_tile(pool_y_s, BF16),
            ml_y_p, tail_tile(ml_y_s.reshape(bs, ml_width), BF16)]
    specs = (_stacked_specs(x_main, x_tail) + _stacked_specs(pool_y_p, rows[3])
             + _stacked_specs(ml_y_p, rows[5]))
    wr, br = router_weights(0)
    x1, hn, rinfo, rinfo_t, counts = _mix_route(
        _outproj_route_kernel, "outproj_route", n, specs, rows, w_out_ab[0].astype(BF16),
        norm_ffn[0][None, :], wr, br, tril)
    x2, h1 = _moe(x1, hn, rinfo, rinfo_t, counts, moe_w_gate, moe_w_up, moe_w_down, 0,
                  norm_mix[1][None, :], last_layer=False)

    prep = _s5_prep(s5_log_step[0], s5_lam_re[0], s5_lam_im[0], s5_b_re[0], s5_b_im[0],
                    s5_c_re[0], s5_c_im[0], s5_d[0])
    y_p, hre_p, him_p = _s5_prompt(h1, prep, bp, tp)
    s5_re_p = hre_p.reshape(bp, ngrp_s5, n_state)
    s5_im_p = him_p.reshape(bp, ngrp_s5, n_state)

    gpb = S5_GROUPS_PER_BLOCK
    per_group = lambda a: jnp.transpose(
        a.reshape(ngrp_s5 // gpb, gw_s5, gpb, n_state), (0, 2, 1, 3)).reshape(ngrp_s5, gw_s5, n_state)
    bbr, bbi = per_group(prep[7]), per_group(prep[8])
    lbr, lbi = prep[9].reshape(ngrp_s5, 1, n_state), prep[10].reshape(ngrp_s5, 1, n_state)
    d_g = s5_d[0].reshape(ngrp_s5, 1, gw_s5)
    u_s = h1[n_p:n_p + bs].reshape(bs, ngrp_s5, gw_s5).transpose(1, 0, 2)
    h_re = jnp.transpose(state_s5_re[0], (1, 0, 2))
    h_im = jnp.transpose(state_s5_im[0], (1, 0, 2))
    cc = jnp.concatenate([jnp.transpose(s5_c_re[0], (0, 2, 1)),
                          -jnp.transpose(s5_c_im[0], (0, 2, 1))], axis=1)
    hn_s, y_s = _s5_step(
        _pad_to(u_s, 2, LANES),
        jnp.concatenate([h_re, h_im], axis=2), jnp.concatenate([h_im, h_re], axis=2),
        _pad_to(jnp.concatenate([bbr, bbi], axis=2), 1, LANES),
        jnp.concatenate([lbr, lbr], axis=2), jnp.concatenate([-lbi, lbi], axis=2),
        _pad_to(cc, 2, LANES), _pad_to(d_g, 2, LANES))
    s5_re_s = jnp.transpose(hn_s[:, :, :n_state], (1, 0, 2))
    s5_im_s = jnp.transpose(hn_s[:, :, n_state:], (1, 0, 2))
    y_tail = tail_tile(jnp.transpose(y_s[:, :, :gw_s5], (1, 0, 2)).reshape(bs, d), F32)

    wr, br = router_weights(1)
    specs = [pl.BlockSpec((tm, d), lambda i: (i, 0))] + _stacked_specs(y_p, y_tail)
    x3, hn, rinfo, rinfo_t, counts = _mix_route(
        _glu_route_kernel, "glu_route", n, specs, [x2, y_p, y_tail], w_glu[0].astype(BF16),
        norm_ffn[1][None, :], wr, br, tril)
    y_main, y_last = _moe(x3, hn, rinfo, rinfo_t, counts, moe_w_gate, moe_w_up, moe_w_down, 1,
                          norm_final[None, :], last_layer=True)

    return (y_main.reshape(bp, tp, d), y_last[:bs].reshape(bs, 1, d),
            pool_p[None], c_p[None], n_p_st.reshape(1, bp, nh, dh), m_p[:, :, 0, 0][None],
            s5_re_p[None], s5_im_p[None],
            pool_s[None], c_s[None], n_s_st.reshape(1, bs, nh, dh), m_s.reshape(1, bs, nh),
            s5_re_s[None], s5_im_s[None])
```

```python
import functools

import jax
import jax.numpy as jnp
from jax import lax
from jax.experimental import pallas as pl
from jax.experimental.pallas import tpu as pltpu

F32 = jnp.float32
BF16 = jnp.bfloat16
I32 = jnp.int32

PAST_LEN = 16384
POOL_WINDOWS = (2, 4, 8, 16)
POOL_BUF = max(POOL_WINDOWS) - 1
MLSTM_CHUNK = 128
S5_SUB = 16
MOE_GROUPS = 4
MOE_EXPERTS_PER_GROUP = 8
RMS_EPS = 1e-6

LANES = 128
SUBLANES = 8
VMEM_LIMIT_BYTES = 56 * 1024 * 1024

TOKEN_TILE = 512
EXPERT_ROW_TILE = 512
MLSTM_SEQ_PER_STEP = 2
MLSTM_STEP_BATCH = 16
POOL_TIME_TILE = 512
DMA_ISSUE_UNROLL = 8
S5_GROUPS_PER_BLOCK = 4
S5_BATCH_PER_STEP = 4

HIGHEST = lax.Precision.HIGHEST


def _params(*sem):
    return pltpu.CompilerParams(dimension_semantics=sem, vmem_limit_bytes=VMEM_LIMIT_BYTES)


def _rms(x, g):
    return x * lax.rsqrt(jnp.mean(x * x, axis=-1, keepdims=True) + RMS_EPS) * g


def _dot(a, b):
    return jnp.dot(a, b, preferred_element_type=F32)


def _dot_nt(a, b):
    return lax.dot_general(a, b, (((1,), (1,)), ((), ())), preferred_element_type=F32)


def _dot_tn(a, b):
    return lax.dot_general(a, b, (((0,), (0,)), ((), ())), preferred_element_type=F32)


def _stacked(main_ref, tail_ref):
    last = pl.program_id(0) == pl.num_programs(0) - 1
    return jnp.where(last, tail_ref[...], main_ref[...])


def _stacked_specs(main, tail):
    tm, w = tail.shape
    last_main = main.shape[0] // tm - 1
    return [pl.BlockSpec((tm, w), lambda i: (jnp.minimum(i, last_main), 0)),
            pl.BlockSpec((tm, w), lambda i: (0, 0))]


def _inproj_kernel(xm_ref, xt_ref, g_ref, w_ref, wg_ref, wgt_ref, bg_ref, bgt_ref,
                   z_ref, gates_ref, gatest_ref):
    h = _rms(_stacked(xm_ref, xt_ref), g_ref[...]).astype(BF16)
    z_ref[...] = _dot(h, w_ref[...])
    gates_ref[...] = _dot(h, wg_ref[...]) + bg_ref[...]
    gatest_ref[...] = _dot_nt(wgt_ref[...], h) + bgt_ref[...]


def _inproj(x_main, x_tail, g, w, wg, wgt, bg, bgt):
    d = x_main.shape[1]
    tm = TOKEN_TILE
    n = x_main.shape[0] + tm
    nz = w.shape[1]
    ng = wgt.shape[0]
    full = lambda i: (0, 0)
    return pl.pallas_call(
        _inproj_kernel,
        grid=(n // tm,),
        in_specs=_stacked_specs(x_main, x_tail) + [
            pl.BlockSpec((1, d), full),
            pl.BlockSpec((d, nz), full),
            pl.BlockSpec((d, LANES), full),
            pl.BlockSpec((ng, d), full),
            pl.BlockSpec((1, LANES), full),
            pl.BlockSpec((ng, 1), full),
        ],
        out_specs=[
            pl.BlockSpec((tm, nz), lambda i: (i, 0)),
            pl.BlockSpec((tm, LANES), lambda i: (i, 0)),
            pl.BlockSpec((ng, tm), lambda i: (0, i)),
        ],
        out_shape=[
            jax.ShapeDtypeStruct((n, nz), F32),
            jax.ShapeDtypeStruct((n, LANES), F32),
            jax.ShapeDtypeStruct((ng, n), F32),
        ],
        compiler_params=_params("parallel"),
        name="inproj",
    )(x_main, x_tail, g, w, wg, wgt, bg, bgt)


def _pool_prompt_kernel(u_ref, pw_ref, ps_ref, y_ref, st_ref, ext_ref, *, tt, gw):
    t = pl.program_id(1)
    nt = pl.num_programs(1)
    halo = POOL_BUF + 1
    width = ext_ref.shape[1]

    @pl.when(t == 0)
    def _():
        ext_ref[0:halo, :] = jnp.zeros((halo, width), F32)

    u = u_ref[...]
    ext_ref[halo:halo + tt, :] = u
    pos = t * tt + lax.broadcasted_iota(I32, (tt, 1), 0)
    for g, w in enumerate(POOL_WINDOWS):
        c0 = g * gw
        acc = u[:, c0:c0 + gw]
        for j in range(1, w):
            acc = acc + ext_ref[halo - j:halo - j + tt, c0:c0 + gw]
        cnt = jnp.minimum(w, pos + 1).astype(F32)
        d = acc / cnt - u[:, c0:c0 + gw]
        y = _dot(d.astype(BF16), pw_ref[g]) * ps_ref[:, c0:c0 + gw]
        y_ref[:, c0:c0 + gw] = y.astype(BF16)

    @pl.when(t == nt - 1)
    def _():
        st_ref[...] = ext_ref[tt + 1:tt + halo, :]

    ext_ref[0:halo, :] = ext_ref[tt:tt + halo, :]


def _pool_prompt(z, pw, ps, batch, seq):
    width = ps.shape[1]
    gw = width // len(POOL_WINDOWS)
    tt = POOL_TIME_TILE
    nt = seq // tt
    return pl.pallas_call(
        functools.partial(_pool_prompt_kernel, tt=tt, gw=gw),
        grid=(batch, nt),
        in_specs=[
            pl.BlockSpec((tt, width), lambda b, t: (b * nt + t, 0)),
            pl.BlockSpec(pw.shape, lambda b, t: (0, 0, 0)),
            pl.BlockSpec((1, width), lambda b, t: (0, 0)),
        ],
        out_specs=[
            pl.BlockSpec((tt, width), lambda b, t: (b * nt + t, 0)),
            pl.BlockSpec((None, POOL_BUF, width), lambda b, t: (b, 0, 0)),
        ],
        out_shape=[
            jax.ShapeDtypeStruct((batch * seq, width), BF16),
            jax.ShapeDtypeStruct((batch, POOL_BUF, width), F32),
        ],
        scratch_shapes=[pltpu.VMEM((POOL_BUF + 1 + tt, width), F32)],
        compiler_params=_params("parallel", "arbitrary"),
        name="pool_prompt",
    )(z, pw, ps)


def _pool_step_kernel(u_ref, buf_ref, pw_ref, ps_ref, y_ref, nb_ref, *, gw):
    u = u_ref[...]
    for g, w in enumerate(POOL_WINDOWS):
        c0 = g * gw
        acc = u[:, c0:c0 + gw]
        for j in range(1, w):
            acc = acc + buf_ref[POOL_BUF - j, :, c0:c0 + gw]
        cnt = float(min(w, PAST_LEN + 1))
        d = acc / cnt - u[:, c0:c0 + gw]
        y = _dot(d.astype(BF16), pw_ref[g]) * ps_ref[:, c0:c0 + gw]
        y_ref[:, c0:c0 + gw] = y.astype(BF16)
    nb_ref[0:POOL_BUF - 1] = buf_ref[1:POOL_BUF]
    nb_ref[POOL_BUF - 1] = u


def _pool_step(z, buf_t, pw, ps, row0):
    _, batch, width = buf_t.shape
    gw = width // len(POOL_WINDOWS)
    return pl.pallas_call(
        functools.partial(_pool_step_kernel, gw=gw),
        grid=(1,),
        in_specs=[
            pl.BlockSpec((batch, width), lambda i: (row0 // batch, 0)),
            pl.BlockSpec(buf_t.shape, lambda i: (0, 0, 0)),
            pl.BlockSpec(pw.shape, lambda i: (0, 0, 0)),
            pl.BlockSpec((1, width), lambda i: (0, 0)),
        ],
        out_specs=[
            pl.BlockSpec((batch, width), lambda i: (0, 0)),
            pl.BlockSpec(buf_t.shape, lambda i: (0, 0, 0)),
        ],
        out_shape=[
            jax.ShapeDtypeStruct((batch, width), BF16),
            jax.ShapeDtypeStruct(buf_t.shape, F32),
        ],
        compiler_params=_params("arbitrary"),
        name="pool_step",
    )(z, buf_t, pw, ps)


def _mlstm_prompt_kernel(*refs, nh, dh, nseq):
    seq_in = [refs[5 * s:5 * s + 5] for s in range(nseq)]
    h_ref, c_out, n_out, m_out, c_s, n_s, m_s = refs[5 * nseq:]
    ci = pl.program_id(1)
    nc = pl.num_programs(1)
    ln = seq_in[0][0].shape[0]

    @pl.when(ci == 0)
    def _():
        c_s[...] = jnp.zeros(c_s.shape, F32)
        n_s[...] = jnp.zeros(n_s.shape, F32)
        m_s[...] = jnp.zeros(m_s.shape, F32)

    row = lax.broadcasted_iota(I32, (ln, ln), 0)
    col = lax.broadcasted_iota(I32, (ln, ln), 1)
    causal = col <= row
    eye = col == row
    lane = lax.broadcasted_iota(I32, (nh, ln), 1)

    def to_col(r):
        return jnp.sum(jnp.where(eye, r, 0.0), axis=1, keepdims=True)

    scale = dh ** -0.5
    pairs = [(s, h) for s in range(nseq) for h in range(nh)]
    c_old = {p: c_s[p[0], p[1]] for p in pairs}
    n_old = {p: n_s[p[0], p[1]] for p in pairs}
    m_old = {p: m_s[p[0], p[1]][:, 0:1] for p in pairs}
    gates = []
    for s in range(nseq):
        gt = seq_in[s][4][...]
        bc_all = jax.nn.log_sigmoid(gt[nh:2 * nh])
        sh = 1
        while sh < ln:
            bc_all = bc_all + jnp.where(lane >= sh, pltpu.roll(bc_all, sh, 1), 0.0)
            sh *= 2
        gates.append((gt[0:nh], bc_all))
    new_state = {}
    for s, h in pairs:
        q_ref, k_ref, v_ref, o_ref, _ = seq_in[s]
        li_all, bc_all = gates[s]
        sl = slice(h * dh, (h + 1) * dh)
        q = q_ref[:, sl]
        k = k_ref[:, sl] * scale
        v = v_ref[:, sl]
        qb, kb, vb = q.astype(BF16), k.astype(BF16), v.astype(BF16)
        li_r = li_all[h:h + 1]
        bc_r = bc_all[h:h + 1]
        bc_c = to_col(bc_r)
        m0, c, n = m_old[s, h], c_old[s, h], n_old[s, h]

        dmat = jnp.where(causal, bc_c - bc_r + li_r, -jnp.inf)
        inter = bc_c + m0
        m_row = jnp.maximum(inter, jnp.max(dmat, axis=1, keepdims=True))
        w_intra = jnp.exp(dmat - m_row)
        w_inter = jnp.exp(inter - m_row)
        sc = _dot_nt(qb, kb) * w_intra
        num = _dot(sc.astype(BF16), vb) + w_inter * _dot_nt(qb, c.astype(BF16))
        den = jnp.sum(sc, axis=1, keepdims=True) + w_inter * jnp.sum(q * n, axis=1, keepdims=True)
        hh = num / jnp.maximum(jnp.abs(den), jnp.exp(-m_row))
        h_ref[s, :, sl] = (hh * jax.nn.sigmoid(o_ref[:, sl])).astype(BF16)

        b_end = bc_r[:, ln - 1:ln]
        g_r = b_end - bc_r + li_r
        m_new = jnp.maximum(b_end + m0, jnp.max(g_r, axis=1, keepdims=True))
        wg_c = jnp.exp(to_col(g_r) - m_new)
        decay = jnp.exp(b_end + m0 - m_new)
        new_state[s, h] = (decay * c + _dot_tn((v * wg_c).astype(BF16), kb),
                           decay * n + jnp.sum(wg_c * k, axis=0, keepdims=True),
                           jnp.broadcast_to(m_new, (1, dh)))

    for (s, h), (c_new, n_new, m_new) in new_state.items():
        c_s[s, h] = c_new
        n_s[s, h] = n_new
        m_s[s, h] = m_new

    @pl.when(ci == nc - 1)
    def _():
        c_out[...] = c_s[...]
        n_out[...] = n_s[...]
        m_out[...] = m_s[...]


def _mlstm_prompt(z, gates_t, batch, seq, nh, dh):
    ln = MLSTM_CHUNK
    nc = seq // ln
    nseq = MLSTM_SEQ_PER_STEP
    width = nh * dh
    ng = gates_t.shape[0]
    rows_of = lambda s: (lambda b, c: (b * nseq + s) * nc + c)
    in_specs, operands = [], []
    for s in range(nseq):
        r = rows_of(s)
        for j in (1, 2, 3, 4):
            in_specs.append(pl.BlockSpec((ln, width), lambda b, c, r=r, j=j: (r(b, c), j)))
        in_specs.append(pl.BlockSpec((ng, ln), lambda b, c, r=r: (0, r(b, c))))
        operands += [z, z, z, z, gates_t]
    h_spec = pl.BlockSpec((None, nseq, ln, width), lambda b, c: (b, 0, c, 0))
    st = lambda a, b_: pl.BlockSpec((nseq, nh, a, b_), lambda b, c: (b, 0, 0, 0))
    return pl.pallas_call(
        functools.partial(_mlstm_prompt_kernel, nh=nh, dh=dh, nseq=nseq),
        grid=(batch // nseq, nc),
        in_specs=in_specs,
        out_specs=[h_spec, st(dh, dh), st(1, dh), st(1, dh)],
        out_shape=[
            jax.ShapeDtypeStruct((batch // nseq, nseq, seq, width), BF16),
            jax.ShapeDtypeStruct((batch, nh, dh, dh), F32),
            jax.ShapeDtypeStruct((batch, nh, 1, dh), F32),
            jax.ShapeDtypeStruct((batch, nh, 1, dh), F32),
        ],
        scratch_shapes=[pltpu.VMEM((nseq, nh, dh, dh), F32), pltpu.VMEM((nseq, nh, 1, dh), F32),
                        pltpu.VMEM((nseq, nh, 1, dh), F32)],
        compiler_params=_params("parallel", "arbitrary"),
        name="mlstm_prompt",
    )(*operands)


def _mlstm_step_kernel(q_ref, k_ref, v_ref, o_ref, li_ref, fp_ref, c_ref, n_ref, m_ref,
                       h_ref, c_out, n_out, m_out, *, nh, dh):
    eye = (lax.broadcasted_iota(I32, (1, dh, dh), 1) == lax.broadcasted_iota(I32, (1, dh, dh), 2))
    scale = dh ** -0.5
    for h in range(nh):
        sl = slice(h * dh, (h + 1) * dh)
        q = q_ref[:, :, sl]
        k = k_ref[:, :, sl] * scale
        v = v_ref[:, :, sl]
        c = c_ref[:, h]
        n = n_ref[:, h]
        m = m_ref[:, h]
        li = li_ref[:, h]
        lf = jax.nn.log_sigmoid(fp_ref[:, h])
        inter = lf + m
        m_row = jnp.maximum(inter, li)
        w_intra = jnp.exp(li - m_row)
        w_inter = jnp.exp(inter - m_row)
        sc = jnp.sum(q * k, axis=-1, keepdims=True) * w_intra
        v_c = jnp.sum(jnp.where(eye, v, 0.0), axis=-1, keepdims=True)
        num = sc * v_c + w_inter * jnp.sum(c * q, axis=-1, keepdims=True)
        den = sc + w_inter * jnp.sum(n * q, axis=-1, keepdims=True)
        h_c = num / jnp.maximum(jnp.abs(den), jnp.exp(-m_row))
        h_l = jnp.sum(jnp.where(eye, h_c, 0.0), axis=1, keepdims=True)
        h_ref[:, :, sl] = h_l * jax.nn.sigmoid(o_ref[:, :, sl])
        wg = jnp.exp(li - m_row)
        decay = jnp.exp(inter - m_row)
        c_out[:, h] = decay * c + (v_c * wg) * k
        n_out[:, h] = decay * n + wg * k
        m_out[:, h] = m_row


def _mlstm_step(z3, li, fp, c, n, m, nh, dh):
    batch = c.shape[0]
    bb = MLSTM_STEP_BATCH
    width = nh * dh
    blk = lambda j: pl.BlockSpec((bb, 1, width), lambda i: (i, 0, j))
    st4 = lambda a, b: pl.BlockSpec((bb, nh, a, b), lambda i: (i, 0, 0, 0))
    return pl.pallas_call(
        functools.partial(_mlstm_step_kernel, nh=nh, dh=dh),
        grid=(batch // bb,),
        in_specs=[blk(1), blk(2), blk(3), blk(4), st4(1, 1), st4(1, 1),
                  st4(dh, dh), st4(1, dh), st4(1, 1)],
        out_specs=[pl.BlockSpec((bb, 1, width), lambda i: (i, 0, 0)),
                   st4(dh, dh), st4(1, dh), st4(1, 1)],
        out_shape=[
            jax.ShapeDtypeStruct((batch, 1, width), F32),
            jax.ShapeDtypeStruct((batch, nh, dh, dh), F32),
            jax.ShapeDtypeStruct((batch, nh, 1, dh), F32),
            jax.ShapeDtypeStruct((batch, nh, 1, 1), F32),
        ],
        compiler_params=_params("parallel"),
        name="mlstm_step",
    )(z3, z3, z3, z3, li, fp, c, n, m)


def _split_bf16(a):
    hi = a.astype(BF16)
    return hi, (a - hi.astype(F32)).astype(BF16)


def _route(hn, wr_ref, br_ref, tril_ref, carry_ref, rinfo_ref, rinfo_t_ref):
    ngrp, epg = MOE_GROUPS, MOE_EXPERTS_PER_GROUP
    h_hi, h_lo = _split_bf16(hn)
    both = _dot(h_hi, wr_ref[...])
    logits = both[:, 0:LANES] + (both[:, LANES:2 * LANES] + _dot(h_lo, wr_ref[:, 0:LANES])) + br_ref[...]
    tm = logits.shape[0]
    lane = lax.broadcasted_iota(I32, (tm, LANES), 1)
    neg = -jnp.inf

    def first_max(x):
        mx = jnp.max(x, axis=1, keepdims=True)
        idx = jnp.min(jnp.where(x == mx, lane, LANES), axis=1, keepdims=True)
        return mx, idx

    is_grp = lane < ngrp
    gmax, gsel = first_max(jnp.where(is_grp, logits, neg))
    g_w = 1.0 / jnp.sum(jnp.where(is_grp, jnp.exp(logits - gmax), 0.0), axis=1, keepdims=True)
    lo = ngrp + gsel * epg
    el = jnp.where((lane >= lo) & (lane < lo + epg), logits, neg)
    v1, i1 = first_max(el)
    v2, i2 = first_max(jnp.where(lane == i1, neg, el))
    e2 = jnp.exp(v2 - v1)
    w1 = g_w / (1.0 + e2)
    w2 = g_w * e2 / (1.0 + e2)
    eid1 = i1 - ngrp
    eid2 = i2 - ngrp

    hit1 = lane == eid1
    hit2 = lane == eid2
    onehot = jnp.where(hit1 | hit2, 1.0, 0.0)
    carry = carry_ref[...]
    prefix = _dot(tril_ref[...], onehot.astype(BF16)) + carry
    rank1 = jnp.sum(jnp.where(hit1, prefix, 0.0), axis=1, keepdims=True)
    rank2 = jnp.sum(jnp.where(hit2, prefix, 0.0), axis=1, keepdims=True)
    carry_ref[...] = carry + jnp.sum(onehot, axis=0, keepdims=True)

    cols = (eid1.astype(F32), eid2.astype(F32), w1, w2, rank1, rank2)
    info = jnp.zeros((tm, LANES), F32)
    for j, cval in enumerate(cols):
        info = jnp.where(lane == j, cval, info)
    rinfo_ref[...] = info
    pick = jnp.where(lax.broadcasted_iota(I32, (SUBLANES, LANES), 0)
                     == lax.broadcasted_iota(I32, (SUBLANES, LANES), 1), 1.0, 0.0).astype(BF16)
    parts = [_dot_nt(pick, p) for p in _split3(info)]
    rinfo_t_ref[...] = parts[0] + (parts[1] + parts[2])


def _outproj_route_kernel(xm_ref, xt_ref, pm_ref, pt_ref, mm_ref, mt_ref,
                          wo_ref, g_ref, wr_ref, br_ref, tril_ref,
                          x1_ref, hn_ref, rinfo_ref, rinfo_t_ref, cnt_ref, carry_ref):
    @pl.when(pl.program_id(0) == 0)
    def _():
        carry_ref[...] = jnp.zeros(carry_ref.shape, F32)

    half = pm_ref.shape[1]
    mix = (_dot(_stacked(pm_ref, pt_ref), wo_ref[0:half, :])
           + _dot(_stacked(mm_ref, mt_ref), wo_ref[half:2 * half, :]))
    x1 = _stacked(xm_ref, xt_ref) + mix
    x1_ref[...] = x1
    hn = _rms(x1, g_ref[...])
    hn_ref[...] = hn
    _route(hn, wr_ref, br_ref, tril_ref, carry_ref, rinfo_ref, rinfo_t_ref)
    cnt_ref[...] = carry_ref[...]


def _glu_route_kernel(x_ref, ym_ref, yt_ref, wglu_ref, g_ref, wr_ref, br_ref, tril_ref,
                      x1_ref, hn_ref, rinfo_ref, rinfo_t_ref, cnt_ref, carry_ref):
    @pl.when(pl.program_id(0) == 0)
    def _():
        carry_ref[...] = jnp.zeros(carry_ref.shape, F32)

    d = x_ref.shape[1]
    ag = _dot(jax.nn.gelu(_stacked(ym_ref, yt_ref)).astype(BF16), wglu_ref[...])
    x1 = x_ref[...] + ag[:, 0:d] * jax.nn.sigmoid(ag[:, d:2 * d])
    x1_ref[...] = x1
    hn = _rms(x1, g_ref[...])
    hn_ref[...] = hn
    _route(hn, wr_ref, br_ref, tril_ref, carry_ref, rinfo_ref, rinfo_t_ref)
    cnt_ref[...] = carry_ref[...]


def _mix_route(kernel, name, n, row_specs, rows, w, g, wr, br, tril):
    d = g.shape[1]
    tm = TOKEN_TILE
    full = lambda i: (0, 0)
    return pl.pallas_call(
        kernel,
        grid=(n // tm,),
        in_specs=row_specs + [
            pl.BlockSpec(w.shape, full),
            pl.BlockSpec((1, d), full),
            pl.BlockSpec((d, 2 * LANES), full),
            pl.BlockSpec((1, LANES), full),
            pl.BlockSpec((tm, tm), full),
        ],
        out_specs=[
            pl.BlockSpec((tm, d), lambda i: (i, 0)),
            pl.BlockSpec((tm, d), lambda i: (i, 0)),
            pl.BlockSpec((tm, LANES), lambda i: (i, 0)),
            pl.BlockSpec((SUBLANES, tm), lambda i: (0, i)),
            pl.BlockSpec((1, LANES), full),
        ],
        out_shape=[
            jax.ShapeDtypeStruct((n, d), F32),
            jax.ShapeDtypeStruct((n, d), F32),
            jax.ShapeDtypeStruct((n, LANES), F32),
            jax.ShapeDtypeStruct((SUBLANES, n), F32),
            jax.ShapeDtypeStruct((1, LANES), F32),
        ],
        scratch_shapes=[pltpu.VMEM((1, LANES), F32)],
        compiler_params=_params("arbitrary"),
        name=name,
    )(*rows, w, g, wr, br, tril)


def _index_copy(pos_hbm, idx_s, sem_i, tile, slot):
    return pltpu.make_async_copy(pos_hbm.at[tile], idx_s.at[slot], sem_i.at[slot])


def _dispatch_kernel(pos_hbm, hn_ref, xs_hbm, idx_s, sem_i, sem_d):
    i = pl.program_id(0)
    nt = pl.num_programs(0)
    tm = hn_ref.shape[0]
    slot = i % 2

    @pl.when(i == 0)
    def _():
        _index_copy(pos_hbm, idx_s, sem_i, 0, 0).start()

    _index_copy(pos_hbm, idx_s, sem_i, i, slot).wait()

    @pl.when(i + 1 < nt)
    def _():
        _index_copy(pos_hbm, idx_s, sem_i, i + 1, 1 - slot).start()

    def issue(r, carry):
        row = hn_ref.at[pl.ds(r, 1)]
        pltpu.make_async_copy(row, xs_hbm.at[pl.ds(idx_s[slot, 0, r], 1)], sem_d).start(priority=0)
        pltpu.make_async_copy(row, xs_hbm.at[pl.ds(idx_s[slot, 0, tm + r], 1)], sem_d).start(priority=1)
        return carry

    lax.fori_loop(0, tm, issue, 0, unroll=DMA_ISSUE_UNROLL)
    whole = pltpu.make_async_copy(hn_ref, xs_hbm.at[pl.ds(0, tm)], sem_d)
    whole.wait()
    whole.wait()


def _dispatch(pos_tiles, hn):
    n, d = hn.shape
    tm = TOKEN_TILE
    return pl.pallas_call(
        _dispatch_kernel,
        grid=(n // tm,),
        in_specs=[pl.BlockSpec(memory_space=pl.ANY), pl.BlockSpec((tm, d), lambda i: (i, 0))],
        out_specs=pl.BlockSpec(memory_space=pl.ANY),
        out_shape=jax.ShapeDtypeStruct((2 * n, d), F32),
        scratch_shapes=[pltpu.SMEM((2, 1, 2 * tm), I32), pltpu.SemaphoreType.DMA((2,)),
                        pltpu.SemaphoreType.DMA],
        compiler_params=_params("arbitrary"),
        name="moe_dispatch",
    )(pos_tiles, hn)


def _moe_kernel(vt_ref, ve_ref, von_ref, lo_ref, hi_ref, xs_ref, wg_ref, wu_ref, wd_ref,
                eo_ref, wgb, wub, wdb):
    v = pl.program_id(0)
    tr = xs_ref.shape[0]
    prev = jnp.maximum(v - 1, 0)
    e = ve_ref[v]
    new_expert = jnp.logical_or(v == 0, e != ve_ref[prev])
    first_visit = jnp.logical_or(v == 0, vt_ref[v] != vt_ref[prev])

    @pl.when(von_ref[v] == 1)
    def _():
        @pl.when(new_expert)
        def _():
            wgb[...] = wg_ref[...].astype(BF16)
            wub[...] = wu_ref[...].astype(BF16)
            wdb[...] = wd_ref[...].astype(BF16)

        x = xs_ref[...].astype(BF16)
        act = jax.nn.silu(_dot(x, wgb[...])) * _dot(x, wub[...])
        row = vt_ref[v] * tr + lax.broadcasted_iota(I32, (tr, 1), 0)
        act = jnp.where((row >= lo_ref[e]) & (row < hi_ref[e]), act, 0.0)
        res = _dot(act.astype(BF16), wdb[...])

        @pl.when(first_visit)
        def _():
            eo_ref[...] = res

        @pl.when(jnp.logical_not(first_visit))
        def _():
            eo_ref[...] += res


def _moe_experts(vt, ve, von, lo, hi, xs, wg, wu, wd, layer):
    nv = vt.shape[0]
    rows, d = xs.shape
    hid = wg.shape[3]
    tr = EXPERT_ROW_TILE
    wspec = lambda a, b: pl.BlockSpec((None, None, a, b),
                                      lambda v, vt, ve, von, lo, hi: (layer, ve[v], 0, 0))
    grid_spec = pltpu.PrefetchScalarGridSpec(
        num_scalar_prefetch=5,
        grid=(nv,),
        in_specs=[
            pl.BlockSpec((tr, d), lambda v, vt, ve, von, lo, hi: (vt[v], 0)),
            wspec(d, hid), wspec(d, hid), wspec(hid, d),
        ],
        out_specs=pl.BlockSpec((tr, d), lambda v, vt, ve, von, lo, hi: (vt[v], 0)),
        scratch_shapes=[pltpu.VMEM((d, hid), BF16), pltpu.VMEM((d, hid), BF16),
                        pltpu.VMEM((hid, d), BF16)],
    )
    return pl.pallas_call(
        _moe_kernel,
        grid_spec=grid_spec,
        out_shape=jax.ShapeDtypeStruct((rows, d), F32),
        compiler_params=_params("arbitrary"),
        name="moe_experts",
    )(vt, ve, von, lo, hi, xs, wg, wu, wd)


def _combine_kernel(pos_hbm, eo_hbm, x_ref, rinfo_ref, g_ref, o1_ref, o2_ref, idx_s, a_buf, b_buf,
                    sem_i, sem_a, sem_b, *, last_layer):
    i = pl.program_id(0)
    nt = pl.num_programs(0)
    tm = a_buf.shape[1]
    slot = i % 2

    def gathers(s):
        def issue(r, carry):
            pltpu.make_async_copy(eo_hbm.at[pl.ds(idx_s[s, 0, r], 1)],
                                  a_buf.at[s, pl.ds(r, 1)], sem_a.at[s]).start(priority=0)
            pltpu.make_async_copy(eo_hbm.at[pl.ds(idx_s[s, 0, tm + r], 1)],
                                  b_buf.at[s, pl.ds(r, 1)], sem_b.at[s]).start(priority=1)
            return carry

        lax.fori_loop(0, tm, issue, 0, unroll=DMA_ISSUE_UNROLL)

    @pl.when(i == 0)
    def _():
        first = _index_copy(pos_hbm, idx_s, sem_i, 0, 0)
        first.start()
        first.wait()
        gathers(0)

        @pl.when(nt > 1)
        def _():
            _index_copy(pos_hbm, idx_s, sem_i, 1, 1).start()

    @pl.when(i + 1 < nt)
    def _():
        _index_copy(pos_hbm, idx_s, sem_i, i + 1, 1 - slot).wait()
        gathers(1 - slot)

    @pl.when(i + 2 < nt)
    def _():
        _index_copy(pos_hbm, idx_s, sem_i, i + 2, slot).start()

    pltpu.make_async_copy(eo_hbm.at[pl.ds(0, tm)], a_buf.at[slot], sem_a.at[slot]).wait()
    pltpu.make_async_copy(eo_hbm.at[pl.ds(0, tm)], b_buf.at[slot], sem_b.at[slot]).wait()
    info = rinfo_ref[...]
    x2 = x_ref[...] + (info[:, 2:3] * a_buf[slot] + info[:, 3:4] * b_buf[slot])
    hn = _rms(x2, g_ref[...])
    if last_layer:
        @pl.when(i < nt - 1)
        def _():
            o1_ref[...] = hn

        @pl.when(i == nt - 1)
        def _():
            o2_ref[...] = hn
    else:
        o1_ref[...] = x2
        o2_ref[...] = hn


def _combine(pos_tiles, eo, x, rinfo, g, last_layer):
    n, d = x.shape
    tm = TOKEN_TILE
    nt = n // tm
    row = pl.BlockSpec((tm, d), lambda i: (i, 0))
    if last_layer:
        out_specs = [pl.BlockSpec((tm, d), lambda i: (jnp.minimum(i, nt - 2), 0)),
                     pl.BlockSpec((tm, d), lambda i: (0, 0))]
        out_shape = [jax.ShapeDtypeStruct((n - tm, d), F32), jax.ShapeDtypeStruct((tm, d), F32)]
    else:
        out_specs = [row, row]
        out_shape = [jax.ShapeDtypeStruct((n, d), F32), jax.ShapeDtypeStruct((n, d), F32)]
    return pl.pallas_call(
        functools.partial(_combine_kernel, last_layer=last_layer),
        grid=(nt,),
        in_specs=[
            pl.BlockSpec(memory_space=pl.ANY),
            pl.BlockSpec(memory_space=pl.ANY),
            row,
            pl.BlockSpec((tm, LANES), lambda i: (i, 0)),
            pl.BlockSpec((1, d), lambda i: (0, 0)),
        ],
        out_specs=out_specs,
        out_shape=out_shape,
        scratch_shapes=[
            pltpu.SMEM((2, 1, 2 * tm), I32),
            pltpu.VMEM((2, tm, d), F32),
            pltpu.VMEM((2, tm, d), F32),
            pltpu.SemaphoreType.DMA((2,)),
            pltpu.SemaphoreType.DMA((2,)),
            pltpu.SemaphoreType.DMA((2,)),
        ],
        compiler_params=_params("arbitrary"),
        name="moe_combine",
    )(pos_tiles, eo, x, rinfo, g)


def _moe(x1, hn, rinfo, rinfo_t, counts, wg, wu, wd, layer, next_gain, last_layer):
    n = x1.shape[0]
    ne = wg.shape[1]
    tr = EXPERT_ROW_TILE
    tm = TOKEN_TILE
    nv = (2 * n) // tr + ne - 1
    eid = rinfo_t[0:2].astype(I32)
    rank = rinfo_t[4:6].astype(I32)
    cnt = counts[0, :ne].astype(I32)
    seg_end = jnp.cumsum(cnt)
    seg_start = seg_end - cnt
    experts = jnp.arange(ne, dtype=I32)
    pos = rank + jnp.sum(jnp.where(eid[:, None, :] == experts[None, :, None],
                                   seg_start[None, :, None], 0), axis=1)
    pos_tiles = pos.reshape(2, n // tm, tm).transpose(1, 0, 2).reshape(n // tm, 1, 2 * tm)

    first_tile = seg_start // tr
    tiles_e = jnp.where(cnt > 0, (seg_end - 1) // tr - first_tile + 1, 0)
    v_end = jnp.cumsum(tiles_e)
    v_start = v_end - tiles_e
    total = v_end[-1]
    vis = jnp.arange(nv, dtype=I32)
    vc = jnp.minimum(vis, jnp.maximum(total - 1, 0))
    ve = jnp.sum((vc[:, None] >= v_end[None, :]).astype(I32), axis=1)
    pick = lambda tab: jnp.sum(jnp.where(ve[:, None] == experts, tab, 0), axis=1)
    vt = pick(first_tile) + (vc - pick(v_start))
    von = (vis < total).astype(I32)

    xs = _dispatch(pos_tiles, hn)
    eo = _moe_experts(vt, ve, von, seg_start, seg_end, xs, wg, wu, wd, layer)
    return _combine(pos_tiles, eo, x1, rinfo, next_gain, last_layer)


def _cis(log_mag, ang):
    mag = jnp.exp(log_mag)
    return mag * jnp.cos(ang), mag * jnp.sin(ang)


def _split3(a):
    p1 = a.astype(BF16)
    r1 = a - p1.astype(F32)
    p2 = r1.astype(BF16)
    return p1, p2, (r1 - p2.astype(F32)).astype(BF16)


def _select_dot(a, sel, sel_first=False):
    sel = sel.astype(BF16)
    parts = [(_dot(sel, p) if sel_first else _dot(p, sel)) for p in _split3(a)]
    return parts[0] + (parts[1] + parts[2])


def _dot3(a, b):
    a_hi, a_lo = _split_bf16(a)
    b_hi, b_lo = _split_bf16(b)
    return _dot(a_hi, b_hi) + (_dot(a_lo, b_hi) + _dot(a_hi, b_lo))


def _s5_prep_kernel(lsc_ref, lsr_ref, lrc_ref, lic_ref, lrr_ref, lir_ref, ctre_ref, ctim_ref,
                    btre_ref, btim_ref, d_ref,
                    t_ref, wre_ref, wim_ref, cyre_ref, cyim_ref, apr_ref, api_ref,
                    bbr_ref, bbi_ref, lbr_ref, lbi_ref, *, gw, gpb, sub):
    blk = gpb * gw
    wide = sub * blk
    nsl = lrc_ref.shape[1]
    ns = nsl // gpb
    sh_blk, sh_gw, sh_ns = blk.bit_length() - 1, gw.bit_length() - 1, ns.bit_length() - 1
    dt_c, dt_r = jnp.exp(lsc_ref[0]), jnp.exp(lsr_ref[0])
    ldt_c_re, ldt_c_im = lrc_ref[0] * dt_c, lic_ref[0] * dt_c
    ldt_r_re, ldt_r_im = lrr_ref[0] * dt_r, lir_ref[0] * dt_r

    lane_w = lax.broadcasted_iota(I32, (1, wide), 1)
    spread = jnp.where((lax.broadcasted_iota(I32, (gw, wide), 1) & (gw - 1))
                       == lax.broadcasted_iota(I32, (gw, wide), 0), 1.0, 0.0)
    cre = _select_dot(ctre_ref[0], spread)
    cim = _select_dot(ctim_ref[0], spread)
    same = (jnp.right_shift(lax.broadcasted_iota(I32, (nsl, 1), 0), sh_ns)
            == (jnp.right_shift(lane_w, sh_gw) & (gpb - 1)))
    assert 2 * blk == LANES
    tau = lax.broadcasted_iota(I32, (1, LANES), 1).astype(F32)
    pw_re, pw_im = _cis(tau * ldt_c_re, tau * ldt_c_im)
    low = lax.broadcasted_iota(I32, (nsl, LANES), 1) < blk

    def spread_pow(p, first):
        col = lambda t: jnp.broadcast_to(p[:, t:t + 1], (nsl, LANES))
        return jnp.concatenate([jnp.where(low, col(first + 2 * m), col(first + 2 * m + 1))
                                for m in range(sub // 2)], axis=1)

    def c_lam_pow(first):
        pr, pi = spread_pow(pw_re, first), spread_pow(pw_im, first)
        return (jnp.where(same, pr * cre - pi * cim, 0.0),
                jnp.where(same, -(pr * cim + pi * cre), 0.0))

    clr0, cli0 = c_lam_pow(0)
    clr1, cli1 = c_lam_pow(1)
    cyre_ref[0] = clr1.astype(BF16)
    cyim_ref[0] = cli1.astype(BF16)

    lbr, lbi = _cis(ldt_r_re, ldt_r_im)
    lbr_ref[0] = lbr
    lbi_ref[0] = lbi
    lr, li = lrr_ref[0], lir_ref[0]
    nr, ni = lbr - 1.0, lbi
    den = lr * lr + li * li
    fr = (nr * lr + ni * li) / den
    fi = (ni * lr - nr * li) / den
    bre, bim = btre_ref[0], btim_ref[0]
    bbr = fr * bre - fi * bim
    bbi = fr * bim + fi * bre
    bbr_ref[0] = bbr
    bbi_ref[0] = bbi

    lane_t = lax.broadcasted_iota(I32, (gw, wide), 1)
    chan = lax.broadcasted_iota(I32, (gw, wide), 0)
    for g in range(gpb):
        ps = slice(g * ns, (g + 1) * ns)
        r = _dot3(bbr[:, ps], clr0[ps, :]) + _dot3(bbi[:, ps], cli0[ps, :])
        r = r + jnp.where(lane_t == g * gw + chan, d_ref[0][:, g:g + 1], 0.0)
        for j in range(sub):
            tb = r if j == 0 else jnp.where(lane_t >= blk * j, pltpu.roll(r, blk * j, 1), 0.0)
            r0 = j * blk + g * gw
            t_ref[0, r0:r0 + gw, :] = tb.astype(BF16)

    rows = lax.broadcasted_iota(I32, (wide, 1), 0)
    spread_t = jnp.where((lax.broadcasted_iota(I32, (wide, gw), 0) & (gw - 1))
                         == lax.broadcasted_iota(I32, (wide, gw), 1), 1.0, 0.0)
    bbr_t = _select_dot(bbr, spread_t, sel_first=True)
    bbi_t = _select_dot(bbi, spread_t, sel_first=True)
    same_w = ((jnp.right_shift(rows, sh_gw) & (gpb - 1))
              == jnp.right_shift(lax.broadcasted_iota(I32, (1, nsl), 1), sh_ns))
    rj = ((sub - 1) - lax.broadcasted_iota(I32, (sub, 1), 0)).astype(F32)
    q_re, q_im = _cis(rj * ldt_r_re, rj * ldt_r_im)
    per_step = lambda q: jnp.concatenate(
        [jnp.broadcast_to(q[j:j + 1, :], (blk, nsl)) for j in range(sub)], axis=0)
    pr, pi = per_step(q_re), per_step(q_im)
    wre_ref[0] = jnp.where(same_w, pr * bbr_t - pi * bbi_t, 0.0).astype(BF16)
    wim_ref[0] = jnp.where(same_w, pr * bbi_t + pi * bbr_t, 0.0).astype(BF16)

    nlev = apr_ref.shape[1]
    pw = (sub * jnp.left_shift(1, lax.broadcasted_iota(I32, (nlev, 1), 0))).astype(F32)
    apr, api = _cis(pw * ldt_r_re, pw * ldt_r_im)
    apr_ref[0] = apr
    api_ref[0] = api


def _s5_prep(log_step, lam_re, lam_im, b_re, b_im, c_re, c_im, d_skip):
    ng, ns = lam_re.shape
    gw = b_re.shape[2]
    sub, gpb = S5_SUB, S5_GROUPS_PER_BLOCK
    nblk = ng // gpb
    nsl = gpb * ns
    wide = sub * gpb * gw
    ls = jnp.repeat(log_step, ns)
    ct = lambda c: jnp.transpose(c, (0, 2, 1)).reshape(nblk, nsl, gw)
    bt = lambda b: jnp.transpose(b.reshape(nblk, gpb, ns, gw), (0, 3, 1, 2)).reshape(nblk, gw, nsl)
    ins = [ls.reshape(nblk, nsl, 1), ls.reshape(nblk, 1, nsl),
           lam_re.reshape(nblk, nsl, 1), lam_im.reshape(nblk, nsl, 1),
           lam_re.reshape(nblk, 1, nsl), lam_im.reshape(nblk, 1, nsl),
           ct(c_re), ct(c_im), bt(b_re), bt(b_im),
           jnp.transpose(d_skip.reshape(nblk, gpb, gw), (0, 2, 1))]
    spec = lambda a: pl.BlockSpec((1,) + a.shape[1:], lambda g: (g, 0, 0))
    outs = [
        jax.ShapeDtypeStruct((nblk, wide, wide), BF16),
        jax.ShapeDtypeStruct((nblk, wide, nsl), BF16),
        jax.ShapeDtypeStruct((nblk, wide, nsl), BF16),
        jax.ShapeDtypeStruct((nblk, nsl, wide), BF16),
        jax.ShapeDtypeStruct((nblk, nsl, wide), BF16),
        jax.ShapeDtypeStruct((nblk, SUBLANES, nsl), F32),
        jax.ShapeDtypeStruct((nblk, SUBLANES, nsl), F32),
        jax.ShapeDtypeStruct((nblk, gw, nsl), F32),
        jax.ShapeDtypeStruct((nblk, gw, nsl), F32),
        jax.ShapeDtypeStruct((nblk, 1, nsl), F32),
        jax.ShapeDtypeStruct((nblk, 1, nsl), F32),
    ]
    return pl.pallas_call(
        functools.partial(_s5_prep_kernel, gw=gw, gpb=gpb, sub=sub),
        grid=(nblk,),
        in_specs=[spec(a) for a in ins],
        out_specs=[spec(o) for o in outs],
        out_shape=outs,
        compiler_params=_params("parallel"),
        name="s5_prep",
    )(*ins)


def _s5_prompt_kernel(x_ref, t_ref, wre_ref, wim_ref, cyre_ref, cyim_ref, apr_ref, api_ref,
                      y_ref, hre_out, him_out, u_s, y4_s, sre_s, sim_s, *, nb, seq, sub):
    nk = seq // sub
    rows = nb * nk
    hl = LANES // 2
    pad = nk // 2
    nlev = nk.bit_length() - 1
    nsl = sre_s.shape[1]
    low = lax.broadcasted_iota(I32, (nk, LANES), 1) < hl
    kidx = lax.broadcasted_iota(I32, (rows, 1), 0) & (nk - 1)
    sre_s[0:pad, :] = jnp.zeros((pad, nsl), F32)
    sim_s[0:pad, :] = jnp.zeros((pad, nsl), F32)

    def step_rows(b, j):
        return (pl.ds(b * seq + j, nk, stride=sub), slice(None))

    for b in range(nb):
        rs = slice(b * nk, (b + 1) * nk)
        for m in range(sub // 2):
            ls = slice(m * LANES, (m + 1) * LANES)
            s0 = x_ref[step_rows(b, 2 * m)]
            s1 = x_ref[step_rows(b, 2 * m + 1)]
            u_s[0, rs, ls] = jnp.where(low, s0, pltpu.roll(s1, hl, 1)).astype(BF16)
            u_s[1, rs, ls] = jnp.where(low, pltpu.roll(s0, hl, 1), s1).astype(BF16)

    live = slice(pad, pad + rows)
    for half in range(2):
        u = u_s[half]
        hre = _dot(u, wre_ref[half])
        him = _dot(u, wim_ref[half])
        for i in range(nlev):
            s = 1 << i
            sre_s[live, :] = hre
            sim_s[live, :] = him
            pre = sre_s[pad - s:pad - s + rows, :]
            pim = sim_s[pad - s:pad - s + rows, :]
            ar = apr_ref[half, i:i + 1, :]
            ai = api_ref[half, i:i + 1, :]
            ok = kidx >= s
            hre = hre + jnp.where(ok, ar * pre - ai * pim, 0.0)
            him = him + jnp.where(ok, ar * pim + ai * pre, 0.0)
        sre_s[live, :] = hre
        sim_s[live, :] = him
        ls = slice(half * nsl, (half + 1) * nsl)
        for b in range(nb):
            last = pad + (b + 1) * nk - 1
            hre_out[b:b + 1, ls] = sre_s[last:last + 1, :]
            him_out[b:b + 1, ls] = sim_s[last:last + 1, :]
        started = kidx >= 1
        hpre = jnp.where(started, sre_s[pad - 1:pad - 1 + rows, :], 0.0).astype(BF16)
        hpim = jnp.where(started, sim_s[pad - 1:pad - 1 + rows, :], 0.0).astype(BF16)
        y4_s[half] = _dot(u, t_ref[half]) + _dot(hpre, cyre_ref[half]) + _dot(hpim, cyim_ref[half])

    for b in range(nb):
        rs = slice(b * nk, (b + 1) * nk)
        for m in range(sub // 2):
            ls = slice(m * LANES, (m + 1) * LANES)
            ca = y4_s[0, rs, ls]
            cb = y4_s[1, rs, ls]
            y_ref[step_rows(b, 2 * m)] = jnp.where(low, ca, pltpu.roll(cb, hl, 1))
            y_ref[step_rows(b, 2 * m + 1)] = jnp.where(low, pltpu.roll(ca, hl, 1), cb)


def _s5_prompt(h, prep, batch, seq):
    d = h.shape[1]
    tmat, wre, wim, cyre, cyim, apr, api = prep[:7]
    nblk, wide, nsl = wre.shape
    sub, nb = S5_SUB, S5_BATCH_PER_STEP
    nk = seq // sub
    assert nk & (nk - 1) == 0 and nk.bit_length() - 1 <= apr.shape[1]
    ntile = d // LANES
    assert nblk == 2 * ntile
    rows = nb * nk
    wspec = lambda a: pl.BlockSpec((2,) + a.shape[1:], lambda t, b: (t, 0, 0))
    xspec = pl.BlockSpec((nb * seq, LANES), lambda t, b: (b, t))
    sspec = pl.BlockSpec((None, nb, 2 * nsl), lambda t, b: (b, 0, t))
    return pl.pallas_call(
        functools.partial(_s5_prompt_kernel, nb=nb, seq=seq, sub=sub),
        grid=(ntile, batch // nb),
        in_specs=[xspec, wspec(tmat), wspec(wre), wspec(wim), wspec(cyre), wspec(cyim),
                  wspec(apr), wspec(api)],
        out_specs=[xspec, sspec, sspec],
        out_shape=[
            jax.ShapeDtypeStruct((batch * seq, d), F32),
            jax.ShapeDtypeStruct((batch // nb, nb, ntile * 2 * nsl), F32),
            jax.ShapeDtypeStruct((batch // nb, nb, ntile * 2 * nsl), F32),
        ],
        scratch_shapes=[
            pltpu.VMEM((2, rows, wide), BF16),
            pltpu.VMEM((2, rows, wide), F32),
            pltpu.VMEM((nk // 2 + rows, nsl), F32),
            pltpu.VMEM((nk // 2 + rows, nsl), F32),
        ],
        compiler_params=_params("parallel", "parallel"),
        name="s5_prompt",
    )(h, tmat, wre, wim, cyre, cyim, apr, api)


def _s5_step_kernel(u_ref, h_ref, hsw_ref, bb_ref, la_ref, lb_ref, cc_ref, d_ref, hn_ref, y_ref):
    for g in range(u_ref.shape[0]):
        u = u_ref[g]
        hn = (la_ref[g] * h_ref[g] + lb_ref[g] * hsw_ref[g]
              + jnp.dot(u, bb_ref[g], precision=HIGHEST, preferred_element_type=F32))
        hn_ref[g] = hn
        y_ref[g] = (jnp.dot(hn, cc_ref[g], precision=HIGHEST, preferred_element_type=F32)
                    + d_ref[g] * u)


def _s5_step(u_p, h_cat, h_swp, bb_cat, la, lb, cc, d_p):
    ng = u_p.shape[0]
    gb = SUBLANES
    spec = lambda a: pl.BlockSpec((gb,) + a.shape[1:], lambda g: (g, 0, 0))
    ins = [u_p, h_cat, h_swp, bb_cat, la, lb, cc, d_p]
    outs = [jax.ShapeDtypeStruct(h_cat.shape, F32), jax.ShapeDtypeStruct(u_p.shape, F32)]
    return pl.pallas_call(
        _s5_step_kernel,
        grid=(ng // gb,),
        in_specs=[spec(a) for a in ins],
        out_specs=[spec(o) for o in outs],
        out_shape=outs,
        compiler_params=_params("parallel"),
        name="s5_step",
    )(*ins)


def _pad_to(a, axis, size):
    pad = [(0, 0)] * a.ndim
    pad[axis] = (0, size - a.shape[axis])
    return jnp.pad(a, pad)


def kernel(x_prompt, x_sample, state_pool, state_mlstm_c, state_mlstm_n, state_mlstm_m, state_s5_re, state_s5_im, norm_mix, norm_ffn, norm_final, w_in_ab, b_gates, pool_w, pool_scale, w_out_ab, s5_lam_re, s5_lam_im, s5_log_step, s5_b_re, s5_b_im, s5_c_re, s5_c_im, s5_d, w_glu, moe_w_group, moe_b_group, moe_w_expert, moe_b_expert, moe_w_gate, moe_w_up, moe_w_down):
    bp, tp, d = x_prompt.shape
    bs = x_sample.shape[0]
    n_p = bp * tp
    tm = TOKEN_TILE
    assert n_p % tm == 0 and bs <= tm
    n_fill = tm - bs
    nh, dh = state_mlstm_c.shape[2], state_mlstm_c.shape[3]
    pool_width = state_pool.shape[3]
    ml_width = nh * dh
    n_main = pool_width + 4 * ml_width
    n_gates = 2 * nh
    ngrp_s5, n_state = s5_lam_re.shape[1], s5_lam_re.shape[2]
    gw_s5 = d // ngrp_s5

    n = n_p + tm

    def tail_tile(sample_rows, dtype):
        return _pad_to(sample_rows.astype(dtype), 0, tm)

    x_main = x_prompt.reshape(n_p, d)
    x_tail = tail_tile(x_sample.reshape(bs, d), F32)
    tril = jnp.tril(jnp.ones((tm, tm), BF16), -1)

    def router_weights(l):
        wr = _pad_to(jnp.concatenate([moe_w_group[l], moe_w_expert[l]], axis=1), 1, LANES)
        br = jnp.concatenate([moe_b_group[l], moe_b_expert[l]])[None, :]
        hi = wr.astype(BF16)
        lo = (wr - hi.astype(F32)).astype(BF16)
        return jnp.concatenate([hi, lo], axis=1), _pad_to(br, 1, LANES)

    w_in = w_in_ab[0]
    w_g = w_in[:, n_main:]
    z, gates, gates_t = _inproj(
        x_main, x_tail, norm_mix[0][None, :], w_in[:, :n_main].astype(BF16),
        _pad_to(w_g, 1, LANES).astype(BF16), w_g.T.astype(BF16),
        _pad_to(b_gates[0][None, :], 1, LANES), b_gates[0][:, None])

    pw = pool_w[0].astype(BF16)
    ps = pool_scale[0][None, :]
    pool_y_p, pool_p = _pool_prompt(z, pw, ps, bp, tp)
    pool_y_s, pool_s_t = _pool_step(z, jnp.transpose(state_pool[0], (1, 0, 2)), pw, ps, n_p)
    pool_s = jnp.transpose(pool_s_t, (1, 0, 2))

    ml_y_p, c_p, n_p_st, m_p = _mlstm_prompt(z, gates_t, bp, tp, nh, dh)
    ml_y_p = ml_y_p.reshape(n_p, ml_width)
    g_s = gates[n_p:n_p + bs, :n_gates]
    ml_y_s, c_s, n_s_st, m_s = _mlstm_step(
        z[n_p:n_p + bs].reshape(bs, 1, n_main),
        g_s[:, :nh].reshape(bs, nh, 1, 1), g_s[:, nh:].reshape(bs, nh, 1, 1),
        state_mlstm_c[0], state_mlstm_n[0].reshape(bs, nh, 1, dh),
        state_mlstm_m[0].reshape(bs, nh, 1, 1), nh, dh)

    rows = [x_main, x_tail, pool_y_p, tail_tile(pool_y_s, BF16),
            ml_y_p, tail_tile(ml_y_s.reshape(bs, ml_width), BF16)]
    specs = (_stacked_specs(x_main, x_tail) + _stacked_specs(pool_y_p, rows[3])
             + _stacked_specs(ml_y_p, rows[5]))
    wr, br = router_weights(0)
    x1, hn, rinfo, rinfo_t, counts = _mix_route(
        _outproj_route_kernel, "outproj_route", n, specs, rows, w_out_ab[0].astype(BF16),
        norm_ffn[0][None, :], wr, br, tril)
    x2, h1 = _moe(x1, hn, rinfo, rinfo_t, counts, moe_w_gate, moe_w_up, moe_w_down, 0,
                  norm_mix[1][None, :], last_layer=False)

    prep = _s5_prep(s5_log_step[0], s5_lam_re[0], s5_lam_im[0], s5_b_re[0], s5_b_im[0],
                    s5_c_re[0], s5_c_im[0], s5_d[0])
    y_p, hre_p, him_p = _s5_prompt(h1, prep, bp, tp)
    s5_re_p = hre_p.reshape(bp, ngrp_s5, n_state)
    s5_im_p = him_p.reshape(bp, ngrp_s5, n_state)

    gpb = S5_GROUPS_PER_BLOCK
    per_group = lambda a: jnp.transpose(
        a.reshape(ngrp_s5 // gpb, gw_s5, gpb, n_state), (0, 2, 1, 3)).reshape(ngrp_s5, gw_s5, n_state)
    bbr, bbi = per_group(prep[7]), per_group(prep[8])
    lbr, lbi = prep[9].reshape(ngrp_s5, 1, n_state), prep[10].reshape(ngrp_s5, 1, n_state)
    d_g = s5_d[0].reshape(ngrp_s5, 1, gw_s5)
    u_s = h1[n_p:n_p + bs].reshape(bs, ngrp_s5, gw_s5).transpose(1, 0, 2)
    h_re = jnp.transpose(state_s5_re[0], (1, 0, 2))
    h_im = jnp.transpose(state_s5_im[0], (1, 0, 2))
    cc = jnp.concatenate([jnp.transpose(s5_c_re[0], (0, 2, 1)),
                          -jnp.transpose(s5_c_im[0], (0, 2, 1))], axis=1)
    hn_s, y_s = _s5_step(
        _pad_to(u_s, 2, LANES),
        jnp.concatenate([h_re, h_im], axis=2), jnp.concatenate([h_im, h_re], axis=2),
        _pad_to(jnp.concatenate([bbr, bbi], axis=2), 1, LANES),
        jnp.concatenate([lbr, lbr], axis=2), jnp.concatenate([-lbi, lbi], axis=2),
        _pad_to(cc, 2, LANES), _pad_to(d_g, 2, LANES))
    s5_re_s = jnp.transpose(hn_s[:, :, :n_state], (1, 0, 2))
    s5_im_s = jnp.transpose(hn_s[:, :, n_state:], (1, 0, 2))
    y_tail = tail_tile(jnp.transpose(y_s[:, :, :gw_s5], (1, 0, 2)).reshape(bs, d), F32)

    wr, br = router_weights(1)
    specs = [pl.BlockSpec((tm, d), lambda i: (i, 0))] + _stacked_specs(y_p, y_tail)
    x3, hn, rinfo, rinfo_t, counts = _mix_route(
        _glu_route_kernel, "glu_route", n, specs, [x2, y_p, y_tail], w_glu[0].astype(BF16),
        norm_ffn[1][None, :], wr, br, tril)
    y_main, y_last = _moe(x3, hn, rinfo, rinfo_t, counts, moe_w_gate, moe_w_up, moe_w_down, 1,
                          norm_final[None, :], last_layer=True)

    return (y_main.reshape(bp, tp, d), y_last[:bs].reshape(bs, 1, d),
            pool_p[None], c_p[None], n_p_st.reshape(1, bp, nh, dh), m_p[:, :, 0, 0][None],
            s5_re_p[None], s5_im_p[None],
            pool_s[None], c_s[None], n_s_st.reshape(1, bs, nh, dh), m_s.reshape(1, bs, nh),
            s5_re_s[None], s5_im_s[None])
```

```python
import functools

import jax
import jax.numpy as jnp
from jax import lax
from jax.experimental import pallas as pl
from jax.experimental.pallas import tpu as pltpu

F32 = jnp.float32
BF16 = jnp.bfloat16
I32 = jnp.int32

PAST_LEN = 16384
POOL_WINDOWS = (2, 4, 8, 16)
POOL_BUF = max(POOL_WINDOWS) - 1
MLSTM_CHUNK = 128
S5_SUB = 16
MOE_GROUPS = 4
MOE_EXPERTS_PER_GROUP = 8
RMS_EPS = 1e-6

LANES = 128
SUBLANES = 8
VMEM_LIMIT_BYTES = 56 * 1024 * 1024

TOKEN_TILE = 512
EXPERT_ROW_TILE = 512
MLSTM_SEQ_PER_STEP = 2
MLSTM_STEP_BATCH = 16
POOL_TIME_TILE = 1024
DMA_ISSUE_UNROLL = 8
S5_GROUPS_PER_BLOCK = 4
S5_BATCH_PER_STEP = 4

HIGHEST = lax.Precision.HIGHEST


def _params(*sem):
    return pltpu.CompilerParams(dimension_semantics=sem, vmem_limit_bytes=VMEM_LIMIT_BYTES)


def _rms(x, g):
    return x * lax.rsqrt(jnp.mean(x * x, axis=-1, keepdims=True) + RMS_EPS) * g


def _dot(a, b):
    return jnp.dot(a, b, preferred_element_type=F32)


def _dot_nt(a, b):
    return lax.dot_general(a, b, (((1,), (1,)), ((), ())), preferred_element_type=F32)


def _dot_tn(a, b):
    return lax.dot_general(a, b, (((0,), (0,)), ((), ())), preferred_element_type=F32)


def _stacked(main_ref, tail_ref):
    last = pl.program_id(0) == pl.num_programs(0) - 1
    return jnp.where(last, tail_ref[...], main_ref[...])


def _stacked_specs(main, tail):
    tm, w = tail.shape
    last_main = main.shape[0] // tm - 1
    return [pl.BlockSpec((tm, w), lambda i: (jnp.minimum(i, last_main), 0)),
            pl.BlockSpec((tm, w), lambda i: (0, 0))]


def _inproj_kernel(xm_ref, xt_ref, g_ref, w_ref, wg_ref, wgt_ref, bg_ref, bgt_ref,
                   z_ref, gates_ref, gatest_ref):
    h = _rms(_stacked(xm_ref, xt_ref), g_ref[...]).astype(BF16)
    z_ref[...] = _dot(h, w_ref[...])
    gates_ref[...] = _dot(h, wg_ref[...]) + bg_ref[...]
    gatest_ref[...] = _dot_nt(wgt_ref[...], h) + bgt_ref[...]


def _inproj(x_main, x_tail, g, w, wg, wgt, bg, bgt):
    d = x_main.shape[1]
    tm = TOKEN_TILE
    n = x_main.shape[0] + tm
    nz = w.shape[1]
    ng = wgt.shape[0]
    full = lambda i: (0, 0)
    return pl.pallas_call(
        _inproj_kernel,
        grid=(n // tm,),
        in_specs=_stacked_specs(x_main, x_tail) + [
            pl.BlockSpec((1, d), full),
            pl.BlockSpec((d, nz), full),
            pl.BlockSpec((d, LANES), full),
            pl.BlockSpec((ng, d), full),
            pl.BlockSpec((1, LANES), full),
            pl.BlockSpec((ng, 1), full),
        ],
        out_specs=[
            pl.BlockSpec((tm, nz), lambda i: (i, 0)),
            pl.BlockSpec((tm, LANES), lambda i: (i, 0)),
            pl.BlockSpec((ng, tm), lambda i: (0, i)),
        ],
        out_shape=[
            jax.ShapeDtypeStruct((n, nz), F32),
            jax.ShapeDtypeStruct((n, LANES), F32),
            jax.ShapeDtypeStruct((ng, n), F32),
        ],
        compiler_params=_params("parallel"),
        name="inproj",
    )(x_main, x_tail, g, w, wg, wgt, bg, bgt)


def _pool_prompt_kernel(u_ref, pw_ref, ps_ref, y_ref, st_ref, ext_ref, *, tt, gw):
    t = pl.program_id(1)
    nt = pl.num_programs(1)
    halo = POOL_BUF + 1
    width = ext_ref.shape[1]

    @pl.when(t == 0)
    def _():
        ext_ref[0:halo, :] = jnp.zeros((halo, width), F32)

    u = u_ref[...]
    ext_ref[halo:halo + tt, :] = u
    pos = t * tt + lax.broadcasted_iota(I32, (tt, 1), 0)
    for g, w in enumerate(POOL_WINDOWS):
        c0 = g * gw
        acc = u[:, c0:c0 + gw]
        for j in range(1, w):
            acc = acc + ext_ref[halo - j:halo - j + tt, c0:c0 + gw]
        cnt = jnp.minimum(w, pos + 1).astype(F32)
        d = acc / cnt - u[:, c0:c0 + gw]
        y = _dot(d.astype(BF16), pw_ref[g]) * ps_ref[:, c0:c0 + gw]
        y_ref[:, c0:c0 + gw] = y.astype(BF16)

    @pl.when(t == nt - 1)
    def _():
        st_ref[...] = ext_ref[tt + 1:tt + halo, :]

    ext_ref[0:halo, :] = ext_ref[tt:tt + halo, :]


def _pool_prompt(z, pw, ps, batch, seq):
    width = ps.shape[1]
    gw = width // len(POOL_WINDOWS)
    tt = POOL_TIME_TILE
    nt = seq // tt
    return pl.pallas_call(
        functools.partial(_pool_prompt_kernel, tt=tt, gw=gw),
        grid=(batch, nt),
        in_specs=[
            pl.BlockSpec((tt, width), lambda b, t: (b * nt + t, 0)),
            pl.BlockSpec(pw.shape, lambda b, t: (0, 0, 0)),
            pl.BlockSpec((1, width), lambda b, t: (0, 0)),
        ],
        out_specs=[
            pl.BlockSpec((tt, width), lambda b, t: (b * nt + t, 0)),
            pl.BlockSpec((None, POOL_BUF, width), lambda b, t: (b, 0, 0)),
        ],
        out_shape=[
            jax.ShapeDtypeStruct((batch * seq, width), BF16),
            jax.ShapeDtypeStruct((batch, POOL_BUF, width), F32),
        ],
        scratch_shapes=[pltpu.VMEM((POOL_BUF + 1 + tt, width), F32)],
        compiler_params=_params("parallel", "arbitrary"),
        name="pool_prompt",
    )(z, pw, ps)


def _pool_step_kernel(u_ref, buf_ref, pw_ref, ps_ref, y_ref, nb_ref, *, gw):
    u = u_ref[...]
    for g, w in enumerate(POOL_WINDOWS):
        c0 = g * gw
        acc = u[:, c0:c0 + gw]
        for j in range(1, w):
            acc = acc + buf_ref[POOL_BUF - j, :, c0:c0 + gw]
        cnt = float(min(w, PAST_LEN + 1))
        d = acc / cnt - u[:, c0:c0 + gw]
        y = _dot(d.astype(BF16), pw_ref[g]) * ps_ref[:, c0:c0 + gw]
        y_ref[:, c0:c0 + gw] = y.astype(BF16)
    nb_ref[0:POOL_BUF - 1] = buf_ref[1:POOL_BUF]
    nb_ref[POOL_BUF - 1] = u


def _pool_step(z, buf_t, pw, ps, row0):
    _, batch, width = buf_t.shape
    gw = width // len(POOL_WINDOWS)
    return pl.pallas_call(
        functools.partial(_pool_step_kernel, gw=gw),
        grid=(1,),
        in_specs=[
            pl.BlockSpec((batch, width), lambda i: (row0 // batch, 0)),
            pl.BlockSpec(buf_t.shape, lambda i: (0, 0, 0)),
            pl.BlockSpec(pw.shape, lambda i: (0, 0, 0)),
            pl.BlockSpec((1, width), lambda i: (0, 0)),
        ],
        out_specs=[
            pl.BlockSpec((batch, width), lambda i: (0, 0)),
            pl.BlockSpec(buf_t.shape, lambda i: (0, 0, 0)),
        ],
        out_shape=[
            jax.ShapeDtypeStruct((batch, width), BF16),
            jax.ShapeDtypeStruct(buf_t.shape, F32),
        ],
        compiler_params=_params("arbitrary"),
        name="pool_step",
    )(z, buf_t, pw, ps)


def _mlstm_prompt_kernel(*refs, nh, dh, nseq):
    seq_in = [refs[5 * s:5 * s + 5] for s in range(nseq)]
    h_ref, c_out, n_out, m_out, c_s, n_s, m_s = refs[5 * nseq:]
    ci = pl.program_id(1)
    nc = pl.num_programs(1)
    ln = seq_in[0][0].shape[0]

    @pl.when(ci == 0)
    def _():
        c_s[...] = jnp.zeros(c_s.shape, F32)
        n_s[...] = jnp.zeros(n_s.shape, F32)
        m_s[...] = jnp.zeros(m_s.shape, F32)

    row = lax.broadcasted_iota(I32, (ln, ln), 0)
    col = lax.broadcasted_iota(I32, (ln, ln), 1)
    causal = col <= row
    eye = col == row
    lane = lax.broadcasted_iota(I32, (nh, ln), 1)

    def to_col(r):
        return jnp.sum(jnp.where(eye, r, 0.0), axis=1, keepdims=True)

    scale = dh ** -0.5
    pairs = [(s, h) for s in range(nseq) for h in range(nh)]
    c_old = {p: c_s[p[0], p[1]] for p in pairs}
    n_old = {p: n_s[p[0], p[1]] for p in pairs}
    m_old = {p: m_s[p[0], p[1]][:, 0:1] for p in pairs}
    gates = []
    for s in range(nseq):
        gt = seq_in[s][4][...]
        bc_all = jax.nn.log_sigmoid(gt[nh:2 * nh])
        sh = 1
        while sh < ln:
            bc_all = bc_all + jnp.where(lane >= sh, pltpu.roll(bc_all, sh, 1), 0.0)
            sh *= 2
        gates.append((gt[0:nh], bc_all))
    new_state = {}
    for s, h in pairs:
        q_ref, k_ref, v_ref, o_ref, _ = seq_in[s]
        li_all, bc_all = gates[s]
        sl = slice(h * dh, (h + 1) * dh)
        q = q_ref[:, sl]
        k = k_ref[:, sl] * scale
        v = v_ref[:, sl]
        qb, kb, vb = q.astype(BF16), k.astype(BF16), v.astype(BF16)
        li_r = li_all[h:h + 1]
        bc_r = bc_all[h:h + 1]
        bc_c = to_col(bc_r)
        m0, c, n = m_old[s, h], c_old[s, h], n_old[s, h]

        dmat = jnp.where(causal, bc_c - bc_r + li_r, -jnp.inf)
        inter = bc_c + m0
        m_row = jnp.maximum(inter, jnp.max(dmat, axis=1, keepdims=True))
        w_intra = jnp.exp(dmat - m_row)
        w_inter = jnp.exp(inter - m_row)
        sc = _dot_nt(qb, kb) * w_intra
        num = _dot(sc.astype(BF16), vb) + w_inter * _dot_nt(qb, c.astype(BF16))
        den = jnp.sum(sc, axis=1, keepdims=True) + w_inter * jnp.sum(q * n, axis=1, keepdims=True)
        hh = num / jnp.maximum(jnp.abs(den), jnp.exp(-m_row))
        h_ref[s, :, sl] = (hh * jax.nn.sigmoid(o_ref[:, sl])).astype(BF16)

        b_end = bc_r[:, ln - 1:ln]
        g_r = b_end - bc_r + li_r
        m_new = jnp.maximum(b_end + m0, jnp.max(g_r, axis=1, keepdims=True))
        wg_c = jnp.exp(to_col(g_r) - m_new)
        decay = jnp.exp(b_end + m0 - m_new)
        new_state[s, h] = (decay * c + _dot_tn((v * wg_c).astype(BF16), kb),
                           decay * n + jnp.sum(wg_c * k, axis=0, keepdims=True),
                           jnp.broadcast_to(m_new, (1, dh)))

    for (s, h), (c_new, n_new, m_new) in new_state.items():
        c_s[s, h] = c_new
        n_s[s, h] = n_new
        m_s[s, h] = m_new

    @pl.when(ci == nc - 1)
    def _():
        c_out[...] = c_s[...]
        n_out[...] = n_s[...]
        m_out[...] = m_s[...]


def _mlstm_prompt(z, gates_t, batch, seq, nh, dh):
    ln = MLSTM_CHUNK
    nc = seq // ln
    nseq = MLSTM_SEQ_PER_STEP
    width = nh * dh
    ng = gates_t.shape[0]
    rows_of = lambda s: (lambda b, c: (b * nseq + s) * nc + c)
    in_specs, operands = [], []
    for s in range(nseq):
        r = rows_of(s)
        for j in (1, 2, 3, 4):
            in_specs.append(pl.BlockSpec((ln, width), lambda b, c, r=r, j=j: (r(b, c), j)))
        in_specs.append(pl.BlockSpec((ng, ln), lambda b, c, r=r: (0, r(b, c))))
        operands += [z, z, z, z, gates_t]
    h_spec = pl.BlockSpec((None, nseq, ln, width), lambda b, c: (b, 0, c, 0))
    st = lambda a, b_: pl.BlockSpec((nseq, nh, a, b_), lambda b, c: (b, 0, 0, 0))
    return pl.pallas_call(
        functools.partial(_mlstm_prompt_kernel, nh=nh, dh=dh, nseq=nseq),
        grid=(batch // nseq, nc),
        in_specs=in_specs,
        out_specs=[h_spec, st(dh, dh), st(1, dh), st(1, dh)],
        out_shape=[
            jax.ShapeDtypeStruct((batch // nseq, nseq, seq, width), BF16),
            jax.ShapeDtypeStruct((batch, nh, dh, dh), F32),
            jax.ShapeDtypeStruct((batch, nh, 1, dh), F32),
            jax.ShapeDtypeStruct((batch, nh, 1, dh), F32),
        ],
        scratch_shapes=[pltpu.VMEM((nseq, nh, dh, dh), F32), pltpu.VMEM((nseq, nh, 1, dh), F32),
                        pltpu.VMEM((nseq, nh, 1, dh), F32)],
        compiler_params=_params("parallel", "arbitrary"),
        name="mlstm_prompt",
    )(*operands)


def _mlstm_step_kernel(q_ref, k_ref, v_ref, o_ref, li_ref, fp_ref, c_ref, n_ref, m_ref,
                       h_ref, c_out, n_out, m_out, *, nh, dh):
    eye = (lax.broadcasted_iota(I32, (1, dh, dh), 1) == lax.broadcasted_iota(I32, (1, dh, dh), 2))
    scale = dh ** -0.5
    for h in range(nh):
        sl = slice(h * dh, (h + 1) * dh)
        q = q_ref[:, :, sl]
        k = k_ref[:, :, sl] * scale
        v = v_ref[:, :, sl]
        c = c_ref[:, h]
        n = n_ref[:, h]
        m = m_ref[:, h]
        li = li_ref[:, h]
        lf = jax.nn.log_sigmoid(fp_ref[:, h])
        inter = lf + m
        m_row = jnp.maximum(inter, li)
        w_intra = jnp.exp(li - m_row)
        w_inter = jnp.exp(inter - m_row)
        sc = jnp.sum(q * k, axis=-1, keepdims=True) * w_intra
        v_c = jnp.sum(jnp.where(eye, v, 0.0), axis=-1, keepdims=True)
        num = sc * v_c + w_inter * jnp.sum(c * q, axis=-1, keepdims=True)
        den = sc + w_inter * jnp.sum(n * q, axis=-1, keepdims=True)
        h_c = num / jnp.maximum(jnp.abs(den), jnp.exp(-m_row))
        h_l = jnp.sum(jnp.where(eye, h_c, 0.0), axis=1, keepdims=True)
        h_ref[:, :, sl] = h_l * jax.nn.sigmoid(o_ref[:, :, sl])
        wg = jnp.exp(li - m_row)
        decay = jnp.exp(inter - m_row)
        c_out[:, h] = decay * c + (v_c * wg) * k
        n_out[:, h] = decay * n + wg * k
        m_out[:, h] = m_row


def _mlstm_step(z3, li, fp, c, n, m, nh, dh):
    batch = c.shape[0]
    bb = MLSTM_STEP_BATCH
    width = nh * dh
    blk = lambda j: pl.BlockSpec((bb, 1, width), lambda i: (i, 0, j))
    st4 = lambda a, b: pl.BlockSpec((bb, nh, a, b), lambda i: (i, 0, 0, 0))
    return pl.pallas_call(
        functools.partial(_mlstm_step_kernel, nh=nh, dh=dh),
        grid=(batch // bb,),
        in_specs=[blk(1), blk(2), blk(3), blk(4), st4(1, 1), st4(1, 1),
                  st4(dh, dh), st4(1, dh), st4(1, 1)],
        out_specs=[pl.BlockSpec((bb, 1, width), lambda i: (i, 0, 0)),
                   st4(dh, dh), st4(1, dh), st4(1, 1)],
        out_shape=[
            jax.ShapeDtypeStruct((batch, 1, width), F32),
            jax.ShapeDtypeStruct((batch, nh, dh, dh), F32),
            jax.ShapeDtypeStruct((batch, nh, 1, dh), F32),
            jax.ShapeDtypeStruct((batch, nh, 1, 1), F32),
        ],
        compiler_params=_params("parallel"),
        name="mlstm_step",
    )(z3, z3, z3, z3, li, fp, c, n, m)


def _split_bf16(a):
    hi = a.astype(BF16)
    return hi, (a - hi.astype(F32)).astype(BF16)


def _route(hn, wr_ref, br_ref, tril_ref, carry_ref, rinfo_ref, rinfo_t_ref):
    ngrp, epg = MOE_GROUPS, MOE_EXPERTS_PER_GROUP
    h_hi, h_lo = _split_bf16(hn)
    both = _dot(h_hi, wr_ref[...])
    logits = both[:, 0:LANES] + (both[:, LANES:2 * LANES] + _dot(h_lo, wr_ref[:, 0:LANES])) + br_ref[...]
    tm = logits.shape[0]
    lane = lax.broadcasted_iota(I32, (tm, LANES), 1)
    neg = -jnp.inf

    def first_max(x):
        mx = jnp.max(x, axis=1, keepdims=True)
        idx = jnp.min(jnp.where(x == mx, lane, LANES), axis=1, keepdims=True)
        return mx, idx

    is_grp = lane < ngrp
    gmax, gsel = first_max(jnp.where(is_grp, logits, neg))
    g_w = 1.0 / jnp.sum(jnp.where(is_grp, jnp.exp(logits - gmax), 0.0), axis=1, keepdims=True)
    lo = ngrp + gsel * epg
    el = jnp.where((lane >= lo) & (lane < lo + epg), logits, neg)
    v1, i1 = first_max(el)
    v2, i2 = first_max(jnp.where(lane == i1, neg, el))
    e2 = jnp.exp(v2 - v1)
    w1 = g_w / (1.0 + e2)
    w2 = g_w * e2 / (1.0 + e2)
    eid1 = i1 - ngrp
    eid2 = i2 - ngrp

    hit1 = lane == eid1
    hit2 = lane == eid2
    onehot = jnp.where(hit1 | hit2, 1.0, 0.0)
    carry = carry_ref[...]
    prefix = _dot(tril_ref[...], onehot.astype(BF16)) + carry
    rank1 = jnp.sum(jnp.where(hit1, prefix, 0.0), axis=1, keepdims=True)
    rank2 = jnp.sum(jnp.where(hit2, prefix, 0.0), axis=1, keepdims=True)
    carry_ref[...] = carry + jnp.sum(onehot, axis=0, keepdims=True)

    cols = (eid1.astype(F32), eid2.astype(F32), w1, w2, rank1, rank2)
    info = jnp.zeros((tm, LANES), F32)
    for j, cval in enumerate(cols):
        info = jnp.where(lane == j, cval, info)
    rinfo_ref[...] = info
    pick = jnp.where(lax.broadcasted_iota(I32, (SUBLANES, LANES), 0)
                     == lax.broadcasted_iota(I32, (SUBLANES, LANES), 1), 1.0, 0.0).astype(BF16)
    parts = [_dot_nt(pick, p) for p in _split3(info)]
    rinfo_t_ref[...] = parts[0] + (parts[1] + parts[2])


def _outproj_route_kernel(xm_ref, xt_ref, pm_ref, pt_ref, mm_ref, mt_ref,
                          wo_ref, g_ref, wr_ref, br_ref, tril_ref,
                          x1_ref, hn_ref, rinfo_ref, rinfo_t_ref, cnt_ref, carry_ref):
    @pl.when(pl.program_id(0) == 0)
    def _():
        carry_ref[...] = jnp.zeros(carry_ref.shape, F32)

    half = pm_ref.shape[1]
    mix = (_dot(_stacked(pm_ref, pt_ref), wo_ref[0:half, :])
           + _dot(_stacked(mm_ref, mt_ref), wo_ref[half:2 * half, :]))
    x1 = _stacked(xm_ref, xt_ref) + mix
    x1_ref[...] = x1
    hn = _rms(x1, g_ref[...])
    hn_ref[...] = hn
    _route(hn, wr_ref, br_ref, tril_ref, carry_ref, rinfo_ref, rinfo_t_ref)
    cnt_ref[...] = carry_ref[...]


def _glu_route_kernel(x_ref, ym_ref, yt_ref, wglu_ref, g_ref, wr_ref, br_ref, tril_ref,
                      x1_ref, hn_ref, rinfo_ref, rinfo_t_ref, cnt_ref, carry_ref):
    @pl.when(pl.program_id(0) == 0)
    def _():
        carry_ref[...] = jnp.zeros(carry_ref.shape, F32)

    d = x_ref.shape[1]
    ag = _dot(jax.nn.gelu(_stacked(ym_ref, yt_ref)).astype(BF16), wglu_ref[...])
    x1 = x_ref[...] + ag[:, 0:d] * jax.nn.sigmoid(ag[:, d:2 * d])
    x1_ref[...] = x1
    hn = _rms(x1, g_ref[...])
    hn_ref[...] = hn
    _route(hn, wr_ref, br_ref, tril_ref, carry_ref, rinfo_ref, rinfo_t_ref)
    cnt_ref[...] = carry_ref[...]


def _mix_route(kernel, name, n, row_specs, rows, w, g, wr, br, tril):
    d = g.shape[1]
    tm = TOKEN_TILE
    full = lambda i: (0, 0)
    return pl.pallas_call(
        kernel,
        grid=(n // tm,),
        in_specs=row_specs + [
            pl.BlockSpec(w.shape, full),
            pl.BlockSpec((1, d), full),
            pl.BlockSpec((d, 2 * LANES), full),
            pl.BlockSpec((1, LANES), full),
            pl.BlockSpec((tm, tm), full),
        ],
        out_specs=[
            pl.BlockSpec((tm, d), lambda i: (i, 0)),
            pl.BlockSpec((tm, d), lambda i: (i, 0)),
            pl.BlockSpec((tm, LANES), lambda i: (i, 0)),
            pl.BlockSpec((SUBLANES, tm), lambda i: (0, i)),
            pl.BlockSpec((1, LANES), full),
        ],
        out_shape=[
            jax.ShapeDtypeStruct((n, d), F32),
            jax.ShapeDtypeStruct((n, d), F32),
            jax.ShapeDtypeStruct((n, LANES), F32),
            jax.ShapeDtypeStruct((SUBLANES, n), F32),
            jax.ShapeDtypeStruct((1, LANES), F32),
        ],
        scratch_shapes=[pltpu.VMEM((1, LANES), F32)],
        compiler_params=_params("arbitrary"),
        name=name,
    )(*rows, w, g, wr, br, tril)


def _index_copy(pos_hbm, idx_s, sem_i, tile, slot):
    return pltpu.make_async_copy(pos_hbm.at[tile], idx_s.at[slot], sem_i.at[slot])


def _dispatch_kernel(pos_hbm, hn_hbm, xs_hbm, idx_s, tiles, sem_i, sem_l, sem_d):
    i = pl.program_id(0)
    nt = pl.num_programs(0)
    nslot, tm, _ = tiles.shape
    islot = i % 2

    def load(t):
        s = t % nslot
        return pltpu.make_async_copy(hn_hbm.at[pl.ds(t * tm, tm)], tiles.at[s], sem_l.at[s])

    def wait_rows(t):
        s = t % nslot
        whole = pltpu.make_async_copy(tiles.at[s], xs_hbm.at[pl.ds(0, tm)], sem_d.at[s])
        whole.wait()
        whole.wait()

    @pl.when(i == 0)
    def _():
        _index_copy(pos_hbm, idx_s, sem_i, 0, 0).start()
        load(0).start()

        @pl.when(nt > 1)
        def _():
            load(1).start()

    @pl.when(i >= 2)
    def _():
        wait_rows(i - 2)

    @pl.when(i + 2 < nt)
    def _():
        load(i + 2).start()

    _index_copy(pos_hbm, idx_s, sem_i, i, islot).wait()

    @pl.when(i + 1 < nt)
    def _():
        _index_copy(pos_hbm, idx_s, sem_i, i + 1, 1 - islot).start()

    load(i).wait()
    slot = i % nslot

    def issue(r, carry):
        row = tiles.at[slot, pl.ds(r, 1)]
        pltpu.make_async_copy(row, xs_hbm.at[pl.ds(idx_s[islot, 0, r], 1)],
                              sem_d.at[slot]).start(priority=0)
        pltpu.make_async_copy(row, xs_hbm.at[pl.ds(idx_s[islot, 0, tm + r], 1)],
                              sem_d.at[slot]).start(priority=1)
        return carry

    lax.fori_loop(0, tm, issue, 0, unroll=DMA_ISSUE_UNROLL)

    @pl.when(i == nt - 1)
    def _():
        @pl.when(nt > 1)
        def _():
            wait_rows(i - 1)

        wait_rows(i)


def _dispatch(pos_tiles, hn):
    n, d = hn.shape
    tm = TOKEN_TILE
    nslot = 4
    return pl.pallas_call(
        _dispatch_kernel,
        grid=(n // tm,),
        in_specs=[pl.BlockSpec(memory_space=pl.ANY), pl.BlockSpec(memory_space=pl.ANY)],
        out_specs=pl.BlockSpec(memory_space=pl.ANY),
        out_shape=jax.ShapeDtypeStruct((2 * n, d), F32),
        scratch_shapes=[pltpu.SMEM((2, 1, 2 * tm), I32), pltpu.VMEM((nslot, tm, d), F32),
                        pltpu.SemaphoreType.DMA((2,)), pltpu.SemaphoreType.DMA((nslot,)),
                        pltpu.SemaphoreType.DMA((nslot,))],
        compiler_params=_params("arbitrary"),
        name="moe_dispatch",
    )(pos_tiles, hn)


def _moe_kernel(vt_ref, ve_ref, von_ref, vnext_ref, vslot_ref, lo_ref, hi_ref,
                xs_ref, wg_hbm, wu_hbm, wd_hbm, eo_ref,
                wg_f, wu_f, wd_f, wgb, wub, wdb, sem_w, *, layer):
    v = pl.program_id(0)
    tr = xs_ref.shape[0]
    prev = jnp.maximum(v - 1, 0)
    e = ve_ref[v]
    new_expert = jnp.logical_or(v == 0, e != ve_ref[prev])
    first_visit = jnp.logical_or(v == 0, vt_ref[v] != vt_ref[prev])

    def fetch(expert, slot):
        return [pltpu.make_async_copy(src.at[layer, expert], dst.at[slot], sem_w.at[slot, j])
                for j, (src, dst) in enumerate(((wg_hbm, wg_f), (wu_hbm, wu_f), (wd_hbm, wd_f)))]

    @pl.when(von_ref[v] == 1)
    def _():
        @pl.when(new_expert)
        def _():
            slot = vslot_ref[v]

            @pl.when(v == 0)
            def _():
                for cp in fetch(e, slot):
                    cp.start()

            for cp in fetch(e, slot):
                cp.wait()
            wgb[...] = wg_f[slot].astype(BF16)
            wub[...] = wu_f[slot].astype(BF16)
            wdb[...] = wd_f[slot].astype(BF16)

            @pl.when(vnext_ref[v] >= 0)
            def _():
                for cp in fetch(vnext_ref[v], 1 - slot):
                    cp.start()

        x = xs_ref[...].astype(BF16)
        act = jax.nn.silu(_dot(x, wgb[...])) * _dot(x, wub[...])
        row = vt_ref[v] * tr + lax.broadcasted_iota(I32, (tr, 1), 0)
        act = jnp.where((row >= lo_ref[e]) & (row < hi_ref[e]), act, 0.0)
        res = _dot(act.astype(BF16), wdb[...])

        @pl.when(first_visit)
        def _():
            eo_ref[...] = res

        @pl.when(jnp.logical_not(first_visit))
        def _():
            eo_ref[...] += res


def _moe_experts(vt, ve, von, vnext, vslot, lo, hi, xs, wg, wu, wd, layer):
    nv = vt.shape[0]
    rows, d = xs.shape
    hid = wg.shape[3]
    tr = EXPERT_ROW_TILE
    tile = pl.BlockSpec((tr, d), lambda v, vt, *_: (vt[v], 0))
    grid_spec = pltpu.PrefetchScalarGridSpec(
        num_scalar_prefetch=7,
        grid=(nv,),
        in_specs=[tile] + [pl.BlockSpec(memory_space=pl.ANY)] * 3,
        out_specs=tile,
        scratch_shapes=[
            pltpu.VMEM((2, d, hid), F32), pltpu.VMEM((2, d, hid), F32), pltpu.VMEM((2, hid, d), F32),
            pltpu.VMEM((d, hid), BF16), pltpu.VMEM((d, hid), BF16), pltpu.VMEM((hid, d), BF16),
            pltpu.SemaphoreType.DMA((2, 3)),
        ],
    )
    return pl.pallas_call(
        functools.partial(_moe_kernel, layer=layer),
        grid_spec=grid_spec,
        out_shape=jax.ShapeDtypeStruct((rows, d), F32),
        compiler_params=_params("arbitrary"),
        name="moe_experts",
    )(vt, ve, von, vnext, vslot, lo, hi, xs, wg, wu, wd)


def _combine_kernel(pos_hbm, eo_hbm, x_ref, rinfo_ref, g_ref, o1_ref, o2_ref, idx_s, a_buf, b_buf,
                    sem_i, sem_a, sem_b, *, last_layer):
    i = pl.program_id(0)
    nt = pl.num_programs(0)
    tm = a_buf.shape[1]
    slot = i % 2

    def gathers(s):
        def issue(r, carry):
            pltpu.make_async_copy(eo_hbm.at[pl.ds(idx_s[s, 0, r], 1)],
                                  a_buf.at[s, pl.ds(r, 1)], sem_a.at[s]).start(priority=0)
            pltpu.make_async_copy(eo_hbm.at[pl.ds(idx_s[s, 0, tm + r], 1)],
                                  b_buf.at[s, pl.ds(r, 1)], sem_b.at[s]).start(priority=1)
            return carry

        lax.fori_loop(0, tm, issue, 0, unroll=DMA_ISSUE_UNROLL)

    @pl.when(i == 0)
    def _():
        first = _index_copy(pos_hbm, idx_s, sem_i, 0, 0)
        first.start()
        first.wait()
        gathers(0)

        @pl.when(nt > 1)
        def _():
            _index_copy(pos_hbm, idx_s, sem_i, 1, 1).start()

    @pl.when(i + 1 < nt)
    def _():
        _index_copy(pos_hbm, idx_s, sem_i, i + 1, 1 - slot).wait()
        gathers(1 - slot)

    @pl.when(i + 2 < nt)
    def _():
        _index_copy(pos_hbm, idx_s, sem_i, i + 2, slot).start()

    pltpu.make_async_copy(eo_hbm.at[pl.ds(0, tm)], a_buf.at[slot], sem_a.at[slot]).wait()
    pltpu.make_async_copy(eo_hbm.at[pl.ds(0, tm)], b_buf.at[slot], sem_b.at[slot]).wait()
    info = rinfo_ref[...]
    x2 = x_ref[...] + (info[:, 2:3] * a_buf[slot] + info[:, 3:4] * b_buf[slot])
    hn = _rms(x2, g_ref[...])
    if last_layer:
        @pl.when(i < nt - 1)
        def _():
            o1_ref[...] = hn

        @pl.when(i == nt - 1)
        def _():
            o2_ref[...] = hn
    else:
        o1_ref[...] = x2
        o2_ref[...] = hn


def _combine(pos_tiles, eo, x, rinfo, g, last_layer):
    n, d = x.shape
    tm = TOKEN_TILE
    nt = n // tm
    row = pl.BlockSpec((tm, d), lambda i: (i, 0))
    if last_layer:
        out_specs = [pl.BlockSpec((tm, d), lambda i: (jnp.minimum(i, nt - 2), 0)),
                     pl.BlockSpec((tm, d), lambda i: (0, 0))]
        out_shape = [jax.ShapeDtypeStruct((n - tm, d), F32), jax.ShapeDtypeStruct((tm, d), F32)]
    else:
        out_specs = [row, row]
        out_shape = [jax.ShapeDtypeStruct((n, d), F32), jax.ShapeDtypeStruct((n, d), F32)]
    return pl.pallas_call(
        functools.partial(_combine_kernel, last_layer=last_layer),
        grid=(nt,),
        in_specs=[
            pl.BlockSpec(memory_space=pl.ANY),
            pl.BlockSpec(memory_space=pl.ANY),
            row,
            pl.BlockSpec((tm, LANES), lambda i: (i, 0)),
            pl.BlockSpec((1, d), lambda i: (0, 0)),
        ],
        out_specs=out_specs,
        out_shape=out_shape,
        scratch_shapes=[
            pltpu.SMEM((2, 1, 2 * tm), I32),
            pltpu.VMEM((2, tm, d), F32),
            pltpu.VMEM((2, tm, d), F32),
            pltpu.SemaphoreType.DMA((2,)),
            pltpu.SemaphoreType.DMA((2,)),
            pltpu.SemaphoreType.DMA((2,)),
        ],
        compiler_params=_params("arbitrary"),
        name="moe_combine",
    )(pos_tiles, eo, x, rinfo, g)


def _moe(x1, hn, rinfo, rinfo_t, counts, wg, wu, wd, layer, next_gain, last_layer):
    n = x1.shape[0]
    ne = wg.shape[1]
    tr = EXPERT_ROW_TILE
    tm = TOKEN_TILE
    nv = (2 * n) // tr + ne - 1
    eid = rinfo_t[0:2].astype(I32)
    rank = rinfo_t[4:6].astype(I32)
    cnt = counts[0, :ne].astype(I32)
    seg_end = jnp.cumsum(cnt)
    seg_start = seg_end - cnt
    experts = jnp.arange(ne, dtype=I32)
    pos = rank + jnp.sum(jnp.where(eid[:, None, :] == experts[None, :, None],
                                   seg_start[None, :, None], 0), axis=1)
    pos_tiles = pos.reshape(2, n // tm, tm).transpose(1, 0, 2).reshape(n // tm, 1, 2 * tm)

    first_tile = seg_start // tr
    tiles_e = jnp.where(cnt > 0, (seg_end - 1) // tr - first_tile + 1, 0)
    v_end = jnp.cumsum(tiles_e)
    v_start = v_end - tiles_e
    total = v_end[-1]
    vis = jnp.arange(nv, dtype=I32)
    vc = jnp.minimum(vis, jnp.maximum(total - 1, 0))
    ve = jnp.sum((vc[:, None] >= v_end[None, :]).astype(I32), axis=1)
    pick = lambda tab: jnp.sum(jnp.where(ve[:, None] == experts, tab, 0), axis=1)
    vt = pick(first_tile) + (vc - pick(v_start))
    von = (vis < total).astype(I32)
    present = cnt > 0
    later = present[None, :] & (experts[None, :] > experts[:, None])
    next_of = jnp.min(jnp.where(later, experts[None, :], ne), axis=1)
    vnext = pick(jnp.where(next_of < ne, next_of, -1))
    vslot = pick(jnp.cumsum(present.astype(I32)) - 1) & 1

    xs = _dispatch(pos_tiles, hn)
    eo = _moe_experts(vt, ve, von, vnext, vslot, seg_start, seg_end, xs, wg, wu, wd, layer)
    return _combine(pos_tiles, eo, x1, rinfo, next_gain, last_layer)


def _cis(log_mag, ang):
    mag = jnp.exp(log_mag)
    return mag * jnp.cos(ang), mag * jnp.sin(ang)


def _split3(a):
    p1 = a.astype(BF16)
    r1 = a - p1.astype(F32)
    p2 = r1.astype(BF16)
    return p1, p2, (r1 - p2.astype(F32)).astype(BF16)


def _select_dot(a, sel, sel_first=False):
    sel = sel.astype(BF16)
    parts = [(_dot(sel, p) if sel_first else _dot(p, sel)) for p in _split3(a)]
    return parts[0] + (parts[1] + parts[2])


def _dot3(a, b):
    a_hi, a_lo = _split_bf16(a)
    b_hi, b_lo = _split_bf16(b)
    return _dot(a_hi, b_hi) + (_dot(a_lo, b_hi) + _dot(a_hi, b_lo))


def _s5_prep_kernel(lsc_ref, lsr_ref, lrc_ref, lic_ref, lrr_ref, lir_ref, ctre_ref, ctim_ref,
                    btre_ref, btim_ref, d_ref,
                    t_ref, wre_ref, wim_ref, cyre_ref, cyim_ref, apr_ref, api_ref,
                    bbr_ref, bbi_ref, lbr_ref, lbi_ref, *, gw, gpb, sub):
    blk = gpb * gw
    wide = sub * blk
    nsl = lrc_ref.shape[1]
    ns = nsl // gpb
    sh_blk, sh_gw, sh_ns = blk.bit_length() - 1, gw.bit_length() - 1, ns.bit_length() - 1
    dt_c, dt_r = jnp.exp(lsc_ref[0]), jnp.exp(lsr_ref[0])
    ldt_c_re, ldt_c_im = lrc_ref[0] * dt_c, lic_ref[0] * dt_c
    ldt_r_re, ldt_r_im = lrr_ref[0] * dt_r, lir_ref[0] * dt_r

    lane_w = lax.broadcasted_iota(I32, (1, wide), 1)
    spread = jnp.where((lax.broadcasted_iota(I32, (gw, wide), 1) & (gw - 1))
                       == lax.broadcasted_iota(I32, (gw, wide), 0), 1.0, 0.0)
    cre = _select_dot(ctre_ref[0], spread)
    cim = _select_dot(ctim_ref[0], spread)
    same = (jnp.right_shift(lax.broadcasted_iota(I32, (nsl, 1), 0), sh_ns)
            == (jnp.right_shift(lane_w, sh_gw) & (gpb - 1)))
    assert 2 * blk == LANES
    tau = lax.broadcasted_iota(I32, (1, LANES), 1).astype(F32)
    pw_re, pw_im = _cis(tau * ldt_c_re, tau * ldt_c_im)
    low = lax.broadcasted_iota(I32, (nsl, LANES), 1) < blk

    def spread_pow(p, first):
        col = lambda t: jnp.broadcast_to(p[:, t:t + 1], (nsl, LANES))
        return jnp.concatenate([jnp.where(low, col(first + 2 * m), col(first + 2 * m + 1))
                                for m in range(sub // 2)], axis=1)

    def c_lam_pow(first):
        pr, pi = spread_pow(pw_re, first), spread_pow(pw_im, first)
        return (jnp.where(same, pr * cre - pi * cim, 0.0),
                jnp.where(same, -(pr * cim + pi * cre), 0.0))

    clr0, cli0 = c_lam_pow(0)
    clr1, cli1 = c_lam_pow(1)
    cyre_ref[0] = clr1.astype(BF16)
    cyim_ref[0] = cli1.astype(BF16)

    lbr, lbi = _cis(ldt_r_re, ldt_r_im)
    lbr_ref[0] = lbr
    lbi_ref[0] = lbi
    lr, li = lrr_ref[0], lir_ref[0]
    nr, ni = lbr - 1.0, lbi
    den = lr * lr + li * li
    fr = (nr * lr + ni * li) / den
    fi = (ni * lr - nr * li) / den
    bre, bim = btre_ref[0], btim_ref[0]
    bbr = fr * bre - fi * bim
    bbi = fr * bim + fi * bre
    bbr_ref[0] = bbr
    bbi_ref[0] = bbi

    lane_t = lax.broadcasted_iota(I32, (gw, wide), 1)
    chan = lax.broadcasted_iota(I32, (gw, wide), 0)
    for g in range(gpb):
        ps = slice(g * ns, (g + 1) * ns)
        r = _dot3(bbr[:, ps], clr0[ps, :]) + _dot3(bbi[:, ps], cli0[ps, :])
        r = r + jnp.where(lane_t == g * gw + chan, d_ref[0][:, g:g + 1], 0.0)
        for j in range(sub):
            tb = r if j == 0 else jnp.where(lane_t >= blk * j, pltpu.roll(r, blk * j, 1), 0.0)
            r0 = j * blk + g * gw
            t_ref[0, r0:r0 + gw, :] = tb.astype(BF16)

    rows = lax.broadcasted_iota(I32, (wide, 1), 0)
    spread_t = jnp.where((lax.broadcasted_iota(I32, (wide, gw), 0) & (gw - 1))
                         == lax.broadcasted_iota(I32, (wide, gw), 1), 1.0, 0.0)
    bbr_t = _select_dot(bbr, spread_t, sel_first=True)
    bbi_t = _select_dot(bbi, spread_t, sel_first=True)
    same_w = ((jnp.right_shift(rows, sh_gw) & (gpb - 1))
              == jnp.right_shift(lax.broadcasted_iota(I32, (1, nsl), 1), sh_ns))
    rj = ((sub - 1) - lax.broadcasted_iota(I32, (sub, 1), 0)).astype(F32)
    q_re, q_im = _cis(rj * ldt_r_re, rj * ldt_r_im)
    per_step = lambda q: jnp.concatenate(
        [jnp.broadcast_to(q[j:j + 1, :], (blk, nsl)) for j in range(sub)], axis=0)
    pr, pi = per_step(q_re), per_step(q_im)
    wre_ref[0] = jnp.where(same_w, pr * bbr_t - pi * bbi_t, 0.0).astype(BF16)
    wim_ref[0] = jnp.where(same_w, pr * bbi_t + pi * bbr_t, 0.0).astype(BF16)

    nlev = apr_ref.shape[1]
    pw = (sub * jnp.left_shift(1, lax.broadcasted_iota(I32, (nlev, 1), 0))).astype(F32)
    apr, api = _cis(pw * ldt_r_re, pw * ldt_r_im)
    apr_ref[0] = apr
    api_ref[0] = api


def _s5_prep(log_step, lam_re, lam_im, b_re, b_im, c_re, c_im, d_skip):
    ng, ns = lam_re.shape
    gw = b_re.shape[2]
    sub, gpb = S5_SUB, S5_GROUPS_PER_BLOCK
    nblk = ng // gpb
    nsl = gpb * ns
    wide = sub * gpb * gw
    ls = jnp.repeat(log_step, ns)
    ct = lambda c: jnp.transpose(c, (0, 2, 1)).reshape(nblk, nsl, gw)
    bt = lambda b: jnp.transpose(b.reshape(nblk, gpb, ns, gw), (0, 3, 1, 2)).reshape(nblk, gw, nsl)
    ins = [ls.reshape(nblk, nsl, 1), ls.reshape(nblk, 1, nsl),
           lam_re.reshape(nblk, nsl, 1), lam_im.reshape(nblk, nsl, 1),
           lam_re.reshape(nblk, 1, nsl), lam_im.reshape(nblk, 1, nsl),
           ct(c_re), ct(c_im), bt(b_re), bt(b_im),
           jnp.transpose(d_skip.reshape(nblk, gpb, gw), (0, 2, 1))]
    spec = lambda a: pl.BlockSpec((1,) + a.shape[1:], lambda g: (g, 0, 0))
    outs = [
        jax.ShapeDtypeStruct((nblk, wide, wide), BF16),
        jax.ShapeDtypeStruct((nblk, wide, nsl), BF16),
        jax.ShapeDtypeStruct((nblk, wide, nsl), BF16),
        jax.ShapeDtypeStruct((nblk, nsl, wide), BF16),
        jax.ShapeDtypeStruct((nblk, nsl, wide), BF16),
        jax.ShapeDtypeStruct((nblk, SUBLANES, nsl), F32),
        jax.ShapeDtypeStruct((nblk, SUBLANES, nsl), F32),
        jax.ShapeDtypeStruct((nblk, gw, nsl), F32),
        jax.ShapeDtypeStruct((nblk, gw, nsl), F32),
        jax.ShapeDtypeStruct((nblk, 1, nsl), F32),
        jax.ShapeDtypeStruct((nblk, 1, nsl), F32),
    ]
    return pl.pallas_call(
        functools.partial(_s5_prep_kernel, gw=gw, gpb=gpb, sub=sub),
        grid=(nblk,),
        in_specs=[spec(a) for a in ins],
        out_specs=[spec(o) for o in outs],
        out_shape=outs,
        compiler_params=_params("parallel"),
        name="s5_prep",
    )(*ins)


def _s5_prompt_kernel(x_ref, t_ref, wre_ref, wim_ref, cyre_ref, cyim_ref, apr_ref, api_ref,
                      y_ref, hre_out, him_out, u_s, y4_s, sre_s, sim_s, *, nb, seq, sub):
    nk = seq // sub
    rows = nb * nk
    hl = LANES // 2
    pad = nk // 2
    nlev = nk.bit_length() - 1
    nsl = sre_s.shape[1]
    low = lax.broadcasted_iota(I32, (nk, LANES), 1) < hl
    kidx = lax.broadcasted_iota(I32, (rows, 1), 0) & (nk - 1)
    sre_s[0:pad, :] = jnp.zeros((pad, nsl), F32)
    sim_s[0:pad, :] = jnp.zeros((pad, nsl), F32)

    def step_rows(b, j):
        return (pl.ds(b * seq + j, nk, stride=sub), slice(None))

    for b in range(nb):
        rs = slice(b * nk, (b + 1) * nk)
        for m in range(sub // 2):
            ls = slice(m * LANES, (m + 1) * LANES)
            s0 = x_ref[step_rows(b, 2 * m)]
            s1 = x_ref[step_rows(b, 2 * m + 1)]
            u_s[0, rs, ls] = jnp.where(low, s0, pltpu.roll(s1, hl, 1)).astype(BF16)
            u_s[1, rs, ls] = jnp.where(low, pltpu.roll(s0, hl, 1), s1).astype(BF16)

    live = slice(pad, pad + rows)
    for half in range(2):
        u = u_s[half]
        hre = _dot(u, wre_ref[half])
        him = _dot(u, wim_ref[half])
        for i in range(nlev):
            s = 1 << i
            sre_s[live, :] = hre
            sim_s[live, :] = him
            pre = sre_s[pad - s:pad - s + rows, :]
            pim = sim_s[pad - s:pad - s + rows, :]
            ar = apr_ref[half, i:i + 1, :]
            ai = api_ref[half, i:i + 1, :]
            ok = kidx >= s
            hre = hre + jnp.where(ok, ar * pre - ai * pim, 0.0)
            him = him + jnp.where(ok, ar * pim + ai * pre, 0.0)
        sre_s[live, :] = hre
        sim_s[live, :] = him
        ls = slice(half * nsl, (half + 1) * nsl)
        for b in range(nb):
            last = pad + (b + 1) * nk - 1
            hre_out[b:b + 1, ls] = sre_s[last:last + 1, :]
            him_out[b:b + 1, ls] = sim_s[last:last + 1, :]
        started = kidx >= 1
        hpre = jnp.where(started, sre_s[pad - 1:pad - 1 + rows, :], 0.0).astype(BF16)
        hpim = jnp.where(started, sim_s[pad - 1:pad - 1 + rows, :], 0.0).astype(BF16)
        y4_s[half] = _dot(u, t_ref[half]) + _dot(hpre, cyre_ref[half]) + _dot(hpim, cyim_ref[half])

    for b in range(nb):
        rs = slice(b * nk, (b + 1) * nk)
        for m in range(sub // 2):
            ls = slice(m * LANES, (m + 1) * LANES)
            ca = y4_s[0, rs, ls]
            cb = y4_s[1, rs, ls]
            y_ref[step_rows(b, 2 * m)] = jnp.where(low, ca, pltpu.roll(cb, hl, 1))
            y_ref[step_rows(b, 2 * m + 1)] = jnp.where(low, pltpu.roll(ca, hl, 1), cb)


def _s5_prompt(h, prep, batch, seq):
    d = h.shape[1]
    tmat, wre, wim, cyre, cyim, apr, api = prep[:7]
    nblk, wide, nsl = wre.shape
    sub, nb = S5_SUB, S5_BATCH_PER_STEP
    nk = seq // sub
    assert nk & (nk - 1) == 0 and nk.bit_length() - 1 <= apr.shape[1]
    ntile = d // LANES
    assert nblk == 2 * ntile
    rows = nb * nk
    wspec = lambda a: pl.BlockSpec((2,) + a.shape[1:], lambda t, b: (t, 0, 0))
    xspec = pl.BlockSpec((nb * seq, LANES), lambda t, b: (b, t))
    sspec = pl.BlockSpec((None, nb, 2 * nsl), lambda t, b: (b, 0, t))
    return pl.pallas_call(
        functools.partial(_s5_prompt_kernel, nb=nb, seq=seq, sub=sub),
        grid=(ntile, batch // nb),
        in_specs=[xspec, wspec(tmat), wspec(wre), wspec(wim), wspec(cyre), wspec(cyim),
                  wspec(apr), wspec(api)],
        out_specs=[xspec, sspec, sspec],
        out_shape=[
            jax.ShapeDtypeStruct((batch * seq, d), F32),
            jax.ShapeDtypeStruct((batch // nb, nb, ntile * 2 * nsl), F32),
            jax.ShapeDtypeStruct((batch // nb, nb, ntile * 2 * nsl), F32),
        ],
        scratch_shapes=[
            pltpu.VMEM((2, rows, wide), BF16),
            pltpu.VMEM((2, rows, wide), F32),
            pltpu.VMEM((nk // 2 + rows, nsl), F32),
            pltpu.VMEM((nk // 2 + rows, nsl), F32),
        ],
        compiler_params=_params("parallel", "parallel"),
        name="s5_prompt",
    )(h, tmat, wre, wim, cyre, cyim, apr, api)


def _s5_step_kernel(u_ref, h_ref, hsw_ref, bb_ref, la_ref, lb_ref, cc_ref, d_ref, hn_ref, y_ref):
    for g in range(u_ref.shape[0]):
        u = u_ref[g]
        hn = (la_ref[g] * h_ref[g] + lb_ref[g] * hsw_ref[g]
              + jnp.dot(u, bb_ref[g], precision=HIGHEST, preferred_element_type=F32))
        hn_ref[g] = hn
        y_ref[g] = (jnp.dot(hn, cc_ref[g], precision=HIGHEST, preferred_element_type=F32)
                    + d_ref[g] * u)


def _s5_step(u_p, h_cat, h_swp, bb_cat, la, lb, cc, d_p):
    ng = u_p.shape[0]
    gb = SUBLANES
    spec = lambda a: pl.BlockSpec((gb,) + a.shape[1:], lambda g: (g, 0, 0))
    ins = [u_p, h_cat, h_swp, bb_cat, la, lb, cc, d_p]
    outs = [jax.ShapeDtypeStruct(h_cat.shape, F32), jax.ShapeDtypeStruct(u_p.shape, F32)]
    return pl.pallas_call(
        _s5_step_kernel,
        grid=(ng // gb,),
        in_specs=[spec(a) for a in ins],
        out_specs=[spec(o) for o in outs],
        out_shape=outs,
        compiler_params=_params("parallel"),
        name="s5_step",
    )(*ins)


def _pad_to(a, axis, size):
    pad = [(0, 0)] * a.ndim
    pad[axis] = (0, size - a.shape[axis])
    return jnp.pad(a, pad)


def kernel(x_prompt, x_sample, state_pool, state_mlstm_c, state_mlstm_n, state_mlstm_m, state_s5_re, state_s5_im, norm_mix, norm_ffn, norm_final, w_in_ab, b_gates, pool_w, pool_scale, w_out_ab, s5_lam_re, s5_lam_im, s5_log_step, s5_b_re, s5_b_im, s5_c_re, s5_c_im, s5_d, w_glu, moe_w_group, moe_b_group, moe_w_expert, moe_b_expert, moe_w_gate, moe_w_up, moe_w_down):
    bp, tp, d = x_prompt.shape
    bs = x_sample.shape[0]
    n_p = bp * tp
    tm = TOKEN_TILE
    assert n_p % tm == 0 and bs <= tm
    n_fill = tm - bs
    nh, dh = state_mlstm_c.shape[2], state_mlstm_c.shape[3]
    pool_width = state_pool.shape[3]
    ml_width = nh * dh
    n_main = pool_width + 4 * ml_width
    n_gates = 2 * nh
    ngrp_s5, n_state = s5_lam_re.shape[1], s5_lam_re.shape[2]
    gw_s5 = d // ngrp_s5

    n = n_p + tm

    def tail_tile(sample_rows, dtype):
        return _pad_to(sample_rows.astype(dtype), 0, tm)

    x_main = x_prompt.reshape(n_p, d)
    x_tail = tail_tile(x_sample.reshape(bs, d), F32)
    tril = jnp.tril(jnp.ones((tm, tm), BF16), -1)

    def router_weights(l):
        wr = _pad_to(jnp.concatenate([moe_w_group[l], moe_w_expert[l]], axis=1), 1, LANES)
        br = jnp.concatenate([moe_b_group[l], moe_b_expert[l]])[None, :]
        hi = wr.astype(BF16)
        lo = (wr - hi.astype(F32)).astype(BF16)
        return jnp.concatenate([hi, lo], axis=1), _pad_to(br, 1, LANES)

    w_in = w_in_ab[0]
    w_g = w_in[:, n_main:]
    z, gates, gates_t = _inproj(
        x_main, x_tail, norm_mix[0][None, :], w_in[:, :n_main].astype(BF16),
        _pad_to(w_g, 1, LANES).astype(BF16), w_g.T.astype(BF16),
        _pad_to(b_gates[0][None, :], 1, LANES), b_gates[0][:, None])

    pw = pool_w[0].astype(BF16)
    ps = pool_scale[0][None, :]
    pool_y_p, pool_p = _pool_prompt(z, pw, ps, bp, tp)
    pool_y_s, pool_s_t = _pool_step(z, jnp.transpose(state_pool[0], (1, 0, 2)), pw, ps, n_p)
    pool_s = jnp.transpose(pool_s_t, (1, 0, 2))

    ml_y_p, c_p, n_p_st, m_p = _mlstm_prompt(z, gates_t, bp, tp, nh, dh)
    ml_y_p = ml_y_p.reshape(n_p, ml_width)
    g_s = gates[n_p:n_p + bs, :n_gates]
    ml_y_s, c_s, n_s_st, m_s = _mlstm_step(
        z[n_p:n_p + bs].reshape(bs, 1, n_main),
        g_s[:, :nh].reshape(bs, nh, 1, 1), g_s[:, nh:].reshape(bs, nh, 1, 1),
        state_mlstm_c[0], state_mlstm_n[0].reshape(bs, nh, 1, dh),
        state_mlstm_m[0].reshape(bs, nh, 1, 1), nh, dh)

    rows = [x_main, x_tail, pool_y_p, tail_tile(pool_y_s, BF16),
            ml_y_p, tail_tile(ml_y_s.reshape(bs, ml_width), BF16)]
    specs = (_stacked_specs(x_main, x_tail) + _stacked_specs(pool_y_p, rows[3])
             + _stacked_specs(ml_y_p, rows[5]))
    wr, br = router_weights(0)
    x1, hn, rinfo, rinfo_t, counts = _mix_route(
        _outproj_route_kernel, "outproj_route", n, specs, rows, w_out_ab[0].astype(BF16),
        norm_ffn[0][None, :], wr, br, tril)
    x2, h1 = _moe(x1, hn, rinfo, rinfo_t, counts, moe_w_gate, moe_w_up, moe_w_down, 0,
                  norm_mix[1][None, :], last_layer=False)

    prep = _s5_prep(s5_log_step[0], s5_lam_re[0], s5_lam_im[0], s5_b_re[0], s5_b_im[0],
                    s5_c_re[0], s5_c_im[0], s5_d[0])
    y_p, hre_p, him_p = _s5_prompt(h1, prep, bp, tp)
    s5_re_p = hre_p.reshape(bp, ngrp_s5, n_state)
    s5_im_p = him_p.reshape(bp, ngrp_s5, n_state)

    gpb = S5_GROUPS_PER_BLOCK
    per_group = lambda a: jnp.transpose(
        a.reshape(ngrp_s5 // gpb, gw_s5, gpb, n_state), (0, 2, 1, 3)).reshape(ngrp_s5, gw_s5, n_state)
    bbr, bbi = per_group(prep[7]), per_group(prep[8])
    lbr, lbi = prep[9].reshape(ngrp_s5, 1, n_state), prep[10].reshape(ngrp_s5, 1, n_state)
    d_g = s5_d[0].reshape(ngrp_s5, 1, gw_s5)
    u_s = h1[n_p:n_p + bs].reshape(bs, ngrp_s5, gw_s5).transpose(1, 0, 2)
    h_re = jnp.transpose(state_s5_re[0], (1, 0, 2))
    h_im = jnp.transpose(state_s5_im[0], (1, 0, 2))
    cc = jnp.concatenate([jnp.transpose(s5_c_re[0], (0, 2, 1)),
                          -jnp.transpose(s5_c_im[0], (0, 2, 1))], axis=1)
    hn_s, y_s = _s5_step(
        _pad_to(u_s, 2, LANES),
        jnp.concatenate([h_re, h_im], axis=2), jnp.concatenate([h_im, h_re], axis=2),
        _pad_to(jnp.concatenate([bbr, bbi], axis=2), 1, LANES),
        jnp.concatenate([lbr, lbr], axis=2), jnp.concatenate([-lbi, lbi], axis=2),
        _pad_to(cc, 2, LANES), _pad_to(d_g, 2, LANES))
    s5_re_s = jnp.transpose(hn_s[:, :, :n_state], (1, 0, 2))
    s5_im_s = jnp.transpose(hn_s[:, :, n_state:], (1, 0, 2))
    y_tail = tail_tile(jnp.transpose(y_s[:, :, :gw_s5], (1, 0, 2)).reshape(bs, d), F32)

    wr, br = router_weights(1)
    specs = [pl.BlockSpec((tm, d), lambda i: (i, 0))] + _stacked_specs(y_p, y_tail)
    x3, hn, rinfo, rinfo_t, counts = _mix_route(
        _glu_route_kernel, "glu_route", n, specs, [x2, y_p, y_tail], w_glu[0].astype(BF16),
        norm_ffn[1][None, :], wr, br, tril)
    y_main, y_last = _moe(x3, hn, rinfo, rinfo_t, counts, moe_w_gate, moe_w_up, moe_w_down, 1,
                          norm_final[None, :], last_layer=True)

    return (y_main.reshape(bp, tp, d), y_last[:bs].reshape(bs, 1, d),
            pool_p[None], c_p[None], n_p_st.reshape(1, bp, nh, dh), m_p[:, :, 0, 0][None],
            s5_re_p[None], s5_im_p[None],
            pool_s[None], c_s[None], n_s_st.reshape(1, bs, nh, dh), m_s.reshape(1, bs, nh),
            s5_re_s[None], s5_im_s[None])
```

```python
import functools

import jax
import jax.numpy as jnp
from jax import lax
from jax.experimental import pallas as pl
from jax.experimental.pallas import tpu as pltpu

F32 = jnp.float32
BF16 = jnp.bfloat16
I32 = jnp.int32

PAST_LEN = 16384
POOL_WINDOWS = (2, 4, 8, 16)
POOL_BUF = max(POOL_WINDOWS) - 1
MLSTM_CHUNK = 128
S5_SUB = 16
MOE_GROUPS = 4
MOE_EXPERTS_PER_GROUP = 8
RMS_EPS = 1e-6

LANES = 128
SUBLANES = 8
VMEM_LIMIT_BYTES = 56 * 1024 * 1024

TOKEN_TILE = 512
EXPERT_ROW_TILE = 512
MLSTM_SEQ_PER_STEP = 8
MLSTM_STEP_BATCH = 16
POOL_TIME_TILE = 1024
DMA_ISSUE_UNROLL = 8
S5_GROUPS_PER_BLOCK = 4
S5_BATCH_PER_STEP = 4

HIGHEST = lax.Precision.HIGHEST


def _params(*sem):
    return pltpu.CompilerParams(dimension_semantics=sem, vmem_limit_bytes=VMEM_LIMIT_BYTES)


def _rms(x, g):
    return x * lax.rsqrt(jnp.mean(x * x, axis=-1, keepdims=True) + RMS_EPS) * g


def _dot(a, b):
    return jnp.dot(a, b, preferred_element_type=F32)


def _dot_nt(a, b):
    return lax.dot_general(a, b, (((1,), (1,)), ((), ())), preferred_element_type=F32)


def _dot_tn(a, b):
    return lax.dot_general(a, b, (((0,), (0,)), ((), ())), preferred_element_type=F32)


def _stacked(main_ref, tail_ref):
    last = pl.program_id(0) == pl.num_programs(0) - 1
    return jnp.where(last, tail_ref[...], main_ref[...])


def _stacked_specs(main, tail):
    tm, w = tail.shape
    last_main = main.shape[0] // tm - 1
    return [pl.BlockSpec((tm, w), lambda i: (jnp.minimum(i, last_main), 0)),
            pl.BlockSpec((tm, w), lambda i: (0, 0))]


def _inproj_kernel(xm_ref, xt_ref, g_ref, w_ref, wg_ref, wgt_ref, bg_ref, bgt_ref,
                   z_ref, gates_ref, gatest_ref):
    h = _rms(_stacked(xm_ref, xt_ref), g_ref[...]).astype(BF16)
    z_ref[...] = _dot(h, w_ref[...])
    gates_ref[...] = _dot(h, wg_ref[...]) + bg_ref[...]
    gatest_ref[...] = _dot_nt(wgt_ref[...], h) + bgt_ref[...]


def _inproj(x_main, x_tail, g, w, wg, wgt, bg, bgt):
    d = x_main.shape[1]
    tm = TOKEN_TILE
    n = x_main.shape[0] + tm
    nz = w.shape[1]
    ng = wgt.shape[0]
    full = lambda i: (0, 0)
    return pl.pallas_call(
        _inproj_kernel,
        grid=(n // tm,),
        in_specs=_stacked_specs(x_main, x_tail) + [
            pl.BlockSpec((1, d), full),
            pl.BlockSpec((d, nz), full),
            pl.BlockSpec((d, LANES), full),
            pl.BlockSpec((ng, d), full),
            pl.BlockSpec((1, LANES), full),
            pl.BlockSpec((ng, 1), full),
        ],
        out_specs=[
            pl.BlockSpec((tm, nz), lambda i: (i, 0)),
            pl.BlockSpec((tm, LANES), lambda i: (i, 0)),
            pl.BlockSpec((ng, tm), lambda i: (0, i)),
        ],
        out_shape=[
            jax.ShapeDtypeStruct((n, nz), F32),
            jax.ShapeDtypeStruct((n, LANES), F32),
            jax.ShapeDtypeStruct((ng, n), F32),
        ],
        compiler_params=_params("parallel"),
        name="inproj",
    )(x_main, x_tail, g, w, wg, wgt, bg, bgt)


def _pool_prompt_kernel(u_ref, pw_ref, ps_ref, y_ref, st_ref, ext_ref, *, tt, gw):
    t = pl.program_id(1)
    nt = pl.num_programs(1)
    halo = POOL_BUF + 1
    width = ext_ref.shape[1]

    @pl.when(t == 0)
    def _():
        ext_ref[0:halo, :] = jnp.zeros((halo, width), F32)

    u = u_ref[...]
    ext_ref[halo:halo + tt, :] = u
    pos = t * tt + lax.broadcasted_iota(I32, (tt, 1), 0)
    for g, w in enumerate(POOL_WINDOWS):
        c0 = g * gw
        acc = u[:, c0:c0 + gw]
        for j in range(1, w):
            acc = acc + ext_ref[halo - j:halo - j + tt, c0:c0 + gw]
        cnt = jnp.minimum(w, pos + 1).astype(F32)
        d = acc / cnt - u[:, c0:c0 + gw]
        y = _dot(d.astype(BF16), pw_ref[g]) * ps_ref[:, c0:c0 + gw]
        y_ref[:, c0:c0 + gw] = y.astype(BF16)

    @pl.when(t == nt - 1)
    def _():
        st_ref[...] = ext_ref[tt + 1:tt + halo, :]

    ext_ref[0:halo, :] = ext_ref[tt:tt + halo, :]


def _pool_prompt(z, pw, ps, batch, seq):
    width = ps.shape[1]
    gw = width // len(POOL_WINDOWS)
    tt = POOL_TIME_TILE
    nt = seq // tt
    return pl.pallas_call(
        functools.partial(_pool_prompt_kernel, tt=tt, gw=gw),
        grid=(batch, nt),
        in_specs=[
            pl.BlockSpec((tt, width), lambda b, t: (b * nt + t, 0)),
            pl.BlockSpec(pw.shape, lambda b, t: (0, 0, 0)),
            pl.BlockSpec((1, width), lambda b, t: (0, 0)),
        ],
        out_specs=[
            pl.BlockSpec((tt, width), lambda b, t: (b * nt + t, 0)),
            pl.BlockSpec((None, POOL_BUF, width), lambda b, t: (b, 0, 0)),
        ],
        out_shape=[
            jax.ShapeDtypeStruct((batch * seq, width), BF16),
            jax.ShapeDtypeStruct((batch, POOL_BUF, width), F32),
        ],
        scratch_shapes=[pltpu.VMEM((POOL_BUF + 1 + tt, width), F32)],
        compiler_params=_params("parallel", "arbitrary"),
        name="pool_prompt",
    )(z, pw, ps)


def _pool_step_kernel(u_ref, buf_ref, pw_ref, ps_ref, y_ref, nb_ref, *, gw):
    u = u_ref[...]
    for g, w in enumerate(POOL_WINDOWS):
        c0 = g * gw
        acc = u[:, c0:c0 + gw]
        for j in range(1, w):
            acc = acc + buf_ref[POOL_BUF - j, :, c0:c0 + gw]
        cnt = float(min(w, PAST_LEN + 1))
        d = acc / cnt - u[:, c0:c0 + gw]
        y = _dot(d.astype(BF16), pw_ref[g]) * ps_ref[:, c0:c0 + gw]
        y_ref[:, c0:c0 + gw] = y.astype(BF16)
    nb_ref[0:POOL_BUF - 1] = buf_ref[1:POOL_BUF]
    nb_ref[POOL_BUF - 1] = u


def _pool_step(z, buf_t, pw, ps, row0):
    _, batch, width = buf_t.shape
    gw = width // len(POOL_WINDOWS)
    return pl.pallas_call(
        functools.partial(_pool_step_kernel, gw=gw),
        grid=(1,),
        in_specs=[
            pl.BlockSpec((batch, width), lambda i: (row0 // batch, 0)),
            pl.BlockSpec(buf_t.shape, lambda i: (0, 0, 0)),
            pl.BlockSpec(pw.shape, lambda i: (0, 0, 0)),
            pl.BlockSpec((1, width), lambda i: (0, 0)),
        ],
        out_specs=[
            pl.BlockSpec((batch, width), lambda i: (0, 0)),
            pl.BlockSpec(buf_t.shape, lambda i: (0, 0, 0)),
        ],
        out_shape=[
            jax.ShapeDtypeStruct((batch, width), BF16),
            jax.ShapeDtypeStruct(buf_t.shape, F32),
        ],
        compiler_params=_params("arbitrary"),
        name="pool_step",
    )(z, buf_t, pw, ps)


def _mlstm_prompt_kernel(*refs, nh, dh, nseq):
    seq_in = [refs[5 * s:5 * s + 5] for s in range(nseq)]
    h_ref, c_out, n_out, m_out, c_s, n_s, m_s = refs[5 * nseq:]
    ci = pl.program_id(1)
    nc = pl.num_programs(1)
    ln = seq_in[0][0].shape[0]

    @pl.when(ci == 0)
    def _():
        c_s[...] = jnp.zeros(c_s.shape, F32)
        n_s[...] = jnp.zeros(n_s.shape, F32)
        m_s[...] = jnp.zeros(m_s.shape, F32)

    row = lax.broadcasted_iota(I32, (ln, ln), 0)
    col = lax.broadcasted_iota(I32, (ln, ln), 1)
    causal_t = row <= col
    eye = col == row
    lane = lax.broadcasted_iota(I32, (nh, ln), 1)

    def to_col(r):
        return jnp.sum(jnp.where(eye, r, 0.0), axis=1, keepdims=True)

    scale = dh ** -0.5
    pairs = [(s, h) for s in range(nseq) for h in range(nh)]
    c_old = {p: c_s[p[0], p[1]] for p in pairs}
    n_old = {p: n_s[p[0], p[1]] for p in pairs}
    m_old = {p: m_s[p[0], p[1]][:, 0:1] for p in pairs}
    gates = []
    for s in range(nseq):
        gt = seq_in[s][4][...]
        bc_all = jax.nn.log_sigmoid(gt[nh:2 * nh])
        sh = 1
        while sh < ln:
            bc_all = bc_all + jnp.where(lane >= sh, pltpu.roll(bc_all, sh, 1), 0.0)
            sh *= 2
        gates.append((gt[0:nh], bc_all))
    ch = {}
    for s, h in pairs:
        q_ref, k_ref, v_ref, _, _ = seq_in[s]
        li_all, bc_all = gates[s]
        sl = slice(h * dh, (h + 1) * dh)
        k = k_ref[:, sl] * scale
        v = v_ref[:, sl]
        li_r, bc_r = li_all[h:h + 1], bc_all[h:h + 1]
        b_end = bc_r[:, ln - 1:ln]
        g_r = b_end - bc_r + li_r
        m0 = m_old[s, h]
        m_new = jnp.maximum(b_end + m0, jnp.max(g_r, axis=1, keepdims=True))
        ch[s, h] = dict(
            qb=q_ref[:, sl].astype(BF16), kb=k.astype(BF16), vb=v.astype(BF16), v=v,
            inter=bc_r + m0, m_new=m_new, wg_r=jnp.exp(g_r - m_new),
            decay=jnp.exp(b_end + m0 - m_new),
            dmat_t=jnp.where(causal_t, bc_r - to_col(bc_r - li_r), -jnp.inf))
    for p in pairs:
        d = ch[p]
        d["st"] = _dot_nt(d["kb"], d["qb"])
        d["cq_t"] = _dot_nt(c_old[p].astype(BF16), d["qb"])
        d["nq"] = _dot_nt(jnp.broadcast_to(n_old[p], (SUBLANES, dh)).astype(BF16), d["qb"])[0:1]
        d["c_add"] = _dot((d["v"].T * d["wg_r"]).astype(BF16), d["kb"])
        d["n_add"] = _dot(jnp.broadcast_to(d["wg_r"], (SUBLANES, ln)).astype(BF16), d["kb"])[0:1]
    for p in pairs:
        d = ch[p]
        d["m_row"] = jnp.maximum(d["inter"], jnp.max(d["dmat_t"], axis=0, keepdims=True))
        d["w_inter"] = jnp.exp(d["inter"] - d["m_row"])
        d["sc_t"] = d["st"] * jnp.exp(d["dmat_t"] - d["m_row"])
    for p in pairs:
        d = ch[p]
        d["pv_t"] = _dot_tn(d["vb"], d["sc_t"].astype(BF16))
    new_state = {}
    for s, h in pairs:
        d = ch[s, h]
        sl = slice(h * dh, (h + 1) * dh)
        num_t = d["pv_t"] + d["w_inter"] * d["cq_t"]
        den = jnp.sum(d["sc_t"], axis=0, keepdims=True) + d["w_inter"] * d["nq"]
        hh = (num_t / jnp.maximum(jnp.abs(den), jnp.exp(-d["m_row"]))).T
        h_ref[s, :, sl] = (hh * jax.nn.sigmoid(seq_in[s][3][:, sl])).astype(BF16)
        new_state[s, h] = (d["decay"] * c_old[s, h] + d["c_add"],
                           d["decay"] * n_old[s, h] + d["n_add"],
                           jnp.broadcast_to(d["m_new"], (1, dh)))

    for (s, h), (c_new, n_new, m_new) in new_state.items():
        c_s[s, h] = c_new
        n_s[s, h] = n_new
        m_s[s, h] = m_new

    @pl.when(ci == nc - 1)
    def _():
        c_out[...] = c_s[...]
        n_out[...] = n_s[...]
        m_out[...] = m_s[...]


def _mlstm_prompt(z, gates_t, batch, seq, nh, dh):
    ln = MLSTM_CHUNK
    nc = seq // ln
    nseq = MLSTM_SEQ_PER_STEP
    width = nh * dh
    ng = gates_t.shape[0]
    rows_of = lambda s: (lambda b, c: (b * nseq + s) * nc + c)
    in_specs, operands = [], []
    for s in range(nseq):
        r = rows_of(s)
        for j in (1, 2, 3, 4):
            in_specs.append(pl.BlockSpec((ln, width), lambda b, c, r=r, j=j: (r(b, c), j)))
        in_specs.append(pl.BlockSpec((ng, ln), lambda b, c, r=r: (0, r(b, c))))
        operands += [z, z, z, z, gates_t]
    h_spec = pl.BlockSpec((None, nseq, ln, width), lambda b, c: (b, 0, c, 0))
    st = lambda a, b_: pl.BlockSpec((nseq, nh, a, b_), lambda b, c: (b, 0, 0, 0))
    return pl.pallas_call(
        functools.partial(_mlstm_prompt_kernel, nh=nh, dh=dh, nseq=nseq),
        grid=(batch // nseq, nc),
        in_specs=in_specs,
        out_specs=[h_spec, st(dh, dh), st(1, dh), st(1, dh)],
        out_shape=[
            jax.ShapeDtypeStruct((batch // nseq, nseq, seq, width), BF16),
            jax.ShapeDtypeStruct((batch, nh, dh, dh), F32),
            jax.ShapeDtypeStruct((batch, nh, 1, dh), F32),
            jax.ShapeDtypeStruct((batch, nh, 1, dh), F32),
        ],
        scratch_shapes=[pltpu.VMEM((nseq, nh, dh, dh), F32), pltpu.VMEM((nseq, nh, 1, dh), F32),
                        pltpu.VMEM((nseq, nh, 1, dh), F32)],
        compiler_params=_params("parallel", "arbitrary"),
        name="mlstm_prompt",
    )(*operands)


def _mlstm_step_kernel(q_ref, k_ref, v_ref, o_ref, li_ref, fp_ref, c_ref, n_ref, m_ref,
                       h_ref, c_out, n_out, m_out, *, nh, dh):
    eye = (lax.broadcasted_iota(I32, (1, dh, dh), 1) == lax.broadcasted_iota(I32, (1, dh, dh), 2))
    scale = dh ** -0.5
    for h in range(nh):
        sl = slice(h * dh, (h + 1) * dh)
        q = q_ref[:, :, sl]
        k = k_ref[:, :, sl] * scale
        v = v_ref[:, :, sl]
        c = c_ref[:, h]
        n = n_ref[:, h]
        m = m_ref[:, h]
        li = li_ref[:, h]
        lf = jax.nn.log_sigmoid(fp_ref[:, h])
        inter = lf + m
        m_row = jnp.maximum(inter, li)
        w_intra = jnp.exp(li - m_row)
        w_inter = jnp.exp(inter - m_row)
        sc = jnp.sum(q * k, axis=-1, keepdims=True) * w_intra
        v_c = jnp.sum(jnp.where(eye, v, 0.0), axis=-1, keepdims=True)
        num = sc * v_c + w_inter * jnp.sum(c * q, axis=-1, keepdims=True)
        den = sc + w_inter * jnp.sum(n * q, axis=-1, keepdims=True)
        h_c = num / jnp.maximum(jnp.abs(den), jnp.exp(-m_row))
        h_l = jnp.sum(jnp.where(eye, h_c, 0.0), axis=1, keepdims=True)
        h_ref[:, :, sl] = h_l * jax.nn.sigmoid(o_ref[:, :, sl])
        wg = jnp.exp(li - m_row)
        decay = jnp.exp(inter - m_row)
        c_out[:, h] = decay * c + (v_c * wg) * k
        n_out[:, h] = decay * n + wg * k
        m_out[:, h] = m_row


def _mlstm_step(z3, li, fp, c, n, m, nh, dh):
    batch = c.shape[0]
    bb = MLSTM_STEP_BATCH
    width = nh * dh
    blk = lambda j: pl.BlockSpec((bb, 1, width), lambda i: (i, 0, j))
    st4 = lambda a, b: pl.BlockSpec((bb, nh, a, b), lambda i: (i, 0, 0, 0))
    return pl.pallas_call(
        functools.partial(_mlstm_step_kernel, nh=nh, dh=dh),
        grid=(batch // bb,),
        in_specs=[blk(1), blk(2), blk(3), blk(4), st4(1, 1), st4(1, 1),
                  st4(dh, dh), st4(1, dh), st4(1, 1)],
        out_specs=[pl.BlockSpec((bb, 1, width), lambda i: (i, 0, 0)),
                   st4(dh, dh), st4(1, dh), st4(1, 1)],
        out_shape=[
            jax.ShapeDtypeStruct((batch, 1, width), F32),
            jax.ShapeDtypeStruct((batch, nh, dh, dh), F32),
            jax.ShapeDtypeStruct((batch, nh, 1, dh), F32),
            jax.ShapeDtypeStruct((batch, nh, 1, 1), F32),
        ],
        compiler_params=_params("parallel"),
        name="mlstm_step",
    )(z3, z3, z3, z3, li, fp, c, n, m)


def _split_bf16(a):
    hi = a.astype(BF16)
    return hi, (a - hi.astype(F32)).astype(BF16)


def _route(hn, wr_ref, br_ref, tril_ref, carry_ref, rinfo_ref, rinfo_t_ref):
    ngrp, epg = MOE_GROUPS, MOE_EXPERTS_PER_GROUP
    h_hi, h_lo = _split_bf16(hn)
    both = _dot(h_hi, wr_ref[...])
    logits = both[:, 0:LANES] + (both[:, LANES:2 * LANES] + _dot(h_lo, wr_ref[:, 0:LANES])) + br_ref[...]
    tm = logits.shape[0]
    lane = lax.broadcasted_iota(I32, (tm, LANES), 1)
    neg = -jnp.inf

    def first_max(x):
        mx = jnp.max(x, axis=1, keepdims=True)
        idx = jnp.min(jnp.where(x == mx, lane, LANES), axis=1, keepdims=True)
        return mx, idx

    is_grp = lane < ngrp
    gmax, gsel = first_max(jnp.where(is_grp, logits, neg))
    g_w = 1.0 / jnp.sum(jnp.where(is_grp, jnp.exp(logits - gmax), 0.0), axis=1, keepdims=True)
    lo = ngrp + gsel * epg
    el = jnp.where((lane >= lo) & (lane < lo + epg), logits, neg)
    v1, i1 = first_max(el)
    v2, i2 = first_max(jnp.where(lane == i1, neg, el))
    e2 = jnp.exp(v2 - v1)
    w1 = g_w / (1.0 + e2)
    w2 = g_w * e2 / (1.0 + e2)
    eid1 = i1 - ngrp
    eid2 = i2 - ngrp

    hit1 = lane == eid1
    hit2 = lane == eid2
    onehot = jnp.where(hit1 | hit2, 1.0, 0.0)
    carry = carry_ref[...]
    prefix = _dot(tril_ref[...], onehot.astype(BF16)) + carry
    rank1 = jnp.sum(jnp.where(hit1, prefix, 0.0), axis=1, keepdims=True)
    rank2 = jnp.sum(jnp.where(hit2, prefix, 0.0), axis=1, keepdims=True)
    carry_ref[...] = carry + jnp.sum(onehot, axis=0, keepdims=True)

    cols = (eid1.astype(F32), eid2.astype(F32), w1, w2, rank1, rank2)
    info = jnp.zeros((tm, LANES), F32)
    for j, cval in enumerate(cols):
        info = jnp.where(lane == j, cval, info)
    rinfo_ref[...] = info
    pick = jnp.where(lax.broadcasted_iota(I32, (SUBLANES, LANES), 0)
                     == lax.broadcasted_iota(I32, (SUBLANES, LANES), 1), 1.0, 0.0).astype(BF16)
    parts = [_dot_nt(pick, p) for p in _split3(info)]
    rinfo_t_ref[...] = parts[0] + (parts[1] + parts[2])


def _outproj_route_kernel(xm_ref, xt_ref, pm_ref, pt_ref, mm_ref, mt_ref,
                          wo_ref, g_ref, wr_ref, br_ref, tril_ref,
                          x1_ref, hn_ref, rinfo_ref, rinfo_t_ref, cnt_ref, carry_ref):
    @pl.when(pl.program_id(0) == 0)
    def _():
        carry_ref[...] = jnp.zeros(carry_ref.shape, F32)

    half = pm_ref.shape[1]
    mix = (_dot(_stacked(pm_ref, pt_ref), wo_ref[0:half, :])
           + _dot(_stacked(mm_ref, mt_ref), wo_ref[half:2 * half, :]))
    x1 = _stacked(xm_ref, xt_ref) + mix
    x1_ref[...] = x1
    hn = _rms(x1, g_ref[...])
    hn_ref[...] = hn
    _route(hn, wr_ref, br_ref, tril_ref, carry_ref, rinfo_ref, rinfo_t_ref)
    cnt_ref[...] = carry_ref[...]


def _glu_route_kernel(x_ref, ym_ref, yt_ref, wglu_ref, g_ref, wr_ref, br_ref, tril_ref,
                      x1_ref, hn_ref, rinfo_ref, rinfo_t_ref, cnt_ref, carry_ref):
    @pl.when(pl.program_id(0) == 0)
    def _():
        carry_ref[...] = jnp.zeros(carry_ref.shape, F32)

    d = x_ref.shape[1]
    ag = _dot(jax.nn.gelu(_stacked(ym_ref, yt_ref)).astype(BF16), wglu_ref[...])
    x1 = x_ref[...] + ag[:, 0:d] * jax.nn.sigmoid(ag[:, d:2 * d])
    x1_ref[...] = x1
    hn = _rms(x1, g_ref[...])
    hn_ref[...] = hn
    _route(hn, wr_ref, br_ref, tril_ref, carry_ref, rinfo_ref, rinfo_t_ref)
    cnt_ref[...] = carry_ref[...]


def _mix_route(kernel, name, n, row_specs, rows, w, g, wr, br, tril):
    d = g.shape[1]
    tm = TOKEN_TILE
    full = lambda i: (0, 0)
    return pl.pallas_call(
        kernel,
        grid=(n // tm,),
        in_specs=row_specs + [
            pl.BlockSpec(w.shape, full),
            pl.BlockSpec((1, d), full),
            pl.BlockSpec((d, 2 * LANES), full),
            pl.BlockSpec((1, LANES), full),
            pl.BlockSpec((tm, tm), full),
        ],
        out_specs=[
            pl.BlockSpec((tm, d), lambda i: (i, 0)),
            pl.BlockSpec((tm, d), lambda i: (i, 0)),
            pl.BlockSpec((tm, LANES), lambda i: (i, 0)),
            pl.BlockSpec((SUBLANES, tm), lambda i: (0, i)),
            pl.BlockSpec((1, LANES), full),
        ],
        out_shape=[
            jax.ShapeDtypeStruct((n, d), F32),
            jax.ShapeDtypeStruct((n, d), F32),
            jax.ShapeDtypeStruct((n, LANES), F32),
            jax.ShapeDtypeStruct((SUBLANES, n), F32),
            jax.ShapeDtypeStruct((1, LANES), F32),
        ],
        scratch_shapes=[pltpu.VMEM((1, LANES), F32)],
        compiler_params=_params("arbitrary"),
        name=name,
    )(*rows, w, g, wr, br, tril)


def _index_copy(pos_hbm, idx_s, sem_i, tile, slot):
    return pltpu.make_async_copy(pos_hbm.at[tile], idx_s.at[slot], sem_i.at[slot])


def _dispatch_kernel(pos_hbm, hn_hbm, xs_hbm, idx_s, tiles, sem_i, sem_l, sem_d):
    i = pl.program_id(0)
    nt = pl.num_programs(0)
    nslot, tm, _ = tiles.shape
    islot = i % 2

    def load(t):
        s = t % nslot
        return pltpu.make_async_copy(hn_hbm.at[pl.ds(t * tm, tm)], tiles.at[s], sem_l.at[s])

    def wait_rows(t):
        s = t % nslot
        whole = pltpu.make_async_copy(tiles.at[s], xs_hbm.at[pl.ds(0, tm)], sem_d.at[s])
        whole.wait()
        whole.wait()

    @pl.when(i == 0)
    def _():
        _index_copy(pos_hbm, idx_s, sem_i, 0, 0).start()
        load(0).start()

        @pl.when(nt > 1)
        def _():
            load(1).start()

    @pl.when(i >= 2)
    def _():
        wait_rows(i - 2)

    @pl.when(i + 2 < nt)
    def _():
        load(i + 2).start()

    _index_copy(pos_hbm, idx_s, sem_i, i, islot).wait()

    @pl.when(i + 1 < nt)
    def _():
        _index_copy(pos_hbm, idx_s, sem_i, i + 1, 1 - islot).start()

    load(i).wait()
    slot = i % nslot

    def issue(r, carry):
        row = tiles.at[slot, pl.ds(r, 1)]
        pltpu.make_async_copy(row, xs_hbm.at[pl.ds(idx_s[islot, 0, r], 1)],
                              sem_d.at[slot]).start(priority=0)
        pltpu.make_async_copy(row, xs_hbm.at[pl.ds(idx_s[islot, 0, tm + r], 1)],
                              sem_d.at[slot]).start(priority=1)
        return carry

    lax.fori_loop(0, tm, issue, 0, unroll=DMA_ISSUE_UNROLL)

    @pl.when(i == nt - 1)
    def _():
        @pl.when(nt > 1)
        def _():
            wait_rows(i - 1)

        wait_rows(i)


def _dispatch(pos_tiles, hn):
    n, d = hn.shape
    tm = TOKEN_TILE
    nslot = 4
    return pl.pallas_call(
        _dispatch_kernel,
        grid=(n // tm,),
        in_specs=[pl.BlockSpec(memory_space=pl.ANY), pl.BlockSpec(memory_space=pl.ANY)],
        out_specs=pl.BlockSpec(memory_space=pl.ANY),
        out_shape=jax.ShapeDtypeStruct((2 * n, d), F32),
        scratch_shapes=[pltpu.SMEM((2, 1, 2 * tm), I32), pltpu.VMEM((nslot, tm, d), F32),
                        pltpu.SemaphoreType.DMA((2,)), pltpu.SemaphoreType.DMA((nslot,)),
                        pltpu.SemaphoreType.DMA((nslot,))],
        compiler_params=_params("arbitrary"),
        name="moe_dispatch",
    )(pos_tiles, hn)


def _moe_kernel(vt_ref, ve_ref, von_ref, vnext_ref, vslot_ref, lo_ref, hi_ref,
                xs_ref, wg_hbm, wu_hbm, wd_hbm, eo_ref,
                wg_f, wu_f, wd_f, wgb, wub, wdb, sem_w, *, layer):
    v = pl.program_id(0)
    tr = xs_ref.shape[0]
    prev = jnp.maximum(v - 1, 0)
    e = ve_ref[v]
    new_expert = jnp.logical_or(v == 0, e != ve_ref[prev])
    first_visit = jnp.logical_or(v == 0, vt_ref[v] != vt_ref[prev])

    def fetch(expert, slot):
        return [pltpu.make_async_copy(src.at[layer, expert], dst.at[slot], sem_w.at[slot, j])
                for j, (src, dst) in enumerate(((wg_hbm, wg_f), (wu_hbm, wu_f), (wd_hbm, wd_f)))]

    @pl.when(von_ref[v] == 1)
    def _():
        @pl.when(new_expert)
        def _():
            slot = vslot_ref[v]

            @pl.when(v == 0)
            def _():
                for cp in fetch(e, slot):
                    cp.start()

            for cp in fetch(e, slot):
                cp.wait()
            wgb[...] = wg_f[slot].astype(BF16)
            wub[...] = wu_f[slot].astype(BF16)
            wdb[...] = wd_f[slot].astype(BF16)

            @pl.when(vnext_ref[v] >= 0)
            def _():
                for cp in fetch(vnext_ref[v], 1 - slot):
                    cp.start()

        x = xs_ref[...].astype(BF16)
        act = jax.nn.silu(_dot(x, wgb[...])) * _dot(x, wub[...])
        row = vt_ref[v] * tr + lax.broadcasted_iota(I32, (tr, 1), 0)
        act = jnp.where((row >= lo_ref[e]) & (row < hi_ref[e]), act, 0.0)
        res = _dot(act.astype(BF16), wdb[...])

        @pl.when(first_visit)
        def _():
            eo_ref[...] = res

        @pl.when(jnp.logical_not(first_visit))
        def _():
            eo_ref[...] += res


def _moe_experts(vt, ve, von, vnext, vslot, lo, hi, xs, wg, wu, wd, layer):
    nv = vt.shape[0]
    rows, d = xs.shape
    hid = wg.shape[3]
    tr = EXPERT_ROW_TILE
    tile = pl.BlockSpec((tr, d), lambda v, vt, *_: (vt[v], 0))
    grid_spec = pltpu.PrefetchScalarGridSpec(
        num_scalar_prefetch=7,
        grid=(nv,),
        in_specs=[tile] + [pl.BlockSpec(memory_space=pl.ANY)] * 3,
        out_specs=tile,
        scratch_shapes=[
            pltpu.VMEM((2, d, hid), F32), pltpu.VMEM((2, d, hid), F32), pltpu.VMEM((2, hid, d), F32),
            pltpu.VMEM((d, hid), BF16), pltpu.VMEM((d, hid), BF16), pltpu.VMEM((hid, d), BF16),
            pltpu.SemaphoreType.DMA((2, 3)),
        ],
    )
    return pl.pallas_call(
        functools.partial(_moe_kernel, layer=layer),
        grid_spec=grid_spec,
        out_shape=jax.ShapeDtypeStruct((rows, d), F32),
        compiler_params=_params("arbitrary"),
        name="moe_experts",
    )(vt, ve, von, vnext, vslot, lo, hi, xs, wg, wu, wd)


def _combine_kernel(pos_hbm, eo_hbm, x_ref, rinfo_ref, g_ref, o1_ref, o2_ref, idx_s, a_buf, b_buf,
                    sem_i, sem_a, sem_b, *, last_layer):
    i = pl.program_id(0)
    nt = pl.num_programs(0)
    tm = a_buf.shape[1]
    slot = i % 2

    def gathers(s):
        def issue(r, carry):
            pltpu.make_async_copy(eo_hbm.at[pl.ds(idx_s[s, 0, r], 1)],
                                  a_buf.at[s, pl.ds(r, 1)], sem_a.at[s]).start(priority=0)
            pltpu.make_async_copy(eo_hbm.at[pl.ds(idx_s[s, 0, tm + r], 1)],
                                  b_buf.at[s, pl.ds(r, 1)], sem_b.at[s]).start(priority=1)
            return carry

        lax.fori_loop(0, tm, issue, 0, unroll=DMA_ISSUE_UNROLL)

    @pl.when(i == 0)
    def _():
        first = _index_copy(pos_hbm, idx_s, sem_i, 0, 0)
        first.start()
        first.wait()
        gathers(0)

        @pl.when(nt > 1)
        def _():
            _index_copy(pos_hbm, idx_s, sem_i, 1, 1).start()

    @pl.when(i + 1 < nt)
    def _():
        _index_copy(pos_hbm, idx_s, sem_i, i + 1, 1 - slot).wait()
        gathers(1 - slot)

    @pl.when(i + 2 < nt)
    def _():
        _index_copy(pos_hbm, idx_s, sem_i, i + 2, slot).start()

    pltpu.make_async_copy(eo_hbm.at[pl.ds(0, tm)], a_buf.at[slot], sem_a.at[slot]).wait()
    pltpu.make_async_copy(eo_hbm.at[pl.ds(0, tm)], b_buf.at[slot], sem_b.at[slot]).wait()
    info = rinfo_ref[...]
    x2 = x_ref[...] + (info[:, 2:3] * a_buf[slot] + info[:, 3:4] * b_buf[slot])
    hn = _rms(x2, g_ref[...])
    if last_layer:
        @pl.when(i < nt - 1)
        def _():
            o1_ref[...] = hn

        @pl.when(i == nt - 1)
        def _():
            o2_ref[...] = hn
    else:
        o1_ref[...] = x2
        o2_ref[...] = hn


def _combine(pos_tiles, eo, x, rinfo, g, last_layer):
    n, d = x.shape
    tm = TOKEN_TILE
    nt = n // tm
    row = pl.BlockSpec((tm, d), lambda i: (i, 0))
    if last_layer:
        out_specs = [pl.BlockSpec((tm, d), lambda i: (jnp.minimum(i, nt - 2), 0)),
                     pl.BlockSpec((tm, d), lambda i: (0, 0))]
        out_shape = [jax.ShapeDtypeStruct((n - tm, d), F32), jax.ShapeDtypeStruct((tm, d), F32)]
    else:
        out_specs = [row, row]
        out_shape = [jax.ShapeDtypeStruct((n, d), F32), jax.ShapeDtypeStruct((n, d), F32)]
    return pl.pallas_call(
        functools.partial(_combine_kernel, last_layer=last_layer),
        grid=(nt,),
        in_specs=[
            pl.BlockSpec(memory_space=pl.ANY),
            pl.BlockSpec(memory_space=pl.ANY),
            row,
            pl.BlockSpec((tm, LANES), lambda i: (i, 0)),
            pl.BlockSpec((1, d), lambda i: (0, 0)),
        ],
        out_specs=out_specs,
        out_shape=out_shape,
        scratch_shapes=[
            pltpu.SMEM((2, 1, 2 * tm), I32),
            pltpu.VMEM((2, tm, d), F32),
            pltpu.VMEM((2, tm, d), F32),
            pltpu.SemaphoreType.DMA((2,)),
            pltpu.SemaphoreType.DMA((2,)),
            pltpu.SemaphoreType.DMA((2,)),
        ],
        compiler_params=_params("arbitrary"),
        name="moe_combine",
    )(pos_tiles, eo, x, rinfo, g)


def _moe(x1, hn, rinfo, rinfo_t, counts, wg, wu, wd, layer, next_gain, last_layer):
    n = x1.shape[0]
    ne = wg.shape[1]
    tr = EXPERT_ROW_TILE
    tm = TOKEN_TILE
    nv = (2 * n) // tr + ne - 1
    eid = rinfo_t[0:2].astype(I32)
    rank = rinfo_t[4:6].astype(I32)
    cnt = counts[0, :ne].astype(I32)
    seg_end = jnp.cumsum(cnt)
    seg_start = seg_end - cnt
    experts = jnp.arange(ne, dtype=I32)
    pos = rank + jnp.sum(jnp.where(eid[:, None, :] == experts[None, :, None],
                                   seg_start[None, :, None], 0), axis=1)
    pos_tiles = pos.reshape(2, n // tm, tm).transpose(1, 0, 2).reshape(n // tm, 1, 2 * tm)

    first_tile = seg_start // tr
    tiles_e = jnp.where(cnt > 0, (seg_end - 1) // tr - first_tile + 1, 0)
    v_end = jnp.cumsum(tiles_e)
    v_start = v_end - tiles_e
    total = v_end[-1]
    vis = jnp.arange(nv, dtype=I32)
    vc = jnp.minimum(vis, jnp.maximum(total - 1, 0))
    ve = jnp.sum((vc[:, None] >= v_end[None, :]).astype(I32), axis=1)
    pick = lambda tab: jnp.sum(jnp.where(ve[:, None] == experts, tab, 0), axis=1)
    vt = pick(first_tile) + (vc - pick(v_start))
    von = (vis < total).astype(I32)
    present = cnt > 0
    later = present[None, :] & (experts[None, :] > experts[:, None])
    next_of = jnp.min(jnp.where(later, experts[None, :], ne), axis=1)
    vnext = pick(jnp.where(next_of < ne, next_of, -1))
    vslot = pick(jnp.cumsum(present.astype(I32)) - 1) & 1

    xs = _dispatch(pos_tiles, hn)
    eo = _moe_experts(vt, ve, von, vnext, vslot, seg_start, seg_end, xs, wg, wu, wd, layer)
    return _combine(pos_tiles, eo, x1, rinfo, next_gain, last_layer)


def _cis(log_mag, ang):
    mag = jnp.exp(log_mag)
    return mag * jnp.cos(ang), mag * jnp.sin(ang)


def _split3(a):
    p1 = a.astype(BF16)
    r1 = a - p1.astype(F32)
    p2 = r1.astype(BF16)
    return p1, p2, (r1 - p2.astype(F32)).astype(BF16)


def _select_dot(a, sel, sel_first=False):
    sel = sel.astype(BF16)
    parts = [(_dot(sel, p) if sel_first else _dot(p, sel)) for p in _split3(a)]
    return parts[0] + (parts[1] + parts[2])


def _dot3(a, b):
    a_hi, a_lo = _split_bf16(a)
    b_hi, b_lo = _split_bf16(b)
    return _dot(a_hi, b_hi) + (_dot(a_lo, b_hi) + _dot(a_hi, b_lo))


def _s5_prep_kernel(lsc_ref, lsr_ref, lrc_ref, lic_ref, lrr_ref, lir_ref, ctre_ref, ctim_ref,
                    btre_ref, btim_ref, d_ref,
                    t_ref, wre_ref, wim_ref, cyre_ref, cyim_ref, apr_ref, api_ref,
                    bbr_ref, bbi_ref, lbr_ref, lbi_ref, *, gw, gpb, sub):
    blk = gpb * gw
    wide = sub * blk
    nsl = lrc_ref.shape[1]
    ns = nsl // gpb
    sh_blk, sh_gw, sh_ns = blk.bit_length() - 1, gw.bit_length() - 1, ns.bit_length() - 1
    dt_c, dt_r = jnp.exp(lsc_ref[0]), jnp.exp(lsr_ref[0])
    ldt_c_re, ldt_c_im = lrc_ref[0] * dt_c, lic_ref[0] * dt_c
    ldt_r_re, ldt_r_im = lrr_ref[0] * dt_r, lir_ref[0] * dt_r

    lane_w = lax.broadcasted_iota(I32, (1, wide), 1)
    spread = jnp.where((lax.broadcasted_iota(I32, (gw, wide), 1) & (gw - 1))
                       == lax.broadcasted_iota(I32, (gw, wide), 0), 1.0, 0.0)
    cre = _select_dot(ctre_ref[0], spread)
    cim = _select_dot(ctim_ref[0], spread)
    same = (jnp.right_shift(lax.broadcasted_iota(I32, (nsl, 1), 0), sh_ns)
            == (jnp.right_shift(lane_w, sh_gw) & (gpb - 1)))
    assert 2 * blk == LANES
    tau = lax.broadcasted_iota(I32, (1, LANES), 1).astype(F32)
    pw_re, pw_im = _cis(tau * ldt_c_re, tau * ldt_c_im)
    low = lax.broadcasted_iota(I32, (nsl, LANES), 1) < blk

    def spread_pow(p, first):
        col = lambda t: jnp.broadcast_to(p[:, t:t + 1], (nsl, LANES))
        return jnp.concatenate([jnp.where(low, col(first + 2 * m), col(first + 2 * m + 1))
                                for m in range(sub // 2)], axis=1)

    def c_lam_pow(first):
        pr, pi = spread_pow(pw_re, first), spread_pow(pw_im, first)
        return (jnp.where(same, pr * cre - pi * cim, 0.0),
                jnp.where(same, -(pr * cim + pi * cre), 0.0))

    clr0, cli0 = c_lam_pow(0)
    clr1, cli1 = c_lam_pow(1)
    cyre_ref[0] = clr1.astype(BF16)
    cyim_ref[0] = cli1.astype(BF16)

    lbr, lbi = _cis(ldt_r_re, ldt_r_im)
    lbr_ref[0] = lbr
    lbi_ref[0] = lbi
    lr, li = lrr_ref[0], lir_ref[0]
    nr, ni = lbr - 1.0, lbi
    den = lr * lr + li * li
    fr = (nr * lr + ni * li) / den
    fi = (ni * lr - nr * li) / den
    bre, bim = btre_ref[0], btim_ref[0]
    bbr = fr * bre - fi * bim
    bbi = fr * bim + fi * bre
    bbr_ref[0] = bbr
    bbi_ref[0] = bbi

    lane_t = lax.broadcasted_iota(I32, (gw, wide), 1)
    chan = lax.broadcasted_iota(I32, (gw, wide), 0)
    for g in range(gpb):
        ps = slice(g * ns, (g + 1) * ns)
        r = _dot3(bbr[:, ps], clr0[ps, :]) + _dot3(bbi[:, ps], cli0[ps, :])
        r = r + jnp.where(lane_t == g * gw + chan, d_ref[0][:, g:g + 1], 0.0)
        for j in range(sub):
            tb = r if j == 0 else jnp.where(lane_t >= blk * j, pltpu.roll(r, blk * j, 1), 0.0)
            r0 = j * blk + g * gw
            t_ref[0, r0:r0 + gw, :] = tb.astype(BF16)

    rows = lax.broadcasted_iota(I32, (wide, 1), 0)
    spread_t = jnp.where((lax.broadcasted_iota(I32, (wide, gw), 0) & (gw - 1))
                         == lax.broadcasted_iota(I32, (wide, gw), 1), 1.0, 0.0)
    bbr_t = _select_dot(bbr, spread_t, sel_first=True)
    bbi_t = _select_dot(bbi, spread_t, sel_first=True)
    same_w = ((jnp.right_shift(rows, sh_gw) & (gpb - 1))
              == jnp.right_shift(lax.broadcasted_iota(I32, (1, nsl), 1), sh_ns))
    rj = ((sub - 1) - lax.broadcasted_iota(I32, (sub, 1), 0)).astype(F32)
    q_re, q_im = _cis(rj * ldt_r_re, rj * ldt_r_im)
    per_step = lambda q: jnp.concatenate(
        [jnp.broadcast_to(q[j:j + 1, :], (blk, nsl)) for j in range(sub)], axis=0)
    pr, pi = per_step(q_re), per_step(q_im)
    wre_ref[0] = jnp.where(same_w, pr * bbr_t - pi * bbi_t, 0.0).astype(BF16)
    wim_ref[0] = jnp.where(same_w, pr * bbi_t + pi * bbr_t, 0.0).astype(BF16)

    nlev = apr_ref.shape[1]
    pw = (sub * jnp.left_shift(1, lax.broadcasted_iota(I32, (nlev, 1), 0))).astype(F32)
    apr, api = _cis(pw * ldt_r_re, pw * ldt_r_im)
    apr_ref[0] = apr
    api_ref[0] = api


def _s5_prep(log_step, lam_re, lam_im, b_re, b_im, c_re, c_im, d_skip):
    ng, ns = lam_re.shape
    gw = b_re.shape[2]
    sub, gpb = S5_SUB, S5_GROUPS_PER_BLOCK
    nblk = ng // gpb
    nsl = gpb * ns
    wide = sub * gpb * gw
    ls = jnp.repeat(log_step, ns)
    ct = lambda c: jnp.transpose(c, (0, 2, 1)).reshape(nblk, nsl, gw)
    bt = lambda b: jnp.transpose(b.reshape(nblk, gpb, ns, gw), (0, 3, 1, 2)).reshape(nblk, gw, nsl)
    ins = [ls.reshape(nblk, nsl, 1), ls.reshape(nblk, 1, nsl),
           lam_re.reshape(nblk, nsl, 1), lam_im.reshape(nblk, nsl, 1),
           lam_re.reshape(nblk, 1, nsl), lam_im.reshape(nblk, 1, nsl),
           ct(c_re), ct(c_im), bt(b_re), bt(b_im),
           jnp.transpose(d_skip.reshape(nblk, gpb, gw), (0, 2, 1))]
    spec = lambda a: pl.BlockSpec((1,) + a.shape[1:], lambda g: (g, 0, 0))
    outs = [
        jax.ShapeDtypeStruct((nblk, wide, wide), BF16),
        jax.ShapeDtypeStruct((nblk, wide, nsl), BF16),
        jax.ShapeDtypeStruct((nblk, wide, nsl), BF16),
        jax.ShapeDtypeStruct((nblk, nsl, wide), BF16),
        jax.ShapeDtypeStruct((nblk, nsl, wide), BF16),
        jax.ShapeDtypeStruct((nblk, SUBLANES, nsl), F32),
        jax.ShapeDtypeStruct((nblk, SUBLANES, nsl), F32),
        jax.ShapeDtypeStruct((nblk, gw, nsl), F32),
        jax.ShapeDtypeStruct((nblk, gw, nsl), F32),
        jax.ShapeDtypeStruct((nblk, 1, nsl), F32),
        jax.ShapeDtypeStruct((nblk, 1, nsl), F32),
    ]
    return pl.pallas_call(
        functools.partial(_s5_prep_kernel, gw=gw, gpb=gpb, sub=sub),
        grid=(nblk,),
        in_specs=[spec(a) for a in ins],
        out_specs=[spec(o) for o in outs],
        out_shape=outs,
        compiler_params=_params("parallel"),
        name="s5_prep",
    )(*ins)


def _s5_prompt_kernel(x_ref, t_ref, wre_ref, wim_ref, cyre_ref, cyim_ref, apr_ref, api_ref,
                      y_ref, hre_out, him_out, u_s, y4_s, sre_s, sim_s, *, nb, seq, sub):
    nk = seq // sub
    rows = nb * nk
    hl = LANES // 2
    pad = nk // 2
    nlev = nk.bit_length() - 1
    nsl = sre_s.shape[1]
    low = lax.broadcasted_iota(I32, (nk, LANES), 1) < hl
    kidx = lax.broadcasted_iota(I32, (rows, 1), 0) & (nk - 1)
    sre_s[0:pad, :] = jnp.zeros((pad, nsl), F32)
    sim_s[0:pad, :] = jnp.zeros((pad, nsl), F32)

    def step_rows(b, j):
        return (pl.ds(b * seq + j, nk, stride=sub), slice(None))

    for b in range(nb):
        rs = slice(b * nk, (b + 1) * nk)
        for m in range(sub // 2):
            ls = slice(m * LANES, (m + 1) * LANES)
            s0 = x_ref[step_rows(b, 2 * m)]
            s1 = x_ref[step_rows(b, 2 * m + 1)]
            u_s[0, rs, ls] = jnp.where(low, s0, pltpu.roll(s1, hl, 1)).astype(BF16)
            u_s[1, rs, ls] = jnp.where(low, pltpu.roll(s0, hl, 1), s1).astype(BF16)

    live = slice(pad, pad + rows)
    for half in range(2):
        u = u_s[half]
        hre = _dot(u, wre_ref[half])
        him = _dot(u, wim_ref[half])
        for i in range(nlev):
            s = 1 << i
            sre_s[live, :] = hre
            sim_s[live, :] = him
            pre = sre_s[pad - s:pad - s + rows, :]
            pim = sim_s[pad - s:pad - s + rows, :]
            ar = apr_ref[half, i:i + 1, :]
            ai = api_ref[half, i:i + 1, :]
            ok = kidx >= s
            hre = hre + jnp.where(ok, ar * pre - ai * pim, 0.0)
            him = him + jnp.where(ok, ar * pim + ai * pre, 0.0)
        sre_s[live, :] = hre
        sim_s[live, :] = him
        ls = slice(half * nsl, (half + 1) * nsl)
        for b in range(nb):
            last = pad + (b + 1) * nk - 1
            hre_out[b:b + 1, ls] = sre_s[last:last + 1, :]
            him_out[b:b + 1, ls] = sim_s[last:last + 1, :]
        started = kidx >= 1
        hpre = jnp.where(started, sre_s[pad - 1:pad - 1 + rows, :], 0.0).astype(BF16)
        hpim = jnp.where(started, sim_s[pad - 1:pad - 1 + rows, :], 0.0).astype(BF16)
        y4_s[half] = _dot(u, t_ref[half]) + _dot(hpre, cyre_ref[half]) + _dot(hpim, cyim_ref[half])

    for b in range(nb):
        rs = slice(b * nk, (b + 1) * nk)
        for m in range(sub // 2):
            ls = slice(m * LANES, (m + 1) * LANES)
            ca = y4_s[0, rs, ls]
            cb = y4_s[1, rs, ls]
            y_ref[step_rows(b, 2 * m)] = jnp.where(low, ca, pltpu.roll(cb, hl, 1))
            y_ref[step_rows(b, 2 * m + 1)] = jnp.where(low, pltpu.roll(ca, hl, 1), cb)


def _s5_prompt(h, prep, batch, seq):
    d = h.shape[1]
    tmat, wre, wim, cyre, cyim, apr, api = prep[:7]
    nblk, wide, nsl = wre.shape
    sub, nb = S5_SUB, S5_BATCH_PER_STEP
    nk = seq // sub
    assert nk & (nk - 1) == 0 and nk.bit_length() - 1 <= apr.shape[1]
    ntile = d // LANES
    assert nblk == 2 * ntile
    rows = nb * nk
    wspec = lambda a: pl.BlockSpec((2,) + a.shape[1:], lambda t, b: (t, 0, 0))
    xspec = pl.BlockSpec((nb * seq, LANES), lambda t, b: (b, t))
    sspec = pl.BlockSpec((None, nb, 2 * nsl), lambda t, b: (b, 0, t))
    return pl.pallas_call(
        functools.partial(_s5_prompt_kernel, nb=nb, seq=seq, sub=sub),
        grid=(ntile, batch // nb),
        in_specs=[xspec, wspec(tmat), wspec(wre), wspec(wim), wspec(cyre), wspec(cyim),
                  wspec(apr), wspec(api)],
        out_specs=[xspec, sspec, sspec],
        out_shape=[
            jax.ShapeDtypeStruct((batch * seq, d), F32),
            jax.ShapeDtypeStruct((batch // nb, nb, ntile * 2 * nsl), F32),
            jax.ShapeDtypeStruct((batch // nb, nb, ntile * 2 * nsl), F32),
        ],
        scratch_shapes=[
            pltpu.VMEM((2, rows, wide), BF16),
            pltpu.VMEM((2, rows, wide), F32),
            pltpu.VMEM((nk // 2 + rows, nsl), F32),
            pltpu.VMEM((nk // 2 + rows, nsl), F32),
        ],
        compiler_params=_params("parallel", "parallel"),
        name="s5_prompt",
    )(h, tmat, wre, wim, cyre, cyim, apr, api)


def _s5_step_kernel(u_ref, h_ref, hsw_ref, bb_ref, la_ref, lb_ref, cc_ref, d_ref, hn_ref, y_ref):
    for g in range(u_ref.shape[0]):
        u = u_ref[g]
        hn = (la_ref[g] * h_ref[g] + lb_ref[g] * hsw_ref[g]
              + jnp.dot(u, bb_ref[g], precision=HIGHEST, preferred_element_type=F32))
        hn_ref[g] = hn
        y_ref[g] = (jnp.dot(hn, cc_ref[g], precision=HIGHEST, preferred_element_type=F32)
                    + d_ref[g] * u)


def _s5_step(u_p, h_cat, h_swp, bb_cat, la, lb, cc, d_p):
    ng = u_p.shape[0]
    gb = SUBLANES
    spec = lambda a: pl.BlockSpec((gb,) + a.shape[1:], lambda g: (g, 0, 0))
    ins = [u_p, h_cat, h_swp, bb_cat, la, lb, cc, d_p]
    outs = [jax.ShapeDtypeStruct(h_cat.shape, F32), jax.ShapeDtypeStruct(u_p.shape, F32)]
    return pl.pallas_call(
        _s5_step_kernel,
        grid=(ng // gb,),
        in_specs=[spec(a) for a in ins],
        out_specs=[spec(o) for o in outs],
        out_shape=outs,
        compiler_params=_params("parallel"),
        name="s5_step",
    )(*ins)


def _pad_to(a, axis, size):
    pad = [(0, 0)] * a.ndim
    pad[axis] = (0, size - a.shape[axis])
    return jnp.pad(a, pad)


def kernel(x_prompt, x_sample, state_pool, state_mlstm_c, state_mlstm_n, state_mlstm_m, state_s5_re, state_s5_im, norm_mix, norm_ffn, norm_final, w_in_ab, b_gates, pool_w, pool_scale, w_out_ab, s5_lam_re, s5_lam_im, s5_log_step, s5_b_re, s5_b_im, s5_c_re, s5_c_im, s5_d, w_glu, moe_w_group, moe_b_group, moe_w_expert, moe_b_expert, moe_w_gate, moe_w_up, moe_w_down):
    bp, tp, d = x_prompt.shape
    bs = x_sample.shape[0]
    n_p = bp * tp
    tm = TOKEN_TILE
    assert n_p % tm == 0 and bs <= tm
    n_fill = tm - bs
    nh, dh = state_mlstm_c.shape[2], state_mlstm_c.shape[3]
    pool_width = state_pool.shape[3]
    ml_width = nh * dh
    n_main = pool_width + 4 * ml_width
    n_gates = 2 * nh
    ngrp_s5, n_state = s5_lam_re.shape[1], s5_lam_re.shape[2]
    gw_s5 = d // ngrp_s5

    n = n_p + tm

    def tail_tile(sample_rows, dtype):
        return _pad_to(sample_rows.astype(dtype), 0, tm)

    x_main = x_prompt.reshape(n_p, d)
    x_tail = tail_tile(x_sample.reshape(bs, d), F32)
    tril = jnp.tril(jnp.ones((tm, tm), BF16), -1)

    def router_weights(l):
        wr = _pad_to(jnp.concatenate([moe_w_group[l], moe_w_expert[l]], axis=1), 1, LANES)
        br = jnp.concatenate([moe_b_group[l], moe_b_expert[l]])[None, :]
        hi = wr.astype(BF16)
        lo = (wr - hi.astype(F32)).astype(BF16)
        return jnp.concatenate([hi, lo], axis=1), _pad_to(br, 1, LANES)

    w_in = w_in_ab[0]
    w_g = w_in[:, n_main:]
    z, gates, gates_t = _inproj(
        x_main, x_tail, norm_mix[0][None, :], w_in[:, :n_main].astype(BF16),
        _pad_to(w_g, 1, LANES).astype(BF16), w_g.T.astype(BF16),
        _pad_to(b_gates[0][None, :], 1, LANES), b_gates[0][:, None])

    pw = pool_w[0].astype(BF16)
    ps = pool_scale[0][None, :]
    pool_y_p, pool_p = _pool_prompt(z, pw, ps, bp, tp)
    pool_y_s, pool_s_t = _pool_step(z, jnp.transpose(state_pool[0], (1, 0, 2)), pw, ps, n_p)
    pool_s = jnp.transpose(pool_s_t, (1, 0, 2))

    ml_y_p, c_p, n_p_st, m_p = _mlstm_prompt(z, gates_t, bp, tp, nh, dh)
    ml_y_p = ml_y_p.reshape(n_p, ml_width)
    g_s = gates[n_p:n_p + bs, :n_gates]
    ml_y_s, c_s, n_s_st, m_s = _mlstm_step(
        z[n_p:n_p + bs].reshape(bs, 1, n_main),
        g_s[:, :nh].reshape(bs, nh, 1, 1), g_s[:, nh:].reshape(bs, nh, 1, 1),
        state_mlstm_c[0], state_mlstm_n[0].reshape(bs, nh, 1, dh),
        state_mlstm_m[0].reshape(bs, nh, 1, 1), nh, dh)

    rows = [x_main, x_tail, pool_y_p, tail_tile(pool_y_s, BF16),
            ml_y_p, tail_tile(ml_y_s.reshape(bs, ml_width), BF16)]
    specs = (_stacked_specs(x_main, x_tail) + _stacked_specs(pool_y_p, rows[3])
             + _stacked_specs(ml_y_p, rows[5]))
    wr, br = router_weights(0)
    x1, hn, rinfo, rinfo_t, counts = _mix_route(
        _outproj_route_kernel, "outproj_route", n, specs, rows, w_out_ab[0].astype(BF16),
        norm_ffn[0][None, :], wr, br, tril)
    x2, h1 = _moe(x1, hn, rinfo, rinfo_t, counts, moe_w_gate, moe_w_up, moe_w_down, 0,
                  norm_mix[1][None, :], last_layer=False)

    prep = _s5_prep(s5_log_step[0], s5_lam_re[0], s5_lam_im[0], s5_b_re[0], s5_b_im[0],
                    s5_c_re[0], s5_c_im[0], s5_d[0])
    y_p, hre_p, him_p = _s5_prompt(h1, prep, bp, tp)
    s5_re_p = hre_p.reshape(bp, ngrp_s5, n_state)
    s5_im_p = him_p.reshape(bp, ngrp_s5, n_state)

    gpb = S5_GROUPS_PER_BLOCK
    per_group = lambda a: jnp.transpose(
        a.reshape(ngrp_s5 // gpb, gw_s5, gpb, n_state), (0, 2, 1, 3)).reshape(ngrp_s5, gw_s5, n_state)
    bbr, bbi = per_group(prep[7]), per_group(prep[8])
    lbr, lbi = prep[9].reshape(ngrp_s5, 1, n_state), prep[10].reshape(ngrp_s5, 1, n_state)
    d_g = s5_d[0].reshape(ngrp_s5, 1, gw_s5)
    u_s = h1[n_p:n_p + bs].reshape(bs, ngrp_s5, gw_s5).transpose(1, 0, 2)
    h_re = jnp.transpose(state_s5_re[0], (1, 0, 2))
    h_im = jnp.transpose(state_s5_im[0], (1, 0, 2))
    cc = jnp.concatenate([jnp.transpose(s5_c_re[0], (0, 2, 1)),
                          -jnp.transpose(s5_c_im[0], (0, 2, 1))], axis=1)
    hn_s, y_s = _s5_step(
        _pad_to(u_s, 2, LANES),
        jnp.concatenate([h_re, h_im], axis=2), jnp.concatenate([h_im, h_re], axis=2),
        _pad_to(jnp.concatenate([bbr, bbi], axis=2), 1, LANES),
        jnp.concatenate([lbr, lbr], axis=2), jnp.concatenate([-lbi, lbi], axis=2),
        _pad_to(cc, 2, LANES), _pad_to(d_g, 2, LANES))
    s5_re_s = jnp.transpose(hn_s[:, :, :n_state], (1, 0, 2))
    s5_im_s = jnp.transpose(hn_s[:, :, n_state:], (1, 0, 2))
    y_tail = tail_tile(jnp.transpose(y_s[:, :, :gw_s5], (1, 0, 2)).reshape(bs, d), F32)

    wr, br = router_weights(1)
    specs = [pl.BlockSpec((tm, d), lambda i: (i, 0))] + _stacked_specs(y_p, y_tail)
    x3, hn, rinfo, rinfo_t, counts = _mix_route(
        _glu_route_kernel, "glu_route", n, specs, [x2, y_p, y_tail], w_glu[0].astype(BF16),
        norm_ffn[1][None, :], wr, br, tril)
    y_main, y_last = _moe(x3, hn, rinfo, rinfo_t, counts, moe_w_gate, moe_w_up, moe_w_down, 1,
                          norm_final[None, :], last_layer=True)

    return (y_main.reshape(bp, tp, d), y_last[:bs].reshape(bs, 1, d),
            pool_p[None], c_p[None], n_p_st.reshape(1, bp, nh, dh), m_p[:, :, 0, 0][None],
            s5_re_p[None], s5_im_p[None],
            pool_s[None], c_s[None], n_s_st.reshape(1, bs, nh, dh), m_s.reshape(1, bs, nh),
            s5_re_s[None], s5_im_s[None])
```

```python
import functools

import jax
import jax.numpy as jnp
from jax import lax
from jax.experimental import pallas as pl
from jax.experimental.pallas import tpu as pltpu

F32 = jnp.float32
BF16 = jnp.bfloat16
I32 = jnp.int32

PAST_LEN = 16384
POOL_WINDOWS = (2, 4, 8, 16)
POOL_BUF = max(POOL_WINDOWS) - 1
MLSTM_CHUNK = 128
S5_SUB = 16
MOE_GROUPS = 4
MOE_EXPERTS_PER_GROUP = 8
RMS_EPS = 1e-6

LANES = 128
SUBLANES = 8
VMEM_LIMIT_BYTES = 56 * 1024 * 1024

TOKEN_TILE = 512
EXPERT_ROW_TILE = 512
MLSTM_SEQ_PER_STEP = 8
MLSTM_STEP_BATCH = 16
POOL_TIME_TILE = 1024
DMA_ISSUE_UNROLL = 8
S5_GROUPS_PER_BLOCK = 4
S5_BATCH_PER_STEP = 4

HIGHEST = lax.Precision.HIGHEST


def _params(*sem):
    return pltpu.CompilerParams(dimension_semantics=sem, vmem_limit_bytes=VMEM_LIMIT_BYTES)


def _rms(x, g):
    return x * lax.rsqrt(jnp.mean(x * x, axis=-1, keepdims=True) + RMS_EPS) * g


def _dot(a, b):
    return jnp.dot(a, b, preferred_element_type=F32)


def _dot_nt(a, b):
    return lax.dot_general(a, b, (((1,), (1,)), ((), ())), preferred_element_type=F32)


def _dot_tn(a, b):
    return lax.dot_general(a, b, (((0,), (0,)), ((), ())), preferred_element_type=F32)


def _stacked(main_ref, tail_ref):
    last = pl.program_id(0) == pl.num_programs(0) - 1
    return jnp.where(last, tail_ref[...], main_ref[...])


def _stacked_specs(main, tail):
    tm, w = tail.shape
    last_main = main.shape[0] // tm - 1
    return [pl.BlockSpec((tm, w), lambda i: (jnp.minimum(i, last_main), 0)),
            pl.BlockSpec((tm, w), lambda i: (0, 0))]


def _inproj_kernel(xm_ref, xt_ref, g_ref, w_ref, wg_ref, wgt_ref, bg_ref, bgt_ref,
                   z_ref, gates_ref, gatest_ref):
    h = _rms(_stacked(xm_ref, xt_ref), g_ref[...]).astype(BF16)
    z_ref[...] = _dot(h, w_ref[...])
    gates_ref[...] = _dot(h, wg_ref[...]) + bg_ref[...]
    gatest_ref[...] = _dot_nt(wgt_ref[...], h) + bgt_ref[...]


def _inproj(x_main, x_tail, g, w, wg, wgt, bg, bgt):
    d = x_main.shape[1]
    tm = TOKEN_TILE
    n = x_main.shape[0] + tm
    nz = w.shape[1]
    ng = wgt.shape[0]
    full = lambda i: (0, 0)
    return pl.pallas_call(
        _inproj_kernel,
        grid=(n // tm,),
        in_specs=_stacked_specs(x_main, x_tail) + [
            pl.BlockSpec((1, d), full),
            pl.BlockSpec((d, nz), full),
            pl.BlockSpec((d, LANES), full),
            pl.BlockSpec((ng, d), full),
            pl.BlockSpec((1, LANES), full),
            pl.BlockSpec((ng, 1), full),
        ],
        out_specs=[
            pl.BlockSpec((tm, nz), lambda i: (i, 0)),
            pl.BlockSpec((tm, LANES), lambda i: (i, 0)),
            pl.BlockSpec((ng, tm), lambda i: (0, i)),
        ],
        out_shape=[
            jax.ShapeDtypeStruct((n, nz), F32),
            jax.ShapeDtypeStruct((n, LANES), F32),
            jax.ShapeDtypeStruct((ng, n), F32),
        ],
        compiler_params=_params("parallel"),
        name="inproj",
    )(x_main, x_tail, g, w, wg, wgt, bg, bgt)


def _pool_prompt_kernel(u_ref, pw_ref, ps_ref, y_ref, st_ref, ext_ref, *, tt, gw):
    t = pl.program_id(1)
    nt = pl.num_programs(1)
    halo = POOL_BUF + 1
    width = ext_ref.shape[1]

    @pl.when(t == 0)
    def _():
        ext_ref[0:halo, :] = jnp.zeros((halo, width), F32)

    u = u_ref[...]
    ext_ref[halo:halo + tt, :] = u
    pos = t * tt + lax.broadcasted_iota(I32, (tt, 1), 0)
    for g, w in enumerate(POOL_WINDOWS):
        c0 = g * gw
        acc = u[:, c0:c0 + gw]
        for j in range(1, w):
            acc = acc + ext_ref[halo - j:halo - j + tt, c0:c0 + gw]
        cnt = jnp.minimum(w, pos + 1).astype(F32)
        d = acc / cnt - u[:, c0:c0 + gw]
        y = _dot(d.astype(BF16), pw_ref[g]) * ps_ref[:, c0:c0 + gw]
        y_ref[:, c0:c0 + gw] = y.astype(BF16)

    @pl.when(t == nt - 1)
    def _():
        st_ref[...] = ext_ref[tt + 1:tt + halo, :]

    ext_ref[0:halo, :] = ext_ref[tt:tt + halo, :]


def _pool_prompt(z, pw, ps, batch, seq):
    width = ps.shape[1]
    gw = width // len(POOL_WINDOWS)
    tt = POOL_TIME_TILE
    nt = seq // tt
    return pl.pallas_call(
        functools.partial(_pool_prompt_kernel, tt=tt, gw=gw),
        grid=(batch, nt),
        in_specs=[
            pl.BlockSpec((tt, width), lambda b, t: (b * nt + t, 0)),
            pl.BlockSpec(pw.shape, lambda b, t: (0, 0, 0)),
            pl.BlockSpec((1, width), lambda b, t: (0, 0)),
        ],
        out_specs=[
            pl.BlockSpec((tt, width), lambda b, t: (b * nt + t, 0)),
            pl.BlockSpec((None, POOL_BUF, width), lambda b, t: (b, 0, 0)),
        ],
        out_shape=[
            jax.ShapeDtypeStruct((batch * seq, width), BF16),
            jax.ShapeDtypeStruct((batch, POOL_BUF, width), F32),
        ],
        scratch_shapes=[pltpu.VMEM((POOL_BUF + 1 + tt, width), F32)],
        compiler_params=_params("parallel", "arbitrary"),
        name="pool_prompt",
    )(z, pw, ps)


def _pool_step_kernel(u_ref, buf_ref, pw_ref, ps_ref, y_ref, nb_ref, *, gw):
    u = u_ref[...]
    for g, w in enumerate(POOL_WINDOWS):
        c0 = g * gw
        acc = u[:, c0:c0 + gw]
        for j in range(1, w):
            acc = acc + buf_ref[POOL_BUF - j, :, c0:c0 + gw]
        cnt = float(min(w, PAST_LEN + 1))
        d = acc / cnt - u[:, c0:c0 + gw]
        y = _dot(d.astype(BF16), pw_ref[g]) * ps_ref[:, c0:c0 + gw]
        y_ref[:, c0:c0 + gw] = y.astype(BF16)
    nb_ref[0:POOL_BUF - 1] = buf_ref[1:POOL_BUF]
    nb_ref[POOL_BUF - 1] = u


def _pool_step(z, buf_t, pw, ps, row0):
    _, batch, width = buf_t.shape
    gw = width // len(POOL_WINDOWS)
    return pl.pallas_call(
        functools.partial(_pool_step_kernel, gw=gw),
        grid=(1,),
        in_specs=[
            pl.BlockSpec((batch, width), lambda i: (row0 // batch, 0)),
            pl.BlockSpec(buf_t.shape, lambda i: (0, 0, 0)),
            pl.BlockSpec(pw.shape, lambda i: (0, 0, 0)),
            pl.BlockSpec((1, width), lambda i: (0, 0)),
        ],
        out_specs=[
            pl.BlockSpec((batch, width), lambda i: (0, 0)),
            pl.BlockSpec(buf_t.shape, lambda i: (0, 0, 0)),
        ],
        out_shape=[
            jax.ShapeDtypeStruct((batch, width), BF16),
            jax.ShapeDtypeStruct(buf_t.shape, F32),
        ],
        compiler_params=_params("arbitrary"),
        name="pool_step",
    )(z, buf_t, pw, ps)


def _mlstm_prompt_kernel(*refs, nh, dh, nseq):
    seq_in = [refs[5 * s:5 * s + 5] for s in range(nseq)]
    h_ref, c_out, n_out, m_out, c_s, n_s, m_s = refs[5 * nseq:]
    ci = pl.program_id(1)
    nc = pl.num_programs(1)
    ln = seq_in[0][0].shape[0]

    @pl.when(ci == 0)
    def _():
        c_s[...] = jnp.zeros(c_s.shape, F32)
        n_s[...] = jnp.zeros(n_s.shape, F32)
        m_s[...] = jnp.zeros(m_s.shape, F32)

    row = lax.broadcasted_iota(I32, (ln, ln), 0)
    col = lax.broadcasted_iota(I32, (ln, ln), 1)
    causal_t = row <= col
    eye = col == row
    lane = lax.broadcasted_iota(I32, (nh, ln), 1)

    def to_col(r):
        return jnp.sum(jnp.where(eye, r, 0.0), axis=1, keepdims=True)

    scale = dh ** -0.5
    pairs = [(s, h) for s in range(nseq) for h in range(nh)]
    c_old = {p: c_s[p[0], p[1]] for p in pairs}
    n_old = {p: n_s[p[0], p[1]] for p in pairs}
    m_old = {p: m_s[p[0], p[1]][:, 0:1] for p in pairs}
    gates = []
    for s in range(nseq):
        gt = seq_in[s][4][...]
        bc_all = jax.nn.log_sigmoid(gt[nh:2 * nh])
        sh = 1
        while sh < ln:
            bc_all = bc_all + jnp.where(lane >= sh, pltpu.roll(bc_all, sh, 1), 0.0)
            sh *= 2
        gates.append((gt[0:nh], bc_all))
    ch = {}
    for s, h in pairs:
        q_ref, k_ref, v_ref, _, _ = seq_in[s]
        li_all, bc_all = gates[s]
        sl = slice(h * dh, (h + 1) * dh)
        k = k_ref[:, sl] * scale
        v = v_ref[:, sl]
        li_r, bc_r = li_all[h:h + 1], bc_all[h:h + 1]
        b_end = bc_r[:, ln - 1:ln]
        g_r = b_end - bc_r + li_r
        m0 = m_old[s, h]
        m_new = jnp.maximum(b_end + m0, jnp.max(g_r, axis=1, keepdims=True))
        ch[s, h] = dict(
            qb=q_ref[:, sl].astype(BF16), kb=k.astype(BF16), vb=v.astype(BF16), v=v,
            inter=bc_r + m0, m_new=m_new, wg_r=jnp.exp(g_r - m_new),
            decay=jnp.exp(b_end + m0 - m_new),
            dmat_t=jnp.where(causal_t, bc_r - to_col(bc_r - li_r), -jnp.inf))
    for p in pairs:
        d = ch[p]
        d["st"] = _dot_nt(d["kb"], d["qb"])
        d["cq_t"] = _dot_nt(c_old[p].astype(BF16), d["qb"])
        d["nq"] = _dot_nt(jnp.broadcast_to(n_old[p], (SUBLANES, dh)).astype(BF16), d["qb"])[0:1]
        d["c_add"] = _dot((d["v"].T * d["wg_r"]).astype(BF16), d["kb"])
        d["n_add"] = _dot(jnp.broadcast_to(d["wg_r"], (SUBLANES, ln)).astype(BF16), d["kb"])[0:1]
    for p in pairs:
        d = ch[p]
        d["m_row"] = jnp.maximum(d["inter"], jnp.max(d["dmat_t"], axis=0, keepdims=True))
        d["w_inter"] = jnp.exp(d["inter"] - d["m_row"])
        d["sc_t"] = d["st"] * jnp.exp(d["dmat_t"] - d["m_row"])
    for p in pairs:
        d = ch[p]
        d["pv_t"] = _dot_tn(d["vb"], d["sc_t"].astype(BF16))
    new_state = {}
    for s, h in pairs:
        d = ch[s, h]
        sl = slice(h * dh, (h + 1) * dh)
        num_t = d["pv_t"] + d["w_inter"] * d["cq_t"]
        den = jnp.sum(d["sc_t"], axis=0, keepdims=True) + d["w_inter"] * d["nq"]
        hh = (num_t / jnp.maximum(jnp.abs(den), jnp.exp(-d["m_row"]))).T
        h_ref[s, :, sl] = (hh * jax.nn.sigmoid(seq_in[s][3][:, sl])).astype(BF16)
        new_state[s, h] = (d["decay"] * c_old[s, h] + d["c_add"],
                           d["decay"] * n_old[s, h] + d["n_add"],
                           jnp.broadcast_to(d["m_new"], (1, dh)))

    for (s, h), (c_new, n_new, m_new) in new_state.items():
        c_s[s, h] = c_new
        n_s[s, h] = n_new
        m_s[s, h] = m_new

    @pl.when(ci == nc - 1)
    def _():
        c_out[...] = c_s[...]
        n_out[...] = n_s[...]
        m_out[...] = m_s[...]


def _mlstm_prompt(z, gates_t, batch, seq, nh, dh):
    ln = MLSTM_CHUNK
    nc = seq // ln
    nseq = MLSTM_SEQ_PER_STEP
    width = nh * dh
    ng = gates_t.shape[0]
    rows_of = lambda s: (lambda b, c: (b * nseq + s) * nc + c)
    in_specs, operands = [], []
    for s in range(nseq):
        r = rows_of(s)
        for j in (1, 2, 3, 4):
            in_specs.append(pl.BlockSpec((ln, width), lambda b, c, r=r, j=j: (r(b, c), j)))
        in_specs.append(pl.BlockSpec((ng, ln), lambda b, c, r=r: (0, r(b, c))))
        operands += [z, z, z, z, gates_t]
    h_spec = pl.BlockSpec((None, nseq, ln, width), lambda b, c: (b, 0, c, 0))
    st = lambda a, b_: pl.BlockSpec((nseq, nh, a, b_), lambda b, c: (b, 0, 0, 0))
    return pl.pallas_call(
        functools.partial(_mlstm_prompt_kernel, nh=nh, dh=dh, nseq=nseq),
        grid=(batch // nseq, nc),
        in_specs=in_specs,
        out_specs=[h_spec, st(dh, dh), st(1, dh), st(1, dh)],
        out_shape=[
            jax.ShapeDtypeStruct((batch // nseq, nseq, seq, width), BF16),
            jax.ShapeDtypeStruct((batch, nh, dh, dh), F32),
            jax.ShapeDtypeStruct((batch, nh, 1, dh), F32),
            jax.ShapeDtypeStruct((batch, nh, 1, dh), F32),
        ],
        scratch_shapes=[pltpu.VMEM((nseq, nh, dh, dh), F32), pltpu.VMEM((nseq, nh, 1, dh), F32),
                        pltpu.VMEM((nseq, nh, 1, dh), F32)],
        compiler_params=_params("parallel", "arbitrary"),
        name="mlstm_prompt",
    )(*operands)


def _mlstm_step_kernel(q_ref, k_ref, v_ref, o_ref, li_ref, fp_ref, c_ref, n_ref, m_ref,
                       h_ref, c_out, n_out, m_out, *, nh, dh):
    eye = (lax.broadcasted_iota(I32, (1, dh, dh), 1) == lax.broadcasted_iota(I32, (1, dh, dh), 2))
    scale = dh ** -0.5
    for h in range(nh):
        sl = slice(h * dh, (h + 1) * dh)
        q = q_ref[:, :, sl]
        k = k_ref[:, :, sl] * scale
        v = v_ref[:, :, sl]
        c = c_ref[:, h]
        n = n_ref[:, h]
        m = m_ref[:, h]
        li = li_ref[:, h]
        lf = jax.nn.log_sigmoid(fp_ref[:, h])
        inter = lf + m
        m_row = jnp.maximum(inter, li)
        w_intra = jnp.exp(li - m_row)
        w_inter = jnp.exp(inter - m_row)
        sc = jnp.sum(q * k, axis=-1, keepdims=True) * w_intra
        v_c = jnp.sum(jnp.where(eye, v, 0.0), axis=-1, keepdims=True)
        num = sc * v_c + w_inter * jnp.sum(c * q, axis=-1, keepdims=True)
        den = sc + w_inter * jnp.sum(n * q, axis=-1, keepdims=True)
        h_c = num / jnp.maximum(jnp.abs(den), jnp.exp(-m_row))
        h_l = jnp.sum(jnp.where(eye, h_c, 0.0), axis=1, keepdims=True)
        h_ref[:, :, sl] = h_l * jax.nn.sigmoid(o_ref[:, :, sl])
        wg = jnp.exp(li - m_row)
        decay = jnp.exp(inter - m_row)
        c_out[:, h] = decay * c + (v_c * wg) * k
        n_out[:, h] = decay * n + wg * k
        m_out[:, h] = m_row


def _mlstm_step(z3, li, fp, c, n, m, nh, dh):
    batch = c.shape[0]
    bb = MLSTM_STEP_BATCH
    width = nh * dh
    blk = lambda j: pl.BlockSpec((bb, 1, width), lambda i: (i, 0, j))
    st4 = lambda a, b: pl.BlockSpec((bb, nh, a, b), lambda i: (i, 0, 0, 0))
    return pl.pallas_call(
        functools.partial(_mlstm_step_kernel, nh=nh, dh=dh),
        grid=(batch // bb,),
        in_specs=[blk(1), blk(2), blk(3), blk(4), st4(1, 1), st4(1, 1),
                  st4(dh, dh), st4(1, dh), st4(1, 1)],
        out_specs=[pl.BlockSpec((bb, 1, width), lambda i: (i, 0, 0)),
                   st4(dh, dh), st4(1, dh), st4(1, 1)],
        out_shape=[
            jax.ShapeDtypeStruct((batch, 1, width), F32),
            jax.ShapeDtypeStruct((batch, nh, dh, dh), F32),
            jax.ShapeDtypeStruct((batch, nh, 1, dh), F32),
            jax.ShapeDtypeStruct((batch, nh, 1, 1), F32),
        ],
        compiler_params=_params("parallel"),
        name="mlstm_step",
    )(z3, z3, z3, z3, li, fp, c, n, m)


def _split_bf16(a):
    hi = a.astype(BF16)
    return hi, (a - hi.astype(F32)).astype(BF16)


def _route(hn, wr_ref, br_ref, tril_ref, carry_ref, rinfo_ref, rinfo_t_ref):
    ngrp, epg = MOE_GROUPS, MOE_EXPERTS_PER_GROUP
    h_hi, h_lo = _split_bf16(hn)
    both = _dot(h_hi, wr_ref[...])
    logits = both[:, 0:LANES] + (both[:, LANES:2 * LANES] + _dot(h_lo, wr_ref[:, 0:LANES])) + br_ref[...]
    tm = logits.shape[0]
    lane = lax.broadcasted_iota(I32, (tm, LANES), 1)
    neg = -jnp.inf

    def first_max(x):
        mx = jnp.max(x, axis=1, keepdims=True)
        idx = jnp.min(jnp.where(x == mx, lane, LANES), axis=1, keepdims=True)
        return mx, idx

    is_grp = lane < ngrp
    gmax, gsel = first_max(jnp.where(is_grp, logits, neg))
    g_w = 1.0 / jnp.sum(jnp.where(is_grp, jnp.exp(logits - gmax), 0.0), axis=1, keepdims=True)
    lo = ngrp + gsel * epg
    el = jnp.where((lane >= lo) & (lane < lo + epg), logits, neg)
    v1, i1 = first_max(el)
    v2, i2 = first_max(jnp.where(lane == i1, neg, el))
    e2 = jnp.exp(v2 - v1)
    w1 = g_w / (1.0 + e2)
    w2 = g_w * e2 / (1.0 + e2)
    eid1 = i1 - ngrp
    eid2 = i2 - ngrp

    hit1 = lane == eid1
    hit2 = lane == eid2
    onehot = jnp.where(hit1 | hit2, 1.0, 0.0)
    carry = carry_ref[...]
    prefix = _dot(tril_ref[...], onehot.astype(BF16)) + carry
    rank1 = jnp.sum(jnp.where(hit1, prefix, 0.0), axis=1, keepdims=True)
    rank2 = jnp.sum(jnp.where(hit2, prefix, 0.0), axis=1, keepdims=True)
    carry_ref[...] = carry + jnp.sum(onehot, axis=0, keepdims=True)

    cols = (eid1.astype(F32), eid2.astype(F32), w1, w2, rank1, rank2)
    info = jnp.zeros((tm, LANES), F32)
    for j, cval in enumerate(cols):
        info = jnp.where(lane == j, cval, info)
    rinfo_ref[...] = info
    pick = jnp.where(lax.broadcasted_iota(I32, (SUBLANES, LANES), 0)
                     == lax.broadcasted_iota(I32, (SUBLANES, LANES), 1), 1.0, 0.0).astype(BF16)
    parts = [_dot_nt(pick, p) for p in _split3(info)]
    rinfo_t_ref[...] = parts[0] + (parts[1] + parts[2])


def _outproj_route_kernel(xm_ref, xt_ref, pm_ref, pt_ref, mm_ref, mt_ref,
                          wo_ref, g_ref, wr_ref, br_ref, tril_ref,
                          x1_ref, hn_ref, rinfo_ref, rinfo_t_ref, cnt_ref, carry_ref):
    @pl.when(pl.program_id(0) == 0)
    def _():
        carry_ref[...] = jnp.zeros(carry_ref.shape, F32)

    half = pm_ref.shape[1]
    mix = (_dot(_stacked(pm_ref, pt_ref), wo_ref[0:half, :])
           + _dot(_stacked(mm_ref, mt_ref), wo_ref[half:2 * half, :]))
    x1 = _stacked(xm_ref, xt_ref) + mix
    x1_ref[...] = x1
    hn = _rms(x1, g_ref[...])
    hn_ref[...] = hn
    _route(hn, wr_ref, br_ref, tril_ref, carry_ref, rinfo_ref, rinfo_t_ref)
    cnt_ref[...] = carry_ref[...]


def _glu_route_kernel(x_ref, ym_ref, yt_ref, wglu_ref, g_ref, wr_ref, br_ref, tril_ref,
                      x1_ref, hn_ref, rinfo_ref, rinfo_t_ref, cnt_ref, carry_ref):
    @pl.when(pl.program_id(0) == 0)
    def _():
        carry_ref[...] = jnp.zeros(carry_ref.shape, F32)

    d = x_ref.shape[1]
    ag = _dot(jax.nn.gelu(_stacked(ym_ref, yt_ref)).astype(BF16), wglu_ref[...])
    x1 = x_ref[...] + ag[:, 0:d] * jax.nn.sigmoid(ag[:, d:2 * d])
    x1_ref[...] = x1
    hn = _rms(x1, g_ref[...])
    hn_ref[...] = hn
    _route(hn, wr_ref, br_ref, tril_ref, carry_ref, rinfo_ref, rinfo_t_ref)
    cnt_ref[...] = carry_ref[...]


def _mix_route(kernel, name, n, row_specs, rows, w, g, wr, br, tril):
    d = g.shape[1]
    tm = TOKEN_TILE
    full = lambda i: (0, 0)
    return pl.pallas_call(
        kernel,
        grid=(n // tm,),
        in_specs=row_specs + [
            pl.BlockSpec(w.shape, full),
            pl.BlockSpec((1, d), full),
            pl.BlockSpec((d, 2 * LANES), full),
            pl.BlockSpec((1, LANES), full),
            pl.BlockSpec((tm, tm), full),
        ],
        out_specs=[
            pl.BlockSpec((tm, d), lambda i: (i, 0)),
            pl.BlockSpec((tm, d), lambda i: (i, 0)),
            pl.BlockSpec((tm, LANES), lambda i: (i, 0)),
            pl.BlockSpec((SUBLANES, tm), lambda i: (0, i)),
            pl.BlockSpec((1, LANES), full),
        ],
        out_shape=[
            jax.ShapeDtypeStruct((n, d), F32),
            jax.ShapeDtypeStruct((n, d), F32),
            jax.ShapeDtypeStruct((n, LANES), F32),
            jax.ShapeDtypeStruct((SUBLANES, n), F32),
            jax.ShapeDtypeStruct((1, LANES), F32),
        ],
        scratch_shapes=[pltpu.VMEM((1, LANES), F32)],
        compiler_params=_params("arbitrary"),
        name=name,
    )(*rows, w, g, wr, br, tril)


def _index_copy(pos_hbm, idx_s, sem_i, tile, slot):
    return pltpu.make_async_copy(pos_hbm.at[tile], idx_s.at[slot], sem_i.at[slot])


def _dispatch_kernel(pos_hbm, hn_hbm, xs_hbm, idx_s, tiles, sem_i, sem_l, sem_d):
    i = pl.program_id(0)
    nt = pl.num_programs(0)
    nslot, tm, _ = tiles.shape
    islot = i % 2

    def load(t):
        s = t % nslot
        return pltpu.make_async_copy(hn_hbm.at[pl.ds(t * tm, tm)], tiles.at[s], sem_l.at[s])

    def wait_rows(t):
        s = t % nslot
        whole = pltpu.make_async_copy(tiles.at[s], xs_hbm.at[pl.ds(0, tm)], sem_d.at[s])
        whole.wait()
        whole.wait()

    @pl.when(i == 0)
    def _():
        _index_copy(pos_hbm, idx_s, sem_i, 0, 0).start()
        load(0).start()

        @pl.when(nt > 1)
        def _():
            load(1).start()

    @pl.when(i >= 2)
    def _():
        wait_rows(i - 2)

    @pl.when(i + 2 < nt)
    def _():
        load(i + 2).start()

    _index_copy(pos_hbm, idx_s, sem_i, i, islot).wait()

    @pl.when(i + 1 < nt)
    def _():
        _index_copy(pos_hbm, idx_s, sem_i, i + 1, 1 - islot).start()

    load(i).wait()
    slot = i % nslot

    def issue(r, carry):
        row = tiles.at[slot, pl.ds(r, 1)]
        pltpu.make_async_copy(row, xs_hbm.at[pl.ds(idx_s[islot, 0, r], 1)],
                              sem_d.at[slot]).start(priority=0)
        pltpu.make_async_copy(row, xs_hbm.at[pl.ds(idx_s[islot, 0, tm + r], 1)],
                              sem_d.at[slot]).start(priority=1)
        return carry

    lax.fori_loop(0, tm, issue, 0, unroll=DMA_ISSUE_UNROLL)

    @pl.when(i == nt - 1)
    def _():
        @pl.when(nt > 1)
        def _():
            wait_rows(i - 1)

        wait_rows(i)


def _dispatch(pos_tiles, hn):
    n, d = hn.shape
    tm = TOKEN_TILE
    nslot = 4
    return pl.pallas_call(
        _dispatch_kernel,
        grid=(n // tm,),
        in_specs=[pl.BlockSpec(memory_space=pl.ANY), pl.BlockSpec(memory_space=pl.ANY)],
        out_specs=pl.BlockSpec(memory_space=pl.ANY),
        out_shape=jax.ShapeDtypeStruct((2 * n, d), F32),
        scratch_shapes=[pltpu.SMEM((2, 1, 2 * tm), I32), pltpu.VMEM((nslot, tm, d), F32),
                        pltpu.SemaphoreType.DMA((2,)), pltpu.SemaphoreType.DMA((nslot,)),
                        pltpu.SemaphoreType.DMA((nslot,))],
        compiler_params=_params("arbitrary"),
        name="moe_dispatch",
    )(pos_tiles, hn)


def _moe_kernel(vt_ref, ve_ref, von_ref, vnext_ref, vslot_ref, lo_ref, hi_ref,
                xs_ref, wg_hbm, wu_hbm, wd_hbm, eo_ref,
                wg_f, wu_f, wd_f, wgb, wub, wdb, sem_w, *, layer):
    v = pl.program_id(0)
    tr = xs_ref.shape[0]
    prev = jnp.maximum(v - 1, 0)
    e = ve_ref[v]
    new_expert = jnp.logical_or(v == 0, e != ve_ref[prev])
    first_visit = jnp.logical_or(v == 0, vt_ref[v] != vt_ref[prev])

    def fetch(expert, slot):
        return [pltpu.make_async_copy(src.at[layer, expert], dst.at[slot], sem_w.at[slot, j])
                for j, (src, dst) in enumerate(((wg_hbm, wg_f), (wu_hbm, wu_f), (wd_hbm, wd_f)))]

    @pl.when(von_ref[v] == 1)
    def _():
        @pl.when(new_expert)
        def _():
            slot = vslot_ref[v]

            @pl.when(v == 0)
            def _():
                for cp in fetch(e, slot):
                    cp.start()

            for cp in fetch(e, slot):
                cp.wait()
            wgb[...] = wg_f[slot].astype(BF16)
            wub[...] = wu_f[slot].astype(BF16)
            wdb[...] = wd_f[slot].astype(BF16)

            @pl.when(vnext_ref[v] >= 0)
            def _():
                for cp in fetch(vnext_ref[v], 1 - slot):
                    cp.start()

        x = xs_ref[...].astype(BF16)
        act = jax.nn.silu(_dot(x, wgb[...])) * _dot(x, wub[...])
        row = vt_ref[v] * tr + lax.broadcasted_iota(I32, (tr, 1), 0)
        act = jnp.where((row >= lo_ref[e]) & (row < hi_ref[e]), act, 0.0)
        res = _dot(act.astype(BF16), wdb[...])

        @pl.when(first_visit)
        def _():
            eo_ref[...] = res

        @pl.when(jnp.logical_not(first_visit))
        def _():
            eo_ref[...] += res


def _moe_experts(vt, ve, von, vnext, vslot, lo, hi, xs, wg, wu, wd, layer):
    nv = vt.shape[0]
    rows, d = xs.shape
    hid = wg.shape[3]
    tr = EXPERT_ROW_TILE
    tile = pl.BlockSpec((tr, d), lambda v, vt, *_: (vt[v], 0))
    grid_spec = pltpu.PrefetchScalarGridSpec(
        num_scalar_prefetch=7,
        grid=(nv,),
        in_specs=[tile] + [pl.BlockSpec(memory_space=pl.ANY)] * 3,
        out_specs=tile,
        scratch_shapes=[
            pltpu.VMEM((2, d, hid), F32), pltpu.VMEM((2, d, hid), F32), pltpu.VMEM((2, hid, d), F32),
            pltpu.VMEM((d, hid), BF16), pltpu.VMEM((d, hid), BF16), pltpu.VMEM((hid, d), BF16),
            pltpu.SemaphoreType.DMA((2, 3)),
        ],
    )
    return pl.pallas_call(
        functools.partial(_moe_kernel, layer=layer),
        grid_spec=grid_spec,
        out_shape=jax.ShapeDtypeStruct((rows, d), F32),
        compiler_params=_params("arbitrary"),
        name="moe_experts",
    )(vt, ve, von, vnext, vslot, lo, hi, xs, wg, wu, wd)


def _combine_kernel(pos_hbm, eo_hbm, x_ref, rinfo_ref, g_ref, o1_ref, o2_ref, idx_s, a_buf, b_buf,
                    sem_i, sem_a, sem_b, *, last_layer):
    i = pl.program_id(0)
    nt = pl.num_programs(0)
    tm = a_buf.shape[1]
    slot = i % 2

    def gathers(s):
        def issue(r, carry):
            pltpu.make_async_copy(eo_hbm.at[pl.ds(idx_s[s, 0, r], 1)],
                                  a_buf.at[s, pl.ds(r, 1)], sem_a.at[s]).start(priority=0)
            pltpu.make_async_copy(eo_hbm.at[pl.ds(idx_s[s, 0, tm + r], 1)],
                                  b_buf.at[s, pl.ds(r, 1)], sem_b.at[s]).start(priority=1)
            return carry

        lax.fori_loop(0, tm, issue, 0, unroll=DMA_ISSUE_UNROLL)

    @pl.when(i == 0)
    def _():
        first = _index_copy(pos_hbm, idx_s, sem_i, 0, 0)
        first.start()
        first.wait()
        gathers(0)

        @pl.when(nt > 1)
        def _():
            _index_copy(pos_hbm, idx_s, sem_i, 1, 1).start()

    @pl.when(i + 1 < nt)
    def _():
        _index_copy(pos_hbm, idx_s, sem_i, i + 1, 1 - slot).wait()
        gathers(1 - slot)

    @pl.when(i + 2 < nt)
    def _():
        _index_copy(pos_hbm, idx_s, sem_i, i + 2, slot).start()

    pltpu.make_async_copy(eo_hbm.at[pl.ds(0, tm)], a_buf.at[slot], sem_a.at[slot]).wait()
    pltpu.make_async_copy(eo_hbm.at[pl.ds(0, tm)], b_buf.at[slot], sem_b.at[slot]).wait()
    info = rinfo_ref[...]
    x2 = x_ref[...] + (info[:, 2:3] * a_buf[slot] + info[:, 3:4] * b_buf[slot])
    hn = _rms(x2, g_ref[...])
    if last_layer:
        @pl.when(i < nt - 1)
        def _():
            o1_ref[...] = hn

        @pl.when(i == nt - 1)
        def _():
            o2_ref[...] = hn
    else:
        o1_ref[...] = x2
        o2_ref[...] = hn


def _combine(pos_tiles, eo, x, rinfo, g, last_layer):
    n, d = x.shape
    tm = TOKEN_TILE
    nt = n // tm
    row = pl.BlockSpec((tm, d), lambda i: (i, 0))
    if last_layer:
        out_specs = [pl.BlockSpec((tm, d), lambda i: (jnp.minimum(i, nt - 2), 0)),
                     pl.BlockSpec((tm, d), lambda i: (0, 0))]
        out_shape = [jax.ShapeDtypeStruct((n - tm, d), F32), jax.ShapeDtypeStruct((tm, d), F32)]
    else:
        out_specs = [row, row]
        out_shape = [jax.ShapeDtypeStruct((n, d), F32), jax.ShapeDtypeStruct((n, d), F32)]
    return pl.pallas_call(
        functools.partial(_combine_kernel, last_layer=last_layer),
        grid=(nt,),
        in_specs=[
            pl.BlockSpec(memory_space=pl.ANY),
            pl.BlockSpec(memory_space=pl.ANY),
            row,
            pl.BlockSpec((tm, LANES), lambda i: (i, 0)),
            pl.BlockSpec((1, d), lambda i: (0, 0)),
        ],
        out_specs=out_specs,
        out_shape=out_shape,
        scratch_shapes=[
            pltpu.SMEM((2, 1, 2 * tm), I32),
            pltpu.VMEM((2, tm, d), F32),
            pltpu.VMEM((2, tm, d), F32),
            pltpu.SemaphoreType.DMA((2,)),
            pltpu.SemaphoreType.DMA((2,)),
            pltpu.SemaphoreType.DMA((2,)),
        ],
        compiler_params=_params("arbitrary"),
        name="moe_combine",
    )(pos_tiles, eo, x, rinfo, g)


def _moe(x1, hn, rinfo, rinfo_t, counts, wg, wu, wd, layer, next_gain, last_layer):
    n = x1.shape[0]
    ne = wg.shape[1]
    tr = EXPERT_ROW_TILE
    tm = TOKEN_TILE
    nv = (2 * n) // tr + ne - 1
    eid = rinfo_t[0:2].astype(I32)
    rank = rinfo_t[4:6].astype(I32)
    cnt = counts[0, :ne].astype(I32)
    seg_end = jnp.cumsum(cnt)
    seg_start = seg_end - cnt
    experts = jnp.arange(ne, dtype=I32)
    pos = rank + jnp.sum(jnp.where(eid[:, None, :] == experts[None, :, None],
                                   seg_start[None, :, None], 0), axis=1)
    pos_tiles = pos.reshape(2, n // tm, tm).transpose(1, 0, 2).reshape(n // tm, 1, 2 * tm)

    first_tile = seg_start // tr
    tiles_e = jnp.where(cnt > 0, (seg_end - 1) // tr - first_tile + 1, 0)
    v_end = jnp.cumsum(tiles_e)
    v_start = v_end - tiles_e
    total = v_end[-1]
    vis = jnp.arange(nv, dtype=I32)
    vc = jnp.minimum(vis, jnp.maximum(total - 1, 0))
    ve = jnp.sum((vc[:, None] >= v_end[None, :]).astype(I32), axis=1)
    pick = lambda tab: jnp.sum(jnp.where(ve[:, None] == experts, tab, 0), axis=1)
    vt = pick(first_tile) + (vc - pick(v_start))
    von = (vis < total).astype(I32)
    present = cnt > 0
    later = present[None, :] & (experts[None, :] > experts[:, None])
    next_of = jnp.min(jnp.where(later, experts[None, :], ne), axis=1)
    vnext = pick(jnp.where(next_of < ne, next_of, -1))
    vslot = pick(jnp.cumsum(present.astype(I32)) - 1) & 1

    xs = _dispatch(pos_tiles, hn)
    eo = _moe_experts(vt, ve, von, vnext, vslot, seg_start, seg_end, xs, wg, wu, wd, layer)
    return _combine(pos_tiles, eo, x1, rinfo, next_gain, last_layer)


def _cis(log_mag, ang):
    mag = jnp.exp(log_mag)
    return mag * jnp.cos(ang), mag * jnp.sin(ang)


def _split3(a):
    p1 = a.astype(BF16)
    r1 = a - p1.astype(F32)
    p2 = r1.astype(BF16)
    return p1, p2, (r1 - p2.astype(F32)).astype(BF16)


def _select_dot(a, sel, sel_first=False):
    sel = sel.astype(BF16)
    parts = [(_dot(sel, p) if sel_first else _dot(p, sel)) for p in _split3(a)]
    return parts[0] + (parts[1] + parts[2])


def _dot3(a, b):
    a_hi, a_lo = _split_bf16(a)
    b_hi, b_lo = _split_bf16(b)
    return _dot(a_hi, b_hi) + (_dot(a_lo, b_hi) + _dot(a_hi, b_lo))


def _s5_prep_kernel(lsc_ref, lsr_ref, lrc_ref, lic_ref, lrr_ref, lir_ref, ctre_ref, ctim_ref,
                    btre_ref, btim_ref, d_ref,
                    t_ref, wre_ref, wim_ref, cyre_ref, cyim_ref, apr_ref, api_ref,
                    bbr_ref, bbi_ref, lbr_ref, lbi_ref, *, gw, gpb, sub):
    blk = gpb * gw
    wide = sub * blk
    nsl = lrc_ref.shape[1]
    ns = nsl // gpb
    sh_blk, sh_gw, sh_ns = blk.bit_length() - 1, gw.bit_length() - 1, ns.bit_length() - 1
    dt_c, dt_r = jnp.exp(lsc_ref[0]), jnp.exp(lsr_ref[0])
    ldt_c_re, ldt_c_im = lrc_ref[0] * dt_c, lic_ref[0] * dt_c
    ldt_r_re, ldt_r_im = lrr_ref[0] * dt_r, lir_ref[0] * dt_r

    lane_w = lax.broadcasted_iota(I32, (1, wide), 1)
    spread = jnp.where((lax.broadcasted_iota(I32, (gw, wide), 1) & (gw - 1))
                       == lax.broadcasted_iota(I32, (gw, wide), 0), 1.0, 0.0)
    cre = _select_dot(ctre_ref[0], spread)
    cim = _select_dot(ctim_ref[0], spread)
    same = (jnp.right_shift(lax.broadcasted_iota(I32, (nsl, 1), 0), sh_ns)
            == (jnp.right_shift(lane_w, sh_gw) & (gpb - 1)))
    assert 2 * blk == LANES
    tau = lax.broadcasted_iota(I32, (1, LANES), 1).astype(F32)
    pw_re, pw_im = _cis(tau * ldt_c_re, tau * ldt_c_im)
    low = lax.broadcasted_iota(I32, (nsl, LANES), 1) < blk

    def spread_pow(p, first):
        col = lambda t: jnp.broadcast_to(p[:, t:t + 1], (nsl, LANES))
        return jnp.concatenate([jnp.where(low, col(first + 2 * m), col(first + 2 * m + 1))
                                for m in range(sub // 2)], axis=1)

    def c_lam_pow(first):
        pr, pi = spread_pow(pw_re, first), spread_pow(pw_im, first)
        return (jnp.where(same, pr * cre - pi * cim, 0.0),
                jnp.where(same, -(pr * cim + pi * cre), 0.0))

    clr0, cli0 = c_lam_pow(0)
    clr1, cli1 = c_lam_pow(1)
    cyre_ref[0] = clr1.astype(BF16)
    cyim_ref[0] = cli1.astype(BF16)

    lbr, lbi = _cis(ldt_r_re, ldt_r_im)
    lbr_ref[0] = lbr
    lbi_ref[0] = lbi
    lr, li = lrr_ref[0], lir_ref[0]
    nr, ni = lbr - 1.0, lbi
    den = lr * lr + li * li
    fr = (nr * lr + ni * li) / den
    fi = (ni * lr - nr * li) / den
    bre, bim = btre_ref[0], btim_ref[0]
    bbr = fr * bre - fi * bim
    bbi = fr * bim + fi * bre
    bbr_ref[0] = bbr
    bbi_ref[0] = bbi

    lane_t = lax.broadcasted_iota(I32, (gw, wide), 1)
    chan = lax.broadcasted_iota(I32, (gw, wide), 0)
    for g in range(gpb):
        ps = slice(g * ns, (g + 1) * ns)
        r = _dot3(bbr[:, ps], clr0[ps, :]) + _dot3(bbi[:, ps], cli0[ps, :])
        r = r + jnp.where(lane_t == g * gw + chan, d_ref[0][:, g:g + 1], 0.0)
        for j in range(sub):
            tb = r if j == 0 else jnp.where(lane_t >= blk * j, pltpu.roll(r, blk * j, 1), 0.0)
            r0 = j * blk + g * gw
            t_ref[0, r0:r0 + gw, :] = tb.astype(BF16)

    rows = lax.broadcasted_iota(I32, (wide, 1), 0)
    spread_t = jnp.where((lax.broadcasted_iota(I32, (wide, gw), 0) & (gw - 1))
                         == lax.broadcasted_iota(I32, (wide, gw), 1), 1.0, 0.0)
    bbr_t = _select_dot(bbr, spread_t, sel_first=True)
    bbi_t = _select_dot(bbi, spread_t, sel_first=True)
    same_w = ((jnp.right_shift(rows, sh_gw) & (gpb - 1))
              == jnp.right_shift(lax.broadcasted_iota(I32, (1, nsl), 1), sh_ns))
    rj = ((sub - 1) - lax.broadcasted_iota(I32, (sub, 1), 0)).astype(F32)
    q_re, q_im = _cis(rj * ldt_r_re, rj * ldt_r_im)
    per_step = lambda q: jnp.concatenate(
        [jnp.broadcast_to(q[j:j + 1, :], (blk, nsl)) for j in range(sub)], axis=0)
    pr, pi = per_step(q_re), per_step(q_im)
    wre_ref[0] = jnp.where(same_w, pr * bbr_t - pi * bbi_t, 0.0).astype(BF16)
    wim_ref[0] = jnp.where(same_w, pr * bbi_t + pi * bbr_t, 0.0).astype(BF16)

    nlev = apr_ref.shape[1]
    pw = (sub * jnp.left_shift(1, lax.broadcasted_iota(I32, (nlev, 1), 0))).astype(F32)
    apr, api = _cis(pw * ldt_r_re, pw * ldt_r_im)
    apr_ref[0] = apr
    api_ref[0] = api


def _s5_prep(log_step, lam_re, lam_im, b_re, b_im, c_re, c_im, d_skip):
    ng, ns = lam_re.shape
    gw = b_re.shape[2]
    sub, gpb = S5_SUB, S5_GROUPS_PER_BLOCK
    nblk = ng // gpb
    nsl = gpb * ns
    wide = sub * gpb * gw
    ls = jnp.repeat(log_step, ns)
    ct = lambda c: jnp.transpose(c, (0, 2, 1)).reshape(nblk, nsl, gw)
    bt = lambda b: jnp.transpose(b.reshape(nblk, gpb, ns, gw), (0, 3, 1, 2)).reshape(nblk, gw, nsl)
    ins = [ls.reshape(nblk, nsl, 1), ls.reshape(nblk, 1, nsl),
           lam_re.reshape(nblk, nsl, 1), lam_im.reshape(nblk, nsl, 1),
           lam_re.reshape(nblk, 1, nsl), lam_im.reshape(nblk, 1, nsl),
           ct(c_re), ct(c_im), bt(b_re), bt(b_im),
           jnp.transpose(d_skip.reshape(nblk, gpb, gw), (0, 2, 1))]
    spec = lambda a: pl.BlockSpec((1,) + a.shape[1:], lambda g: (g, 0, 0))
    outs = [
        jax.ShapeDtypeStruct((nblk, wide, wide), BF16),
        jax.ShapeDtypeStruct((nblk, wide, nsl), BF16),
        jax.ShapeDtypeStruct((nblk, wide, nsl), BF16),
        jax.ShapeDtypeStruct((nblk, nsl, wide), BF16),
        jax.ShapeDtypeStruct((nblk, nsl, wide), BF16),
        jax.ShapeDtypeStruct((nblk, SUBLANES, nsl), F32),
        jax.ShapeDtypeStruct((nblk, SUBLANES, nsl), F32),
        jax.ShapeDtypeStruct((nblk, gw, nsl), F32),
        jax.ShapeDtypeStruct((nblk, gw, nsl), F32),
        jax.ShapeDtypeStruct((nblk, 1, nsl), F32),
        jax.ShapeDtypeStruct((nblk, 1, nsl), F32),
    ]
    return pl.pallas_call(
        functools.partial(_s5_prep_kernel, gw=gw, gpb=gpb, sub=sub),
        grid=(nblk,),
        in_specs=[spec(a) for a in ins],
        out_specs=[spec(o) for o in outs],
        out_shape=outs,
        compiler_params=_params("parallel"),
        name="s5_prep",
    )(*ins)


def _s5_prompt_kernel(x_ref, t_ref, wre_ref, wim_ref, cyre_ref, cyim_ref, apr_ref, api_ref,
                      y_ref, hre_out, him_out, u_s, y4_s, *, nb, seq, sub):
    nk = seq // sub
    rows = nb * nk
    hl = LANES // 2
    nlev = nk.bit_length() - 1
    nsl = wre_ref.shape[2]
    low = lax.broadcasted_iota(I32, (nk, LANES), 1) < hl
    kidx = lax.broadcasted_iota(I32, (rows, 1), 0) & (nk - 1)

    def step_rows(b, j):
        return (pl.ds(b * seq + j, nk, stride=sub), slice(None))

    for b in range(nb):
        rs = slice(b * nk, (b + 1) * nk)
        for m in range(sub // 2):
            ls = slice(m * LANES, (m + 1) * LANES)
            s0 = x_ref[step_rows(b, 2 * m)]
            s1 = x_ref[step_rows(b, 2 * m + 1)]
            u_s[0, rs, ls] = jnp.where(low, s0, pltpu.roll(s1, hl, 1)).astype(BF16)
            u_s[1, rs, ls] = jnp.where(low, pltpu.roll(s0, hl, 1), s1).astype(BF16)

    def shifted(a, s):
        return jnp.where(kidx >= s, pltpu.roll(a, s, 0), 0.0)

    halves = range(2)
    us = [u_s[h] for h in halves]
    hre = [_dot(us[h], wre_ref[h]) for h in halves]
    him = [_dot(us[h], wim_ref[h]) for h in halves]
    cw = 2 * LANES
    y_conv = [jnp.concatenate(
        [_dot(us[h][:, 0:c0 + cw], t_ref[h, 0:c0 + cw, c0:c0 + cw]) for c0 in range(0, us[h].shape[1], cw)],
        axis=1) for h in halves]
    for i in range(nlev):
        for h in halves:
            ar = apr_ref[h, i:i + 1, :]
            ai = api_ref[h, i:i + 1, :]
            pre, pim = shifted(hre[h], 1 << i), shifted(him[h], 1 << i)
            hre[h], him[h] = hre[h] + (ar * pre - ai * pim), him[h] + (ar * pim + ai * pre)
    for h in halves:
        ls = slice(h * nsl, (h + 1) * nsl)
        for b in range(nb):
            last = (b + 1) * nk - 1
            hre_out[b:b + 1, ls] = hre[h][last:last + 1, :]
            him_out[b:b + 1, ls] = him[h][last:last + 1, :]
        hpre = shifted(hre[h], 1).astype(BF16)
        hpim = shifted(him[h], 1).astype(BF16)
        y4_s[h] = y_conv[h] + _dot(hpre, cyre_ref[h]) + _dot(hpim, cyim_ref[h])

    for b in range(nb):
        rs = slice(b * nk, (b + 1) * nk)
        for m in range(sub // 2):
            ls = slice(m * LANES, (m + 1) * LANES)
            ca = y4_s[0, rs, ls]
            cb = y4_s[1, rs, ls]
            y_ref[step_rows(b, 2 * m)] = jnp.where(low, ca, pltpu.roll(cb, hl, 1))
            y_ref[step_rows(b, 2 * m + 1)] = jnp.where(low, pltpu.roll(ca, hl, 1), cb)


def _s5_prompt(h, prep, batch, seq):
    d = h.shape[1]
    tmat, wre, wim, cyre, cyim, apr, api = prep[:7]
    nblk, wide, nsl = wre.shape
    sub, nb = S5_SUB, S5_BATCH_PER_STEP
    nk = seq // sub
    assert nk & (nk - 1) == 0 and nk.bit_length() - 1 <= apr.shape[1]
    ntile = d // LANES
    assert nblk == 2 * ntile
    rows = nb * nk
    wspec = lambda a: pl.BlockSpec((2,) + a.shape[1:], lambda t, b: (t, 0, 0))
    xspec = pl.BlockSpec((nb * seq, LANES), lambda t, b: (b, t))
    sspec = pl.BlockSpec((None, nb, 2 * nsl), lambda t, b: (b, 0, t))
    return pl.pallas_call(
        functools.partial(_s5_prompt_kernel, nb=nb, seq=seq, sub=sub),
        grid=(ntile, batch // nb),
        in_specs=[xspec, wspec(tmat), wspec(wre), wspec(wim), wspec(cyre), wspec(cyim),
                  wspec(apr), wspec(api)],
        out_specs=[xspec, sspec, sspec],
        out_shape=[
            jax.ShapeDtypeStruct((batch * seq, d), F32),
            jax.ShapeDtypeStruct((batch // nb, nb, ntile * 2 * nsl), F32),
            jax.ShapeDtypeStruct((batch // nb, nb, ntile * 2 * nsl), F32),
        ],
        scratch_shapes=[
            pltpu.VMEM((2, rows, wide), BF16),
            pltpu.VMEM((2, rows, wide), F32),
        ],
        compiler_params=_params("parallel", "parallel"),
        name="s5_prompt",
    )(h, tmat, wre, wim, cyre, cyim, apr, api)


def _s5_step_kernel(u_ref, h_ref, hsw_ref, bb_ref, la_ref, lb_ref, cc_ref, d_ref, hn_ref, y_ref):
    for g in range(u_ref.shape[0]):
        u = u_ref[g]
        hn = (la_ref[g] * h_ref[g] + lb_ref[g] * hsw_ref[g]
              + jnp.dot(u, bb_ref[g], precision=HIGHEST, preferred_element_type=F32))
        hn_ref[g] = hn
        y_ref[g] = (jnp.dot(hn, cc_ref[g], precision=HIGHEST, preferred_element_type=F32)
                    + d_ref[g] * u)


def _s5_step(u_p, h_cat, h_swp, bb_cat, la, lb, cc, d_p):
    ng = u_p.shape[0]
    gb = SUBLANES
    spec = lambda a: pl.BlockSpec((gb,) + a.shape[1:], lambda g: (g, 0, 0))
    ins = [u_p, h_cat, h_swp, bb_cat, la, lb, cc, d_p]
    outs = [jax.ShapeDtypeStruct(h_cat.shape, F32), jax.ShapeDtypeStruct(u_p.shape, F32)]
    return pl.pallas_call(
        _s5_step_kernel,
        grid=(ng // gb,),
        in_specs=[spec(a) for a in ins],
        out_specs=[spec(o) for o in outs],
        out_shape=outs,
        compiler_params=_params("parallel"),
        name="s5_step",
    )(*ins)


def _pad_to(a, axis, size):
    pad = [(0, 0)] * a.ndim
    pad[axis] = (0, size - a.shape[axis])
    return jnp.pad(a, pad)


def kernel(x_prompt, x_sample, state_pool, state_mlstm_c, state_mlstm_n, state_mlstm_m, state_s5_re, state_s5_im, norm_mix, norm_ffn, norm_final, w_in_ab, b_gates, pool_w, pool_scale, w_out_ab, s5_lam_re, s5_lam_im, s5_log_step, s5_b_re, s5_b_im, s5_c_re, s5_c_im, s5_d, w_glu, moe_w_group, moe_b_group, moe_w_expert, moe_b_expert, moe_w_gate, moe_w_up, moe_w_down):
    bp, tp, d = x_prompt.shape
    bs = x_sample.shape[0]
    n_p = bp * tp
    tm = TOKEN_TILE
    assert n_p % tm == 0 and bs <= tm
    n_fill = tm - bs
    nh, dh = state_mlstm_c.shape[2], state_mlstm_c.shape[3]
    pool_width = state_pool.shape[3]
    ml_width = nh * dh
    n_main = pool_width + 4 * ml_width
    n_gates = 2 * nh
    ngrp_s5, n_state = s5_lam_re.shape[1], s5_lam_re.shape[2]
    gw_s5 = d // ngrp_s5

    n = n_p + tm

    def tail_tile(sample_rows, dtype):
        return _pad_to(sample_rows.astype(dtype), 0, tm)

    x_main = x_prompt.reshape(n_p, d)
    x_tail = tail_tile(x_sample.reshape(bs, d), F32)
    tril = jnp.tril(jnp.ones((tm, tm), BF16), -1)

    def router_weights(l):
        wr = _pad_to(jnp.concatenate([moe_w_group[l], moe_w_expert[l]], axis=1), 1, LANES)
        br = jnp.concatenate([moe_b_group[l], moe_b_expert[l]])[None, :]
        hi = wr.astype(BF16)
        lo = (wr - hi.astype(F32)).astype(BF16)
        return jnp.concatenate([hi, lo], axis=1), _pad_to(br, 1, LANES)

    w_in = w_in_ab[0]
    w_g = w_in[:, n_main:]
    z, gates, gates_t = _inproj(
        x_main, x_tail, norm_mix[0][None, :], w_in[:, :n_main].astype(BF16),
        _pad_to(w_g, 1, LANES).astype(BF16), w_g.T.astype(BF16),
        _pad_to(b_gates[0][None, :], 1, LANES), b_gates[0][:, None])

    pw = pool_w[0].astype(BF16)
    ps = pool_scale[0][None, :]
    pool_y_p, pool_p = _pool_prompt(z, pw, ps, bp, tp)
    pool_y_s, pool_s_t = _pool_step(z, jnp.transpose(state_pool[0], (1, 0, 2)), pw, ps, n_p)
    pool_s = jnp.transpose(pool_s_t, (1, 0, 2))

    ml_y_p, c_p, n_p_st, m_p = _mlstm_prompt(z, gates_t, bp, tp, nh, dh)
    ml_y_p = ml_y_p.reshape(n_p, ml_width)
    g_s = gates[n_p:n_p + bs, :n_gates]
    ml_y_s, c_s, n_s_st, m_s = _mlstm_step(
        z[n_p:n_p + bs].reshape(bs, 1, n_main),
        g_s[:, :nh].reshape(bs, nh, 1, 1), g_s[:, nh:].reshape(bs, nh, 1, 1),
        state_mlstm_c[0], state_mlstm_n[0].reshape(bs, nh, 1, dh),
        state_mlstm_m[0].reshape(bs, nh, 1, 1), nh, dh)

    rows = [x_main, x_tail, pool_y_p, tail_tile(pool_y_s, BF16),
            ml_y_p, tail_tile(ml_y_s.reshape(bs, ml_width), BF16)]
    specs = (_stacked_specs(x_main, x_tail) + _stacked_specs(pool_y_p, rows[3])
             + _stacked_specs(ml_y_p, rows[5]))
    wr, br = router_weights(0)
    x1, hn, rinfo, rinfo_t, counts = _mix_route(
        _outproj_route_kernel, "outproj_route", n, specs, rows, w_out_ab[0].astype(BF16),
        norm_ffn[0][None, :], wr, br, tril)
    x2, h1 = _moe(x1, hn, rinfo, rinfo_t, counts, moe_w_gate, moe_w_up, moe_w_down, 0,
                  norm_mix[1][None, :], last_layer=False)

    prep = _s5_prep(s5_log_step[0], s5_lam_re[0], s5_lam_im[0], s5_b_re[0], s5_b_im[0],
                    s5_c_re[0], s5_c_im[0], s5_d[0])
    y_p, hre_p, him_p = _s5_prompt(h1, prep, bp, tp)
    s5_re_p = hre_p.reshape(bp, ngrp_s5, n_state)
    s5_im_p = him_p.reshape(bp, ngrp_s5, n_state)

    gpb = S5_GROUPS_PER_BLOCK
    per_group = lambda a: jnp.transpose(
        a.reshape(ngrp_s5 // gpb, gw_s5, gpb, n_state), (0, 2, 1, 3)).reshape(ngrp_s5, gw_s5, n_state)
    bbr, bbi = per_group(prep[7]), per_group(prep[8])
    lbr, lbi = prep[9].reshape(ngrp_s5, 1, n_state), prep[10].reshape(ngrp_s5, 1, n_state)
    d_g = s5_d[0].reshape(ngrp_s5, 1, gw_s5)
    u_s = h1[n_p:n_p + bs].reshape(bs, ngrp_s5, gw_s5).transpose(1, 0, 2)
    h_re = jnp.transpose(state_s5_re[0], (1, 0, 2))
    h_im = jnp.transpose(state_s5_im[0], (1, 0, 2))
    cc = jnp.concatenate([jnp.transpose(s5_c_re[0], (0, 2, 1)),
                          -jnp.transpose(s5_c_im[0], (0, 2, 1))], axis=1)
    hn_s, y_s = _s5_step(
        _pad_to(u_s, 2, LANES),
        jnp.concatenate([h_re, h_im], axis=2), jnp.concatenate([h_im, h_re], axis=2),
        _pad_to(jnp.concatenate([bbr, bbi], axis=2), 1, LANES),
        jnp.concatenate([lbr, lbr], axis=2), jnp.concatenate([-lbi, lbi], axis=2),
        _pad_to(cc, 2, LANES), _pad_to(d_g, 2, LANES))
    s5_re_s = jnp.transpose(hn_s[:, :, :n_state], (1, 0, 2))
    s5_im_s = jnp.transpose(hn_s[:, :, n_state:], (1, 0, 2))
    y_tail = tail_tile(jnp.transpose(y_s[:, :, :gw_s5], (1, 0, 2)).reshape(bs, d), F32)

    wr, br = router_weights(1)
    specs = [pl.BlockSpec((tm, d), lambda i: (i, 0))] + _stacked_specs(y_p, y_tail)
    x3, hn, rinfo, rinfo_t, counts = _mix_route(
        _glu_route_kernel, "glu_route", n, specs, [x2, y_p, y_tail], w_glu[0].astype(BF16),
        norm_ffn[1][None, :], wr, br, tril)
    y_main, y_last = _moe(x3, hn, rinfo, rinfo_t, counts, moe_w_gate, moe_w_up, moe_w_down, 1,
                          norm_final[None, :], last_layer=True)

    return (y_main.reshape(bp, tp, d), y_last[:bs].reshape(bs, 1, d),
            pool_p[None], c_p[None], n_p_st.reshape(1, bp, nh, dh), m_p[:, :, 0, 0][None],
            s5_re_p[None], s5_im_p[None],
            pool_s[None], c_s[None], n_s_st.reshape(1, bs, nh, dh), m_s.reshape(1, bs, nh),
            s5_re_s[None], s5_im_s[None])
```

```python
import functools

import jax
import jax.numpy as jnp
from jax import lax
from jax.experimental import pallas as pl
from jax.experimental.pallas import tpu as pltpu

F32 = jnp.float32
BF16 = jnp.bfloat16
I32 = jnp.int32

PAST_LEN = 16384
POOL_WINDOWS = (2, 4, 8, 16)
POOL_BUF = max(POOL_WINDOWS) - 1
MLSTM_CHUNK = 128
S5_SUB = 16
MOE_GROUPS = 4
MOE_EXPERTS_PER_GROUP = 8
RMS_EPS = 1e-6

LANES = 128
SUBLANES = 8
VMEM_LIMIT_BYTES = 56 * 1024 * 1024

TOKEN_TILE = 512
EXPERT_ROW_TILE = 512
MLSTM_SEQ_PER_STEP = 8
MLSTM_STEP_BATCH = 16
POOL_TIME_TILE = 1024
DMA_ISSUE_UNROLL = 8
S5_GROUPS_PER_BLOCK = 4
S5_BATCH_PER_STEP = 4


def _params(*sem):
    return pltpu.CompilerParams(dimension_semantics=sem, vmem_limit_bytes=VMEM_LIMIT_BYTES)


def _rms(x, g):
    return x * lax.rsqrt(jnp.mean(x * x, axis=-1, keepdims=True) + RMS_EPS) * g


def _dot(a, b):
    return jnp.dot(a, b, preferred_element_type=F32)


def _dot_nt(a, b):
    return lax.dot_general(a, b, (((1,), (1,)), ((), ())), preferred_element_type=F32)


def _dot_tn(a, b):
    return lax.dot_general(a, b, (((0,), (0,)), ((), ())), preferred_element_type=F32)


def _stacked(main_ref, tail_ref):
    last = pl.program_id(0) == pl.num_programs(0) - 1
    return jnp.where(last, tail_ref[...], main_ref[...])


def _stacked_specs(main, tail):
    tm, w = tail.shape
    last_main = main.shape[0] // tm - 1
    return [pl.BlockSpec((tm, w), lambda i: (jnp.minimum(i, last_main), 0)),
            pl.BlockSpec((tm, w), lambda i: (0, 0))]


def _inproj_kernel(xm_ref, xt_ref, g_ref, w_ref, bg_ref, z_ref, gates_ref, gatest_ref):
    h = _rms(_stacked(xm_ref, xt_ref), g_ref[...]).astype(BF16)
    zg = _dot(h, w_ref[...])
    nz = z_ref.shape[1]
    z_ref[...] = zg[:, 0:nz]
    gates = zg[:, nz:nz + LANES] + bg_ref[...]
    gates_ref[...] = gates
    gatest_ref[...] = gates.T[0:gatest_ref.shape[0], :]


def _inproj(x_main, x_tail, g, w, bg, ng):
    d = x_main.shape[1]
    tm = TOKEN_TILE
    n = x_main.shape[0] + tm
    nz = w.shape[1] - LANES
    full = lambda i: (0, 0)
    return pl.pallas_call(
        _inproj_kernel,
        grid=(n // tm,),
        in_specs=_stacked_specs(x_main, x_tail) + [
            pl.BlockSpec((1, d), full),
            pl.BlockSpec((d, nz + LANES), full),
            pl.BlockSpec((1, LANES), full),
        ],
        out_specs=[
            pl.BlockSpec((tm, nz), lambda i: (i, 0)),
            pl.BlockSpec((tm, LANES), lambda i: (i, 0)),
            pl.BlockSpec((ng, tm), lambda i: (0, i)),
        ],
        out_shape=[
            jax.ShapeDtypeStruct((n, nz), F32),
            jax.ShapeDtypeStruct((n, LANES), F32),
            jax.ShapeDtypeStruct((ng, n), F32),
        ],
        compiler_params=_params("parallel"),
        name="inproj",
    )(x_main, x_tail, g, w, bg)


def _pool_prompt_kernel(u_ref, pw_ref, ps_ref, y_ref, st_ref, ext_ref, *, tt, gw):
    t = pl.program_id(1)
    nt = pl.num_programs(1)
    halo = POOL_BUF + 1
    width = ext_ref.shape[1]

    @pl.when(t == 0)
    def _():
        ext_ref[0:halo, :] = jnp.zeros((halo, width), F32)

    u = u_ref[...]
    ext_ref[halo:halo + tt, :] = u
    pos = t * tt + lax.broadcasted_iota(I32, (tt, 1), 0)
    for g, w in enumerate(POOL_WINDOWS):
        c0 = g * gw
        acc = u[:, c0:c0 + gw]
        for j in range(1, w):
            acc = acc + ext_ref[halo - j:halo - j + tt, c0:c0 + gw]
        cnt = jnp.minimum(w, pos + 1).astype(F32)
        d = acc / cnt - u[:, c0:c0 + gw]
        y = _dot(d.astype(BF16), pw_ref[g]) * ps_ref[:, c0:c0 + gw]
        y_ref[:, c0:c0 + gw] = y.astype(BF16)

    @pl.when(t == nt - 1)
    def _():
        st_ref[...] = ext_ref[tt + 1:tt + halo, :]

    ext_ref[0:halo, :] = ext_ref[tt:tt + halo, :]


def _pool_prompt(z, pw, ps, batch, seq):
    width = ps.shape[1]
    gw = width // len(POOL_WINDOWS)
    tt = POOL_TIME_TILE
    nt = seq // tt
    return pl.pallas_call(
        functools.partial(_pool_prompt_kernel, tt=tt, gw=gw),
        grid=(batch, nt),
        in_specs=[
            pl.BlockSpec((tt, width), lambda b, t: (b * nt + t, 0)),
            pl.BlockSpec(pw.shape, lambda b, t: (0, 0, 0)),
            pl.BlockSpec((1, width), lambda b, t: (0, 0)),
        ],
        out_specs=[
            pl.BlockSpec((tt, width), lambda b, t: (b * nt + t, 0)),
            pl.BlockSpec((None, POOL_BUF, width), lambda b, t: (b, 0, 0)),
        ],
        out_shape=[
            jax.ShapeDtypeStruct((batch * seq, width), BF16),
            jax.ShapeDtypeStruct((batch, POOL_BUF, width), F32),
        ],
        scratch_shapes=[pltpu.VMEM((POOL_BUF + 1 + tt, width), F32)],
        compiler_params=_params("parallel", "arbitrary"),
        name="pool_prompt",
    )(z, pw, ps)


def _pool_step_kernel(u_ref, buf_ref, pw_ref, ps_ref, y_ref, nb_ref, *, gw):
    u = u_ref[...]
    for g, w in enumerate(POOL_WINDOWS):
        c0 = g * gw
        acc = u[:, c0:c0 + gw]
        for j in range(1, w):
            acc = acc + buf_ref[POOL_BUF - j, :, c0:c0 + gw]
        cnt = float(min(w, PAST_LEN + 1))
        d = acc / cnt - u[:, c0:c0 + gw]
        y = _dot(d.astype(BF16), pw_ref[g]) * ps_ref[:, c0:c0 + gw]
        y_ref[:, c0:c0 + gw] = y.astype(BF16)
    nb_ref[0:POOL_BUF - 1] = buf_ref[1:POOL_BUF]
    nb_ref[POOL_BUF - 1] = u


def _pool_step(z, buf_t, pw, ps, row0):
    _, batch, width = buf_t.shape
    gw = width // len(POOL_WINDOWS)
    return pl.pallas_call(
        functools.partial(_pool_step_kernel, gw=gw),
        grid=(1,),
        in_specs=[
            pl.BlockSpec((batch, width), lambda i: (row0 // batch, 0)),
            pl.BlockSpec(buf_t.shape, lambda i: (0, 0, 0)),
            pl.BlockSpec(pw.shape, lambda i: (0, 0, 0)),
            pl.BlockSpec((1, width), lambda i: (0, 0)),
        ],
        out_specs=[
            pl.BlockSpec((batch, width), lambda i: (0, 0)),
            pl.BlockSpec(buf_t.shape, lambda i: (0, 0, 0)),
        ],
        out_shape=[
            jax.ShapeDtypeStruct((batch, width), BF16),
            jax.ShapeDtypeStruct(buf_t.shape, F32),
        ],
        compiler_params=_params("arbitrary"),
        name="pool_step",
    )(z, buf_t, pw, ps)


def _mlstm_prompt_kernel(*refs, nh, dh, nseq):
    seq_in = [refs[5 * s:5 * s + 5] for s in range(nseq)]
    h_ref, c_out, n_out, m_out, c_s, n_s, m_s = refs[5 * nseq:]
    ci = pl.program_id(1)
    nc = pl.num_programs(1)
    ln = seq_in[0][0].shape[0]

    @pl.when(ci == 0)
    def _():
        c_s[...] = jnp.zeros(c_s.shape, F32)
        n_s[...] = jnp.zeros(n_s.shape, F32)
        m_s[...] = jnp.zeros(m_s.shape, F32)

    row = lax.broadcasted_iota(I32, (ln, ln), 0)
    col = lax.broadcasted_iota(I32, (ln, ln), 1)
    causal_t = row <= col
    eye = col == row
    lane = lax.broadcasted_iota(I32, (nh, ln), 1)

    def to_col(r):
        return jnp.sum(jnp.where(eye, r, 0.0), axis=1, keepdims=True)

    scale = dh ** -0.5
    pairs = [(s, h) for s in range(nseq) for h in range(nh)]
    c_old = {p: c_s[p[0], p[1]] for p in pairs}
    n_old = {p: n_s[p[0], p[1]] for p in pairs}
    m_old = {p: m_s[p[0], p[1]][:, 0:1] for p in pairs}
    gates = []
    for s in range(nseq):
        gt = seq_in[s][4][...]
        bc_all = jax.nn.log_sigmoid(gt[nh:2 * nh])
        sh = 1
        while sh < ln:
            bc_all = bc_all + jnp.where(lane >= sh, pltpu.roll(bc_all, sh, 1), 0.0)
            sh *= 2
        gates.append((gt[0:nh], bc_all))
    ch = {}
    for s, h in pairs:
        q_ref, k_ref, v_ref, _, _ = seq_in[s]
        li_all, bc_all = gates[s]
        sl = slice(h * dh, (h + 1) * dh)
        k = k_ref[:, sl] * scale
        v = v_ref[:, sl]
        li_r, bc_r = li_all[h:h + 1], bc_all[h:h + 1]
        b_end = bc_r[:, ln - 1:ln]
        g_r = b_end - bc_r + li_r
        m0 = m_old[s, h]
        m_new = jnp.maximum(b_end + m0, jnp.max(g_r, axis=1, keepdims=True))
        ch[s, h] = dict(
            qb=q_ref[:, sl].astype(BF16), kb=k.astype(BF16), vb=v.astype(BF16), v=v,
            inter=bc_r + m0, m_new=m_new, wg_r=jnp.exp(g_r - m_new),
            decay=jnp.exp(b_end + m0 - m_new),
            dmat_t=jnp.where(causal_t, bc_r - to_col(bc_r - li_r), -jnp.inf))
    for p in pairs:
        d = ch[p]
        d["st"] = _dot_nt(d["kb"], d["qb"])
        d["cq_t"] = _dot_nt(c_old[p].astype(BF16), d["qb"])
        d["nq"] = _dot_nt(jnp.broadcast_to(n_old[p], (SUBLANES, dh)).astype(BF16), d["qb"])[0:1]
        d["c_add"] = _dot((d["v"].T * d["wg_r"]).astype(BF16), d["kb"])
        d["n_add"] = _dot(jnp.broadcast_to(d["wg_r"], (SUBLANES, ln)).astype(BF16), d["kb"])[0:1]
    for p in pairs:
        d = ch[p]
        d["m_row"] = jnp.maximum(d["inter"], jnp.max(d["dmat_t"], axis=0, keepdims=True))
        d["w_inter"] = jnp.exp(d["inter"] - d["m_row"])
        d["sc_t"] = d["st"] * jnp.exp(d["dmat_t"] - d["m_row"])
    for p in pairs:
        d = ch[p]
        d["pv_t"] = _dot_tn(d["vb"], d["sc_t"].astype(BF16))
    new_state = {}
    for s, h in pairs:
        d = ch[s, h]
        sl = slice(h * dh, (h + 1) * dh)
        num_t = d["pv_t"] + d["w_inter"] * d["cq_t"]
        den = jnp.sum(d["sc_t"], axis=0, keepdims=True) + d["w_inter"] * d["nq"]
        hh = (num_t / jnp.maximum(jnp.abs(den), jnp.exp(-d["m_row"]))).T
        h_ref[s, :, sl] = (hh * jax.nn.sigmoid(seq_in[s][3][:, sl])).astype(BF16)
        new_state[s, h] = (d["decay"] * c_old[s, h] + d["c_add"],
                           d["decay"] * n_old[s, h] + d["n_add"],
                           jnp.broadcast_to(d["m_new"], (1, dh)))

    for (s, h), (c_new, n_new, m_new) in new_state.items():
        c_s[s, h] = c_new
        n_s[s, h] = n_new
        m_s[s, h] = m_new

    @pl.when(ci == nc - 1)
    def _():
        c_out[...] = c_s[...]
        n_out[...] = n_s[...]
        m_out[...] = m_s[...]


def _mlstm_prompt(z, gates_t, batch, seq, nh, dh):
    ln = MLSTM_CHUNK
    nc = seq // ln
    nseq = MLSTM_SEQ_PER_STEP
    width = nh * dh
    ng = gates_t.shape[0]
    rows_of = lambda s: (lambda b, c: (b * nseq + s) * nc + c)
    in_specs, operands = [], []
    for s in range(nseq):
        r = rows_of(s)
        for j in (1, 2, 3, 4):
            in_specs.append(pl.BlockSpec((ln, width), lambda b, c, r=r, j=j: (r(b, c), j)))
        in_specs.append(pl.BlockSpec((ng, ln), lambda b, c, r=r: (0, r(b, c))))
        operands += [z, z, z, z, gates_t]
    h_spec = pl.BlockSpec((None, nseq, ln, width), lambda b, c: (b, 0, c, 0))
    st = lambda a, b_: pl.BlockSpec((nseq, nh, a, b_), lambda b, c: (b, 0, 0, 0))
    return pl.pallas_call(
        functools.partial(_mlstm_prompt_kernel, nh=nh, dh=dh, nseq=nseq),
        grid=(batch // nseq, nc),
        in_specs=in_specs,
        out_specs=[h_spec, st(dh, dh), st(1, dh), st(1, dh)],
        out_shape=[
            jax.ShapeDtypeStruct((batch // nseq, nseq, seq, width), BF16),
            jax.ShapeDtypeStruct((batch, nh, dh, dh), F32),
            jax.ShapeDtypeStruct((batch, nh, 1, dh), F32),
            jax.ShapeDtypeStruct((batch, nh, 1, dh), F32),
        ],
        scratch_shapes=[pltpu.VMEM((nseq, nh, dh, dh), F32), pltpu.VMEM((nseq, nh, 1, dh), F32),
                        pltpu.VMEM((nseq, nh, 1, dh), F32)],
        compiler_params=_params("parallel", "arbitrary"),
        name="mlstm_prompt",
    )(*operands)


def _mlstm_step_kernel(q_ref, k_ref, v_ref, o_ref, li_ref, fp_ref, c_ref, n_ref, m_ref,
                       h_ref, c_out, n_out, m_out, *, nh, dh):
    eye = (lax.broadcasted_iota(I32, (1, dh, dh), 1) == lax.broadcasted_iota(I32, (1, dh, dh), 2))
    scale = dh ** -0.5
    for h in range(nh):
        sl = slice(h * dh, (h + 1) * dh)
        q = q_ref[:, :, sl]
        k = k_ref[:, :, sl] * scale
        v = v_ref[:, :, sl]
        c = c_ref[:, h]
        n = n_ref[:, h]
        m = m_ref[:, h]
        li = li_ref[:, h]
        lf = jax.nn.log_sigmoid(fp_ref[:, h])
        inter = lf + m
        m_row = jnp.maximum(inter, li)
        w_intra = jnp.exp(li - m_row)
        w_inter = jnp.exp(inter - m_row)
        sc = jnp.sum(q * k, axis=-1, keepdims=True) * w_intra
        v_c = jnp.sum(jnp.where(eye, v, 0.0), axis=-1, keepdims=True)
        num = sc * v_c + w_inter * jnp.sum(c * q, axis=-1, keepdims=True)
        den = sc + w_inter * jnp.sum(n * q, axis=-1, keepdims=True)
        h_c = num / jnp.maximum(jnp.abs(den), jnp.exp(-m_row))
        h_l = jnp.sum(jnp.where(eye, h_c, 0.0), axis=1, keepdims=True)
        h_ref[:, :, sl] = h_l * jax.nn.sigmoid(o_ref[:, :, sl])
        wg = jnp.exp(li - m_row)
        decay = jnp.exp(inter - m_row)
        c_out[:, h] = decay * c + (v_c * wg) * k
        n_out[:, h] = decay * n + wg * k
        m_out[:, h] = m_row


def _mlstm_step(z3, li, fp, c, n, m, nh, dh):
    batch = c.shape[0]
    bb = MLSTM_STEP_BATCH
    width = nh * dh
    blk = lambda j: pl.BlockSpec((bb, 1, width), lambda i: (i, 0, j))
    st4 = lambda a, b: pl.BlockSpec((bb, nh, a, b), lambda i: (i, 0, 0, 0))
    return pl.pallas_call(
        functools.partial(_mlstm_step_kernel, nh=nh, dh=dh),
        grid=(batch // bb,),
        in_specs=[blk(1), blk(2), blk(3), blk(4), st4(1, 1), st4(1, 1),
                  st4(dh, dh), st4(1, dh), st4(1, 1)],
        out_specs=[pl.BlockSpec((bb, 1, width), lambda i: (i, 0, 0)),
                   st4(dh, dh), st4(1, dh), st4(1, 1)],
        out_shape=[
            jax.ShapeDtypeStruct((batch, 1, width), F32),
            jax.ShapeDtypeStruct((batch, nh, dh, dh), F32),
            jax.ShapeDtypeStruct((batch, nh, 1, dh), F32),
            jax.ShapeDtypeStruct((batch, nh, 1, 1), F32),
        ],
        compiler_params=_params("parallel"),
        name="mlstm_step",
    )(z3, z3, z3, z3, li, fp, c, n, m)


def _split_bf16(a):
    hi = a.astype(BF16)
    return hi, (a - hi.astype(F32)).astype(BF16)


def _route(hn, wr_ref, br_ref, tril_ref, carry_ref, rinfo_ref, rinfo_t_ref):
    ngrp, epg = MOE_GROUPS, MOE_EXPERTS_PER_GROUP
    h_hi, h_lo = _split_bf16(hn)
    both = _dot(h_hi, wr_ref[...])
    logits = both[:, 0:LANES] + (both[:, LANES:2 * LANES] + _dot(h_lo, wr_ref[:, 0:LANES])) + br_ref[...]
    tm = logits.shape[0]
    lane = lax.broadcasted_iota(I32, (tm, LANES), 1)
    neg = -jnp.inf

    def first_max(x):
        mx = jnp.max(x, axis=1, keepdims=True)
        idx = jnp.min(jnp.where(x == mx, lane, LANES), axis=1, keepdims=True)
        return mx, idx

    is_grp = lane < ngrp
    gmax, gsel = first_max(jnp.where(is_grp, logits, neg))
    g_w = 1.0 / jnp.sum(jnp.where(is_grp, jnp.exp(logits - gmax), 0.0), axis=1, keepdims=True)
    lo = ngrp + gsel * epg
    el = jnp.where((lane >= lo) & (lane < lo + epg), logits, neg)
    v1, i1 = first_max(el)
    v2, i2 = first_max(jnp.where(lane == i1, neg, el))
    e2 = jnp.exp(v2 - v1)
    w1 = g_w / (1.0 + e2)
    w2 = g_w * e2 / (1.0 + e2)
    eid1 = i1 - ngrp
    eid2 = i2 - ngrp

    hit1 = lane == eid1
    hit2 = lane == eid2
    onehot = jnp.where(hit1 | hit2, 1.0, 0.0)
    carry = carry_ref[...]
    prefix = _dot(tril_ref[...], onehot.astype(BF16)) + carry
    rank1 = jnp.sum(jnp.where(hit1, prefix, 0.0), axis=1, keepdims=True)
    rank2 = jnp.sum(jnp.where(hit2, prefix, 0.0), axis=1, keepdims=True)
    carry_ref[...] = carry + jnp.sum(onehot, axis=0, keepdims=True)

    cols = (eid1.astype(F32), eid2.astype(F32), w1, w2, rank1, rank2)
    info = jnp.zeros((tm, LANES), F32)
    for j, cval in enumerate(cols):
        info = jnp.where(lane == j, cval, info)
    rinfo_ref[...] = info
    pick = jnp.where(lax.broadcasted_iota(I32, (SUBLANES, LANES), 0)
                     == lax.broadcasted_iota(I32, (SUBLANES, LANES), 1), 1.0, 0.0).astype(BF16)
    parts = [_dot_nt(pick, p) for p in _split3(info)]
    rinfo_t_ref[...] = parts[0] + (parts[1] + parts[2])


def _outproj_route_kernel(xm_ref, xt_ref, pm_ref, pt_ref, mm_ref, mt_ref,
                          wo_ref, g_ref, wr_ref, br_ref, tril_ref,
                          x1_ref, hn_ref, rinfo_ref, rinfo_t_ref, cnt_ref, carry_ref):
    @pl.when(pl.program_id(0) == 0)
    def _():
        carry_ref[...] = jnp.zeros(carry_ref.shape, F32)

    half = pm_ref.shape[1]
    mix = (_dot(_stacked(pm_ref, pt_ref), wo_ref[0:half, :])
           + _dot(_stacked(mm_ref, mt_ref), wo_ref[half:2 * half, :]))
    x1 = _stacked(xm_ref, xt_ref) + mix
    x1_ref[...] = x1
    hn = _rms(x1, g_ref[...])
    hn_ref[...] = hn
    _route(hn, wr_ref, br_ref, tril_ref, carry_ref, rinfo_ref, rinfo_t_ref)
    cnt_ref[...] = carry_ref[...]


def _glu_route_kernel(x_ref, ym_ref, yt_ref, wglu_ref, g_ref, wr_ref, br_ref, tril_ref,
                      x1_ref, hn_ref, rinfo_ref, rinfo_t_ref, cnt_ref, carry_ref):
    @pl.when(pl.program_id(0) == 0)
    def _():
        carry_ref[...] = jnp.zeros(carry_ref.shape, F32)

    d = x_ref.shape[1]
    ag = _dot(jax.nn.gelu(_stacked(ym_ref, yt_ref)).astype(BF16), wglu_ref[...])
    x1 = x_ref[...] + ag[:, 0:d] * jax.nn.sigmoid(ag[:, d:2 * d])
    x1_ref[...] = x1
    hn = _rms(x1, g_ref[...])
    hn_ref[...] = hn
    _route(hn, wr_ref, br_ref, tril_ref, carry_ref, rinfo_ref, rinfo_t_ref)
    cnt_ref[...] = carry_ref[...]


def _mix_route(kernel, name, n, row_specs, rows, w, g, wr, br, tril):
    d = g.shape[1]
    tm = TOKEN_TILE
    full = lambda i: (0, 0)
    return pl.pallas_call(
        kernel,
        grid=(n // tm,),
        in_specs=row_specs + [
            pl.BlockSpec(w.shape, full),
            pl.BlockSpec((1, d), full),
            pl.BlockSpec((d, 2 * LANES), full),
            pl.BlockSpec((1, LANES), full),
            pl.BlockSpec((tm, tm), full),
        ],
        out_specs=[
            pl.BlockSpec((tm, d), lambda i: (i, 0)),
            pl.BlockSpec((tm, d), lambda i: (i, 0)),
            pl.BlockSpec((tm, LANES), lambda i: (i, 0)),
            pl.BlockSpec((SUBLANES, tm), lambda i: (0, i)),
            pl.BlockSpec((1, LANES), full),
        ],
        out_shape=[
            jax.ShapeDtypeStruct((n, d), F32),
            jax.ShapeDtypeStruct((n, d), F32),
            jax.ShapeDtypeStruct((n, LANES), F32),
            jax.ShapeDtypeStruct((SUBLANES, n), F32),
            jax.ShapeDtypeStruct((1, LANES), F32),
        ],
        scratch_shapes=[pltpu.VMEM((1, LANES), F32)],
        compiler_params=_params("arbitrary"),
        name=name,
    )(*rows, w, g, wr, br, tril)


def _index_copy(pos_hbm, idx_s, sem_i, tile, slot):
    return pltpu.make_async_copy(pos_hbm.at[tile], idx_s.at[slot], sem_i.at[slot])


def _dispatch_kernel(pos_hbm, hn_hbm, xs_hbm, idx_s, tiles, sem_i, sem_l, sem_d):
    i = pl.program_id(0)
    nt = pl.num_programs(0)
    nslot, tm, _ = tiles.shape
    islot = i % 2

    def load(t):
        s = t % nslot
        return pltpu.make_async_copy(hn_hbm.at[pl.ds(t * tm, tm)], tiles.at[s], sem_l.at[s])

    def wait_rows(t):
        s = t % nslot
        whole = pltpu.make_async_copy(tiles.at[s], xs_hbm.at[pl.ds(0, tm)], sem_d.at[s])
        whole.wait()
        whole.wait()

    @pl.when(i == 0)
    def _():
        _index_copy(pos_hbm, idx_s, sem_i, 0, 0).start()
        load(0).start()

        @pl.when(nt > 1)
        def _():
            load(1).start()

    @pl.when(i >= 2)
    def _():
        wait_rows(i - 2)

    @pl.when(i + 2 < nt)
    def _():
        load(i + 2).start()

    _index_copy(pos_hbm, idx_s, sem_i, i, islot).wait()

    @pl.when(i + 1 < nt)
    def _():
        _index_copy(pos_hbm, idx_s, sem_i, i + 1, 1 - islot).start()

    load(i).wait()
    slot = i % nslot

    def issue(r, carry):
        row = tiles.at[slot, pl.ds(r, 1)]
        pltpu.make_async_copy(row, xs_hbm.at[pl.ds(idx_s[islot, 0, r], 1)],
                              sem_d.at[slot]).start(priority=0)
        pltpu.make_async_copy(row, xs_hbm.at[pl.ds(idx_s[islot, 0, tm + r], 1)],
                              sem_d.at[slot]).start(priority=1)
        return carry

    lax.fori_loop(0, tm, issue, 0, unroll=DMA_ISSUE_UNROLL)

    @pl.when(i == nt - 1)
    def _():
        @pl.when(nt > 1)
        def _():
            wait_rows(i - 1)

        wait_rows(i)


def _dispatch(pos_tiles, hn):
    n, d = hn.shape
    tm = TOKEN_TILE
    nslot = 4
    return pl.pallas_call(
        _dispatch_kernel,
        grid=(n // tm,),
        in_specs=[pl.BlockSpec(memory_space=pl.ANY), pl.BlockSpec(memory_space=pl.ANY)],
        out_specs=pl.BlockSpec(memory_space=pl.ANY),
        out_shape=jax.ShapeDtypeStruct((2 * n, d), F32),
        scratch_shapes=[pltpu.SMEM((2, 1, 2 * tm), I32), pltpu.VMEM((nslot, tm, d), F32),
                        pltpu.SemaphoreType.DMA((2,)), pltpu.SemaphoreType.DMA((nslot,)),
                        pltpu.SemaphoreType.DMA((nslot,))],
        compiler_params=_params("arbitrary"),
        name="moe_dispatch",
    )(pos_tiles, hn)


def _moe_kernel(vt_ref, ve_ref, von_ref, vnext_ref, vslot_ref, lo_ref, hi_ref,
                xs_ref, wg_hbm, wu_hbm, wd_hbm, eo_ref,
                wg_f, wu_f, wd_f, wgb, wub, wdb, sem_w, *, layer):
    v = pl.program_id(0)
    tr = xs_ref.shape[0]
    prev = jnp.maximum(v - 1, 0)
    e = ve_ref[v]
    new_expert = jnp.logical_or(v == 0, e != ve_ref[prev])
    first_visit = jnp.logical_or(v == 0, vt_ref[v] != vt_ref[prev])

    def fetch(expert, slot):
        return [pltpu.make_async_copy(src.at[layer, expert], dst.at[slot], sem_w.at[slot, j])
                for j, (src, dst) in enumerate(((wg_hbm, wg_f), (wu_hbm, wu_f), (wd_hbm, wd_f)))]

    @pl.when(von_ref[v] == 1)
    def _():
        @pl.when(new_expert)
        def _():
            slot = vslot_ref[v]

            @pl.when(v == 0)
            def _():
                for cp in fetch(e, slot):
                    cp.start()

            for cp in fetch(e, slot):
                cp.wait()
            wgb[...] = wg_f[slot].astype(BF16)
            wub[...] = wu_f[slot].astype(BF16)
            wdb[...] = wd_f[slot].astype(BF16)

            @pl.when(vnext_ref[v] >= 0)
            def _():
                for cp in fetch(vnext_ref[v], 1 - slot):
                    cp.start()

        x = xs_ref[...].astype(BF16)
        act = jax.nn.silu(_dot(x, wgb[...])) * _dot(x, wub[...])
        row = vt_ref[v] * tr + lax.broadcasted_iota(I32, (tr, 1), 0)
        act = jnp.where((row >= lo_ref[e]) & (row < hi_ref[e]), act, 0.0)
        res = _dot(act.astype(BF16), wdb[...])

        @pl.when(first_visit)
        def _():
            eo_ref[...] = res

        @pl.when(jnp.logical_not(first_visit))
        def _():
            eo_ref[...] += res


def _moe_experts(vt, ve, von, vnext, vslot, lo, hi, xs, wg, wu, wd, layer):
    nv = vt.shape[0]
    rows, d = xs.shape
    hid = wg.shape[3]
    tr = EXPERT_ROW_TILE
    tile = pl.BlockSpec((tr, d), lambda v, vt, *_: (vt[v], 0))
    grid_spec = pltpu.PrefetchScalarGridSpec(
        num_scalar_prefetch=7,
        grid=(nv,),
        in_specs=[tile] + [pl.BlockSpec(memory_space=pl.ANY)] * 3,
        out_specs=tile,
        scratch_shapes=[
            pltpu.VMEM((2, d, hid), F32), pltpu.VMEM((2, d, hid), F32), pltpu.VMEM((2, hid, d), F32),
            pltpu.VMEM((d, hid), BF16), pltpu.VMEM((d, hid), BF16), pltpu.VMEM((hid, d), BF16),
            pltpu.SemaphoreType.DMA((2, 3)),
        ],
    )
    return pl.pallas_call(
        functools.partial(_moe_kernel, layer=layer),
        grid_spec=grid_spec,
        out_shape=jax.ShapeDtypeStruct((rows, d), F32),
        compiler_params=_params("arbitrary"),
        name="moe_experts",
    )(vt, ve, von, vnext, vslot, lo, hi, xs, wg, wu, wd)


def _combine_kernel(pos_hbm, eo_hbm, x_ref, rinfo_ref, g_ref, o1_ref, o2_ref, idx_s, a_buf, b_buf,
                    sem_i, sem_a, sem_b, *, last_layer):
    i = pl.program_id(0)
    nt = pl.num_programs(0)
    tm = a_buf.shape[1]
    slot = i % 2

    def gathers(s):
        def issue(r, carry):
            pltpu.make_async_copy(eo_hbm.at[pl.ds(idx_s[s, 0, r], 1)],
                                  a_buf.at[s, pl.ds(r, 1)], sem_a.at[s]).start(priority=0)
            pltpu.make_async_copy(eo_hbm.at[pl.ds(idx_s[s, 0, tm + r], 1)],
                                  b_buf.at[s, pl.ds(r, 1)], sem_b.at[s]).start(priority=1)
            return carry

        lax.fori_loop(0, tm, issue, 0, unroll=DMA_ISSUE_UNROLL)

    @pl.when(i == 0)
    def _():
        first = _index_copy(pos_hbm, idx_s, sem_i, 0, 0)
        first.start()
        first.wait()
        gathers(0)

        @pl.when(nt > 1)
        def _():
            _index_copy(pos_hbm, idx_s, sem_i, 1, 1).start()

    @pl.when(i + 1 < nt)
    def _():
        _index_copy(pos_hbm, idx_s, sem_i, i + 1, 1 - slot).wait()
        gathers(1 - slot)

    @pl.when(i + 2 < nt)
    def _():
        _index_copy(pos_hbm, idx_s, sem_i, i + 2, slot).start()

    pltpu.make_async_copy(eo_hbm.at[pl.ds(0, tm)], a_buf.at[slot], sem_a.at[slot]).wait()
    pltpu.make_async_copy(eo_hbm.at[pl.ds(0, tm)], b_buf.at[slot], sem_b.at[slot]).wait()
    info = rinfo_ref[...]
    x2 = x_ref[...] + (info[:, 2:3] * a_buf[slot] + info[:, 3:4] * b_buf[slot])
    hn = _rms(x2, g_ref[...])
    if last_layer:
        @pl.when(i < nt - 1)
        def _():
            o1_ref[...] = hn

        @pl.when(i == nt - 1)
        def _():
            o2_ref[...] = hn
    else:
        o1_ref[...] = x2
        o2_ref[...] = hn


def _combine(pos_tiles, eo, x, rinfo, g, last_layer):
    n, d = x.shape
    tm = TOKEN_TILE
    nt = n // tm
    row = pl.BlockSpec((tm, d), lambda i: (i, 0))
    if last_layer:
        out_specs = [pl.BlockSpec((tm, d), lambda i: (jnp.minimum(i, nt - 2), 0)),
                     pl.BlockSpec((tm, d), lambda i: (0, 0))]
        out_shape = [jax.ShapeDtypeStruct((n - tm, d), F32), jax.ShapeDtypeStruct((tm, d), F32)]
    else:
        out_specs = [row, row]
        out_shape = [jax.ShapeDtypeStruct((n, d), F32), jax.ShapeDtypeStruct((n, d), F32)]
    return pl.pallas_call(
        functools.partial(_combine_kernel, last_layer=last_layer),
        grid=(nt,),
        in_specs=[
            pl.BlockSpec(memory_space=pl.ANY),
            pl.BlockSpec(memory_space=pl.ANY),
            row,
            pl.BlockSpec((tm, LANES), lambda i: (i, 0)),
            pl.BlockSpec((1, d), lambda i: (0, 0)),
        ],
        out_specs=out_specs,
        out_shape=out_shape,
        scratch_shapes=[
            pltpu.SMEM((2, 1, 2 * tm), I32),
            pltpu.VMEM((2, tm, d), F32),
            pltpu.VMEM((2, tm, d), F32),
            pltpu.SemaphoreType.DMA((2,)),
            pltpu.SemaphoreType.DMA((2,)),
            pltpu.SemaphoreType.DMA((2,)),
        ],
        compiler_params=_params("arbitrary"),
        name="moe_combine",
    )(pos_tiles, eo, x, rinfo, g)


def _moe(x1, hn, rinfo, rinfo_t, counts, wg, wu, wd, layer, next_gain, last_layer):
    n = x1.shape[0]
    ne = wg.shape[1]
    tr = EXPERT_ROW_TILE
    tm = TOKEN_TILE
    nv = (2 * n) // tr + ne - 1
    eid = rinfo_t[0:2].astype(I32)
    rank = rinfo_t[4:6].astype(I32)
    cnt = counts[0, :ne].astype(I32)
    seg_end = jnp.cumsum(cnt)
    seg_start = seg_end - cnt
    experts = jnp.arange(ne, dtype=I32)
    pos = rank + jnp.sum(jnp.where(eid[:, None, :] == experts[None, :, None],
                                   seg_start[None, :, None], 0), axis=1)
    pos_tiles = pos.reshape(2, n // tm, tm).transpose(1, 0, 2).reshape(n // tm, 1, 2 * tm)

    first_tile = seg_start // tr
    tiles_e = jnp.where(cnt > 0, (seg_end - 1) // tr - first_tile + 1, 0)
    v_end = jnp.cumsum(tiles_e)
    v_start = v_end - tiles_e
    total = v_end[-1]
    vis = jnp.arange(nv, dtype=I32)
    vc = jnp.minimum(vis, jnp.maximum(total - 1, 0))
    ve = jnp.sum((vc[:, None] >= v_end[None, :]).astype(I32), axis=1)
    pick = lambda tab: jnp.sum(jnp.where(ve[:, None] == experts, tab, 0), axis=1)
    vt = pick(first_tile) + (vc - pick(v_start))
    von = (vis < total).astype(I32)
    present = cnt > 0
    later = present[None, :] & (experts[None, :] > experts[:, None])
    next_of = jnp.min(jnp.where(later, experts[None, :], ne), axis=1)
    vnext = pick(jnp.where(next_of < ne, next_of, -1))
    vslot = pick(jnp.cumsum(present.astype(I32)) - 1) & 1

    xs = _dispatch(pos_tiles, hn)
    eo = _moe_experts(vt, ve, von, vnext, vslot, seg_start, seg_end, xs, wg, wu, wd, layer)
    return _combine(pos_tiles, eo, x1, rinfo, next_gain, last_layer)


def _cis(log_mag, ang):
    mag = jnp.exp(log_mag)
    return mag * jnp.cos(ang), mag * jnp.sin(ang)


def _split3(a):
    p1 = a.astype(BF16)
    r1 = a - p1.astype(F32)
    p2 = r1.astype(BF16)
    return p1, p2, (r1 - p2.astype(F32)).astype(BF16)


def _select_dot(a, sel, sel_first=False):
    sel = sel.astype(BF16)
    parts = [(_dot(sel, p) if sel_first else _dot(p, sel)) for p in _split3(a)]
    return parts[0] + (parts[1] + parts[2])


def _dot3(a, b):
    a_hi, a_lo = _split_bf16(a)
    b_hi, b_lo = _split_bf16(b)
    return _dot(a_hi, b_hi) + (_dot(a_lo, b_hi) + _dot(a_hi, b_lo))


def _s5_prep_kernel(lsc_ref, lsr_ref, lrc_ref, lic_ref, lrr_ref, lir_ref, ctre_ref, ctim_ref,
                    btre_ref, btim_ref, d_ref,
                    t_ref, wre_ref, wim_ref, cyre_ref, cyim_ref, apr_ref, api_ref,
                    bbr_ref, bbi_ref, lbr_ref, lbi_ref, *, gw, gpb, sub):
    blk = gpb * gw
    wide = sub * blk
    nsl = lrc_ref.shape[1]
    ns = nsl // gpb
    sh_gw, sh_ns = gw.bit_length() - 1, ns.bit_length() - 1
    dt_c, dt_r = jnp.exp(lsc_ref[0]), jnp.exp(lsr_ref[0])
    ldt_c_re, ldt_c_im = lrc_ref[0] * dt_c, lic_ref[0] * dt_c
    ldt_r_re, ldt_r_im = lrr_ref[0] * dt_r, lir_ref[0] * dt_r

    lane_w = lax.broadcasted_iota(I32, (1, wide), 1)
    spread = jnp.where((lax.broadcasted_iota(I32, (gw, wide), 1) & (gw - 1))
                       == lax.broadcasted_iota(I32, (gw, wide), 0), 1.0, 0.0)
    cre = _select_dot(ctre_ref[0], spread)
    cim = _select_dot(ctim_ref[0], spread)
    same = (jnp.right_shift(lax.broadcasted_iota(I32, (nsl, 1), 0), sh_ns)
            == (jnp.right_shift(lane_w, sh_gw) & (gpb - 1)))
    assert 2 * blk == LANES
    tau = lax.broadcasted_iota(I32, (1, LANES), 1).astype(F32)
    pw_re, pw_im = _cis(tau * ldt_c_re, tau * ldt_c_im)
    low = lax.broadcasted_iota(I32, (nsl, LANES), 1) < blk

    def spread_pow(p, first):
        col = lambda t: jnp.broadcast_to(p[:, t:t + 1], (nsl, LANES))
        return jnp.concatenate([jnp.where(low, col(first + 2 * m), col(first + 2 * m + 1))
                                for m in range(sub // 2)], axis=1)

    def c_lam_pow(first):
        pr, pi = spread_pow(pw_re, first), spread_pow(pw_im, first)
        return (jnp.where(same, pr * cre - pi * cim, 0.0),
                jnp.where(same, -(pr * cim + pi * cre), 0.0))

    clr0, cli0 = c_lam_pow(0)
    clr1, cli1 = c_lam_pow(1)
    cyre_ref[0] = clr1.astype(BF16)
    cyim_ref[0] = cli1.astype(BF16)

    lbr, lbi = _cis(ldt_r_re, ldt_r_im)
    lbr_ref[0] = lbr
    lbi_ref[0] = lbi
    lr, li = lrr_ref[0], lir_ref[0]
    nr, ni = lbr - 1.0, lbi
    den = lr * lr + li * li
    fr = (nr * lr + ni * li) / den
    fi = (ni * lr - nr * li) / den
    bre, bim = btre_ref[0], btim_ref[0]
    bbr = fr * bre - fi * bim
    bbi = fr * bim + fi * bre
    bbr_ref[0] = bbr
    bbi_ref[0] = bbi

    lane_t = lax.broadcasted_iota(I32, (gw, wide), 1)
    chan = lax.broadcasted_iota(I32, (gw, wide), 0)
    for g in range(gpb):
        ps = slice(g * ns, (g + 1) * ns)
        r = _dot3(bbr[:, ps], clr0[ps, :]) + _dot3(bbi[:, ps], cli0[ps, :])
        r = r + jnp.where(lane_t == g * gw + chan, d_ref[0][:, g:g + 1], 0.0)
        for j in range(sub):
            tb = r if j == 0 else jnp.where(lane_t >= blk * j, pltpu.roll(r, blk * j, 1), 0.0)
            r0 = j * blk + g * gw
            t_ref[0, r0:r0 + gw, :] = tb.astype(BF16)

    rows = lax.broadcasted_iota(I32, (wide, 1), 0)
    spread_t = jnp.where((lax.broadcasted_iota(I32, (wide, gw), 0) & (gw - 1))
                         == lax.broadcasted_iota(I32, (wide, gw), 1), 1.0, 0.0)
    bbr_t = _select_dot(bbr, spread_t, sel_first=True)
    bbi_t = _select_dot(bbi, spread_t, sel_first=True)
    same_w = ((jnp.right_shift(rows, sh_gw) & (gpb - 1))
              == jnp.right_shift(lax.broadcasted_iota(I32, (1, nsl), 1), sh_ns))
    rj = ((sub - 1) - lax.broadcasted_iota(I32, (sub, 1), 0)).astype(F32)
    q_re, q_im = _cis(rj * ldt_r_re, rj * ldt_r_im)
    per_step = lambda q: jnp.concatenate(
        [jnp.broadcast_to(q[j:j + 1, :], (blk, nsl)) for j in range(sub)], axis=0)
    pr, pi = per_step(q_re), per_step(q_im)
    wre_ref[0] = jnp.where(same_w, pr * bbr_t - pi * bbi_t, 0.0).astype(BF16)
    wim_ref[0] = jnp.where(same_w, pr * bbi_t + pi * bbr_t, 0.0).astype(BF16)

    nlev = apr_ref.shape[1]
    pw = (sub * jnp.left_shift(1, lax.broadcasted_iota(I32, (nlev, 1), 0))).astype(F32)
    apr, api = _cis(pw * ldt_r_re, pw * ldt_r_im)
    apr_ref[0] = apr
    api_ref[0] = api


def _s5_prep(log_step, lam_re, lam_im, b_re, b_im, c_re, c_im, d_skip):
    ng, ns = lam_re.shape
    gw = b_re.shape[2]
    sub, gpb = S5_SUB, S5_GROUPS_PER_BLOCK
    nblk = ng // gpb
    nsl = gpb * ns
    wide = sub * gpb * gw
    ls = jnp.repeat(log_step, ns)
    ct = lambda c: jnp.transpose(c, (0, 2, 1)).reshape(nblk, nsl, gw)
    bt = lambda b: jnp.transpose(b.reshape(nblk, gpb, ns, gw), (0, 3, 1, 2)).reshape(nblk, gw, nsl)
    ins = [ls.reshape(nblk, nsl, 1), ls.reshape(nblk, 1, nsl),
           lam_re.reshape(nblk, nsl, 1), lam_im.reshape(nblk, nsl, 1),
           lam_re.reshape(nblk, 1, nsl), lam_im.reshape(nblk, 1, nsl),
           ct(c_re), ct(c_im), bt(b_re), bt(b_im),
           jnp.transpose(d_skip.reshape(nblk, gpb, gw), (0, 2, 1))]
    spec = lambda a: pl.BlockSpec((1,) + a.shape[1:], lambda g: (g, 0, 0))
    outs = [
        jax.ShapeDtypeStruct((nblk, wide, wide), BF16),
        jax.ShapeDtypeStruct((nblk, wide, nsl), BF16),
        jax.ShapeDtypeStruct((nblk, wide, nsl), BF16),
        jax.ShapeDtypeStruct((nblk, nsl, wide), BF16),
        jax.ShapeDtypeStruct((nblk, nsl, wide), BF16),
        jax.ShapeDtypeStruct((nblk, SUBLANES, nsl), F32),
        jax.ShapeDtypeStruct((nblk, SUBLANES, nsl), F32),
        jax.ShapeDtypeStruct((nblk, gw, nsl), F32),
        jax.ShapeDtypeStruct((nblk, gw, nsl), F32),
        jax.ShapeDtypeStruct((nblk, 1, nsl), F32),
        jax.ShapeDtypeStruct((nblk, 1, nsl), F32),
    ]
    return pl.pallas_call(
        functools.partial(_s5_prep_kernel, gw=gw, gpb=gpb, sub=sub),
        grid=(nblk,),
        in_specs=[spec(a) for a in ins],
        out_specs=[spec(o) for o in outs],
        out_shape=outs,
        compiler_params=_params("parallel"),
        name="s5_prep",
    )(*ins)


def _s5_prompt_kernel(x_ref, t_ref, wre_ref, wim_ref, cyre_ref, cyim_ref, apr_ref, api_ref,
                      y_ref, hre_out, him_out, u_s, y4_s, *, nb, seq, sub):
    nk = seq // sub
    rows = nb * nk
    hl = LANES // 2
    nlev = nk.bit_length() - 1
    nsl = wre_ref.shape[2]
    low = lax.broadcasted_iota(I32, (nk, LANES), 1) < hl
    kidx = lax.broadcasted_iota(I32, (rows, 1), 0) & (nk - 1)

    def step_rows(b, j):
        return (pl.ds(b * seq + j, nk, stride=sub), slice(None))

    for b in range(nb):
        rs = slice(b * nk, (b + 1) * nk)
        for m in range(sub // 2):
            ls = slice(m * LANES, (m + 1) * LANES)
            s0 = x_ref[step_rows(b, 2 * m)]
            s1 = x_ref[step_rows(b, 2 * m + 1)]
            u_s[0, rs, ls] = jnp.where(low, s0, pltpu.roll(s1, hl, 1)).astype(BF16)
            u_s[1, rs, ls] = jnp.where(low, pltpu.roll(s0, hl, 1), s1).astype(BF16)

    def shifted(a, s):
        return jnp.where(kidx >= s, pltpu.roll(a, s, 0), 0.0)

    halves = range(2)
    us = [u_s[h] for h in halves]
    hre = [_dot(us[h], wre_ref[h]) for h in halves]
    him = [_dot(us[h], wim_ref[h]) for h in halves]
    cw = 2 * LANES
    y_conv = [jnp.concatenate(
        [_dot(us[h][:, 0:c0 + cw], t_ref[h, 0:c0 + cw, c0:c0 + cw]) for c0 in range(0, us[h].shape[1], cw)],
        axis=1) for h in halves]
    for i in range(nlev):
        for h in halves:
            ar = apr_ref[h, i:i + 1, :]
            ai = api_ref[h, i:i + 1, :]
            pre, pim = shifted(hre[h], 1 << i), shifted(him[h], 1 << i)
            hre[h], him[h] = hre[h] + (ar * pre - ai * pim), him[h] + (ar * pim + ai * pre)
    for h in halves:
        ls = slice(h * nsl, (h + 1) * nsl)
        for b in range(nb):
            last = (b + 1) * nk - 1
            hre_out[b:b + 1, ls] = hre[h][last:last + 1, :]
            him_out[b:b + 1, ls] = him[h][last:last + 1, :]
        hpre = shifted(hre[h], 1).astype(BF16)
        hpim = shifted(him[h], 1).astype(BF16)
        y4_s[h] = y_conv[h] + _dot(hpre, cyre_ref[h]) + _dot(hpim, cyim_ref[h])

    for b in range(nb):
        rs = slice(b * nk, (b + 1) * nk)
        for m in range(sub // 2):
            ls = slice(m * LANES, (m + 1) * LANES)
            ca = y4_s[0, rs, ls]
            cb = y4_s[1, rs, ls]
            y_ref[step_rows(b, 2 * m)] = jnp.where(low, ca, pltpu.roll(cb, hl, 1))
            y_ref[step_rows(b, 2 * m + 1)] = jnp.where(low, pltpu.roll(ca, hl, 1), cb)


def _s5_prompt(h, prep, batch, seq):
    d = h.shape[1]
    tmat, wre, wim, cyre, cyim, apr, api = prep[:7]
    nblk, wide, nsl = wre.shape
    sub, nb = S5_SUB, S5_BATCH_PER_STEP
    nk = seq // sub
    assert nk & (nk - 1) == 0 and nk.bit_length() - 1 <= apr.shape[1]
    ntile = d // LANES
    assert nblk == 2 * ntile
    rows = nb * nk
    wspec = lambda a: pl.BlockSpec((2,) + a.shape[1:], lambda t, b: (t, 0, 0))
    xspec = pl.BlockSpec((nb * seq, LANES), lambda t, b: (b, t))
    sspec = pl.BlockSpec((None, nb, 2 * nsl), lambda t, b: (b, 0, t))
    return pl.pallas_call(
        functools.partial(_s5_prompt_kernel, nb=nb, seq=seq, sub=sub),
        grid=(ntile, batch // nb),
        in_specs=[xspec, wspec(tmat), wspec(wre), wspec(wim), wspec(cyre), wspec(cyim),
                  wspec(apr), wspec(api)],
        out_specs=[xspec, sspec, sspec],
        out_shape=[
            jax.ShapeDtypeStruct((batch * seq, d), F32),
            jax.ShapeDtypeStruct((batch // nb, nb, ntile * 2 * nsl), F32),
            jax.ShapeDtypeStruct((batch // nb, nb, ntile * 2 * nsl), F32),
        ],
        scratch_shapes=[
            pltpu.VMEM((2, rows, wide), BF16),
            pltpu.VMEM((2, rows, wide), F32),
        ],
        compiler_params=_params("parallel", "parallel"),
        name="s5_prompt",
    )(h, tmat, wre, wim, cyre, cyim, apr, api)


def _s5_step_kernel(u_ref, h_ref, hsw_ref, bb_ref, la_ref, lb_ref, cc_ref, d_ref, hn_ref, y_ref):
    for g in range(u_ref.shape[0]):
        u = u_ref[g]
        hn = la_ref[g] * h_ref[g] + lb_ref[g] * hsw_ref[g] + _dot3(u, bb_ref[g])
        hn_ref[g] = hn
        y_ref[g] = _dot3(hn, cc_ref[g]) + d_ref[g] * u


def _s5_step(u_p, h_cat, h_swp, bb_cat, la, lb, cc, d_p):
    ng = u_p.shape[0]
    gb = SUBLANES
    spec = lambda a: pl.BlockSpec((gb,) + a.shape[1:], lambda g: (g, 0, 0))
    ins = [u_p, h_cat, h_swp, bb_cat, la, lb, cc, d_p]
    outs = [jax.ShapeDtypeStruct(h_cat.shape, F32), jax.ShapeDtypeStruct(u_p.shape, F32)]
    return pl.pallas_call(
        _s5_step_kernel,
        grid=(ng // gb,),
        in_specs=[spec(a) for a in ins],
        out_specs=[spec(o) for o in outs],
        out_shape=outs,
        compiler_params=_params("parallel"),
        name="s5_step",
    )(*ins)


def _pad_to(a, axis, size):
    pad = [(0, 0)] * a.ndim
    pad[axis] = (0, size - a.shape[axis])
    return jnp.pad(a, pad)


def kernel(x_prompt, x_sample, state_pool, state_mlstm_c, state_mlstm_n, state_mlstm_m, state_s5_re, state_s5_im, norm_mix, norm_ffn, norm_final, w_in_ab, b_gates, pool_w, pool_scale, w_out_ab, s5_lam_re, s5_lam_im, s5_log_step, s5_b_re, s5_b_im, s5_c_re, s5_c_im, s5_d, w_glu, moe_w_group, moe_b_group, moe_w_expert, moe_b_expert, moe_w_gate, moe_w_up, moe_w_down):
    bp, tp, d = x_prompt.shape
    bs = x_sample.shape[0]
    n_p = bp * tp
    tm = TOKEN_TILE
    assert n_p % tm == 0 and bs <= tm
    nh, dh = state_mlstm_c.shape[2], state_mlstm_c.shape[3]
    pool_width = state_pool.shape[3]
    ml_width = nh * dh
    n_main = pool_width + 4 * ml_width
    n_gates = 2 * nh
    ngrp_s5, n_state = s5_lam_re.shape[1], s5_lam_re.shape[2]
    gw_s5 = d // ngrp_s5

    n = n_p + tm

    def tail_tile(sample_rows, dtype):
        return _pad_to(sample_rows.astype(dtype), 0, tm)

    x_main = x_prompt.reshape(n_p, d)
    x_tail = tail_tile(x_sample.reshape(bs, d), F32)
    tril = jnp.tril(jnp.ones((tm, tm), BF16), -1)

    def router_weights(l):
        wr = _pad_to(jnp.concatenate([moe_w_group[l], moe_w_expert[l]], axis=1), 1, LANES)
        br = jnp.concatenate([moe_b_group[l], moe_b_expert[l]])[None, :]
        hi = wr.astype(BF16)
        lo = (wr - hi.astype(F32)).astype(BF16)
        return jnp.concatenate([hi, lo], axis=1), _pad_to(br, 1, LANES)

    w_in = w_in_ab[0]
    z, gates, gates_t = _inproj(
        x_main, x_tail, norm_mix[0][None, :], _pad_to(w_in, 1, n_main + LANES).astype(BF16),
        _pad_to(b_gates[0][None, :], 1, LANES), n_gates)

    pw = pool_w[0].astype(BF16)
    ps = pool_scale[0][None, :]
    pool_y_p, pool_p = _pool_prompt(z, pw, ps, bp, tp)
    pool_y_s, pool_s_t = _pool_step(z, jnp.transpose(state_pool[0], (1, 0, 2)), pw, ps, n_p)
    pool_s = jnp.transpose(pool_s_t, (1, 0, 2))

    ml_y_p, c_p, n_p_st, m_p = _mlstm_prompt(z, gates_t, bp, tp, nh, dh)
    ml_y_p = ml_y_p.reshape(n_p, ml_width)
    g_s = gates[n_p:n_p + bs, :n_gates]
    ml_y_s, c_s, n_s_st, m_s = _mlstm_step(
        z[n_p:n_p + bs].reshape(bs, 1, n_main),
        g_s[:, :nh].reshape(bs, nh, 1, 1), g_s[:, nh:].reshape(bs, nh, 1, 1),
        state_mlstm_c[0], state_mlstm_n[0].reshape(bs, nh, 1, dh),
        state_mlstm_m[0].reshape(bs, nh, 1, 1), nh, dh)

    rows = [x_main, x_tail, pool_y_p, tail_tile(pool_y_s, BF16),
            ml_y_p, tail_tile(ml_y_s.reshape(bs, ml_width), BF16)]
    specs = (_stacked_specs(x_main, x_tail) + _stacked_specs(pool_y_p, rows[3])
             + _stacked_specs(ml_y_p, rows[5]))
    wr, br = router_weights(0)
    x1, hn, rinfo, rinfo_t, counts = _mix_route(
        _outproj_route_kernel, "outproj_route", n, specs, rows, w_out_ab[0].astype(BF16),
        norm_ffn[0][None, :], wr, br, tril)
    x2, h1 = _moe(x1, hn, rinfo, rinfo_t, counts, moe_w_gate, moe_w_up, moe_w_down, 0,
                  norm_mix[1][None, :], last_layer=False)

    prep = _s5_prep(s5_log_step[0], s5_lam_re[0], s5_lam_im[0], s5_b_re[0], s5_b_im[0],
                    s5_c_re[0], s5_c_im[0], s5_d[0])
    y_p, hre_p, him_p = _s5_prompt(h1, prep, bp, tp)
    s5_re_p = hre_p.reshape(bp, ngrp_s5, n_state)
    s5_im_p = him_p.reshape(bp, ngrp_s5, n_state)

    gpb = S5_GROUPS_PER_BLOCK
    per_group = lambda a: jnp.transpose(
        a.reshape(ngrp_s5 // gpb, gw_s5, gpb, n_state), (0, 2, 1, 3)).reshape(ngrp_s5, gw_s5, n_state)
    bbr, bbi = per_group(prep[7]), per_group(prep[8])
    lbr, lbi = prep[9].reshape(ngrp_s5, 1, n_state), prep[10].reshape(ngrp_s5, 1, n_state)
    d_g = s5_d[0].reshape(ngrp_s5, 1, gw_s5)
    u_s = h1[n_p:n_p + bs].reshape(bs, ngrp_s5, gw_s5).transpose(1, 0, 2)
    h_re = jnp.transpose(state_s5_re[0], (1, 0, 2))
    h_im = jnp.transpose(state_s5_im[0], (1, 0, 2))
    cc = jnp.concatenate([jnp.transpose(s5_c_re[0], (0, 2, 1)),
                          -jnp.transpose(s5_c_im[0], (0, 2, 1))], axis=1)
    hn_s, y_s = _s5_step(
        _pad_to(u_s, 2, LANES),
        jnp.concatenate([h_re, h_im], axis=2), jnp.concatenate([h_im, h_re], axis=2),
        _pad_to(jnp.concatenate([bbr, bbi], axis=2), 1, LANES),
        jnp.concatenate([lbr, lbr], axis=2), jnp.concatenate([-lbi, lbi], axis=2),
        _pad_to(cc, 2, LANES), _pad_to(d_g, 2, LANES))
    s5_re_s = jnp.transpose(hn_s[:, :, :n_state], (1, 0, 2))
    s5_im_s = jnp.transpose(hn_s[:, :, n_state:], (1, 0, 2))
    y_tail = tail_tile(jnp.transpose(y_s[:, :, :gw_s5], (1, 0, 2)).reshape(bs, d), F32)

    wr, br = router_weights(1)
    specs = [pl.BlockSpec((tm, d), lambda i: (i, 0))] + _stacked_specs(y_p, y_tail)
    x3, hn, rinfo, rinfo_t, counts = _mix_route(
        _glu_route_kernel, "glu_route", n, specs, [x2, y_p, y_tail], w_glu[0].astype(BF16),
        norm_ffn[1][None, :], wr, br, tril)
    y_main, y_last = _moe(x3, hn, rinfo, rinfo_t, counts, moe_w_gate, moe_w_up, moe_w_down, 1,
                          norm_final[None, :], last_layer=True)

    return (y_main.reshape(bp, tp, d), y_last[:bs].reshape(bs, 1, d),
            pool_p[None], c_p[None], n_p_st.reshape(1, bp, nh, dh), m_p[:, :, 0, 0][None],
            s5_re_p[None], s5_im_p[None],
            pool_s[None], c_s[None], n_s_st.reshape(1, bs, nh, dh), m_s.reshape(1, bs, nh),
            s5_re_s[None], s5_im_s[None])
```

```python
import functools

import jax
import jax.numpy as jnp
from jax import lax
from jax.experimental import pallas as pl
from jax.experimental.pallas import tpu as pltpu

F32 = jnp.float32
BF16 = jnp.bfloat16
I32 = jnp.int32

PAST_LEN = 16384
POOL_WINDOWS = (2, 4, 8, 16)
POOL_BUF = max(POOL_WINDOWS) - 1
MLSTM_CHUNK = 128
S5_SUB = 16
MOE_GROUPS = 4
MOE_EXPERTS_PER_GROUP = 8
RMS_EPS = 1e-6

LANES = 128
SUBLANES = 8
VMEM_LIMIT_BYTES = 56 * 1024 * 1024

TOKEN_TILE = 512
EXPERT_ROW_TILE = 512
MLSTM_SEQ_PER_STEP = 8
MLSTM_STEP_BATCH = 16
POOL_TIME_TILE = 1024
DMA_ISSUE_UNROLL = 8
S5_GROUPS_PER_BLOCK = 4
S5_BATCH_PER_STEP = 4


def _params(*sem):
    return pltpu.CompilerParams(dimension_semantics=sem, vmem_limit_bytes=VMEM_LIMIT_BYTES)


def _rms(x, g):
    return x * lax.rsqrt(jnp.mean(x * x, axis=-1, keepdims=True) + RMS_EPS) * g


def _dot(a, b):
    return jnp.dot(a, b, preferred_element_type=F32)


def _dot_nt(a, b):
    return lax.dot_general(a, b, (((1,), (1,)), ((), ())), preferred_element_type=F32)


def _dot_tn(a, b):
    return lax.dot_general(a, b, (((0,), (0,)), ((), ())), preferred_element_type=F32)


def _stacked(main_ref, tail_ref):
    last = pl.program_id(0) == pl.num_programs(0) - 1
    return jnp.where(last, tail_ref[...], main_ref[...])


def _stacked_specs(main, tail):
    tm, w = tail.shape
    last_main = main.shape[0] // tm - 1
    return [pl.BlockSpec((tm, w), lambda i: (jnp.minimum(i, last_main), 0)),
            pl.BlockSpec((tm, w), lambda i: (0, 0))]


def _inproj_kernel(xm_ref, xt_ref, g_ref, w_ref, bg_ref, z_ref, gates_ref, gatest_ref):
    h = _rms(_stacked(xm_ref, xt_ref), g_ref[...]).astype(BF16)
    zg = _dot(h, w_ref[...])
    nz = z_ref.shape[1]
    z_ref[...] = zg[:, 0:nz]
    gates = zg[:, nz:nz + LANES] + bg_ref[...]
    gates_ref[...] = gates
    gatest_ref[...] = gates.T[0:gatest_ref.shape[0], :]


def _inproj(x_main, x_tail, g, w, bg, ng):
    d = x_main.shape[1]
    tm = TOKEN_TILE
    n = x_main.shape[0] + tm
    nz = w.shape[1] - LANES
    full = lambda i: (0, 0)
    return pl.pallas_call(
        _inproj_kernel,
        grid=(n // tm,),
        in_specs=_stacked_specs(x_main, x_tail) + [
            pl.BlockSpec((1, d), full),
            pl.BlockSpec((d, nz + LANES), full),
            pl.BlockSpec((1, LANES), full),
        ],
        out_specs=[
            pl.BlockSpec((tm, nz), lambda i: (i, 0)),
            pl.BlockSpec((tm, LANES), lambda i: (i, 0)),
            pl.BlockSpec((ng, tm), lambda i: (0, i)),
        ],
        out_shape=[
            jax.ShapeDtypeStruct((n, nz), F32),
            jax.ShapeDtypeStruct((n, LANES), F32),
            jax.ShapeDtypeStruct((ng, n), F32),
        ],
        compiler_params=_params("parallel"),
        name="inproj",
    )(x_main, x_tail, g, w, bg)


def _pool_prompt_kernel(u_ref, pw_ref, ps_ref, y_ref, st_ref, ext_ref, *, tt, gw):
    t = pl.program_id(1)
    nt = pl.num_programs(1)
    halo = POOL_BUF + 1
    width = ext_ref.shape[1]

    @pl.when(t == 0)
    def _():
        ext_ref[0:halo, :] = jnp.zeros((halo, width), F32)

    u = u_ref[...]
    ext_ref[halo:halo + tt, :] = u
    pos = t * tt + lax.broadcasted_iota(I32, (tt, 1), 0)
    for g, w in enumerate(POOL_WINDOWS):
        c0 = g * gw
        acc = u[:, c0:c0 + gw]
        for j in range(1, w):
            acc = acc + ext_ref[halo - j:halo - j + tt, c0:c0 + gw]
        cnt = jnp.minimum(w, pos + 1).astype(F32)
        d = acc / cnt - u[:, c0:c0 + gw]
        y = _dot(d.astype(BF16), pw_ref[g]) * ps_ref[:, c0:c0 + gw]
        y_ref[:, c0:c0 + gw] = y.astype(BF16)

    @pl.when(t == nt - 1)
    def _():
        st_ref[...] = ext_ref[tt + 1:tt + halo, :]

    ext_ref[0:halo, :] = ext_ref[tt:tt + halo, :]


def _pool_prompt(z, pw, ps, batch, seq):
    width = ps.shape[1]
    gw = width // len(POOL_WINDOWS)
    tt = POOL_TIME_TILE
    nt = seq // tt
    return pl.pallas_call(
        functools.partial(_pool_prompt_kernel, tt=tt, gw=gw),
        grid=(batch, nt),
        in_specs=[
            pl.BlockSpec((tt, width), lambda b, t: (b * nt + t, 0)),
            pl.BlockSpec(pw.shape, lambda b, t: (0, 0, 0)),
            pl.BlockSpec((1, width), lambda b, t: (0, 0)),
        ],
        out_specs=[
            pl.BlockSpec((tt, width), lambda b, t: (b * nt + t, 0)),
            pl.BlockSpec((None, POOL_BUF, width), lambda b, t: (b, 0, 0)),
        ],
        out_shape=[
            jax.ShapeDtypeStruct((batch * seq, width), BF16),
            jax.ShapeDtypeStruct((batch, POOL_BUF, width), F32),
        ],
        scratch_shapes=[pltpu.VMEM((POOL_BUF + 1 + tt, width), F32)],
        compiler_params=_params("parallel", "arbitrary"),
        name="pool_prompt",
    )(z, pw, ps)


def _pool_step_kernel(u_ref, buf_ref, pw_ref, ps_ref, y_ref, nb_ref, *, gw):
    u = u_ref[...]
    for g, w in enumerate(POOL_WINDOWS):
        c0 = g * gw
        acc = u[:, c0:c0 + gw]
        for j in range(1, w):
            acc = acc + buf_ref[POOL_BUF - j, :, c0:c0 + gw]
        cnt = float(min(w, PAST_LEN + 1))
        d = acc / cnt - u[:, c0:c0 + gw]
        y = _dot(d.astype(BF16), pw_ref[g]) * ps_ref[:, c0:c0 + gw]
        y_ref[:, c0:c0 + gw] = y.astype(BF16)
    nb_ref[0:POOL_BUF - 1] = buf_ref[1:POOL_BUF]
    nb_ref[POOL_BUF - 1] = u


def _pool_step(z, buf_t, pw, ps, row0):
    _, batch, width = buf_t.shape
    gw = width // len(POOL_WINDOWS)
    return pl.pallas_call(
        functools.partial(_pool_step_kernel, gw=gw),
        grid=(1,),
        in_specs=[
            pl.BlockSpec((batch, width), lambda i: (row0 // batch, 0)),
            pl.BlockSpec(buf_t.shape, lambda i: (0, 0, 0)),
            pl.BlockSpec(pw.shape, lambda i: (0, 0, 0)),
            pl.BlockSpec((1, width), lambda i: (0, 0)),
        ],
        out_specs=[
            pl.BlockSpec((batch, width), lambda i: (0, 0)),
            pl.BlockSpec(buf_t.shape, lambda i: (0, 0, 0)),
        ],
        out_shape=[
            jax.ShapeDtypeStruct((batch, width), BF16),
            jax.ShapeDtypeStruct(buf_t.shape, F32),
        ],
        compiler_params=_params("arbitrary"),
        name="pool_step",
    )(z, buf_t, pw, ps)


def _mlstm_prompt_kernel(*refs, nh, dh, nseq):
    seq_in = [refs[5 * s:5 * s + 5] for s in range(nseq)]
    h_ref, c_out, n_out, m_out, c_s, n_s, m_s = refs[5 * nseq:]
    ci = pl.program_id(1)
    nc = pl.num_programs(1)
    ln = seq_in[0][0].shape[0]

    @pl.when(ci == 0)
    def _():
        c_s[...] = jnp.zeros(c_s.shape, F32)
        n_s[...] = jnp.zeros(n_s.shape, F32)
        m_s[...] = jnp.zeros(m_s.shape, F32)

    row = lax.broadcasted_iota(I32, (ln, ln), 0)
    col = lax.broadcasted_iota(I32, (ln, ln), 1)
    causal_t = row <= col
    eye = col == row
    lane = lax.broadcasted_iota(I32, (nh, ln), 1)

    def to_col(r):
        return jnp.sum(jnp.where(eye, r, 0.0), axis=1, keepdims=True)

    scale = dh ** -0.5
    pairs = [(s, h) for s in range(nseq) for h in range(nh)]
    c_old = {p: c_s[p[0], p[1]] for p in pairs}
    n_old = {p: n_s[p[0], p[1]] for p in pairs}
    m_old = {p: m_s[p[0], p[1]][:, 0:1] for p in pairs}
    gates = []
    for s in range(nseq):
        gt = seq_in[s][4][...]
        bc_all = jax.nn.log_sigmoid(gt[nh:2 * nh])
        sh = 1
        while sh < ln:
            bc_all = bc_all + jnp.where(lane >= sh, pltpu.roll(bc_all, sh, 1), 0.0)
            sh *= 2
        gates.append((gt[0:nh], bc_all))
    ch = {}
    for s, h in pairs:
        q_ref, k_ref, v_ref, _, _ = seq_in[s]
        li_all, bc_all = gates[s]
        sl = slice(h * dh, (h + 1) * dh)
        k = k_ref[:, sl] * scale
        v = v_ref[:, sl]
        li_r, bc_r = li_all[h:h + 1], bc_all[h:h + 1]
        b_end = bc_r[:, ln - 1:ln]
        g_r = b_end - bc_r + li_r
        m0 = m_old[s, h]
        m_new = jnp.maximum(b_end + m0, jnp.max(g_r, axis=1, keepdims=True))
        ch[s, h] = dict(
            qb=q_ref[:, sl].astype(BF16), kb=k.astype(BF16), vb=v.astype(BF16), v=v,
            inter=bc_r + m0, m_new=m_new, wg_r=jnp.exp(g_r - m_new),
            decay=jnp.exp(b_end + m0 - m_new),
            dmat_t=jnp.where(causal_t, bc_r - to_col(bc_r - li_r), -jnp.inf))
    for p in pairs:
        d = ch[p]
        d["st"] = _dot_nt(d["kb"], d["qb"])
        d["cq_t"] = _dot_nt(c_old[p].astype(BF16), d["qb"])
        d["nq"] = _dot_nt(jnp.broadcast_to(n_old[p], (SUBLANES, dh)).astype(BF16), d["qb"])[0:1]
        d["c_add"] = _dot((d["v"].T * d["wg_r"]).astype(BF16), d["kb"])
        d["n_add"] = _dot(jnp.broadcast_to(d["wg_r"], (SUBLANES, ln)).astype(BF16), d["kb"])[0:1]
    for p in pairs:
        d = ch[p]
        d["m_row"] = jnp.maximum(d["inter"], jnp.max(d["dmat_t"], axis=0, keepdims=True))
        d["w_inter"] = jnp.exp(d["inter"] - d["m_row"])
        d["sc_t"] = d["st"] * jnp.exp(d["dmat_t"] - d["m_row"])
    for p in pairs:
        d = ch[p]
        d["pv_t"] = _dot_tn(d["vb"], d["sc_t"].astype(BF16))
    new_state = {}
    for s, h in pairs:
        d = ch[s, h]
        sl = slice(h * dh, (h + 1) * dh)
        num_t = d["pv_t"] + d["w_inter"] * d["cq_t"]
        den = jnp.sum(d["sc_t"], axis=0, keepdims=True) + d["w_inter"] * d["nq"]
        hh = (num_t / jnp.maximum(jnp.abs(den), jnp.exp(-d["m_row"]))).T
        h_ref[s, :, sl] = (hh * jax.nn.sigmoid(seq_in[s][3][:, sl])).astype(BF16)
        new_state[s, h] = (d["decay"] * c_old[s, h] + d["c_add"],
                           d["decay"] * n_old[s, h] + d["n_add"],
                           jnp.broadcast_to(d["m_new"], (1, dh)))

    for (s, h), (c_new, n_new, m_new) in new_state.items():
        c_s[s, h] = c_new
        n_s[s, h] = n_new
        m_s[s, h] = m_new

    @pl.when(ci == nc - 1)
    def _():
        c_out[...] = c_s[...]
        n_out[...] = n_s[...]
        m_out[...] = m_s[...]


def _mlstm_prompt(z, gates_t, batch, seq, nh, dh):
    ln = MLSTM_CHUNK
    nc = seq // ln
    nseq = min(MLSTM_SEQ_PER_STEP, batch)
    assert batch % nseq == 0
    width = nh * dh
    ng = gates_t.shape[0]
    rows_of = lambda s: (lambda b, c: (b * nseq + s) * nc + c)
    in_specs, operands = [], []
    for s in range(nseq):
        r = rows_of(s)
        for j in (1, 2, 3, 4):
            in_specs.append(pl.BlockSpec((ln, width), lambda b, c, r=r, j=j: (r(b, c), j)))
        in_specs.append(pl.BlockSpec((ng, ln), lambda b, c, r=r: (0, r(b, c))))
        operands += [z, z, z, z, gates_t]
    h_spec = pl.BlockSpec((None, nseq, ln, width), lambda b, c: (b, 0, c, 0))
    st = lambda a, b_: pl.BlockSpec((nseq, nh, a, b_), lambda b, c: (b, 0, 0, 0))
    return pl.pallas_call(
        functools.partial(_mlstm_prompt_kernel, nh=nh, dh=dh, nseq=nseq),
        grid=(batch // nseq, nc),
        in_specs=in_specs,
        out_specs=[h_spec, st(dh, dh), st(1, dh), st(1, dh)],
        out_shape=[
            jax.ShapeDtypeStruct((batch // nseq, nseq, seq, width), BF16),
            jax.ShapeDtypeStruct((batch, nh, dh, dh), F32),
            jax.ShapeDtypeStruct((batch, nh, 1, dh), F32),
            jax.ShapeDtypeStruct((batch, nh, 1, dh), F32),
        ],
        scratch_shapes=[pltpu.VMEM((nseq, nh, dh, dh), F32), pltpu.VMEM((nseq, nh, 1, dh), F32),
                        pltpu.VMEM((nseq, nh, 1, dh), F32)],
        compiler_params=_params("parallel", "arbitrary"),
        name="mlstm_prompt",
    )(*operands)


def _mlstm_step_kernel(q_ref, k_ref, v_ref, o_ref, li_ref, fp_ref, c_ref, n_ref, m_ref,
                       h_ref, c_out, n_out, m_out, *, nh, dh):
    eye = (lax.broadcasted_iota(I32, (1, dh, dh), 1) == lax.broadcasted_iota(I32, (1, dh, dh), 2))
    scale = dh ** -0.5
    for h in range(nh):
        sl = slice(h * dh, (h + 1) * dh)
        q = q_ref[:, :, sl]
        k = k_ref[:, :, sl] * scale
        v = v_ref[:, :, sl]
        c = c_ref[:, h]
        n = n_ref[:, h]
        m = m_ref[:, h]
        li = li_ref[:, h]
        lf = jax.nn.log_sigmoid(fp_ref[:, h])
        inter = lf + m
        m_row = jnp.maximum(inter, li)
        w_intra = jnp.exp(li - m_row)
        w_inter = jnp.exp(inter - m_row)
        sc = jnp.sum(q * k, axis=-1, keepdims=True) * w_intra
        q8 = jnp.broadcast_to(q, (q.shape[0], SUBLANES, dh)).astype(BF16)
        cq = jnp.einsum("bqk,bvk->bqv", q8, c.astype(BF16),
                        preferred_element_type=F32)[:, 0:1, :]
        num = sc * v + w_inter * cq
        den = sc + w_inter * jnp.sum(n * q, axis=-1, keepdims=True)
        h_l = num / jnp.maximum(jnp.abs(den), jnp.exp(-m_row))
        h_ref[:, :, sl] = h_l * jax.nn.sigmoid(o_ref[:, :, sl])
        v_c = jnp.sum(jnp.where(eye, v, 0.0), axis=-1, keepdims=True)
        wg = jnp.exp(li - m_row)
        decay = jnp.exp(inter - m_row)
        c_out[:, h] = decay * c + (v_c * wg) * k
        n_out[:, h] = decay * n + wg * k
        m_out[:, h] = m_row


def _mlstm_step(z3, li, fp, c, n, m, nh, dh):
    batch = c.shape[0]
    bb = MLSTM_STEP_BATCH
    width = nh * dh
    blk = lambda j: pl.BlockSpec((bb, 1, width), lambda i: (i, 0, j))
    st4 = lambda a, b: pl.BlockSpec((bb, nh, a, b), lambda i: (i, 0, 0, 0))
    return pl.pallas_call(
        functools.partial(_mlstm_step_kernel, nh=nh, dh=dh),
        grid=(batch // bb,),
        in_specs=[blk(1), blk(2), blk(3), blk(4), st4(1, 1), st4(1, 1),
                  st4(dh, dh), st4(1, dh), st4(1, 1)],
        out_specs=[pl.BlockSpec((bb, 1, width), lambda i: (i, 0, 0)),
                   st4(dh, dh), st4(1, dh), st4(1, 1)],
        out_shape=[
            jax.ShapeDtypeStruct((batch, 1, width), F32),
            jax.ShapeDtypeStruct((batch, nh, dh, dh), F32),
            jax.ShapeDtypeStruct((batch, nh, 1, dh), F32),
            jax.ShapeDtypeStruct((batch, nh, 1, 1), F32),
        ],
        compiler_params=_params("parallel"),
        name="mlstm_step",
    )(z3, z3, z3, z3, li, fp, c, n, m)


def _split_bf16(a):
    hi = a.astype(BF16)
    return hi, (a - hi.astype(F32)).astype(BF16)


def _route(hn, wr_ref, br_ref, tril_ref, carry_ref, rinfo_ref, rinfo_t_ref):
    ngrp, epg = MOE_GROUPS, MOE_EXPERTS_PER_GROUP
    h_hi, h_lo = _split_bf16(hn)
    both = _dot(h_hi, wr_ref[...])
    logits = both[:, 0:LANES] + (both[:, LANES:2 * LANES] + _dot(h_lo, wr_ref[:, 0:LANES])) + br_ref[...]
    tm = logits.shape[0]
    lane = lax.broadcasted_iota(I32, (tm, LANES), 1)
    neg = -jnp.inf

    def first_max(x):
        mx = jnp.max(x, axis=1, keepdims=True)
        idx = jnp.min(jnp.where(x == mx, lane, LANES), axis=1, keepdims=True)
        return mx, idx

    is_grp = lane < ngrp
    gmax, gsel = first_max(jnp.where(is_grp, logits, neg))
    g_w = 1.0 / jnp.sum(jnp.where(is_grp, jnp.exp(logits - gmax), 0.0), axis=1, keepdims=True)
    lo = ngrp + gsel * epg
    el = jnp.where((lane >= lo) & (lane < lo + epg), logits, neg)
    v1, i1 = first_max(el)
    v2, i2 = first_max(jnp.where(lane == i1, neg, el))
    e2 = jnp.exp(v2 - v1)
    w1 = g_w / (1.0 + e2)
    w2 = g_w * e2 / (1.0 + e2)
    eid1 = i1 - ngrp
    eid2 = i2 - ngrp

    hit1 = lane == eid1
    hit2 = lane == eid2
    onehot = jnp.where(hit1 | hit2, 1.0, 0.0)
    carry = carry_ref[...]
    prefix = _dot(tril_ref[...], onehot.astype(BF16)) + carry
    rank1 = jnp.sum(jnp.where(hit1, prefix, 0.0), axis=1, keepdims=True)
    rank2 = jnp.sum(jnp.where(hit2, prefix, 0.0), axis=1, keepdims=True)
    carry_ref[...] = carry + jnp.sum(onehot, axis=0, keepdims=True)

    cols = (eid1.astype(F32), eid2.astype(F32), w1, w2, rank1, rank2)
    info = jnp.zeros((tm, LANES), F32)
    for j, cval in enumerate(cols):
        info = jnp.where(lane == j, cval, info)
    rinfo_ref[...] = info
    pick = jnp.where(lax.broadcasted_iota(I32, (SUBLANES, LANES), 0)
                     == lax.broadcasted_iota(I32, (SUBLANES, LANES), 1), 1.0, 0.0).astype(BF16)
    parts = [_dot_nt(pick, p) for p in _split3(info)]
    rinfo_t_ref[...] = parts[0] + (parts[1] + parts[2])


def _outproj_route_kernel(xm_ref, xt_ref, pm_ref, pt_ref, mm_ref, mt_ref,
                          wo_ref, g_ref, wr_ref, br_ref, tril_ref,
                          x1_ref, hn_ref, rinfo_ref, rinfo_t_ref, cnt_ref, carry_ref):
    @pl.when(pl.program_id(0) == 0)
    def _():
        carry_ref[...] = jnp.zeros(carry_ref.shape, F32)

    half = pm_ref.shape[1]
    mix = (_dot(_stacked(pm_ref, pt_ref), wo_ref[0:half, :])
           + _dot(_stacked(mm_ref, mt_ref), wo_ref[half:2 * half, :]))
    x1 = _stacked(xm_ref, xt_ref) + mix
    x1_ref[...] = x1
    hn = _rms(x1, g_ref[...])
    hn_ref[...] = hn
    _route(hn, wr_ref, br_ref, tril_ref, carry_ref, rinfo_ref, rinfo_t_ref)
    cnt_ref[...] = carry_ref[...]


def _glu_route_kernel(x_ref, ym_ref, yt_ref, wglu_ref, g_ref, wr_ref, br_ref, tril_ref,
                      x1_ref, hn_ref, rinfo_ref, rinfo_t_ref, cnt_ref, carry_ref):
    @pl.when(pl.program_id(0) == 0)
    def _():
        carry_ref[...] = jnp.zeros(carry_ref.shape, F32)

    d = x_ref.shape[1]
    ag = _dot(jax.nn.gelu(_stacked(ym_ref, yt_ref)).astype(BF16), wglu_ref[...])
    x1 = x_ref[...] + ag[:, 0:d] * jax.nn.sigmoid(ag[:, d:2 * d])
    x1_ref[...] = x1
    hn = _rms(x1, g_ref[...])
    hn_ref[...] = hn
    _route(hn, wr_ref, br_ref, tril_ref, carry_ref, rinfo_ref, rinfo_t_ref)
    cnt_ref[...] = carry_ref[...]


def _mix_route(kernel, name, n, row_specs, rows, w, g, wr, br, tril):
    d = g.shape[1]
    tm = TOKEN_TILE
    full = lambda i: (0, 0)
    return pl.pallas_call(
        kernel,
        grid=(n // tm,),
        in_specs=row_specs + [
            pl.BlockSpec(w.shape, full),
            pl.BlockSpec((1, d), full),
            pl.BlockSpec((d, 2 * LANES), full),
            pl.BlockSpec((1, LANES), full),
            pl.BlockSpec((tm, tm), full),
        ],
        out_specs=[
            pl.BlockSpec((tm, d), lambda i: (i, 0)),
            pl.BlockSpec((tm, d), lambda i: (i, 0)),
            pl.BlockSpec((tm, LANES), lambda i: (i, 0)),
            pl.BlockSpec((SUBLANES, tm), lambda i: (0, i)),
            pl.BlockSpec((1, LANES), full),
        ],
        out_shape=[
            jax.ShapeDtypeStruct((n, d), F32),
            jax.ShapeDtypeStruct((n, d), F32),
            jax.ShapeDtypeStruct((n, LANES), F32),
            jax.ShapeDtypeStruct((SUBLANES, n), F32),
            jax.ShapeDtypeStruct((1, LANES), F32),
        ],
        scratch_shapes=[pltpu.VMEM((1, LANES), F32)],
        compiler_params=_params("arbitrary"),
        name=name,
    )(*rows, w, g, wr, br, tril)


def _index_copy(pos_hbm, idx_s, sem_i, tile, slot):
    return pltpu.make_async_copy(pos_hbm.at[tile], idx_s.at[slot], sem_i.at[slot])


def _dispatch_kernel(pos_hbm, hn_hbm, xs_hbm, idx_s, tiles, sem_i, sem_l, sem_d):
    i = pl.program_id(0)
    nt = pl.num_programs(0)
    nslot, tm, _ = tiles.shape
    islot = i % 2

    def load(t):
        s = t % nslot
        return pltpu.make_async_copy(hn_hbm.at[pl.ds(t * tm, tm)], tiles.at[s], sem_l.at[s])

    def wait_rows(t):
        s = t % nslot
        whole = pltpu.make_async_copy(tiles.at[s], xs_hbm.at[pl.ds(0, tm)], sem_d.at[s])
        whole.wait()
        whole.wait()

    @pl.when(i == 0)
    def _():
        _index_copy(pos_hbm, idx_s, sem_i, 0, 0).start()
        load(0).start()

        @pl.when(nt > 1)
        def _():
            load(1).start()

    @pl.when(i >= 2)
    def _():
        wait_rows(i - 2)

    @pl.when(i + 2 < nt)
    def _():
        load(i + 2).start()

    _index_copy(pos_hbm, idx_s, sem_i, i, islot).wait()

    @pl.when(i + 1 < nt)
    def _():
        _index_copy(pos_hbm, idx_s, sem_i, i + 1, 1 - islot).start()

    load(i).wait()
    slot = i % nslot

    def issue(r, carry):
        row = tiles.at[slot, pl.ds(r, 1)]
        pltpu.make_async_copy(row, xs_hbm.at[pl.ds(idx_s[islot, 0, r], 1)],
                              sem_d.at[slot]).start(priority=0)
        pltpu.make_async_copy(row, xs_hbm.at[pl.ds(idx_s[islot, 0, tm + r], 1)],
                              sem_d.at[slot]).start(priority=1)
        return carry

    lax.fori_loop(0, tm, issue, 0, unroll=DMA_ISSUE_UNROLL)

    @pl.when(i == nt - 1)
    def _():
        @pl.when(nt > 1)
        def _():
            wait_rows(i - 1)

        wait_rows(i)


def _dispatch(pos_tiles, hn):
    n, d = hn.shape
    tm = TOKEN_TILE
    nslot = 4
    return pl.pallas_call(
        _dispatch_kernel,
        grid=(n // tm,),
        in_specs=[pl.BlockSpec(memory_space=pl.ANY), pl.BlockSpec(memory_space=pl.ANY)],
        out_specs=pl.BlockSpec(memory_space=pl.ANY),
        out_shape=jax.ShapeDtypeStruct((2 * n, d), F32),
        scratch_shapes=[pltpu.SMEM((2, 1, 2 * tm), I32), pltpu.VMEM((nslot, tm, d), F32),
                        pltpu.SemaphoreType.DMA((2,)), pltpu.SemaphoreType.DMA((nslot,)),
                        pltpu.SemaphoreType.DMA((nslot,))],
        compiler_params=_params("arbitrary"),
        name="moe_dispatch",
    )(pos_tiles, hn)


def _moe_kernel(vt_ref, ve_ref, von_ref, vnext_ref, vslot_ref, lo_ref, hi_ref,
                xs_ref, wg_hbm, wu_hbm, wd_hbm, eo_ref,
                wg_f, wu_f, wd_f, wgb, wub, wdb, sem_w, *, layer):
    v = pl.program_id(0)
    tr = xs_ref.shape[0]
    prev = jnp.maximum(v - 1, 0)
    e = ve_ref[v]
    new_expert = jnp.logical_or(v == 0, e != ve_ref[prev])
    first_visit = jnp.logical_or(v == 0, vt_ref[v] != vt_ref[prev])

    def fetch(expert, slot):
        return [pltpu.make_async_copy(src.at[layer, expert], dst.at[slot], sem_w.at[slot, j])
                for j, (src, dst) in enumerate(((wg_hbm, wg_f), (wu_hbm, wu_f), (wd_hbm, wd_f)))]

    @pl.when(von_ref[v] == 1)
    def _():
        @pl.when(new_expert)
        def _():
            slot = vslot_ref[v]

            @pl.when(v == 0)
            def _():
                for cp in fetch(e, slot):
                    cp.start()

            for cp in fetch(e, slot):
                cp.wait()
            wgb[...] = wg_f[slot].astype(BF16)
            wub[...] = wu_f[slot].astype(BF16)
            wdb[...] = wd_f[slot].astype(BF16)

            @pl.when(vnext_ref[v] >= 0)
            def _():
                for cp in fetch(vnext_ref[v], 1 - slot):
                    cp.start()

        x = xs_ref[...].astype(BF16)
        act = jax.nn.silu(_dot(x, wgb[...])) * _dot(x, wub[...])
        row = vt_ref[v] * tr + lax.broadcasted_iota(I32, (tr, 1), 0)
        act = jnp.where((row >= lo_ref[e]) & (row < hi_ref[e]), act, 0.0)
        res = _dot(act.astype(BF16), wdb[...])

        @pl.when(first_visit)
        def _():
            eo_ref[...] = res

        @pl.when(jnp.logical_not(first_visit))
        def _():
            eo_ref[...] += res


def _moe_experts(vt, ve, von, vnext, vslot, lo, hi, xs, wg, wu, wd, layer):
    nv = vt.shape[0]
    rows, d = xs.shape
    hid = wg.shape[3]
    tr = EXPERT_ROW_TILE
    tile = pl.BlockSpec((tr, d), lambda v, vt, *_: (vt[v], 0))
    grid_spec = pltpu.PrefetchScalarGridSpec(
        num_scalar_prefetch=7,
        grid=(nv,),
        in_specs=[tile] + [pl.BlockSpec(memory_space=pl.ANY)] * 3,
        out_specs=tile,
        scratch_shapes=[
            pltpu.VMEM((2, d, hid), F32), pltpu.VMEM((2, d, hid), F32), pltpu.VMEM((2, hid, d), F32),
            pltpu.VMEM((d, hid), BF16), pltpu.VMEM((d, hid), BF16), pltpu.VMEM((hid, d), BF16),
            pltpu.SemaphoreType.DMA((2, 3)),
        ],
    )
    return pl.pallas_call(
        functools.partial(_moe_kernel, layer=layer),
        grid_spec=grid_spec,
        out_shape=jax.ShapeDtypeStruct((rows, d), F32),
        compiler_params=_params("arbitrary"),
        name="moe_experts",
    )(vt, ve, von, vnext, vslot, lo, hi, xs, wg, wu, wd)


def _combine_kernel(pos_hbm, eo_hbm, x_ref, rinfo_ref, g_ref, o1_ref, o2_ref, idx_s, a_buf, b_buf,
                    sem_i, sem_a, sem_b, *, last_layer):
    i = pl.program_id(0)
    nt = pl.num_programs(0)
    tm = a_buf.shape[1]
    slot = i % 2

    def gathers(s):
        def issue(r, carry):
            pltpu.make_async_copy(eo_hbm.at[pl.ds(idx_s[s, 0, r], 1)],
                                  a_buf.at[s, pl.ds(r, 1)], sem_a.at[s]).start(priority=0)
            pltpu.make_async_copy(eo_hbm.at[pl.ds(idx_s[s, 0, tm + r], 1)],
                                  b_buf.at[s, pl.ds(r, 1)], sem_b.at[s]).start(priority=1)
            return carry

        lax.fori_loop(0, tm, issue, 0, unroll=DMA_ISSUE_UNROLL)

    @pl.when(i == 0)
    def _():
        first = _index_copy(pos_hbm, idx_s, sem_i, 0, 0)
        first.start()
        first.wait()
        gathers(0)

        @pl.when(nt > 1)
        def _():
            _index_copy(pos_hbm, idx_s, sem_i, 1, 1).start()

    @pl.when(i + 1 < nt)
    def _():
        _index_copy(pos_hbm, idx_s, sem_i, i + 1, 1 - slot).wait()
        gathers(1 - slot)

    @pl.when(i + 2 < nt)
    def _():
        _index_copy(pos_hbm, idx_s, sem_i, i + 2, slot).start()

    pltpu.make_async_copy(eo_hbm.at[pl.ds(0, tm)], a_buf.at[slot], sem_a.at[slot]).wait()
    pltpu.make_async_copy(eo_hbm.at[pl.ds(0, tm)], b_buf.at[slot], sem_b.at[slot]).wait()
    info = rinfo_ref[...]
    x2 = x_ref[...] + (info[:, 2:3] * a_buf[slot] + info[:, 3:4] * b_buf[slot])
    hn = _rms(x2, g_ref[...])
    if last_layer:
        @pl.when(i < nt - 1)
        def _():
            o1_ref[...] = hn

        @pl.when(i == nt - 1)
        def _():
            o2_ref[...] = hn
    else:
        o1_ref[...] = x2
        o2_ref[...] = hn


def _combine(pos_tiles, eo, x, rinfo, g, last_layer):
    n, d = x.shape
    tm = TOKEN_TILE
    nt = n // tm
    row = pl.BlockSpec((tm, d), lambda i: (i, 0))
    if last_layer:
        out_specs = [pl.BlockSpec((tm, d), lambda i: (jnp.minimum(i, nt - 2), 0)),
                     pl.BlockSpec((tm, d), lambda i: (0, 0))]
        out_shape = [jax.ShapeDtypeStruct((n - tm, d), F32), jax.ShapeDtypeStruct((tm, d), F32)]
    else:
        out_specs = [row, row]
        out_shape = [jax.ShapeDtypeStruct((n, d), F32), jax.ShapeDtypeStruct((n, d), F32)]
    return pl.pallas_call(
        functools.partial(_combine_kernel, last_layer=last_layer),
        grid=(nt,),
        in_specs=[
            pl.BlockSpec(memory_space=pl.ANY),
            pl.BlockSpec(memory_space=pl.ANY),
            row,
            pl.BlockSpec((tm, LANES), lambda i: (i, 0)),
            pl.BlockSpec((1, d), lambda i: (0, 0)),
        ],
        out_specs=out_specs,
        out_shape=out_shape,
        scratch_shapes=[
            pltpu.SMEM((2, 1, 2 * tm), I32),
            pltpu.VMEM((2, tm, d), F32),
            pltpu.VMEM((2, tm, d), F32),
            pltpu.SemaphoreType.DMA((2,)),
            pltpu.SemaphoreType.DMA((2,)),
            pltpu.SemaphoreType.DMA((2,)),
        ],
        compiler_params=_params("arbitrary"),
        name="moe_combine",
    )(pos_tiles, eo, x, rinfo, g)


def _moe(x1, hn, rinfo, rinfo_t, counts, wg, wu, wd, layer, next_gain, last_layer):
    n = x1.shape[0]
    ne = wg.shape[1]
    tr = EXPERT_ROW_TILE
    tm = TOKEN_TILE
    nv = (2 * n) // tr + ne - 1
    eid = rinfo_t[0:2].astype(I32)
    rank = rinfo_t[4:6].astype(I32)
    cnt = counts[0, :ne].astype(I32)
    seg_end = jnp.cumsum(cnt)
    seg_start = seg_end - cnt
    experts = jnp.arange(ne, dtype=I32)
    pos = rank + jnp.sum(jnp.where(eid[:, None, :] == experts[None, :, None],
                                   seg_start[None, :, None], 0), axis=1)
    pos_tiles = pos.reshape(2, n // tm, tm).transpose(1, 0, 2).reshape(n // tm, 1, 2 * tm)

    first_tile = seg_start // tr
    tiles_e = jnp.where(cnt > 0, (seg_end - 1) // tr - first_tile + 1, 0)
    v_end = jnp.cumsum(tiles_e)
    v_start = v_end - tiles_e
    total = v_end[-1]
    vis = jnp.arange(nv, dtype=I32)
    vc = jnp.minimum(vis, jnp.maximum(total - 1, 0))
    ve = jnp.sum((vc[:, None] >= v_end[None, :]).astype(I32), axis=1)
    pick = lambda tab: jnp.sum(jnp.where(ve[:, None] == experts, tab, 0), axis=1)
    vt = pick(first_tile) + (vc - pick(v_start))
    von = (vis < total).astype(I32)
    present = cnt > 0
    later = present[None, :] & (experts[None, :] > experts[:, None])
    next_of = jnp.min(jnp.where(later, experts[None, :], ne), axis=1)
    vnext = pick(jnp.where(next_of < ne, next_of, -1))
    vslot = pick(jnp.cumsum(present.astype(I32)) - 1) & 1

    xs = _dispatch(pos_tiles, hn)
    eo = _moe_experts(vt, ve, von, vnext, vslot, seg_start, seg_end, xs, wg, wu, wd, layer)
    return _combine(pos_tiles, eo, x1, rinfo, next_gain, last_layer)


def _cis(log_mag, ang):
    mag = jnp.exp(log_mag)
    return mag * jnp.cos(ang), mag * jnp.sin(ang)


def _split3(a):
    p1 = a.astype(BF16)
    r1 = a - p1.astype(F32)
    p2 = r1.astype(BF16)
    return p1, p2, (r1 - p2.astype(F32)).astype(BF16)


def _select_dot(a, sel, sel_first=False):
    sel = sel.astype(BF16)
    parts = [(_dot(sel, p) if sel_first else _dot(p, sel)) for p in _split3(a)]
    return parts[0] + (parts[1] + parts[2])


def _dot3(a, b):
    a_hi, a_lo = _split_bf16(a)
    b_hi, b_lo = _split_bf16(b)
    return _dot(a_hi, b_hi) + (_dot(a_lo, b_hi) + _dot(a_hi, b_lo))


def _s5_prep_kernel(lsc_ref, lsr_ref, lrc_ref, lic_ref, lrr_ref, lir_ref, ctre_ref, ctim_ref,
                    btre_ref, btim_ref, d_ref,
                    t_ref, wre_ref, wim_ref, cyre_ref, cyim_ref, apr_ref, api_ref,
                    bbr_ref, bbi_ref, lbr_ref, lbi_ref, *, gw, gpb, sub):
    blk = gpb * gw
    wide = sub * blk
    nsl = lrc_ref.shape[1]
    ns = nsl // gpb
    sh_gw, sh_ns = gw.bit_length() - 1, ns.bit_length() - 1
    dt_c, dt_r = jnp.exp(lsc_ref[0]), jnp.exp(lsr_ref[0])
    ldt_c_re, ldt_c_im = lrc_ref[0] * dt_c, lic_ref[0] * dt_c
    ldt_r_re, ldt_r_im = lrr_ref[0] * dt_r, lir_ref[0] * dt_r

    lane_w = lax.broadcasted_iota(I32, (1, wide), 1)
    spread = jnp.where((lax.broadcasted_iota(I32, (gw, wide), 1) & (gw - 1))
                       == lax.broadcasted_iota(I32, (gw, wide), 0), 1.0, 0.0)
    cre = _select_dot(ctre_ref[0], spread)
    cim = _select_dot(ctim_ref[0], spread)
    same = (jnp.right_shift(lax.broadcasted_iota(I32, (nsl, 1), 0), sh_ns)
            == (jnp.right_shift(lane_w, sh_gw) & (gpb - 1)))
    assert 2 * blk == LANES
    tau = lax.broadcasted_iota(I32, (1, LANES), 1).astype(F32)
    pw_re, pw_im = _cis(tau * ldt_c_re, tau * ldt_c_im)
    low = lax.broadcasted_iota(I32, (nsl, LANES), 1) < blk

    def spread_pow(p, first):
        col = lambda t: jnp.broadcast_to(p[:, t:t + 1], (nsl, LANES))
        return jnp.concatenate([jnp.where(low, col(first + 2 * m), col(first + 2 * m + 1))
                                for m in range(sub // 2)], axis=1)

    def c_lam_pow(first):
        pr, pi = spread_pow(pw_re, first), spread_pow(pw_im, first)
        return (jnp.where(same, pr * cre - pi * cim, 0.0),
                jnp.where(same, -(pr * cim + pi * cre), 0.0))

    clr0, cli0 = c_lam_pow(0)
    clr1, cli1 = c_lam_pow(1)
    cyre_ref[0] = clr1.astype(BF16)
    cyim_ref[0] = cli1.astype(BF16)

    lbr, lbi = _cis(ldt_r_re, ldt_r_im)
    lbr_ref[0] = lbr
    lbi_ref[0] = lbi
    lr, li = lrr_ref[0], lir_ref[0]
    nr, ni = lbr - 1.0, lbi
    den = lr * lr + li * li
    fr = (nr * lr + ni * li) / den
    fi = (ni * lr - nr * li) / den
    bre, bim = btre_ref[0], btim_ref[0]
    bbr = fr * bre - fi * bim
    bbi = fr * bim + fi * bre
    bbr_ref[0] = bbr
    bbi_ref[0] = bbi

    lane_t = lax.broadcasted_iota(I32, (gw, wide), 1)
    chan = lax.broadcasted_iota(I32, (gw, wide), 0)
    for g in range(gpb):
        ps = slice(g * ns, (g + 1) * ns)
        r = _dot3(bbr[:, ps], clr0[ps, :]) + _dot3(bbi[:, ps], cli0[ps, :])
        r = r + jnp.where(lane_t == g * gw + chan, d_ref[0][:, g:g + 1], 0.0)
        for j in range(sub):
            tb = r if j == 0 else jnp.where(lane_t >= blk * j, pltpu.roll(r, blk * j, 1), 0.0)
            r0 = j * blk + g * gw
            t_ref[0, r0:r0 + gw, :] = tb.astype(BF16)

    rows = lax.broadcasted_iota(I32, (wide, 1), 0)
    spread_t = jnp.where((lax.broadcasted_iota(I32, (wide, gw), 0) & (gw - 1))
                         == lax.broadcasted_iota(I32, (wide, gw), 1), 1.0, 0.0)
    bbr_t = _select_dot(bbr, spread_t, sel_first=True)
    bbi_t = _select_dot(bbi, spread_t, sel_first=True)
    same_w = ((jnp.right_shift(rows, sh_gw) & (gpb - 1))
              == jnp.right_shift(lax.broadcasted_iota(I32, (1, nsl), 1), sh_ns))
    rj = ((sub - 1) - lax.broadcasted_iota(I32, (sub, 1), 0)).astype(F32)
    q_re, q_im = _cis(rj * ldt_r_re, rj * ldt_r_im)
    per_step = lambda q: jnp.concatenate(
        [jnp.broadcast_to(q[j:j + 1, :], (blk, nsl)) for j in range(sub)], axis=0)
    pr, pi = per_step(q_re), per_step(q_im)
    wre_ref[0] = jnp.where(same_w, pr * bbr_t - pi * bbi_t, 0.0).astype(BF16)
    wim_ref[0] = jnp.where(same_w, pr * bbi_t + pi * bbr_t, 0.0).astype(BF16)

    nlev = apr_ref.shape[1]
    pw = (sub * jnp.left_shift(1, lax.broadcasted_iota(I32, (nlev, 1), 0))).astype(F32)
    apr, api = _cis(pw * ldt_r_re, pw * ldt_r_im)
    apr_ref[0] = apr
    api_ref[0] = api


def _s5_prep(log_step, lam_re, lam_im, b_re, b_im, c_re, c_im, d_skip):
    ng, ns = lam_re.shape
    gw = b_re.shape[2]
    sub, gpb = S5_SUB, S5_GROUPS_PER_BLOCK
    nblk = ng // gpb
    nsl = gpb * ns
    wide = sub * gpb * gw
    ls = jnp.repeat(log_step, ns)
    ct = lambda c: jnp.transpose(c, (0, 2, 1)).reshape(nblk, nsl, gw)
    bt = lambda b: jnp.transpose(b.reshape(nblk, gpb, ns, gw), (0, 3, 1, 2)).reshape(nblk, gw, nsl)
    ins = [ls.reshape(nblk, nsl, 1), ls.reshape(nblk, 1, nsl),
           lam_re.reshape(nblk, nsl, 1), lam_im.reshape(nblk, nsl, 1),
           lam_re.reshape(nblk, 1, nsl), lam_im.reshape(nblk, 1, nsl),
           ct(c_re), ct(c_im), bt(b_re), bt(b_im),
           jnp.transpose(d_skip.reshape(nblk, gpb, gw), (0, 2, 1))]
    spec = lambda a: pl.BlockSpec((1,) + a.shape[1:], lambda g: (g, 0, 0))
    outs = [
        jax.ShapeDtypeStruct((nblk, wide, wide), BF16),
        jax.ShapeDtypeStruct((nblk, wide, nsl), BF16),
        jax.ShapeDtypeStruct((nblk, wide, nsl), BF16),
        jax.ShapeDtypeStruct((nblk, nsl, wide), BF16),
        jax.ShapeDtypeStruct((nblk, nsl, wide), BF16),
        jax.ShapeDtypeStruct((nblk, SUBLANES, nsl), F32),
        jax.ShapeDtypeStruct((nblk, SUBLANES, nsl), F32),
        jax.ShapeDtypeStruct((nblk, gw, nsl), F32),
        jax.ShapeDtypeStruct((nblk, gw, nsl), F32),
        jax.ShapeDtypeStruct((nblk, 1, nsl), F32),
        jax.ShapeDtypeStruct((nblk, 1, nsl), F32),
    ]
    return pl.pallas_call(
        functools.partial(_s5_prep_kernel, gw=gw, gpb=gpb, sub=sub),
        grid=(nblk,),
        in_specs=[spec(a) for a in ins],
        out_specs=[spec(o) for o in outs],
        out_shape=outs,
        compiler_params=_params("parallel"),
        name="s5_prep",
    )(*ins)


def _s5_prompt_kernel(x_ref, t_ref, wre_ref, wim_ref, cyre_ref, cyim_ref, apr_ref, api_ref,
                      y_ref, hre_out, him_out, u_s, y4_s, *, nb, seq, sub):
    nk = seq // sub
    rows = nb * nk
    hl = LANES // 2
    nlev = nk.bit_length() - 1
    nsl = wre_ref.shape[2]
    low = lax.broadcasted_iota(I32, (nk, LANES), 1) < hl
    kidx = lax.broadcasted_iota(I32, (rows, 1), 0) & (nk - 1)

    def step_rows(b, j):
        return (pl.ds(b * seq + j, nk, stride=sub), slice(None))

    for b in range(nb):
        rs = slice(b * nk, (b + 1) * nk)
        for m in range(sub // 2):
            ls = slice(m * LANES, (m + 1) * LANES)
            s0 = x_ref[step_rows(b, 2 * m)]
            s1 = x_ref[step_rows(b, 2 * m + 1)]
            u_s[0, rs, ls] = jnp.where(low, s0, pltpu.roll(s1, hl, 1)).astype(BF16)
            u_s[1, rs, ls] = jnp.where(low, pltpu.roll(s0, hl, 1), s1).astype(BF16)

    def shifted(a, s):
        return jnp.where(kidx >= s, pltpu.roll(a, s, 0), 0.0)

    halves = range(2)
    us = [u_s[h] for h in halves]
    hre = [_dot(us[h], wre_ref[h]) for h in halves]
    him = [_dot(us[h], wim_ref[h]) for h in halves]
    cw = 2 * LANES
    y_conv = [jnp.concatenate(
        [_dot(us[h][:, 0:c0 + cw], t_ref[h, 0:c0 + cw, c0:c0 + cw]) for c0 in range(0, us[h].shape[1], cw)],
        axis=1) for h in halves]
    for i in range(nlev):
        for h in halves:
            ar = apr_ref[h, i:i + 1, :]
            ai = api_ref[h, i:i + 1, :]
            pre, pim = shifted(hre[h], 1 << i), shifted(him[h], 1 << i)
            hre[h], him[h] = hre[h] + (ar * pre - ai * pim), him[h] + (ar * pim + ai * pre)
    for h in halves:
        ls = slice(h * nsl, (h + 1) * nsl)
        for b in range(nb):
            last = (b + 1) * nk - 1
            hre_out[b:b + 1, ls] = hre[h][last:last + 1, :]
            him_out[b:b + 1, ls] = him[h][last:last + 1, :]
        hpre = shifted(hre[h], 1).astype(BF16)
        hpim = shifted(him[h], 1).astype(BF16)
        y4_s[h] = y_conv[h] + _dot(hpre, cyre_ref[h]) + _dot(hpim, cyim_ref[h])

    for b in range(nb):
        rs = slice(b * nk, (b + 1) * nk)
        for m in range(sub // 2):
            ls = slice(m * LANES, (m + 1) * LANES)
            ca = y4_s[0, rs, ls]
            cb = y4_s[1, rs, ls]
            y_ref[step_rows(b, 2 * m)] = jnp.where(low, ca, pltpu.roll(cb, hl, 1))
            y_ref[step_rows(b, 2 * m + 1)] = jnp.where(low, pltpu.roll(ca, hl, 1), cb)


def _s5_prompt(h, prep, batch, seq):
    d = h.shape[1]
    tmat, wre, wim, cyre, cyim, apr, api = prep[:7]
    nblk, wide, nsl = wre.shape
    sub, nb = S5_SUB, S5_BATCH_PER_STEP
    nk = seq // sub
    assert nk & (nk - 1) == 0 and nk.bit_length() - 1 <= apr.shape[1]
    ntile = d // LANES
    assert nblk == 2 * ntile
    rows = nb * nk
    wspec = lambda a: pl.BlockSpec((2,) + a.shape[1:], lambda t, b: (t, 0, 0))
    xspec = pl.BlockSpec((nb * seq, LANES), lambda t, b: (b, t))
    sspec = pl.BlockSpec((None, nb, 2 * nsl), lambda t, b: (b, 0, t))
    return pl.pallas_call(
        functools.partial(_s5_prompt_kernel, nb=nb, seq=seq, sub=sub),
        grid=(ntile, batch // nb),
        in_specs=[xspec, wspec(tmat), wspec(wre), wspec(wim), wspec(cyre), wspec(cyim),
                  wspec(apr), wspec(api)],
        out_specs=[xspec, sspec, sspec],
        out_shape=[
            jax.ShapeDtypeStruct((batch * seq, d), F32),
            jax.ShapeDtypeStruct((batch // nb, nb, ntile * 2 * nsl), F32),
            jax.ShapeDtypeStruct((batch // nb, nb, ntile * 2 * nsl), F32),
        ],
        scratch_shapes=[
            pltpu.VMEM((2, rows, wide), BF16),
            pltpu.VMEM((2, rows, wide), F32),
        ],
        compiler_params=_params("parallel", "parallel"),
        name="s5_prompt",
    )(h, tmat, wre, wim, cyre, cyim, apr, api)


def _s5_step_kernel(u_ref, h_ref, hsw_ref, bb_ref, la_ref, lb_ref, cc_ref, d_ref, hn_ref, y_ref):
    for g in range(u_ref.shape[0]):
        u = u_ref[g]
        hn = la_ref[g] * h_ref[g] + lb_ref[g] * hsw_ref[g] + _dot3(u, bb_ref[g])
        hn_ref[g] = hn
        y_ref[g] = _dot3(hn, cc_ref[g]) + d_ref[g] * u


def _s5_step(u_p, h_cat, h_swp, bb_cat, la, lb, cc, d_p):
    ng = u_p.shape[0]
    gb = SUBLANES
    spec = lambda a: pl.BlockSpec((gb,) + a.shape[1:], lambda g: (g, 0, 0))
    ins = [u_p, h_cat, h_swp, bb_cat, la, lb, cc, d_p]
    outs = [jax.ShapeDtypeStruct(h_cat.shape, F32), jax.ShapeDtypeStruct(u_p.shape, F32)]
    return pl.pallas_call(
        _s5_step_kernel,
        grid=(ng // gb,),
        in_specs=[spec(a) for a in ins],
        out_specs=[spec(o) for o in outs],
        out_shape=outs,
        compiler_params=_params("parallel"),
        name="s5_step",
    )(*ins)


def _pad_to(a, axis, size):
    pad = [(0, 0)] * a.ndim
    pad[axis] = (0, size - a.shape[axis])
    return jnp.pad(a, pad)


def kernel(x_prompt, x_sample, state_pool, state_mlstm_c, state_mlstm_n, state_mlstm_m, state_s5_re, state_s5_im, norm_mix, norm_ffn, norm_final, w_in_ab, b_gates, pool_w, pool_scale, w_out_ab, s5_lam_re, s5_lam_im, s5_log_step, s5_b_re, s5_b_im, s5_c_re, s5_c_im, s5_d, w_glu, moe_w_group, moe_b_group, moe_w_expert, moe_b_expert, moe_w_gate, moe_w_up, moe_w_down):
    bp, tp, d = x_prompt.shape
    bs = x_sample.shape[0]
    n_p = bp * tp
    tm = TOKEN_TILE
    assert n_p % tm == 0 and bs <= tm
    nh, dh = state_mlstm_c.shape[2], state_mlstm_c.shape[3]
    pool_width = state_pool.shape[3]
    ml_width = nh * dh
    n_main = pool_width + 4 * ml_width
    n_gates = 2 * nh
    ngrp_s5, n_state = s5_lam_re.shape[1], s5_lam_re.shape[2]
    gw_s5 = d // ngrp_s5

    n = n_p + tm

    def tail_tile(sample_rows, dtype):
        return _pad_to(sample_rows.astype(dtype), 0, tm)

    x_main = x_prompt.reshape(n_p, d)
    x_tail = tail_tile(x_sample.reshape(bs, d), F32)
    tril = jnp.tril(jnp.ones((tm, tm), BF16), -1)

    def router_weights(l):
        wr = _pad_to(jnp.concatenate([moe_w_group[l], moe_w_expert[l]], axis=1), 1, LANES)
        br = jnp.concatenate([moe_b_group[l], moe_b_expert[l]])[None, :]
        hi = wr.astype(BF16)
        lo = (wr - hi.astype(F32)).astype(BF16)
        return jnp.concatenate([hi, lo], axis=1), _pad_to(br, 1, LANES)

    w_in = w_in_ab[0]
    z, gates, gates_t = _inproj(
        x_main, x_tail, norm_mix[0][None, :], _pad_to(w_in, 1, n_main + LANES).astype(BF16),
        _pad_to(b_gates[0][None, :], 1, LANES), n_gates)

    pw = pool_w[0].astype(BF16)
    ps = pool_scale[0][None, :]
    pool_y_p, pool_p = _pool_prompt(z, pw, ps, bp, tp)
    pool_y_s, pool_s_t = _pool_step(z, jnp.transpose(state_pool[0], (1, 0, 2)), pw, ps, n_p)
    pool_s = jnp.transpose(pool_s_t, (1, 0, 2))

    ml_y_p, c_p, n_p_st, m_p = _mlstm_prompt(z, gates_t, bp, tp, nh, dh)
    ml_y_p = ml_y_p.reshape(n_p, ml_width)
    g_s = gates[n_p:n_p + bs, :n_gates]
    ml_y_s, c_s, n_s_st, m_s = _mlstm_step(
        z[n_p:n_p + bs].reshape(bs, 1, n_main),
        g_s[:, :nh].reshape(bs, nh, 1, 1), g_s[:, nh:].reshape(bs, nh, 1, 1),
        state_mlstm_c[0], state_mlstm_n[0].reshape(bs, nh, 1, dh),
        state_mlstm_m[0].reshape(bs, nh, 1, 1), nh, dh)

    rows = [x_main, x_tail, pool_y_p, tail_tile(pool_y_s, BF16),
            ml_y_p, tail_tile(ml_y_s.reshape(bs, ml_width), BF16)]
    specs = (_stacked_specs(x_main, x_tail) + _stacked_specs(pool_y_p, rows[3])
             + _stacked_specs(ml_y_p, rows[5]))
    wr, br = router_weights(0)
    x1, hn, rinfo, rinfo_t, counts = _mix_route(
        _outproj_route_kernel, "outproj_route", n, specs, rows, w_out_ab[0].astype(BF16),
        norm_ffn[0][None, :], wr, br, tril)
    x2, h1 = _moe(x1, hn, rinfo, rinfo_t, counts, moe_w_gate, moe_w_up, moe_w_down, 0,
                  norm_mix[1][None, :], last_layer=False)

    prep = _s5_prep(s5_log_step[0], s5_lam_re[0], s5_lam_im[0], s5_b_re[0], s5_b_im[0],
                    s5_c_re[0], s5_c_im[0], s5_d[0])
    y_p, hre_p, him_p = _s5_prompt(h1, prep, bp, tp)
    s5_re_p = hre_p.reshape(bp, ngrp_s5, n_state)
    s5_im_p = him_p.reshape(bp, ngrp_s5, n_state)

    gpb = S5_GROUPS_PER_BLOCK
    per_group = lambda a: jnp.transpose(
        a.reshape(ngrp_s5 // gpb, gw_s5, gpb, n_state), (0, 2, 1, 3)).reshape(ngrp_s5, gw_s5, n_state)
    bbr, bbi = per_group(prep[7]), per_group(prep[8])
    lbr, lbi = prep[9].reshape(ngrp_s5, 1, n_state), prep[10].reshape(ngrp_s5, 1, n_state)
    d_g = s5_d[0].reshape(ngrp_s5, 1, gw_s5)
    u_s = h1[n_p:n_p + bs].reshape(bs, ngrp_s5, gw_s5).transpose(1, 0, 2)
    h_re = jnp.transpose(state_s5_re[0], (1, 0, 2))
    h_im = jnp.transpose(state_s5_im[0], (1, 0, 2))
    cc = jnp.concatenate([jnp.transpose(s5_c_re[0], (0, 2, 1)),
                          -jnp.transpose(s5_c_im[0], (0, 2, 1))], axis=1)
    hn_s, y_s = _s5_step(
        _pad_to(u_s, 2, LANES),
        jnp.concatenate([h_re, h_im], axis=2), jnp.concatenate([h_im, h_re], axis=2),
        _pad_to(jnp.concatenate([bbr, bbi], axis=2), 1, LANES),
        jnp.concatenate([lbr, lbr], axis=2), jnp.concatenate([-lbi, lbi], axis=2),
        _pad_to(cc, 2, LANES), _pad_to(d_g, 2, LANES))
    s5_re_s = jnp.transpose(hn_s[:, :, :n_state], (1, 0, 2))
    s5_im_s = jnp.transpose(hn_s[:, :, n_state:], (1, 0, 2))
    y_tail = tail_tile(jnp.transpose(y_s[:, :, :gw_s5], (1, 0, 2)).reshape(bs, d), F32)

    wr, br = router_weights(1)
    specs = [pl.BlockSpec((tm, d), lambda i: (i, 0))] + _stacked_specs(y_p, y_tail)
    x3, hn, rinfo, rinfo_t, counts = _mix_route(
        _glu_route_kernel, "glu_route", n, specs, [x2, y_p, y_tail], w_glu[0].astype(BF16),
        norm_ffn[1][None, :], wr, br, tril)
    y_main, y_last = _moe(x3, hn, rinfo, rinfo_t, counts, moe_w_gate, moe_w_up, moe_w_down, 1,
                          norm_final[None, :], last_layer=True)

    return (y_main.reshape(bp, tp, d), y_last[:bs].reshape(bs, 1, d),
            pool_p[None], c_p[None], n_p_st.reshape(1, bp, nh, dh), m_p[:, :, 0, 0][None],
            s5_re_p[None], s5_im_p[None],
            pool_s[None], c_s[None], n_s_st.reshape(1, bs, nh, dh), m_s.reshape(1, bs, nh),
            s5_re_s[None], s5_im_s[None])
```

```python
import functools

import jax
import jax.numpy as jnp
from jax import lax
from jax.experimental import pallas as pl
from jax.experimental.pallas import tpu as pltpu

F32 = jnp.float32
BF16 = jnp.bfloat16
I32 = jnp.int32

PAST_LEN = 16384
POOL_WINDOWS = (2, 4, 8, 16)
POOL_BUF = max(POOL_WINDOWS) - 1
MLSTM_CHUNK = 128
S5_SUB = 16
MOE_GROUPS = 4
MOE_EXPERTS_PER_GROUP = 8
RMS_EPS = 1e-6

LANES = 128
SUBLANES = 8
VMEM_LIMIT_BYTES = 56 * 1024 * 1024

TOKEN_TILE = 512
EXPERT_ROW_TILE = 512
MLSTM_SEQ_PER_STEP = 8
MLSTM_STEP_BATCH = 16
POOL_TIME_TILE = 1024
DMA_ISSUE_UNROLL = 8
S5_GROUPS_PER_BLOCK = 4
S5_BATCH_PER_STEP = 4


def _params(*sem):
    return pltpu.CompilerParams(dimension_semantics=sem, vmem_limit_bytes=VMEM_LIMIT_BYTES)


def _rms(x, g):
    return x * lax.rsqrt(jnp.mean(x * x, axis=-1, keepdims=True) + RMS_EPS) * g


def _dot(a, b):
    return jnp.dot(a, b, preferred_element_type=F32)


def _dot_nt(a, b):
    return lax.dot_general(a, b, (((1,), (1,)), ((), ())), preferred_element_type=F32)


def _dot_tn(a, b):
    return lax.dot_general(a, b, (((0,), (0,)), ((), ())), preferred_element_type=F32)


def _stacked(main_ref, tail_ref):
    last = pl.program_id(0) == pl.num_programs(0) - 1
    return jnp.where(last, tail_ref[...], main_ref[...])


def _stacked_specs(main, tail):
    tm, w = tail.shape
    last_main = main.shape[0] // tm - 1
    return [pl.BlockSpec((tm, w), lambda i: (jnp.minimum(i, last_main), 0)),
            pl.BlockSpec((tm, w), lambda i: (0, 0))]


def _inproj_kernel(xm_ref, xt_ref, g_ref, w_ref, bg_ref, z_ref, gates_ref, gatest_ref):
    h = _rms(_stacked(xm_ref, xt_ref), g_ref[...]).astype(BF16)
    zg = _dot(h, w_ref[...])
    nz = z_ref.shape[1]
    z_ref[...] = zg[:, 0:nz]
    gates = zg[:, nz:nz + LANES] + bg_ref[...]
    gates_ref[...] = gates
    gatest_ref[...] = gates.T[0:gatest_ref.shape[0], :]


def _inproj(x_main, x_tail, g, w, bg, ng):
    d = x_main.shape[1]
    tm = TOKEN_TILE
    n = x_main.shape[0] + tm
    nz = w.shape[1] - LANES
    full = lambda i: (0, 0)
    return pl.pallas_call(
        _inproj_kernel,
        grid=(n // tm,),
        in_specs=_stacked_specs(x_main, x_tail) + [
            pl.BlockSpec((1, d), full),
            pl.BlockSpec((d, nz + LANES), full),
            pl.BlockSpec((1, LANES), full),
        ],
        out_specs=[
            pl.BlockSpec((tm, nz), lambda i: (i, 0)),
            pl.BlockSpec((tm, LANES), lambda i: (i, 0)),
            pl.BlockSpec((ng, tm), lambda i: (0, i)),
        ],
        out_shape=[
            jax.ShapeDtypeStruct((n, nz), F32),
            jax.ShapeDtypeStruct((n, LANES), F32),
            jax.ShapeDtypeStruct((ng, n), F32),
        ],
        compiler_params=_params("parallel"),
        name="inproj",
    )(x_main, x_tail, g, w, bg)


def _pool_prompt_kernel(u_ref, pw_ref, ps_ref, y_ref, st_ref, ext_ref, *, tt, gw):
    t = pl.program_id(1)
    nt = pl.num_programs(1)
    halo = POOL_BUF + 1
    width = ext_ref.shape[1]

    @pl.when(t == 0)
    def _():
        ext_ref[0:halo, :] = jnp.zeros((halo, width), F32)

    u = u_ref[...]
    ext_ref[halo:halo + tt, :] = u
    pos = t * tt + lax.broadcasted_iota(I32, (tt, 1), 0)
    for g, w in enumerate(POOL_WINDOWS):
        c0 = g * gw
        acc = u[:, c0:c0 + gw]
        for j in range(1, w):
            acc = acc + ext_ref[halo - j:halo - j + tt, c0:c0 + gw]
        cnt = jnp.minimum(w, pos + 1).astype(F32)
        d = acc / cnt - u[:, c0:c0 + gw]
        y = _dot(d.astype(BF16), pw_ref[g]) * ps_ref[:, c0:c0 + gw]
        y_ref[:, c0:c0 + gw] = y.astype(BF16)

    @pl.when(t == nt - 1)
    def _():
        st_ref[...] = ext_ref[tt + 1:tt + halo, :]

    ext_ref[0:halo, :] = ext_ref[tt:tt + halo, :]


def _pool_prompt(z, pw, ps, batch, seq):
    width = ps.shape[1]
    gw = width // len(POOL_WINDOWS)
    tt = POOL_TIME_TILE
    nt = seq // tt
    return pl.pallas_call(
        functools.partial(_pool_prompt_kernel, tt=tt, gw=gw),
        grid=(batch, nt),
        in_specs=[
            pl.BlockSpec((tt, width), lambda b, t: (b * nt + t, 0)),
            pl.BlockSpec(pw.shape, lambda b, t: (0, 0, 0)),
            pl.BlockSpec((1, width), lambda b, t: (0, 0)),
        ],
        out_specs=[
            pl.BlockSpec((tt, width), lambda b, t: (b * nt + t, 0)),
            pl.BlockSpec((None, POOL_BUF, width), lambda b, t: (b, 0, 0)),
        ],
        out_shape=[
            jax.ShapeDtypeStruct((batch * seq, width), BF16),
            jax.ShapeDtypeStruct((batch, POOL_BUF, width), F32),
        ],
        scratch_shapes=[pltpu.VMEM((POOL_BUF + 1 + tt, width), F32)],
        compiler_params=_params("parallel", "arbitrary"),
        name="pool_prompt",
    )(z, pw, ps)


def _pool_step_kernel(u_ref, buf_ref, pw_ref, ps_ref, y_ref, nb_ref, *, gw):
    u = u_ref[...]
    for g, w in enumerate(POOL_WINDOWS):
        c0 = g * gw
        acc = u[:, c0:c0 + gw]
        for j in range(1, w):
            acc = acc + buf_ref[POOL_BUF - j, :, c0:c0 + gw]
        cnt = float(min(w, PAST_LEN + 1))
        d = acc / cnt - u[:, c0:c0 + gw]
        y = _dot(d.astype(BF16), pw_ref[g]) * ps_ref[:, c0:c0 + gw]
        y_ref[:, c0:c0 + gw] = y.astype(BF16)
    nb_ref[0:POOL_BUF - 1] = buf_ref[1:POOL_BUF]
    nb_ref[POOL_BUF - 1] = u


def _pool_step(z, buf_t, pw, ps, row0):
    _, batch, width = buf_t.shape
    gw = width // len(POOL_WINDOWS)
    return pl.pallas_call(
        functools.partial(_pool_step_kernel, gw=gw),
        grid=(1,),
        in_specs=[
            pl.BlockSpec((batch, width), lambda i: (row0 // batch, 0)),
            pl.BlockSpec(buf_t.shape, lambda i: (0, 0, 0)),
            pl.BlockSpec(pw.shape, lambda i: (0, 0, 0)),
            pl.BlockSpec((1, width), lambda i: (0, 0)),
        ],
        out_specs=[
            pl.BlockSpec((batch, width), lambda i: (0, 0)),
            pl.BlockSpec(buf_t.shape, lambda i: (0, 0, 0)),
        ],
        out_shape=[
            jax.ShapeDtypeStruct((batch, width), BF16),
            jax.ShapeDtypeStruct(buf_t.shape, F32),
        ],
        compiler_params=_params("arbitrary"),
        name="pool_step",
    )(z, buf_t, pw, ps)


def _mlstm_prompt_kernel(*refs, nh, dh, nseq):
    seq_in = [refs[5 * s:5 * s + 5] for s in range(nseq)]
    h_ref, c_out, n_out, m_out, c_s, n_s, m_s = refs[5 * nseq:]
    ci = pl.program_id(1)
    nc = pl.num_programs(1)
    ln = seq_in[0][0].shape[0]

    @pl.when(ci == 0)
    def _():
        c_s[...] = jnp.zeros(c_s.shape, F32)
        n_s[...] = jnp.zeros(n_s.shape, F32)
        m_s[...] = jnp.zeros(m_s.shape, F32)

    row = lax.broadcasted_iota(I32, (ln, ln), 0)
    col = lax.broadcasted_iota(I32, (ln, ln), 1)
    causal_t = row <= col
    eye = col == row
    lane = lax.broadcasted_iota(I32, (nh, ln), 1)

    def to_col(r):
        return jnp.sum(jnp.where(eye, r, 0.0), axis=1, keepdims=True)

    scale = dh ** -0.5
    pairs = [(s, h) for s in range(nseq) for h in range(nh)]
    c_old = {p: c_s[p[0], p[1]] for p in pairs}
    n_old = {p: n_s[p[0], p[1]] for p in pairs}
    m_old = {p: m_s[p[0], p[1]][:, 0:1] for p in pairs}
    gates = []
    for s in range(nseq):
        gt = seq_in[s][4][...]
        bc_all = jax.nn.log_sigmoid(gt[nh:2 * nh])
        sh = 1
        while sh < ln:
            bc_all = bc_all + jnp.where(lane >= sh, pltpu.roll(bc_all, sh, 1), 0.0)
            sh *= 2
        gates.append((gt[0:nh], bc_all))
    ch = {}
    for s, h in pairs:
        q_ref, k_ref, v_ref, _, _ = seq_in[s]
        li_all, bc_all = gates[s]
        sl = slice(h * dh, (h + 1) * dh)
        k = k_ref[:, sl] * scale
        v = v_ref[:, sl]
        li_r, bc_r = li_all[h:h + 1], bc_all[h:h + 1]
        b_end = bc_r[:, ln - 1:ln]
        g_r = b_end - bc_r + li_r
        m0 = m_old[s, h]
        m_new = jnp.maximum(b_end + m0, jnp.max(g_r, axis=1, keepdims=True))
        ch[s, h] = dict(
            qb=q_ref[:, sl].astype(BF16), kb=k.astype(BF16), vb=v.astype(BF16), v=v,
            inter=bc_r + m0, m_new=m_new, wg_r=jnp.exp(g_r - m_new),
            decay=jnp.exp(b_end + m0 - m_new),
            dmat_t=jnp.where(causal_t, bc_r - to_col(bc_r - li_r), -jnp.inf))
    for p in pairs:
        d = ch[p]
        d["st"] = _dot_nt(d["kb"], d["qb"])
        d["cq_t"] = _dot_nt(c_old[p].astype(BF16), d["qb"])
        d["nq"] = _dot_nt(jnp.broadcast_to(n_old[p], (SUBLANES, dh)).astype(BF16), d["qb"])[0:1]
        d["c_add"] = _dot((d["v"].T * d["wg_r"]).astype(BF16), d["kb"])
        d["n_add"] = _dot(jnp.broadcast_to(d["wg_r"], (SUBLANES, ln)).astype(BF16), d["kb"])[0:1]
    for p in pairs:
        d = ch[p]
        d["m_row"] = jnp.maximum(d["inter"], jnp.max(d["dmat_t"], axis=0, keepdims=True))
        d["w_inter"] = jnp.exp(d["inter"] - d["m_row"])
        d["sc_t"] = d["st"] * jnp.exp(d["dmat_t"] - d["m_row"])
    for p in pairs:
        d = ch[p]
        d["pv_t"] = _dot_tn(d["vb"], d["sc_t"].astype(BF16))
    new_state = {}
    for s, h in pairs:
        d = ch[s, h]
        sl = slice(h * dh, (h + 1) * dh)
        num_t = d["pv_t"] + d["w_inter"] * d["cq_t"]
        den = jnp.sum(d["sc_t"], axis=0, keepdims=True) + d["w_inter"] * d["nq"]
        hh = (num_t / jnp.maximum(jnp.abs(den), jnp.exp(-d["m_row"]))).T
        h_ref[s, :, sl] = (hh * jax.nn.sigmoid(seq_in[s][3][:, sl])).astype(BF16)
        new_state[s, h] = (d["decay"] * c_old[s, h] + d["c_add"],
                           d["decay"] * n_old[s, h] + d["n_add"],
                           jnp.broadcast_to(d["m_new"], (1, dh)))

    for (s, h), (c_new, n_new, m_new) in new_state.items():
        c_s[s, h] = c_new
        n_s[s, h] = n_new
        m_s[s, h] = m_new

    @pl.when(ci == nc - 1)
    def _():
        c_out[...] = c_s[...]
        n_out[...] = n_s[...]
        m_out[...] = m_s[...]


def _mlstm_prompt(z, gates_t, batch, seq, nh, dh):
    ln = MLSTM_CHUNK
    nc = seq // ln
    nseq = min(MLSTM_SEQ_PER_STEP, batch)
    assert batch % nseq == 0
    width = nh * dh
    ng = gates_t.shape[0]
    rows_of = lambda s: (lambda b, c: (b * nseq + s) * nc + c)
    in_specs, operands = [], []
    for s in range(nseq):
        r = rows_of(s)
        for j in (1, 2, 3, 4):
            in_specs.append(pl.BlockSpec((ln, width), lambda b, c, r=r, j=j: (r(b, c), j)))
        in_specs.append(pl.BlockSpec((ng, ln), lambda b, c, r=r: (0, r(b, c))))
        operands += [z, z, z, z, gates_t]
    h_spec = pl.BlockSpec((None, nseq, ln, width), lambda b, c: (b, 0, c, 0))
    st = lambda a, b_: pl.BlockSpec((nseq, nh, a, b_), lambda b, c: (b, 0, 0, 0))
    return pl.pallas_call(
        functools.partial(_mlstm_prompt_kernel, nh=nh, dh=dh, nseq=nseq),
        grid=(batch // nseq, nc),
        in_specs=in_specs,
        out_specs=[h_spec, st(dh, dh), st(1, dh), st(1, dh)],
        out_shape=[
            jax.ShapeDtypeStruct((batch // nseq, nseq, seq, width), BF16),
            jax.ShapeDtypeStruct((batch, nh, dh, dh), F32),
            jax.ShapeDtypeStruct((batch, nh, 1, dh), F32),
            jax.ShapeDtypeStruct((batch, nh, 1, dh), F32),
        ],
        scratch_shapes=[pltpu.VMEM((nseq, nh, dh, dh), F32), pltpu.VMEM((nseq, nh, 1, dh), F32),
                        pltpu.VMEM((nseq, nh, 1, dh), F32)],
        compiler_params=_params("parallel", "arbitrary"),
        name="mlstm_prompt",
    )(*operands)


def _mlstm_step_kernel(q_ref, k_ref, v_ref, o_ref, li_ref, fp_ref, c_ref, n_ref, m_ref,
                       h_ref, c_out, n_out, m_out, *, nh, dh):
    eye = (lax.broadcasted_iota(I32, (1, dh, dh), 1) == lax.broadcasted_iota(I32, (1, dh, dh), 2))
    scale = dh ** -0.5
    for h in range(nh):
        sl = slice(h * dh, (h + 1) * dh)
        q = q_ref[:, :, sl]
        k = k_ref[:, :, sl] * scale
        v = v_ref[:, :, sl]
        c = c_ref[:, h]
        n = n_ref[:, h]
        m = m_ref[:, h]
        li = li_ref[:, h]
        lf = jax.nn.log_sigmoid(fp_ref[:, h])
        inter = lf + m
        m_row = jnp.maximum(inter, li)
        w_intra = jnp.exp(li - m_row)
        w_inter = jnp.exp(inter - m_row)
        sc = jnp.sum(q * k, axis=-1, keepdims=True) * w_intra
        q8 = jnp.broadcast_to(q, (q.shape[0], SUBLANES, dh)).astype(BF16)
        cq = jnp.einsum("bqk,bvk->bqv", q8, c.astype(BF16),
                        preferred_element_type=F32)[:, 0:1, :]
        num = sc * v + w_inter * cq
        den = sc + w_inter * jnp.sum(n * q, axis=-1, keepdims=True)
        h_l = num / jnp.maximum(jnp.abs(den), jnp.exp(-m_row))
        h_ref[:, :, sl] = h_l * jax.nn.sigmoid(o_ref[:, :, sl])
        v_c = jnp.sum(jnp.where(eye, v, 0.0), axis=-1, keepdims=True)
        wg = jnp.exp(li - m_row)
        decay = jnp.exp(inter - m_row)
        c_out[:, h] = decay * c + (v_c * wg) * k
        n_out[:, h] = decay * n + wg * k
        m_out[:, h] = m_row


def _mlstm_step(z3, li, fp, c, n, m, nh, dh):
    batch = c.shape[0]
    bb = MLSTM_STEP_BATCH
    width = nh * dh
    blk = lambda j: pl.BlockSpec((bb, 1, width), lambda i: (i, 0, j))
    st4 = lambda a, b: pl.BlockSpec((bb, nh, a, b), lambda i: (i, 0, 0, 0))
    return pl.pallas_call(
        functools.partial(_mlstm_step_kernel, nh=nh, dh=dh),
        grid=(batch // bb,),
        in_specs=[blk(1), blk(2), blk(3), blk(4), st4(1, 1), st4(1, 1),
                  st4(dh, dh), st4(1, dh), st4(1, 1)],
        out_specs=[pl.BlockSpec((bb, 1, width), lambda i: (i, 0, 0)),
                   st4(dh, dh), st4(1, dh), st4(1, 1)],
        out_shape=[
            jax.ShapeDtypeStruct((batch, 1, width), F32),
            jax.ShapeDtypeStruct((batch, nh, dh, dh), F32),
            jax.ShapeDtypeStruct((batch, nh, 1, dh), F32),
            jax.ShapeDtypeStruct((batch, nh, 1, 1), F32),
        ],
        compiler_params=_params("parallel"),
        name="mlstm_step",
    )(z3, z3, z3, z3, li, fp, c, n, m)


def _split_bf16(a):
    hi = a.astype(BF16)
    return hi, (a - hi.astype(F32)).astype(BF16)


def _route(hn, wr_ref, br_ref, tril_ref, carry_ref, rinfo_ref, rinfo_t_ref):
    ngrp, epg = MOE_GROUPS, MOE_EXPERTS_PER_GROUP
    h_hi, h_lo = _split_bf16(hn)
    both = _dot(h_hi, wr_ref[...])
    logits = both[:, 0:LANES] + (both[:, LANES:2 * LANES] + _dot(h_lo, wr_ref[:, 0:LANES])) + br_ref[...]
    tm = logits.shape[0]
    lane = lax.broadcasted_iota(I32, (tm, LANES), 1)
    neg = -jnp.inf

    def first_max(x):
        mx = jnp.max(x, axis=1, keepdims=True)
        idx = jnp.min(jnp.where(x == mx, lane, LANES), axis=1, keepdims=True)
        return mx, idx

    is_grp = lane < ngrp
    gmax, gsel = first_max(jnp.where(is_grp, logits, neg))
    g_w = 1.0 / jnp.sum(jnp.where(is_grp, jnp.exp(logits - gmax), 0.0), axis=1, keepdims=True)
    lo = ngrp + gsel * epg
    el = jnp.where((lane >= lo) & (lane < lo + epg), logits, neg)
    v1, i1 = first_max(el)
    v2, i2 = first_max(jnp.where(lane == i1, neg, el))
    e2 = jnp.exp(v2 - v1)
    w1 = g_w / (1.0 + e2)
    w2 = g_w * e2 / (1.0 + e2)
    eid1 = i1 - ngrp
    eid2 = i2 - ngrp

    hit1 = lane == eid1
    hit2 = lane == eid2
    onehot = jnp.where(hit1 | hit2, 1.0, 0.0)
    carry = carry_ref[...]
    prefix = _dot(tril_ref[...], onehot.astype(BF16)) + carry
    rank1 = jnp.sum(jnp.where(hit1, prefix, 0.0), axis=1, keepdims=True)
    rank2 = jnp.sum(jnp.where(hit2, prefix, 0.0), axis=1, keepdims=True)
    carry_ref[...] = carry + jnp.sum(onehot, axis=0, keepdims=True)

    cols = (eid1.astype(F32), eid2.astype(F32), w1, w2, rank1, rank2)
    info = jnp.zeros((tm, LANES), F32)
    for j, cval in enumerate(cols):
        info = jnp.where(lane == j, cval, info)
    rinfo_ref[...] = info
    pick = jnp.where(lax.broadcasted_iota(I32, (SUBLANES, LANES), 0)
                     == lax.broadcasted_iota(I32, (SUBLANES, LANES), 1), 1.0, 0.0).astype(BF16)
    parts = [_dot_nt(pick, p) for p in _split3(info)]
    rinfo_t_ref[...] = parts[0] + (parts[1] + parts[2])


def _outproj_route_kernel(xm_ref, xt_ref, pm_ref, pt_ref, mm_ref, mt_ref,
                          wo_ref, g_ref, wr_ref, br_ref, tril_ref,
                          x1_ref, hn_ref, rinfo_ref, rinfo_t_ref, cnt_ref, carry_ref):
    @pl.when(pl.program_id(0) == 0)
    def _():
        carry_ref[...] = jnp.zeros(carry_ref.shape, F32)

    half = pm_ref.shape[1]
    mix = (_dot(_stacked(pm_ref, pt_ref), wo_ref[0:half, :])
           + _dot(_stacked(mm_ref, mt_ref), wo_ref[half:2 * half, :]))
    x1 = _stacked(xm_ref, xt_ref) + mix
    x1_ref[...] = x1
    hn = _rms(x1, g_ref[...])
    hn_ref[...] = hn
    _route(hn, wr_ref, br_ref, tril_ref, carry_ref, rinfo_ref, rinfo_t_ref)
    cnt_ref[...] = carry_ref[...]


def _glu_route_kernel(x_ref, ym_ref, yt_ref, wglu_ref, g_ref, wr_ref, br_ref, tril_ref,
                      x1_ref, hn_ref, rinfo_ref, rinfo_t_ref, cnt_ref, carry_ref):
    @pl.when(pl.program_id(0) == 0)
    def _():
        carry_ref[...] = jnp.zeros(carry_ref.shape, F32)

    d = x_ref.shape[1]
    ag = _dot(jax.nn.gelu(_stacked(ym_ref, yt_ref)).astype(BF16), wglu_ref[...])
    x1 = x_ref[...] + ag[:, 0:d] * jax.nn.sigmoid(ag[:, d:2 * d])
    x1_ref[...] = x1
    hn = _rms(x1, g_ref[...])
    hn_ref[...] = hn
    _route(hn, wr_ref, br_ref, tril_ref, carry_ref, rinfo_ref, rinfo_t_ref)
    cnt_ref[...] = carry_ref[...]


def _mix_route(kernel, name, n, row_specs, rows, w, g, wr, br, tril):
    d = g.shape[1]
    tm = TOKEN_TILE
    full = lambda i: (0, 0)
    return pl.pallas_call(
        kernel,
        grid=(n // tm,),
        in_specs=row_specs + [
            pl.BlockSpec(w.shape, full),
            pl.BlockSpec((1, d), full),
            pl.BlockSpec((d, 2 * LANES), full),
            pl.BlockSpec((1, LANES), full),
            pl.BlockSpec((tm, tm), full),
        ],
        out_specs=[
            pl.BlockSpec((tm, d), lambda i: (i, 0)),
            pl.BlockSpec((tm, d), lambda i: (i, 0)),
            pl.BlockSpec((tm, LANES), lambda i: (i, 0)),
            pl.BlockSpec((SUBLANES, tm), lambda i: (0, i)),
            pl.BlockSpec((1, LANES), full),
        ],
        out_shape=[
            jax.ShapeDtypeStruct((n, d), F32),
            jax.ShapeDtypeStruct((n, d), F32),
            jax.ShapeDtypeStruct((n, LANES), F32),
            jax.ShapeDtypeStruct((SUBLANES, n), F32),
            jax.ShapeDtypeStruct((1, LANES), F32),
        ],
        scratch_shapes=[pltpu.VMEM((1, LANES), F32)],
        compiler_params=_params("arbitrary"),
        name=name,
    )(*rows, w, g, wr, br, tril)


def _index_copy(pos_hbm, idx_s, sem_i, tile, slot):
    return pltpu.make_async_copy(pos_hbm.at[tile], idx_s.at[slot], sem_i.at[slot])


def _dispatch_kernel(pos_hbm, hn_hbm, xs_hbm, idx_s, tiles, sem_i, sem_l, sem_d):
    i = pl.program_id(0)
    nt = pl.num_programs(0)
    nslot, tm, _ = tiles.shape
    islot = i % 2

    def load(t):
        s = t % nslot
        return pltpu.make_async_copy(hn_hbm.at[pl.ds(t * tm, tm)], tiles.at[s], sem_l.at[s])

    def wait_rows(t):
        s = t % nslot
        whole = pltpu.make_async_copy(tiles.at[s], xs_hbm.at[pl.ds(0, tm)], sem_d.at[s])
        whole.wait()
        whole.wait()

    @pl.when(i == 0)
    def _():
        _index_copy(pos_hbm, idx_s, sem_i, 0, 0).start()
        load(0).start()

        @pl.when(nt > 1)
        def _():
            load(1).start()

    @pl.when(i >= 2)
    def _():
        wait_rows(i - 2)

    @pl.when(i + 2 < nt)
    def _():
        load(i + 2).start()

    _index_copy(pos_hbm, idx_s, sem_i, i, islot).wait()

    @pl.when(i + 1 < nt)
    def _():
        _index_copy(pos_hbm, idx_s, sem_i, i + 1, 1 - islot).start()

    load(i).wait()
    slot = i % nslot

    def issue(r, carry):
        row = tiles.at[slot, pl.ds(r, 1)]
        pltpu.make_async_copy(row, xs_hbm.at[pl.ds(idx_s[islot, 0, r], 1)],
                              sem_d.at[slot]).start(priority=0)
        pltpu.make_async_copy(row, xs_hbm.at[pl.ds(idx_s[islot, 0, tm + r], 1)],
                              sem_d.at[slot]).start(priority=1)
        return carry

    lax.fori_loop(0, tm, issue, 0, unroll=DMA_ISSUE_UNROLL)

    @pl.when(i == nt - 1)
    def _():
        @pl.when(nt > 1)
        def _():
            wait_rows(i - 1)

        wait_rows(i)


def _dispatch(pos_tiles, hn):
    n, d = hn.shape
    tm = TOKEN_TILE
    nslot = 4
    return pl.pallas_call(
        _dispatch_kernel,
        grid=(n // tm,),
        in_specs=[pl.BlockSpec(memory_space=pl.ANY), pl.BlockSpec(memory_space=pl.ANY)],
        out_specs=pl.BlockSpec(memory_space=pl.ANY),
        out_shape=jax.ShapeDtypeStruct((2 * n, d), F32),
        scratch_shapes=[pltpu.SMEM((2, 1, 2 * tm), I32), pltpu.VMEM((nslot, tm, d), F32),
                        pltpu.SemaphoreType.DMA((2,)), pltpu.SemaphoreType.DMA((nslot,)),
                        pltpu.SemaphoreType.DMA((nslot,))],
        compiler_params=_params("arbitrary"),
        name="moe_dispatch",
    )(pos_tiles, hn)


def _moe_kernel(vt_ref, ve_ref, von_ref, vnext_ref, vslot_ref, lo_ref, hi_ref,
                xs_ref, wg_hbm, wu_hbm, wd_hbm, eo_ref,
                wg_f, wu_f, wd_f, wgb, wub, wdb, sem_w, *, layer):
    v = pl.program_id(0)
    tr = xs_ref.shape[0]
    prev = jnp.maximum(v - 1, 0)
    e = ve_ref[v]
    new_expert = jnp.logical_or(v == 0, e != ve_ref[prev])
    first_visit = jnp.logical_or(v == 0, vt_ref[v] != vt_ref[prev])

    def fetch(expert, slot):
        return [pltpu.make_async_copy(src.at[layer, expert], dst.at[slot], sem_w.at[slot, j])
                for j, (src, dst) in enumerate(((wg_hbm, wg_f), (wu_hbm, wu_f), (wd_hbm, wd_f)))]

    @pl.when(von_ref[v] == 1)
    def _():
        @pl.when(new_expert)
        def _():
            slot = vslot_ref[v]

            @pl.when(v == 0)
            def _():
                for cp in fetch(e, slot):
                    cp.start()

            for cp in fetch(e, slot):
                cp.wait()
            wgb[...] = wg_f[slot].astype(BF16)
            wub[...] = wu_f[slot].astype(BF16)
            wdb[...] = wd_f[slot].astype(BF16)

            @pl.when(vnext_ref[v] >= 0)
            def _():
                for cp in fetch(vnext_ref[v], 1 - slot):
                    cp.start()

        x = xs_ref[...].astype(BF16)
        act = jax.nn.silu(_dot(x, wgb[...])) * _dot(x, wub[...])
        row = vt_ref[v] * tr + lax.broadcasted_iota(I32, (tr, 1), 0)
        act = jnp.where((row >= lo_ref[e]) & (row < hi_ref[e]), act, 0.0)
        res = _dot(act.astype(BF16), wdb[...])

        @pl.when(first_visit)
        def _():
            eo_ref[...] = res

        @pl.when(jnp.logical_not(first_visit))
        def _():
            eo_ref[...] += res


def _moe_experts(vt, ve, von, vnext, vslot, lo, hi, xs, wg, wu, wd, layer):
    nv = vt.shape[0]
    rows, d = xs.shape
    hid = wg.shape[3]
    tr = EXPERT_ROW_TILE
    tile = pl.BlockSpec((tr, d), lambda v, vt, *_: (vt[v], 0))
    grid_spec = pltpu.PrefetchScalarGridSpec(
        num_scalar_prefetch=7,
        grid=(nv,),
        in_specs=[tile] + [pl.BlockSpec(memory_space=pl.ANY)] * 3,
        out_specs=tile,
        scratch_shapes=[
            pltpu.VMEM((2, d, hid), F32), pltpu.VMEM((2, d, hid), F32), pltpu.VMEM((2, hid, d), F32),
            pltpu.VMEM((d, hid), BF16), pltpu.VMEM((d, hid), BF16), pltpu.VMEM((hid, d), BF16),
            pltpu.SemaphoreType.DMA((2, 3)),
        ],
    )
    return pl.pallas_call(
        functools.partial(_moe_kernel, layer=layer),
        grid_spec=grid_spec,
        out_shape=jax.ShapeDtypeStruct((rows, d), F32),
        compiler_params=_params("arbitrary"),
        name="moe_experts",
    )(vt, ve, von, vnext, vslot, lo, hi, xs, wg, wu, wd)


def _combine_kernel(pos_hbm, eo_hbm, x_ref, rinfo_ref, g_ref, o1_ref, o2_ref, idx_s, a_buf, b_buf,
                    sem_i, sem_a, sem_b, *, last_layer):
    i = pl.program_id(0)
    nt = pl.num_programs(0)
    tm = a_buf.shape[1]
    slot = i % 2

    def gathers(s):
        def issue(r, carry):
            pltpu.make_async_copy(eo_hbm.at[pl.ds(idx_s[s, 0, r], 1)],
                                  a_buf.at[s, pl.ds(r, 1)], sem_a.at[s]).start(priority=0)
            pltpu.make_async_copy(eo_hbm.at[pl.ds(idx_s[s, 0, tm + r], 1)],
                                  b_buf.at[s, pl.ds(r, 1)], sem_b.at[s]).start(priority=1)
            return carry

        lax.fori_loop(0, tm, issue, 0, unroll=DMA_ISSUE_UNROLL)

    @pl.when(i == 0)
    def _():
        first = _index_copy(pos_hbm, idx_s, sem_i, 0, 0)
        first.start()
        first.wait()
        gathers(0)

        @pl.when(nt > 1)
        def _():
            _index_copy(pos_hbm, idx_s, sem_i, 1, 1).start()

    @pl.when(i + 1 < nt)
    def _():
        _index_copy(pos_hbm, idx_s, sem_i, i + 1, 1 - slot).wait()
        gathers(1 - slot)

    @pl.when(i + 2 < nt)
    def _():
        _index_copy(pos_hbm, idx_s, sem_i, i + 2, slot).start()

    pltpu.make_async_copy(eo_hbm.at[pl.ds(0, tm)], a_buf.at[slot], sem_a.at[slot]).wait()
    pltpu.make_async_copy(eo_hbm.at[pl.ds(0, tm)], b_buf.at[slot], sem_b.at[slot]).wait()
    info = rinfo_ref[...]
    x2 = x_ref[...] + (info[:, 2:3] * a_buf[slot] + info[:, 3:4] * b_buf[slot])
    hn = _rms(x2, g_ref[...])
    if last_layer:
        @pl.when(i < nt - 1)
        def _():
            o1_ref[...] = hn

        @pl.when(i == nt - 1)
        def _():
            o2_ref[...] = hn
    else:
        o1_ref[...] = x2
        o2_ref[...] = hn


def _combine(pos_tiles, eo, x, rinfo, g, last_layer):
    n, d = x.shape
    tm = TOKEN_TILE
    nt = n // tm
    row = pl.BlockSpec((tm, d), lambda i: (i, 0))
    if last_layer:
        out_specs = [pl.BlockSpec((tm, d), lambda i: (jnp.minimum(i, nt - 2), 0)),
                     pl.BlockSpec((tm, d), lambda i: (0, 0))]
        out_shape = [jax.ShapeDtypeStruct((n - tm, d), F32), jax.ShapeDtypeStruct((tm, d), F32)]
    else:
        out_specs = [row, row]
        out_shape = [jax.ShapeDtypeStruct((n, d), F32), jax.ShapeDtypeStruct((n, d), F32)]
    return pl.pallas_call(
        functools.partial(_combine_kernel, last_layer=last_layer),
        grid=(nt,),
        in_specs=[
            pl.BlockSpec(memory_space=pl.ANY),
            pl.BlockSpec(memory_space=pl.ANY),
            row,
            pl.BlockSpec((tm, LANES), lambda i: (i, 0)),
            pl.BlockSpec((1, d), lambda i: (0, 0)),
        ],
        out_specs=out_specs,
        out_shape=out_shape,
        scratch_shapes=[
            pltpu.SMEM((2, 1, 2 * tm), I32),
            pltpu.VMEM((2, tm, d), F32),
            pltpu.VMEM((2, tm, d), F32),
            pltpu.SemaphoreType.DMA((2,)),
            pltpu.SemaphoreType.DMA((2,)),
            pltpu.SemaphoreType.DMA((2,)),
        ],
        compiler_params=_params("arbitrary"),
        name="moe_combine",
    )(pos_tiles, eo, x, rinfo, g)


def _moe(x1, hn, rinfo, rinfo_t, counts, wg, wu, wd, layer, next_gain, last_layer):
    n = x1.shape[0]
    ne = wg.shape[1]
    tr = EXPERT_ROW_TILE
    tm = TOKEN_TILE
    nv = (2 * n) // tr + ne - 1
    eid = rinfo_t[0:2].astype(I32)
    rank = rinfo_t[4:6].astype(I32)
    cnt = counts[0, :ne].astype(I32)
    seg_end = jnp.cumsum(cnt)
    seg_start = seg_end - cnt
    experts = jnp.arange(ne, dtype=I32)
    pos = rank + jnp.sum(jnp.where(eid[:, None, :] == experts[None, :, None],
                                   seg_start[None, :, None], 0), axis=1)
    pos_tiles = pos.reshape(2, n // tm, tm).transpose(1, 0, 2).reshape(n // tm, 1, 2 * tm)

    first_tile = seg_start // tr
    tiles_e = jnp.where(cnt > 0, (seg_end - 1) // tr - first_tile + 1, 0)
    v_end = jnp.cumsum(tiles_e)
    v_start = v_end - tiles_e
    total = v_end[-1]
    vis = jnp.arange(nv, dtype=I32)
    vc = jnp.minimum(vis, jnp.maximum(total - 1, 0))
    ve = jnp.sum((vc[:, None] >= v_end[None, :]).astype(I32), axis=1)
    pick = lambda tab: jnp.sum(jnp.where(ve[:, None] == experts, tab, 0), axis=1)
    vt = pick(first_tile) + (vc - pick(v_start))
    von = (vis < total).astype(I32)
    present = cnt > 0
    later = present[None, :] & (experts[None, :] > experts[:, None])
    next_of = jnp.min(jnp.where(later, experts[None, :], ne), axis=1)
    vnext = pick(jnp.where(next_of < ne, next_of, -1))
    vslot = pick(jnp.cumsum(present.astype(I32)) - 1) & 1

    xs = _dispatch(pos_tiles, hn)
    eo = _moe_experts(vt, ve, von, vnext, vslot, seg_start, seg_end, xs, wg, wu, wd, layer)
    return _combine(pos_tiles, eo, x1, rinfo, next_gain, last_layer)


def _cis(log_mag, ang):
    mag = jnp.exp(log_mag)
    return mag * jnp.cos(ang), mag * jnp.sin(ang)


def _split3(a):
    p1 = a.astype(BF16)
    r1 = a - p1.astype(F32)
    p2 = r1.astype(BF16)
    return p1, p2, (r1 - p2.astype(F32)).astype(BF16)


def _select_dot(a, sel, sel_first=False):
    sel = sel.astype(BF16)
    parts = [(_dot(sel, p) if sel_first else _dot(p, sel)) for p in _split3(a)]
    return parts[0] + (parts[1] + parts[2])


def _dot3(a, b):
    a_hi, a_lo = _split_bf16(a)
    b_hi, b_lo = _split_bf16(b)
    return _dot(a_hi, b_hi) + (_dot(a_lo, b_hi) + _dot(a_hi, b_lo))


def _s5_prep_kernel(lsc_ref, lsr_ref, lrc_ref, lic_ref, lrr_ref, lir_ref, ctre_ref, ctim_ref,
                    btre_ref, btim_ref, d_ref,
                    t_ref, wre_ref, wim_ref, cyre_ref, cyim_ref, apr_ref, api_ref,
                    bbr_ref, bbi_ref, lbr_ref, lbi_ref, *, gw, gpb, sub):
    blk = gpb * gw
    wide = sub * blk
    nsl = lrc_ref.shape[1]
    ns = nsl // gpb
    sh_gw, sh_ns = gw.bit_length() - 1, ns.bit_length() - 1
    dt_c, dt_r = jnp.exp(lsc_ref[0]), jnp.exp(lsr_ref[0])
    ldt_c_re, ldt_c_im = lrc_ref[0] * dt_c, lic_ref[0] * dt_c
    ldt_r_re, ldt_r_im = lrr_ref[0] * dt_r, lir_ref[0] * dt_r

    lane_w = lax.broadcasted_iota(I32, (1, wide), 1)
    spread = jnp.where((lax.broadcasted_iota(I32, (gw, wide), 1) & (gw - 1))
                       == lax.broadcasted_iota(I32, (gw, wide), 0), 1.0, 0.0)
    cre = _select_dot(ctre_ref[0], spread)
    cim = _select_dot(ctim_ref[0], spread)
    same = (jnp.right_shift(lax.broadcasted_iota(I32, (nsl, 1), 0), sh_ns)
            == (jnp.right_shift(lane_w, sh_gw) & (gpb - 1)))
    assert 2 * blk == LANES
    tau = lax.broadcasted_iota(I32, (1, LANES), 1).astype(F32)
    pw_re, pw_im = _cis(tau * ldt_c_re, tau * ldt_c_im)
    low = lax.broadcasted_iota(I32, (nsl, LANES), 1) < blk

    def spread_pow(p, first):
        col = lambda t: jnp.broadcast_to(p[:, t:t + 1], (nsl, LANES))
        return jnp.concatenate([jnp.where(low, col(first + 2 * m), col(first + 2 * m + 1))
                                for m in range(sub // 2)], axis=1)

    def c_lam_pow(first):
        pr, pi = spread_pow(pw_re, first), spread_pow(pw_im, first)
        return (jnp.where(same, pr * cre - pi * cim, 0.0),
                jnp.where(same, -(pr * cim + pi * cre), 0.0))

    clr0, cli0 = c_lam_pow(0)
    clr1, cli1 = c_lam_pow(1)
    cyre_ref[0] = clr1.astype(BF16)
    cyim_ref[0] = cli1.astype(BF16)

    lbr, lbi = _cis(ldt_r_re, ldt_r_im)
    lbr_ref[0] = lbr
    lbi_ref[0] = lbi
    lr, li = lrr_ref[0], lir_ref[0]
    nr, ni = lbr - 1.0, lbi
    den = lr * lr + li * li
    fr = (nr * lr + ni * li) / den
    fi = (ni * lr - nr * li) / den
    bre, bim = btre_ref[0], btim_ref[0]
    bbr = fr * bre - fi * bim
    bbi = fr * bim + fi * bre
    bbr_ref[0] = bbr
    bbi_ref[0] = bbi

    lane_t = lax.broadcasted_iota(I32, (gw, wide), 1)
    chan = lax.broadcasted_iota(I32, (gw, wide), 0)
    for g in range(gpb):
        ps = slice(g * ns, (g + 1) * ns)
        r = _dot3(bbr[:, ps], clr0[ps, :]) + _dot3(bbi[:, ps], cli0[ps, :])
        r = r + jnp.where(lane_t == g * gw + chan, d_ref[0][:, g:g + 1], 0.0)
        for j in range(sub):
            tb = r if j == 0 else jnp.where(lane_t >= blk * j, pltpu.roll(r, blk * j, 1), 0.0)
            r0 = j * blk + g * gw
            t_ref[0, r0:r0 + gw, :] = tb.astype(BF16)

    rows = lax.broadcasted_iota(I32, (wide, 1), 0)
    spread_t = jnp.where((lax.broadcasted_iota(I32, (wide, gw), 0) & (gw - 1))
                         == lax.broadcasted_iota(I32, (wide, gw), 1), 1.0, 0.0)
    bbr_t = _select_dot(bbr, spread_t, sel_first=True)
    bbi_t = _select_dot(bbi, spread_t, sel_first=True)
    same_w = ((jnp.right_shift(rows, sh_gw) & (gpb - 1))
              == jnp.right_shift(lax.broadcasted_iota(I32, (1, nsl), 1), sh_ns))
    rj = ((sub - 1) - lax.broadcasted_iota(I32, (sub, 1), 0)).astype(F32)
    q_re, q_im = _cis(rj * ldt_r_re, rj * ldt_r_im)
    per_step = lambda q: jnp.concatenate(
        [jnp.broadcast_to(q[j:j + 1, :], (blk, nsl)) for j in range(sub)], axis=0)
    pr, pi = per_step(q_re), per_step(q_im)
    wre_ref[0] = jnp.where(same_w, pr * bbr_t - pi * bbi_t, 0.0).astype(BF16)
    wim_ref[0] = jnp.where(same_w, pr * bbi_t + pi * bbr_t, 0.0).astype(BF16)

    nlev = apr_ref.shape[1]
    pw = (sub * jnp.left_shift(1, lax.broadcasted_iota(I32, (nlev, 1), 0))).astype(F32)
    apr, api = _cis(pw * ldt_r_re, pw * ldt_r_im)
    apr_ref[0] = apr
    api_ref[0] = api


def _s5_prep(log_step, lam_re, lam_im, b_re, b_im, c_re, c_im, d_skip):
    ng, ns = lam_re.shape
    gw = b_re.shape[2]
    sub, gpb = S5_SUB, S5_GROUPS_PER_BLOCK
    nblk = ng // gpb
    nsl = gpb * ns
    wide = sub * gpb * gw
    ls = jnp.repeat(log_step, ns)
    ct = lambda c: jnp.transpose(c, (0, 2, 1)).reshape(nblk, nsl, gw)
    bt = lambda b: jnp.transpose(b.reshape(nblk, gpb, ns, gw), (0, 3, 1, 2)).reshape(nblk, gw, nsl)
    ins = [ls.reshape(nblk, nsl, 1), ls.reshape(nblk, 1, nsl),
           lam_re.reshape(nblk, nsl, 1), lam_im.reshape(nblk, nsl, 1),
           lam_re.reshape(nblk, 1, nsl), lam_im.reshape(nblk, 1, nsl),
           ct(c_re), ct(c_im), bt(b_re), bt(b_im),
           jnp.transpose(d_skip.reshape(nblk, gpb, gw), (0, 2, 1))]
    spec = lambda a: pl.BlockSpec((1,) + a.shape[1:], lambda g: (g, 0, 0))
    outs = [
        jax.ShapeDtypeStruct((nblk, wide, wide), BF16),
        jax.ShapeDtypeStruct((nblk, wide, nsl), BF16),
        jax.ShapeDtypeStruct((nblk, wide, nsl), BF16),
        jax.ShapeDtypeStruct((nblk, nsl, wide), BF16),
        jax.ShapeDtypeStruct((nblk, nsl, wide), BF16),
        jax.ShapeDtypeStruct((nblk, SUBLANES, nsl), F32),
        jax.ShapeDtypeStruct((nblk, SUBLANES, nsl), F32),
        jax.ShapeDtypeStruct((nblk, gw, nsl), F32),
        jax.ShapeDtypeStruct((nblk, gw, nsl), F32),
        jax.ShapeDtypeStruct((nblk, 1, nsl), F32),
        jax.ShapeDtypeStruct((nblk, 1, nsl), F32),
    ]
    return pl.pallas_call(
        functools.partial(_s5_prep_kernel, gw=gw, gpb=gpb, sub=sub),
        grid=(nblk,),
        in_specs=[spec(a) for a in ins],
        out_specs=[spec(o) for o in outs],
        out_shape=outs,
        compiler_params=_params("parallel"),
        name="s5_prep",
    )(*ins)


def _s5_prompt_kernel(x_ref, t_ref, wre_ref, wim_ref, cyre_ref, cyim_ref, apr_ref, api_ref,
                      y_ref, hre_out, him_out, u_s, y4_s, *, nb, seq, sub):
    nk = seq // sub
    rows = nb * nk
    hl = LANES // 2
    nlev = nk.bit_length() - 1
    nsl = wre_ref.shape[2]
    low = lax.broadcasted_iota(I32, (nk, LANES), 1) < hl
    kidx = lax.broadcasted_iota(I32, (rows, 1), 0) & (nk - 1)

    def step_rows(b, j):
        return (pl.ds(b * seq + j, nk, stride=sub), slice(None))

    for b in range(nb):
        rs = slice(b * nk, (b + 1) * nk)
        for m in range(sub // 2):
            ls = slice(m * LANES, (m + 1) * LANES)
            s0 = x_ref[step_rows(b, 2 * m)]
            s1 = x_ref[step_rows(b, 2 * m + 1)]
            u_s[0, rs, ls] = jnp.where(low, s0, pltpu.roll(s1, hl, 1)).astype(BF16)
            u_s[1, rs, ls] = jnp.where(low, pltpu.roll(s0, hl, 1), s1).astype(BF16)

    def shifted(a, s):
        return jnp.where(kidx >= s, pltpu.roll(a, s, 0), 0.0)

    halves = range(2)
    us = [u_s[h] for h in halves]
    hre = [_dot(us[h], wre_ref[h]) for h in halves]
    him = [_dot(us[h], wim_ref[h]) for h in halves]
    cw = 2 * LANES
    y_conv = [jnp.concatenate(
        [_dot(us[h][:, 0:c0 + cw], t_ref[h, 0:c0 + cw, c0:c0 + cw]) for c0 in range(0, us[h].shape[1], cw)],
        axis=1) for h in halves]
    for i in range(nlev):
        for h in halves:
            ar = apr_ref[h, i:i + 1, :]
            ai = api_ref[h, i:i + 1, :]
            pre, pim = shifted(hre[h], 1 << i), shifted(him[h], 1 << i)
            hre[h], him[h] = hre[h] + (ar * pre - ai * pim), him[h] + (ar * pim + ai * pre)
    for h in halves:
        ls = slice(h * nsl, (h + 1) * nsl)
        for b in range(nb):
            last = (b + 1) * nk - 1
            hre_out[b:b + 1, ls] = hre[h][last:last + 1, :]
            him_out[b:b + 1, ls] = him[h][last:last + 1, :]
        hpre = shifted(hre[h], 1).astype(BF16)
        hpim = shifted(him[h], 1).astype(BF16)
        y4_s[h] = y_conv[h] + _dot(hpre, cyre_ref[h]) + _dot(hpim, cyim_ref[h])

    for b in range(nb):
        rs = slice(b * nk, (b + 1) * nk)
        for m in range(sub // 2):
            ls = slice(m * LANES, (m + 1) * LANES)
            ca = y4_s[0, rs, ls]
            cb = y4_s[1, rs, ls]
            y_ref[step_rows(b, 2 * m)] = jnp.where(low, ca, pltpu.roll(cb, hl, 1))
            y_ref[step_rows(b, 2 * m + 1)] = jnp.where(low, pltpu.roll(ca, hl, 1), cb)


def _s5_prompt(h, prep, batch, seq):
    d = h.shape[1]
    tmat, wre, wim, cyre, cyim, apr, api = prep[:7]
    nblk, wide, nsl = wre.shape
    sub, nb = S5_SUB, S5_BATCH_PER_STEP
    nk = seq // sub
    assert nk & (nk - 1) == 0 and nk.bit_length() - 1 <= apr.shape[1]
    ntile = d // LANES
    assert nblk == 2 * ntile
    rows = nb * nk
    wspec = lambda a: pl.BlockSpec((2,) + a.shape[1:], lambda t, b: (t, 0, 0))
    xspec = pl.BlockSpec((nb * seq, LANES), lambda t, b: (b, t))
    sspec = pl.BlockSpec((None, nb, 2 * nsl), lambda t, b: (b, 0, t))
    return pl.pallas_call(
        functools.partial(_s5_prompt_kernel, nb=nb, seq=seq, sub=sub),
        grid=(ntile, batch // nb),
        in_specs=[xspec, wspec(tmat), wspec(wre), wspec(wim), wspec(cyre), wspec(cyim),
                  wspec(apr), wspec(api)],
        out_specs=[xspec, sspec, sspec],
        out_shape=[
            jax.ShapeDtypeStruct((batch * seq, d), F32),
            jax.ShapeDtypeStruct((batch // nb, nb, ntile * 2 * nsl), F32),
            jax.ShapeDtypeStruct((batch // nb, nb, ntile * 2 * nsl), F32),
        ],
        scratch_shapes=[
            pltpu.VMEM((2, rows, wide), BF16),
            pltpu.VMEM((2, rows, wide), F32),
        ],
        compiler_params=_params("parallel", "parallel"),
        name="s5_prompt",
    )(h, tmat, wre, wim, cyre, cyim, apr, api)


def _s5_step_kernel(u_ref, hre_ref, him_ref, bbr_ref, bbi_ref, lbr_ref, lbi_ref, cre_ref, cim_ref,
                    d_ref, nre_ref, nim_ref, y_ref):
    for g in range(u_ref.shape[0]):
        u = u_ref[g]
        hre, him, lbr, lbi = hre_ref[g], him_ref[g], lbr_ref[g], lbi_ref[g]
        nre = lbr * hre - lbi * him + _dot3(u, bbr_ref[g])
        nim = lbr * him + lbi * hre + _dot3(u, bbi_ref[g])
        nre_ref[g] = nre
        nim_ref[g] = nim
        y_ref[g] = _dot3(nre, cre_ref[g]) - _dot3(nim, cim_ref[g]) + d_ref[g] * u


def _s5_step(u_s, h_re, h_im, bbr, bbi, lbr, lbi, cre_t, cim_t, d_g):
    ng = u_s.shape[0]
    gb = SUBLANES
    spec = lambda a: pl.BlockSpec((gb,) + a.shape[1:], lambda g: (g, 0, 0))
    ins = [u_s, h_re, h_im, bbr, bbi, lbr, lbi, cre_t, cim_t, d_g]
    outs = [jax.ShapeDtypeStruct(h_re.shape, F32), jax.ShapeDtypeStruct(h_re.shape, F32),
            jax.ShapeDtypeStruct(u_s.shape, F32)]
    return pl.pallas_call(
        _s5_step_kernel,
        grid=(ng // gb,),
        in_specs=[spec(a) for a in ins],
        out_specs=[spec(o) for o in outs],
        out_shape=outs,
        compiler_params=_params("parallel"),
        name="s5_step",
    )(*ins)


def _pad_to(a, axis, size):
    pad = [(0, 0)] * a.ndim
    pad[axis] = (0, size - a.shape[axis])
    return jnp.pad(a, pad)


def kernel(x_prompt, x_sample, state_pool, state_mlstm_c, state_mlstm_n, state_mlstm_m, state_s5_re, state_s5_im, norm_mix, norm_ffn, norm_final, w_in_ab, b_gates, pool_w, pool_scale, w_out_ab, s5_lam_re, s5_lam_im, s5_log_step, s5_b_re, s5_b_im, s5_c_re, s5_c_im, s5_d, w_glu, moe_w_group, moe_b_group, moe_w_expert, moe_b_expert, moe_w_gate, moe_w_up, moe_w_down):
    bp, tp, d = x_prompt.shape
    bs = x_sample.shape[0]
    n_p = bp * tp
    tm = TOKEN_TILE
    assert n_p % tm == 0 and bs <= tm
    nh, dh = state_mlstm_c.shape[2], state_mlstm_c.shape[3]
    pool_width = state_pool.shape[3]
    ml_width = nh * dh
    n_main = pool_width + 4 * ml_width
    n_gates = 2 * nh
    ngrp_s5, n_state = s5_lam_re.shape[1], s5_lam_re.shape[2]
    gw_s5 = d // ngrp_s5

    n = n_p + tm

    def tail_tile(sample_rows, dtype):
        return _pad_to(sample_rows.astype(dtype), 0, tm)

    x_main = x_prompt.reshape(n_p, d)
    x_tail = tail_tile(x_sample.reshape(bs, d), F32)
    tril = jnp.tril(jnp.ones((tm, tm), BF16), -1)

    def router_weights(l):
        wr = _pad_to(jnp.concatenate([moe_w_group[l], moe_w_expert[l]], axis=1), 1, LANES)
        br = jnp.concatenate([moe_b_group[l], moe_b_expert[l]])[None, :]
        hi = wr.astype(BF16)
        lo = (wr - hi.astype(F32)).astype(BF16)
        return jnp.concatenate([hi, lo], axis=1), _pad_to(br, 1, LANES)

    w_in = w_in_ab[0]
    z, gates, gates_t = _inproj(
        x_main, x_tail, norm_mix[0][None, :], _pad_to(w_in, 1, n_main + LANES).astype(BF16),
        _pad_to(b_gates[0][None, :], 1, LANES), n_gates)

    pw = pool_w[0].astype(BF16)
    ps = pool_scale[0][None, :]
    pool_y_p, pool_p = _pool_prompt(z, pw, ps, bp, tp)
    pool_y_s, pool_s_t = _pool_step(z, jnp.transpose(state_pool[0], (1, 0, 2)), pw, ps, n_p)
    pool_s = jnp.transpose(pool_s_t, (1, 0, 2))

    ml_y_p, c_p, n_p_st, m_p = _mlstm_prompt(z, gates_t, bp, tp, nh, dh)
    ml_y_p = ml_y_p.reshape(n_p, ml_width)
    g_s = gates[n_p:n_p + bs, :n_gates]
    ml_y_s, c_s, n_s_st, m_s = _mlstm_step(
        z[n_p:n_p + bs].reshape(bs, 1, n_main),
        g_s[:, :nh].reshape(bs, nh, 1, 1), g_s[:, nh:].reshape(bs, nh, 1, 1),
        state_mlstm_c[0], state_mlstm_n[0].reshape(bs, nh, 1, dh),
        state_mlstm_m[0].reshape(bs, nh, 1, 1), nh, dh)

    rows = [x_main, x_tail, pool_y_p, tail_tile(pool_y_s, BF16),
            ml_y_p, tail_tile(ml_y_s.reshape(bs, ml_width), BF16)]
    specs = (_stacked_specs(x_main, x_tail) + _stacked_specs(pool_y_p, rows[3])
             + _stacked_specs(ml_y_p, rows[5]))
    wr, br = router_weights(0)
    x1, hn, rinfo, rinfo_t, counts = _mix_route(
        _outproj_route_kernel, "outproj_route", n, specs, rows, w_out_ab[0].astype(BF16),
        norm_ffn[0][None, :], wr, br, tril)
    x2, h1 = _moe(x1, hn, rinfo, rinfo_t, counts, moe_w_gate, moe_w_up, moe_w_down, 0,
                  norm_mix[1][None, :], last_layer=False)

    prep = _s5_prep(s5_log_step[0], s5_lam_re[0], s5_lam_im[0], s5_b_re[0], s5_b_im[0],
                    s5_c_re[0], s5_c_im[0], s5_d[0])
    y_p, hre_p, him_p = _s5_prompt(h1, prep, bp, tp)
    s5_re_p = hre_p.reshape(bp, ngrp_s5, n_state)
    s5_im_p = him_p.reshape(bp, ngrp_s5, n_state)

    gpb = S5_GROUPS_PER_BLOCK
    per_group = lambda a: jnp.transpose(
        a.reshape(ngrp_s5 // gpb, gw_s5, gpb, n_state), (0, 2, 1, 3)).reshape(ngrp_s5, gw_s5, n_state)
    bbr, bbi = per_group(prep[7]), per_group(prep[8])
    lbr, lbi = prep[9].reshape(ngrp_s5, 1, n_state), prep[10].reshape(ngrp_s5, 1, n_state)
    d_g = s5_d[0].reshape(ngrp_s5, 1, gw_s5)
    u_s = h1[n_p:n_p + bs].reshape(bs, ngrp_s5, gw_s5).transpose(1, 0, 2)
    h_re = jnp.transpose(state_s5_re[0], (1, 0, 2))
    h_im = jnp.transpose(state_s5_im[0], (1, 0, 2))
    nre_s, nim_s, y_s = _s5_step(
        u_s, h_re, h_im, bbr, bbi, lbr, lbi,
        jnp.transpose(s5_c_re[0], (0, 2, 1)), jnp.transpose(s5_c_im[0], (0, 2, 1)), d_g)
    s5_re_s = jnp.transpose(nre_s, (1, 0, 2))
    s5_im_s = jnp.transpose(nim_s, (1, 0, 2))
    y_tail = tail_tile(jnp.transpose(y_s, (1, 0, 2)).reshape(bs, d), F32)

    wr, br = router_weights(1)
    specs = [pl.BlockSpec((tm, d), lambda i: (i, 0))] + _stacked_specs(y_p, y_tail)
    x3, hn, rinfo, rinfo_t, counts = _mix_route(
        _glu_route_kernel, "glu_route", n, specs, [x2, y_p, y_tail], w_glu[0].astype(BF16),
        norm_ffn[1][None, :], wr, br, tril)
    y_main, y_last = _moe(x3, hn, rinfo, rinfo_t, counts, moe_w_gate, moe_w_up, moe_w_down, 1,
                          norm_final[None, :], last_layer=True)

    return (y_main.reshape(bp, tp, d), y_last[:bs].reshape(bs, 1, d),
            pool_p[None], c_p[None], n_p_st.reshape(1, bp, nh, dh), m_p[:, :, 0, 0][None],
            s5_re_p[None], s5_im_p[None],
            pool_s[None], c_s[None], n_s_st.reshape(1, bs, nh, dh), m_s.reshape(1, bs, nh),
            s5_re_s[None], s5_im_s[None])
```

```python
import functools

import jax
import jax.numpy as jnp
from jax import lax
from jax.experimental import pallas as pl
from jax.experimental.pallas import tpu as pltpu

F32 = jnp.float32
BF16 = jnp.bfloat16
I32 = jnp.int32

PAST_LEN = 16384
POOL_WINDOWS = (2, 4, 8, 16)
POOL_BUF = max(POOL_WINDOWS) - 1
MLSTM_CHUNK = 128
S5_SUB = 16
MOE_GROUPS = 4
MOE_EXPERTS_PER_GROUP = 8
RMS_EPS = 1e-6

LANES = 128
SUBLANES = 8
VMEM_LIMIT_BYTES = 56 * 1024 * 1024

TOKEN_TILE = 512
EXPERT_ROW_TILE = 512
MLSTM_SEQ_PER_STEP = 8
MLSTM_STEP_BATCH = 16
POOL_TIME_TILE = 1024
DMA_ISSUE_UNROLL = 8
S5_GROUPS_PER_BLOCK = 4
S5_BATCH_PER_STEP = 4


def _params(*sem):
    return pltpu.CompilerParams(dimension_semantics=sem, vmem_limit_bytes=VMEM_LIMIT_BYTES)


def _rms(x, g):
    return x * lax.rsqrt(jnp.mean(x * x, axis=-1, keepdims=True) + RMS_EPS) * g


def _dot(a, b):
    return jnp.dot(a, b, preferred_element_type=F32)


def _dot_nt(a, b):
    return lax.dot_general(a, b, (((1,), (1,)), ((), ())), preferred_element_type=F32)


def _dot_tn(a, b):
    return lax.dot_general(a, b, (((0,), (0,)), ((), ())), preferred_element_type=F32)


def _stacked(main_ref, tail_ref):
    last = pl.program_id(0) == pl.num_programs(0) - 1
    return jnp.where(last, tail_ref[...], main_ref[...])


def _stacked_specs(main, tail):
    tm, w = tail.shape
    last_main = main.shape[0] // tm - 1
    return [pl.BlockSpec((tm, w), lambda i: (jnp.minimum(i, last_main), 0)),
            pl.BlockSpec((tm, w), lambda i: (0, 0))]


def _weight_spec(w):
    return pl.BlockSpec(w.shape, lambda i: (0,) * w.ndim, pipeline_mode=pl.Buffered(1))


def _inproj_kernel(xm_ref, xt_ref, g_ref, w_ref, bg_ref, z_ref, gates_ref, gatest_ref, wb):
    @pl.when(pl.program_id(0) == 0)
    def _():
        wb[...] = jnp.zeros(wb.shape, BF16)
        wb[:, 0:w_ref.shape[1]] = w_ref[...].astype(BF16)

    h = _rms(_stacked(xm_ref, xt_ref), g_ref[...]).astype(BF16)
    zg = _dot(h, wb[...])
    nz = z_ref.shape[1]
    z_ref[...] = zg[:, 0:nz]
    gates = zg[:, nz:nz + LANES] + bg_ref[...]
    gates_ref[...] = gates
    gatest_ref[...] = gates.T[0:gatest_ref.shape[0], :]


def _inproj(x_main, x_tail, g, w, bg, ng):
    d = x_main.shape[1]
    tm = TOKEN_TILE
    n = x_main.shape[0] + tm
    nz = w.shape[1] - ng
    full = lambda i: (0, 0)
    return pl.pallas_call(
        _inproj_kernel,
        grid=(n // tm,),
        in_specs=_stacked_specs(x_main, x_tail) + [
            pl.BlockSpec((1, d), full),
            _weight_spec(w),
            pl.BlockSpec((1, LANES), full),
        ],
        out_specs=[
            pl.BlockSpec((tm, nz), lambda i: (i, 0)),
            pl.BlockSpec((tm, LANES), lambda i: (i, 0)),
            pl.BlockSpec((ng, tm), lambda i: (0, i)),
        ],
        out_shape=[
            jax.ShapeDtypeStruct((n, nz), F32),
            jax.ShapeDtypeStruct((n, LANES), F32),
            jax.ShapeDtypeStruct((ng, n), F32),
        ],
        scratch_shapes=[pltpu.VMEM((d, nz + LANES), BF16)],
        compiler_params=_params("arbitrary"),
        name="inproj",
    )(x_main, x_tail, g, w, bg)


def _pool_prompt_kernel(u_ref, pw_ref, ps_ref, y_ref, st_ref, ext_ref, *, tt, gw):
    t = pl.program_id(1)
    nt = pl.num_programs(1)
    halo = POOL_BUF + 1
    width = ext_ref.shape[1]

    @pl.when(t == 0)
    def _():
        ext_ref[0:halo, :] = jnp.zeros((halo, width), F32)

    u = u_ref[...]
    ext_ref[halo:halo + tt, :] = u
    pos = t * tt + lax.broadcasted_iota(I32, (tt, 1), 0)
    for g, w in enumerate(POOL_WINDOWS):
        c0 = g * gw
        acc = u[:, c0:c0 + gw]
        for j in range(1, w):
            acc = acc + ext_ref[halo - j:halo - j + tt, c0:c0 + gw]
        cnt = jnp.minimum(w, pos + 1).astype(F32)
        d = acc / cnt - u[:, c0:c0 + gw]
        y = _dot(d.astype(BF16), pw_ref[g]) * ps_ref[:, c0:c0 + gw]
        y_ref[:, c0:c0 + gw] = y.astype(BF16)

    @pl.when(t == nt - 1)
    def _():
        st_ref[...] = ext_ref[tt + 1:tt + halo, :]

    ext_ref[0:halo, :] = ext_ref[tt:tt + halo, :]


def _pool_prompt(z, pw, ps, batch, seq):
    width = ps.shape[1]
    gw = width // len(POOL_WINDOWS)
    tt = POOL_TIME_TILE
    nt = seq // tt
    return pl.pallas_call(
        functools.partial(_pool_prompt_kernel, tt=tt, gw=gw),
        grid=(batch, nt),
        in_specs=[
            pl.BlockSpec((tt, width), lambda b, t: (b * nt + t, 0)),
            pl.BlockSpec(pw.shape, lambda b, t: (0, 0, 0)),
            pl.BlockSpec((1, width), lambda b, t: (0, 0)),
        ],
        out_specs=[
            pl.BlockSpec((tt, width), lambda b, t: (b * nt + t, 0)),
            pl.BlockSpec((None, POOL_BUF, width), lambda b, t: (b, 0, 0)),
        ],
        out_shape=[
            jax.ShapeDtypeStruct((batch * seq, width), BF16),
            jax.ShapeDtypeStruct((batch, POOL_BUF, width), F32),
        ],
        scratch_shapes=[pltpu.VMEM((POOL_BUF + 1 + tt, width), F32)],
        compiler_params=_params("parallel", "arbitrary"),
        name="pool_prompt",
    )(z, pw, ps)


def _pool_step_kernel(u_ref, buf_ref, pw_ref, ps_ref, y_ref, nb_ref, *, gw):
    u = u_ref[...]
    for g, w in enumerate(POOL_WINDOWS):
        c0 = g * gw
        acc = u[:, c0:c0 + gw]
        for j in range(1, w):
            acc = acc + buf_ref[POOL_BUF - j, :, c0:c0 + gw]
        cnt = float(min(w, PAST_LEN + 1))
        d = acc / cnt - u[:, c0:c0 + gw]
        y = _dot(d.astype(BF16), pw_ref[g]) * ps_ref[:, c0:c0 + gw]
        y_ref[:, c0:c0 + gw] = y.astype(BF16)
    nb_ref[0:POOL_BUF - 1] = buf_ref[1:POOL_BUF]
    nb_ref[POOL_BUF - 1] = u


def _pool_step(z, buf_t, pw, ps, row0):
    _, batch, width = buf_t.shape
    gw = width // len(POOL_WINDOWS)
    return pl.pallas_call(
        functools.partial(_pool_step_kernel, gw=gw),
        grid=(1,),
        in_specs=[
            pl.BlockSpec((batch, width), lambda i: (row0 // batch, 0)),
            pl.BlockSpec(buf_t.shape, lambda i: (0, 0, 0)),
            pl.BlockSpec(pw.shape, lambda i: (0, 0, 0)),
            pl.BlockSpec((1, width), lambda i: (0, 0)),
        ],
        out_specs=[
            pl.BlockSpec((batch, width), lambda i: (0, 0)),
            pl.BlockSpec(buf_t.shape, lambda i: (0, 0, 0)),
        ],
        out_shape=[
            jax.ShapeDtypeStruct((batch, width), BF16),
            jax.ShapeDtypeStruct(buf_t.shape, F32),
        ],
        compiler_params=_params("arbitrary"),
        name="pool_step",
    )(z, buf_t, pw, ps)


def _mlstm_prompt_kernel(*refs, nh, dh, nseq):
    seq_in = [refs[5 * s:5 * s + 5] for s in range(nseq)]
    h_ref, c_out, n_out, m_out, c_s, n_s, m_s = refs[5 * nseq:]
    ci = pl.program_id(1)
    nc = pl.num_programs(1)
    ln = seq_in[0][0].shape[0]

    @pl.when(ci == 0)
    def _():
        c_s[...] = jnp.zeros(c_s.shape, F32)
        n_s[...] = jnp.zeros(n_s.shape, F32)
        m_s[...] = jnp.zeros(m_s.shape, F32)

    row = lax.broadcasted_iota(I32, (ln, ln), 0)
    col = lax.broadcasted_iota(I32, (ln, ln), 1)
    causal_t = row <= col
    eye = col == row
    lane = lax.broadcasted_iota(I32, (nh, ln), 1)

    def to_col(r):
        return jnp.sum(jnp.where(eye, r, 0.0), axis=1, keepdims=True)

    scale = dh ** -0.5
    pairs = [(s, h) for s in range(nseq) for h in range(nh)]
    c_old = {p: c_s[p[0], p[1]] for p in pairs}
    n_old = {p: n_s[p[0], p[1]] for p in pairs}
    m_old = {p: m_s[p[0], p[1]][:, 0:1] for p in pairs}
    gates = []
    for s in range(nseq):
        gt = seq_in[s][4][...]
        bc_all = jax.nn.log_sigmoid(gt[nh:2 * nh])
        sh = 1
        while sh < ln:
            bc_all = bc_all + jnp.where(lane >= sh, pltpu.roll(bc_all, sh, 1), 0.0)
            sh *= 2
        gates.append((gt[0:nh], bc_all))
    ch = {}
    for s, h in pairs:
        q_ref, k_ref, v_ref, _, _ = seq_in[s]
        li_all, bc_all = gates[s]
        sl = slice(h * dh, (h + 1) * dh)
        k = k_ref[:, sl] * scale
        v = v_ref[:, sl]
        li_r, bc_r = li_all[h:h + 1], bc_all[h:h + 1]
        b_end = bc_r[:, ln - 1:ln]
        g_r = b_end - bc_r + li_r
        m0 = m_old[s, h]
        m_new = jnp.maximum(b_end + m0, jnp.max(g_r, axis=1, keepdims=True))
        ch[s, h] = dict(
            qb=q_ref[:, sl].astype(BF16), kb=k.astype(BF16), vb=v.astype(BF16), v=v,
            inter=bc_r + m0, m_new=m_new, wg_r=jnp.exp(g_r - m_new),
            decay=jnp.exp(b_end + m0 - m_new),
            dmat_t=jnp.where(causal_t, bc_r - to_col(bc_r - li_r), -jnp.inf))
    for p in pairs:
        d = ch[p]
        d["st"] = _dot_nt(d["kb"], d["qb"])
        d["cq_t"] = _dot_nt(c_old[p].astype(BF16), d["qb"])
        d["nq"] = _dot_nt(jnp.broadcast_to(n_old[p], (SUBLANES, dh)).astype(BF16), d["qb"])[0:1]
        d["c_add"] = _dot((d["v"].T * d["wg_r"]).astype(BF16), d["kb"])
        d["n_add"] = _dot(jnp.broadcast_to(d["wg_r"], (SUBLANES, ln)).astype(BF16), d["kb"])[0:1]
    for p in pairs:
        d = ch[p]
        d["m_row"] = jnp.maximum(d["inter"], jnp.max(d["dmat_t"], axis=0, keepdims=True))
        d["w_inter"] = jnp.exp(d["inter"] - d["m_row"])
        d["sc_t"] = d["st"] * jnp.exp(d["dmat_t"] - d["m_row"])
    for p in pairs:
        d = ch[p]
        d["pv_t"] = _dot_tn(d["vb"], d["sc_t"].astype(BF16))
    new_state = {}
    for s, h in pairs:
        d = ch[s, h]
        sl = slice(h * dh, (h + 1) * dh)
        num_t = d["pv_t"] + d["w_inter"] * d["cq_t"]
        den = jnp.sum(d["sc_t"], axis=0, keepdims=True) + d["w_inter"] * d["nq"]
        hh = (num_t / jnp.maximum(jnp.abs(den), jnp.exp(-d["m_row"]))).T
        h_ref[s, :, sl] = (hh * jax.nn.sigmoid(seq_in[s][3][:, sl])).astype(BF16)
        new_state[s, h] = (d["decay"] * c_old[s, h] + d["c_add"],
                           d["decay"] * n_old[s, h] + d["n_add"],
                           jnp.broadcast_to(d["m_new"], (1, dh)))

    for (s, h), (c_new, n_new, m_new) in new_state.items():
        c_s[s, h] = c_new
        n_s[s, h] = n_new
        m_s[s, h] = m_new

    @pl.when(ci == nc - 1)
    def _():
        c_out[...] = c_s[...]
        n_out[...] = n_s[...]
        m_out[...] = m_s[...]


def _mlstm_prompt(z, gates_t, batch, seq, nh, dh):
    ln = MLSTM_CHUNK
    nc = seq // ln
    nseq = min(MLSTM_SEQ_PER_STEP, batch)
    assert batch % nseq == 0
    width = nh * dh
    ng = gates_t.shape[0]
    rows_of = lambda s: (lambda b, c: (b * nseq + s) * nc + c)
    in_specs, operands = [], []
    for s in range(nseq):
        r = rows_of(s)
        for j in (1, 2, 3, 4):
            in_specs.append(pl.BlockSpec((ln, width), lambda b, c, r=r, j=j: (r(b, c), j)))
        in_specs.append(pl.BlockSpec((ng, ln), lambda b, c, r=r: (0, r(b, c))))
        operands += [z, z, z, z, gates_t]
    h_spec = pl.BlockSpec((None, nseq, ln, width), lambda b, c: (b, 0, c, 0))
    st = lambda a, b_: pl.BlockSpec((nseq, nh, a, b_), lambda b, c: (b, 0, 0, 0))
    return pl.pallas_call(
        functools.partial(_mlstm_prompt_kernel, nh=nh, dh=dh, nseq=nseq),
        grid=(batch // nseq, nc),
        in_specs=in_specs,
        out_specs=[h_spec, st(dh, dh), st(1, dh), st(1, dh)],
        out_shape=[
            jax.ShapeDtypeStruct((batch // nseq, nseq, seq, width), BF16),
            jax.ShapeDtypeStruct((batch, nh, dh, dh), F32),
            jax.ShapeDtypeStruct((batch, nh, 1, dh), F32),
            jax.ShapeDtypeStruct((batch, nh, 1, dh), F32),
        ],
        scratch_shapes=[pltpu.VMEM((nseq, nh, dh, dh), F32), pltpu.VMEM((nseq, nh, 1, dh), F32),
                        pltpu.VMEM((nseq, nh, 1, dh), F32)],
        compiler_params=_params("parallel", "arbitrary"),
        name="mlstm_prompt",
    )(*operands)


def _mlstm_step_kernel(q_ref, k_ref, v_ref, o_ref, li_ref, fp_ref, c_ref, n_ref, m_ref,
                       h_ref, c_out, n_out, m_out, *, nh, dh):
    eye = (lax.broadcasted_iota(I32, (1, dh, dh), 1) == lax.broadcasted_iota(I32, (1, dh, dh), 2))
    scale = dh ** -0.5
    for h in range(nh):
        sl = slice(h * dh, (h + 1) * dh)
        q = q_ref[:, :, sl]
        k = k_ref[:, :, sl] * scale
        v = v_ref[:, :, sl]
        c = c_ref[:, h]
        n = n_ref[:, h]
        m = m_ref[:, h]
        li = li_ref[:, h]
        lf = jax.nn.log_sigmoid(fp_ref[:, h])
        inter = lf + m
        m_row = jnp.maximum(inter, li)
        w_intra = jnp.exp(li - m_row)
        w_inter = jnp.exp(inter - m_row)
        sc = jnp.sum(q * k, axis=-1, keepdims=True) * w_intra
        q8 = jnp.broadcast_to(q, (q.shape[0], SUBLANES, dh)).astype(BF16)
        cq = jnp.einsum("bqk,bvk->bqv", q8, c.astype(BF16),
                        preferred_element_type=F32)[:, 0:1, :]
        num = sc * v + w_inter * cq
        den = sc + w_inter * jnp.sum(n * q, axis=-1, keepdims=True)
        h_l = num / jnp.maximum(jnp.abs(den), jnp.exp(-m_row))
        h_ref[:, :, sl] = h_l * jax.nn.sigmoid(o_ref[:, :, sl])
        v_c = jnp.sum(jnp.where(eye, v, 0.0), axis=-1, keepdims=True)
        wg = jnp.exp(li - m_row)
        decay = jnp.exp(inter - m_row)
        c_out[:, h] = decay * c + (v_c * wg) * k
        n_out[:, h] = decay * n + wg * k
        m_out[:, h] = m_row


def _mlstm_step(z3, li, fp, c, n, m, nh, dh):
    batch = c.shape[0]
    bb = MLSTM_STEP_BATCH
    width = nh * dh
    blk = lambda j: pl.BlockSpec((bb, 1, width), lambda i: (i, 0, j))
    st4 = lambda a, b: pl.BlockSpec((bb, nh, a, b), lambda i: (i, 0, 0, 0))
    return pl.pallas_call(
        functools.partial(_mlstm_step_kernel, nh=nh, dh=dh),
        grid=(batch // bb,),
        in_specs=[blk(1), blk(2), blk(3), blk(4), st4(1, 1), st4(1, 1),
                  st4(dh, dh), st4(1, dh), st4(1, 1)],
        out_specs=[pl.BlockSpec((bb, 1, width), lambda i: (i, 0, 0)),
                   st4(dh, dh), st4(1, dh), st4(1, 1)],
        out_shape=[
            jax.ShapeDtypeStruct((batch, 1, width), F32),
            jax.ShapeDtypeStruct((batch, nh, dh, dh), F32),
            jax.ShapeDtypeStruct((batch, nh, 1, dh), F32),
            jax.ShapeDtypeStruct((batch, nh, 1, 1), F32),
        ],
        compiler_params=_params("parallel"),
        name="mlstm_step",
    )(z3, z3, z3, z3, li, fp, c, n, m)


def _split_bf16(a):
    hi = a.astype(BF16)
    return hi, (a - hi.astype(F32)).astype(BF16)


def _route(hn, wr_ref, br_ref, tril_ref, carry_ref, rinfo_ref, rinfo_t_ref):
    ngrp, epg = MOE_GROUPS, MOE_EXPERTS_PER_GROUP
    h_hi, h_lo = _split_bf16(hn)
    both = _dot(h_hi, wr_ref[...])
    logits = both[:, 0:LANES] + (both[:, LANES:2 * LANES] + _dot(h_lo, wr_ref[:, 0:LANES])) + br_ref[...]
    tm = logits.shape[0]
    lane = lax.broadcasted_iota(I32, (tm, LANES), 1)
    neg = -jnp.inf

    def first_max(x):
        mx = jnp.max(x, axis=1, keepdims=True)
        idx = jnp.min(jnp.where(x == mx, lane, LANES), axis=1, keepdims=True)
        return mx, idx

    is_grp = lane < ngrp
    gmax, gsel = first_max(jnp.where(is_grp, logits, neg))
    g_w = 1.0 / jnp.sum(jnp.where(is_grp, jnp.exp(logits - gmax), 0.0), axis=1, keepdims=True)
    lo = ngrp + gsel * epg
    el = jnp.where((lane >= lo) & (lane < lo + epg), logits, neg)
    v1, i1 = first_max(el)
    v2, i2 = first_max(jnp.where(lane == i1, neg, el))
    e2 = jnp.exp(v2 - v1)
    w1 = g_w / (1.0 + e2)
    w2 = g_w * e2 / (1.0 + e2)
    eid1 = i1 - ngrp
    eid2 = i2 - ngrp

    hit1 = lane == eid1
    hit2 = lane == eid2
    onehot = jnp.where(hit1 | hit2, 1.0, 0.0)
    carry = carry_ref[...]
    prefix = _dot(tril_ref[...], onehot.astype(BF16)) + carry
    rank1 = jnp.sum(jnp.where(hit1, prefix, 0.0), axis=1, keepdims=True)
    rank2 = jnp.sum(jnp.where(hit2, prefix, 0.0), axis=1, keepdims=True)
    carry_ref[...] = carry + jnp.sum(onehot, axis=0, keepdims=True)

    cols = (eid1.astype(F32), eid2.astype(F32), w1, w2, rank1, rank2)
    info = jnp.zeros((tm, LANES), F32)
    for j, cval in enumerate(cols):
        info = jnp.where(lane == j, cval, info)
    rinfo_ref[...] = info
    pick = jnp.where(lax.broadcasted_iota(I32, (SUBLANES, LANES), 0)
                     == lax.broadcasted_iota(I32, (SUBLANES, LANES), 1), 1.0, 0.0).astype(BF16)
    parts = [_dot_nt(pick, p) for p in _split3(info)]
    rinfo_t_ref[...] = parts[0] + (parts[1] + parts[2])


def _outproj_route_kernel(xm_ref, xt_ref, pm_ref, pt_ref, mm_ref, mt_ref,
                          wo_ref, g_ref, wr_ref, br_ref, tril_ref,
                          x1_ref, hn_ref, rinfo_ref, rinfo_t_ref, cnt_ref, carry_ref, wb):
    @pl.when(pl.program_id(0) == 0)
    def _():
        carry_ref[...] = jnp.zeros(carry_ref.shape, F32)
        wb[...] = wo_ref[...].astype(BF16)

    half = pm_ref.shape[1]
    mix = (_dot(_stacked(pm_ref, pt_ref), wb[0:half, :])
           + _dot(_stacked(mm_ref, mt_ref), wb[half:2 * half, :]))
    x1 = _stacked(xm_ref, xt_ref) + mix
    x1_ref[...] = x1
    hn = _rms(x1, g_ref[...])
    hn_ref[...] = hn
    _route(hn, wr_ref, br_ref, tril_ref, carry_ref, rinfo_ref, rinfo_t_ref)
    cnt_ref[...] = carry_ref[...]


def _glu_route_kernel(x_ref, ym_ref, yt_ref, wglu_ref, g_ref, wr_ref, br_ref, tril_ref,
                      x1_ref, hn_ref, rinfo_ref, rinfo_t_ref, cnt_ref, carry_ref, wb):
    @pl.when(pl.program_id(0) == 0)
    def _():
        carry_ref[...] = jnp.zeros(carry_ref.shape, F32)
        wb[...] = wglu_ref[...].astype(BF16)

    d = x_ref.shape[1]
    ag = _dot(jax.nn.gelu(_stacked(ym_ref, yt_ref)).astype(BF16), wb[...])
    x1 = x_ref[...] + ag[:, 0:d] * jax.nn.sigmoid(ag[:, d:2 * d])
    x1_ref[...] = x1
    hn = _rms(x1, g_ref[...])
    hn_ref[...] = hn
    _route(hn, wr_ref, br_ref, tril_ref, carry_ref, rinfo_ref, rinfo_t_ref)
    cnt_ref[...] = carry_ref[...]


def _mix_route(kernel, name, n, row_specs, rows, w, g, wr, br, tril):
    d = g.shape[1]
    tm = TOKEN_TILE
    full = lambda i: (0, 0)
    return pl.pallas_call(
        kernel,
        grid=(n // tm,),
        in_specs=row_specs + [
            _weight_spec(w),
            pl.BlockSpec((1, d), full),
            pl.BlockSpec((d, 2 * LANES), full),
            pl.BlockSpec((1, LANES), full),
            pl.BlockSpec((tm, tm), full),
        ],
        out_specs=[
            pl.BlockSpec((tm, d), lambda i: (i, 0)),
            pl.BlockSpec((tm, d), lambda i: (i, 0)),
            pl.BlockSpec((tm, LANES), lambda i: (i, 0)),
            pl.BlockSpec((SUBLANES, tm), lambda i: (0, i)),
            pl.BlockSpec((1, LANES), full),
        ],
        out_shape=[
            jax.ShapeDtypeStruct((n, d), F32),
            jax.ShapeDtypeStruct((n, d), F32),
            jax.ShapeDtypeStruct((n, LANES), F32),
            jax.ShapeDtypeStruct((SUBLANES, n), F32),
            jax.ShapeDtypeStruct((1, LANES), F32),
        ],
        scratch_shapes=[pltpu.VMEM((1, LANES), F32), pltpu.VMEM(w.shape, BF16)],
        compiler_params=_params("arbitrary"),
        name=name,
    )(*rows, w, g, wr, br, tril)


def _index_copy(pos_hbm, idx_s, sem_i, tile, slot):
    return pltpu.make_async_copy(pos_hbm.at[tile], idx_s.at[slot], sem_i.at[slot])


def _dispatch_kernel(pos_hbm, hn_hbm, xs_hbm, idx_s, tiles, sem_i, sem_l, sem_d):
    i = pl.program_id(0)
    nt = pl.num_programs(0)
    nslot, tm, _ = tiles.shape
    islot = i % 2

    def load(t):
        s = t % nslot
        return pltpu.make_async_copy(hn_hbm.at[pl.ds(t * tm, tm)], tiles.at[s], sem_l.at[s])

    def wait_rows(t):
        s = t % nslot
        whole = pltpu.make_async_copy(tiles.at[s], xs_hbm.at[pl.ds(0, tm)], sem_d.at[s])
        whole.wait()
        whole.wait()

    @pl.when(i == 0)
    def _():
        _index_copy(pos_hbm, idx_s, sem_i, 0, 0).start()
        load(0).start()

        @pl.when(nt > 1)
        def _():
            load(1).start()

    @pl.when(i >= 2)
    def _():
        wait_rows(i - 2)

    @pl.when(i + 2 < nt)
    def _():
        load(i + 2).start()

    _index_copy(pos_hbm, idx_s, sem_i, i, islot).wait()

    @pl.when(i + 1 < nt)
    def _():
        _index_copy(pos_hbm, idx_s, sem_i, i + 1, 1 - islot).start()

    load(i).wait()
    slot = i % nslot

    def issue(r, carry):
        row = tiles.at[slot, pl.ds(r, 1)]
        pltpu.make_async_copy(row, xs_hbm.at[pl.ds(idx_s[islot, 0, r], 1)],
                              sem_d.at[slot]).start(priority=0)
        pltpu.make_async_copy(row, xs_hbm.at[pl.ds(idx_s[islot, 0, tm + r], 1)],
                              sem_d.at[slot]).start(priority=1)
        return carry

    lax.fori_loop(0, tm, issue, 0, unroll=DMA_ISSUE_UNROLL)

    @pl.when(i == nt - 1)
    def _():
        @pl.when(nt > 1)
        def _():
            wait_rows(i - 1)

        wait_rows(i)


def _dispatch(pos_tiles, hn):
    n, d = hn.shape
    tm = TOKEN_TILE
    nslot = 4
    return pl.pallas_call(
        _dispatch_kernel,
        grid=(n // tm,),
        in_specs=[pl.BlockSpec(memory_space=pl.ANY), pl.BlockSpec(memory_space=pl.ANY)],
        out_specs=pl.BlockSpec(memory_space=pl.ANY),
        out_shape=jax.ShapeDtypeStruct((2 * n, d), F32),
        scratch_shapes=[pltpu.SMEM((2, 1, 2 * tm), I32), pltpu.VMEM((nslot, tm, d), F32),
                        pltpu.SemaphoreType.DMA((2,)), pltpu.SemaphoreType.DMA((nslot,)),
                        pltpu.SemaphoreType.DMA((nslot,))],
        compiler_params=_params("arbitrary"),
        name="moe_dispatch",
    )(pos_tiles, hn)


def _moe_kernel(vt_ref, ve_ref, von_ref, vnext_ref, vslot_ref, lo_ref, hi_ref,
                xs_ref, wg_hbm, wu_hbm, wd_hbm, eo_ref,
                wg_f, wu_f, wd_f, wgb, wub, wdb, sem_w, *, layer):
    v = pl.program_id(0)
    tr = xs_ref.shape[0]
    prev = jnp.maximum(v - 1, 0)
    e = ve_ref[v]
    new_expert = jnp.logical_or(v == 0, e != ve_ref[prev])
    first_visit = jnp.logical_or(v == 0, vt_ref[v] != vt_ref[prev])

    def fetch(expert, slot):
        return [pltpu.make_async_copy(src.at[layer, expert], dst.at[slot], sem_w.at[slot, j])
                for j, (src, dst) in enumerate(((wg_hbm, wg_f), (wu_hbm, wu_f), (wd_hbm, wd_f)))]

    @pl.when(von_ref[v] == 1)
    def _():
        @pl.when(new_expert)
        def _():
            slot = vslot_ref[v]

            @pl.when(v == 0)
            def _():
                for cp in fetch(e, slot):
                    cp.start()

            for cp in fetch(e, slot):
                cp.wait()
            wgb[...] = wg_f[slot].astype(BF16)
            wub[...] = wu_f[slot].astype(BF16)
            wdb[...] = wd_f[slot].astype(BF16)

            @pl.when(vnext_ref[v] >= 0)
            def _():
                for cp in fetch(vnext_ref[v], 1 - slot):
                    cp.start()

        x = xs_ref[...].astype(BF16)
        act = jax.nn.silu(_dot(x, wgb[...])) * _dot(x, wub[...])
        row = vt_ref[v] * tr + lax.broadcasted_iota(I32, (tr, 1), 0)
        act = jnp.where((row >= lo_ref[e]) & (row < hi_ref[e]), act, 0.0)
        res = _dot(act.astype(BF16), wdb[...])

        @pl.when(first_visit)
        def _():
            eo_ref[...] = res

        @pl.when(jnp.logical_not(first_visit))
        def _():
            eo_ref[...] += res


def _moe_experts(vt, ve, von, vnext, vslot, lo, hi, xs, wg, wu, wd, layer):
    nv = vt.shape[0]
    rows, d = xs.shape
    hid = wg.shape[3]
    tr = EXPERT_ROW_TILE
    tile = pl.BlockSpec((tr, d), lambda v, vt, *_: (vt[v], 0))
    grid_spec = pltpu.PrefetchScalarGridSpec(
        num_scalar_prefetch=7,
        grid=(nv,),
        in_specs=[tile] + [pl.BlockSpec(memory_space=pl.ANY)] * 3,
        out_specs=tile,
        scratch_shapes=[
            pltpu.VMEM((2, d, hid), F32), pltpu.VMEM((2, d, hid), F32), pltpu.VMEM((2, hid, d), F32),
            pltpu.VMEM((d, hid), BF16), pltpu.VMEM((d, hid), BF16), pltpu.VMEM((hid, d), BF16),
            pltpu.SemaphoreType.DMA((2, 3)),
        ],
    )
    return pl.pallas_call(
        functools.partial(_moe_kernel, layer=layer),
        grid_spec=grid_spec,
        out_shape=jax.ShapeDtypeStruct((rows, d), F32),
        compiler_params=_params("arbitrary"),
        name="moe_experts",
    )(vt, ve, von, vnext, vslot, lo, hi, xs, wg, wu, wd)


def _combine_kernel(pos_hbm, eo_hbm, x_ref, rinfo_ref, g_ref, o1_ref, o2_ref, idx_s, a_buf, b_buf,
                    sem_i, sem_a, sem_b, *, last_layer):
    i = pl.program_id(0)
    nt = pl.num_programs(0)
    tm = a_buf.shape[1]
    slot = i % 2

    def gathers(s):
        def issue(r, carry):
            pltpu.make_async_copy(eo_hbm.at[pl.ds(idx_s[s, 0, r], 1)],
                                  a_buf.at[s, pl.ds(r, 1)], sem_a.at[s]).start(priority=0)
            pltpu.make_async_copy(eo_hbm.at[pl.ds(idx_s[s, 0, tm + r], 1)],
                                  b_buf.at[s, pl.ds(r, 1)], sem_b.at[s]).start(priority=1)
            return carry

        lax.fori_loop(0, tm, issue, 0, unroll=DMA_ISSUE_UNROLL)

    @pl.when(i == 0)
    def _():
        first = _index_copy(pos_hbm, idx_s, sem_i, 0, 0)
        first.start()
        first.wait()
        gathers(0)

        @pl.when(nt > 1)
        def _():
            _index_copy(pos_hbm, idx_s, sem_i, 1, 1).start()

    @pl.when(i + 1 < nt)
    def _():
        _index_copy(pos_hbm, idx_s, sem_i, i + 1, 1 - slot).wait()
        gathers(1 - slot)

    @pl.when(i + 2 < nt)
    def _():
        _index_copy(pos_hbm, idx_s, sem_i, i + 2, slot).start()

    pltpu.make_async_copy(eo_hbm.at[pl.ds(0, tm)], a_buf.at[slot], sem_a.at[slot]).wait()
    pltpu.make_async_copy(eo_hbm.at[pl.ds(0, tm)], b_buf.at[slot], sem_b.at[slot]).wait()
    info = rinfo_ref[...]
    x2 = x_ref[...] + (info[:, 2:3] * a_buf[slot] + info[:, 3:4] * b_buf[slot])
    hn = _rms(x2, g_ref[...])
    if last_layer:
        @pl.when(i < nt - 1)
        def _():
            o1_ref[...] = hn

        @pl.when(i == nt - 1)
        def _():
            o2_ref[...] = hn
    else:
        o1_ref[...] = x2
        o2_ref[...] = hn


def _combine(pos_tiles, eo, x, rinfo, g, last_layer):
    n, d = x.shape
    tm = TOKEN_TILE
    nt = n // tm
    row = pl.BlockSpec((tm, d), lambda i: (i, 0))
    if last_layer:
        out_specs = [pl.BlockSpec((tm, d), lambda i: (jnp.minimum(i, nt - 2), 0)),
                     pl.BlockSpec((tm, d), lambda i: (0, 0))]
        out_shape = [jax.ShapeDtypeStruct((n - tm, d), F32), jax.ShapeDtypeStruct((tm, d), F32)]
    else:
        out_specs = [row, row]
        out_shape = [jax.ShapeDtypeStruct((n, d), F32), jax.ShapeDtypeStruct((n, d), F32)]
    return pl.pallas_call(
        functools.partial(_combine_kernel, last_layer=last_layer),
        grid=(nt,),
        in_specs=[
            pl.BlockSpec(memory_space=pl.ANY),
            pl.BlockSpec(memory_space=pl.ANY),
            row,
            pl.BlockSpec((tm, LANES), lambda i: (i, 0)),
            pl.BlockSpec((1, d), lambda i: (0, 0)),
        ],
        out_specs=out_specs,
        out_shape=out_shape,
        scratch_shapes=[
            pltpu.SMEM((2, 1, 2 * tm), I32),
            pltpu.VMEM((2, tm, d), F32),
            pltpu.VMEM((2, tm, d), F32),
            pltpu.SemaphoreType.DMA((2,)),
            pltpu.SemaphoreType.DMA((2,)),
            pltpu.SemaphoreType.DMA((2,)),
        ],
        compiler_params=_params("arbitrary"),
        name="moe_combine",
    )(pos_tiles, eo, x, rinfo, g)


def _moe(x1, hn, rinfo, rinfo_t, counts, wg, wu, wd, layer, next_gain, last_layer):
    n = x1.shape[0]
    ne = wg.shape[1]
    tr = EXPERT_ROW_TILE
    tm = TOKEN_TILE
    nv = (2 * n) // tr + ne - 1
    eid = rinfo_t[0:2].astype(I32)
    rank = rinfo_t[4:6].astype(I32)
    cnt = counts[0, :ne].astype(I32)
    seg_end = jnp.cumsum(cnt)
    seg_start = seg_end - cnt
    experts = jnp.arange(ne, dtype=I32)
    pos = rank + jnp.sum(jnp.where(eid[:, None, :] == experts[None, :, None],
                                   seg_start[None, :, None], 0), axis=1)
    pos_tiles = pos.reshape(2, n // tm, tm).transpose(1, 0, 2).reshape(n // tm, 1, 2 * tm)

    first_tile = seg_start // tr
    tiles_e = jnp.where(cnt > 0, (seg_end - 1) // tr - first_tile + 1, 0)
    v_end = jnp.cumsum(tiles_e)
    v_start = v_end - tiles_e
    total = v_end[-1]
    vis = jnp.arange(nv, dtype=I32)
    vc = jnp.minimum(vis, jnp.maximum(total - 1, 0))
    ve = jnp.sum((vc[:, None] >= v_end[None, :]).astype(I32), axis=1)
    pick = lambda tab: jnp.sum(jnp.where(ve[:, None] == experts, tab, 0), axis=1)
    vt = pick(first_tile) + (vc - pick(v_start))
    von = (vis < total).astype(I32)
    present = cnt > 0
    later = present[None, :] & (experts[None, :] > experts[:, None])
    next_of = jnp.min(jnp.where(later, experts[None, :], ne), axis=1)
    vnext = pick(jnp.where(next_of < ne, next_of, -1))
    vslot = pick(jnp.cumsum(present.astype(I32)) - 1) & 1

    xs = _dispatch(pos_tiles, hn)
    eo = _moe_experts(vt, ve, von, vnext, vslot, seg_start, seg_end, xs, wg, wu, wd, layer)
    return _combine(pos_tiles, eo, x1, rinfo, next_gain, last_layer)


def _cis(log_mag, ang):
    mag = jnp.exp(log_mag)
    return mag * jnp.cos(ang), mag * jnp.sin(ang)


def _split3(a):
    p1 = a.astype(BF16)
    r1 = a - p1.astype(F32)
    p2 = r1.astype(BF16)
    return p1, p2, (r1 - p2.astype(F32)).astype(BF16)


def _select_dot(a, sel, sel_first=False):
    sel = sel.astype(BF16)
    parts = [(_dot(sel, p) if sel_first else _dot(p, sel)) for p in _split3(a)]
    return parts[0] + (parts[1] + parts[2])


def _dot3(a, b):
    a_hi, a_lo = _split_bf16(a)
    b_hi, b_lo = _split_bf16(b)
    return _dot(a_hi, b_hi) + (_dot(a_lo, b_hi) + _dot(a_hi, b_lo))


def _s5_prep_kernel(lsc_ref, lsr_ref, lrc_ref, lic_ref, lrr_ref, lir_ref, ctre_ref, ctim_ref,
                    btre_ref, btim_ref, d_ref,
                    t_ref, wre_ref, wim_ref, cyre_ref, cyim_ref, apr_ref, api_ref,
                    bbr_ref, bbi_ref, lbr_ref, lbi_ref, *, gw, gpb, sub):
    blk = gpb * gw
    wide = sub * blk
    nsl = lrc_ref.shape[1]
    ns = nsl // gpb
    sh_gw, sh_ns = gw.bit_length() - 1, ns.bit_length() - 1
    dt_c, dt_r = jnp.exp(lsc_ref[0]), jnp.exp(lsr_ref[0])
    ldt_c_re, ldt_c_im = lrc_ref[0] * dt_c, lic_ref[0] * dt_c
    ldt_r_re, ldt_r_im = lrr_ref[0] * dt_r, lir_ref[0] * dt_r

    lane_w = lax.broadcasted_iota(I32, (1, wide), 1)
    spread = jnp.where((lax.broadcasted_iota(I32, (gw, wide), 1) & (gw - 1))
                       == lax.broadcasted_iota(I32, (gw, wide), 0), 1.0, 0.0)
    cre = _select_dot(ctre_ref[0], spread)
    cim = _select_dot(ctim_ref[0], spread)
    same = (jnp.right_shift(lax.broadcasted_iota(I32, (nsl, 1), 0), sh_ns)
            == (jnp.right_shift(lane_w, sh_gw) & (gpb - 1)))
    assert 2 * blk == LANES
    tau = lax.broadcasted_iota(I32, (1, LANES), 1).astype(F32)
    pw_re, pw_im = _cis(tau * ldt_c_re, tau * ldt_c_im)
    low = lax.broadcasted_iota(I32, (nsl, LANES), 1) < blk

    def spread_pow(p, first):
        col = lambda t: jnp.broadcast_to(p[:, t:t + 1], (nsl, LANES))
        return jnp.concatenate([jnp.where(low, col(first + 2 * m), col(first + 2 * m + 1))
                                for m in range(sub // 2)], axis=1)

    def c_lam_pow(first):
        pr, pi = spread_pow(pw_re, first), spread_pow(pw_im, first)
        return (jnp.where(same, pr * cre - pi * cim, 0.0),
                jnp.where(same, -(pr * cim + pi * cre), 0.0))

    clr0, cli0 = c_lam_pow(0)
    clr1, cli1 = c_lam_pow(1)
    cyre_ref[0] = clr1.astype(BF16)
    cyim_ref[0] = cli1.astype(BF16)

    lbr, lbi = _cis(ldt_r_re, ldt_r_im)
    lbr_ref[0] = lbr
    lbi_ref[0] = lbi
    lr, li = lrr_ref[0], lir_ref[0]
    nr, ni = lbr - 1.0, lbi
    den = lr * lr + li * li
    fr = (nr * lr + ni * li) / den
    fi = (ni * lr - nr * li) / den
    bre, bim = btre_ref[0], btim_ref[0]
    bbr = fr * bre - fi * bim
    bbi = fr * bim + fi * bre
    bbr_ref[0] = bbr
    bbi_ref[0] = bbi

    lane_t = lax.broadcasted_iota(I32, (gw, wide), 1)
    chan = lax.broadcasted_iota(I32, (gw, wide), 0)
    for g in range(gpb):
        ps = slice(g * ns, (g + 1) * ns)
        r = _dot3(bbr[:, ps], clr0[ps, :]) + _dot3(bbi[:, ps], cli0[ps, :])
        r = r + jnp.where(lane_t == g * gw + chan, d_ref[0][:, g:g + 1], 0.0)
        for j in range(sub):
            tb = r if j == 0 else jnp.where(lane_t >= blk * j, pltpu.roll(r, blk * j, 1), 0.0)
            r0 = j * blk + g * gw
            t_ref[0, r0:r0 + gw, :] = tb.astype(BF16)

    rows = lax.broadcasted_iota(I32, (wide, 1), 0)
    spread_t = jnp.where((lax.broadcasted_iota(I32, (wide, gw), 0) & (gw - 1))
                         == lax.broadcasted_iota(I32, (wide, gw), 1), 1.0, 0.0)
    bbr_t = _select_dot(bbr, spread_t, sel_first=True)
    bbi_t = _select_dot(bbi, spread_t, sel_first=True)
    same_w = ((jnp.right_shift(rows, sh_gw) & (gpb - 1))
              == jnp.right_shift(lax.broadcasted_iota(I32, (1, nsl), 1), sh_ns))
    rj = ((sub - 1) - lax.broadcasted_iota(I32, (sub, 1), 0)).astype(F32)
    q_re, q_im = _cis(rj * ldt_r_re, rj * ldt_r_im)
    per_step = lambda q: jnp.concatenate(
        [jnp.broadcast_to(q[j:j + 1, :], (blk, nsl)) for j in range(sub)], axis=0)
    pr, pi = per_step(q_re), per_step(q_im)
    wre_ref[0] = jnp.where(same_w, pr * bbr_t - pi * bbi_t, 0.0).astype(BF16)
    wim_ref[0] = jnp.where(same_w, pr * bbi_t + pi * bbr_t, 0.0).astype(BF16)

    nlev = apr_ref.shape[1]
    pw = (sub * jnp.left_shift(1, lax.broadcasted_iota(I32, (nlev, 1), 0))).astype(F32)
    apr, api = _cis(pw * ldt_r_re, pw * ldt_r_im)
    apr_ref[0] = apr
    api_ref[0] = api


def _s5_prep(log_step, lam_re, lam_im, b_re, b_im, c_re, c_im, d_skip):
    ng, ns = lam_re.shape
    gw = b_re.shape[2]
    sub, gpb = S5_SUB, S5_GROUPS_PER_BLOCK
    nblk = ng // gpb
    nsl = gpb * ns
    wide = sub * gpb * gw
    ls = jnp.repeat(log_step, ns)
    ct = lambda c: jnp.transpose(c, (0, 2, 1)).reshape(nblk, nsl, gw)
    bt = lambda b: jnp.transpose(b.reshape(nblk, gpb, ns, gw), (0, 3, 1, 2)).reshape(nblk, gw, nsl)
    ins = [ls.reshape(nblk, nsl, 1), ls.reshape(nblk, 1, nsl),
           lam_re.reshape(nblk, nsl, 1), lam_im.reshape(nblk, nsl, 1),
           lam_re.reshape(nblk, 1, nsl), lam_im.reshape(nblk, 1, nsl),
           ct(c_re), ct(c_im), bt(b_re), bt(b_im),
           jnp.transpose(d_skip.reshape(nblk, gpb, gw), (0, 2, 1))]
    spec = lambda a: pl.BlockSpec((1,) + a.shape[1:], lambda g: (g, 0, 0))
    outs = [
        jax.ShapeDtypeStruct((nblk, wide, wide), BF16),
        jax.ShapeDtypeStruct((nblk, wide, nsl), BF16),
        jax.ShapeDtypeStruct((nblk, wide, nsl), BF16),
        jax.ShapeDtypeStruct((nblk, nsl, wide), BF16),
        jax.ShapeDtypeStruct((nblk, nsl, wide), BF16),
        jax.ShapeDtypeStruct((nblk, SUBLANES, nsl), F32),
        jax.ShapeDtypeStruct((nblk, SUBLANES, nsl), F32),
        jax.ShapeDtypeStruct((nblk, gw, nsl), F32),
        jax.ShapeDtypeStruct((nblk, gw, nsl), F32),
        jax.ShapeDtypeStruct((nblk, 1, nsl), F32),
        jax.ShapeDtypeStruct((nblk, 1, nsl), F32),
    ]
    return pl.pallas_call(
        functools.partial(_s5_prep_kernel, gw=gw, gpb=gpb, sub=sub),
        grid=(nblk,),
        in_specs=[spec(a) for a in ins],
        out_specs=[spec(o) for o in outs],
        out_shape=outs,
        compiler_params=_params("parallel"),
        name="s5_prep",
    )(*ins)


def _s5_prompt_kernel(x_ref, t_ref, wre_ref, wim_ref, cyre_ref, cyim_ref, apr_ref, api_ref,
                      y_ref, hre_out, him_out, u_s, y4_s, *, nb, seq, sub):
    nk = seq // sub
    rows = nb * nk
    hl = LANES // 2
    nlev = nk.bit_length() - 1
    nsl = wre_ref.shape[2]
    low = lax.broadcasted_iota(I32, (nk, LANES), 1) < hl
    kidx = lax.broadcasted_iota(I32, (rows, 1), 0) & (nk - 1)

    def step_rows(b, j):
        return (pl.ds(b * seq + j, nk, stride=sub), slice(None))

    for b in range(nb):
        rs = slice(b * nk, (b + 1) * nk)
        for m in range(sub // 2):
            ls = slice(m * LANES, (m + 1) * LANES)
            s0 = x_ref[step_rows(b, 2 * m)]
            s1 = x_ref[step_rows(b, 2 * m + 1)]
            u_s[0, rs, ls] = jnp.where(low, s0, pltpu.roll(s1, hl, 1)).astype(BF16)
            u_s[1, rs, ls] = jnp.where(low, pltpu.roll(s0, hl, 1), s1).astype(BF16)

    def shifted(a, s):
        return jnp.where(kidx >= s, pltpu.roll(a, s, 0), 0.0)

    halves = range(2)
    us = [u_s[h] for h in halves]
    hre = [_dot(us[h], wre_ref[h]) for h in halves]
    him = [_dot(us[h], wim_ref[h]) for h in halves]
    cw = 2 * LANES
    y_conv = [jnp.concatenate(
        [_dot(us[h][:, 0:c0 + cw], t_ref[h, 0:c0 + cw, c0:c0 + cw]) for c0 in range(0, us[h].shape[1], cw)],
        axis=1) for h in halves]
    for i in range(nlev):
        for h in halves:
            ar = apr_ref[h, i:i + 1, :]
            ai = api_ref[h, i:i + 1, :]
            pre, pim = shifted(hre[h], 1 << i), shifted(him[h], 1 << i)
            hre[h], him[h] = hre[h] + (ar * pre - ai * pim), him[h] + (ar * pim + ai * pre)
    for h in halves:
        ls = slice(h * nsl, (h + 1) * nsl)
        for b in range(nb):
            last = (b + 1) * nk - 1
            hre_out[b:b + 1, ls] = hre[h][last:last + 1, :]
            him_out[b:b + 1, ls] = him[h][last:last + 1, :]
        hpre = shifted(hre[h], 1).astype(BF16)
        hpim = shifted(him[h], 1).astype(BF16)
        y4_s[h] = y_conv[h] + _dot(hpre, cyre_ref[h]) + _dot(hpim, cyim_ref[h])

    for b in range(nb):
        rs = slice(b * nk, (b + 1) * nk)
        for m in range(sub // 2):
            ls = slice(m * LANES, (m + 1) * LANES)
            ca = y4_s[0, rs, ls]
            cb = y4_s[1, rs, ls]
            y_ref[step_rows(b, 2 * m)] = jnp.where(low, ca, pltpu.roll(cb, hl, 1))
            y_ref[step_rows(b, 2 * m + 1)] = jnp.where(low, pltpu.roll(ca, hl, 1), cb)


def _s5_prompt(h, prep, batch, seq):
    d = h.shape[1]
    tmat, wre, wim, cyre, cyim, apr, api = prep[:7]
    nblk, wide, nsl = wre.shape
    sub, nb = S5_SUB, S5_BATCH_PER_STEP
    nk = seq // sub
    assert nk & (nk - 1) == 0 and nk.bit_length() - 1 <= apr.shape[1]
    ntile = d // LANES
    assert nblk == 2 * ntile
    rows = nb * nk
    wspec = lambda a: pl.BlockSpec((2,) + a.shape[1:], lambda t, b: (t, 0, 0))
    xspec = pl.BlockSpec((nb * seq, LANES), lambda t, b: (b, t))
    sspec = pl.BlockSpec((None, nb, 2 * nsl), lambda t, b: (b, 0, t))
    return pl.pallas_call(
        functools.partial(_s5_prompt_kernel, nb=nb, seq=seq, sub=sub),
        grid=(ntile, batch // nb),
        in_specs=[xspec, wspec(tmat), wspec(wre), wspec(wim), wspec(cyre), wspec(cyim),
                  wspec(apr), wspec(api)],
        out_specs=[xspec, sspec, sspec],
        out_shape=[
            jax.ShapeDtypeStruct((batch * seq, d), F32),
            jax.ShapeDtypeStruct((batch // nb, nb, ntile * 2 * nsl), F32),
            jax.ShapeDtypeStruct((batch // nb, nb, ntile * 2 * nsl), F32),
        ],
        scratch_shapes=[
            pltpu.VMEM((2, rows, wide), BF16),
            pltpu.VMEM((2, rows, wide), F32),
        ],
        compiler_params=_params("parallel", "parallel"),
        name="s5_prompt",
    )(h, tmat, wre, wim, cyre, cyim, apr, api)


def _s5_step_kernel(u_ref, hre_ref, him_ref, bbr_ref, bbi_ref, lbr_ref, lbi_ref, cre_ref, cim_ref,
                    d_ref, nre_ref, nim_ref, y_ref):
    for g in range(u_ref.shape[0]):
        u = u_ref[g]
        hre, him, lbr, lbi = hre_ref[g], him_ref[g], lbr_ref[g], lbi_ref[g]
        nre = lbr * hre - lbi * him + _dot3(u, bbr_ref[g])
        nim = lbr * him + lbi * hre + _dot3(u, bbi_ref[g])
        nre_ref[g] = nre
        nim_ref[g] = nim
        y_ref[g] = _dot3(nre, cre_ref[g]) - _dot3(nim, cim_ref[g]) + d_ref[g] * u


def _s5_step(u_s, h_re, h_im, bbr, bbi, lbr, lbi, cre_t, cim_t, d_g):
    ng = u_s.shape[0]
    gb = SUBLANES
    spec = lambda a: pl.BlockSpec((gb,) + a.shape[1:], lambda g: (g, 0, 0))
    ins = [u_s, h_re, h_im, bbr, bbi, lbr, lbi, cre_t, cim_t, d_g]
    outs = [jax.ShapeDtypeStruct(h_re.shape, F32), jax.ShapeDtypeStruct(h_re.shape, F32),
            jax.ShapeDtypeStruct(u_s.shape, F32)]
    return pl.pallas_call(
        _s5_step_kernel,
        grid=(ng // gb,),
        in_specs=[spec(a) for a in ins],
        out_specs=[spec(o) for o in outs],
        out_shape=outs,
        compiler_params=_params("parallel"),
        name="s5_step",
    )(*ins)


def _pad_to(a, axis, size):
    pad = [(0, 0)] * a.ndim
    pad[axis] = (0, size - a.shape[axis])
    return jnp.pad(a, pad)


def kernel(x_prompt, x_sample, state_pool, state_mlstm_c, state_mlstm_n, state_mlstm_m, state_s5_re, state_s5_im, norm_mix, norm_ffn, norm_final, w_in_ab, b_gates, pool_w, pool_scale, w_out_ab, s5_lam_re, s5_lam_im, s5_log_step, s5_b_re, s5_b_im, s5_c_re, s5_c_im, s5_d, w_glu, moe_w_group, moe_b_group, moe_w_expert, moe_b_expert, moe_w_gate, moe_w_up, moe_w_down):
    bp, tp, d = x_prompt.shape
    bs = x_sample.shape[0]
    n_p = bp * tp
    tm = TOKEN_TILE
    assert n_p % tm == 0 and bs <= tm
    nh, dh = state_mlstm_c.shape[2], state_mlstm_c.shape[3]
    pool_width = state_pool.shape[3]
    ml_width = nh * dh
    n_main = pool_width + 4 * ml_width
    n_gates = 2 * nh
    ngrp_s5, n_state = s5_lam_re.shape[1], s5_lam_re.shape[2]
    gw_s5 = d // ngrp_s5

    n = n_p + tm

    def tail_tile(sample_rows, dtype):
        return _pad_to(sample_rows.astype(dtype), 0, tm)

    x_main = x_prompt.reshape(n_p, d)
    x_tail = tail_tile(x_sample.reshape(bs, d), F32)
    tril = jnp.tril(jnp.ones((tm, tm), BF16), -1)

    def router_weights(l):
        wr = _pad_to(jnp.concatenate([moe_w_group[l], moe_w_expert[l]], axis=1), 1, LANES)
        br = jnp.concatenate([moe_b_group[l], moe_b_expert[l]])[None, :]
        hi = wr.astype(BF16)
        lo = (wr - hi.astype(F32)).astype(BF16)
        return jnp.concatenate([hi, lo], axis=1), _pad_to(br, 1, LANES)

    w_in = w_in_ab[0]
    z, gates, gates_t = _inproj(
        x_main, x_tail, norm_mix[0][None, :], w_in,
        _pad_to(b_gates[0][None, :], 1, LANES), n_gates)

    pw = pool_w[0].astype(BF16)
    ps = pool_scale[0][None, :]
    pool_y_p, pool_p = _pool_prompt(z, pw, ps, bp, tp)
    pool_y_s, pool_s_t = _pool_step(z, jnp.transpose(state_pool[0], (1, 0, 2)), pw, ps, n_p)
    pool_s = jnp.transpose(pool_s_t, (1, 0, 2))

    ml_y_p, c_p, n_p_st, m_p = _mlstm_prompt(z, gates_t, bp, tp, nh, dh)
    ml_y_p = ml_y_p.reshape(n_p, ml_width)
    g_s = gates[n_p:n_p + bs, :n_gates]
    ml_y_s, c_s, n_s_st, m_s = _mlstm_step(
        z[n_p:n_p + bs].reshape(bs, 1, n_main),
        g_s[:, :nh].reshape(bs, nh, 1, 1), g_s[:, nh:].reshape(bs, nh, 1, 1),
        state_mlstm_c[0], state_mlstm_n[0].reshape(bs, nh, 1, dh),
        state_mlstm_m[0].reshape(bs, nh, 1, 1), nh, dh)

    rows = [x_main, x_tail, pool_y_p, tail_tile(pool_y_s, BF16),
            ml_y_p, tail_tile(ml_y_s.reshape(bs, ml_width), BF16)]
    specs = (_stacked_specs(x_main, x_tail) + _stacked_specs(pool_y_p, rows[3])
             + _stacked_specs(ml_y_p, rows[5]))
    wr, br = router_weights(0)
    x1, hn, rinfo, rinfo_t, counts = _mix_route(
        _outproj_route_kernel, "outproj_route", n, specs, rows, w_out_ab[0],
        norm_ffn[0][None, :], wr, br, tril)
    x2, h1 = _moe(x1, hn, rinfo, rinfo_t, counts, moe_w_gate, moe_w_up, moe_w_down, 0,
                  norm_mix[1][None, :], last_layer=False)

    prep = _s5_prep(s5_log_step[0], s5_lam_re[0], s5_lam_im[0], s5_b_re[0], s5_b_im[0],
                    s5_c_re[0], s5_c_im[0], s5_d[0])
    y_p, hre_p, him_p = _s5_prompt(h1, prep, bp, tp)
    s5_re_p = hre_p.reshape(bp, ngrp_s5, n_state)
    s5_im_p = him_p.reshape(bp, ngrp_s5, n_state)

    gpb = S5_GROUPS_PER_BLOCK
    per_group = lambda a: jnp.transpose(
        a.reshape(ngrp_s5 // gpb, gw_s5, gpb, n_state), (0, 2, 1, 3)).reshape(ngrp_s5, gw_s5, n_state)
    bbr, bbi = per_group(prep[7]), per_group(prep[8])
    lbr, lbi = prep[9].reshape(ngrp_s5, 1, n_state), prep[10].reshape(ngrp_s5, 1, n_state)
    d_g = s5_d[0].reshape(ngrp_s5, 1, gw_s5)
    u_s = h1[n_p:n_p + bs].reshape(bs, ngrp_s5, gw_s5).transpose(1, 0, 2)
    h_re = jnp.transpose(state_s5_re[0], (1, 0, 2))
    h_im = jnp.transpose(state_s5_im[0], (1, 0, 2))
    nre_s, nim_s, y_s = _s5_step(
        u_s, h_re, h_im, bbr, bbi, lbr, lbi,
        jnp.transpose(s5_c_re[0], (0, 2, 1)), jnp.transpose(s5_c_im[0], (0, 2, 1)), d_g)
    s5_re_s = jnp.transpose(nre_s, (1, 0, 2))
    s5_im_s = jnp.transpose(nim_s, (1, 0, 2))
    y_tail = tail_tile(jnp.transpose(y_s, (1, 0, 2)).reshape(bs, d), F32)

    wr, br = router_weights(1)
    specs = [pl.BlockSpec((tm, d), lambda i: (i, 0))] + _stacked_specs(y_p, y_tail)
    x3, hn, rinfo, rinfo_t, counts = _mix_route(
        _glu_route_kernel, "glu_route", n, specs, [x2, y_p, y_tail], w_glu[0],
        norm_ffn[1][None, :], wr, br, tril)
    y_main, y_last = _moe(x3, hn, rinfo, rinfo_t, counts, moe_w_gate, moe_w_up, moe_w_down, 1,
                          norm_final[None, :], last_layer=True)

    return (y_main.reshape(bp, tp, d), y_last[:bs].reshape(bs, 1, d),
            pool_p[None], c_p[None], n_p_st.reshape(1, bp, nh, dh), m_p[:, :, 0, 0][None],
            s5_re_p[None], s5_im_p[None],
            pool_s[None], c_s[None], n_s_st.reshape(1, bs, nh, dh), m_s.reshape(1, bs, nh),
            s5_re_s[None], s5_im_s[None])
```

```python
import functools

import jax
import jax.numpy as jnp
from jax import lax
from jax.experimental import pallas as pl
from jax.experimental.pallas import tpu as pltpu

F32 = jnp.float32
BF16 = jnp.bfloat16
I32 = jnp.int32

PAST_LEN = 16384
POOL_WINDOWS = (2, 4, 8, 16)
POOL_BUF = max(POOL_WINDOWS) - 1
MLSTM_CHUNK = 128
S5_SUB = 16
MOE_GROUPS = 4
MOE_EXPERTS_PER_GROUP = 8
RMS_EPS = 1e-6

LANES = 128
SUBLANES = 8
VMEM_LIMIT_BYTES = 56 * 1024 * 1024

TOKEN_TILE = 512
EXPERT_ROW_TILE = 512
MLSTM_SEQ_PER_STEP = 8
MLSTM_STEP_BATCH = 16
POOL_TIME_TILE = 1024
DMA_ISSUE_UNROLL = 8
S5_GROUPS_PER_BLOCK = 4
S5_BATCH_PER_STEP = 4


def _params(*sem):
    return pltpu.CompilerParams(dimension_semantics=sem, vmem_limit_bytes=VMEM_LIMIT_BYTES)


def _rms(x, g):
    return x * lax.rsqrt(jnp.mean(x * x, axis=-1, keepdims=True) + RMS_EPS) * g


def _dot(a, b):
    return jnp.dot(a, b, preferred_element_type=F32)


def _dot_nt(a, b):
    return lax.dot_general(a, b, (((1,), (1,)), ((), ())), preferred_element_type=F32)


def _dot_tn(a, b):
    return lax.dot_general(a, b, (((0,), (0,)), ((), ())), preferred_element_type=F32)


def _stacked(main_ref, tail_ref):
    last = pl.program_id(0) == pl.num_programs(0) - 1
    return jnp.where(last, tail_ref[...], main_ref[...])


def _stacked_specs(main, tail):
    tm, w = tail.shape
    last_main = main.shape[0] // tm - 1
    return [pl.BlockSpec((tm, w), lambda i: (jnp.minimum(i, last_main), 0)),
            pl.BlockSpec((tm, w), lambda i: (0, 0))]


def _weight_spec(w):
    return pl.BlockSpec(w.shape, lambda i: (0,) * w.ndim, pipeline_mode=pl.Buffered(1))


def _inproj_kernel(xm_ref, xt_ref, g_ref, w_ref, bg_ref, z_ref, gates_ref, gatest_ref, wb):
    @pl.when(pl.program_id(0) == 0)
    def _():
        wb[...] = jnp.zeros(wb.shape, BF16)
        wb[:, 0:w_ref.shape[1]] = w_ref[...].astype(BF16)

    h = _rms(_stacked(xm_ref, xt_ref), g_ref[...]).astype(BF16)
    zg = _dot(h, wb[...])
    nz = z_ref.shape[1]
    z_ref[...] = zg[:, 0:nz]
    gates = zg[:, nz:nz + LANES] + bg_ref[...]
    gates_ref[...] = gates
    gatest_ref[...] = gates.T[0:gatest_ref.shape[0], :]


def _inproj(x_main, x_tail, g, w, bg, ng):
    d = x_main.shape[1]
    tm = TOKEN_TILE
    n = x_main.shape[0] + tm
    nz = w.shape[1] - ng
    full = lambda i: (0, 0)
    return pl.pallas_call(
        _inproj_kernel,
        grid=(n // tm,),
        in_specs=_stacked_specs(x_main, x_tail) + [
            pl.BlockSpec((1, d), full),
            _weight_spec(w),
            pl.BlockSpec((1, LANES), full),
        ],
        out_specs=[
            pl.BlockSpec((tm, nz), lambda i: (i, 0)),
            pl.BlockSpec((tm, LANES), lambda i: (i, 0)),
            pl.BlockSpec((ng, tm), lambda i: (0, i)),
        ],
        out_shape=[
            jax.ShapeDtypeStruct((n, nz), F32),
            jax.ShapeDtypeStruct((n, LANES), F32),
            jax.ShapeDtypeStruct((ng, n), F32),
        ],
        scratch_shapes=[pltpu.VMEM((d, nz + LANES), BF16)],
        compiler_params=_params("arbitrary"),
        name="inproj",
    )(x_main, x_tail, g, w, bg)


def _pool_prompt_kernel(u_ref, pw_ref, ps_ref, y_ref, st_ref, ext_ref, *, tt, gw):
    t = pl.program_id(1)
    nt = pl.num_programs(1)
    halo = POOL_BUF + 1
    width = ext_ref.shape[1]

    @pl.when(t == 0)
    def _():
        ext_ref[0:halo, :] = jnp.zeros((halo, width), F32)

    u = u_ref[...]
    ext_ref[halo:halo + tt, :] = u
    pos = t * tt + lax.broadcasted_iota(I32, (tt, 1), 0)
    for g, w in enumerate(POOL_WINDOWS):
        c0 = g * gw
        acc = u[:, c0:c0 + gw]
        for j in range(1, w):
            acc = acc + ext_ref[halo - j:halo - j + tt, c0:c0 + gw]
        cnt = jnp.minimum(w, pos + 1).astype(F32)
        d = acc / cnt - u[:, c0:c0 + gw]
        y = _dot(d.astype(BF16), pw_ref[g]) * ps_ref[:, c0:c0 + gw]
        y_ref[:, c0:c0 + gw] = y.astype(BF16)

    @pl.when(t == nt - 1)
    def _():
        st_ref[...] = ext_ref[tt + 1:tt + halo, :]

    ext_ref[0:halo, :] = ext_ref[tt:tt + halo, :]


def _pool_prompt(z, pw, ps, batch, seq):
    width = ps.shape[1]
    gw = width // len(POOL_WINDOWS)
    tt = POOL_TIME_TILE
    nt = seq // tt
    return pl.pallas_call(
        functools.partial(_pool_prompt_kernel, tt=tt, gw=gw),
        grid=(batch, nt),
        in_specs=[
            pl.BlockSpec((tt, width), lambda b, t: (b * nt + t, 0)),
            pl.BlockSpec(pw.shape, lambda b, t: (0, 0, 0)),
            pl.BlockSpec((1, width), lambda b, t: (0, 0)),
        ],
        out_specs=[
            pl.BlockSpec((tt, width), lambda b, t: (b * nt + t, 0)),
            pl.BlockSpec((None, POOL_BUF, width), lambda b, t: (b, 0, 0)),
        ],
        out_shape=[
            jax.ShapeDtypeStruct((batch * seq, width), BF16),
            jax.ShapeDtypeStruct((batch, POOL_BUF, width), F32),
        ],
        scratch_shapes=[pltpu.VMEM((POOL_BUF + 1 + tt, width), F32)],
        compiler_params=_params("parallel", "arbitrary"),
        name="pool_prompt",
    )(z, pw, ps)


def _pool_step_kernel(u_ref, buf_ref, pw_ref, ps_ref, y_ref, nb_ref, *, gw):
    u = u_ref[...]
    for g, w in enumerate(POOL_WINDOWS):
        c0 = g * gw
        acc = u[:, c0:c0 + gw]
        for j in range(1, w):
            acc = acc + buf_ref[POOL_BUF - j, :, c0:c0 + gw]
        cnt = float(min(w, PAST_LEN + 1))
        d = acc / cnt - u[:, c0:c0 + gw]
        y = _dot(d.astype(BF16), pw_ref[g]) * ps_ref[:, c0:c0 + gw]
        y_ref[:, c0:c0 + gw] = y.astype(BF16)
    nb_ref[0:POOL_BUF - 1] = buf_ref[1:POOL_BUF]
    nb_ref[POOL_BUF - 1] = u


def _pool_step(z, buf_t, pw, ps, row0):
    _, batch, width = buf_t.shape
    gw = width // len(POOL_WINDOWS)
    return pl.pallas_call(
        functools.partial(_pool_step_kernel, gw=gw),
        grid=(1,),
        in_specs=[
            pl.BlockSpec((batch, width), lambda i: (row0 // batch, 0)),
            pl.BlockSpec(buf_t.shape, lambda i: (0, 0, 0)),
            pl.BlockSpec(pw.shape, lambda i: (0, 0, 0)),
            pl.BlockSpec((1, width), lambda i: (0, 0)),
        ],
        out_specs=[
            pl.BlockSpec((batch, width), lambda i: (0, 0)),
            pl.BlockSpec(buf_t.shape, lambda i: (0, 0, 0)),
        ],
        out_shape=[
            jax.ShapeDtypeStruct((batch, width), BF16),
            jax.ShapeDtypeStruct(buf_t.shape, F32),
        ],
        compiler_params=_params("arbitrary"),
        name="pool_step",
    )(z, buf_t, pw, ps)


def _mlstm_prompt_kernel(*refs, nh, dh, nseq):
    seq_in = [refs[5 * s:5 * s + 5] for s in range(nseq)]
    h_ref, c_out, n_out, m_out, c_s, n_s, m_s = refs[5 * nseq:]
    ci = pl.program_id(1)
    nc = pl.num_programs(1)
    ln = seq_in[0][0].shape[0]

    @pl.when(ci == 0)
    def _():
        c_s[...] = jnp.zeros(c_s.shape, F32)
        n_s[...] = jnp.zeros(n_s.shape, F32)
        m_s[...] = jnp.zeros(m_s.shape, F32)

    row = lax.broadcasted_iota(I32, (ln, ln), 0)
    col = lax.broadcasted_iota(I32, (ln, ln), 1)
    causal_t = row <= col
    eye = col == row
    lane = lax.broadcasted_iota(I32, (nh, ln), 1)

    def to_col(r):
        return jnp.sum(jnp.where(eye, r, 0.0), axis=1, keepdims=True)

    scale = dh ** -0.5
    pairs = [(s, h) for s in range(nseq) for h in range(nh)]
    c_old = {p: c_s[p[0], p[1]] for p in pairs}
    n_old = {p: n_s[p[0], p[1]] for p in pairs}
    m_old = {p: m_s[p[0], p[1]][:, 0:1] for p in pairs}
    gates = []
    for s in range(nseq):
        gt = seq_in[s][4][...]
        bc_all = jax.nn.log_sigmoid(gt[nh:2 * nh])
        sh = 1
        while sh < ln:
            bc_all = bc_all + jnp.where(lane >= sh, pltpu.roll(bc_all, sh, 1), 0.0)
            sh *= 2
        gates.append((gt[0:nh], bc_all))
    ch = {}
    for s, h in pairs:
        q_ref, k_ref, v_ref, _, _ = seq_in[s]
        li_all, bc_all = gates[s]
        sl = slice(h * dh, (h + 1) * dh)
        k = k_ref[:, sl] * scale
        v = v_ref[:, sl]
        li_r, bc_r = li_all[h:h + 1], bc_all[h:h + 1]
        b_end = bc_r[:, ln - 1:ln]
        g_r = b_end - bc_r + li_r
        m0 = m_old[s, h]
        m_new = jnp.maximum(b_end + m0, jnp.max(g_r, axis=1, keepdims=True))
        ch[s, h] = dict(
            qb=q_ref[:, sl].astype(BF16), kb=k.astype(BF16), vb=v.astype(BF16), v=v,
            inter=bc_r + m0, m_new=m_new, wg_r=jnp.exp(g_r - m_new),
            decay=jnp.exp(b_end + m0 - m_new),
            dmat_t=jnp.where(causal_t, bc_r - to_col(bc_r - li_r), -jnp.inf))
    for p in pairs:
        d = ch[p]
        d["st"] = _dot_nt(d["kb"], d["qb"])
        d["cq_t"] = _dot_nt(c_old[p].astype(BF16), d["qb"])
        d["nq"] = _dot_nt(jnp.broadcast_to(n_old[p], (SUBLANES, dh)).astype(BF16), d["qb"])[0:1]
        d["c_add"] = _dot((d["v"].T * d["wg_r"]).astype(BF16), d["kb"])
        d["n_add"] = _dot(jnp.broadcast_to(d["wg_r"], (SUBLANES, ln)).astype(BF16), d["kb"])[0:1]
    for p in pairs:
        d = ch[p]
        d["m_row"] = jnp.maximum(d["inter"], jnp.max(d["dmat_t"], axis=0, keepdims=True))
        d["w_inter"] = jnp.exp(d["inter"] - d["m_row"])
        d["sc_t"] = d["st"] * jnp.exp(d["dmat_t"] - d["m_row"])
    for p in pairs:
        d = ch[p]
        d["pv_t"] = _dot_tn(d["vb"], d["sc_t"].astype(BF16))
    new_state = {}
    for s, h in pairs:
        d = ch[s, h]
        sl = slice(h * dh, (h + 1) * dh)
        num_t = d["pv_t"] + d["w_inter"] * d["cq_t"]
        den = jnp.sum(d["sc_t"], axis=0, keepdims=True) + d["w_inter"] * d["nq"]
        hh = (num_t / jnp.maximum(jnp.abs(den), jnp.exp(-d["m_row"]))).T
        h_ref[s, :, sl] = (hh * jax.nn.sigmoid(seq_in[s][3][:, sl])).astype(BF16)
        new_state[s, h] = (d["decay"] * c_old[s, h] + d["c_add"],
                           d["decay"] * n_old[s, h] + d["n_add"],
                           jnp.broadcast_to(d["m_new"], (1, dh)))

    for (s, h), (c_new, n_new, m_new) in new_state.items():
        c_s[s, h] = c_new
        n_s[s, h] = n_new
        m_s[s, h] = m_new

    @pl.when(ci == nc - 1)
    def _():
        c_out[...] = c_s[...]
        n_out[...] = n_s[...]
        m_out[...] = m_s[...]


def _mlstm_prompt(z, gates_t, batch, seq, nh, dh):
    ln = MLSTM_CHUNK
    nc = seq // ln
    nseq = min(MLSTM_SEQ_PER_STEP, batch)
    assert batch % nseq == 0
    width = nh * dh
    ng = gates_t.shape[0]
    rows_of = lambda s: (lambda b, c: (b * nseq + s) * nc + c)
    in_specs, operands = [], []
    for s in range(nseq):
        r = rows_of(s)
        for j in (1, 2, 3, 4):
            in_specs.append(pl.BlockSpec((ln, width), lambda b, c, r=r, j=j: (r(b, c), j)))
        in_specs.append(pl.BlockSpec((ng, ln), lambda b, c, r=r: (0, r(b, c))))
        operands += [z, z, z, z, gates_t]
    h_spec = pl.BlockSpec((None, nseq, ln, width), lambda b, c: (b, 0, c, 0))
    st = lambda a, b_: pl.BlockSpec((nseq, nh, a, b_), lambda b, c: (b, 0, 0, 0))
    return pl.pallas_call(
        functools.partial(_mlstm_prompt_kernel, nh=nh, dh=dh, nseq=nseq),
        grid=(batch // nseq, nc),
        in_specs=in_specs,
        out_specs=[h_spec, st(dh, dh), st(1, dh), st(1, dh)],
        out_shape=[
            jax.ShapeDtypeStruct((batch // nseq, nseq, seq, width), BF16),
            jax.ShapeDtypeStruct((batch, nh, dh, dh), F32),
            jax.ShapeDtypeStruct((batch, nh, 1, dh), F32),
            jax.ShapeDtypeStruct((batch, nh, 1, dh), F32),
        ],
        scratch_shapes=[pltpu.VMEM((nseq, nh, dh, dh), F32), pltpu.VMEM((nseq, nh, 1, dh), F32),
                        pltpu.VMEM((nseq, nh, 1, dh), F32)],
        compiler_params=_params("parallel", "arbitrary"),
        name="mlstm_prompt",
    )(*operands)


def _mlstm_step_kernel(q_ref, k_ref, v_ref, o_ref, li_ref, fp_ref, c_ref, n_ref, m_ref,
                       h_ref, c_out, n_out, m_out, *, nh, dh):
    eye = (lax.broadcasted_iota(I32, (1, dh, dh), 1) == lax.broadcasted_iota(I32, (1, dh, dh), 2))
    scale = dh ** -0.5
    for h in range(nh):
        sl = slice(h * dh, (h + 1) * dh)
        q = q_ref[:, :, sl]
        k = k_ref[:, :, sl] * scale
        v = v_ref[:, :, sl]
        c = c_ref[:, h]
        n = n_ref[:, h]
        m = m_ref[:, h]
        li = li_ref[:, h]
        lf = jax.nn.log_sigmoid(fp_ref[:, h])
        inter = lf + m
        m_row = jnp.maximum(inter, li)
        w_intra = jnp.exp(li - m_row)
        w_inter = jnp.exp(inter - m_row)
        sc = jnp.sum(q * k, axis=-1, keepdims=True) * w_intra
        q8 = jnp.broadcast_to(q, (q.shape[0], SUBLANES, dh)).astype(BF16)
        cq = jnp.einsum("bqk,bvk->bqv", q8, c.astype(BF16),
                        preferred_element_type=F32)[:, 0:1, :]
        num = sc * v + w_inter * cq
        den = sc + w_inter * jnp.sum(n * q, axis=-1, keepdims=True)
        h_l = num / jnp.maximum(jnp.abs(den), jnp.exp(-m_row))
        h_ref[:, :, sl] = h_l * jax.nn.sigmoid(o_ref[:, :, sl])
        v_c = jnp.sum(jnp.where(eye, v, 0.0), axis=-1, keepdims=True)
        wg = jnp.exp(li - m_row)
        decay = jnp.exp(inter - m_row)
        c_out[:, h] = decay * c + (v_c * wg) * k
        n_out[:, h] = decay * n + wg * k
        m_out[:, h] = m_row


def _mlstm_step(z3, li, fp, c, n, m, nh, dh):
    batch = c.shape[0]
    bb = MLSTM_STEP_BATCH
    width = nh * dh
    blk = lambda j: pl.BlockSpec((bb, 1, width), lambda i: (i, 0, j))
    st4 = lambda a, b: pl.BlockSpec((bb, nh, a, b), lambda i: (i, 0, 0, 0))
    return pl.pallas_call(
        functools.partial(_mlstm_step_kernel, nh=nh, dh=dh),
        grid=(batch // bb,),
        in_specs=[blk(1), blk(2), blk(3), blk(4), st4(1, 1), st4(1, 1),
                  st4(dh, dh), st4(1, dh), st4(1, 1)],
        out_specs=[pl.BlockSpec((bb, 1, width), lambda i: (i, 0, 0)),
                   st4(dh, dh), st4(1, dh), st4(1, 1)],
        out_shape=[
            jax.ShapeDtypeStruct((batch, 1, width), F32),
            jax.ShapeDtypeStruct((batch, nh, dh, dh), F32),
            jax.ShapeDtypeStruct((batch, nh, 1, dh), F32),
            jax.ShapeDtypeStruct((batch, nh, 1, 1), F32),
        ],
        compiler_params=_params("parallel"),
        name="mlstm_step",
    )(z3, z3, z3, z3, li, fp, c, n, m)


def _split_bf16(a):
    hi = a.astype(BF16)
    return hi, (a - hi.astype(F32)).astype(BF16)


def _route(hn, wr_ref, br_ref, tril_ref, carry_ref, rinfo_ref, rinfo_t_ref):
    ngrp, epg = MOE_GROUPS, MOE_EXPERTS_PER_GROUP
    h_hi, h_lo = _split_bf16(hn)
    both = _dot(h_hi, wr_ref[...])
    logits = both[:, 0:LANES] + (both[:, LANES:2 * LANES] + _dot(h_lo, wr_ref[:, 0:LANES])) + br_ref[...]
    tm = logits.shape[0]
    lane = lax.broadcasted_iota(I32, (tm, LANES), 1)
    neg = -jnp.inf

    def first_max(x):
        mx = jnp.max(x, axis=1, keepdims=True)
        idx = jnp.min(jnp.where(x == mx, lane, LANES), axis=1, keepdims=True)
        return mx, idx

    is_grp = lane < ngrp
    gmax, gsel = first_max(jnp.where(is_grp, logits, neg))
    g_w = 1.0 / jnp.sum(jnp.where(is_grp, jnp.exp(logits - gmax), 0.0), axis=1, keepdims=True)
    lo = ngrp + gsel * epg
    el = jnp.where((lane >= lo) & (lane < lo + epg), logits, neg)
    v1, i1 = first_max(el)
    v2, i2 = first_max(jnp.where(lane == i1, neg, el))
    e2 = jnp.exp(v2 - v1)
    w1 = g_w / (1.0 + e2)
    w2 = g_w * e2 / (1.0 + e2)
    eid1 = i1 - ngrp
    eid2 = i2 - ngrp

    hit1 = lane == eid1
    hit2 = lane == eid2
    onehot = jnp.where(hit1 | hit2, 1.0, 0.0)
    carry = carry_ref[...]
    prefix = _dot(tril_ref[...], onehot.astype(BF16)) + carry
    rank1 = jnp.sum(jnp.where(hit1, prefix, 0.0), axis=1, keepdims=True)
    rank2 = jnp.sum(jnp.where(hit2, prefix, 0.0), axis=1, keepdims=True)
    carry_ref[...] = carry + jnp.sum(onehot, axis=0, keepdims=True)

    cols = (eid1.astype(F32), eid2.astype(F32), w1, w2, rank1, rank2)
    info = jnp.zeros((tm, LANES), F32)
    for j, cval in enumerate(cols):
        info = jnp.where(lane == j, cval, info)
    rinfo_ref[...] = info
    pick = jnp.where(lax.broadcasted_iota(I32, (SUBLANES, LANES), 0)
                     == lax.broadcasted_iota(I32, (SUBLANES, LANES), 1), 1.0, 0.0).astype(BF16)
    parts = [_dot_nt(pick, p) for p in _split3(info)]
    rinfo_t_ref[...] = parts[0] + (parts[1] + parts[2])


def _outproj_route_kernel(xm_ref, xt_ref, pm_ref, pt_ref, mm_ref, mt_ref,
                          wo_ref, g_ref, wr_ref, br_ref, tril_ref,
                          x1_ref, hn_ref, rinfo_ref, rinfo_t_ref, cnt_ref, carry_ref, wb):
    @pl.when(pl.program_id(0) == 0)
    def _():
        carry_ref[...] = jnp.zeros(carry_ref.shape, F32)
        wb[...] = wo_ref[...].astype(BF16)

    half = pm_ref.shape[1]
    mix = (_dot(_stacked(pm_ref, pt_ref), wb[0:half, :])
           + _dot(_stacked(mm_ref, mt_ref), wb[half:2 * half, :]))
    x1 = _stacked(xm_ref, xt_ref) + mix
    x1_ref[...] = x1
    hn = _rms(x1, g_ref[...])
    hn_ref[...] = hn
    _route(hn, wr_ref, br_ref, tril_ref, carry_ref, rinfo_ref, rinfo_t_ref)
    cnt_ref[...] = carry_ref[...]


def _glu_route_kernel(x_ref, ym_ref, yt_ref, wglu_ref, g_ref, wr_ref, br_ref, tril_ref,
                      x1_ref, hn_ref, rinfo_ref, rinfo_t_ref, cnt_ref, carry_ref, wb):
    @pl.when(pl.program_id(0) == 0)
    def _():
        carry_ref[...] = jnp.zeros(carry_ref.shape, F32)
        wb[...] = wglu_ref[...].astype(BF16)

    d = x_ref.shape[1]
    ag = _dot(jax.nn.gelu(_stacked(ym_ref, yt_ref)).astype(BF16), wb[...])
    x1 = x_ref[...] + ag[:, 0:d] * jax.nn.sigmoid(ag[:, d:2 * d])
    x1_ref[...] = x1
    hn = _rms(x1, g_ref[...])
    hn_ref[...] = hn
    _route(hn, wr_ref, br_ref, tril_ref, carry_ref, rinfo_ref, rinfo_t_ref)
    cnt_ref[...] = carry_ref[...]


def _mix_route(kernel, name, n, row_specs, rows, w, g, wr, br, tril):
    d = g.shape[1]
    tm = TOKEN_TILE
    full = lambda i: (0, 0)
    return pl.pallas_call(
        kernel,
        grid=(n // tm,),
        in_specs=row_specs + [
            _weight_spec(w),
            pl.BlockSpec((1, d), full),
            pl.BlockSpec((d, 2 * LANES), full),
            pl.BlockSpec((1, LANES), full),
            pl.BlockSpec((tm, tm), full),
        ],
        out_specs=[
            pl.BlockSpec((tm, d), lambda i: (i, 0)),
            pl.BlockSpec((tm, d), lambda i: (i, 0)),
            pl.BlockSpec((tm, LANES), lambda i: (i, 0)),
            pl.BlockSpec((SUBLANES, tm), lambda i: (0, i)),
            pl.BlockSpec((1, LANES), full),
        ],
        out_shape=[
            jax.ShapeDtypeStruct((n, d), F32),
            jax.ShapeDtypeStruct((n, d), F32),
            jax.ShapeDtypeStruct((n, LANES), F32),
            jax.ShapeDtypeStruct((SUBLANES, n), F32),
            jax.ShapeDtypeStruct((1, LANES), F32),
        ],
        scratch_shapes=[pltpu.VMEM((1, LANES), F32), pltpu.VMEM(w.shape, BF16)],
        compiler_params=_params("arbitrary"),
        name=name,
    )(*rows, w, g, wr, br, tril)


def _index_copy(pos_hbm, idx_s, sem_i, tile, slot):
    return pltpu.make_async_copy(pos_hbm.at[tile], idx_s.at[slot], sem_i.at[slot])


def _dispatch_kernel(pos_hbm, hn_hbm, xs_hbm, idx_s, tiles, sem_i, sem_l, sem_d):
    i = pl.program_id(0)
    nt = pl.num_programs(0)
    nslot, tm, _ = tiles.shape
    islot = i % 2

    def load(t):
        s = t % nslot
        return pltpu.make_async_copy(hn_hbm.at[pl.ds(t * tm, tm)], tiles.at[s], sem_l.at[s])

    def wait_rows(t):
        s = t % nslot
        whole = pltpu.make_async_copy(tiles.at[s], xs_hbm.at[pl.ds(0, tm)], sem_d.at[s])
        whole.wait()
        whole.wait()

    @pl.when(i == 0)
    def _():
        _index_copy(pos_hbm, idx_s, sem_i, 0, 0).start()
        load(0).start()

        @pl.when(nt > 1)
        def _():
            load(1).start()

    @pl.when(i >= 2)
    def _():
        wait_rows(i - 2)

    @pl.when(i + 2 < nt)
    def _():
        load(i + 2).start()

    _index_copy(pos_hbm, idx_s, sem_i, i, islot).wait()

    @pl.when(i + 1 < nt)
    def _():
        _index_copy(pos_hbm, idx_s, sem_i, i + 1, 1 - islot).start()

    load(i).wait()
    slot = i % nslot

    def issue(r, carry):
        row = tiles.at[slot, pl.ds(r, 1)]
        pltpu.make_async_copy(row, xs_hbm.at[pl.ds(idx_s[islot, 0, r], 1)],
                              sem_d.at[slot]).start(priority=0)
        pltpu.make_async_copy(row, xs_hbm.at[pl.ds(idx_s[islot, 0, tm + r], 1)],
                              sem_d.at[slot]).start(priority=1)
        return carry

    lax.fori_loop(0, tm, issue, 0, unroll=DMA_ISSUE_UNROLL)

    @pl.when(i == nt - 1)
    def _():
        @pl.when(nt > 1)
        def _():
            wait_rows(i - 1)

        wait_rows(i)


def _dispatch(pos_tiles, hn):
    n, d = hn.shape
    tm = TOKEN_TILE
    nslot = 4
    return pl.pallas_call(
        _dispatch_kernel,
        grid=(n // tm,),
        in_specs=[pl.BlockSpec(memory_space=pl.ANY), pl.BlockSpec(memory_space=pl.ANY)],
        out_specs=pl.BlockSpec(memory_space=pl.ANY),
        out_shape=jax.ShapeDtypeStruct((2 * n, d), F32),
        scratch_shapes=[pltpu.SMEM((2, 1, 2 * tm), I32), pltpu.VMEM((nslot, tm, d), F32),
                        pltpu.SemaphoreType.DMA((2,)), pltpu.SemaphoreType.DMA((nslot,)),
                        pltpu.SemaphoreType.DMA((nslot,))],
        compiler_params=_params("arbitrary"),
        name="moe_dispatch",
    )(pos_tiles, hn)


def _moe_kernel(vt_ref, ve_ref, von_ref, vnext_ref, vslot_ref, lo_ref, hi_ref,
                xs_ref, wg_hbm, wu_hbm, wd_hbm, eo_ref,
                wg_f, wu_f, wd_f, wgb, wub, wdb, sem_w, *, layer):
    v = pl.program_id(0)
    tr = xs_ref.shape[0]
    prev = jnp.maximum(v - 1, 0)
    e = ve_ref[v]
    new_expert = jnp.logical_or(v == 0, e != ve_ref[prev])
    first_visit = jnp.logical_or(v == 0, vt_ref[v] != vt_ref[prev])

    def fetch(expert, slot):
        return [pltpu.make_async_copy(src.at[layer, expert], dst.at[slot], sem_w.at[slot, j])
                for j, (src, dst) in enumerate(((wg_hbm, wg_f), (wu_hbm, wu_f), (wd_hbm, wd_f)))]

    @pl.when(von_ref[v] == 1)
    def _():
        @pl.when(new_expert)
        def _():
            slot = vslot_ref[v]

            @pl.when(v == 0)
            def _():
                for cp in fetch(e, slot):
                    cp.start()

            for cp in fetch(e, slot):
                cp.wait()
            wgb[...] = wg_f[slot].astype(BF16)
            wub[...] = wu_f[slot].astype(BF16)
            wdb[...] = wd_f[slot].astype(BF16)

            @pl.when(vnext_ref[v] >= 0)
            def _():
                for cp in fetch(vnext_ref[v], 1 - slot):
                    cp.start()

        x = xs_ref[...].astype(BF16)
        act = jax.nn.silu(_dot(x, wgb[...])) * _dot(x, wub[...])
        row = vt_ref[v] * tr + lax.broadcasted_iota(I32, (tr, 1), 0)
        act = jnp.where((row >= lo_ref[e]) & (row < hi_ref[e]), act, 0.0)
        res = _dot(act.astype(BF16), wdb[...])

        @pl.when(first_visit)
        def _():
            eo_ref[...] = res

        @pl.when(jnp.logical_not(first_visit))
        def _():
            eo_ref[...] += res


def _moe_experts(vt, ve, von, vnext, vslot, lo, hi, xs, wg, wu, wd, layer):
    nv = vt.shape[0]
    rows, d = xs.shape
    hid = wg.shape[3]
    tr = EXPERT_ROW_TILE
    tile = pl.BlockSpec((tr, d), lambda v, vt, *_: (vt[v], 0))
    grid_spec = pltpu.PrefetchScalarGridSpec(
        num_scalar_prefetch=7,
        grid=(nv,),
        in_specs=[tile] + [pl.BlockSpec(memory_space=pl.ANY)] * 3,
        out_specs=tile,
        scratch_shapes=[
            pltpu.VMEM((2, d, hid), F32), pltpu.VMEM((2, d, hid), F32), pltpu.VMEM((2, hid, d), F32),
            pltpu.VMEM((d, hid), BF16), pltpu.VMEM((d, hid), BF16), pltpu.VMEM((hid, d), BF16),
            pltpu.SemaphoreType.DMA((2, 3)),
        ],
    )
    return pl.pallas_call(
        functools.partial(_moe_kernel, layer=layer),
        grid_spec=grid_spec,
        out_shape=jax.ShapeDtypeStruct((rows, d), F32),
        compiler_params=_params("arbitrary"),
        name="moe_experts",
    )(vt, ve, von, vnext, vslot, lo, hi, xs, wg, wu, wd)


def _combine_kernel(pos_hbm, eo_hbm, x_ref, rinfo_ref, g_ref, o1_ref, o2_ref, idx_s, a_buf, b_buf,
                    sem_i, sem_a, sem_b, *, last_layer):
    i = pl.program_id(0)
    nt = pl.num_programs(0)
    tm = a_buf.shape[1]
    slot = i % 2

    def gathers(s):
        def issue(r, carry):
            pltpu.make_async_copy(eo_hbm.at[pl.ds(idx_s[s, 0, r], 1)],
                                  a_buf.at[s, pl.ds(r, 1)], sem_a.at[s]).start(priority=0)
            pltpu.make_async_copy(eo_hbm.at[pl.ds(idx_s[s, 0, tm + r], 1)],
                                  b_buf.at[s, pl.ds(r, 1)], sem_b.at[s]).start(priority=1)
            return carry

        lax.fori_loop(0, tm, issue, 0, unroll=DMA_ISSUE_UNROLL)

    @pl.when(i == 0)
    def _():
        first = _index_copy(pos_hbm, idx_s, sem_i, 0, 0)
        first.start()
        first.wait()
        gathers(0)

        @pl.when(nt > 1)
        def _():
            _index_copy(pos_hbm, idx_s, sem_i, 1, 1).start()

    @pl.when(i + 1 < nt)
    def _():
        _index_copy(pos_hbm, idx_s, sem_i, i + 1, 1 - slot).wait()
        gathers(1 - slot)

    @pl.when(i + 2 < nt)
    def _():
        _index_copy(pos_hbm, idx_s, sem_i, i + 2, slot).start()

    pltpu.make_async_copy(eo_hbm.at[pl.ds(0, tm)], a_buf.at[slot], sem_a.at[slot]).wait()
    pltpu.make_async_copy(eo_hbm.at[pl.ds(0, tm)], b_buf.at[slot], sem_b.at[slot]).wait()
    info = rinfo_ref[...]
    x2 = x_ref[...] + (info[:, 2:3] * a_buf[slot] + info[:, 3:4] * b_buf[slot])
    hn = _rms(x2, g_ref[...])
    if last_layer:
        @pl.when(i < nt - 1)
        def _():
            o1_ref[...] = hn

        @pl.when(i == nt - 1)
        def _():
            o2_ref[...] = hn
    else:
        o1_ref[...] = x2
        o2_ref[...] = hn


def _combine(pos_tiles, eo, x, rinfo, g, last_layer):
    n, d = x.shape
    tm = TOKEN_TILE
    nt = n // tm
    row = pl.BlockSpec((tm, d), lambda i: (i, 0))
    if last_layer:
        out_specs = [pl.BlockSpec((tm, d), lambda i: (jnp.minimum(i, nt - 2), 0)),
                     pl.BlockSpec((tm, d), lambda i: (0, 0))]
        out_shape = [jax.ShapeDtypeStruct((n - tm, d), F32), jax.ShapeDtypeStruct((tm, d), F32)]
    else:
        out_specs = [row, row]
        out_shape = [jax.ShapeDtypeStruct((n, d), F32), jax.ShapeDtypeStruct((n, d), F32)]
    return pl.pallas_call(
        functools.partial(_combine_kernel, last_layer=last_layer),
        grid=(nt,),
        in_specs=[
            pl.BlockSpec(memory_space=pl.ANY),
            pl.BlockSpec(memory_space=pl.ANY),
            row,
            pl.BlockSpec((tm, LANES), lambda i: (i, 0)),
            pl.BlockSpec((1, d), lambda i: (0, 0)),
        ],
        out_specs=out_specs,
        out_shape=out_shape,
        scratch_shapes=[
            pltpu.SMEM((2, 1, 2 * tm), I32),
            pltpu.VMEM((2, tm, d), F32),
            pltpu.VMEM((2, tm, d), F32),
            pltpu.SemaphoreType.DMA((2,)),
            pltpu.SemaphoreType.DMA((2,)),
            pltpu.SemaphoreType.DMA((2,)),
        ],
        compiler_params=_params("arbitrary"),
        name="moe_combine",
    )(pos_tiles, eo, x, rinfo, g)


def _moe(x1, hn, rinfo, rinfo_t, counts, wg, wu, wd, layer, next_gain, last_layer):
    n = x1.shape[0]
    ne = wg.shape[1]
    tr = EXPERT_ROW_TILE
    tm = TOKEN_TILE
    nv = (2 * n) // tr + ne - 1
    eid = rinfo_t[0:2].astype(I32)
    rank = rinfo_t[4:6].astype(I32)
    cnt = counts[0, :ne].astype(I32)
    seg_end = jnp.cumsum(cnt)
    seg_start = seg_end - cnt
    experts = jnp.arange(ne, dtype=I32)
    pos = rank + jnp.sum(jnp.where(eid[:, None, :] == experts[None, :, None],
                                   seg_start[None, :, None], 0), axis=1)
    pos_tiles = pos.reshape(2, n // tm, tm).transpose(1, 0, 2).reshape(n // tm, 1, 2 * tm)

    first_tile = seg_start // tr
    tiles_e = jnp.where(cnt > 0, (seg_end - 1) // tr - first_tile + 1, 0)
    v_end = jnp.cumsum(tiles_e)
    v_start = v_end - tiles_e
    total = v_end[-1]
    vis = jnp.arange(nv, dtype=I32)
    vc = jnp.minimum(vis, jnp.maximum(total - 1, 0))
    ve = jnp.sum((vc[:, None] >= v_end[None, :]).astype(I32), axis=1)
    pick = lambda tab: jnp.sum(jnp.where(ve[:, None] == experts, tab, 0), axis=1)
    vt = pick(first_tile) + (vc - pick(v_start))
    von = (vis < total).astype(I32)
    present = cnt > 0
    later = present[None, :] & (experts[None, :] > experts[:, None])
    next_of = jnp.min(jnp.where(later, experts[None, :], ne), axis=1)
    vnext = pick(jnp.where(next_of < ne, next_of, -1))
    vslot = pick(jnp.cumsum(present.astype(I32)) - 1) & 1

    xs = _dispatch(pos_tiles, hn)
    eo = _moe_experts(vt, ve, von, vnext, vslot, seg_start, seg_end, xs, wg, wu, wd, layer)
    return _combine(pos_tiles, eo, x1, rinfo, next_gain, last_layer)


def _cis(log_mag, ang):
    mag = jnp.exp(log_mag)
    return mag * jnp.cos(ang), mag * jnp.sin(ang)


def _split3(a):
    p1 = a.astype(BF16)
    r1 = a - p1.astype(F32)
    p2 = r1.astype(BF16)
    return p1, p2, (r1 - p2.astype(F32)).astype(BF16)


def _select_dot(a, sel, sel_first=False):
    sel = sel.astype(BF16)
    parts = [(_dot(sel, p) if sel_first else _dot(p, sel)) for p in _split3(a)]
    return parts[0] + (parts[1] + parts[2])


def _dot3(a, b):
    a_hi, a_lo = _split_bf16(a)
    b_hi, b_lo = _split_bf16(b)
    return _dot(a_hi, b_hi) + (_dot(a_lo, b_hi) + _dot(a_hi, b_lo))


def _s5_prep_kernel(lsr_ref, lrr_ref, lir_ref, ctre_ref, ctim_ref,
                    btre_ref, btim_ref, d_ref,
                    t_ref, wre_ref, wim_ref, cyre_ref, cyim_ref, apr_ref, api_ref,
                    bbr_ref, bbi_ref, lbr_ref, lbi_ref, *, gw, gpb, sub):
    blk = gpb * gw
    wide = sub * blk
    nsl = lrr_ref.shape[2]
    ns = nsl // gpb
    sh_gw, sh_ns = gw.bit_length() - 1, ns.bit_length() - 1
    dt_r = jnp.exp(lsr_ref[0])
    ldt_r_re, ldt_r_im = lrr_ref[0] * dt_r, lir_ref[0] * dt_r
    diag = (lax.broadcasted_iota(I32, (nsl, nsl), 0) == lax.broadcasted_iota(I32, (nsl, nsl), 1))
    to_col = lambda r: jnp.sum(jnp.where(diag, r, 0.0), axis=1, keepdims=True)
    ldt_c_re, ldt_c_im = to_col(ldt_r_re), to_col(ldt_r_im)

    lane_w = lax.broadcasted_iota(I32, (1, wide), 1)
    spread = jnp.where((lax.broadcasted_iota(I32, (gw, wide), 1) & (gw - 1))
                       == lax.broadcasted_iota(I32, (gw, wide), 0), 1.0, 0.0)
    cre = _select_dot(ctre_ref[0], spread)
    cim = _select_dot(ctim_ref[0], spread)
    same = (jnp.right_shift(lax.broadcasted_iota(I32, (nsl, 1), 0), sh_ns)
            == (jnp.right_shift(lane_w, sh_gw) & (gpb - 1)))
    assert 2 * blk == LANES
    tau = lax.broadcasted_iota(I32, (1, LANES), 1).astype(F32)
    pw_re, pw_im = _cis(tau * ldt_c_re, tau * ldt_c_im)
    low = lax.broadcasted_iota(I32, (nsl, LANES), 1) < blk

    def spread_pow(p, first):
        col = lambda t: jnp.broadcast_to(p[:, t:t + 1], (nsl, LANES))
        return jnp.concatenate([jnp.where(low, col(first + 2 * m), col(first + 2 * m + 1))
                                for m in range(sub // 2)], axis=1)

    def c_lam_pow(first):
        pr, pi = spread_pow(pw_re, first), spread_pow(pw_im, first)
        return (jnp.where(same, pr * cre - pi * cim, 0.0),
                jnp.where(same, -(pr * cim + pi * cre), 0.0))

    clr0, cli0 = c_lam_pow(0)
    clr1, cli1 = c_lam_pow(1)
    cyre_ref[0] = clr1.astype(BF16)
    cyim_ref[0] = cli1.astype(BF16)

    lbr, lbi = _cis(ldt_r_re, ldt_r_im)
    lbr_ref[0] = lbr
    lbi_ref[0] = lbi
    lr, li = lrr_ref[0], lir_ref[0]
    nr, ni = lbr - 1.0, lbi
    den = lr * lr + li * li
    fr = (nr * lr + ni * li) / den
    fi = (ni * lr - nr * li) / den
    bre, bim = btre_ref[0], btim_ref[0]
    bbr = fr * bre - fi * bim
    bbi = fr * bim + fi * bre
    bbr_ref[0] = bbr
    bbi_ref[0] = bbi

    lane_t = lax.broadcasted_iota(I32, (gw, wide), 1)
    chan = lax.broadcasted_iota(I32, (gw, wide), 0)
    for g in range(gpb):
        ps = slice(g * ns, (g + 1) * ns)
        r = _dot3(bbr[:, ps], clr0[ps, :]) + _dot3(bbi[:, ps], cli0[ps, :])
        r = r + jnp.where(lane_t == g * gw + chan, d_ref[0][:, g:g + 1], 0.0)
        for j in range(sub):
            tb = r if j == 0 else jnp.where(lane_t >= blk * j, pltpu.roll(r, blk * j, 1), 0.0)
            r0 = j * blk + g * gw
            t_ref[0, r0:r0 + gw, :] = tb.astype(BF16)

    rows = lax.broadcasted_iota(I32, (wide, 1), 0)
    spread_t = jnp.where((lax.broadcasted_iota(I32, (wide, gw), 0) & (gw - 1))
                         == lax.broadcasted_iota(I32, (wide, gw), 1), 1.0, 0.0)
    bbr_t = _select_dot(bbr, spread_t, sel_first=True)
    bbi_t = _select_dot(bbi, spread_t, sel_first=True)
    same_w = ((jnp.right_shift(rows, sh_gw) & (gpb - 1))
              == jnp.right_shift(lax.broadcasted_iota(I32, (1, nsl), 1), sh_ns))
    rj = ((sub - 1) - lax.broadcasted_iota(I32, (sub, 1), 0)).astype(F32)
    q_re, q_im = _cis(rj * ldt_r_re, rj * ldt_r_im)
    per_step = lambda q: jnp.concatenate(
        [jnp.broadcast_to(q[j:j + 1, :], (blk, nsl)) for j in range(sub)], axis=0)
    pr, pi = per_step(q_re), per_step(q_im)
    wre_ref[0] = jnp.where(same_w, pr * bbr_t - pi * bbi_t, 0.0).astype(BF16)
    wim_ref[0] = jnp.where(same_w, pr * bbi_t + pi * bbr_t, 0.0).astype(BF16)

    nlev = apr_ref.shape[1]
    pw = (sub * jnp.left_shift(1, lax.broadcasted_iota(I32, (nlev, 1), 0))).astype(F32)
    apr, api = _cis(pw * ldt_r_re, pw * ldt_r_im)
    apr_ref[0] = apr
    api_ref[0] = api


def _s5_prep(log_step, lam_re, lam_im, b_re, b_im, c_re, c_im, d_skip):
    ng, ns = lam_re.shape
    gw = b_re.shape[2]
    sub, gpb = S5_SUB, S5_GROUPS_PER_BLOCK
    nblk = ng // gpb
    nsl = gpb * ns
    wide = sub * gpb * gw
    ls = jnp.repeat(log_step, ns)
    ct = lambda c: jnp.transpose(c, (0, 2, 1)).reshape(nblk, nsl, gw)
    bt = lambda b: jnp.transpose(b.reshape(nblk, gpb, ns, gw), (0, 3, 1, 2)).reshape(nblk, gw, nsl)
    ins = [ls.reshape(nblk, 1, nsl), lam_re.reshape(nblk, 1, nsl), lam_im.reshape(nblk, 1, nsl),
           ct(c_re), ct(c_im), bt(b_re), bt(b_im),
           jnp.transpose(d_skip.reshape(nblk, gpb, gw), (0, 2, 1))]
    spec = lambda a: pl.BlockSpec((1,) + a.shape[1:], lambda g: (g, 0, 0))
    outs = [
        jax.ShapeDtypeStruct((nblk, wide, wide), BF16),
        jax.ShapeDtypeStruct((nblk, wide, nsl), BF16),
        jax.ShapeDtypeStruct((nblk, wide, nsl), BF16),
        jax.ShapeDtypeStruct((nblk, nsl, wide), BF16),
        jax.ShapeDtypeStruct((nblk, nsl, wide), BF16),
        jax.ShapeDtypeStruct((nblk, SUBLANES, nsl), F32),
        jax.ShapeDtypeStruct((nblk, SUBLANES, nsl), F32),
        jax.ShapeDtypeStruct((nblk, gw, nsl), F32),
        jax.ShapeDtypeStruct((nblk, gw, nsl), F32),
        jax.ShapeDtypeStruct((nblk, 1, nsl), F32),
        jax.ShapeDtypeStruct((nblk, 1, nsl), F32),
    ]
    return pl.pallas_call(
        functools.partial(_s5_prep_kernel, gw=gw, gpb=gpb, sub=sub),
        grid=(nblk,),
        in_specs=[spec(a) for a in ins],
        out_specs=[spec(o) for o in outs],
        out_shape=outs,
        compiler_params=_params("parallel"),
        name="s5_prep",
    )(*ins)


def _s5_prompt_kernel(x_ref, t_ref, wre_ref, wim_ref, cyre_ref, cyim_ref, apr_ref, api_ref,
                      y_ref, hre_out, him_out, u_s, y4_s, *, nb, seq, sub):
    nk = seq // sub
    rows = nb * nk
    hl = LANES // 2
    nlev = nk.bit_length() - 1
    nsl = wre_ref.shape[2]
    low = lax.broadcasted_iota(I32, (nk, LANES), 1) < hl
    kidx = lax.broadcasted_iota(I32, (rows, 1), 0) & (nk - 1)

    def step_rows(b, j):
        return (pl.ds(b * seq + j, nk, stride=sub), slice(None))

    for b in range(nb):
        rs = slice(b * nk, (b + 1) * nk)
        for m in range(sub // 2):
            ls = slice(m * LANES, (m + 1) * LANES)
            s0 = x_ref[step_rows(b, 2 * m)]
            s1 = x_ref[step_rows(b, 2 * m + 1)]
            u_s[0, rs, ls] = jnp.where(low, s0, pltpu.roll(s1, hl, 1)).astype(BF16)
            u_s[1, rs, ls] = jnp.where(low, pltpu.roll(s0, hl, 1), s1).astype(BF16)

    def shifted(a, s):
        return jnp.where(kidx >= s, pltpu.roll(a, s, 0), 0.0)

    halves = range(2)
    us = [u_s[h] for h in halves]
    hre = [_dot(us[h], wre_ref[h]) for h in halves]
    him = [_dot(us[h], wim_ref[h]) for h in halves]
    cw = 2 * LANES
    y_conv = [jnp.concatenate(
        [_dot(us[h][:, 0:c0 + cw], t_ref[h, 0:c0 + cw, c0:c0 + cw]) for c0 in range(0, us[h].shape[1], cw)],
        axis=1) for h in halves]
    for i in range(nlev):
        for h in halves:
            ar = apr_ref[h, i:i + 1, :]
            ai = api_ref[h, i:i + 1, :]
            pre, pim = shifted(hre[h], 1 << i), shifted(him[h], 1 << i)
            hre[h], him[h] = hre[h] + (ar * pre - ai * pim), him[h] + (ar * pim + ai * pre)
    for h in halves:
        ls = slice(h * nsl, (h + 1) * nsl)
        for b in range(nb):
            last = (b + 1) * nk - 1
            hre_out[b:b + 1, ls] = hre[h][last:last + 1, :]
            him_out[b:b + 1, ls] = him[h][last:last + 1, :]
        hpre = shifted(hre[h], 1).astype(BF16)
        hpim = shifted(him[h], 1).astype(BF16)
        y4_s[h] = y_conv[h] + _dot(hpre, cyre_ref[h]) + _dot(hpim, cyim_ref[h])

    for b in range(nb):
        rs = slice(b * nk, (b + 1) * nk)
        for m in range(sub // 2):
            ls = slice(m * LANES, (m + 1) * LANES)
            ca = y4_s[0, rs, ls]
            cb = y4_s[1, rs, ls]
            y_ref[step_rows(b, 2 * m)] = jnp.where(low, ca, pltpu.roll(cb, hl, 1))
            y_ref[step_rows(b, 2 * m + 1)] = jnp.where(low, pltpu.roll(ca, hl, 1), cb)


def _s5_prompt(h, prep, batch, seq):
    d = h.shape[1]
    tmat, wre, wim, cyre, cyim, apr, api = prep[:7]
    nblk, wide, nsl = wre.shape
    sub, nb = S5_SUB, S5_BATCH_PER_STEP
    nk = seq // sub
    assert nk & (nk - 1) == 0 and nk.bit_length() - 1 <= apr.shape[1]
    ntile = d // LANES
    assert nblk == 2 * ntile
    rows = nb * nk
    wspec = lambda a: pl.BlockSpec((2,) + a.shape[1:], lambda t, b: (t, 0, 0))
    xspec = pl.BlockSpec((nb * seq, LANES), lambda t, b: (b, t))
    sspec = pl.BlockSpec((None, nb, 2 * nsl), lambda t, b: (b, 0, t))
    return pl.pallas_call(
        functools.partial(_s5_prompt_kernel, nb=nb, seq=seq, sub=sub),
        grid=(ntile, batch // nb),
        in_specs=[xspec, wspec(tmat), wspec(wre), wspec(wim), wspec(cyre), wspec(cyim),
                  wspec(apr), wspec(api)],
        out_specs=[xspec, sspec, sspec],
        out_shape=[
            jax.ShapeDtypeStruct((batch * seq, d), F32),
            jax.ShapeDtypeStruct((batch // nb, nb, ntile * 2 * nsl), F32),
            jax.ShapeDtypeStruct((batch // nb, nb, ntile * 2 * nsl), F32),
        ],
        scratch_shapes=[
            pltpu.VMEM((2, rows, wide), BF16),
            pltpu.VMEM((2, rows, wide), F32),
        ],
        compiler_params=_params("parallel", "parallel"),
        name="s5_prompt",
    )(h, tmat, wre, wim, cyre, cyim, apr, api)


def _s5_step_kernel(u_ref, hre_ref, him_ref, bbr_ref, bbi_ref, lbr_ref, lbi_ref, cre_ref, cim_ref,
                    d_ref, nre_ref, nim_ref, y_ref):
    for g in range(u_ref.shape[0]):
        u = u_ref[g]
        hre, him, lbr, lbi = hre_ref[g], him_ref[g], lbr_ref[g], lbi_ref[g]
        nre = lbr * hre - lbi * him + _dot3(u, bbr_ref[g])
        nim = lbr * him + lbi * hre + _dot3(u, bbi_ref[g])
        nre_ref[g] = nre
        nim_ref[g] = nim
        y_ref[g] = _dot3(nre, cre_ref[g]) - _dot3(nim, cim_ref[g]) + d_ref[g] * u


def _s5_step(u_s, h_re, h_im, bbr, bbi, lbr, lbi, cre_t, cim_t, d_g):
    ng = u_s.shape[0]
    gb = SUBLANES
    spec = lambda a: pl.BlockSpec((gb,) + a.shape[1:], lambda g: (g, 0, 0))
    ins = [u_s, h_re, h_im, bbr, bbi, lbr, lbi, cre_t, cim_t, d_g]
    outs = [jax.ShapeDtypeStruct(h_re.shape, F32), jax.ShapeDtypeStruct(h_re.shape, F32),
            jax.ShapeDtypeStruct(u_s.shape, F32)]
    return pl.pallas_call(
        _s5_step_kernel,
        grid=(ng // gb,),
        in_specs=[spec(a) for a in ins],
        out_specs=[spec(o) for o in outs],
        out_shape=outs,
        compiler_params=_params("parallel"),
        name="s5_step",
    )(*ins)


def _pad_to(a, axis, size):
    pad = [(0, 0)] * a.ndim
    pad[axis] = (0, size - a.shape[axis])
    return jnp.pad(a, pad)


def kernel(x_prompt, x_sample, state_pool, state_mlstm_c, state_mlstm_n, state_mlstm_m, state_s5_re, state_s5_im, norm_mix, norm_ffn, norm_final, w_in_ab, b_gates, pool_w, pool_scale, w_out_ab, s5_lam_re, s5_lam_im, s5_log_step, s5_b_re, s5_b_im, s5_c_re, s5_c_im, s5_d, w_glu, moe_w_group, moe_b_group, moe_w_expert, moe_b_expert, moe_w_gate, moe_w_up, moe_w_down):
    bp, tp, d = x_prompt.shape
    bs = x_sample.shape[0]
    n_p = bp * tp
    tm = TOKEN_TILE
    assert n_p % tm == 0 and bs <= tm
    nh, dh = state_mlstm_c.shape[2], state_mlstm_c.shape[3]
    pool_width = state_pool.shape[3]
    ml_width = nh * dh
    n_main = pool_width + 4 * ml_width
    n_gates = 2 * nh
    ngrp_s5, n_state = s5_lam_re.shape[1], s5_lam_re.shape[2]
    gw_s5 = d // ngrp_s5

    n = n_p + tm

    def tail_tile(sample_rows, dtype):
        return _pad_to(sample_rows.astype(dtype), 0, tm)

    x_main = x_prompt.reshape(n_p, d)
    x_tail = tail_tile(x_sample.reshape(bs, d), F32)
    tril = jnp.tril(jnp.ones((tm, tm), BF16), -1)

    def router_weights(l):
        wr = _pad_to(jnp.concatenate([moe_w_group[l], moe_w_expert[l]], axis=1), 1, LANES)
        br = jnp.concatenate([moe_b_group[l], moe_b_expert[l]])[None, :]
        hi = wr.astype(BF16)
        lo = (wr - hi.astype(F32)).astype(BF16)
        return jnp.concatenate([hi, lo], axis=1), _pad_to(br, 1, LANES)

    w_in = w_in_ab[0]
    z, gates, gates_t = _inproj(
        x_main, x_tail, norm_mix[0][None, :], w_in,
        _pad_to(b_gates[0][None, :], 1, LANES), n_gates)

    pw = pool_w[0].astype(BF16)
    ps = pool_scale[0][None, :]
    pool_y_p, pool_p = _pool_prompt(z, pw, ps, bp, tp)
    pool_y_s, pool_s_t = _pool_step(z, jnp.transpose(state_pool[0], (1, 0, 2)), pw, ps, n_p)
    pool_s = jnp.transpose(pool_s_t, (1, 0, 2))

    ml_y_p, c_p, n_p_st, m_p = _mlstm_prompt(z, gates_t, bp, tp, nh, dh)
    ml_y_p = ml_y_p.reshape(n_p, ml_width)
    g_s = gates[n_p:n_p + bs, :n_gates]
    ml_y_s, c_s, n_s_st, m_s = _mlstm_step(
        z[n_p:n_p + bs].reshape(bs, 1, n_main),
        g_s[:, :nh].reshape(bs, nh, 1, 1), g_s[:, nh:].reshape(bs, nh, 1, 1),
        state_mlstm_c[0], state_mlstm_n[0].reshape(bs, nh, 1, dh),
        state_mlstm_m[0].reshape(bs, nh, 1, 1), nh, dh)

    rows = [x_main, x_tail, pool_y_p, tail_tile(pool_y_s, BF16),
            ml_y_p, tail_tile(ml_y_s.reshape(bs, ml_width), BF16)]
    specs = (_stacked_specs(x_main, x_tail) + _stacked_specs(pool_y_p, rows[3])
             + _stacked_specs(ml_y_p, rows[5]))
    wr, br = router_weights(0)
    x1, hn, rinfo, rinfo_t, counts = _mix_route(
        _outproj_route_kernel, "outproj_route", n, specs, rows, w_out_ab[0],
        norm_ffn[0][None, :], wr, br, tril)
    x2, h1 = _moe(x1, hn, rinfo, rinfo_t, counts, moe_w_gate, moe_w_up, moe_w_down, 0,
                  norm_mix[1][None, :], last_layer=False)

    prep = _s5_prep(s5_log_step[0], s5_lam_re[0], s5_lam_im[0], s5_b_re[0], s5_b_im[0],
                    s5_c_re[0], s5_c_im[0], s5_d[0])
    y_p, hre_p, him_p = _s5_prompt(h1, prep, bp, tp)
    s5_re_p = hre_p.reshape(bp, ngrp_s5, n_state)
    s5_im_p = him_p.reshape(bp, ngrp_s5, n_state)

    gpb = S5_GROUPS_PER_BLOCK
    per_group = lambda a: jnp.transpose(
        a.reshape(ngrp_s5 // gpb, gw_s5, gpb, n_state), (0, 2, 1, 3)).reshape(ngrp_s5, gw_s5, n_state)
    bbr, bbi = per_group(prep[7]), per_group(prep[8])
    lbr, lbi = prep[9].reshape(ngrp_s5, 1, n_state), prep[10].reshape(ngrp_s5, 1, n_state)
    d_g = s5_d[0].reshape(ngrp_s5, 1, gw_s5)
    u_s = h1[n_p:n_p + bs].reshape(bs, ngrp_s5, gw_s5).transpose(1, 0, 2)
    h_re = jnp.transpose(state_s5_re[0], (1, 0, 2))
    h_im = jnp.transpose(state_s5_im[0], (1, 0, 2))
    nre_s, nim_s, y_s = _s5_step(
        u_s, h_re, h_im, bbr, bbi, lbr, lbi,
        jnp.transpose(s5_c_re[0], (0, 2, 1)), jnp.transpose(s5_c_im[0], (0, 2, 1)), d_g)
    s5_re_s = jnp.transpose(nre_s, (1, 0, 2))
    s5_im_s = jnp.transpose(nim_s, (1, 0, 2))
    y_tail = tail_tile(jnp.transpose(y_s, (1, 0, 2)).reshape(bs, d), F32)

    wr, br = router_weights(1)
    specs = [pl.BlockSpec((tm, d), lambda i: (i, 0))] + _stacked_specs(y_p, y_tail)
    x3, hn, rinfo, rinfo_t, counts = _mix_route(
        _glu_route_kernel, "glu_route", n, specs, [x2, y_p, y_tail], w_glu[0],
        norm_ffn[1][None, :], wr, br, tril)
    y_main, y_last = _moe(x3, hn, rinfo, rinfo_t, counts, moe_w_gate, moe_w_up, moe_w_down, 1,
                          norm_final[None, :], last_layer=True)

    return (y_main.reshape(bp, tp, d), y_last[:bs].reshape(bs, 1, d),
            pool_p[None], c_p[None], n_p_st.reshape(1, bp, nh, dh), m_p[:, :, 0, 0][None],
            s5_re_p[None], s5_im_p[None],
            pool_s[None], c_s[None], n_s_st.reshape(1, bs, nh, dh), m_s.reshape(1, bs, nh),
            s5_re_s[None], s5_im_s[None])
```

```python
import functools

import jax
import jax.numpy as jnp
from jax import lax
from jax.experimental import pallas as pl
from jax.experimental.pallas import tpu as pltpu

F32 = jnp.float32
BF16 = jnp.bfloat16
I32 = jnp.int32

PAST_LEN = 16384
POOL_WINDOWS = (2, 4, 8, 16)
POOL_BUF = max(POOL_WINDOWS) - 1
MLSTM_CHUNK = 128
S5_SUB = 16
MOE_GROUPS = 4
MOE_EXPERTS_PER_GROUP = 8
RMS_EPS = 1e-6

LANES = 128
SUBLANES = 8
VMEM_LIMIT_BYTES = 56 * 1024 * 1024

TOKEN_TILE = 512
EXPERT_ROW_TILE = 512
MLSTM_SEQ_PER_STEP = 8
MLSTM_STEP_BATCH = 16
POOL_TIME_TILE = 1024
DMA_ISSUE_UNROLL = 8
S5_GROUPS_PER_BLOCK = 4
S5_BATCH_PER_STEP = 4


def _params(*sem):
    return pltpu.CompilerParams(dimension_semantics=sem, vmem_limit_bytes=VMEM_LIMIT_BYTES)


def _rms(x, g):
    return x * lax.rsqrt(jnp.mean(x * x, axis=-1, keepdims=True) + RMS_EPS) * g


def _dot(a, b):
    return jnp.dot(a, b, preferred_element_type=F32)


def _dot_nt(a, b):
    return lax.dot_general(a, b, (((1,), (1,)), ((), ())), preferred_element_type=F32)


def _dot_tn(a, b):
    return lax.dot_general(a, b, (((0,), (0,)), ((), ())), preferred_element_type=F32)


def _stacked(main_ref, tail_ref):
    last = pl.program_id(0) == pl.num_programs(0) - 1
    return jnp.where(last, tail_ref[...], main_ref[...])


def _stacked_specs(main, tail):
    tm, w = tail.shape
    last_main = main.shape[0] // tm - 1
    return [pl.BlockSpec((tm, w), lambda i: (jnp.minimum(i, last_main), 0)),
            pl.BlockSpec((tm, w), lambda i: (0, 0))]


def _weight_spec(w):
    return pl.BlockSpec(w.shape, lambda i: (0,) * w.ndim, pipeline_mode=pl.Buffered(1))


def _inproj_kernel(xm_ref, xt_ref, g_ref, w_ref, bg_ref, z_ref, gates_ref, gatest_ref, wb):
    @pl.when(pl.program_id(0) == 0)
    def _():
        wb[...] = jnp.zeros(wb.shape, BF16)
        wb[:, 0:w_ref.shape[1]] = w_ref[...].astype(BF16)

    h = _rms(_stacked(xm_ref, xt_ref), g_ref[...]).astype(BF16)
    zg = _dot(h, wb[...])
    nz = z_ref.shape[1]
    z_ref[...] = zg[:, 0:nz]
    gates = zg[:, nz:nz + LANES] + bg_ref[...]
    gates_ref[...] = gates
    gatest_ref[...] = gates.T[0:gatest_ref.shape[0], :]


def _inproj(x_main, x_tail, g, w, bg, ng):
    d = x_main.shape[1]
    tm = TOKEN_TILE
    n = x_main.shape[0] + tm
    nz = w.shape[1] - ng
    full = lambda i: (0, 0)
    return pl.pallas_call(
        _inproj_kernel,
        grid=(n // tm,),
        in_specs=_stacked_specs(x_main, x_tail) + [
            pl.BlockSpec((1, d), full),
            _weight_spec(w),
            pl.BlockSpec((1, LANES), full),
        ],
        out_specs=[
            pl.BlockSpec((tm, nz), lambda i: (i, 0)),
            pl.BlockSpec((tm, LANES), lambda i: (i, 0)),
            pl.BlockSpec((ng, tm), lambda i: (0, i)),
        ],
        out_shape=[
            jax.ShapeDtypeStruct((n, nz), F32),
            jax.ShapeDtypeStruct((n, LANES), F32),
            jax.ShapeDtypeStruct((ng, n), F32),
        ],
        scratch_shapes=[pltpu.VMEM((d, nz + LANES), BF16)],
        compiler_params=_params("arbitrary"),
        name="inproj",
    )(x_main, x_tail, g, w, bg)


def _pool_prompt_kernel(u_ref, pw_ref, ps_ref, y_ref, st_ref, ext_ref, *, tt, gw):
    t = pl.program_id(1)
    nt = pl.num_programs(1)
    halo = POOL_BUF + 1
    width = ext_ref.shape[1]

    @pl.when(t == 0)
    def _():
        ext_ref[0:halo, :] = jnp.zeros((halo, width), F32)

    u = u_ref[...]
    ext_ref[halo:halo + tt, :] = u
    pos = t * tt + lax.broadcasted_iota(I32, (tt, 1), 0)
    for g, w in enumerate(POOL_WINDOWS):
        c0 = g * gw
        acc = u[:, c0:c0 + gw]
        for j in range(1, w):
            acc = acc + ext_ref[halo - j:halo - j + tt, c0:c0 + gw]
        cnt = jnp.minimum(w, pos + 1).astype(F32)
        d = acc / cnt - u[:, c0:c0 + gw]
        y = _dot(d.astype(BF16), pw_ref[g]) * ps_ref[:, c0:c0 + gw]
        y_ref[:, c0:c0 + gw] = y.astype(BF16)

    @pl.when(t == nt - 1)
    def _():
        st_ref[...] = ext_ref[tt + 1:tt + halo, :]

    ext_ref[0:halo, :] = ext_ref[tt:tt + halo, :]


def _pool_prompt(z, pw, ps, batch, seq):
    width = ps.shape[1]
    gw = width // len(POOL_WINDOWS)
    tt = POOL_TIME_TILE
    nt = seq // tt
    return pl.pallas_call(
        functools.partial(_pool_prompt_kernel, tt=tt, gw=gw),
        grid=(batch, nt),
        in_specs=[
            pl.BlockSpec((tt, width), lambda b, t: (b * nt + t, 0)),
            pl.BlockSpec(pw.shape, lambda b, t: (0, 0, 0)),
            pl.BlockSpec((1, width), lambda b, t: (0, 0)),
        ],
        out_specs=[
            pl.BlockSpec((tt, width), lambda b, t: (b * nt + t, 0)),
            pl.BlockSpec((None, POOL_BUF, width), lambda b, t: (b, 0, 0)),
        ],
        out_shape=[
            jax.ShapeDtypeStruct((batch * seq, width), BF16),
            jax.ShapeDtypeStruct((batch, POOL_BUF, width), F32),
        ],
        scratch_shapes=[pltpu.VMEM((POOL_BUF + 1 + tt, width), F32)],
        compiler_params=_params("parallel", "arbitrary"),
        name="pool_prompt",
    )(z, pw, ps)


def _pool_step_kernel(u_ref, buf_ref, pw_ref, ps_ref, y_ref, nb_ref, *, gw):
    u = u_ref[...]
    width = u.shape[1]
    for g, w in enumerate(POOL_WINDOWS):
        c0 = g * gw
        acc = u[:, c0:c0 + gw]
        for j in range(1, w):
            r0 = (POOL_BUF - j) * width + c0
            acc = acc + buf_ref[:, r0:r0 + gw]
        cnt = float(min(w, PAST_LEN + 1))
        d = acc / cnt - u[:, c0:c0 + gw]
        y = _dot(d.astype(BF16), pw_ref[g]) * ps_ref[:, c0:c0 + gw]
        y_ref[:, c0:c0 + gw] = y.astype(BF16)
    keep = (POOL_BUF - 1) * width
    nb_ref[:, 0:keep] = buf_ref[:, width:width + keep]
    nb_ref[:, keep:keep + width] = u


def _pool_step(z, buf, pw, ps, row0):
    batch, _, width = buf.shape
    gw = width // len(POOL_WINDOWS)
    flat = buf.reshape(batch, POOL_BUF * width)
    y, new = pl.pallas_call(
        functools.partial(_pool_step_kernel, gw=gw),
        grid=(1,),
        in_specs=[
            pl.BlockSpec((batch, width), lambda i: (row0 // batch, 0)),
            pl.BlockSpec(flat.shape, lambda i: (0, 0)),
            pl.BlockSpec(pw.shape, lambda i: (0, 0, 0)),
            pl.BlockSpec((1, width), lambda i: (0, 0)),
        ],
        out_specs=[
            pl.BlockSpec((batch, width), lambda i: (0, 0)),
            pl.BlockSpec(flat.shape, lambda i: (0, 0)),
        ],
        out_shape=[
            jax.ShapeDtypeStruct((batch, width), BF16),
            jax.ShapeDtypeStruct(flat.shape, F32),
        ],
        compiler_params=_params("arbitrary"),
        name="pool_step",
    )(z, flat, pw, ps)
    return y, new.reshape(buf.shape)


def _mlstm_prompt_kernel(*refs, nh, dh, nseq):
    seq_in = [refs[5 * s:5 * s + 5] for s in range(nseq)]
    h_ref, c_out, n_out, m_out, c_s, n_s, m_s = refs[5 * nseq:]
    ci = pl.program_id(1)
    nc = pl.num_programs(1)
    ln = seq_in[0][0].shape[0]

    @pl.when(ci == 0)
    def _():
        c_s[...] = jnp.zeros(c_s.shape, F32)
        n_s[...] = jnp.zeros(n_s.shape, F32)
        m_s[...] = jnp.zeros(m_s.shape, F32)

    row = lax.broadcasted_iota(I32, (ln, ln), 0)
    col = lax.broadcasted_iota(I32, (ln, ln), 1)
    causal_t = row <= col
    eye = col == row
    lane = lax.broadcasted_iota(I32, (nh, ln), 1)

    def to_col(r):
        return jnp.sum(jnp.where(eye, r, 0.0), axis=1, keepdims=True)

    scale = dh ** -0.5
    pairs = [(s, h) for s in range(nseq) for h in range(nh)]
    c_old = {p: c_s[p[0], p[1]] for p in pairs}
    n_old = {p: n_s[p[0], p[1]] for p in pairs}
    m_old = {p: m_s[p[0], p[1]][:, 0:1] for p in pairs}
    gates = []
    for s in range(nseq):
        gt = seq_in[s][4][...]
        bc_all = jax.nn.log_sigmoid(gt[nh:2 * nh])
        sh = 1
        while sh < ln:
            bc_all = bc_all + jnp.where(lane >= sh, pltpu.roll(bc_all, sh, 1), 0.0)
            sh *= 2
        gates.append((gt[0:nh], bc_all))
    ch = {}
    for s, h in pairs:
        q_ref, k_ref, v_ref, _, _ = seq_in[s]
        li_all, bc_all = gates[s]
        sl = slice(h * dh, (h + 1) * dh)
        k = k_ref[:, sl] * scale
        v = v_ref[:, sl]
        li_r, bc_r = li_all[h:h + 1], bc_all[h:h + 1]
        b_end = bc_r[:, ln - 1:ln]
        g_r = b_end - bc_r + li_r
        m0 = m_old[s, h]
        m_new = jnp.maximum(b_end + m0, jnp.max(g_r, axis=1, keepdims=True))
        ch[s, h] = dict(
            qb=q_ref[:, sl].astype(BF16), kb=k.astype(BF16), vb=v.astype(BF16), v=v,
            inter=bc_r + m0, m_new=m_new, wg_r=jnp.exp(g_r - m_new),
            decay=jnp.exp(b_end + m0 - m_new),
            dmat_t=jnp.where(causal_t, bc_r - to_col(bc_r - li_r), -jnp.inf))
    for p in pairs:
        d = ch[p]
        d["st"] = _dot_nt(d["kb"], d["qb"])
        d["cq_t"] = _dot_nt(c_old[p].astype(BF16), d["qb"])
        d["nq"] = _dot_nt(jnp.broadcast_to(n_old[p], (SUBLANES, dh)).astype(BF16), d["qb"])[0:1]
        d["c_add"] = _dot((d["v"].T * d["wg_r"]).astype(BF16), d["kb"])
        d["n_add"] = _dot(jnp.broadcast_to(d["wg_r"], (SUBLANES, ln)).astype(BF16), d["kb"])[0:1]
    for p in pairs:
        d = ch[p]
        d["m_row"] = jnp.maximum(d["inter"], jnp.max(d["dmat_t"], axis=0, keepdims=True))
        d["w_inter"] = jnp.exp(d["inter"] - d["m_row"])
        d["sc_t"] = d["st"] * jnp.exp(d["dmat_t"] - d["m_row"])
    for p in pairs:
        d = ch[p]
        d["pv_t"] = _dot_tn(d["vb"], d["sc_t"].astype(BF16))
    new_state = {}
    for s, h in pairs:
        d = ch[s, h]
        sl = slice(h * dh, (h + 1) * dh)
        num_t = d["pv_t"] + d["w_inter"] * d["cq_t"]
        den = jnp.sum(d["sc_t"], axis=0, keepdims=True) + d["w_inter"] * d["nq"]
        hh = (num_t / jnp.maximum(jnp.abs(den), jnp.exp(-d["m_row"]))).T
        h_ref[s, :, sl] = (hh * jax.nn.sigmoid(seq_in[s][3][:, sl])).astype(BF16)
        new_state[s, h] = (d["decay"] * c_old[s, h] + d["c_add"],
                           d["decay"] * n_old[s, h] + d["n_add"],
                           jnp.broadcast_to(d["m_new"], (1, dh)))

    for (s, h), (c_new, n_new, m_new) in new_state.items():
        c_s[s, h] = c_new
        n_s[s, h] = n_new
        m_s[s, h] = m_new

    @pl.when(ci == nc - 1)
    def _():
        c_out[...] = c_s[...]
        n_out[...] = n_s[...]
        m_out[...] = m_s[...]


def _mlstm_prompt(z, gates_t, batch, seq, nh, dh):
    ln = MLSTM_CHUNK
    nc = seq // ln
    nseq = min(MLSTM_SEQ_PER_STEP, batch)
    assert batch % nseq == 0
    width = nh * dh
    ng = gates_t.shape[0]
    rows_of = lambda s: (lambda b, c: (b * nseq + s) * nc + c)
    in_specs, operands = [], []
    for s in range(nseq):
        r = rows_of(s)
        for j in (1, 2, 3, 4):
            in_specs.append(pl.BlockSpec((ln, width), lambda b, c, r=r, j=j: (r(b, c), j)))
        in_specs.append(pl.BlockSpec((ng, ln), lambda b, c, r=r: (0, r(b, c))))
        operands += [z, z, z, z, gates_t]
    h_spec = pl.BlockSpec((None, nseq, ln, width), lambda b, c: (b, 0, c, 0))
    st = lambda a, b_: pl.BlockSpec((nseq, nh, a, b_), lambda b, c: (b, 0, 0, 0))
    return pl.pallas_call(
        functools.partial(_mlstm_prompt_kernel, nh=nh, dh=dh, nseq=nseq),
        grid=(batch // nseq, nc),
        in_specs=in_specs,
        out_specs=[h_spec, st(dh, dh), st(1, dh), st(1, dh)],
        out_shape=[
            jax.ShapeDtypeStruct((batch // nseq, nseq, seq, width), BF16),
            jax.ShapeDtypeStruct((batch, nh, dh, dh), F32),
            jax.ShapeDtypeStruct((batch, nh, 1, dh), F32),
            jax.ShapeDtypeStruct((batch, nh, 1, dh), F32),
        ],
        scratch_shapes=[pltpu.VMEM((nseq, nh, dh, dh), F32), pltpu.VMEM((nseq, nh, 1, dh), F32),
                        pltpu.VMEM((nseq, nh, 1, dh), F32)],
        compiler_params=_params("parallel", "arbitrary"),
        name="mlstm_prompt",
    )(*operands)


def _mlstm_step_kernel(q_ref, k_ref, v_ref, o_ref, li_ref, fp_ref, c_ref, n_ref, m_ref,
                       h_ref, c_out, n_out, m_out, *, nh, dh):
    eye = (lax.broadcasted_iota(I32, (1, dh, dh), 1) == lax.broadcasted_iota(I32, (1, dh, dh), 2))
    scale = dh ** -0.5
    for h in range(nh):
        sl = slice(h * dh, (h + 1) * dh)
        q = q_ref[:, :, sl]
        k = k_ref[:, :, sl] * scale
        v = v_ref[:, :, sl]
        c = c_ref[:, h]
        n = n_ref[:, h]
        m = m_ref[:, h]
        li = li_ref[:, h]
        lf = jax.nn.log_sigmoid(fp_ref[:, h])
        inter = lf + m
        m_row = jnp.maximum(inter, li)
        w_intra = jnp.exp(li - m_row)
        w_inter = jnp.exp(inter - m_row)
        sc = jnp.sum(q * k, axis=-1, keepdims=True) * w_intra
        q8 = jnp.broadcast_to(q, (q.shape[0], SUBLANES, dh)).astype(BF16)
        cq = jnp.einsum("bqk,bvk->bqv", q8, c.astype(BF16),
                        preferred_element_type=F32)[:, 0:1, :]
        num = sc * v + w_inter * cq
        den = sc + w_inter * jnp.sum(n * q, axis=-1, keepdims=True)
        h_l = num / jnp.maximum(jnp.abs(den), jnp.exp(-m_row))
        h_ref[:, :, sl] = h_l * jax.nn.sigmoid(o_ref[:, :, sl])
        v_c = jnp.sum(jnp.where(eye, v, 0.0), axis=-1, keepdims=True)
        wg = jnp.exp(li - m_row)
        decay = jnp.exp(inter - m_row)
        c_out[:, h] = decay * c + (v_c * wg) * k
        n_out[:, h] = decay * n + wg * k
        m_out[:, h] = m_row


def _mlstm_step(z3, li, fp, c, n, m, nh, dh):
    batch = c.shape[0]
    bb = MLSTM_STEP_BATCH
    width = nh * dh
    blk = lambda j: pl.BlockSpec((bb, 1, width), lambda i: (i, 0, j))
    st4 = lambda a, b: pl.BlockSpec((bb, nh, a, b), lambda i: (i, 0, 0, 0))
    return pl.pallas_call(
        functools.partial(_mlstm_step_kernel, nh=nh, dh=dh),
        grid=(batch // bb,),
        in_specs=[blk(1), blk(2), blk(3), blk(4), st4(1, 1), st4(1, 1),
                  st4(dh, dh), st4(1, dh), st4(1, 1)],
        out_specs=[pl.BlockSpec((bb, 1, width), lambda i: (i, 0, 0)),
                   st4(dh, dh), st4(1, dh), st4(1, 1)],
        out_shape=[
            jax.ShapeDtypeStruct((batch, 1, width), F32),
            jax.ShapeDtypeStruct((batch, nh, dh, dh), F32),
            jax.ShapeDtypeStruct((batch, nh, 1, dh), F32),
            jax.ShapeDtypeStruct((batch, nh, 1, 1), F32),
        ],
        compiler_params=_params("parallel"),
        name="mlstm_step",
    )(z3, z3, z3, z3, li, fp, c, n, m)


def _split_bf16(a):
    hi = a.astype(BF16)
    return hi, (a - hi.astype(F32)).astype(BF16)


def _route(hn, wr_ref, br_ref, tril_ref, carry_ref, rinfo_ref, rinfo_t_ref):
    ngrp, epg = MOE_GROUPS, MOE_EXPERTS_PER_GROUP
    h_hi, h_lo = _split_bf16(hn)
    both = _dot(h_hi, wr_ref[...])
    logits = both[:, 0:LANES] + (both[:, LANES:2 * LANES] + _dot(h_lo, wr_ref[:, 0:LANES])) + br_ref[...]
    tm = logits.shape[0]
    lane = lax.broadcasted_iota(I32, (tm, LANES), 1)
    neg = -jnp.inf

    def first_max(x):
        mx = jnp.max(x, axis=1, keepdims=True)
        idx = jnp.min(jnp.where(x == mx, lane, LANES), axis=1, keepdims=True)
        return mx, idx

    is_grp = lane < ngrp
    gmax, gsel = first_max(jnp.where(is_grp, logits, neg))
    g_w = 1.0 / jnp.sum(jnp.where(is_grp, jnp.exp(logits - gmax), 0.0), axis=1, keepdims=True)
    lo = ngrp + gsel * epg
    el = jnp.where((lane >= lo) & (lane < lo + epg), logits, neg)
    v1, i1 = first_max(el)
    v2, i2 = first_max(jnp.where(lane == i1, neg, el))
    e2 = jnp.exp(v2 - v1)
    w1 = g_w / (1.0 + e2)
    w2 = g_w * e2 / (1.0 + e2)
    eid1 = i1 - ngrp
    eid2 = i2 - ngrp

    hit1 = lane == eid1
    hit2 = lane == eid2
    onehot = jnp.where(hit1 | hit2, 1.0, 0.0)
    carry = carry_ref[...]
    prefix = _dot(tril_ref[...], onehot.astype(BF16)) + carry
    rank1 = jnp.sum(jnp.where(hit1, prefix, 0.0), axis=1, keepdims=True)
    rank2 = jnp.sum(jnp.where(hit2, prefix, 0.0), axis=1, keepdims=True)
    carry_ref[...] = carry + jnp.sum(onehot, axis=0, keepdims=True)

    cols = (eid1.astype(F32), eid2.astype(F32), w1, w2, rank1, rank2)
    info = jnp.zeros((tm, LANES), F32)
    for j, cval in enumerate(cols):
        info = jnp.where(lane == j, cval, info)
    rinfo_ref[...] = info
    pick = jnp.where(lax.broadcasted_iota(I32, (SUBLANES, LANES), 0)
                     == lax.broadcasted_iota(I32, (SUBLANES, LANES), 1), 1.0, 0.0).astype(BF16)
    parts = [_dot_nt(pick, p) for p in _split3(info)]
    rinfo_t_ref[...] = parts[0] + (parts[1] + parts[2])


def _outproj_route_kernel(xm_ref, xt_ref, pm_ref, pt_ref, mm_ref, mt_ref,
                          wo_ref, g_ref, wr_ref, br_ref, tril_ref,
                          x1_ref, hn_ref, rinfo_ref, rinfo_t_ref, cnt_ref, carry_ref, wb):
    @pl.when(pl.program_id(0) == 0)
    def _():
        carry_ref[...] = jnp.zeros(carry_ref.shape, F32)
        wb[...] = wo_ref[...].astype(BF16)

    half = pm_ref.shape[1]
    mix = (_dot(_stacked(pm_ref, pt_ref), wb[0:half, :])
           + _dot(_stacked(mm_ref, mt_ref), wb[half:2 * half, :]))
    x1 = _stacked(xm_ref, xt_ref) + mix
    x1_ref[...] = x1
    hn = _rms(x1, g_ref[...])
    hn_ref[...] = hn
    _route(hn, wr_ref, br_ref, tril_ref, carry_ref, rinfo_ref, rinfo_t_ref)
    cnt_ref[...] = carry_ref[...]


def _glu_route_kernel(x_ref, ym_ref, yt_ref, wglu_ref, g_ref, wr_ref, br_ref, tril_ref,
                      x1_ref, hn_ref, rinfo_ref, rinfo_t_ref, cnt_ref, carry_ref, wb):
    @pl.when(pl.program_id(0) == 0)
    def _():
        carry_ref[...] = jnp.zeros(carry_ref.shape, F32)
        wb[...] = wglu_ref[...].astype(BF16)

    d = x_ref.shape[1]
    ag = _dot(jax.nn.gelu(_stacked(ym_ref, yt_ref)).astype(BF16), wb[...])
    x1 = x_ref[...] + ag[:, 0:d] * jax.nn.sigmoid(ag[:, d:2 * d])
    x1_ref[...] = x1
    hn = _rms(x1, g_ref[...])
    hn_ref[...] = hn
    _route(hn, wr_ref, br_ref, tril_ref, carry_ref, rinfo_ref, rinfo_t_ref)
    cnt_ref[...] = carry_ref[...]


def _mix_route(kernel, name, n, row_specs, rows, w, g, wr, br, tril):
    d = g.shape[1]
    tm = TOKEN_TILE
    full = lambda i: (0, 0)
    return pl.pallas_call(
        kernel,
        grid=(n // tm,),
        in_specs=row_specs + [
            _weight_spec(w),
            pl.BlockSpec((1, d), full),
            pl.BlockSpec((d, 2 * LANES), full),
            pl.BlockSpec((1, LANES), full),
            pl.BlockSpec((tm, tm), full),
        ],
        out_specs=[
            pl.BlockSpec((tm, d), lambda i: (i, 0)),
            pl.BlockSpec((tm, d), lambda i: (i, 0)),
            pl.BlockSpec((tm, LANES), lambda i: (i, 0)),
            pl.BlockSpec((SUBLANES, tm), lambda i: (0, i)),
            pl.BlockSpec((1, LANES), full),
        ],
        out_shape=[
            jax.ShapeDtypeStruct((n, d), F32),
            jax.ShapeDtypeStruct((n, d), F32),
            jax.ShapeDtypeStruct((n, LANES), F32),
            jax.ShapeDtypeStruct((SUBLANES, n), F32),
            jax.ShapeDtypeStruct((1, LANES), F32),
        ],
        scratch_shapes=[pltpu.VMEM((1, LANES), F32), pltpu.VMEM(w.shape, BF16)],
        compiler_params=_params("arbitrary"),
        name=name,
    )(*rows, w, g, wr, br, tril)


def _index_copy(pos_hbm, idx_s, sem_i, tile, slot):
    return pltpu.make_async_copy(pos_hbm.at[tile], idx_s.at[slot], sem_i.at[slot])


def _dispatch_kernel(pos_hbm, hn_hbm, xs_hbm, idx_s, tiles, sem_i, sem_l, sem_d):
    i = pl.program_id(0)
    nt = pl.num_programs(0)
    nslot, tm, _ = tiles.shape
    islot = i % 2

    def load(t):
        s = t % nslot
        return pltpu.make_async_copy(hn_hbm.at[pl.ds(t * tm, tm)], tiles.at[s], sem_l.at[s])

    def wait_rows(t):
        s = t % nslot
        whole = pltpu.make_async_copy(tiles.at[s], xs_hbm.at[pl.ds(0, tm)], sem_d.at[s])
        whole.wait()
        whole.wait()

    @pl.when(i == 0)
    def _():
        _index_copy(pos_hbm, idx_s, sem_i, 0, 0).start()
        load(0).start()

        @pl.when(nt > 1)
        def _():
            load(1).start()

    @pl.when(i >= 2)
    def _():
        wait_rows(i - 2)

    @pl.when(i + 2 < nt)
    def _():
        load(i + 2).start()

    _index_copy(pos_hbm, idx_s, sem_i, i, islot).wait()

    @pl.when(i + 1 < nt)
    def _():
        _index_copy(pos_hbm, idx_s, sem_i, i + 1, 1 - islot).start()

    load(i).wait()
    slot = i % nslot

    def issue(r, carry):
        row = tiles.at[slot, pl.ds(r, 1)]
        pltpu.make_async_copy(row, xs_hbm.at[pl.ds(idx_s[islot, 0, r], 1)],
                              sem_d.at[slot]).start(priority=0)
        pltpu.make_async_copy(row, xs_hbm.at[pl.ds(idx_s[islot, 0, tm + r], 1)],
                              sem_d.at[slot]).start(priority=1)
        return carry

    lax.fori_loop(0, tm, issue, 0, unroll=DMA_ISSUE_UNROLL)

    @pl.when(i == nt - 1)
    def _():
        @pl.when(nt > 1)
        def _():
            wait_rows(i - 1)

        wait_rows(i)


def _dispatch(pos_tiles, hn):
    n, d = hn.shape
    tm = TOKEN_TILE
    nslot = 4
    return pl.pallas_call(
        _dispatch_kernel,
        grid=(n // tm,),
        in_specs=[pl.BlockSpec(memory_space=pl.ANY), pl.BlockSpec(memory_space=pl.ANY)],
        out_specs=pl.BlockSpec(memory_space=pl.ANY),
        out_shape=jax.ShapeDtypeStruct((2 * n, d), F32),
        scratch_shapes=[pltpu.SMEM((2, 1, 2 * tm), I32), pltpu.VMEM((nslot, tm, d), F32),
                        pltpu.SemaphoreType.DMA((2,)), pltpu.SemaphoreType.DMA((nslot,)),
                        pltpu.SemaphoreType.DMA((nslot,))],
        compiler_params=_params("arbitrary"),
        name="moe_dispatch",
    )(pos_tiles, hn)


def _moe_kernel(vt_ref, ve_ref, von_ref, vnext_ref, vslot_ref, lo_ref, hi_ref,
                xs_ref, wg_hbm, wu_hbm, wd_hbm, eo_ref,
                wg_f, wu_f, wd_f, wgb, wub, wdb, sem_w, *, layer):
    v = pl.program_id(0)
    tr = xs_ref.shape[0]
    prev = jnp.maximum(v - 1, 0)
    e = ve_ref[v]
    new_expert = jnp.logical_or(v == 0, e != ve_ref[prev])
    first_visit = jnp.logical_or(v == 0, vt_ref[v] != vt_ref[prev])

    def fetch(expert, slot):
        return [pltpu.make_async_copy(src.at[layer, expert], dst.at[slot], sem_w.at[slot, j])
                for j, (src, dst) in enumerate(((wg_hbm, wg_f), (wu_hbm, wu_f), (wd_hbm, wd_f)))]

    @pl.when(von_ref[v] == 1)
    def _():
        @pl.when(new_expert)
        def _():
            slot = vslot_ref[v]

            @pl.when(v == 0)
            def _():
                for cp in fetch(e, slot):
                    cp.start()

            for cp in fetch(e, slot):
                cp.wait()
            wgb[...] = wg_f[slot].astype(BF16)
            wub[...] = wu_f[slot].astype(BF16)
            wdb[...] = wd_f[slot].astype(BF16)

            @pl.when(vnext_ref[v] >= 0)
            def _():
                for cp in fetch(vnext_ref[v], 1 - slot):
                    cp.start()

        x = xs_ref[...].astype(BF16)
        act = jax.nn.silu(_dot(x, wgb[...])) * _dot(x, wub[...])
        row = vt_ref[v] * tr + lax.broadcasted_iota(I32, (tr, 1), 0)
        act = jnp.where((row >= lo_ref[e]) & (row < hi_ref[e]), act, 0.0)
        res = _dot(act.astype(BF16), wdb[...])

        @pl.when(first_visit)
        def _():
            eo_ref[...] = res

        @pl.when(jnp.logical_not(first_visit))
        def _():
            eo_ref[...] += res


def _moe_experts(vt, ve, von, vnext, vslot, lo, hi, xs, wg, wu, wd, layer):
    nv = vt.shape[0]
    rows, d = xs.shape
    hid = wg.shape[3]
    tr = EXPERT_ROW_TILE
    tile = pl.BlockSpec((tr, d), lambda v, vt, *_: (vt[v], 0))
    grid_spec = pltpu.PrefetchScalarGridSpec(
        num_scalar_prefetch=7,
        grid=(nv,),
        in_specs=[tile] + [pl.BlockSpec(memory_space=pl.ANY)] * 3,
        out_specs=tile,
        scratch_shapes=[
            pltpu.VMEM((2, d, hid), F32), pltpu.VMEM((2, d, hid), F32), pltpu.VMEM((2, hid, d), F32),
            pltpu.VMEM((d, hid), BF16), pltpu.VMEM((d, hid), BF16), pltpu.VMEM((hid, d), BF16),
            pltpu.SemaphoreType.DMA((2, 3)),
        ],
    )
    return pl.pallas_call(
        functools.partial(_moe_kernel, layer=layer),
        grid_spec=grid_spec,
        out_shape=jax.ShapeDtypeStruct((rows, d), F32),
        compiler_params=_params("arbitrary"),
        name="moe_experts",
    )(vt, ve, von, vnext, vslot, lo, hi, xs, wg, wu, wd)


def _combine_kernel(pos_hbm, eo_hbm, x_ref, rinfo_ref, g_ref, o1_ref, o2_ref, idx_s, a_buf, b_buf,
                    sem_i, sem_a, sem_b, *, last_layer):
    i = pl.program_id(0)
    nt = pl.num_programs(0)
    tm = a_buf.shape[1]
    slot = i % 2

    def gathers(s):
        def issue(r, carry):
            pltpu.make_async_copy(eo_hbm.at[pl.ds(idx_s[s, 0, r], 1)],
                                  a_buf.at[s, pl.ds(r, 1)], sem_a.at[s]).start(priority=0)
            pltpu.make_async_copy(eo_hbm.at[pl.ds(idx_s[s, 0, tm + r], 1)],
                                  b_buf.at[s, pl.ds(r, 1)], sem_b.at[s]).start(priority=1)
            return carry

        lax.fori_loop(0, tm, issue, 0, unroll=DMA_ISSUE_UNROLL)

    @pl.when(i == 0)
    def _():
        first = _index_copy(pos_hbm, idx_s, sem_i, 0, 0)
        first.start()
        first.wait()
        gathers(0)

        @pl.when(nt > 1)
        def _():
            _index_copy(pos_hbm, idx_s, sem_i, 1, 1).start()

    @pl.when(i + 1 < nt)
    def _():
        _index_copy(pos_hbm, idx_s, sem_i, i + 1, 1 - slot).wait()
        gathers(1 - slot)

    @pl.when(i + 2 < nt)
    def _():
        _index_copy(pos_hbm, idx_s, sem_i, i + 2, slot).start()

    pltpu.make_async_copy(eo_hbm.at[pl.ds(0, tm)], a_buf.at[slot], sem_a.at[slot]).wait()
    pltpu.make_async_copy(eo_hbm.at[pl.ds(0, tm)], b_buf.at[slot], sem_b.at[slot]).wait()
    info = rinfo_ref[...]
    x2 = x_ref[...] + (info[:, 2:3] * a_buf[slot] + info[:, 3:4] * b_buf[slot])
    hn = _rms(x2, g_ref[...])
    if last_layer:
        @pl.when(i < nt - 1)
        def _():
            o1_ref[...] = hn

        @pl.when(i == nt - 1)
        def _():
            o2_ref[...] = hn
    else:
        o1_ref[...] = x2
        o2_ref[...] = hn


def _combine(pos_tiles, eo, x, rinfo, g, last_layer):
    n, d = x.shape
    tm = TOKEN_TILE
    nt = n // tm
    row = pl.BlockSpec((tm, d), lambda i: (i, 0))
    if last_layer:
        out_specs = [pl.BlockSpec((tm, d), lambda i: (jnp.minimum(i, nt - 2), 0)),
                     pl.BlockSpec((tm, d), lambda i: (0, 0))]
        out_shape = [jax.ShapeDtypeStruct((n - tm, d), F32), jax.ShapeDtypeStruct((tm, d), F32)]
    else:
        out_specs = [row, row]
        out_shape = [jax.ShapeDtypeStruct((n, d), F32), jax.ShapeDtypeStruct((n, d), F32)]
    return pl.pallas_call(
        functools.partial(_combine_kernel, last_layer=last_layer),
        grid=(nt,),
        in_specs=[
            pl.BlockSpec(memory_space=pl.ANY),
            pl.BlockSpec(memory_space=pl.ANY),
            row,
            pl.BlockSpec((tm, LANES), lambda i: (i, 0)),
            pl.BlockSpec((1, d), lambda i: (0, 0)),
        ],
        out_specs=out_specs,
        out_shape=out_shape,
        scratch_shapes=[
            pltpu.SMEM((2, 1, 2 * tm), I32),
            pltpu.VMEM((2, tm, d), F32),
            pltpu.VMEM((2, tm, d), F32),
            pltpu.SemaphoreType.DMA((2,)),
            pltpu.SemaphoreType.DMA((2,)),
            pltpu.SemaphoreType.DMA((2,)),
        ],
        compiler_params=_params("arbitrary"),
        name="moe_combine",
    )(pos_tiles, eo, x, rinfo, g)


def _moe(x1, hn, rinfo, rinfo_t, counts, wg, wu, wd, layer, next_gain, last_layer):
    n = x1.shape[0]
    ne = wg.shape[1]
    tr = EXPERT_ROW_TILE
    tm = TOKEN_TILE
    nv = (2 * n) // tr + ne - 1
    eid = rinfo_t[0:2].astype(I32)
    rank = rinfo_t[4:6].astype(I32)
    cnt = counts[0, :ne].astype(I32)
    seg_end = jnp.cumsum(cnt)
    seg_start = seg_end - cnt
    experts = jnp.arange(ne, dtype=I32)
    pos = rank + jnp.sum(jnp.where(eid[:, None, :] == experts[None, :, None],
                                   seg_start[None, :, None], 0), axis=1)
    pos_tiles = pos.reshape(2, n // tm, tm).transpose(1, 0, 2).reshape(n // tm, 1, 2 * tm)

    first_tile = seg_start // tr
    tiles_e = jnp.where(cnt > 0, (seg_end - 1) // tr - first_tile + 1, 0)
    v_end = jnp.cumsum(tiles_e)
    v_start = v_end - tiles_e
    total = v_end[-1]
    vis = jnp.arange(nv, dtype=I32)
    vc = jnp.minimum(vis, jnp.maximum(total - 1, 0))
    ve = jnp.sum((vc[:, None] >= v_end[None, :]).astype(I32), axis=1)
    pick = lambda tab: jnp.sum(jnp.where(ve[:, None] == experts, tab, 0), axis=1)
    vt = pick(first_tile) + (vc - pick(v_start))
    von = (vis < total).astype(I32)
    present = cnt > 0
    later = present[None, :] & (experts[None, :] > experts[:, None])
    next_of = jnp.min(jnp.where(later, experts[None, :], ne), axis=1)
    vnext = pick(jnp.where(next_of < ne, next_of, -1))
    vslot = pick(jnp.cumsum(present.astype(I32)) - 1) & 1

    xs = _dispatch(pos_tiles, hn)
    eo = _moe_experts(vt, ve, von, vnext, vslot, seg_start, seg_end, xs, wg, wu, wd, layer)
    return _combine(pos_tiles, eo, x1, rinfo, next_gain, last_layer)


def _cis(log_mag, ang):
    mag = jnp.exp(log_mag)
    return mag * jnp.cos(ang), mag * jnp.sin(ang)


def _split3(a):
    p1 = a.astype(BF16)
    r1 = a - p1.astype(F32)
    p2 = r1.astype(BF16)
    return p1, p2, (r1 - p2.astype(F32)).astype(BF16)


def _select_dot(a, sel, sel_first=False):
    sel = sel.astype(BF16)
    parts = [(_dot(sel, p) if sel_first else _dot(p, sel)) for p in _split3(a)]
    return parts[0] + (parts[1] + parts[2])


def _dot3(a, b):
    a_hi, a_lo = _split_bf16(a)
    b_hi, b_lo = _split_bf16(b)
    return _dot(a_hi, b_hi) + (_dot(a_lo, b_hi) + _dot(a_hi, b_lo))


def _s5_prep_kernel(lsr_ref, lrr_ref, lir_ref, ctre_ref, ctim_ref,
                    btre_ref, btim_ref, d_ref,
                    t_ref, wre_ref, wim_ref, cyre_ref, cyim_ref, apr_ref, api_ref,
                    bbr_ref, bbi_ref, lbr_ref, lbi_ref, *, gw, gpb, sub):
    blk = gpb * gw
    wide = sub * blk
    nsl = lrr_ref.shape[2]
    ns = nsl // gpb
    sh_gw, sh_ns = gw.bit_length() - 1, ns.bit_length() - 1
    dt_r = jnp.exp(lsr_ref[0])
    ldt_r_re, ldt_r_im = lrr_ref[0] * dt_r, lir_ref[0] * dt_r
    diag = (lax.broadcasted_iota(I32, (nsl, nsl), 0) == lax.broadcasted_iota(I32, (nsl, nsl), 1))
    to_col = lambda r: jnp.sum(jnp.where(diag, r, 0.0), axis=1, keepdims=True)
    ldt_c_re, ldt_c_im = to_col(ldt_r_re), to_col(ldt_r_im)

    lane_w = lax.broadcasted_iota(I32, (1, wide), 1)
    spread = jnp.where((lax.broadcasted_iota(I32, (gw, wide), 1) & (gw - 1))
                       == lax.broadcasted_iota(I32, (gw, wide), 0), 1.0, 0.0)
    cre = _select_dot(ctre_ref[0], spread)
    cim = _select_dot(ctim_ref[0], spread)
    same = (jnp.right_shift(lax.broadcasted_iota(I32, (nsl, 1), 0), sh_ns)
            == (jnp.right_shift(lane_w, sh_gw) & (gpb - 1)))
    assert 2 * blk == LANES
    tau = lax.broadcasted_iota(I32, (1, LANES), 1).astype(F32)
    pw_re, pw_im = _cis(tau * ldt_c_re, tau * ldt_c_im)
    low = lax.broadcasted_iota(I32, (nsl, LANES), 1) < blk

    def spread_pow(p, first):
        col = lambda t: jnp.broadcast_to(p[:, t:t + 1], (nsl, LANES))
        return jnp.concatenate([jnp.where(low, col(first + 2 * m), col(first + 2 * m + 1))
                                for m in range(sub // 2)], axis=1)

    def c_lam_pow(first):
        pr, pi = spread_pow(pw_re, first), spread_pow(pw_im, first)
        return (jnp.where(same, pr * cre - pi * cim, 0.0),
                jnp.where(same, -(pr * cim + pi * cre), 0.0))

    clr0, cli0 = c_lam_pow(0)
    clr1, cli1 = c_lam_pow(1)
    cyre_ref[0] = clr1.astype(BF16)
    cyim_ref[0] = cli1.astype(BF16)

    lbr, lbi = _cis(ldt_r_re, ldt_r_im)
    lbr_ref[0] = lbr
    lbi_ref[0] = lbi
    lr, li = lrr_ref[0], lir_ref[0]
    nr, ni = lbr - 1.0, lbi
    den = lr * lr + li * li
    fr = (nr * lr + ni * li) / den
    fi = (ni * lr - nr * li) / den
    bre, bim = btre_ref[0], btim_ref[0]
    bbr = fr * bre - fi * bim
    bbi = fr * bim + fi * bre
    bbr_ref[0] = bbr
    bbi_ref[0] = bbi

    lane_t = lax.broadcasted_iota(I32, (gw, wide), 1)
    chan = lax.broadcasted_iota(I32, (gw, wide), 0)
    for g in range(gpb):
        ps = slice(g * ns, (g + 1) * ns)
        r = _dot3(bbr[:, ps], clr0[ps, :]) + _dot3(bbi[:, ps], cli0[ps, :])
        r = r + jnp.where(lane_t == g * gw + chan, d_ref[0][:, g:g + 1], 0.0)
        for j in range(sub):
            tb = r if j == 0 else jnp.where(lane_t >= blk * j, pltpu.roll(r, blk * j, 1), 0.0)
            r0 = j * blk + g * gw
            t_ref[0, r0:r0 + gw, :] = tb.astype(BF16)

    rows = lax.broadcasted_iota(I32, (wide, 1), 0)
    spread_t = jnp.where((lax.broadcasted_iota(I32, (wide, gw), 0) & (gw - 1))
                         == lax.broadcasted_iota(I32, (wide, gw), 1), 1.0, 0.0)
    bbr_t = _select_dot(bbr, spread_t, sel_first=True)
    bbi_t = _select_dot(bbi, spread_t, sel_first=True)
    same_w = ((jnp.right_shift(rows, sh_gw) & (gpb - 1))
              == jnp.right_shift(lax.broadcasted_iota(I32, (1, nsl), 1), sh_ns))
    rj = ((sub - 1) - lax.broadcasted_iota(I32, (sub, 1), 0)).astype(F32)
    q_re, q_im = _cis(rj * ldt_r_re, rj * ldt_r_im)
    per_step = lambda q: jnp.concatenate(
        [jnp.broadcast_to(q[j:j + 1, :], (blk, nsl)) for j in range(sub)], axis=0)
    pr, pi = per_step(q_re), per_step(q_im)
    wre_ref[0] = jnp.where(same_w, pr * bbr_t - pi * bbi_t, 0.0).astype(BF16)
    wim_ref[0] = jnp.where(same_w, pr * bbi_t + pi * bbr_t, 0.0).astype(BF16)

    nlev = apr_ref.shape[1]
    pw = (sub * jnp.left_shift(1, lax.broadcasted_iota(I32, (nlev, 1), 0))).astype(F32)
    apr, api = _cis(pw * ldt_r_re, pw * ldt_r_im)
    apr_ref[0] = apr
    api_ref[0] = api


def _s5_prep(log_step, lam_re, lam_im, b_re, b_im, c_re, c_im, d_skip):
    ng, ns = lam_re.shape
    gw = b_re.shape[2]
    sub, gpb = S5_SUB, S5_GROUPS_PER_BLOCK
    nblk = ng // gpb
    nsl = gpb * ns
    wide = sub * gpb * gw
    ls = jnp.repeat(log_step, ns)
    ct = lambda c: jnp.transpose(c, (0, 2, 1)).reshape(nblk, nsl, gw)
    bt = lambda b: jnp.transpose(b.reshape(nblk, gpb, ns, gw), (0, 3, 1, 2)).reshape(nblk, gw, nsl)
    ins = [ls.reshape(nblk, 1, nsl), lam_re.reshape(nblk, 1, nsl), lam_im.reshape(nblk, 1, nsl),
           ct(c_re), ct(c_im), bt(b_re), bt(b_im),
           jnp.transpose(d_skip.reshape(nblk, gpb, gw), (0, 2, 1))]
    spec = lambda a: pl.BlockSpec((1,) + a.shape[1:], lambda g: (g, 0, 0))
    outs = [
        jax.ShapeDtypeStruct((nblk, wide, wide), BF16),
        jax.ShapeDtypeStruct((nblk, wide, nsl), BF16),
        jax.ShapeDtypeStruct((nblk, wide, nsl), BF16),
        jax.ShapeDtypeStruct((nblk, nsl, wide), BF16),
        jax.ShapeDtypeStruct((nblk, nsl, wide), BF16),
        jax.ShapeDtypeStruct((nblk, SUBLANES, nsl), F32),
        jax.ShapeDtypeStruct((nblk, SUBLANES, nsl), F32),
        jax.ShapeDtypeStruct((nblk, gw, nsl), F32),
        jax.ShapeDtypeStruct((nblk, gw, nsl), F32),
        jax.ShapeDtypeStruct((nblk, 1, nsl), F32),
        jax.ShapeDtypeStruct((nblk, 1, nsl), F32),
    ]
    return pl.pallas_call(
        functools.partial(_s5_prep_kernel, gw=gw, gpb=gpb, sub=sub),
        grid=(nblk,),
        in_specs=[spec(a) for a in ins],
        out_specs=[spec(o) for o in outs],
        out_shape=outs,
        compiler_params=_params("parallel"),
        name="s5_prep",
    )(*ins)


def _s5_prompt_kernel(x_ref, t_ref, wre_ref, wim_ref, cyre_ref, cyim_ref, apr_ref, api_ref,
                      y_ref, hre_out, him_out, u_s, y4_s, *, nb, seq, sub):
    nk = seq // sub
    rows = nb * nk
    hl = LANES // 2
    nlev = nk.bit_length() - 1
    nsl = wre_ref.shape[2]
    low = lax.broadcasted_iota(I32, (nk, LANES), 1) < hl
    kidx = lax.broadcasted_iota(I32, (rows, 1), 0) & (nk - 1)

    def step_rows(b, j):
        return (pl.ds(b * seq + j, nk, stride=sub), slice(None))

    for b in range(nb):
        rs = slice(b * nk, (b + 1) * nk)
        for m in range(sub // 2):
            ls = slice(m * LANES, (m + 1) * LANES)
            s0 = x_ref[step_rows(b, 2 * m)]
            s1 = x_ref[step_rows(b, 2 * m + 1)]
            u_s[0, rs, ls] = jnp.where(low, s0, pltpu.roll(s1, hl, 1)).astype(BF16)
            u_s[1, rs, ls] = jnp.where(low, pltpu.roll(s0, hl, 1), s1).astype(BF16)

    def shifted(a, s):
        return jnp.where(kidx >= s, pltpu.roll(a, s, 0), 0.0)

    halves = range(2)
    us = [u_s[h] for h in halves]
    hre = [_dot(us[h], wre_ref[h]) for h in halves]
    him = [_dot(us[h], wim_ref[h]) for h in halves]
    cw = 2 * LANES
    y_conv = [jnp.concatenate(
        [_dot(us[h][:, 0:c0 + cw], t_ref[h, 0:c0 + cw, c0:c0 + cw]) for c0 in range(0, us[h].shape[1], cw)],
        axis=1) for h in halves]
    for i in range(nlev):
        for h in halves:
            ar = apr_ref[h, i:i + 1, :]
            ai = api_ref[h, i:i + 1, :]
            pre, pim = shifted(hre[h], 1 << i), shifted(him[h], 1 << i)
            hre[h], him[h] = hre[h] + (ar * pre - ai * pim), him[h] + (ar * pim + ai * pre)
    for h in halves:
        ls = slice(h * nsl, (h + 1) * nsl)
        for b in range(nb):
            last = (b + 1) * nk - 1
            hre_out[b:b + 1, ls] = hre[h][last:last + 1, :]
            him_out[b:b + 1, ls] = him[h][last:last + 1, :]
        hpre = shifted(hre[h], 1).astype(BF16)
        hpim = shifted(him[h], 1).astype(BF16)
        y4_s[h] = y_conv[h] + _dot(hpre, cyre_ref[h]) + _dot(hpim, cyim_ref[h])

    for b in range(nb):
        rs = slice(b * nk, (b + 1) * nk)
        for m in range(sub // 2):
            ls = slice(m * LANES, (m + 1) * LANES)
            ca = y4_s[0, rs, ls]
            cb = y4_s[1, rs, ls]
            y_ref[step_rows(b, 2 * m)] = jnp.where(low, ca, pltpu.roll(cb, hl, 1))
            y_ref[step_rows(b, 2 * m + 1)] = jnp.where(low, pltpu.roll(ca, hl, 1), cb)


def _s5_prompt(h, prep, batch, seq):
    d = h.shape[1]
    tmat, wre, wim, cyre, cyim, apr, api = prep[:7]
    nblk, wide, nsl = wre.shape
    sub, nb = S5_SUB, S5_BATCH_PER_STEP
    nk = seq // sub
    assert nk & (nk - 1) == 0 and nk.bit_length() - 1 <= apr.shape[1]
    ntile = d // LANES
    assert nblk == 2 * ntile
    rows = nb * nk
    wspec = lambda a: pl.BlockSpec((2,) + a.shape[1:], lambda t, b: (t, 0, 0))
    xspec = pl.BlockSpec((nb * seq, LANES), lambda t, b: (b, t))
    sspec = pl.BlockSpec((None, nb, 2 * nsl), lambda t, b: (b, 0, t))
    return pl.pallas_call(
        functools.partial(_s5_prompt_kernel, nb=nb, seq=seq, sub=sub),
        grid=(ntile, batch // nb),
        in_specs=[xspec, wspec(tmat), wspec(wre), wspec(wim), wspec(cyre), wspec(cyim),
                  wspec(apr), wspec(api)],
        out_specs=[xspec, sspec, sspec],
        out_shape=[
            jax.ShapeDtypeStruct((batch * seq, d), F32),
            jax.ShapeDtypeStruct((batch // nb, nb, ntile * 2 * nsl), F32),
            jax.ShapeDtypeStruct((batch // nb, nb, ntile * 2 * nsl), F32),
        ],
        scratch_shapes=[
            pltpu.VMEM((2, rows, wide), BF16),
            pltpu.VMEM((2, rows, wide), F32),
        ],
        compiler_params=_params("parallel", "parallel"),
        name="s5_prompt",
    )(h, tmat, wre, wim, cyre, cyim, apr, api)


def _s5_step_kernel(u_ref, hre_ref, him_ref, bbr_ref, bbi_ref, lbr_ref, lbi_ref, cre_ref, cim_ref,
                    d_ref, nre_ref, nim_ref, y_ref):
    for g in range(u_ref.shape[0]):
        u = u_ref[g]
        hre, him, lbr, lbi = hre_ref[g], him_ref[g], lbr_ref[g], lbi_ref[g]
        nre = lbr * hre - lbi * him + _dot3(u, bbr_ref[g])
        nim = lbr * him + lbi * hre + _dot3(u, bbi_ref[g])
        nre_ref[g] = nre
        nim_ref[g] = nim
        y_ref[g] = _dot3(nre, cre_ref[g]) - _dot3(nim, cim_ref[g]) + d_ref[g] * u


def _s5_step(u_s, h_re, h_im, bbr, bbi, lbr, lbi, cre_t, cim_t, d_g):
    ng = u_s.shape[0]
    gb = SUBLANES
    spec = lambda a: pl.BlockSpec((gb,) + a.shape[1:], lambda g: (g, 0, 0))
    ins = [u_s, h_re, h_im, bbr, bbi, lbr, lbi, cre_t, cim_t, d_g]
    outs = [jax.ShapeDtypeStruct(h_re.shape, F32), jax.ShapeDtypeStruct(h_re.shape, F32),
            jax.ShapeDtypeStruct(u_s.shape, F32)]
    return pl.pallas_call(
        _s5_step_kernel,
        grid=(ng // gb,),
        in_specs=[spec(a) for a in ins],
        out_specs=[spec(o) for o in outs],
        out_shape=outs,
        compiler_params=_params("parallel"),
        name="s5_step",
    )(*ins)


def _pad_to(a, axis, size):
    pad = [(0, 0)] * a.ndim
    pad[axis] = (0, size - a.shape[axis])
    return jnp.pad(a, pad)


def kernel(x_prompt, x_sample, state_pool, state_mlstm_c, state_mlstm_n, state_mlstm_m, state_s5_re, state_s5_im, norm_mix, norm_ffn, norm_final, w_in_ab, b_gates, pool_w, pool_scale, w_out_ab, s5_lam_re, s5_lam_im, s5_log_step, s5_b_re, s5_b_im, s5_c_re, s5_c_im, s5_d, w_glu, moe_w_group, moe_b_group, moe_w_expert, moe_b_expert, moe_w_gate, moe_w_up, moe_w_down):
    bp, tp, d = x_prompt.shape
    bs = x_sample.shape[0]
    n_p = bp * tp
    tm = TOKEN_TILE
    assert n_p % tm == 0 and bs <= tm
    nh, dh = state_mlstm_c.shape[2], state_mlstm_c.shape[3]
    pool_width = state_pool.shape[3]
    ml_width = nh * dh
    n_main = pool_width + 4 * ml_width
    n_gates = 2 * nh
    ngrp_s5, n_state = s5_lam_re.shape[1], s5_lam_re.shape[2]
    gw_s5 = d // ngrp_s5

    n = n_p + tm

    def tail_tile(sample_rows, dtype):
        return _pad_to(sample_rows.astype(dtype), 0, tm)

    x_main = x_prompt.reshape(n_p, d)
    x_tail = tail_tile(x_sample.reshape(bs, d), F32)
    tril = jnp.tril(jnp.ones((tm, tm), BF16), -1)

    def router_weights(l):
        wr = _pad_to(jnp.concatenate([moe_w_group[l], moe_w_expert[l]], axis=1), 1, LANES)
        br = jnp.concatenate([moe_b_group[l], moe_b_expert[l]])[None, :]
        hi = wr.astype(BF16)
        lo = (wr - hi.astype(F32)).astype(BF16)
        return jnp.concatenate([hi, lo], axis=1), _pad_to(br, 1, LANES)

    w_in = w_in_ab[0]
    z, gates, gates_t = _inproj(
        x_main, x_tail, norm_mix[0][None, :], w_in,
        _pad_to(b_gates[0][None, :], 1, LANES), n_gates)

    pw = pool_w[0].astype(BF16)
    ps = pool_scale[0][None, :]
    pool_y_p, pool_p = _pool_prompt(z, pw, ps, bp, tp)
    pool_y_s, pool_s = _pool_step(z, state_pool[0], pw, ps, n_p)

    ml_y_p, c_p, n_p_st, m_p = _mlstm_prompt(z, gates_t, bp, tp, nh, dh)
    ml_y_p = ml_y_p.reshape(n_p, ml_width)
    g_s = gates[n_p:n_p + bs, :n_gates]
    ml_y_s, c_s, n_s_st, m_s = _mlstm_step(
        z[n_p:n_p + bs].reshape(bs, 1, n_main),
        g_s[:, :nh].reshape(bs, nh, 1, 1), g_s[:, nh:].reshape(bs, nh, 1, 1),
        state_mlstm_c[0], state_mlstm_n[0].reshape(bs, nh, 1, dh),
        state_mlstm_m[0].reshape(bs, nh, 1, 1), nh, dh)

    rows = [x_main, x_tail, pool_y_p, tail_tile(pool_y_s, BF16),
            ml_y_p, tail_tile(ml_y_s.reshape(bs, ml_width), BF16)]
    specs = (_stacked_specs(x_main, x_tail) + _stacked_specs(pool_y_p, rows[3])
             + _stacked_specs(ml_y_p, rows[5]))
    wr, br = router_weights(0)
    x1, hn, rinfo, rinfo_t, counts = _mix_route(
        _outproj_route_kernel, "outproj_route", n, specs, rows, w_out_ab[0],
        norm_ffn[0][None, :], wr, br, tril)
    x2, h1 = _moe(x1, hn, rinfo, rinfo_t, counts, moe_w_gate, moe_w_up, moe_w_down, 0,
                  norm_mix[1][None, :], last_layer=False)

    prep = _s5_prep(s5_log_step[0], s5_lam_re[0], s5_lam_im[0], s5_b_re[0], s5_b_im[0],
                    s5_c_re[0], s5_c_im[0], s5_d[0])
    y_p, hre_p, him_p = _s5_prompt(h1, prep, bp, tp)
    s5_re_p = hre_p.reshape(bp, ngrp_s5, n_state)
    s5_im_p = him_p.reshape(bp, ngrp_s5, n_state)

    gpb = S5_GROUPS_PER_BLOCK
    per_group = lambda a: jnp.transpose(
        a.reshape(ngrp_s5 // gpb, gw_s5, gpb, n_state), (0, 2, 1, 3)).reshape(ngrp_s5, gw_s5, n_state)
    bbr, bbi = per_group(prep[7]), per_group(prep[8])
    lbr, lbi = prep[9].reshape(ngrp_s5, 1, n_state), prep[10].reshape(ngrp_s5, 1, n_state)
    d_g = s5_d[0].reshape(ngrp_s5, 1, gw_s5)
    u_s = h1[n_p:n_p + bs].reshape(bs, ngrp_s5, gw_s5).transpose(1, 0, 2)
    h_re = jnp.transpose(state_s5_re[0], (1, 0, 2))
    h_im = jnp.transpose(state_s5_im[0], (1, 0, 2))
    nre_s, nim_s, y_s = _s5_step(
        u_s, h_re, h_im, bbr, bbi, lbr, lbi,
        jnp.transpose(s5_c_re[0], (0, 2, 1)), jnp.transpose(s5_c_im[0], (0, 2, 1)), d_g)
    s5_re_s = jnp.transpose(nre_s, (1, 0, 2))
    s5_im_s = jnp.transpose(nim_s, (1, 0, 2))
    y_tail = tail_tile(jnp.transpose(y_s, (1, 0, 2)).reshape(bs, d), F32)

    wr, br = router_weights(1)
    specs = [pl.BlockSpec((tm, d), lambda i: (i, 0))] + _stacked_specs(y_p, y_tail)
    x3, hn, rinfo, rinfo_t, counts = _mix_route(
        _glu_route_kernel, "glu_route", n, specs, [x2, y_p, y_tail], w_glu[0],
        norm_ffn[1][None, :], wr, br, tril)
    y_main, y_last = _moe(x3, hn, rinfo, rinfo_t, counts, moe_w_gate, moe_w_up, moe_w_down, 1,
                          norm_final[None, :], last_layer=True)

    return (y_main.reshape(bp, tp, d), y_last[:bs].reshape(bs, 1, d),
            pool_p[None], c_p[None], n_p_st.reshape(1, bp, nh, dh), m_p[:, :, 0, 0][None],
            s5_re_p[None], s5_im_p[None],
            pool_s[None], c_s[None], n_s_st.reshape(1, bs, nh, dh), m_s.reshape(1, bs, nh),
            s5_re_s[None], s5_im_s[None])
```

```python
import functools

import jax
import jax.numpy as jnp
from jax import lax
from jax.experimental import pallas as pl
from jax.experimental.pallas import tpu as pltpu

F32 = jnp.float32
BF16 = jnp.bfloat16
I32 = jnp.int32

PAST_LEN = 16384
POOL_WINDOWS = (2, 4, 8, 16)
POOL_BUF = max(POOL_WINDOWS) - 1
MLSTM_CHUNK = 128
S5_SUB = 16
MOE_GROUPS = 4
MOE_EXPERTS_PER_GROUP = 8
RMS_EPS = 1e-6

LANES = 128
SUBLANES = 8
VMEM_LIMIT_BYTES = 56 * 1024 * 1024

TOKEN_TILE = 512
EXPERT_ROW_TILE = 512
MLSTM_SEQ_PER_STEP = 8
MLSTM_STEP_BATCH = 16
POOL_TIME_TILE = 2048
DMA_ISSUE_UNROLL = 8
S5_GROUPS_PER_BLOCK = 4
S5_BATCH_PER_STEP = 4


def _params(*sem):
    return pltpu.CompilerParams(dimension_semantics=sem, vmem_limit_bytes=VMEM_LIMIT_BYTES)


def _rms(x, g):
    return x * lax.rsqrt(jnp.mean(x * x, axis=-1, keepdims=True) + RMS_EPS) * g


def _dot(a, b):
    return jnp.dot(a, b, preferred_element_type=F32)


def _dot_nt(a, b):
    return lax.dot_general(a, b, (((1,), (1,)), ((), ())), preferred_element_type=F32)


def _dot_tn(a, b):
    return lax.dot_general(a, b, (((0,), (0,)), ((), ())), preferred_element_type=F32)


def _stacked(main_ref, tail_ref):
    last = pl.program_id(0) == pl.num_programs(0) - 1
    return jnp.where(last, tail_ref[...], main_ref[...])


def _stacked_specs(main, tail):
    tm, w = tail.shape
    last_main = main.shape[0] // tm - 1
    return [pl.BlockSpec((tm, w), lambda i: (jnp.minimum(i, last_main), 0)),
            pl.BlockSpec((tm, w), lambda i: (0, 0))]


def _weight_spec(w):
    return pl.BlockSpec(w.shape, lambda i: (0,) * w.ndim, pipeline_mode=pl.Buffered(1))


def _inproj_kernel(xm_ref, xt_ref, g_ref, w_ref, bg_ref, z_ref, gates_ref, gatest_ref, wb):
    @pl.when(pl.program_id(0) == 0)
    def _():
        wb[...] = jnp.zeros(wb.shape, BF16)
        wb[:, 0:w_ref.shape[1]] = w_ref[...].astype(BF16)

    h = _rms(_stacked(xm_ref, xt_ref), g_ref[...]).astype(BF16)
    zg = _dot(h, wb[...])
    nz = z_ref.shape[1]
    z_ref[...] = zg[:, 0:nz]
    gates = zg[:, nz:nz + LANES] + bg_ref[...]
    gates_ref[...] = gates
    gatest_ref[...] = gates.T[0:gatest_ref.shape[0], :]


def _inproj(x_main, x_tail, g, w, bg, ng):
    d = x_main.shape[1]
    tm = TOKEN_TILE
    n = x_main.shape[0] + tm
    nz = w.shape[1] - ng
    full = lambda i: (0, 0)
    return pl.pallas_call(
        _inproj_kernel,
        grid=(n // tm,),
        in_specs=_stacked_specs(x_main, x_tail) + [
            pl.BlockSpec((1, d), full),
            _weight_spec(w),
            pl.BlockSpec((1, LANES), full),
        ],
        out_specs=[
            pl.BlockSpec((tm, nz), lambda i: (i, 0)),
            pl.BlockSpec((tm, LANES), lambda i: (i, 0)),
            pl.BlockSpec((ng, tm), lambda i: (0, i)),
        ],
        out_shape=[
            jax.ShapeDtypeStruct((n, nz), F32),
            jax.ShapeDtypeStruct((n, LANES), F32),
            jax.ShapeDtypeStruct((ng, n), F32),
        ],
        scratch_shapes=[pltpu.VMEM((d, nz + LANES), BF16)],
        compiler_params=_params("arbitrary"),
        name="inproj",
    )(x_main, x_tail, g, w, bg)


def _pool_prompt_kernel(u_ref, pw_ref, ps_ref, y_ref, st_ref, ext_ref, *, tt, gw):
    t = pl.program_id(1)
    nt = pl.num_programs(1)
    halo = POOL_BUF + 1
    width = ext_ref.shape[1]

    @pl.when(t == 0)
    def _():
        ext_ref[0:halo, :] = jnp.zeros((halo, width), F32)

    u = u_ref[...]
    ext_ref[halo:halo + tt, :] = u
    pos = t * tt + lax.broadcasted_iota(I32, (tt, 1), 0)
    for g, w in enumerate(POOL_WINDOWS):
        c0 = g * gw
        acc = u[:, c0:c0 + gw]
        for j in range(1, w):
            acc = acc + ext_ref[halo - j:halo - j + tt, c0:c0 + gw]
        cnt = jnp.minimum(w, pos + 1).astype(F32)
        d = acc / cnt - u[:, c0:c0 + gw]
        y = _dot(d.astype(BF16), pw_ref[g]) * ps_ref[:, c0:c0 + gw]
        y_ref[:, c0:c0 + gw] = y.astype(BF16)

    @pl.when(t == nt - 1)
    def _():
        st_ref[...] = ext_ref[tt + 1:tt + halo, :]

    ext_ref[0:halo, :] = ext_ref[tt:tt + halo, :]


def _pool_prompt(z, pw, ps, batch, seq):
    width = ps.shape[1]
    gw = width // len(POOL_WINDOWS)
    tt = POOL_TIME_TILE
    nt = seq // tt
    return pl.pallas_call(
        functools.partial(_pool_prompt_kernel, tt=tt, gw=gw),
        grid=(batch, nt),
        in_specs=[
            pl.BlockSpec((tt, width), lambda b, t: (b * nt + t, 0)),
            pl.BlockSpec(pw.shape, lambda b, t: (0, 0, 0)),
            pl.BlockSpec((1, width), lambda b, t: (0, 0)),
        ],
        out_specs=[
            pl.BlockSpec((tt, width), lambda b, t: (b * nt + t, 0)),
            pl.BlockSpec((None, POOL_BUF, width), lambda b, t: (b, 0, 0)),
        ],
        out_shape=[
            jax.ShapeDtypeStruct((batch * seq, width), BF16),
            jax.ShapeDtypeStruct((batch, POOL_BUF, width), F32),
        ],
        scratch_shapes=[pltpu.VMEM((POOL_BUF + 1 + tt, width), F32)],
        compiler_params=_params("parallel", "arbitrary"),
        name="pool_prompt",
    )(z, pw, ps)


def _pool_step_kernel(u_ref, buf_ref, pw_ref, ps_ref, y_ref, nb_ref, *, gw):
    u = u_ref[...]
    for g, w in enumerate(POOL_WINDOWS):
        c0 = g * gw
        acc = u[:, c0:c0 + gw]
        for j in range(1, w):
            acc = acc + buf_ref[POOL_BUF - j, :, c0:c0 + gw]
        cnt = float(min(w, PAST_LEN + 1))
        d = acc / cnt - u[:, c0:c0 + gw]
        y = _dot(d.astype(BF16), pw_ref[g]) * ps_ref[:, c0:c0 + gw]
        y_ref[:, c0:c0 + gw] = y.astype(BF16)
    nb_ref[0:POOL_BUF - 1] = buf_ref[1:POOL_BUF]
    nb_ref[POOL_BUF - 1] = u


def _pool_step(z, buf_t, pw, ps, row0):
    _, batch, width = buf_t.shape
    gw = width // len(POOL_WINDOWS)
    return pl.pallas_call(
        functools.partial(_pool_step_kernel, gw=gw),
        grid=(1,),
        in_specs=[
            pl.BlockSpec((batch, width), lambda i: (row0 // batch, 0)),
            pl.BlockSpec(buf_t.shape, lambda i: (0, 0, 0)),
            pl.BlockSpec(pw.shape, lambda i: (0, 0, 0)),
            pl.BlockSpec((1, width), lambda i: (0, 0)),
        ],
        out_specs=[
            pl.BlockSpec((batch, width), lambda i: (0, 0)),
            pl.BlockSpec(buf_t.shape, lambda i: (0, 0, 0)),
        ],
        out_shape=[
            jax.ShapeDtypeStruct((batch, width), BF16),
            jax.ShapeDtypeStruct(buf_t.shape, F32),
        ],
        compiler_params=_params("arbitrary"),
        name="pool_step",
    )(z, buf_t, pw, ps)


def _mlstm_prompt_kernel(*refs, nh, dh, nseq):
    seq_in = [refs[5 * s:5 * s + 5] for s in range(nseq)]
    h_ref, c_out, n_out, m_out, c_s, n_s, m_s = refs[5 * nseq:]
    ci = pl.program_id(1)
    nc = pl.num_programs(1)
    ln = seq_in[0][0].shape[0]

    @pl.when(ci == 0)
    def _():
        c_s[...] = jnp.zeros(c_s.shape, F32)
        n_s[...] = jnp.zeros(n_s.shape, F32)
        m_s[...] = jnp.zeros(m_s.shape, F32)

    row = lax.broadcasted_iota(I32, (ln, ln), 0)
    col = lax.broadcasted_iota(I32, (ln, ln), 1)
    causal_t = row <= col
    eye = col == row
    lane = lax.broadcasted_iota(I32, (nh, ln), 1)

    def to_col(r):
        return jnp.sum(jnp.where(eye, r, 0.0), axis=1, keepdims=True)

    scale = dh ** -0.5
    pairs = [(s, h) for s in range(nseq) for h in range(nh)]
    c_old = {p: c_s[p[0], p[1]] for p in pairs}
    n_old = {p: n_s[p[0], p[1]] for p in pairs}
    m_old = {p: m_s[p[0], p[1]][:, 0:1] for p in pairs}
    gates = []
    for s in range(nseq):
        gt = seq_in[s][4][...]
        bc_all = jax.nn.log_sigmoid(gt[nh:2 * nh])
        sh = 1
        while sh < ln:
            bc_all = bc_all + jnp.where(lane >= sh, pltpu.roll(bc_all, sh, 1), 0.0)
            sh *= 2
        gates.append((gt[0:nh], bc_all))
    ch = {}
    for s, h in pairs:
        q_ref, k_ref, v_ref, _, _ = seq_in[s]
        li_all, bc_all = gates[s]
        sl = slice(h * dh, (h + 1) * dh)
        k = k_ref[:, sl] * scale
        v = v_ref[:, sl]
        li_r, bc_r = li_all[h:h + 1], bc_all[h:h + 1]
        b_end = bc_r[:, ln - 1:ln]
        g_r = b_end - bc_r + li_r
        m0 = m_old[s, h]
        m_new = jnp.maximum(b_end + m0, jnp.max(g_r, axis=1, keepdims=True))
        ch[s, h] = dict(
            qb=q_ref[:, sl].astype(BF16), kb=k.astype(BF16), vb=v.astype(BF16), v=v,
            inter=bc_r + m0, m_new=m_new, wg_r=jnp.exp(g_r - m_new),
            decay=jnp.exp(b_end + m0 - m_new),
            dmat_t=jnp.where(causal_t, bc_r - to_col(bc_r - li_r), -jnp.inf))
    for p in pairs:
        d = ch[p]
        d["st"] = _dot_nt(d["kb"], d["qb"])
        d["cq_t"] = _dot_nt(c_old[p].astype(BF16), d["qb"])
        d["nq"] = _dot_nt(jnp.broadcast_to(n_old[p], (SUBLANES, dh)).astype(BF16), d["qb"])[0:1]
        d["c_add"] = _dot((d["v"].T * d["wg_r"]).astype(BF16), d["kb"])
        d["n_add"] = _dot(jnp.broadcast_to(d["wg_r"], (SUBLANES, ln)).astype(BF16), d["kb"])[0:1]
    for p in pairs:
        d = ch[p]
        d["m_row"] = jnp.maximum(d["inter"], jnp.max(d["dmat_t"], axis=0, keepdims=True))
        d["w_inter"] = jnp.exp(d["inter"] - d["m_row"])
        d["sc_t"] = d["st"] * jnp.exp(d["dmat_t"] - d["m_row"])
    for p in pairs:
        d = ch[p]
        d["pv_t"] = _dot_tn(d["vb"], d["sc_t"].astype(BF16))
    new_state = {}
    for s, h in pairs:
        d = ch[s, h]
        sl = slice(h * dh, (h + 1) * dh)
        num_t = d["pv_t"] + d["w_inter"] * d["cq_t"]
        den = jnp.sum(d["sc_t"], axis=0, keepdims=True) + d["w_inter"] * d["nq"]
        hh = (num_t / jnp.maximum(jnp.abs(den), jnp.exp(-d["m_row"]))).T
        h_ref[s, :, sl] = (hh * jax.nn.sigmoid(seq_in[s][3][:, sl])).astype(BF16)
        new_state[s, h] = (d["decay"] * c_old[s, h] + d["c_add"],
                           d["decay"] * n_old[s, h] + d["n_add"],
                           jnp.broadcast_to(d["m_new"], (1, dh)))

    for (s, h), (c_new, n_new, m_new) in new_state.items():
        c_s[s, h] = c_new
        n_s[s, h] = n_new
        m_s[s, h] = m_new

    @pl.when(ci == nc - 1)
    def _():
        c_out[...] = c_s[...]
        n_out[...] = n_s[...]
        m_out[...] = m_s[...]


def _mlstm_prompt(z, gates_t, batch, seq, nh, dh):
    ln = MLSTM_CHUNK
    nc = seq // ln
    nseq = min(MLSTM_SEQ_PER_STEP, batch)
    assert batch % nseq == 0
    width = nh * dh
    ng = gates_t.shape[0]
    rows_of = lambda s: (lambda b, c: (b * nseq + s) * nc + c)
    in_specs, operands = [], []
    for s in range(nseq):
        r = rows_of(s)
        for j in (1, 2, 3, 4):
            in_specs.append(pl.BlockSpec((ln, width), lambda b, c, r=r, j=j: (r(b, c), j)))
        in_specs.append(pl.BlockSpec((ng, ln), lambda b, c, r=r: (0, r(b, c))))
        operands += [z, z, z, z, gates_t]
    h_spec = pl.BlockSpec((None, nseq, ln, width), lambda b, c: (b, 0, c, 0))
    st = lambda a, b_: pl.BlockSpec((nseq, nh, a, b_), lambda b, c: (b, 0, 0, 0))
    return pl.pallas_call(
        functools.partial(_mlstm_prompt_kernel, nh=nh, dh=dh, nseq=nseq),
        grid=(batch // nseq, nc),
        in_specs=in_specs,
        out_specs=[h_spec, st(dh, dh), st(1, dh), st(1, dh)],
        out_shape=[
            jax.ShapeDtypeStruct((batch // nseq, nseq, seq, width), BF16),
            jax.ShapeDtypeStruct((batch, nh, dh, dh), F32),
            jax.ShapeDtypeStruct((batch, nh, 1, dh), F32),
            jax.ShapeDtypeStruct((batch, nh, 1, dh), F32),
        ],
        scratch_shapes=[pltpu.VMEM((nseq, nh, dh, dh), F32), pltpu.VMEM((nseq, nh, 1, dh), F32),
                        pltpu.VMEM((nseq, nh, 1, dh), F32)],
        compiler_params=_params("parallel", "arbitrary"),
        name="mlstm_prompt",
    )(*operands)


def _mlstm_step_kernel(q_ref, k_ref, v_ref, o_ref, li_ref, fp_ref, c_ref, n_ref, m_ref,
                       h_ref, c_out, n_out, m_out, *, nh, dh):
    eye = (lax.broadcasted_iota(I32, (1, dh, dh), 1) == lax.broadcasted_iota(I32, (1, dh, dh), 2))
    scale = dh ** -0.5
    for h in range(nh):
        sl = slice(h * dh, (h + 1) * dh)
        q = q_ref[:, :, sl]
        k = k_ref[:, :, sl] * scale
        v = v_ref[:, :, sl]
        c = c_ref[:, h]
        n = n_ref[:, h]
        m = m_ref[:, h]
        li = li_ref[:, h]
        lf = jax.nn.log_sigmoid(fp_ref[:, h])
        inter = lf + m
        m_row = jnp.maximum(inter, li)
        w_intra = jnp.exp(li - m_row)
        w_inter = jnp.exp(inter - m_row)
        sc = jnp.sum(q * k, axis=-1, keepdims=True) * w_intra
        q8 = jnp.broadcast_to(q, (q.shape[0], SUBLANES, dh)).astype(BF16)
        cq = jnp.einsum("bqk,bvk->bqv", q8, c.astype(BF16),
                        preferred_element_type=F32)[:, 0:1, :]
        num = sc * v + w_inter * cq
        den = sc + w_inter * jnp.sum(n * q, axis=-1, keepdims=True)
        h_l = num / jnp.maximum(jnp.abs(den), jnp.exp(-m_row))
        h_ref[:, :, sl] = h_l * jax.nn.sigmoid(o_ref[:, :, sl])
        v_c = jnp.sum(jnp.where(eye, v, 0.0), axis=-1, keepdims=True)
        wg = jnp.exp(li - m_row)
        decay = jnp.exp(inter - m_row)
        c_out[:, h] = decay * c + (v_c * wg) * k
        n_out[:, h] = decay * n + wg * k
        m_out[:, h] = m_row


def _mlstm_step(z3, li, fp, c, n, m, nh, dh):
    batch = c.shape[0]
    bb = MLSTM_STEP_BATCH
    width = nh * dh
    blk = lambda j: pl.BlockSpec((bb, 1, width), lambda i: (i, 0, j))
    st4 = lambda a, b: pl.BlockSpec((bb, nh, a, b), lambda i: (i, 0, 0, 0))
    return pl.pallas_call(
        functools.partial(_mlstm_step_kernel, nh=nh, dh=dh),
        grid=(batch // bb,),
        in_specs=[blk(1), blk(2), blk(3), blk(4), st4(1, 1), st4(1, 1),
                  st4(dh, dh), st4(1, dh), st4(1, 1)],
        out_specs=[pl.BlockSpec((bb, 1, width), lambda i: (i, 0, 0)),
                   st4(dh, dh), st4(1, dh), st4(1, 1)],
        out_shape=[
            jax.ShapeDtypeStruct((batch, 1, width), F32),
            jax.ShapeDtypeStruct((batch, nh, dh, dh), F32),
            jax.ShapeDtypeStruct((batch, nh, 1, dh), F32),
            jax.ShapeDtypeStruct((batch, nh, 1, 1), F32),
        ],
        compiler_params=_params("parallel"),
        name="mlstm_step",
    )(z3, z3, z3, z3, li, fp, c, n, m)


def _split_bf16(a):
    hi = a.astype(BF16)
    return hi, (a - hi.astype(F32)).astype(BF16)


def _route(hn, wr_ref, br_ref, tril_ref, carry_ref, rinfo_ref, rinfo_t_ref):
    ngrp, epg = MOE_GROUPS, MOE_EXPERTS_PER_GROUP
    h_hi, h_lo = _split_bf16(hn)
    both = _dot(h_hi, wr_ref[...])
    logits = both[:, 0:LANES] + (both[:, LANES:2 * LANES] + _dot(h_lo, wr_ref[:, 0:LANES])) + br_ref[...]
    tm = logits.shape[0]
    lane = lax.broadcasted_iota(I32, (tm, LANES), 1)
    neg = -jnp.inf

    def first_max(x):
        mx = jnp.max(x, axis=1, keepdims=True)
        idx = jnp.min(jnp.where(x == mx, lane, LANES), axis=1, keepdims=True)
        return mx, idx

    is_grp = lane < ngrp
    gmax, gsel = first_max(jnp.where(is_grp, logits, neg))
    g_w = 1.0 / jnp.sum(jnp.where(is_grp, jnp.exp(logits - gmax), 0.0), axis=1, keepdims=True)
    lo = ngrp + gsel * epg
    el = jnp.where((lane >= lo) & (lane < lo + epg), logits, neg)
    v1, i1 = first_max(el)
    v2, i2 = first_max(jnp.where(lane == i1, neg, el))
    e2 = jnp.exp(v2 - v1)
    w1 = g_w / (1.0 + e2)
    w2 = g_w * e2 / (1.0 + e2)
    eid1 = i1 - ngrp
    eid2 = i2 - ngrp

    hit1 = lane == eid1
    hit2 = lane == eid2
    onehot = jnp.where(hit1 | hit2, 1.0, 0.0)
    carry = carry_ref[...]
    prefix = _dot(tril_ref[...], onehot.astype(BF16)) + carry
    rank1 = jnp.sum(jnp.where(hit1, prefix, 0.0), axis=1, keepdims=True)
    rank2 = jnp.sum(jnp.where(hit2, prefix, 0.0), axis=1, keepdims=True)
    carry_ref[...] = carry + jnp.sum(onehot, axis=0, keepdims=True)

    cols = (eid1.astype(F32), eid2.astype(F32), w1, w2, rank1, rank2)
    info = jnp.zeros((tm, LANES), F32)
    for j, cval in enumerate(cols):
        info = jnp.where(lane == j, cval, info)
    rinfo_ref[...] = info
    pick = jnp.where(lax.broadcasted_iota(I32, (SUBLANES, LANES), 0)
                     == lax.broadcasted_iota(I32, (SUBLANES, LANES), 1), 1.0, 0.0).astype(BF16)
    parts = [_dot_nt(pick, p) for p in _split3(info)]
    rinfo_t_ref[...] = parts[0] + (parts[1] + parts[2])


def _outproj_route_kernel(xm_ref, xt_ref, pm_ref, pt_ref, mm_ref, mt_ref,
                          wo_ref, g_ref, wr_ref, br_ref, tril_ref,
                          x1_ref, hn_ref, rinfo_ref, rinfo_t_ref, cnt_ref, carry_ref, wb):
    @pl.when(pl.program_id(0) == 0)
    def _():
        carry_ref[...] = jnp.zeros(carry_ref.shape, F32)
        wb[...] = wo_ref[...].astype(BF16)

    half = pm_ref.shape[1]
    mix = (_dot(_stacked(pm_ref, pt_ref), wb[0:half, :])
           + _dot(_stacked(mm_ref, mt_ref), wb[half:2 * half, :]))
    x1 = _stacked(xm_ref, xt_ref) + mix
    x1_ref[...] = x1
    hn = _rms(x1, g_ref[...])
    hn_ref[...] = hn
    _route(hn, wr_ref, br_ref, tril_ref, carry_ref, rinfo_ref, rinfo_t_ref)
    cnt_ref[...] = carry_ref[...]


def _glu_route_kernel(x_ref, ym_ref, yt_ref, wglu_ref, g_ref, wr_ref, br_ref, tril_ref,
                      x1_ref, hn_ref, rinfo_ref, rinfo_t_ref, cnt_ref, carry_ref, wb):
    @pl.when(pl.program_id(0) == 0)
    def _():
        carry_ref[...] = jnp.zeros(carry_ref.shape, F32)
        wb[...] = wglu_ref[...].astype(BF16)

    d = x_ref.shape[1]
    ag = _dot(jax.nn.gelu(_stacked(ym_ref, yt_ref)).astype(BF16), wb[...])
    x1 = x_ref[...] + ag[:, 0:d] * jax.nn.sigmoid(ag[:, d:2 * d])
    x1_ref[...] = x1
    hn = _rms(x1, g_ref[...])
    hn_ref[...] = hn
    _route(hn, wr_ref, br_ref, tril_ref, carry_ref, rinfo_ref, rinfo_t_ref)
    cnt_ref[...] = carry_ref[...]


def _mix_route(kernel, name, n, row_specs, rows, w, g, wr, br, tril):
    d = g.shape[1]
    tm = TOKEN_TILE
    full = lambda i: (0, 0)
    return pl.pallas_call(
        kernel,
        grid=(n // tm,),
        in_specs=row_specs + [
            _weight_spec(w),
            pl.BlockSpec((1, d), full),
            pl.BlockSpec((d, 2 * LANES), full),
            pl.BlockSpec((1, LANES), full),
            pl.BlockSpec((tm, tm), full),
        ],
        out_specs=[
            pl.BlockSpec((tm, d), lambda i: (i, 0)),
            pl.BlockSpec((tm, d), lambda i: (i, 0)),
            pl.BlockSpec((tm, LANES), lambda i: (i, 0)),
            pl.BlockSpec((SUBLANES, tm), lambda i: (0, i)),
            pl.BlockSpec((1, LANES), full),
        ],
        out_shape=[
            jax.ShapeDtypeStruct((n, d), F32),
            jax.ShapeDtypeStruct((n, d), F32),
            jax.ShapeDtypeStruct((n, LANES), F32),
            jax.ShapeDtypeStruct((SUBLANES, n), F32),
            jax.ShapeDtypeStruct((1, LANES), F32),
        ],
        scratch_shapes=[pltpu.VMEM((1, LANES), F32), pltpu.VMEM(w.shape, BF16)],
        compiler_params=_params("arbitrary"),
        name=name,
    )(*rows, w, g, wr, br, tril)


def _index_copy(pos_hbm, idx_s, sem_i, tile, slot):
    return pltpu.make_async_copy(pos_hbm.at[tile], idx_s.at[slot], sem_i.at[slot])


def _dispatch_kernel(pos_hbm, hn_hbm, xs_hbm, idx_s, tiles, sem_i, sem_l, sem_d):
    i = pl.program_id(0)
    nt = pl.num_programs(0)
    nslot, tm, _ = tiles.shape
    islot = i % 2

    def load(t):
        s = t % nslot
        return pltpu.make_async_copy(hn_hbm.at[pl.ds(t * tm, tm)], tiles.at[s], sem_l.at[s])

    def wait_rows(t):
        s = t % nslot
        whole = pltpu.make_async_copy(tiles.at[s], xs_hbm.at[pl.ds(0, tm)], sem_d.at[s])
        whole.wait()
        whole.wait()

    @pl.when(i == 0)
    def _():
        _index_copy(pos_hbm, idx_s, sem_i, 0, 0).start()
        load(0).start()

        @pl.when(nt > 1)
        def _():
            load(1).start()

    @pl.when(i >= 2)
    def _():
        wait_rows(i - 2)

    @pl.when(i + 2 < nt)
    def _():
        load(i + 2).start()

    _index_copy(pos_hbm, idx_s, sem_i, i, islot).wait()

    @pl.when(i + 1 < nt)
    def _():
        _index_copy(pos_hbm, idx_s, sem_i, i + 1, 1 - islot).start()

    load(i).wait()
    slot = i % nslot

    def issue(r, carry):
        row = tiles.at[slot, pl.ds(r, 1)]
        pltpu.make_async_copy(row, xs_hbm.at[pl.ds(idx_s[islot, 0, r], 1)],
                              sem_d.at[slot]).start(priority=0)
        pltpu.make_async_copy(row, xs_hbm.at[pl.ds(idx_s[islot, 0, tm + r], 1)],
                              sem_d.at[slot]).start(priority=1)
        return carry

    lax.fori_loop(0, tm, issue, 0, unroll=DMA_ISSUE_UNROLL)

    @pl.when(i == nt - 1)
    def _():
        @pl.when(nt > 1)
        def _():
            wait_rows(i - 1)

        wait_rows(i)


def _dispatch(pos_tiles, hn):
    n, d = hn.shape
    tm = TOKEN_TILE
    nslot = 4
    return pl.pallas_call(
        _dispatch_kernel,
        grid=(n // tm,),
        in_specs=[pl.BlockSpec(memory_space=pl.ANY), pl.BlockSpec(memory_space=pl.ANY)],
        out_specs=pl.BlockSpec(memory_space=pl.ANY),
        out_shape=jax.ShapeDtypeStruct((2 * n, d), F32),
        scratch_shapes=[pltpu.SMEM((2, 1, 2 * tm), I32), pltpu.VMEM((nslot, tm, d), F32),
                        pltpu.SemaphoreType.DMA((2,)), pltpu.SemaphoreType.DMA((nslot,)),
                        pltpu.SemaphoreType.DMA((nslot,))],
        compiler_params=_params("arbitrary"),
        name="moe_dispatch",
    )(pos_tiles, hn)


def _moe_kernel(vt_ref, ve_ref, von_ref, vnext_ref, vslot_ref, lo_ref, hi_ref,
                xs_ref, wg_hbm, wu_hbm, wd_hbm, eo_ref,
                wg_f, wu_f, wd_f, wgb, wub, wdb, sem_w, *, layer):
    v = pl.program_id(0)
    tr = xs_ref.shape[0]
    prev = jnp.maximum(v - 1, 0)
    e = ve_ref[v]
    new_expert = jnp.logical_or(v == 0, e != ve_ref[prev])
    first_visit = jnp.logical_or(v == 0, vt_ref[v] != vt_ref[prev])

    def fetch(expert, slot):
        return [pltpu.make_async_copy(src.at[layer, expert], dst.at[slot], sem_w.at[slot, j])
                for j, (src, dst) in enumerate(((wg_hbm, wg_f), (wu_hbm, wu_f), (wd_hbm, wd_f)))]

    @pl.when(von_ref[v] == 1)
    def _():
        @pl.when(new_expert)
        def _():
            slot = vslot_ref[v]

            @pl.when(v == 0)
            def _():
                for cp in fetch(e, slot):
                    cp.start()

            for cp in fetch(e, slot):
                cp.wait()
            wgb[...] = wg_f[slot].astype(BF16)
            wub[...] = wu_f[slot].astype(BF16)
            wdb[...] = wd_f[slot].astype(BF16)

            @pl.when(vnext_ref[v] >= 0)
            def _():
                for cp in fetch(vnext_ref[v], 1 - slot):
                    cp.start()

        x = xs_ref[...].astype(BF16)
        act = jax.nn.silu(_dot(x, wgb[...])) * _dot(x, wub[...])
        row = vt_ref[v] * tr + lax.broadcasted_iota(I32, (tr, 1), 0)
        act = jnp.where((row >= lo_ref[e]) & (row < hi_ref[e]), act, 0.0)
        res = _dot(act.astype(BF16), wdb[...])

        @pl.when(first_visit)
        def _():
            eo_ref[...] = res

        @pl.when(jnp.logical_not(first_visit))
        def _():
            eo_ref[...] += res


def _moe_experts(vt, ve, von, vnext, vslot, lo, hi, xs, wg, wu, wd, layer):
    nv = vt.shape[0]
    rows, d = xs.shape
    hid = wg.shape[3]
    tr = EXPERT_ROW_TILE
    tile = pl.BlockSpec((tr, d), lambda v, vt, *_: (vt[v], 0))
    grid_spec = pltpu.PrefetchScalarGridSpec(
        num_scalar_prefetch=7,
        grid=(nv,),
        in_specs=[tile] + [pl.BlockSpec(memory_space=pl.ANY)] * 3,
        out_specs=tile,
        scratch_shapes=[
            pltpu.VMEM((2, d, hid), F32), pltpu.VMEM((2, d, hid), F32), pltpu.VMEM((2, hid, d), F32),
            pltpu.VMEM((d, hid), BF16), pltpu.VMEM((d, hid), BF16), pltpu.VMEM((hid, d), BF16),
            pltpu.SemaphoreType.DMA((2, 3)),
        ],
    )
    return pl.pallas_call(
        functools.partial(_moe_kernel, layer=layer),
        grid_spec=grid_spec,
        out_shape=jax.ShapeDtypeStruct((rows, d), F32),
        compiler_params=_params("arbitrary"),
        name="moe_experts",
    )(vt, ve, von, vnext, vslot, lo, hi, xs, wg, wu, wd)


def _combine_kernel(pos_hbm, eo_hbm, x_ref, rinfo_ref, g_ref, o1_ref, o2_ref, idx_s, a_buf, b_buf,
                    sem_i, sem_a, sem_b, *, last_layer):
    i = pl.program_id(0)
    nt = pl.num_programs(0)
    tm = a_buf.shape[1]
    slot = i % 2

    def gathers(s):
        def issue(r, carry):
            pltpu.make_async_copy(eo_hbm.at[pl.ds(idx_s[s, 0, r], 1)],
                                  a_buf.at[s, pl.ds(r, 1)], sem_a.at[s]).start(priority=0)
            pltpu.make_async_copy(eo_hbm.at[pl.ds(idx_s[s, 0, tm + r], 1)],
                                  b_buf.at[s, pl.ds(r, 1)], sem_b.at[s]).start(priority=1)
            return carry

        lax.fori_loop(0, tm, issue, 0, unroll=DMA_ISSUE_UNROLL)

    @pl.when(i == 0)
    def _():
        first = _index_copy(pos_hbm, idx_s, sem_i, 0, 0)
        first.start()
        first.wait()
        gathers(0)

        @pl.when(nt > 1)
        def _():
            _index_copy(pos_hbm, idx_s, sem_i, 1, 1).start()

    @pl.when(i + 1 < nt)
    def _():
        _index_copy(pos_hbm, idx_s, sem_i, i + 1, 1 - slot).wait()
        gathers(1 - slot)

    @pl.when(i + 2 < nt)
    def _():
        _index_copy(pos_hbm, idx_s, sem_i, i + 2, slot).start()

    pltpu.make_async_copy(eo_hbm.at[pl.ds(0, tm)], a_buf.at[slot], sem_a.at[slot]).wait()
    pltpu.make_async_copy(eo_hbm.at[pl.ds(0, tm)], b_buf.at[slot], sem_b.at[slot]).wait()
    info = rinfo_ref[...]
    x2 = x_ref[...] + (info[:, 2:3] * a_buf[slot] + info[:, 3:4] * b_buf[slot])
    hn = _rms(x2, g_ref[...])
    if last_layer:
        @pl.when(i < nt - 1)
        def _():
            o1_ref[...] = hn

        @pl.when(i == nt - 1)
        def _():
            o2_ref[...] = hn
    else:
        o1_ref[...] = x2
        o2_ref[...] = hn


def _combine(pos_tiles, eo, x, rinfo, g, last_layer):
    n, d = x.shape
    tm = TOKEN_TILE
    nt = n // tm
    row = pl.BlockSpec((tm, d), lambda i: (i, 0))
    if last_layer:
        out_specs = [pl.BlockSpec((tm, d), lambda i: (jnp.minimum(i, nt - 2), 0)),
                     pl.BlockSpec((tm, d), lambda i: (0, 0))]
        out_shape = [jax.ShapeDtypeStruct((n - tm, d), F32), jax.ShapeDtypeStruct((tm, d), F32)]
    else:
        out_specs = [row, row]
        out_shape = [jax.ShapeDtypeStruct((n, d), F32), jax.ShapeDtypeStruct((n, d), F32)]
    return pl.pallas_call(
        functools.partial(_combine_kernel, last_layer=last_layer),
        grid=(nt,),
        in_specs=[
            pl.BlockSpec(memory_space=pl.ANY),
            pl.BlockSpec(memory_space=pl.ANY),
            row,
            pl.BlockSpec((tm, LANES), lambda i: (i, 0)),
            pl.BlockSpec((1, d), lambda i: (0, 0)),
        ],
        out_specs=out_specs,
        out_shape=out_shape,
        scratch_shapes=[
            pltpu.SMEM((2, 1, 2 * tm), I32),
            pltpu.VMEM((2, tm, d), F32),
            pltpu.VMEM((2, tm, d), F32),
            pltpu.SemaphoreType.DMA((2,)),
            pltpu.SemaphoreType.DMA((2,)),
            pltpu.SemaphoreType.DMA((2,)),
        ],
        compiler_params=_params("arbitrary"),
        name="moe_combine",
    )(pos_tiles, eo, x, rinfo, g)


def _moe(x1, hn, rinfo, rinfo_t, counts, wg, wu, wd, layer, next_gain, last_layer):
    n = x1.shape[0]
    ne = wg.shape[1]
    tr = EXPERT_ROW_TILE
    tm = TOKEN_TILE
    nv = (2 * n) // tr + ne - 1
    eid = rinfo_t[0:2].astype(I32)
    rank = rinfo_t[4:6].astype(I32)
    cnt = counts[0, :ne].astype(I32)
    seg_end = jnp.cumsum(cnt)
    seg_start = seg_end - cnt
    experts = jnp.arange(ne, dtype=I32)
    pos = rank + jnp.sum(jnp.where(eid[:, None, :] == experts[None, :, None],
                                   seg_start[None, :, None], 0), axis=1)
    pos_tiles = pos.reshape(2, n // tm, tm).transpose(1, 0, 2).reshape(n // tm, 1, 2 * tm)

    first_tile = seg_start // tr
    tiles_e = jnp.where(cnt > 0, (seg_end - 1) // tr - first_tile + 1, 0)
    v_end = jnp.cumsum(tiles_e)
    v_start = v_end - tiles_e
    total = v_end[-1]
    vis = jnp.arange(nv, dtype=I32)
    vc = jnp.minimum(vis, jnp.maximum(total - 1, 0))
    ve = jnp.sum((vc[:, None] >= v_end[None, :]).astype(I32), axis=1)
    von = (vis < total).astype(I32)
    present = cnt > 0
    later = present[None, :] & (experts[None, :] > experts[:, None])
    next_of = jnp.min(jnp.where(later, experts[None, :], ne), axis=1)
    tabs = jnp.stack([first_tile, v_start, jnp.where(next_of < ne, next_of, -1),
                      jnp.cumsum(present.astype(I32)) - 1])
    picked = jnp.sum(jnp.where(ve[None, :, None] == experts, tabs[:, None, :], 0), axis=2)
    vt = picked[0] + (vc - picked[1])
    vnext = picked[2]
    vslot = picked[3] & 1

    xs = _dispatch(pos_tiles, hn)
    eo = _moe_experts(vt, ve, von, vnext, vslot, seg_start, seg_end, xs, wg, wu, wd, layer)
    return _combine(pos_tiles, eo, x1, rinfo, next_gain, last_layer)


def _cis(log_mag, ang):
    mag = jnp.exp(log_mag)
    return mag * jnp.cos(ang), mag * jnp.sin(ang)


def _split3(a):
    p1 = a.astype(BF16)
    r1 = a - p1.astype(F32)
    p2 = r1.astype(BF16)
    return p1, p2, (r1 - p2.astype(F32)).astype(BF16)


def _select_dot(a, sel, sel_first=False):
    sel = sel.astype(BF16)
    parts = [(_dot(sel, p) if sel_first else _dot(p, sel)) for p in _split3(a)]
    return parts[0] + (parts[1] + parts[2])


def _dot3(a, b):
    a_hi, a_lo = _split_bf16(a)
    b_hi, b_lo = _split_bf16(b)
    return _dot(a_hi, b_hi) + (_dot(a_lo, b_hi) + _dot(a_hi, b_lo))


def _s5_prep_kernel(lsr_ref, lrr_ref, lir_ref, ctre_ref, ctim_ref,
                    btre_ref, btim_ref, d_ref,
                    t_ref, wre_ref, wim_ref, cyre_ref, cyim_ref, apr_ref, api_ref,
                    bbr_ref, bbi_ref, lbr_ref, lbi_ref, *, gw, gpb, sub):
    blk = gpb * gw
    wide = sub * blk
    nsl = lrr_ref.shape[2]
    ns = nsl // gpb
    sh_gw, sh_ns = gw.bit_length() - 1, ns.bit_length() - 1
    dt_r = jnp.exp(lsr_ref[0])
    ldt_r_re, ldt_r_im = lrr_ref[0] * dt_r, lir_ref[0] * dt_r
    diag = (lax.broadcasted_iota(I32, (nsl, nsl), 0) == lax.broadcasted_iota(I32, (nsl, nsl), 1))
    to_col = lambda r: jnp.sum(jnp.where(diag, r, 0.0), axis=1, keepdims=True)
    ldt_c_re, ldt_c_im = to_col(ldt_r_re), to_col(ldt_r_im)

    lane_w = lax.broadcasted_iota(I32, (1, wide), 1)
    spread = jnp.where((lax.broadcasted_iota(I32, (gw, wide), 1) & (gw - 1))
                       == lax.broadcasted_iota(I32, (gw, wide), 0), 1.0, 0.0)
    cre = _select_dot(ctre_ref[0], spread)
    cim = _select_dot(ctim_ref[0], spread)
    same = (jnp.right_shift(lax.broadcasted_iota(I32, (nsl, 1), 0), sh_ns)
            == (jnp.right_shift(lane_w, sh_gw) & (gpb - 1)))
    assert 2 * blk == LANES
    tau = lax.broadcasted_iota(I32, (1, LANES), 1).astype(F32)
    pw_re, pw_im = _cis(tau * ldt_c_re, tau * ldt_c_im)
    low = lax.broadcasted_iota(I32, (nsl, LANES), 1) < blk

    def spread_pow(p, first):
        col = lambda t: jnp.broadcast_to(p[:, t:t + 1], (nsl, LANES))
        return jnp.concatenate([jnp.where(low, col(first + 2 * m), col(first + 2 * m + 1))
                                for m in range(sub // 2)], axis=1)

    def c_lam_pow(first):
        pr, pi = spread_pow(pw_re, first), spread_pow(pw_im, first)
        return (jnp.where(same, pr * cre - pi * cim, 0.0),
                jnp.where(same, -(pr * cim + pi * cre), 0.0))

    clr0, cli0 = c_lam_pow(0)
    clr1, cli1 = c_lam_pow(1)
    cyre_ref[0] = clr1.astype(BF16)
    cyim_ref[0] = cli1.astype(BF16)

    lbr, lbi = _cis(ldt_r_re, ldt_r_im)
    lbr_ref[0] = lbr
    lbi_ref[0] = lbi
    lr, li = lrr_ref[0], lir_ref[0]
    nr, ni = lbr - 1.0, lbi
    den = lr * lr + li * li
    fr = (nr * lr + ni * li) / den
    fi = (ni * lr - nr * li) / den
    bre, bim = btre_ref[0], btim_ref[0]
    bbr = fr * bre - fi * bim
    bbi = fr * bim + fi * bre
    bbr_ref[0] = bbr
    bbi_ref[0] = bbi

    lane_t = lax.broadcasted_iota(I32, (gw, wide), 1)
    chan = lax.broadcasted_iota(I32, (gw, wide), 0)
    for g in range(gpb):
        ps = slice(g * ns, (g + 1) * ns)
        r = _dot3(bbr[:, ps], clr0[ps, :]) + _dot3(bbi[:, ps], cli0[ps, :])
        r = r + jnp.where(lane_t == g * gw + chan, d_ref[0][:, g:g + 1], 0.0)
        for j in range(sub):
            tb = r if j == 0 else jnp.where(lane_t >= blk * j, pltpu.roll(r, blk * j, 1), 0.0)
            r0 = j * blk + g * gw
            t_ref[0, r0:r0 + gw, :] = tb.astype(BF16)

    rows = lax.broadcasted_iota(I32, (wide, 1), 0)
    spread_t = jnp.where((lax.broadcasted_iota(I32, (wide, gw), 0) & (gw - 1))
                         == lax.broadcasted_iota(I32, (wide, gw), 1), 1.0, 0.0)
    bbr_t = _select_dot(bbr, spread_t, sel_first=True)
    bbi_t = _select_dot(bbi, spread_t, sel_first=True)
    same_w = ((jnp.right_shift(rows, sh_gw) & (gpb - 1))
              == jnp.right_shift(lax.broadcasted_iota(I32, (1, nsl), 1), sh_ns))
    rj = ((sub - 1) - lax.broadcasted_iota(I32, (sub, 1), 0)).astype(F32)
    q_re, q_im = _cis(rj * ldt_r_re, rj * ldt_r_im)
    per_step = lambda q: jnp.concatenate(
        [jnp.broadcast_to(q[j:j + 1, :], (blk, nsl)) for j in range(sub)], axis=0)
    pr, pi = per_step(q_re), per_step(q_im)
    wre_ref[0] = jnp.where(same_w, pr * bbr_t - pi * bbi_t, 0.0).astype(BF16)
    wim_ref[0] = jnp.where(same_w, pr * bbi_t + pi * bbr_t, 0.0).astype(BF16)

    nlev = apr_ref.shape[1]
    pw = (sub * jnp.left_shift(1, lax.broadcasted_iota(I32, (nlev, 1), 0))).astype(F32)
    apr, api = _cis(pw * ldt_r_re, pw * ldt_r_im)
    apr_ref[0] = apr
    api_ref[0] = api


def _s5_prep(log_step, lam_re, lam_im, b_re, b_im, c_re, c_im, d_skip):
    ng, ns = lam_re.shape
    gw = b_re.shape[2]
    sub, gpb = S5_SUB, S5_GROUPS_PER_BLOCK
    nblk = ng // gpb
    nsl = gpb * ns
    wide = sub * gpb * gw
    ls = jnp.repeat(log_step, ns)
    ct = lambda c: jnp.transpose(c, (0, 2, 1)).reshape(nblk, nsl, gw)
    bt = lambda b: jnp.transpose(b.reshape(nblk, gpb, ns, gw), (0, 3, 1, 2)).reshape(nblk, gw, nsl)
    ins = [ls.reshape(nblk, 1, nsl), lam_re.reshape(nblk, 1, nsl), lam_im.reshape(nblk, 1, nsl),
           ct(c_re), ct(c_im), bt(b_re), bt(b_im),
           jnp.transpose(d_skip.reshape(nblk, gpb, gw), (0, 2, 1))]
    spec = lambda a: pl.BlockSpec((1,) + a.shape[1:], lambda g: (g, 0, 0))
    outs = [
        jax.ShapeDtypeStruct((nblk, wide, wide), BF16),
        jax.ShapeDtypeStruct((nblk, wide, nsl), BF16),
        jax.ShapeDtypeStruct((nblk, wide, nsl), BF16),
        jax.ShapeDtypeStruct((nblk, nsl, wide), BF16),
        jax.ShapeDtypeStruct((nblk, nsl, wide), BF16),
        jax.ShapeDtypeStruct((nblk, SUBLANES, nsl), F32),
        jax.ShapeDtypeStruct((nblk, SUBLANES, nsl), F32),
        jax.ShapeDtypeStruct((nblk, gw, nsl), F32),
        jax.ShapeDtypeStruct((nblk, gw, nsl), F32),
        jax.ShapeDtypeStruct((nblk, 1, nsl), F32),
        jax.ShapeDtypeStruct((nblk, 1, nsl), F32),
    ]
    return pl.pallas_call(
        functools.partial(_s5_prep_kernel, gw=gw, gpb=gpb, sub=sub),
        grid=(nblk,),
        in_specs=[spec(a) for a in ins],
        out_specs=[spec(o) for o in outs],
        out_shape=outs,
        compiler_params=_params("parallel"),
        name="s5_prep",
    )(*ins)


def _s5_prompt_kernel(x_ref, t_ref, wre_ref, wim_ref, cyre_ref, cyim_ref, apr_ref, api_ref,
                      y_ref, hre_out, him_out, u_s, y4_s, *, nb, seq, sub):
    nk = seq // sub
    rows = nb * nk
    hl = LANES // 2
    nlev = nk.bit_length() - 1
    nsl = wre_ref.shape[2]
    low = lax.broadcasted_iota(I32, (nk, LANES), 1) < hl
    kidx = lax.broadcasted_iota(I32, (rows, 1), 0) & (nk - 1)

    def step_rows(b, j):
        return (pl.ds(b * seq + j, nk, stride=sub), slice(None))

    for b in range(nb):
        rs = slice(b * nk, (b + 1) * nk)
        for m in range(sub // 2):
            ls = slice(m * LANES, (m + 1) * LANES)
            s0 = x_ref[step_rows(b, 2 * m)]
            s1 = x_ref[step_rows(b, 2 * m + 1)]
            u_s[0, rs, ls] = jnp.where(low, s0, pltpu.roll(s1, hl, 1)).astype(BF16)
            u_s[1, rs, ls] = jnp.where(low, pltpu.roll(s0, hl, 1), s1).astype(BF16)

    def shifted(a, s):
        return jnp.where(kidx >= s, pltpu.roll(a, s, 0), 0.0)

    halves = range(2)
    us = [u_s[h] for h in halves]
    hre = [_dot(us[h], wre_ref[h]) for h in halves]
    him = [_dot(us[h], wim_ref[h]) for h in halves]
    cw = 2 * LANES
    y_conv = [jnp.concatenate(
        [_dot(us[h][:, 0:c0 + cw], t_ref[h, 0:c0 + cw, c0:c0 + cw]) for c0 in range(0, us[h].shape[1], cw)],
        axis=1) for h in halves]
    for i in range(nlev):
        for h in halves:
            ar = apr_ref[h, i:i + 1, :]
            ai = api_ref[h, i:i + 1, :]
            pre, pim = shifted(hre[h], 1 << i), shifted(him[h], 1 << i)
            hre[h], him[h] = hre[h] + (ar * pre - ai * pim), him[h] + (ar * pim + ai * pre)
    for h in halves:
        ls = slice(h * nsl, (h + 1) * nsl)
        for b in range(nb):
            last = (b + 1) * nk - 1
            hre_out[b:b + 1, ls] = hre[h][last:last + 1, :]
            him_out[b:b + 1, ls] = him[h][last:last + 1, :]
        hpre = shifted(hre[h], 1).astype(BF16)
        hpim = shifted(him[h], 1).astype(BF16)
        y4_s[h] = y_conv[h] + _dot(hpre, cyre_ref[h]) + _dot(hpim, cyim_ref[h])

    for b in range(nb):
        rs = slice(b * nk, (b + 1) * nk)
        for m in range(sub // 2):
            ls = slice(m * LANES, (m + 1) * LANES)
            ca = y4_s[0, rs, ls]
            cb = y4_s[1, rs, ls]
            y_ref[step_rows(b, 2 * m)] = jnp.where(low, ca, pltpu.roll(cb, hl, 1))
            y_ref[step_rows(b, 2 * m + 1)] = jnp.where(low, pltpu.roll(ca, hl, 1), cb)


def _s5_prompt(h, prep, batch, seq):
    d = h.shape[1]
    tmat, wre, wim, cyre, cyim, apr, api = prep[:7]
    nblk, wide, nsl = wre.shape
    sub, nb = S5_SUB, S5_BATCH_PER_STEP
    nk = seq // sub
    assert nk & (nk - 1) == 0 and nk.bit_length() - 1 <= apr.shape[1]
    ntile = d // LANES
    assert nblk == 2 * ntile
    rows = nb * nk
    wspec = lambda a: pl.BlockSpec((2,) + a.shape[1:], lambda t, b: (t, 0, 0))
    xspec = pl.BlockSpec((nb * seq, LANES), lambda t, b: (b, t))
    sspec = pl.BlockSpec((None, nb, 2 * nsl), lambda t, b: (b, 0, t))
    return pl.pallas_call(
        functools.partial(_s5_prompt_kernel, nb=nb, seq=seq, sub=sub),
        grid=(ntile, batch // nb),
        in_specs=[xspec, wspec(tmat), wspec(wre), wspec(wim), wspec(cyre), wspec(cyim),
                  wspec(apr), wspec(api)],
        out_specs=[xspec, sspec, sspec],
        out_shape=[
            jax.ShapeDtypeStruct((batch * seq, d), F32),
            jax.ShapeDtypeStruct((batch // nb, nb, ntile * 2 * nsl), F32),
            jax.ShapeDtypeStruct((batch // nb, nb, ntile * 2 * nsl), F32),
        ],
        scratch_shapes=[
            pltpu.VMEM((2, rows, wide), BF16),
            pltpu.VMEM((2, rows, wide), F32),
        ],
        compiler_params=_params("parallel", "parallel"),
        name="s5_prompt",
    )(h, tmat, wre, wim, cyre, cyim, apr, api)


def _s5_step_kernel(u_ref, hre_ref, him_ref, bbr_ref, bbi_ref, lbr_ref, lbi_ref, cre_ref, cim_ref,
                    d_ref, nre_ref, nim_ref, y_ref):
    for g in range(u_ref.shape[0]):
        u = u_ref[g]
        hre, him, lbr, lbi = hre_ref[g], him_ref[g], lbr_ref[g], lbi_ref[g]
        nre = lbr * hre - lbi * him + _dot3(u, bbr_ref[g])
        nim = lbr * him + lbi * hre + _dot3(u, bbi_ref[g])
        nre_ref[g] = nre
        nim_ref[g] = nim
        y_ref[g] = _dot3(nre, cre_ref[g]) - _dot3(nim, cim_ref[g]) + d_ref[g] * u


def _s5_step(u_s, h_re, h_im, bbr, bbi, lbr, lbi, cre_t, cim_t, d_g):
    ng = u_s.shape[0]
    gb = SUBLANES
    spec = lambda a: pl.BlockSpec((gb,) + a.shape[1:], lambda g: (g, 0, 0))
    ins = [u_s, h_re, h_im, bbr, bbi, lbr, lbi, cre_t, cim_t, d_g]
    outs = [jax.ShapeDtypeStruct(h_re.shape, F32), jax.ShapeDtypeStruct(h_re.shape, F32),
            jax.ShapeDtypeStruct(u_s.shape, F32)]
    return pl.pallas_call(
        _s5_step_kernel,
        grid=(ng // gb,),
        in_specs=[spec(a) for a in ins],
        out_specs=[spec(o) for o in outs],
        out_shape=outs,
        compiler_params=_params("parallel"),
        name="s5_step",
    )(*ins)


def _pad_to(a, axis, size):
    pad = [(0, 0)] * a.ndim
    pad[axis] = (0, size - a.shape[axis])
    return jnp.pad(a, pad)


def kernel(x_prompt, x_sample, state_pool, state_mlstm_c, state_mlstm_n, state_mlstm_m, state_s5_re, state_s5_im, norm_mix, norm_ffn, norm_final, w_in_ab, b_gates, pool_w, pool_scale, w_out_ab, s5_lam_re, s5_lam_im, s5_log_step, s5_b_re, s5_b_im, s5_c_re, s5_c_im, s5_d, w_glu, moe_w_group, moe_b_group, moe_w_expert, moe_b_expert, moe_w_gate, moe_w_up, moe_w_down):
    bp, tp, d = x_prompt.shape
    bs = x_sample.shape[0]
    n_p = bp * tp
    tm = TOKEN_TILE
    assert n_p % tm == 0 and bs <= tm
    nh, dh = state_mlstm_c.shape[2], state_mlstm_c.shape[3]
    pool_width = state_pool.shape[3]
    ml_width = nh * dh
    n_main = pool_width + 4 * ml_width
    n_gates = 2 * nh
    ngrp_s5, n_state = s5_lam_re.shape[1], s5_lam_re.shape[2]
    gw_s5 = d // ngrp_s5

    n = n_p + tm

    def tail_tile(sample_rows, dtype):
        return _pad_to(sample_rows.astype(dtype), 0, tm)

    x_main = x_prompt.reshape(n_p, d)
    x_tail = tail_tile(x_sample.reshape(bs, d), F32)
    tril = jnp.tril(jnp.ones((tm, tm), BF16), -1)

    def router_weights(l):
        wr = _pad_to(jnp.concatenate([moe_w_group[l], moe_w_expert[l]], axis=1), 1, LANES)
        br = jnp.concatenate([moe_b_group[l], moe_b_expert[l]])[None, :]
        hi = wr.astype(BF16)
        lo = (wr - hi.astype(F32)).astype(BF16)
        return jnp.concatenate([hi, lo], axis=1), _pad_to(br, 1, LANES)

    w_in = w_in_ab[0]
    z, gates, gates_t = _inproj(
        x_main, x_tail, norm_mix[0][None, :], w_in,
        _pad_to(b_gates[0][None, :], 1, LANES), n_gates)

    pw = pool_w[0].astype(BF16)
    ps = pool_scale[0][None, :]
    pool_y_p, pool_p = _pool_prompt(z, pw, ps, bp, tp)
    pool_y_s, pool_s_t = _pool_step(z, jnp.transpose(state_pool[0], (1, 0, 2)), pw, ps, n_p)
    pool_s = jnp.transpose(pool_s_t, (1, 0, 2))

    ml_y_p, c_p, n_p_st, m_p = _mlstm_prompt(z, gates_t, bp, tp, nh, dh)
    ml_y_p = ml_y_p.reshape(n_p, ml_width)
    g_s = gates[n_p:n_p + bs, :n_gates]
    ml_y_s, c_s, n_s_st, m_s = _mlstm_step(
        z[n_p:n_p + bs].reshape(bs, 1, n_main),
        g_s[:, :nh].reshape(bs, nh, 1, 1), g_s[:, nh:].reshape(bs, nh, 1, 1),
        state_mlstm_c[0], state_mlstm_n[0].reshape(bs, nh, 1, dh),
        state_mlstm_m[0].reshape(bs, nh, 1, 1), nh, dh)

    rows = [x_main, x_tail, pool_y_p, tail_tile(pool_y_s, BF16),
            ml_y_p, tail_tile(ml_y_s.reshape(bs, ml_width), BF16)]
    specs = (_stacked_specs(x_main, x_tail) + _stacked_specs(pool_y_p, rows[3])
             + _stacked_specs(ml_y_p, rows[5]))
    wr, br = router_weights(0)
    x1, hn, rinfo, rinfo_t, counts = _mix_route(
        _outproj_route_kernel, "outproj_route", n, specs, rows, w_out_ab[0],
        norm_ffn[0][None, :], wr, br, tril)
    x2, h1 = _moe(x1, hn, rinfo, rinfo_t, counts, moe_w_gate, moe_w_up, moe_w_down, 0,
                  norm_mix[1][None, :], last_layer=False)

    prep = _s5_prep(s5_log_step[0], s5_lam_re[0], s5_lam_im[0], s5_b_re[0], s5_b_im[0],
                    s5_c_re[0], s5_c_im[0], s5_d[0])
    y_p, hre_p, him_p = _s5_prompt(h1, prep, bp, tp)
    s5_re_p = hre_p.reshape(bp, ngrp_s5, n_state)
    s5_im_p = him_p.reshape(bp, ngrp_s5, n_state)

    gpb = S5_GROUPS_PER_BLOCK
    per_group = lambda a: jnp.transpose(
        a.reshape(ngrp_s5 // gpb, gw_s5, gpb, n_state), (0, 2, 1, 3)).reshape(ngrp_s5, gw_s5, n_state)
    bbr, bbi = per_group(prep[7]), per_group(prep[8])
    lbr, lbi = prep[9].reshape(ngrp_s5, 1, n_state), prep[10].reshape(ngrp_s5, 1, n_state)
    d_g = s5_d[0].reshape(ngrp_s5, 1, gw_s5)
    u_s = h1[n_p:n_p + bs].reshape(bs, ngrp_s5, gw_s5).transpose(1, 0, 2)
    h_re = jnp.transpose(state_s5_re[0], (1, 0, 2))
    h_im = jnp.transpose(state_s5_im[0], (1, 0, 2))
    nre_s, nim_s, y_s = _s5_step(
        u_s, h_re, h_im, bbr, bbi, lbr, lbi,
        jnp.transpose(s5_c_re[0], (0, 2, 1)), jnp.transpose(s5_c_im[0], (0, 2, 1)), d_g)
    s5_re_s = jnp.transpose(nre_s, (1, 0, 2))
    s5_im_s = jnp.transpose(nim_s, (1, 0, 2))
    y_tail = tail_tile(jnp.transpose(y_s, (1, 0, 2)).reshape(bs, d), F32)

    wr, br = router_weights(1)
    specs = [pl.BlockSpec((tm, d), lambda i: (i, 0))] + _stacked_specs(y_p, y_tail)
    x3, hn, rinfo, rinfo_t, counts = _mix_route(
        _glu_route_kernel, "glu_route", n, specs, [x2, y_p, y_tail], w_glu[0],
        norm_ffn[1][None, :], wr, br, tril)
    y_main, y_last = _moe(x3, hn, rinfo, rinfo_t, counts, moe_w_gate, moe_w_up, moe_w_down, 1,
                          norm_final[None, :], last_layer=True)

    return (y_main.reshape(bp, tp, d), y_last[:bs].reshape(bs, 1, d),
            pool_p[None], c_p[None], n_p_st.reshape(1, bp, nh, dh), m_p[:, :, 0, 0][None],
            s5_re_p[None], s5_im_p[None],
            pool_s[None], c_s[None], n_s_st.reshape(1, bs, nh, dh), m_s.reshape(1, bs, nh),
            s5_re_s[None], s5_im_s[None])
```

```python
import functools

import jax
import jax.numpy as jnp
from jax import lax
from jax.experimental import pallas as pl
from jax.experimental.pallas import tpu as pltpu

F32 = jnp.float32
BF16 = jnp.bfloat16
I32 = jnp.int32

PAST_LEN = 16384
POOL_WINDOWS = (2, 4, 8, 16)
POOL_BUF = max(POOL_WINDOWS) - 1
MLSTM_CHUNK = 128
S5_SUB = 16
MOE_GROUPS = 4
MOE_EXPERTS_PER_GROUP = 8
RMS_EPS = 1e-6

LANES = 128
SUBLANES = 8
VMEM_LIMIT_BYTES = 56 * 1024 * 1024

TOKEN_TILE = 512
EXPERT_ROW_TILE = 512
MLSTM_SEQ_PER_STEP = 8
MLSTM_STEP_BATCH = 16
POOL_TIME_TILE = 2048
DMA_ISSUE_UNROLL = 8
S5_GROUPS_PER_BLOCK = 4
S5_BATCH_PER_STEP = 4


def _params(*sem):
    return pltpu.CompilerParams(dimension_semantics=sem, vmem_limit_bytes=VMEM_LIMIT_BYTES)


def _rms(x, g):
    return x * lax.rsqrt(jnp.mean(x * x, axis=-1, keepdims=True) + RMS_EPS) * g


def _dot(a, b):
    return jnp.dot(a, b, preferred_element_type=F32)


def _dot_nt(a, b):
    return lax.dot_general(a, b, (((1,), (1,)), ((), ())), preferred_element_type=F32)


def _dot_tn(a, b):
    return lax.dot_general(a, b, (((0,), (0,)), ((), ())), preferred_element_type=F32)


def _stacked(main_ref, tail_ref):
    last = pl.program_id(0) == pl.num_programs(0) - 1
    return jnp.where(last, tail_ref[...], main_ref[...])


def _stacked_specs(main, tail):
    tm, w = tail.shape
    last_main = main.shape[0] // tm - 1
    return [pl.BlockSpec((tm, w), lambda i: (jnp.minimum(i, last_main), 0)),
            pl.BlockSpec((tm, w), lambda i: (0, 0))]


def _weight_spec(w):
    return pl.BlockSpec(w.shape, lambda i: (0,) * w.ndim, pipeline_mode=pl.Buffered(1))


def _inproj_kernel(xm_ref, xt_ref, g_ref, w_ref, bg_ref, z_ref, gates_ref, gatest_ref, wb):
    @pl.when(pl.program_id(0) == 0)
    def _():
        wb[...] = jnp.zeros(wb.shape, BF16)
        wb[:, 0:w_ref.shape[1]] = w_ref[...].astype(BF16)

    h = _rms(_stacked(xm_ref, xt_ref), g_ref[...]).astype(BF16)
    zg = _dot(h, wb[...])
    nz = z_ref.shape[1]
    z_ref[...] = zg[:, 0:nz]
    gates = zg[:, nz:nz + LANES] + bg_ref[...]
    gates_ref[...] = gates
    gatest_ref[...] = gates.T[0:gatest_ref.shape[0], :]


def _inproj(x_main, x_tail, g, w, bg, ng):
    d = x_main.shape[1]
    tm = TOKEN_TILE
    n = x_main.shape[0] + tm
    nz = w.shape[1] - ng
    full = lambda i: (0, 0)
    return pl.pallas_call(
        _inproj_kernel,
        grid=(n // tm,),
        in_specs=_stacked_specs(x_main, x_tail) + [
            pl.BlockSpec((1, d), full),
            _weight_spec(w),
            pl.BlockSpec((1, LANES), full),
        ],
        out_specs=[
            pl.BlockSpec((tm, nz), lambda i: (i, 0)),
            pl.BlockSpec((tm, LANES), lambda i: (i, 0)),
            pl.BlockSpec((ng, tm), lambda i: (0, i)),
        ],
        out_shape=[
            jax.ShapeDtypeStruct((n, nz), F32),
            jax.ShapeDtypeStruct((n, LANES), F32),
            jax.ShapeDtypeStruct((ng, n), F32),
        ],
        scratch_shapes=[pltpu.VMEM((d, nz + LANES), BF16)],
        compiler_params=_params("arbitrary"),
        name="inproj",
    )(x_main, x_tail, g, w, bg)


def _pool_prompt_kernel(u_ref, pw_ref, ps_ref, y_ref, st_ref, ext_ref, *, tt, gw):
    t = pl.program_id(1)
    nt = pl.num_programs(1)
    halo = POOL_BUF + 1
    width = ext_ref.shape[1]

    @pl.when(t == 0)
    def _():
        ext_ref[0:halo, :] = jnp.zeros((halo, width), F32)

    u = u_ref[...]
    ext_ref[halo:halo + tt, :] = u
    pos = t * tt + lax.broadcasted_iota(I32, (tt, 1), 0)
    for g, w in enumerate(POOL_WINDOWS):
        c0 = g * gw
        acc = u[:, c0:c0 + gw]
        for j in range(1, w):
            acc = acc + ext_ref[halo - j:halo - j + tt, c0:c0 + gw]
        cnt = jnp.minimum(w, pos + 1).astype(F32)
        d = acc / cnt - u[:, c0:c0 + gw]
        y = _dot(d.astype(BF16), pw_ref[g]) * ps_ref[:, c0:c0 + gw]
        y_ref[:, c0:c0 + gw] = y.astype(BF16)

    @pl.when(t == nt - 1)
    def _():
        st_ref[...] = ext_ref[tt + 1:tt + halo, :]

    ext_ref[0:halo, :] = ext_ref[tt:tt + halo, :]


def _pool_prompt(z, pw, ps, batch, seq):
    width = ps.shape[1]
    gw = width // len(POOL_WINDOWS)
    tt = POOL_TIME_TILE
    nt = seq // tt
    return pl.pallas_call(
        functools.partial(_pool_prompt_kernel, tt=tt, gw=gw),
        grid=(batch, nt),
        in_specs=[
            pl.BlockSpec((tt, width), lambda b, t: (b * nt + t, 0)),
            pl.BlockSpec(pw.shape, lambda b, t: (0, 0, 0)),
            pl.BlockSpec((1, width), lambda b, t: (0, 0)),
        ],
        out_specs=[
            pl.BlockSpec((tt, width), lambda b, t: (b * nt + t, 0)),
            pl.BlockSpec((None, POOL_BUF, width), lambda b, t: (b, 0, 0)),
        ],
        out_shape=[
            jax.ShapeDtypeStruct((batch * seq, width), BF16),
            jax.ShapeDtypeStruct((batch, POOL_BUF, width), F32),
        ],
        scratch_shapes=[pltpu.VMEM((POOL_BUF + 1 + tt, width), F32)],
        compiler_params=_params("parallel", "arbitrary"),
        name="pool_prompt",
    )(z, pw, ps)


def _pool_step_kernel(u_ref, buf_ref, pw_ref, ps_ref, y_ref, nb_ref, *, gw):
    u = u_ref[...]
    for g, w in enumerate(POOL_WINDOWS):
        c0 = g * gw
        acc = u[:, c0:c0 + gw]
        for j in range(1, w):
            acc = acc + buf_ref[POOL_BUF - j, :, c0:c0 + gw]
        cnt = float(min(w, PAST_LEN + 1))
        d = acc / cnt - u[:, c0:c0 + gw]
        y = _dot(d.astype(BF16), pw_ref[g]) * ps_ref[:, c0:c0 + gw]
        y_ref[:, c0:c0 + gw] = y.astype(BF16)
    nb_ref[0:POOL_BUF - 1] = buf_ref[1:POOL_BUF]
    nb_ref[POOL_BUF - 1] = u


def _pool_step(z, buf_t, pw, ps, row0):
    _, batch, width = buf_t.shape
    gw = width // len(POOL_WINDOWS)
    return pl.pallas_call(
        functools.partial(_pool_step_kernel, gw=gw),
        grid=(1,),
        in_specs=[
            pl.BlockSpec((batch, width), lambda i: (row0 // batch, 0)),
            pl.BlockSpec(buf_t.shape, lambda i: (0, 0, 0)),
            pl.BlockSpec(pw.shape, lambda i: (0, 0, 0)),
            pl.BlockSpec((1, width), lambda i: (0, 0)),
        ],
        out_specs=[
            pl.BlockSpec((batch, width), lambda i: (0, 0)),
            pl.BlockSpec(buf_t.shape, lambda i: (0, 0, 0)),
        ],
        out_shape=[
            jax.ShapeDtypeStruct((batch, width), BF16),
            jax.ShapeDtypeStruct(buf_t.shape, F32),
        ],
        compiler_params=_params("arbitrary"),
        name="pool_step",
    )(z, buf_t, pw, ps)


def _mlstm_prompt_kernel(*refs, nh, dh, nseq):
    seq_in = [refs[5 * s:5 * s + 5] for s in range(nseq)]
    h_ref, c_out, n_out, m_out, c_s, n_s, m_s = refs[5 * nseq:]
    ci = pl.program_id(1)
    nc = pl.num_programs(1)
    ln = seq_in[0][0].shape[0]

    @pl.when(ci == 0)
    def _():
        c_s[...] = jnp.zeros(c_s.shape, F32)
        n_s[...] = jnp.zeros(n_s.shape, F32)
        m_s[...] = jnp.zeros(m_s.shape, F32)

    row = lax.broadcasted_iota(I32, (ln, ln), 0)
    col = lax.broadcasted_iota(I32, (ln, ln), 1)
    causal_t = row <= col
    eye = col == row
    lane = lax.broadcasted_iota(I32, (nh, ln), 1)

    def to_col(r):
        return jnp.sum(jnp.where(eye, r, 0.0), axis=1, keepdims=True)

    scale = dh ** -0.5
    pairs = [(s, h) for s in range(nseq) for h in range(nh)]
    c_old = {p: c_s[p[0], p[1]] for p in pairs}
    n_old = {p: n_s[p[0], p[1]] for p in pairs}
    m_old = {p: m_s[p[0], p[1]][:, 0:1] for p in pairs}
    gates = []
    for s in range(nseq):
        gt = seq_in[s][4][...]
        bc_all = jax.nn.log_sigmoid(gt[nh:2 * nh])
        sh = 1
        while sh < ln:
            bc_all = bc_all + jnp.where(lane >= sh, pltpu.roll(bc_all, sh, 1), 0.0)
            sh *= 2
        gates.append((gt[0:nh], bc_all))
    ch = {}
    for s, h in pairs:
        q_ref, k_ref, v_ref, _, _ = seq_in[s]
        li_all, bc_all = gates[s]
        sl = slice(h * dh, (h + 1) * dh)
        k = k_ref[:, sl] * scale
        v = v_ref[:, sl]
        li_r, bc_r = li_all[h:h + 1], bc_all[h:h + 1]
        b_end = bc_r[:, ln - 1:ln]
        g_r = b_end - bc_r + li_r
        m0 = m_old[s, h]
        m_new = jnp.maximum(b_end + m0, jnp.max(g_r, axis=1, keepdims=True))
        ch[s, h] = dict(
            qb=q_ref[:, sl].astype(BF16), kb=k.astype(BF16), vb=v.astype(BF16), v=v,
            inter=bc_r + m0, m_new=m_new, wg_r=jnp.exp(g_r - m_new),
            decay=jnp.exp(b_end + m0 - m_new),
            dmat_t=jnp.where(causal_t, bc_r - to_col(bc_r - li_r), -jnp.inf))
    for p in pairs:
        d = ch[p]
        d["st"] = _dot_nt(d["kb"], d["qb"])
        d["cq_t"] = _dot_nt(c_old[p].astype(BF16), d["qb"])
        d["nq"] = _dot_nt(jnp.broadcast_to(n_old[p], (SUBLANES, dh)).astype(BF16), d["qb"])[0:1]
        d["c_add"] = _dot((d["v"].T * d["wg_r"]).astype(BF16), d["kb"])
        d["n_add"] = _dot(jnp.broadcast_to(d["wg_r"], (SUBLANES, ln)).astype(BF16), d["kb"])[0:1]
    for p in pairs:
        d = ch[p]
        d["m_row"] = jnp.maximum(d["inter"], jnp.max(d["dmat_t"], axis=0, keepdims=True))
        d["w_inter"] = jnp.exp(d["inter"] - d["m_row"])
        d["sc_t"] = d["st"] * jnp.exp(d["dmat_t"] - d["m_row"])
    for p in pairs:
        d = ch[p]
        d["pv_t"] = _dot_tn(d["vb"], d["sc_t"].astype(BF16))
    new_state = {}
    for s, h in pairs:
        d = ch[s, h]
        sl = slice(h * dh, (h + 1) * dh)
        num_t = d["pv_t"] + d["w_inter"] * d["cq_t"]
        den = jnp.sum(d["sc_t"], axis=0, keepdims=True) + d["w_inter"] * d["nq"]
        hh = (num_t / jnp.maximum(jnp.abs(den), jnp.exp(-d["m_row"]))).T
        h_ref[s, :, sl] = (hh * jax.nn.sigmoid(seq_in[s][3][:, sl])).astype(BF16)
        new_state[s, h] = (d["decay"] * c_old[s, h] + d["c_add"],
                           d["decay"] * n_old[s, h] + d["n_add"],
                           jnp.broadcast_to(d["m_new"], (1, dh)))

    for (s, h), (c_new, n_new, m_new) in new_state.items():
        c_s[s, h] = c_new
        n_s[s, h] = n_new
        m_s[s, h] = m_new

    @pl.when(ci == nc - 1)
    def _():
        c_out[...] = c_s[...]
        n_out[...] = n_s[...]
        m_out[...] = m_s[...]


def _mlstm_prompt(z, gates_t, batch, seq, nh, dh):
    ln = MLSTM_CHUNK
    nc = seq // ln
    nseq = min(MLSTM_SEQ_PER_STEP, batch)
    assert batch % nseq == 0
    width = nh * dh
    ng = gates_t.shape[0]
    rows_of = lambda s: (lambda b, c: (b * nseq + s) * nc + c)
    in_specs, operands = [], []
    for s in range(nseq):
        r = rows_of(s)
        for j in (1, 2, 3, 4):
            in_specs.append(pl.BlockSpec((ln, width), lambda b, c, r=r, j=j: (r(b, c), j)))
        in_specs.append(pl.BlockSpec((ng, ln), lambda b, c, r=r: (0, r(b, c))))
        operands += [z, z, z, z, gates_t]
    h_spec = pl.BlockSpec((None, nseq, ln, width), lambda b, c: (b, 0, c, 0))
    st = lambda a, b_: pl.BlockSpec((nseq, nh, a, b_), lambda b, c: (b, 0, 0, 0))
    return pl.pallas_call(
        functools.partial(_mlstm_prompt_kernel, nh=nh, dh=dh, nseq=nseq),
        grid=(batch // nseq, nc),
        in_specs=in_specs,
        out_specs=[h_spec, st(dh, dh), st(1, dh), st(1, dh)],
        out_shape=[
            jax.ShapeDtypeStruct((batch // nseq, nseq, seq, width), BF16),
            jax.ShapeDtypeStruct((batch, nh, dh, dh), F32),
            jax.ShapeDtypeStruct((batch, nh, 1, dh), F32),
            jax.ShapeDtypeStruct((batch, nh, 1, dh), F32),
        ],
        scratch_shapes=[pltpu.VMEM((nseq, nh, dh, dh), F32), pltpu.VMEM((nseq, nh, 1, dh), F32),
                        pltpu.VMEM((nseq, nh, 1, dh), F32)],
        compiler_params=_params("parallel", "arbitrary"),
        name="mlstm_prompt",
    )(*operands)


def _mlstm_step_kernel(q_ref, k_ref, v_ref, o_ref, li_ref, fp_ref, c_ref, n_ref, m_ref,
                       h_ref, c_out, n_out, m_out, *, nh, dh):
    eye = (lax.broadcasted_iota(I32, (1, dh, dh), 1) == lax.broadcasted_iota(I32, (1, dh, dh), 2))
    scale = dh ** -0.5
    for h in range(nh):
        sl = slice(h * dh, (h + 1) * dh)
        q = q_ref[:, :, sl]
        k = k_ref[:, :, sl] * scale
        v = v_ref[:, :, sl]
        c = c_ref[:, h]
        n = n_ref[:, h]
        m = m_ref[:, h]
        li = li_ref[:, h]
        lf = jax.nn.log_sigmoid(fp_ref[:, h])
        inter = lf + m
        m_row = jnp.maximum(inter, li)
        w_intra = jnp.exp(li - m_row)
        w_inter = jnp.exp(inter - m_row)
        sc = jnp.sum(q * k, axis=-1, keepdims=True) * w_intra
        q8 = jnp.broadcast_to(q, (q.shape[0], SUBLANES, dh)).astype(BF16)
        cq = jnp.einsum("bqk,bvk->bqv", q8, c.astype(BF16),
                        preferred_element_type=F32)[:, 0:1, :]
        num = sc * v + w_inter * cq
        den = sc + w_inter * jnp.sum(n * q, axis=-1, keepdims=True)
        h_l = num / jnp.maximum(jnp.abs(den), jnp.exp(-m_row))
        h_ref[:, :, sl] = h_l * jax.nn.sigmoid(o_ref[:, :, sl])
        v_c = jnp.sum(jnp.where(eye, v, 0.0), axis=-1, keepdims=True)
        wg = jnp.exp(li - m_row)
        decay = jnp.exp(inter - m_row)
        c_out[:, h] = decay * c + (v_c * wg) * k
        n_out[:, h] = decay * n + wg * k
        m_out[:, h] = m_row


def _mlstm_step(z3, li, fp, c, n, m, nh, dh):
    batch = c.shape[0]
    bb = MLSTM_STEP_BATCH
    width = nh * dh
    blk = lambda j: pl.BlockSpec((bb, 1, width), lambda i: (i, 0, j))
    st4 = lambda a, b: pl.BlockSpec((bb, nh, a, b), lambda i: (i, 0, 0, 0))
    return pl.pallas_call(
        functools.partial(_mlstm_step_kernel, nh=nh, dh=dh),
        grid=(batch // bb,),
        in_specs=[blk(1), blk(2), blk(3), blk(4), st4(1, 1), st4(1, 1),
                  st4(dh, dh), st4(1, dh), st4(1, 1)],
        out_specs=[pl.BlockSpec((bb, 1, width), lambda i: (i, 0, 0)),
                   st4(dh, dh), st4(1, dh), st4(1, 1)],
        out_shape=[
            jax.ShapeDtypeStruct((batch, 1, width), F32),
            jax.ShapeDtypeStruct((batch, nh, dh, dh), F32),
            jax.ShapeDtypeStruct((batch, nh, 1, dh), F32),
            jax.ShapeDtypeStruct((batch, nh, 1, 1), F32),
        ],
        compiler_params=_params("parallel"),
        name="mlstm_step",
    )(z3, z3, z3, z3, li, fp, c, n, m)


def _split_bf16(a):
    hi = a.astype(BF16)
    return hi, (a - hi.astype(F32)).astype(BF16)


def _route(hn, wr_ref, br_ref, tril_ref, carry_ref, rinfo_ref, rinfo_t_ref):
    ngrp, epg = MOE_GROUPS, MOE_EXPERTS_PER_GROUP
    h_hi, h_lo = _split_bf16(hn)
    both = _dot(h_hi, wr_ref[...])
    logits = both[:, 0:LANES] + (both[:, LANES:2 * LANES] + _dot(h_lo, wr_ref[:, 0:LANES])) + br_ref[...]
    tm = logits.shape[0]
    lt = logits.T
    sub = lax.broadcasted_iota(I32, (LANES, tm), 0)
    neg = -jnp.inf

    def first_max(x):
        mx = jnp.max(x, axis=0, keepdims=True)
        idx = jnp.min(jnp.where(x == mx, sub, LANES), axis=0, keepdims=True)
        return mx, idx

    is_grp = sub < ngrp
    gmax, gsel = first_max(jnp.where(is_grp, lt, neg))
    g_w = 1.0 / jnp.sum(jnp.where(is_grp, jnp.exp(lt - gmax), 0.0), axis=0, keepdims=True)
    lo = ngrp + gsel * epg
    el = jnp.where((sub >= lo) & (sub < lo + epg), lt, neg)
    v1, i1 = first_max(el)
    v2, i2 = first_max(jnp.where(sub == i1, neg, el))
    e2 = jnp.exp(v2 - v1)
    w1 = g_w / (1.0 + e2)
    w2 = g_w * e2 / (1.0 + e2)
    eid1 = i1 - ngrp
    eid2 = i2 - ngrp

    hit1 = sub == eid1
    hit2 = sub == eid2
    onehot_t = jnp.where(hit1 | hit2, 1.0, 0.0).astype(BF16)
    carry = carry_ref[...]
    diag = (lax.broadcasted_iota(I32, (LANES, LANES), 0) == lax.broadcasted_iota(I32, (LANES, LANES), 1))
    carry_col = jnp.sum(jnp.where(diag, carry, 0.0), axis=1, keepdims=True)
    prefix_t = _dot_nt(onehot_t, tril_ref[...]) + carry_col
    rank1 = jnp.sum(jnp.where(hit1, prefix_t, 0.0), axis=0, keepdims=True)
    rank2 = jnp.sum(jnp.where(hit2, prefix_t, 0.0), axis=0, keepdims=True)
    carry_ref[...] = carry + _dot_nt(jnp.ones((SUBLANES, tm), BF16), onehot_t)[0:1]

    info_t = jnp.concatenate([eid1.astype(F32), eid2.astype(F32), w1, w2, rank1, rank2,
                              jnp.zeros((SUBLANES - 6, tm), F32)], axis=0)
    rinfo_t_ref[...] = info_t
    pick = jnp.where(lax.broadcasted_iota(I32, (SUBLANES, LANES), 0)
                     == lax.broadcasted_iota(I32, (SUBLANES, LANES), 1), 1.0, 0.0).astype(BF16)
    parts = [_dot_tn(p, pick) for p in _split3(info_t)]
    rinfo_ref[...] = parts[0] + (parts[1] + parts[2])


def _outproj_route_kernel(xm_ref, xt_ref, pm_ref, pt_ref, mm_ref, mt_ref,
                          wo_ref, g_ref, wr_ref, br_ref, tril_ref,
                          x1_ref, hn_ref, rinfo_ref, rinfo_t_ref, cnt_ref, carry_ref, wb):
    @pl.when(pl.program_id(0) == 0)
    def _():
        carry_ref[...] = jnp.zeros(carry_ref.shape, F32)
        wb[...] = wo_ref[...].astype(BF16)

    half = pm_ref.shape[1]
    mix = (_dot(_stacked(pm_ref, pt_ref), wb[0:half, :])
           + _dot(_stacked(mm_ref, mt_ref), wb[half:2 * half, :]))
    x1 = _stacked(xm_ref, xt_ref) + mix
    x1_ref[...] = x1
    hn = _rms(x1, g_ref[...])
    hn_ref[...] = hn
    _route(hn, wr_ref, br_ref, tril_ref, carry_ref, rinfo_ref, rinfo_t_ref)
    cnt_ref[...] = carry_ref[...]


def _glu_route_kernel(x_ref, ym_ref, yt_ref, wglu_ref, g_ref, wr_ref, br_ref, tril_ref,
                      x1_ref, hn_ref, rinfo_ref, rinfo_t_ref, cnt_ref, carry_ref, wb):
    @pl.when(pl.program_id(0) == 0)
    def _():
        carry_ref[...] = jnp.zeros(carry_ref.shape, F32)
        wb[...] = wglu_ref[...].astype(BF16)

    d = x_ref.shape[1]
    ag = _dot(jax.nn.gelu(_stacked(ym_ref, yt_ref)).astype(BF16), wb[...])
    x1 = x_ref[...] + ag[:, 0:d] * jax.nn.sigmoid(ag[:, d:2 * d])
    x1_ref[...] = x1
    hn = _rms(x1, g_ref[...])
    hn_ref[...] = hn
    _route(hn, wr_ref, br_ref, tril_ref, carry_ref, rinfo_ref, rinfo_t_ref)
    cnt_ref[...] = carry_ref[...]


def _mix_route(kernel, name, n, row_specs, rows, w, g, wr, br, tril):
    d = g.shape[1]
    tm = TOKEN_TILE
    full = lambda i: (0, 0)
    return pl.pallas_call(
        kernel,
        grid=(n // tm,),
        in_specs=row_specs + [
            _weight_spec(w),
            pl.BlockSpec((1, d), full),
            pl.BlockSpec((d, 2 * LANES), full),
            pl.BlockSpec((1, LANES), full),
            pl.BlockSpec((tm, tm), full),
        ],
        out_specs=[
            pl.BlockSpec((tm, d), lambda i: (i, 0)),
            pl.BlockSpec((tm, d), lambda i: (i, 0)),
            pl.BlockSpec((tm, LANES), lambda i: (i, 0)),
            pl.BlockSpec((SUBLANES, tm), lambda i: (0, i)),
            pl.BlockSpec((1, LANES), full),
        ],
        out_shape=[
            jax.ShapeDtypeStruct((n, d), F32),
            jax.ShapeDtypeStruct((n, d), F32),
            jax.ShapeDtypeStruct((n, LANES), F32),
            jax.ShapeDtypeStruct((SUBLANES, n), F32),
            jax.ShapeDtypeStruct((1, LANES), F32),
        ],
        scratch_shapes=[pltpu.VMEM((1, LANES), F32), pltpu.VMEM(w.shape, BF16)],
        compiler_params=_params("arbitrary"),
        name=name,
    )(*rows, w, g, wr, br, tril)


def _index_copy(pos_hbm, idx_s, sem_i, tile, slot):
    return pltpu.make_async_copy(pos_hbm.at[tile], idx_s.at[slot], sem_i.at[slot])


def _dispatch_kernel(pos_hbm, hn_hbm, xs_hbm, idx_s, tiles, sem_i, sem_l, sem_d):
    i = pl.program_id(0)
    nt = pl.num_programs(0)
    nslot, tm, _ = tiles.shape
    islot = i % 2

    def load(t):
        s = t % nslot
        return pltpu.make_async_copy(hn_hbm.at[pl.ds(t * tm, tm)], tiles.at[s], sem_l.at[s])

    def wait_rows(t):
        s = t % nslot
        whole = pltpu.make_async_copy(tiles.at[s], xs_hbm.at[pl.ds(0, tm)], sem_d.at[s])
        whole.wait()
        whole.wait()

    @pl.when(i == 0)
    def _():
        _index_copy(pos_hbm, idx_s, sem_i, 0, 0).start()
        load(0).start()

        @pl.when(nt > 1)
        def _():
            load(1).start()

    @pl.when(i >= 2)
    def _():
        wait_rows(i - 2)

    @pl.when(i + 2 < nt)
    def _():
        load(i + 2).start()

    _index_copy(pos_hbm, idx_s, sem_i, i, islot).wait()

    @pl.when(i + 1 < nt)
    def _():
        _index_copy(pos_hbm, idx_s, sem_i, i + 1, 1 - islot).start()

    load(i).wait()
    slot = i % nslot

    def issue(r, carry):
        row = tiles.at[slot, pl.ds(r, 1)]
        pltpu.make_async_copy(row, xs_hbm.at[pl.ds(idx_s[islot, 0, r], 1)],
                              sem_d.at[slot]).start(priority=0)
        pltpu.make_async_copy(row, xs_hbm.at[pl.ds(idx_s[islot, 0, tm + r], 1)],
                              sem_d.at[slot]).start(priority=1)
        return carry

    lax.fori_loop(0, tm, issue, 0, unroll=DMA_ISSUE_UNROLL)

    @pl.when(i == nt - 1)
    def _():
        @pl.when(nt > 1)
        def _():
            wait_rows(i - 1)

        wait_rows(i)


def _dispatch(pos_tiles, hn):
    n, d = hn.shape
    tm = TOKEN_TILE
    nslot = 4
    return pl.pallas_call(
        _dispatch_kernel,
        grid=(n // tm,),
        in_specs=[pl.BlockSpec(memory_space=pl.ANY), pl.BlockSpec(memory_space=pl.ANY)],
        out_specs=pl.BlockSpec(memory_space=pl.ANY),
        out_shape=jax.ShapeDtypeStruct((2 * n, d), F32),
        scratch_shapes=[pltpu.SMEM((2, 1, 2 * tm), I32), pltpu.VMEM((nslot, tm, d), F32),
                        pltpu.SemaphoreType.DMA((2,)), pltpu.SemaphoreType.DMA((nslot,)),
                        pltpu.SemaphoreType.DMA((nslot,))],
        compiler_params=_params("arbitrary"),
        name="moe_dispatch",
    )(pos_tiles, hn)


def _moe_kernel(vt_ref, ve_ref, von_ref, vnext_ref, vslot_ref, lo_ref, hi_ref,
                xs_ref, wg_hbm, wu_hbm, wd_hbm, eo_ref,
                wg_f, wu_f, wd_f, wgb, wub, wdb, sem_w, *, layer):
    v = pl.program_id(0)
    tr = xs_ref.shape[0]
    prev = jnp.maximum(v - 1, 0)
    e = ve_ref[v]
    new_expert = jnp.logical_or(v == 0, e != ve_ref[prev])
    first_visit = jnp.logical_or(v == 0, vt_ref[v] != vt_ref[prev])

    def fetch(expert, slot):
        return [pltpu.make_async_copy(src.at[layer, expert], dst.at[slot], sem_w.at[slot, j])
                for j, (src, dst) in enumerate(((wg_hbm, wg_f), (wu_hbm, wu_f), (wd_hbm, wd_f)))]

    @pl.when(von_ref[v] == 1)
    def _():
        @pl.when(new_expert)
        def _():
            slot = vslot_ref[v]

            @pl.when(v == 0)
            def _():
                for cp in fetch(e, slot):
                    cp.start()

            for cp in fetch(e, slot):
                cp.wait()
            wgb[...] = wg_f[slot].astype(BF16)
            wub[...] = wu_f[slot].astype(BF16)
            wdb[...] = wd_f[slot].astype(BF16)

            @pl.when(vnext_ref[v] >= 0)
            def _():
                for cp in fetch(vnext_ref[v], 1 - slot):
                    cp.start()

        x = xs_ref[...].astype(BF16)
        act = jax.nn.silu(_dot(x, wgb[...])) * _dot(x, wub[...])
        row = vt_ref[v] * tr + lax.broadcasted_iota(I32, (tr, 1), 0)
        act = jnp.where((row >= lo_ref[e]) & (row < hi_ref[e]), act, 0.0)
        res = _dot(act.astype(BF16), wdb[...])

        @pl.when(first_visit)
        def _():
            eo_ref[...] = res

        @pl.when(jnp.logical_not(first_visit))
        def _():
            eo_ref[...] += res


def _moe_experts(vt, ve, von, vnext, vslot, lo, hi, xs, wg, wu, wd, layer):
    nv = vt.shape[0]
    rows, d = xs.shape
    hid = wg.shape[3]
    tr = EXPERT_ROW_TILE
    tile = pl.BlockSpec((tr, d), lambda v, vt, *_: (vt[v], 0))
    grid_spec = pltpu.PrefetchScalarGridSpec(
        num_scalar_prefetch=7,
        grid=(nv,),
        in_specs=[tile] + [pl.BlockSpec(memory_space=pl.ANY)] * 3,
        out_specs=tile,
        scratch_shapes=[
            pltpu.VMEM((2, d, hid), F32), pltpu.VMEM((2, d, hid), F32), pltpu.VMEM((2, hid, d), F32),
            pltpu.VMEM((d, hid), BF16), pltpu.VMEM((d, hid), BF16), pltpu.VMEM((hid, d), BF16),
            pltpu.SemaphoreType.DMA((2, 3)),
        ],
    )
    return pl.pallas_call(
        functools.partial(_moe_kernel, layer=layer),
        grid_spec=grid_spec,
        out_shape=jax.ShapeDtypeStruct((rows, d), F32),
        compiler_params=_params("arbitrary"),
        name="moe_experts",
    )(vt, ve, von, vnext, vslot, lo, hi, xs, wg, wu, wd)


def _combine_kernel(pos_hbm, eo_hbm, x_ref, rinfo_ref, g_ref, o1_ref, o2_ref, idx_s, a_buf, b_buf,
                    sem_i, sem_a, sem_b, *, last_layer):
    i = pl.program_id(0)
    nt = pl.num_programs(0)
    tm = a_buf.shape[1]
    slot = i % 2

    def gathers(s):
        def issue(r, carry):
            pltpu.make_async_copy(eo_hbm.at[pl.ds(idx_s[s, 0, r], 1)],
                                  a_buf.at[s, pl.ds(r, 1)], sem_a.at[s]).start(priority=0)
            pltpu.make_async_copy(eo_hbm.at[pl.ds(idx_s[s, 0, tm + r], 1)],
                                  b_buf.at[s, pl.ds(r, 1)], sem_b.at[s]).start(priority=1)
            return carry

        lax.fori_loop(0, tm, issue, 0, unroll=DMA_ISSUE_UNROLL)

    @pl.when(i == 0)
    def _():
        first = _index_copy(pos_hbm, idx_s, sem_i, 0, 0)
        first.start()
        first.wait()
        gathers(0)

        @pl.when(nt > 1)
        def _():
            _index_copy(pos_hbm, idx_s, sem_i, 1, 1).start()

    @pl.when(i + 1 < nt)
    def _():
        _index_copy(pos_hbm, idx_s, sem_i, i + 1, 1 - slot).wait()
        gathers(1 - slot)

    @pl.when(i + 2 < nt)
    def _():
        _index_copy(pos_hbm, idx_s, sem_i, i + 2, slot).start()

    pltpu.make_async_copy(eo_hbm.at[pl.ds(0, tm)], a_buf.at[slot], sem_a.at[slot]).wait()
    pltpu.make_async_copy(eo_hbm.at[pl.ds(0, tm)], b_buf.at[slot], sem_b.at[slot]).wait()
    info = rinfo_ref[...]
    x2 = x_ref[...] + (info[:, 2:3] * a_buf[slot] + info[:, 3:4] * b_buf[slot])
    hn = _rms(x2, g_ref[...])
    if last_layer:
        @pl.when(i < nt - 1)
        def _():
            o1_ref[...] = hn

        @pl.when(i == nt - 1)
        def _():
            o2_ref[...] = hn
    else:
        o1_ref[...] = x2
        o2_ref[...] = hn


def _combine(pos_tiles, eo, x, rinfo, g, last_layer):
    n, d = x.shape
    tm = TOKEN_TILE
    nt = n // tm
    row = pl.BlockSpec((tm, d), lambda i: (i, 0))
    if last_layer:
        out_specs = [pl.BlockSpec((tm, d), lambda i: (jnp.minimum(i, nt - 2), 0)),
                     pl.BlockSpec((tm, d), lambda i: (0, 0))]
        out_shape = [jax.ShapeDtypeStruct((n - tm, d), F32), jax.ShapeDtypeStruct((tm, d), F32)]
    else:
        out_specs = [row, row]
        out_shape = [jax.ShapeDtypeStruct((n, d), F32), jax.ShapeDtypeStruct((n, d), F32)]
    return pl.pallas_call(
        functools.partial(_combine_kernel, last_layer=last_layer),
        grid=(nt,),
        in_specs=[
            pl.BlockSpec(memory_space=pl.ANY),
            pl.BlockSpec(memory_space=pl.ANY),
            row,
            pl.BlockSpec((tm, LANES), lambda i: (i, 0)),
            pl.BlockSpec((1, d), lambda i: (0, 0)),
        ],
        out_specs=out_specs,
        out_shape=out_shape,
        scratch_shapes=[
            pltpu.SMEM((2, 1, 2 * tm), I32),
            pltpu.VMEM((2, tm, d), F32),
            pltpu.VMEM((2, tm, d), F32),
            pltpu.SemaphoreType.DMA((2,)),
            pltpu.SemaphoreType.DMA((2,)),
            pltpu.SemaphoreType.DMA((2,)),
        ],
        compiler_params=_params("arbitrary"),
        name="moe_combine",
    )(pos_tiles, eo, x, rinfo, g)


def _moe(x1, hn, rinfo, rinfo_t, counts, wg, wu, wd, layer, next_gain, last_layer):
    n = x1.shape[0]
    ne = wg.shape[1]
    tr = EXPERT_ROW_TILE
    tm = TOKEN_TILE
    nv = (2 * n) // tr + ne - 1
    eid = rinfo_t[0:2].astype(I32)
    rank = rinfo_t[4:6].astype(I32)
    cnt = counts[0, :ne].astype(I32)
    seg_end = jnp.cumsum(cnt)
    seg_start = seg_end - cnt
    experts = jnp.arange(ne, dtype=I32)
    pos = rank + jnp.sum(jnp.where(eid[:, None, :] == experts[None, :, None],
                                   seg_start[None, :, None], 0), axis=1)
    pos_tiles = pos.reshape(2, n // tm, tm).transpose(1, 0, 2).reshape(n // tm, 1, 2 * tm)

    first_tile = seg_start // tr
    tiles_e = jnp.where(cnt > 0, (seg_end - 1) // tr - first_tile + 1, 0)
    v_end = jnp.cumsum(tiles_e)
    v_start = v_end - tiles_e
    total = v_end[-1]
    vis = jnp.arange(nv, dtype=I32)
    vc = jnp.minimum(vis, jnp.maximum(total - 1, 0))
    ve = jnp.sum((vc[:, None] >= v_end[None, :]).astype(I32), axis=1)
    von = (vis < total).astype(I32)
    present = cnt > 0
    later = present[None, :] & (experts[None, :] > experts[:, None])
    next_of = jnp.min(jnp.where(later, experts[None, :], ne), axis=1)
    tabs = jnp.stack([first_tile, v_start, jnp.where(next_of < ne, next_of, -1),
                      jnp.cumsum(present.astype(I32)) - 1])
    picked = jnp.sum(jnp.where(ve[None, :, None] == experts, tabs[:, None, :], 0), axis=2)
    vt = picked[0] + (vc - picked[1])
    vnext = picked[2]
    vslot = picked[3] & 1

    xs = _dispatch(pos_tiles, hn)
    eo = _moe_experts(vt, ve, von, vnext, vslot, seg_start, seg_end, xs, wg, wu, wd, layer)
    return _combine(pos_tiles, eo, x1, rinfo, next_gain, last_layer)


def _cis(log_mag, ang):
    mag = jnp.exp(log_mag)
    return mag * jnp.cos(ang), mag * jnp.sin(ang)


def _split3(a):
    p1 = a.astype(BF16)
    r1 = a - p1.astype(F32)
    p2 = r1.astype(BF16)
    return p1, p2, (r1 - p2.astype(F32)).astype(BF16)


def _select_dot(a, sel, sel_first=False):
    sel = sel.astype(BF16)
    parts = [(_dot(sel, p) if sel_first else _dot(p, sel)) for p in _split3(a)]
    return parts[0] + (parts[1] + parts[2])


def _dot3(a, b):
    a_hi, a_lo = _split_bf16(a)
    b_hi, b_lo = _split_bf16(b)
    return _dot(a_hi, b_hi) + (_dot(a_lo, b_hi) + _dot(a_hi, b_lo))


def _s5_prep_kernel(lsr_ref, lrr_ref, lir_ref, ctre_ref, ctim_ref,
                    btre_ref, btim_ref, d_ref,
                    t_ref, wre_ref, wim_ref, cyre_ref, cyim_ref, apr_ref, api_ref,
                    bbr_ref, bbi_ref, lbr_ref, lbi_ref, *, gw, gpb, sub):
    blk = gpb * gw
    wide = sub * blk
    nsl = lrr_ref.shape[2]
    ns = nsl // gpb
    sh_gw, sh_ns = gw.bit_length() - 1, ns.bit_length() - 1
    dt_r = jnp.exp(lsr_ref[0])
    ldt_r_re, ldt_r_im = lrr_ref[0] * dt_r, lir_ref[0] * dt_r
    diag = (lax.broadcasted_iota(I32, (nsl, nsl), 0) == lax.broadcasted_iota(I32, (nsl, nsl), 1))
    to_col = lambda r: jnp.sum(jnp.where(diag, r, 0.0), axis=1, keepdims=True)
    ldt_c_re, ldt_c_im = to_col(ldt_r_re), to_col(ldt_r_im)

    lane_w = lax.broadcasted_iota(I32, (1, wide), 1)
    spread = jnp.where((lax.broadcasted_iota(I32, (gw, wide), 1) & (gw - 1))
                       == lax.broadcasted_iota(I32, (gw, wide), 0), 1.0, 0.0)
    cre = _select_dot(ctre_ref[0], spread)
    cim = _select_dot(ctim_ref[0], spread)
    same = (jnp.right_shift(lax.broadcasted_iota(I32, (nsl, 1), 0), sh_ns)
            == (jnp.right_shift(lane_w, sh_gw) & (gpb - 1)))
    assert 2 * blk == LANES
    tau = lax.broadcasted_iota(I32, (1, LANES), 1).astype(F32)
    pw_re, pw_im = _cis(tau * ldt_c_re, tau * ldt_c_im)
    low = lax.broadcasted_iota(I32, (nsl, LANES), 1) < blk

    def spread_pow(p, first):
        col = lambda t: jnp.broadcast_to(p[:, t:t + 1], (nsl, LANES))
        return jnp.concatenate([jnp.where(low, col(first + 2 * m), col(first + 2 * m + 1))
                                for m in range(sub // 2)], axis=1)

    def c_lam_pow(first):
        pr, pi = spread_pow(pw_re, first), spread_pow(pw_im, first)
        return (jnp.where(same, pr * cre - pi * cim, 0.0),
                jnp.where(same, -(pr * cim + pi * cre), 0.0))

    clr0, cli0 = c_lam_pow(0)
    clr1, cli1 = c_lam_pow(1)
    cyre_ref[0] = clr1.astype(BF16)
    cyim_ref[0] = cli1.astype(BF16)

    lbr, lbi = _cis(ldt_r_re, ldt_r_im)
    lbr_ref[0] = lbr
    lbi_ref[0] = lbi
    lr, li = lrr_ref[0], lir_ref[0]
    nr, ni = lbr - 1.0, lbi
    den = lr * lr + li * li
    fr = (nr * lr + ni * li) / den
    fi = (ni * lr - nr * li) / den
    bre, bim = btre_ref[0], btim_ref[0]
    bbr = fr * bre - fi * bim
    bbi = fr * bim + fi * bre
    bbr_ref[0] = bbr
    bbi_ref[0] = bbi

    lane_t = lax.broadcasted_iota(I32, (gw, wide), 1)
    chan = lax.broadcasted_iota(I32, (gw, wide), 0)
    for g in range(gpb):
        ps = slice(g * ns, (g + 1) * ns)
        r = _dot3(bbr[:, ps], clr0[ps, :]) + _dot3(bbi[:, ps], cli0[ps, :])
        r = r + jnp.where(lane_t == g * gw + chan, d_ref[0][:, g:g + 1], 0.0)
        for j in range(sub):
            tb = r if j == 0 else jnp.where(lane_t >= blk * j, pltpu.roll(r, blk * j, 1), 0.0)
            r0 = j * blk + g * gw
            t_ref[0, r0:r0 + gw, :] = tb.astype(BF16)

    rows = lax.broadcasted_iota(I32, (wide, 1), 0)
    spread_t = jnp.where((lax.broadcasted_iota(I32, (wide, gw), 0) & (gw - 1))
                         == lax.broadcasted_iota(I32, (wide, gw), 1), 1.0, 0.0)
    bbr_t = _select_dot(bbr, spread_t, sel_first=True)
    bbi_t = _select_dot(bbi, spread_t, sel_first=True)
    same_w = ((jnp.right_shift(rows, sh_gw) & (gpb - 1))
              == jnp.right_shift(lax.broadcasted_iota(I32, (1, nsl), 1), sh_ns))
    rj = ((sub - 1) - lax.broadcasted_iota(I32, (sub, 1), 0)).astype(F32)
    q_re, q_im = _cis(rj * ldt_r_re, rj * ldt_r_im)
    per_step = lambda q: jnp.concatenate(
        [jnp.broadcast_to(q[j:j + 1, :], (blk, nsl)) for j in range(sub)], axis=0)
    pr, pi = per_step(q_re), per_step(q_im)
    wre_ref[0] = jnp.where(same_w, pr * bbr_t - pi * bbi_t, 0.0).astype(BF16)
    wim_ref[0] = jnp.where(same_w, pr * bbi_t + pi * bbr_t, 0.0).astype(BF16)

    nlev = apr_ref.shape[1]
    pw = (sub * jnp.left_shift(1, lax.broadcasted_iota(I32, (nlev, 1), 0))).astype(F32)
    apr, api = _cis(pw * ldt_r_re, pw * ldt_r_im)
    apr_ref[0] = apr
    api_ref[0] = api


def _s5_prep(log_step, lam_re, lam_im, b_re, b_im, c_re, c_im, d_skip):
    ng, ns = lam_re.shape
    gw = b_re.shape[2]
    sub, gpb = S5_SUB, S5_GROUPS_PER_BLOCK
    nblk = ng // gpb
    nsl = gpb * ns
    wide = sub * gpb * gw
    ls = jnp.repeat(log_step, ns)
    ct = lambda c: jnp.transpose(c, (0, 2, 1)).reshape(nblk, nsl, gw)
    bt = lambda b: jnp.transpose(b.reshape(nblk, gpb, ns, gw), (0, 3, 1, 2)).reshape(nblk, gw, nsl)
    ins = [ls.reshape(nblk, 1, nsl), lam_re.reshape(nblk, 1, nsl), lam_im.reshape(nblk, 1, nsl),
           ct(c_re), ct(c_im), bt(b_re), bt(b_im),
           jnp.transpose(d_skip.reshape(nblk, gpb, gw), (0, 2, 1))]
    spec = lambda a: pl.BlockSpec((1,) + a.shape[1:], lambda g: (g, 0, 0))
    outs = [
        jax.ShapeDtypeStruct((nblk, wide, wide), BF16),
        jax.ShapeDtypeStruct((nblk, wide, nsl), BF16),
        jax.ShapeDtypeStruct((nblk, wide, nsl), BF16),
        jax.ShapeDtypeStruct((nblk, nsl, wide), BF16),
        jax.ShapeDtypeStruct((nblk, nsl, wide), BF16),
        jax.ShapeDtypeStruct((nblk, SUBLANES, nsl), F32),
        jax.ShapeDtypeStruct((nblk, SUBLANES, nsl), F32),
        jax.ShapeDtypeStruct((nblk, gw, nsl), F32),
        jax.ShapeDtypeStruct((nblk, gw, nsl), F32),
        jax.ShapeDtypeStruct((nblk, 1, nsl), F32),
        jax.ShapeDtypeStruct((nblk, 1, nsl), F32),
    ]
    return pl.pallas_call(
        functools.partial(_s5_prep_kernel, gw=gw, gpb=gpb, sub=sub),
        grid=(nblk,),
        in_specs=[spec(a) for a in ins],
        out_specs=[spec(o) for o in outs],
        out_shape=outs,
        compiler_params=_params("parallel"),
        name="s5_prep",
    )(*ins)


def _s5_prompt_kernel(x_ref, t_ref, wre_ref, wim_ref, cyre_ref, cyim_ref, apr_ref, api_ref,
                      y_ref, hre_out, him_out, u_s, y4_s, *, nb, seq, sub):
    nk = seq // sub
    rows = nb * nk
    hl = LANES // 2
    nlev = nk.bit_length() - 1
    nsl = wre_ref.shape[2]
    low = lax.broadcasted_iota(I32, (nk, LANES), 1) < hl
    kidx = lax.broadcasted_iota(I32, (rows, 1), 0) & (nk - 1)

    def step_rows(b, j):
        return (pl.ds(b * seq + j, nk, stride=sub), slice(None))

    for b in range(nb):
        rs = slice(b * nk, (b + 1) * nk)
        for m in range(sub // 2):
            ls = slice(m * LANES, (m + 1) * LANES)
            s0 = x_ref[step_rows(b, 2 * m)]
            s1 = x_ref[step_rows(b, 2 * m + 1)]
            u_s[0, rs, ls] = jnp.where(low, s0, pltpu.roll(s1, hl, 1)).astype(BF16)
            u_s[1, rs, ls] = jnp.where(low, pltpu.roll(s0, hl, 1), s1).astype(BF16)

    def shifted(a, s):
        return jnp.where(kidx >= s, pltpu.roll(a, s, 0), 0.0)

    halves = range(2)
    us = [u_s[h] for h in halves]
    hre = [_dot(us[h], wre_ref[h]) for h in halves]
    him = [_dot(us[h], wim_ref[h]) for h in halves]
    cw = 2 * LANES
    y_conv = [jnp.concatenate(
        [_dot(us[h][:, 0:c0 + cw], t_ref[h, 0:c0 + cw, c0:c0 + cw]) for c0 in range(0, us[h].shape[1], cw)],
        axis=1) for h in halves]
    for i in range(nlev):
        for h in halves:
            ar = apr_ref[h, i:i + 1, :]
            ai = api_ref[h, i:i + 1, :]
            pre, pim = shifted(hre[h], 1 << i), shifted(him[h], 1 << i)
            hre[h], him[h] = hre[h] + (ar * pre - ai * pim), him[h] + (ar * pim + ai * pre)
    for h in halves:
        ls = slice(h * nsl, (h + 1) * nsl)
        for b in range(nb):
            last = (b + 1) * nk - 1
            hre_out[b:b + 1, ls] = hre[h][last:last + 1, :]
            him_out[b:b + 1, ls] = him[h][last:last + 1, :]
        hpre = shifted(hre[h], 1).astype(BF16)
        hpim = shifted(him[h], 1).astype(BF16)
        y4_s[h] = y_conv[h] + _dot(hpre, cyre_ref[h]) + _dot(hpim, cyim_ref[h])

    for b in range(nb):
        rs = slice(b * nk, (b + 1) * nk)
        for m in range(sub // 2):
            ls = slice(m * LANES, (m + 1) * LANES)
            ca = y4_s[0, rs, ls]
            cb = y4_s[1, rs, ls]
            y_ref[step_rows(b, 2 * m)] = jnp.where(low, ca, pltpu.roll(cb, hl, 1))
            y_ref[step_rows(b, 2 * m + 1)] = jnp.where(low, pltpu.roll(ca, hl, 1), cb)


def _s5_prompt(h, prep, batch, seq):
    d = h.shape[1]
    tmat, wre, wim, cyre, cyim, apr, api = prep[:7]
    nblk, wide, nsl = wre.shape
    sub, nb = S5_SUB, S5_BATCH_PER_STEP
    nk = seq // sub
    assert nk & (nk - 1) == 0 and nk.bit_length() - 1 <= apr.shape[1]
    ntile = d // LANES
    assert nblk == 2 * ntile
    rows = nb * nk
    wspec = lambda a: pl.BlockSpec((2,) + a.shape[1:], lambda t, b: (t, 0, 0))
    xspec = pl.BlockSpec((nb * seq, LANES), lambda t, b: (b, t))
    sspec = pl.BlockSpec((None, nb, 2 * nsl), lambda t, b: (b, 0, t))
    return pl.pallas_call(
        functools.partial(_s5_prompt_kernel, nb=nb, seq=seq, sub=sub),
        grid=(ntile, batch // nb),
        in_specs=[xspec, wspec(tmat), wspec(wre), wspec(wim), wspec(cyre), wspec(cyim),
                  wspec(apr), wspec(api)],
        out_specs=[xspec, sspec, sspec],
        out_shape=[
            jax.ShapeDtypeStruct((batch * seq, d), F32),
            jax.ShapeDtypeStruct((batch // nb, nb, ntile * 2 * nsl), F32),
            jax.ShapeDtypeStruct((batch // nb, nb, ntile * 2 * nsl), F32),
        ],
        scratch_shapes=[
            pltpu.VMEM((2, rows, wide), BF16),
            pltpu.VMEM((2, rows, wide), F32),
        ],
        compiler_params=_params("parallel", "parallel"),
        name="s5_prompt",
    )(h, tmat, wre, wim, cyre, cyim, apr, api)


def _s5_step_kernel(u_ref, hre_ref, him_ref, bbr_ref, bbi_ref, lbr_ref, lbi_ref, cre_ref, cim_ref,
                    d_ref, nre_ref, nim_ref, y_ref):
    for g in range(u_ref.shape[0]):
        u = u_ref[g]
        hre, him, lbr, lbi = hre_ref[g], him_ref[g], lbr_ref[g], lbi_ref[g]
        nre = lbr * hre - lbi * him + _dot3(u, bbr_ref[g])
        nim = lbr * him + lbi * hre + _dot3(u, bbi_ref[g])
        nre_ref[g] = nre
        nim_ref[g] = nim
        y_ref[g] = _dot3(nre, cre_ref[g]) - _dot3(nim, cim_ref[g]) + d_ref[g] * u


def _s5_step(u_s, h_re, h_im, bbr, bbi, lbr, lbi, cre_t, cim_t, d_g):
    ng = u_s.shape[0]
    gb = SUBLANES
    spec = lambda a: pl.BlockSpec((gb,) + a.shape[1:], lambda g: (g, 0, 0))
    ins = [u_s, h_re, h_im, bbr, bbi, lbr, lbi, cre_t, cim_t, d_g]
    outs = [jax.ShapeDtypeStruct(h_re.shape, F32), jax.ShapeDtypeStruct(h_re.shape, F32),
            jax.ShapeDtypeStruct(u_s.shape, F32)]
    return pl.pallas_call(
        _s5_step_kernel,
        grid=(ng // gb,),
        in_specs=[spec(a) for a in ins],
        out_specs=[spec(o) for o in outs],
        out_shape=outs,
        compiler_params=_params("parallel"),
        name="s5_step",
    )(*ins)


def _pad_to(a, axis, size):
    pad = [(0, 0)] * a.ndim
    pad[axis] = (0, size - a.shape[axis])
    return jnp.pad(a, pad)


def kernel(x_prompt, x_sample, state_pool, state_mlstm_c, state_mlstm_n, state_mlstm_m, state_s5_re, state_s5_im, norm_mix, norm_ffn, norm_final, w_in_ab, b_gates, pool_w, pool_scale, w_out_ab, s5_lam_re, s5_lam_im, s5_log_step, s5_b_re, s5_b_im, s5_c_re, s5_c_im, s5_d, w_glu, moe_w_group, moe_b_group, moe_w_expert, moe_b_expert, moe_w_gate, moe_w_up, moe_w_down):
    bp, tp, d = x_prompt.shape
    bs = x_sample.shape[0]
    n_p = bp * tp
    tm = TOKEN_TILE
    assert n_p % tm == 0 and bs <= tm
    nh, dh = state_mlstm_c.shape[2], state_mlstm_c.shape[3]
    pool_width = state_pool.shape[3]
    ml_width = nh * dh
    n_main = pool_width + 4 * ml_width
    n_gates = 2 * nh
    ngrp_s5, n_state = s5_lam_re.shape[1], s5_lam_re.shape[2]
    gw_s5 = d // ngrp_s5

    n = n_p + tm

    def tail_tile(sample_rows, dtype):
        return _pad_to(sample_rows.astype(dtype), 0, tm)

    x_main = x_prompt.reshape(n_p, d)
    x_tail = tail_tile(x_sample.reshape(bs, d), F32)
    tril = jnp.tril(jnp.ones((tm, tm), BF16), -1)

    def router_weights(l):
        wr = _pad_to(jnp.concatenate([moe_w_group[l], moe_w_expert[l]], axis=1), 1, LANES)
        br = jnp.concatenate([moe_b_group[l], moe_b_expert[l]])[None, :]
        hi = wr.astype(BF16)
        lo = (wr - hi.astype(F32)).astype(BF16)
        return jnp.concatenate([hi, lo], axis=1), _pad_to(br, 1, LANES)

    w_in = w_in_ab[0]
    z, gates, gates_t = _inproj(
        x_main, x_tail, norm_mix[0][None, :], w_in,
        _pad_to(b_gates[0][None, :], 1, LANES), n_gates)

    pw = pool_w[0].astype(BF16)
    ps = pool_scale[0][None, :]
    pool_y_p, pool_p = _pool_prompt(z, pw, ps, bp, tp)
    pool_y_s, pool_s_t = _pool_step(z, jnp.transpose(state_pool[0], (1, 0, 2)), pw, ps, n_p)
    pool_s = jnp.transpose(pool_s_t, (1, 0, 2))

    ml_y_p, c_p, n_p_st, m_p = _mlstm_prompt(z, gates_t, bp, tp, nh, dh)
    ml_y_p = ml_y_p.reshape(n_p, ml_width)
    g_s = gates[n_p:n_p + bs, :n_gates]
    ml_y_s, c_s, n_s_st, m_s = _mlstm_step(
        z[n_p:n_p + bs].reshape(bs, 1, n_main),
        g_s[:, :nh].reshape(bs, nh, 1, 1), g_s[:, nh:].reshape(bs, nh, 1, 1),
        state_mlstm_c[0], state_mlstm_n[0].reshape(bs, nh, 1, dh),
        state_mlstm_m[0].reshape(bs, nh, 1, 1), nh, dh)

    rows = [x_main, x_tail, pool_y_p, tail_tile(pool_y_s, BF16),
            ml_y_p, tail_tile(ml_y_s.reshape(bs, ml_width), BF16)]
    specs = (_stacked_specs(x_main, x_tail) + _stacked_specs(pool_y_p, rows[3])
             + _stacked_specs(ml_y_p, rows[5]))
    wr, br = router_weights(0)
    x1, hn, rinfo, rinfo_t, counts = _mix_route(
        _outproj_route_kernel, "outproj_route", n, specs, rows, w_out_ab[0],
        norm_ffn[0][None, :], wr, br, tril)
    x2, h1 = _moe(x1, hn, rinfo, rinfo_t, counts, moe_w_gate, moe_w_up, moe_w_down, 0,
                  norm_mix[1][None, :], last_layer=False)

    prep = _s5_prep(s5_log_step[0], s5_lam_re[0], s5_lam_im[0], s5_b_re[0], s5_b_im[0],
                    s5_c_re[0], s5_c_im[0], s5_d[0])
    y_p, hre_p, him_p = _s5_prompt(h1, prep, bp, tp)
    s5_re_p = hre_p.reshape(bp, ngrp_s5, n_state)
    s5_im_p = him_p.reshape(bp, ngrp_s5, n_state)

    gpb = S5_GROUPS_PER_BLOCK
    per_group = lambda a: jnp.transpose(
        a.reshape(ngrp_s5 // gpb, gw_s5, gpb, n_state), (0, 2, 1, 3)).reshape(ngrp_s5, gw_s5, n_state)
    bbr, bbi = per_group(prep[7]), per_group(prep[8])
    lbr, lbi = prep[9].reshape(ngrp_s5, 1, n_state), prep[10].reshape(ngrp_s5, 1, n_state)
    d_g = s5_d[0].reshape(ngrp_s5, 1, gw_s5)
    u_s = h1[n_p:n_p + bs].reshape(bs, ngrp_s5, gw_s5).transpose(1, 0, 2)
    h_re = jnp.transpose(state_s5_re[0], (1, 0, 2))
    h_im = jnp.transpose(state_s5_im[0], (1, 0, 2))
    nre_s, nim_s, y_s = _s5_step(
        u_s, h_re, h_im, bbr, bbi, lbr, lbi,
        jnp.transpose(s5_c_re[0], (0, 2, 1)), jnp.transpose(s5_c_im[0], (0, 2, 1)), d_g)
    s5_re_s = jnp.transpose(nre_s, (1, 0, 2))
    s5_im_s = jnp.transpose(nim_s, (1, 0, 2))
    y_tail = tail_tile(jnp.transpose(y_s, (1, 0, 2)).reshape(bs, d), F32)

    wr, br = router_weights(1)
    specs = [pl.BlockSpec((tm, d), lambda i: (i, 0))] + _stacked_specs(y_p, y_tail)
    x3, hn, rinfo, rinfo_t, counts = _mix_route(
        _glu_route_kernel, "glu_route", n, specs, [x2, y_p, y_tail], w_glu[0],
        norm_ffn[1][None, :], wr, br, tril)
    y_main, y_last = _moe(x3, hn, rinfo, rinfo_t, counts, moe_w_gate, moe_w_up, moe_w_down, 1,
                          norm_final[None, :], last_layer=True)

    return (y_main.reshape(bp, tp, d), y_last[:bs].reshape(bs, 1, d),
            pool_p[None], c_p[None], n_p_st.reshape(1, bp, nh, dh), m_p[:, :, 0, 0][None],
            s5_re_p[None], s5_im_p[None],
            pool_s[None], c_s[None], n_s_st.reshape(1, bs, nh, dh), m_s.reshape(1, bs, nh),
            s5_re_s[None], s5_im_s[None])
```
